```python
import math
import jax, jax.numpy as jnp
from jax import lax
import numpy as np

D_MODEL = 2048
BATCH = 8
SEQ = 2048
DEPTH = 2

N_MIXERS = 2
PLE_DIM = 256
D_FF = ((8 * D_MODEL // 3 + 255) // 256) * 256
EPS = 1e-6

GM_WIDTH = D_MODEL
GM_CHUNK = 128
GM_HEAD_DIM = 128
GM_HEADS = GM_WIDTH // GM_HEAD_DIM

S5_WIDTH = D_MODEL
S5_GROUP = 16
S5_GROUPS = S5_WIDTH // S5_GROUP
S5_STATE = 64
DT_MIN = 1e-3
DT_MAX = 1e-1

N_A = (DEPTH + 1) // 2
N_B = DEPTH // 2

kernel_name = "hybrid_gmlp_s5_interleaved_trunk"


def _rms_norm(x, g):
    xf = x.astype(jnp.float32)
    y = xf * lax.rsqrt(jnp.mean(xf * xf, axis=-1, keepdims=True) + EPS)
    return (y * g.astype(jnp.float32)).astype(x.dtype)


def _layer_norm(x, g, b):
    xf = x.astype(jnp.float32)
    mu = jnp.mean(xf, axis=-1, keepdims=True)
    xc = xf - mu
    y = xc * lax.rsqrt(jnp.mean(xc * xc, axis=-1, keepdims=True) + EPS)
    return (y * g.astype(jnp.float32) + b.astype(jnp.float32)).astype(x.dtype)


def _gmlp_mixer(h, w_in, ln_g, ln_b, w_s, b_s, w_out):
    bsz, seq, _ = h.shape
    z = jax.nn.gelu(h @ w_in)
    u, v = jnp.split(z, 2, axis=-1)
    v = _layer_norm(v, ln_g, ln_b)
    vc = v.reshape(bsz, seq // GM_CHUNK, GM_CHUNK, GM_HEADS, GM_HEAD_DIM)
    causal = jnp.tril(jnp.ones((GM_CHUNK, GM_CHUNK), dtype=bool))
    w_causal = jnp.where(causal, w_s, jnp.zeros_like(w_s)).astype(v.dtype)
    sv = jnp.einsum('hts,bnshd->bnthd', w_causal, vc) + b_s.T[:, :, None].astype(v.dtype)
    return (u * sv.reshape(bsz, seq, GM_WIDTH)) @ w_out


def _lin_combine(left, right):
    a_l, b_l = left
    a_r, b_r = right
    return a_r * a_l, a_r * b_l + b_r


def _s5_mixer(h, w_in, a_re, a_im, log_dt, b_re, b_im, c_re, c_im, d, w_out):
    f32 = jnp.float32
    bsz, seq, _ = h.shape
    u = (h @ w_in).astype(f32)
    lam = lax.complex(a_re.astype(f32), a_im.astype(f32))
    dt = jnp.exp(log_dt.astype(f32))[:, None]
    lam_bar = jnp.exp(lam * dt)
    b_bar = ((lam_bar - 1.0) / lam)[..., None] * lax.complex(b_re.astype(f32), b_im.astype(f32))
    ug = u.reshape(bsz, seq, S5_GROUPS, S5_GROUP)
    bu = lax.complex(jnp.einsum('blgc,gpc->blgp', ug, jnp.real(b_bar)),
                     jnp.einsum('blgc,gpc->blgp', ug, jnp.imag(b_bar)))
    a = jnp.broadcast_to(lam_bar, (1, seq) + lam_bar.shape)
    _, s = lax.associative_scan(_lin_combine, (a, bu), axis=1)
    y = (jnp.einsum('blgp,gcp->blgc', jnp.real(s), c_re.astype(f32))
         - jnp.einsum('blgp,gcp->blgc', jnp.imag(s), c_im.astype(f32)))
    y = y.reshape(bsz, seq, S5_WIDTH) + d.astype(f32) * u
    g = jax.nn.gelu(y).astype(h.dtype)
    val, gate = jnp.split(g @ w_out, 2, axis=-1)
    return val * jax.nn.sigmoid(gate)


def _swiglu(h, w1, w3, w2):
    return (jax.nn.silu(h @ w1) * (h @ w3)) @ w2


def _normal(k, shape, scale):
    return jax.random.normal(k, shape, jnp.float32) * scale


def _fwd_setup_inputs(seed: int = 0) -> dict:
    key = jax.random.key(seed)
    ks = jax.random.split(key, 32)
    D = D_MODEL
    x = _normal(ks[0], (BATCH, SEQ, D), 1.0)
    p = _normal(ks[1], (DEPTH, BATCH, SEQ, PLE_DIM), 1.0)
    norm_mix = 1.0 + _normal(ks[2], (DEPTH, D), 0.02)
    norm_ffn = 1.0 + _normal(ks[3], (DEPTH, D), 0.02)
    norm_ple = 1.0 + _normal(ks[4], (DEPTH, D), 0.02)
    norm_final = 1.0 + _normal(ks[5], (D,), 0.02)
    gm_w_in = _normal(ks[6], (N_A, D, 2 * GM_WIDTH), D ** -0.5)
    gm_ln_g = 1.0 + _normal(ks[7], (N_A, GM_WIDTH), 0.02)
    gm_ln_b = _normal(ks[8], (N_A, GM_WIDTH), 0.02)
    gm_w_s = _normal(ks[9], (N_A, GM_HEADS, GM_CHUNK, GM_CHUNK), GM_CHUNK ** -0.5)
    gm_b_s = 1.0 + _normal(ks[10], (N_A, GM_HEADS, GM_CHUNK), 0.02)
    gm_w_out = _normal(ks[11], (N_A, GM_WIDTH, D), GM_WIDTH ** -0.5)
    s5_w_in = _normal(ks[12], (N_B, D, S5_WIDTH), D ** -0.5)
    s5_a_re = -0.5 + _normal(ks[13], (N_B, S5_GROUPS, S5_STATE), 0.01)
    n_idx = jnp.arange(S5_STATE, dtype=jnp.float32)
    s5_a_im = math.pi * n_idx + _normal(ks[14], (N_B, S5_GROUPS, S5_STATE), 0.01)
    s5_log_dt = jax.random.uniform(ks[15], (N_B, S5_GROUPS), jnp.float32,
                                   math.log(DT_MIN), math.log(DT_MAX))
    b_scale = (S5_GROUP ** -0.5) / math.sqrt(2.0)
    s5_b_re = _normal(ks[16], (N_B, S5_GROUPS, S5_STATE, S5_GROUP), b_scale)
    s5_b_im = _normal(ks[17], (N_B, S5_GROUPS, S5_STATE, S5_GROUP), b_scale)
    c_scale = (S5_STATE ** -0.5) / math.sqrt(2.0)
    s5_c_re = _normal(ks[18], (N_B, S5_GROUPS, S5_GROUP, S5_STATE), c_scale)
    s5_c_im = _normal(ks[19], (N_B, S5_GROUPS, S5_GROUP, S5_STATE), c_scale)
    s5_d = _normal(ks[20], (N_B, S5_WIDTH), 1.0)
    s5_w_out = _normal(ks[21], (N_B, S5_WIDTH, 2 * D), S5_WIDTH ** -0.5)
    ffn_w1 = _normal(ks[22], (DEPTH, D, D_FF), D ** -0.5)
    ffn_w3 = _normal(ks[23], (DEPTH, D, D_FF), D ** -0.5)
    ffn_w2 = _normal(ks[24], (DEPTH, D_FF, D), D_FF ** -0.5)
    ple_w_gate = _normal(ks[25], (DEPTH, D, D), D ** -0.5)
    ple_w_proj = _normal(ks[26], (DEPTH, PLE_DIM, D), PLE_DIM ** -0.5)
    return {
        "x": x, "p": p,
        "norm_mix": norm_mix, "norm_ffn": norm_ffn, "norm_ple": norm_ple, "norm_final": norm_final,
        "gm_w_in": gm_w_in, "gm_ln_g": gm_ln_g, "gm_ln_b": gm_ln_b,
        "gm_w_s": gm_w_s, "gm_b_s": gm_b_s, "gm_w_out": gm_w_out,
        "s5_w_in": s5_w_in, "s5_a_re": s5_a_re, "s5_a_im": s5_a_im, "s5_log_dt": s5_log_dt,
        "s5_b_re": s5_b_re, "s5_b_im": s5_b_im, "s5_c_re": s5_c_re, "s5_c_im": s5_c_im,
        "s5_d": s5_d, "s5_w_out": s5_w_out,
        "ffn_w1": ffn_w1, "ffn_w3": ffn_w3, "ffn_w2": ffn_w2,
        "ple_w_gate": ple_w_gate, "ple_w_proj": ple_w_proj,
    }


def _fwd_reference(x, p, norm_mix, norm_ffn, norm_ple, norm_final,
              gm_w_in, gm_ln_g, gm_ln_b, gm_w_s, gm_b_s, gm_w_out,
              s5_w_in, s5_a_re, s5_a_im, s5_log_dt, s5_b_re, s5_b_im, s5_c_re, s5_c_im,
              s5_d, s5_w_out,
              ffn_w1, ffn_w3, ffn_w2, ple_w_gate, ple_w_proj):
    for i in range(DEPTH):
        j = i // N_MIXERS
        h = _rms_norm(x, norm_mix[i])
        if i % N_MIXERS == 0:
            x = x + _gmlp_mixer(h, gm_w_in[j], gm_ln_g[j], gm_ln_b[j], gm_w_s[j], gm_b_s[j], gm_w_out[j])
        else:
            x = x + _s5_mixer(h, s5_w_in[j], s5_a_re[j], s5_a_im[j], s5_log_dt[j],
                              s5_b_re[j], s5_b_im[j], s5_c_re[j], s5_c_im[j], s5_d[j], s5_w_out[j])
        x = x + _swiglu(_rms_norm(x, norm_ffn[i]), ffn_w1[i], ffn_w3[i], ffn_w2[i])
        gate = jax.nn.sigmoid(_rms_norm(x, norm_ple[i]) @ ple_w_gate[i])
        x = x + gate * (p[i] @ ple_w_proj[i])
    return _rms_norm(x, norm_final)


import jax as _jax
import jax.numpy as _jnp

TWIN_FORMAT = 'train_step'
FWD_PARAMS = ['x', 'p', 'norm_mix', 'norm_ffn', 'norm_ple', 'norm_final', 'gm_w_in', 'gm_ln_g', 'gm_ln_b', 'gm_w_s', 'gm_b_s', 'gm_w_out', 's5_w_in', 's5_a_re', 's5_a_im', 's5_log_dt', 's5_b_re', 's5_b_im', 's5_c_re', 's5_c_im', 's5_d', 's5_w_out', 'ffn_w1', 'ffn_w3', 'ffn_w2', 'ple_w_gate', 'ple_w_proj']
TWIN_WEIGHTS = ['norm_mix', 'norm_ffn', 'norm_ple', 'norm_final', 'gm_w_in', 'gm_ln_g', 'gm_ln_b', 'gm_w_s', 'gm_b_s', 'gm_w_out', 's5_w_in', 's5_a_re', 's5_a_im', 's5_log_dt', 's5_b_re', 's5_b_im', 's5_c_re', 's5_c_im', 's5_d', 's5_w_out', 'ffn_w1', 'ffn_w3', 'ffn_w2', 'ple_w_gate', 'ple_w_proj']
TWIN_DIFF_INPUT = 'x'
TWIN_INPUTS = ['x', 'p', 'norm_mix', 'norm_ffn', 'norm_ple', 'norm_final', 'gm_w_in', 'gm_ln_g', 'gm_ln_b', 'gm_w_s', 'gm_b_s', 'gm_w_out', 's5_w_in', 's5_a_re', 's5_a_im', 's5_log_dt', 's5_b_re', 's5_b_im', 's5_c_re', 's5_c_im', 's5_d', 's5_w_out', 'ffn_w1', 'ffn_w3', 'ffn_w2', 'ple_w_gate', 'ple_w_proj', 'loss_target', 'm_norm_mix', 'm_norm_ffn', 'm_norm_ple', 'm_norm_final', 'm_gm_w_in', 'm_gm_ln_g', 'm_gm_ln_b', 'm_gm_w_s', 'm_gm_b_s', 'm_gm_w_out', 'm_s5_w_in', 'm_s5_a_re', 'm_s5_a_im', 'm_s5_log_dt', 'm_s5_b_re', 'm_s5_b_im', 'm_s5_c_re', 'm_s5_c_im', 'm_s5_d', 'm_s5_w_out', 'm_ffn_w1', 'm_ffn_w3', 'm_ffn_w2', 'm_ple_w_gate', 'm_ple_w_proj', 'v_norm_mix', 'v_norm_ffn', 'v_norm_ple', 'v_norm_final', 'v_gm_w_in', 'v_gm_ln_g', 'v_gm_ln_b', 'v_gm_w_s', 'v_gm_b_s', 'v_gm_w_out', 'v_s5_w_in', 'v_s5_a_re', 'v_s5_a_im', 'v_s5_log_dt', 'v_s5_b_re', 'v_s5_b_im', 'v_s5_c_re', 'v_s5_c_im', 'v_s5_d', 'v_s5_w_out', 'v_ffn_w1', 'v_ffn_w3', 'v_ffn_w2', 'v_ple_w_gate', 'v_ple_w_proj']
TWIN_OUTPUTS = ['loss', 'grad_x', 'grad_norm_mix', 'grad_norm_ffn', 'grad_norm_ple', 'grad_norm_final', 'grad_gm_w_in', 'grad_gm_ln_g', 'grad_gm_ln_b', 'grad_gm_w_s', 'grad_gm_b_s', 'grad_gm_w_out', 'grad_s5_w_in', 'grad_s5_a_re', 'grad_s5_a_im', 'grad_s5_log_dt', 'grad_s5_b_re', 'grad_s5_b_im', 'grad_s5_c_re', 'grad_s5_c_im', 'grad_s5_d', 'grad_s5_w_out', 'grad_ffn_w1', 'grad_ffn_w3', 'grad_ffn_w2', 'grad_ple_w_gate', 'grad_ple_w_proj', 'delta_norm_mix', 'delta_norm_ffn', 'delta_norm_ple', 'delta_norm_final', 'delta_gm_w_in', 'delta_gm_ln_g', 'delta_gm_ln_b', 'delta_gm_w_s', 'delta_gm_b_s', 'delta_gm_w_out', 'delta_s5_w_in', 'delta_s5_a_re', 'delta_s5_a_im', 'delta_s5_log_dt', 'delta_s5_b_re', 'delta_s5_b_im', 'delta_s5_c_re', 'delta_s5_c_im', 'delta_s5_d', 'delta_s5_w_out', 'delta_ffn_w1', 'delta_ffn_w3', 'delta_ffn_w2', 'delta_ple_w_gate', 'delta_ple_w_proj', 'new_m_norm_mix', 'new_m_norm_ffn', 'new_m_norm_ple', 'new_m_norm_final', 'new_m_gm_w_in', 'new_m_gm_ln_g', 'new_m_gm_ln_b', 'new_m_gm_w_s', 'new_m_gm_b_s', 'new_m_gm_w_out', 'new_m_s5_w_in', 'new_m_s5_a_re', 'new_m_s5_a_im', 'new_m_s5_log_dt', 'new_m_s5_b_re', 'new_m_s5_b_im', 'new_m_s5_c_re', 'new_m_s5_c_im', 'new_m_s5_d', 'new_m_s5_w_out', 'new_m_ffn_w1', 'new_m_ffn_w3', 'new_m_ffn_w2', 'new_m_ple_w_gate', 'new_m_ple_w_proj', 'new_v_norm_mix', 'new_v_norm_ffn', 'new_v_norm_ple', 'new_v_norm_final', 'new_v_gm_w_in', 'new_v_gm_ln_g', 'new_v_gm_ln_b', 'new_v_gm_w_s', 'new_v_gm_b_s', 'new_v_gm_w_out', 'new_v_s5_w_in', 'new_v_s5_a_re', 'new_v_s5_a_im', 'new_v_s5_log_dt', 'new_v_s5_b_re', 'new_v_s5_b_im', 'new_v_s5_c_re', 'new_v_s5_c_im', 'new_v_s5_d', 'new_v_s5_w_out', 'new_v_ffn_w1', 'new_v_ffn_w3', 'new_v_ffn_w2', 'new_v_ple_w_gate', 'new_v_ple_w_proj']
TWIN_LEAF_KINDS = {'loss': 'loss', 'grad_x': 'grad_x', 'grad_norm_mix': 'grad_w', 'grad_norm_ffn': 'grad_w', 'grad_norm_ple': 'grad_w', 'grad_norm_final': 'grad_w', 'grad_gm_w_in': 'grad_w', 'grad_gm_ln_g': 'grad_w', 'grad_gm_ln_b': 'grad_w', 'grad_gm_w_s': 'grad_w', 'grad_gm_b_s': 'grad_w', 'grad_gm_w_out': 'grad_w', 'grad_s5_w_in': 'grad_w', 'grad_s5_a_re': 'grad_w', 'grad_s5_a_im': 'grad_w', 'grad_s5_log_dt': 'grad_w', 'grad_s5_b_re': 'grad_w', 'grad_s5_b_im': 'grad_w', 'grad_s5_c_re': 'grad_w', 'grad_s5_c_im': 'grad_w', 'grad_s5_d': 'grad_w', 'grad_s5_w_out': 'grad_w', 'grad_ffn_w1': 'grad_w', 'grad_ffn_w3': 'grad_w', 'grad_ffn_w2': 'grad_w', 'grad_ple_w_gate': 'grad_w', 'grad_ple_w_proj': 'grad_w', 'delta_norm_mix': 'delta_w', 'delta_norm_ffn': 'delta_w', 'delta_norm_ple': 'delta_w', 'delta_norm_final': 'delta_w', 'delta_gm_w_in': 'delta_w', 'delta_gm_ln_g': 'delta_w', 'delta_gm_ln_b': 'delta_w', 'delta_gm_w_s': 'delta_w', 'delta_gm_b_s': 'delta_w', 'delta_gm_w_out': 'delta_w', 'delta_s5_w_in': 'delta_w', 'delta_s5_a_re': 'delta_w', 'delta_s5_a_im': 'delta_w', 'delta_s5_log_dt': 'delta_w', 'delta_s5_b_re': 'delta_w', 'delta_s5_b_im': 'delta_w', 'delta_s5_c_re': 'delta_w', 'delta_s5_c_im': 'delta_w', 'delta_s5_d': 'delta_w', 'delta_s5_w_out': 'delta_w', 'delta_ffn_w1': 'delta_w', 'delta_ffn_w3': 'delta_w', 'delta_ffn_w2': 'delta_w', 'delta_ple_w_gate': 'delta_w', 'delta_ple_w_proj': 'delta_w', 'new_m_norm_mix': 'new_m', 'new_m_norm_ffn': 'new_m', 'new_m_norm_ple': 'new_m', 'new_m_norm_final': 'new_m', 'new_m_gm_w_in': 'new_m', 'new_m_gm_ln_g': 'new_m', 'new_m_gm_ln_b': 'new_m', 'new_m_gm_w_s': 'new_m', 'new_m_gm_b_s': 'new_m', 'new_m_gm_w_out': 'new_m', 'new_m_s5_w_in': 'new_m', 'new_m_s5_a_re': 'new_m', 'new_m_s5_a_im': 'new_m', 'new_m_s5_log_dt': 'new_m', 'new_m_s5_b_re': 'new_m', 'new_m_s5_b_im': 'new_m', 'new_m_s5_c_re': 'new_m', 'new_m_s5_c_im': 'new_m', 'new_m_s5_d': 'new_m', 'new_m_s5_w_out': 'new_m', 'new_m_ffn_w1': 'new_m', 'new_m_ffn_w3': 'new_m', 'new_m_ffn_w2': 'new_m', 'new_m_ple_w_gate': 'new_m', 'new_m_ple_w_proj': 'new_m', 'new_v_norm_mix': 'new_v', 'new_v_norm_ffn': 'new_v', 'new_v_norm_ple': 'new_v', 'new_v_norm_final': 'new_v', 'new_v_gm_w_in': 'new_v', 'new_v_gm_ln_g': 'new_v', 'new_v_gm_ln_b': 'new_v', 'new_v_gm_w_s': 'new_v', 'new_v_gm_b_s': 'new_v', 'new_v_gm_w_out': 'new_v', 'new_v_s5_w_in': 'new_v', 'new_v_s5_a_re': 'new_v', 'new_v_s5_a_im': 'new_v', 'new_v_s5_log_dt': 'new_v', 'new_v_s5_b_re': 'new_v', 'new_v_s5_b_im': 'new_v', 'new_v_s5_c_re': 'new_v', 'new_v_s5_c_im': 'new_v', 'new_v_s5_d': 'new_v', 'new_v_s5_w_out': 'new_v', 'new_v_ffn_w1': 'new_v', 'new_v_ffn_w3': 'new_v', 'new_v_ffn_w2': 'new_v', 'new_v_ple_w_gate': 'new_v', 'new_v_ple_w_proj': 'new_v'}


def _forward(args):
    return _fwd_reference(*[args[k] for k in FWD_PARAMS])


def _output_shape():
    out = _jax.eval_shape(lambda: _forward(_fwd_setup_inputs(0)))
    return out.shape, out.dtype

N_MICROBATCH = 1
ADAM_LR = 0.001
ADAM_B1 = 0.9
ADAM_B2 = 0.999
ADAM_EPS = 1e-08
ADAM_WD = 0.01
ADAM_STEP = 10
PER_EXAMPLE_BATCH_AXIS = {'x': 0, 'p': 1, 'loss_target': 0}
SHARED_INPUTS = []
_WEIGHT_DTYPES = {'norm_mix': _jnp.float32, 'norm_ffn': _jnp.float32, 'norm_ple': _jnp.float32, 'norm_final': _jnp.float32, 'gm_w_in': _jnp.float32, 'gm_ln_g': _jnp.float32, 'gm_ln_b': _jnp.float32, 'gm_w_s': _jnp.float32, 'gm_b_s': _jnp.float32, 'gm_w_out': _jnp.float32, 's5_w_in': _jnp.float32, 's5_a_re': _jnp.float32, 's5_a_im': _jnp.float32, 's5_log_dt': _jnp.float32, 's5_b_re': _jnp.float32, 's5_b_im': _jnp.float32, 's5_c_re': _jnp.float32, 's5_c_im': _jnp.float32, 's5_d': _jnp.float32, 's5_w_out': _jnp.float32, 'ffn_w1': _jnp.float32, 'ffn_w3': _jnp.float32, 'ffn_w2': _jnp.float32, 'ple_w_gate': _jnp.float32, 'ple_w_proj': _jnp.float32}
MOMENT_SCALE = {'norm_mix': 3.752754e-02, 'norm_ffn': 3.446833e-02, 'norm_ple': 8.520867e-03, 'norm_final': 8.017960e+00, 'gm_w_in': 3.560176e-02, 'gm_ln_g': 2.409095e-02, 'gm_ln_b': 2.388128e-02, 'gm_w_s': 2.341138e-02, 'gm_b_s': 3.443403e-02, 'gm_w_out': 4.195780e-02, 's5_w_in': 1.541675e-02, 's5_a_re': 9.091384e-04, 's5_a_im': 8.197892e-04, 's5_log_dt': 6.375439e-01, 's5_b_re': 5.568517e-04, 's5_b_im': 5.513548e-04, 's5_c_re': 1.113041e-03, 's5_c_im': 1.113806e-03, 's5_d': 1.620429e-02, 's5_w_out': 1.116612e-02, 'ffn_w1': 1.475225e-02, 'ffn_w3': 1.430918e-02, 'ffn_w2': 2.373766e-02, 'ple_w_gate': 8.456548e-03, 'ple_w_proj': 2.126767e-02}


def _to_microbatches(a, axis):
    t = _jnp.moveaxis(a, axis, 0)
    t = t.reshape((N_MICROBATCH, t.shape[0] // N_MICROBATCH) + t.shape[1:])
    return _jnp.moveaxis(t, 1, axis + 1)


def setup_inputs(seed: int = 0) -> dict:
    inp = _fwd_setup_inputs(seed)
    key = _jax.random.fold_in(_jax.random.key(seed), 7919)
    shape, _ = _output_shape()
    out = dict(inp)
    out["loss_target"] = _jax.random.normal(_jax.random.fold_in(key, 0), shape, _jnp.float32)
    for i, name in enumerate(TWIN_WEIGHTS):
        w = inp[name].astype(_jnp.float32)
        if MOMENT_SCALE is None:
            s = _jnp.sqrt(_jnp.mean(_jnp.square(w)) + 1e-30)
        else:
            s = MOMENT_SCALE[name]
        km, kv = _jax.random.split(_jax.random.fold_in(key, i + 1))
        out[name] = w
        out["m_" + name] = s * _jax.random.normal(km, w.shape, _jnp.float32)
        out["v_" + name] = (s * s) * _jax.random.uniform(kv, w.shape, _jnp.float32, 0.5, 1.5)
    if N_MICROBATCH > 1:
        for name, axis in PER_EXAMPLE_BATCH_AXIS.items():
            out[name] = _to_microbatches(out[name], axis)
    return {'x': out['x'], 'p': out['p'], 'norm_mix': out['norm_mix'], 'norm_ffn': out['norm_ffn'], 'norm_ple': out['norm_ple'], 'norm_final': out['norm_final'], 'gm_w_in': out['gm_w_in'], 'gm_ln_g': out['gm_ln_g'], 'gm_ln_b': out['gm_ln_b'], 'gm_w_s': out['gm_w_s'], 'gm_b_s': out['gm_b_s'], 'gm_w_out': out['gm_w_out'], 's5_w_in': out['s5_w_in'], 's5_a_re': out['s5_a_re'], 's5_a_im': out['s5_a_im'], 's5_log_dt': out['s5_log_dt'], 's5_b_re': out['s5_b_re'], 's5_b_im': out['s5_b_im'], 's5_c_re': out['s5_c_re'], 's5_c_im': out['s5_c_im'], 's5_d': out['s5_d'], 's5_w_out': out['s5_w_out'], 'ffn_w1': out['ffn_w1'], 'ffn_w3': out['ffn_w3'], 'ffn_w2': out['ffn_w2'], 'ple_w_gate': out['ple_w_gate'], 'ple_w_proj': out['ple_w_proj'], 'loss_target': out['loss_target'], 'm_norm_mix': out['m_norm_mix'], 'm_norm_ffn': out['m_norm_ffn'], 'm_norm_ple': out['m_norm_ple'], 'm_norm_final': out['m_norm_final'], 'm_gm_w_in': out['m_gm_w_in'], 'm_gm_ln_g': out['m_gm_ln_g'], 'm_gm_ln_b': out['m_gm_ln_b'], 'm_gm_w_s': out['m_gm_w_s'], 'm_gm_b_s': out['m_gm_b_s'], 'm_gm_w_out': out['m_gm_w_out'], 'm_s5_w_in': out['m_s5_w_in'], 'm_s5_a_re': out['m_s5_a_re'], 'm_s5_a_im': out['m_s5_a_im'], 'm_s5_log_dt': out['m_s5_log_dt'], 'm_s5_b_re': out['m_s5_b_re'], 'm_s5_b_im': out['m_s5_b_im'], 'm_s5_c_re': out['m_s5_c_re'], 'm_s5_c_im': out['m_s5_c_im'], 'm_s5_d': out['m_s5_d'], 'm_s5_w_out': out['m_s5_w_out'], 'm_ffn_w1': out['m_ffn_w1'], 'm_ffn_w3': out['m_ffn_w3'], 'm_ffn_w2': out['m_ffn_w2'], 'm_ple_w_gate': out['m_ple_w_gate'], 'm_ple_w_proj': out['m_ple_w_proj'], 'v_norm_mix': out['v_norm_mix'], 'v_norm_ffn': out['v_norm_ffn'], 'v_norm_ple': out['v_norm_ple'], 'v_norm_final': out['v_norm_final'], 'v_gm_w_in': out['v_gm_w_in'], 'v_gm_ln_g': out['v_gm_ln_g'], 'v_gm_ln_b': out['v_gm_ln_b'], 'v_gm_w_s': out['v_gm_w_s'], 'v_gm_b_s': out['v_gm_b_s'], 'v_gm_w_out': out['v_gm_w_out'], 'v_s5_w_in': out['v_s5_w_in'], 'v_s5_a_re': out['v_s5_a_re'], 'v_s5_a_im': out['v_s5_a_im'], 'v_s5_log_dt': out['v_s5_log_dt'], 'v_s5_b_re': out['v_s5_b_re'], 'v_s5_b_im': out['v_s5_b_im'], 'v_s5_c_re': out['v_s5_c_re'], 'v_s5_c_im': out['v_s5_c_im'], 'v_s5_d': out['v_s5_d'], 'v_s5_w_out': out['v_s5_w_out'], 'v_ffn_w1': out['v_ffn_w1'], 'v_ffn_w3': out['v_ffn_w3'], 'v_ffn_w2': out['v_ffn_w2'], 'v_ple_w_gate': out['v_ple_w_gate'], 'v_ple_w_proj': out['v_ple_w_proj']}


def _loss(weights, diff, rest, loss_target):
    with _jax.named_scope("forward"):
        args = {**rest, TWIN_DIFF_INPUT: diff, **{k: w.astype(_WEIGHT_DTYPES[k]) for k, w in weights.items()}}
        y = _forward(args)
    with _jax.named_scope("loss_head"):
        err = _jnp.square(y.astype(_jnp.float32) - loss_target)
        return 0.5 * _jnp.sum(_jnp.mean(err, axis=-1)) if err.ndim else 0.5 * err


def _adamw(w, g, m, v):
    m = ADAM_B1 * m + (1.0 - ADAM_B1) * g
    v = ADAM_B2 * v + (1.0 - ADAM_B2) * _jnp.square(g)
    m_hat = m / (1.0 - ADAM_B1 ** ADAM_STEP)
    v_hat = v / (1.0 - ADAM_B2 ** ADAM_STEP)
    delta = -ADAM_LR * (m_hat / (_jnp.sqrt(v_hat) + ADAM_EPS) + ADAM_WD * w)
    return delta, m, v


def reference(x, p, norm_mix, norm_ffn, norm_ple, norm_final, gm_w_in, gm_ln_g, gm_ln_b, gm_w_s, gm_b_s, gm_w_out, s5_w_in, s5_a_re, s5_a_im, s5_log_dt, s5_b_re, s5_b_im, s5_c_re, s5_c_im, s5_d, s5_w_out, ffn_w1, ffn_w3, ffn_w2, ple_w_gate, ple_w_proj, loss_target, m_norm_mix, m_norm_ffn, m_norm_ple, m_norm_final, m_gm_w_in, m_gm_ln_g, m_gm_ln_b, m_gm_w_s, m_gm_b_s, m_gm_w_out, m_s5_w_in, m_s5_a_re, m_s5_a_im, m_s5_log_dt, m_s5_b_re, m_s5_b_im, m_s5_c_re, m_s5_c_im, m_s5_d, m_s5_w_out, m_ffn_w1, m_ffn_w3, m_ffn_w2, m_ple_w_gate, m_ple_w_proj, v_norm_mix, v_norm_ffn, v_norm_ple, v_norm_final, v_gm_w_in, v_gm_ln_g, v_gm_ln_b, v_gm_w_s, v_gm_b_s, v_gm_w_out, v_s5_w_in, v_s5_a_re, v_s5_a_im, v_s5_log_dt, v_s5_b_re, v_s5_b_im, v_s5_c_re, v_s5_c_im, v_s5_d, v_s5_w_out, v_ffn_w1, v_ffn_w3, v_ffn_w2, v_ple_w_gate, v_ple_w_proj):
    given = dict(x=x, p=p, norm_mix=norm_mix, norm_ffn=norm_ffn, norm_ple=norm_ple, norm_final=norm_final, gm_w_in=gm_w_in, gm_ln_g=gm_ln_g, gm_ln_b=gm_ln_b, gm_w_s=gm_w_s, gm_b_s=gm_b_s, gm_w_out=gm_w_out, s5_w_in=s5_w_in, s5_a_re=s5_a_re, s5_a_im=s5_a_im, s5_log_dt=s5_log_dt, s5_b_re=s5_b_re, s5_b_im=s5_b_im, s5_c_re=s5_c_re, s5_c_im=s5_c_im, s5_d=s5_d, s5_w_out=s5_w_out, ffn_w1=ffn_w1, ffn_w3=ffn_w3, ffn_w2=ffn_w2, ple_w_gate=ple_w_gate, ple_w_proj=ple_w_proj, loss_target=loss_target, m_norm_mix=m_norm_mix, m_norm_ffn=m_norm_ffn, m_norm_ple=m_norm_ple, m_norm_final=m_norm_final, m_gm_w_in=m_gm_w_in, m_gm_ln_g=m_gm_ln_g, m_gm_ln_b=m_gm_ln_b, m_gm_w_s=m_gm_w_s, m_gm_b_s=m_gm_b_s, m_gm_w_out=m_gm_w_out, m_s5_w_in=m_s5_w_in, m_s5_a_re=m_s5_a_re, m_s5_a_im=m_s5_a_im, m_s5_log_dt=m_s5_log_dt, m_s5_b_re=m_s5_b_re, m_s5_b_im=m_s5_b_im, m_s5_c_re=m_s5_c_re, m_s5_c_im=m_s5_c_im, m_s5_d=m_s5_d, m_s5_w_out=m_s5_w_out, m_ffn_w1=m_ffn_w1, m_ffn_w3=m_ffn_w3, m_ffn_w2=m_ffn_w2, m_ple_w_gate=m_ple_w_gate, m_ple_w_proj=m_ple_w_proj, v_norm_mix=v_norm_mix, v_norm_ffn=v_norm_ffn, v_norm_ple=v_norm_ple, v_norm_final=v_norm_final, v_gm_w_in=v_gm_w_in, v_gm_ln_g=v_gm_ln_g, v_gm_ln_b=v_gm_ln_b, v_gm_w_s=v_gm_w_s, v_gm_b_s=v_gm_b_s, v_gm_w_out=v_gm_w_out, v_s5_w_in=v_s5_w_in, v_s5_a_re=v_s5_a_re, v_s5_a_im=v_s5_a_im, v_s5_log_dt=v_s5_log_dt, v_s5_b_re=v_s5_b_re, v_s5_b_im=v_s5_b_im, v_s5_c_re=v_s5_c_re, v_s5_c_im=v_s5_c_im, v_s5_d=v_s5_d, v_s5_w_out=v_s5_w_out, v_ffn_w1=v_ffn_w1, v_ffn_w3=v_ffn_w3, v_ffn_w2=v_ffn_w2, v_ple_w_gate=v_ple_w_gate, v_ple_w_proj=v_ple_w_proj)
    weights = {n: given[n] for n in TWIN_WEIGHTS}
    shared = {n: given[n] for n in SHARED_INPUTS}
    per_example = {n: given[n] for n in ['x', 'p']}
    grad_fn = _jax.value_and_grad(_loss, argnums=(0, 1))

    def one_microbatch(ex, loss_target):
        ex = dict(ex)
        diff = ex.pop(TWIN_DIFF_INPUT)
        return grad_fn(weights, diff, {**shared, **ex}, loss_target)

    if N_MICROBATCH == 1:
        loss, (grad_w, grad_x) = one_microbatch(per_example, given["loss_target"])
    else:
        def body(carry, xs):
            loss_sum, grad_sum = carry
            l_k, (gw_k, gx_k) = one_microbatch(xs[0], xs[1])
            with _jax.named_scope("update"):
                return (loss_sum + l_k, _jax.tree.map(_jnp.add, grad_sum, gw_k)), gx_k

        init = (_jnp.zeros((), _jnp.float32), _jax.tree.map(_jnp.zeros_like, weights))
        (loss, grad_w), grad_x = _jax.lax.scan(body, init, (per_example, given["loss_target"]))
    with _jax.named_scope("update"):
        delta_w, new_m, new_v = {}, {}, {}
        for n in TWIN_WEIGHTS:
            delta_w[n], new_m[n], new_v[n] = _adamw(weights[n], grad_w[n], given["m_" + n], given["v_" + n])
    return (loss, grad_x, *[grad_w[n] for n in TWIN_WEIGHTS], *[delta_w[n] for n in TWIN_WEIGHTS],
            *[new_m[n] for n in TWIN_WEIGHTS], *[new_v[n] for n in TWIN_WEIGHTS])
```

```python
import functools
import math

import jax
import jax.numpy as jnp
from jax import lax
from jax.experimental import pallas as pl
from jax.experimental.pallas import tpu as pltpu

F32 = jnp.float32
_MXU = jnp.bfloat16
_WIRE = jnp.bfloat16
_EPS = 1e-6
_VMEM_LIMIT = 48 * 1024 * 1024
_LANES = 128
_MESH = pl.DeviceIdType.MESH

_LR, _B1, _B2, _AEPS, _WD, _STEP = 0.001, 0.9, 0.999, 1e-08, 0.01, 10

_GM_CHUNK = 128
_GM_HEADS = 16
_S5_GT = 8
_S5_P = 64
_S5_C = 16

_NN = (((1,), (0,)), ((), ()))
_NT = (((1,), (1,)), ((), ()))
_TN = (((0,), (0,)), ((), ()))


def _cparams(sem):
    return pltpu.CompilerParams(dimension_semantics=sem, vmem_limit_bytes=_VMEM_LIMIT)


def _sig(x):
    return 1.0 / (1.0 + jnp.exp(-x))


_GC = math.sqrt(2.0 / math.pi)


def _gelu(x):
    return 0.5 * x * (1.0 + jnp.tanh(_GC * (x + 0.044715 * (x * x * x))))


def _gelu_grad(x):
    t = jnp.tanh(_GC * (x + 0.044715 * (x * x * x)))
    return 0.5 * (1.0 + t) + 0.5 * x * (1.0 - t * t) * (_GC * (1.0 + 3.0 * 0.044715 * x * x))


def _dot(a, b, dn):
    return lax.dot_general(a.astype(_MXU), b.astype(_MXU), dn, preferred_element_type=F32)


class _W:
    def __init__(self, arr, kind):
        self.a, self.kind = arr, kind
        self.R, self.C = arr.shape[2], arr.shape[3]

    def full_shape(self):
        return (2 * self.R, 4 * self.C) if self.kind == "col" else (4 * self.R, 2 * self.C)


def _part_index(kind, R, C, tr, tc, rb, cb):
    nr, nc = R // tr, C // tc
    if kind == "col":
        return cb // nc, rb // nr, rb % nr, cb % nc
    return rb // nr, cb // nc, rb % nr, cb % nc


def _wspec(w, tr, tc, rb_fn, cb_fn):
    assert w.R % tr == 0 and w.C % tc == 0, (w.R, w.C, tr, tc)

    def imap(i, j, k):
        return _part_index(w.kind, w.R, w.C, tr, tc, rb_fn(i, j, k), cb_fn(i, j, k))

    return pl.BlockSpec((None, None, tr, tc), imap)


def _gspec(kind, R, C, tr, tc):
    assert R % tr == 0 and C % tc == 0, (R, C, tr, tc)

    def imap(i, j, k):
        part, half, rbi, cbi = _part_index(kind, R, C, tr, tc, i, j)
        return half, part, rbi, cbi

    return pl.BlockSpec((None, None, tr, tc), imap)


def _mm(name, grid, a_ops, b_ops, pairs, acc_shape, n_acc, extras, outs, epilogue):
    nk = grid[2]
    na, nb, ne, no = len(a_ops), len(b_ops), len(extras), len(outs)

    def body(*refs):
        a_refs = refs[:na]
        b_refs = refs[na:na + nb]
        e_refs = refs[na + nb:na + nb + ne]
        o_refs = refs[na + nb + ne:na + nb + ne + no]
        acc_refs = refs[na + nb + ne + no:]
        k = pl.program_id(2)

        @pl.when(k == 0)
        def _():
            for acc in acc_refs:
                acc[...] = jnp.zeros(acc.shape, F32)

        for ai, bi, ci, dn in pairs:
            acc_refs[ci][...] += _dot(a_refs[ai][...], b_refs[bi][...], dn)

        @pl.when(k == nk - 1)
        def _():
            res = epilogue([acc[...] for acc in acc_refs], [e[...] for e in e_refs])
            for o, r in zip(o_refs, res):
                o[...] = r.astype(o.dtype)

    ops = list(a_ops) + list(b_ops) + list(extras)
    return pl.pallas_call(
        body, name=name, grid=grid,
        in_specs=[s for _, s in ops],
        out_specs=[s for _, s in outs],
        out_shape=[s for s, _ in outs],
        scratch_shapes=[pltpu.VMEM(acc_shape, F32) for _ in range(n_acc)],
        compiler_params=_cparams(("parallel", "parallel", "arbitrary")),
    )(*[a for a, _ in ops])


def _bs(shape, fn):
    return pl.BlockSpec(shape, fn)


def _tile_m(L):
    return min(L, 512)


def _mm_nn(name, x, ws, tk, tn, n_out, extras, outs_sd, epilogue, tm=None, cb_offsets=None):
    M, K = x.shape
    tm = tm or _tile_m(M)
    grid = (M // tm, n_out // tn, K // tk)
    a_ops = [(x, _bs((tm, tk), lambda i, j, k: (i, k)))]
    cb_offsets = cb_offsets or [0] * len(ws)
    b_ops = [(w.a, _wspec(w, tk, tn, lambda i, j, k: k, (lambda off: lambda i, j, k: j + off)(off)))
             for w, off in zip(ws, cb_offsets)]
    pairs = [(0, bi, bi, _NN) for bi in range(len(ws))]
    mn = _bs((tm, tn), lambda i, j, k: (i, j))
    ex = [(e, mn) for e in extras]
    outs = [(jax.ShapeDtypeStruct((M, n_out), dt), mn) for dt in outs_sd]
    return _mm(name, grid, a_ops, b_ops, pairs, (tm, tn), len(ws), ex, outs, epilogue)


def _mm_nt(name, xs, ws, tn, tk, extras, outs_sd, epilogue, tm=None):
    M, Nw = xs[0].shape
    Kw = ws[0].full_shape()[0]
    tm = tm or _tile_m(M)
    grid = (M // tm, Kw // tn, Nw // tk)
    a_ops = [(x, _bs((tm, tk), lambda i, j, k: (i, k))) for x in xs]
    b_ops = [(w.a, _wspec(w, tn, tk, lambda i, j, k: j, lambda i, j, k: k)) for w in ws]
    pairs = [(i, i, 0, _NT) for i in range(len(ws))]
    mn = _bs((tm, tn), lambda i, j, k: (i, j))
    ex = [(e, mn) for e in extras]
    outs = [(jax.ShapeDtypeStruct((M, Kw), dt), mn) for dt in outs_sd]
    return _mm(name, grid, a_ops, b_ops, pairs, (tm, tn), 1, ex, outs, epilogue)


def _mm_tn(name, x, dys, kind, R, C, tm, tn, tk=None):
    L, Kw = x.shape
    Nw = dys[0].shape[1]
    tk = tk or min(L, 512)
    grid = (Kw // tm, Nw // tn, L // tk)
    a_ops = [(x, _bs((tk, tm), lambda i, j, k: (k, i)))]
    b_ops = [(dy, _bs((tk, tn), lambda i, j, k: (k, j))) for dy in dys]
    pairs = [(0, bi, bi, _TN) for bi in range(len(dys))]
    gs = _gspec(kind, R, C, tm, tn)
    outs = [(jax.ShapeDtypeStruct((2, 4, R, C), _WIRE), gs) for _ in dys]
    return _mm(name, grid, a_ops, b_ops, pairs, (tm, tn), len(dys), [], outs, lambda accs, ex: accs)


def _row_tile(L):
    return min(L, 256)


def _rowwise(name, body, ins, outs, L, acc_outs=()):
    tr = _row_tile(L)
    n_in, n_out = len(ins), len(outs)

    def kbody(*refs):
        i_refs, o_refs, a_refs = refs[:n_in], refs[n_in:n_in + n_out], refs[n_in + n_out:]
        res, sums = body(*[r[...] for r in i_refs])
        for o, r in zip(o_refs, res):
            o[...] = r.astype(o.dtype)
        if a_refs:
            @pl.when(pl.program_id(0) == 0)
            def _():
                for a in a_refs:
                    a[...] = jnp.zeros(a.shape, F32)
            for a, s in zip(a_refs, sums):
                a[...] += s

    in_specs = []
    for arr, kind in ins:
        if kind == "row":
            in_specs.append(pl.BlockSpec((tr, arr.shape[1]), lambda i: (i, 0)))
        else:
            in_specs.append(pl.BlockSpec(arr.shape, lambda i: (0, 0)))
    out_specs = [pl.BlockSpec((tr, c), lambda i: (i, 0)) for c, _ in outs]
    out_shape = [jax.ShapeDtypeStruct((L, c), dt) for c, dt in outs]
    out_specs += [pl.BlockSpec((1, c), lambda i: (0, 0)) for c in acc_outs]
    out_shape += [jax.ShapeDtypeStruct((1, c), F32) for c in acc_outs]
    return pl.pallas_call(
        kbody, name=name, grid=(L // tr,), in_specs=in_specs, out_specs=out_specs, out_shape=out_shape,
        compiler_params=_cparams(("arbitrary",)),
    )(*[a for a, _ in ins])


def _rms_fwd(name, x, g):
    def body(xv, gv):
        r = lax.rsqrt(jnp.mean(xv * xv, axis=-1, keepdims=True) + _EPS)
        return [xv * r * gv], []
    return _rowwise(name, body, [(x, "row"), (g, "vec")], [(x.shape[1], _MXU)], x.shape[0])[0]


def _rms_bwd(name, dh, x, g, dres):
    def body(dhv, xv, gv, dr):
        r = lax.rsqrt(jnp.mean(xv * xv, axis=-1, keepdims=True) + _EPS)
        xh = xv * r
        dxh = dhv * gv
        dx = r * (dxh - xh * jnp.mean(dxh * xh, axis=-1, keepdims=True))
        return [dr + dx], [jnp.sum(dhv * xh, axis=0, keepdims=True)]
    D = x.shape[1]
    return _rowwise(name, body, [(dh, "row"), (x, "row"), (g, "vec"), (dres, "row")], [(D, F32)], x.shape[0], [D])


def _loss_head(x, g, target):
    D = x.shape[1]

    def body(xv, gv, tv):
        r = lax.rsqrt(jnp.mean(xv * xv, axis=-1, keepdims=True) + _EPS)
        xh = xv * r
        e = xh * gv - tv
        dy = e * (1.0 / D)
        dxh = dy * gv
        dx = r * (dxh - xh * jnp.mean(dxh * xh, axis=-1, keepdims=True))
        row_loss = 0.5 * jnp.mean(e * e, axis=-1, keepdims=True)
        lsum = jnp.sum(row_loss, axis=0, keepdims=True) + jnp.zeros((1, _LANES), F32)
        return [dx], [jnp.sum(dy * xh, axis=0, keepdims=True), lsum]
    return _rowwise("loss_head", body, [(x, "row"), (g, "vec"), (target, "row")], [(D, F32)], x.shape[0], [D, _LANES])


def _ple_bwd_elem(dx, pp, gt):
    def body(dxv, ppv, gtv):
        gt32 = gtv.astype(F32)
        return [dxv * ppv * gt32 * (1.0 - gt32), dxv * gt32], []
    D = dx.shape[1]
    return _rowwise("ple_bwd_elem", body, [(dx, "row"), (pp, "row"), (gt, "row")], [(D, _MXU), (D, _MXU)], dx.shape[0])


def _glu_bwd_elem(dx, val, sg):
    def body(dxv, vv, sv):
        v32, s32 = vv.astype(F32), sv.astype(F32)
        return [jnp.concatenate([dxv * s32, dxv * v32 * s32 * (1.0 - s32)], axis=1)], []
    D = dx.shape[1]
    return _rowwise("glu_bwd_elem", body, [(dx, "row"), (val, "row"), (sg, "row")], [(2 * D, _MXU)], dx.shape[0])[0]


def _gm_common(z, ln_g, ln_b, wc_bf, bsT):
    W = z.shape[1] // 2
    zu, zv = z[:, :W], z[:, W:]
    u, v = _gelu(zu), _gelu(zv)
    mu = jnp.mean(v, axis=-1, keepdims=True)
    vc = v - mu
    rstd = lax.rsqrt(jnp.mean(vc * vc, axis=-1, keepdims=True) + _EPS)
    vh = vc * rstd
    vn = vh * ln_g + ln_b
    vnb = vn.astype(_MXU)
    svs = []
    for h in range(_GM_HEADS):
        sl = slice(h * _LANES, (h + 1) * _LANES)
        svs.append(_dot(wc_bf[h], vnb[:, sl], _NN) + bsT[:, h:h + 1])
    return zu, zv, u, vh, rstd, vnb, svs


def _causal(w):
    t = lax.broadcasted_iota(jnp.int32, w.shape, w.ndim - 2)
    s = lax.broadcasted_iota(jnp.int32, w.shape, w.ndim - 1)
    return jnp.where(s <= t, w, jnp.zeros_like(w))


def _gmlp_fwd(z, ln_g, ln_b, w_s, bsT):
    L, W2 = z.shape
    W = W2 // 2

    def body(z_ref, g_ref, b_ref, ws_ref, bs_ref, m_ref):
        wc = _causal(ws_ref[...]).astype(_MXU)
        _, _, u, _, _, _, svs = _gm_common(z_ref[...], g_ref[...], b_ref[...], wc, bs_ref[...])
        for h in range(_GM_HEADS):
            sl = slice(h * _LANES, (h + 1) * _LANES)
            m_ref[:, sl] = (u[:, sl] * svs[h]).astype(m_ref.dtype)

    return pl.pallas_call(
        body, name="gmlp_fwd", grid=(L // _GM_CHUNK,),
        in_specs=[pl.BlockSpec((_GM_CHUNK, W2), lambda n: (n, 0)),
                  pl.BlockSpec((1, W), lambda n: (0, 0)), pl.BlockSpec((1, W), lambda n: (0, 0)),
                  pl.BlockSpec(w_s.shape, lambda n: (0, 0, 0)), pl.BlockSpec(bsT.shape, lambda n: (0, 0))],
        out_specs=pl.BlockSpec((_GM_CHUNK, W), lambda n: (n, 0)),
        out_shape=jax.ShapeDtypeStruct((L, W), _MXU),
        compiler_params=_cparams(("arbitrary",)),
    )(z, ln_g, ln_b, w_s, bsT)


def _gmlp_bwd(z, dm, ln_g, ln_b, w_s, bsT):
    L, W2 = z.shape
    W = W2 // 2
    T = _GM_CHUNK

    def body(z_ref, dm_ref, g_ref, b_ref, ws_ref, bs_ref, dz_ref, dws_ref, dbs_ref, dg_ref, db_ref):
        @pl.when(pl.program_id(0) == 0)
        def _():
            dws_ref[...] = jnp.zeros(dws_ref.shape, F32)
            dbs_ref[...] = jnp.zeros(dbs_ref.shape, F32)
            dg_ref[...] = jnp.zeros(dg_ref.shape, F32)
            db_ref[...] = jnp.zeros(db_ref.shape, F32)

        wc = _causal(ws_ref[...]).astype(_MXU)
        ln_g_v = g_ref[...]
        zu, zv, u, vh, rstd, vnb, svs = _gm_common(z_ref[...], ln_g_v, b_ref[...], wc, bs_ref[...])
        dmv = dm_ref[...]
        lane = lax.broadcasted_iota(jnp.int32, (T, _LANES), 1)
        dbs = jnp.zeros((T, _LANES), F32)
        dvn_parts = []
        for h in range(_GM_HEADS):
            sl = slice(h * _LANES, (h + 1) * _LANES)
            dsv = dmv[:, sl] * u[:, sl]
            dz_ref[:, sl] = (dmv[:, sl] * svs[h] * _gelu_grad(zu[:, sl])).astype(dz_ref.dtype)
            dbs = dbs + jnp.where(lane == h, jnp.sum(dsv, axis=1, keepdims=True), 0.0)
            dsvb = dsv.astype(_MXU)
            dws_ref[h] += _dot(dsvb, vnb[:, sl], _NT)
            dvn_parts.append(_dot(wc[h], dsvb, _TN))
        dbs_ref[...] += dbs
        dvn = jnp.concatenate(dvn_parts, axis=1)
        dg_ref[...] += jnp.sum(dvn * vh, axis=0, keepdims=True)
        db_ref[...] += jnp.sum(dvn, axis=0, keepdims=True)
        dxh = dvn * ln_g_v
        dv = rstd * (dxh - jnp.mean(dxh, axis=-1, keepdims=True) - vh * jnp.mean(dxh * vh, axis=-1, keepdims=True))
        dz_ref[:, W:] = (dv * _gelu_grad(zv)).astype(dz_ref.dtype)

        @pl.when(pl.program_id(0) == pl.num_programs(0) - 1)
        def _():
            dws_ref[...] = _causal(dws_ref[...])

    return pl.pallas_call(
        body, name="gmlp_bwd", grid=(L // T,),
        in_specs=[pl.BlockSpec((T, W2), lambda n: (n, 0)), pl.BlockSpec((T, W), lambda n: (n, 0)),
                  pl.BlockSpec((1, W), lambda n: (0, 0)), pl.BlockSpec((1, W), lambda n: (0, 0)),
                  pl.BlockSpec(w_s.shape, lambda n: (0, 0, 0)), pl.BlockSpec(bsT.shape, lambda n: (0, 0))],
        out_specs=[pl.BlockSpec((T, W2), lambda n: (n, 0)),
                   pl.BlockSpec(w_s.shape, lambda n: (0, 0, 0)), pl.BlockSpec((T, _LANES), lambda n: (0, 0)),
                   pl.BlockSpec((1, W), lambda n: (0, 0)), pl.BlockSpec((1, W), lambda n: (0, 0))],
        out_shape=[jax.ShapeDtypeStruct((L, W2), _MXU), jax.ShapeDtypeStruct(w_s.shape, F32),
                   jax.ShapeDtypeStruct((T, _LANES), F32),
                   jax.ShapeDtypeStruct((1, W), F32), jax.ShapeDtypeStruct((1, W), F32)],
        compiler_params=_cparams(("arbitrary",)),
    )(z, dm, ln_g, ln_b, w_s, bsT)


def _s5_prep_math(a_re, a_im, log_dt):
    dt = jnp.exp(log_dt)
    xr, xi = a_re * dt, a_im * dt
    e = jnp.exp(xr)
    lbr, lbi = e * jnp.cos(xi), e * jnp.sin(xi)
    dn = a_re * a_re + a_im * a_im
    nr, ni = lbr - 1.0, lbi
    pr, pi = nr * a_re + ni * a_im, ni * a_re - nr * a_im
    return dt, lbr, lbi, dn, nr, ni, pr, pi


def _vm():
    return pl.BlockSpec(memory_space=pltpu.VMEM)


def _s5_prep(a_re, a_im, log_dt, bT_re, bT_im):
    def body(ar_ref, ai_ref, ld_ref, br_ref, bi_ref, lbr_ref, lbi_ref, Br_ref, Bi_ref):
        _, lbr, lbi, dn, _, _, pr, pi = _s5_prep_math(ar_ref[...], ai_ref[...], ld_ref[...])
        cr, ci = pr / dn, pi / dn
        lbr_ref[...] = lbr
        lbi_ref[...] = lbi
        br, bi = br_ref[...], bi_ref[...]
        Br_ref[...] = cr[None] * br - ci[None] * bi
        Bi_ref[...] = cr[None] * bi + ci[None] * br

    sd = jax.ShapeDtypeStruct
    return pl.pallas_call(
        body, name="s5_prep", in_specs=[_vm()] * 5, out_specs=[_vm()] * 4,
        out_shape=[sd(a_re.shape, F32), sd(a_re.shape, F32), sd(bT_re.shape, F32), sd(bT_re.shape, F32)],
    )(a_re, a_im, log_dt, bT_re, bT_im)


def _s5_prep_bwd(a_re, a_im, log_dt, bT_re, bT_im, dlbr_s, dlbi_s, dBr, dBi):
    def body(ar_ref, ai_ref, ld_ref, br_ref, bi_ref, dlr_ref, dli_ref, dBr_ref, dBi_ref,
             dar_ref, dai_ref, dld_ref, dbr_ref, dbi_ref):
        a_re_v, a_im_v = ar_ref[...], ai_ref[...]
        dt, lbr, lbi, dn, nr, ni, pr, pi = _s5_prep_math(a_re_v, a_im_v, ld_ref[...])
        cr, ci = pr / dn, pi / dn
        br, bi, dBr_v, dBi_v = br_ref[...], bi_ref[...], dBr_ref[...], dBi_ref[...]
        dbr_ref[...] = cr[None] * dBr_v + ci[None] * dBi_v
        dbi_ref[...] = cr[None] * dBi_v - ci[None] * dBr_v
        dcr = jnp.sum(br * dBr_v + bi * dBi_v, axis=0)
        dci = jnp.sum(br * dBi_v - bi * dBr_v, axis=0)
        dpr, dpi = dcr / dn, dci / dn
        ddn = -(dcr * pr + dci * pi) / (dn * dn)
        dnr = dpr * a_re_v - dpi * a_im_v
        dni = dpr * a_im_v + dpi * a_re_v
        dlbr = dlr_ref[...] + dnr
        dlbi = dli_ref[...] + dni
        dxr = dlbr * lbr + dlbi * lbi
        dxi = dlbi * lbr - dlbr * lbi
        dar_ref[...] = dpr * nr + dpi * ni + 2.0 * ddn * a_re_v + dxr * dt
        dai_ref[...] = dpr * ni - dpi * nr + 2.0 * ddn * a_im_v + dxi * dt
        dld_ref[...] = jnp.sum(dxr * a_re_v + dxi * a_im_v, axis=1, keepdims=True) * dt

    sd = jax.ShapeDtypeStruct
    return pl.pallas_call(
        body, name="s5_prep_bwd", in_specs=[_vm()] * 9, out_specs=[_vm()] * 5,
        out_shape=[sd(a_re.shape, F32), sd(a_re.shape, F32), sd(log_dt.shape, F32),
                   sd(bT_re.shape, F32), sd(bT_re.shape, F32)],
    )(a_re, a_im, log_dt, bT_re, bT_im, dlbr_s, dlbi_s, dBr, dBi)


def _shift_rows(v, down):
    n = v.shape[0]
    rolled = pltpu.roll(v, 1 if down else n - 1, 0)
    row = lax.broadcasted_iota(jnp.int32, v.shape, 0)
    return jnp.where(row == (0 if down else n - 1), 0.0, rolled)


def _cmul(ar, ai, br, bi):
    return ar * br - ai * bi, ar * bi + ai * br


_SEG = 8


def _seg_rows(k, nk):
    return pl.ds(k, _SEG, stride=nk)


class _Chunked:
    def __init__(self, ref):
        self.ref, self.n = ref, ref.shape[0]

    def rows(self, rows):
        return jnp.concatenate([self.ref[j, rows, :] for j in range(self.n)], axis=1)

    def set_rows(self, rows, v):
        for j in range(self.n):
            self.ref[j, rows, :] = v[:, j * _LANES:(j + 1) * _LANES]

    def all(self):
        return jnp.concatenate([self.ref[j] for j in range(self.n)], axis=1)

    def set_all(self, v):
        for j in range(self.n):
            self.ref[j] = v[:, j * _LANES:(j + 1) * _LANES]


def _segment_inits(er, ei, ar, ai, nk, down):
    pr, pi = ar, ai
    for _ in range(int(math.log2(nk))):
        pr, pi = _cmul(pr, pi, pr, pi)
    fr, fi = er, ei
    for _ in range(_SEG - 1):
        sr, si = _shift_rows(fr, down), _shift_rows(fi, down)
        mr, mi = _cmul(pr, pi, sr, si)
        fr, fi = er + mr, ei + mi
    return _shift_rows(fr, down), _shift_rows(fi, down)


def _scan_states(x_re, x_im, ar, ai, nk):
    lanes = ar.shape[1]

    def step(k, s):
        rows = _seg_rows(k, nk)
        mr, mi = _cmul(ar, ai, s[0], s[1])
        return mr + x_re.rows(rows), mi + x_im.rows(rows)

    zero = jnp.zeros((_SEG, lanes), F32)
    er, ei = lax.fori_loop(0, nk, step, (zero, zero))
    ir, ii = _segment_inits(er, ei, ar, ai, nk, True)

    def step2(k, s):
        rows = _seg_rows(k, nk)
        mr, mi = _cmul(ar, ai, s[0], s[1])
        nr, ni = mr + x_re.rows(rows), mi + x_im.rows(rows)
        x_re.set_rows(rows, nr)
        x_im.set_rows(rows, ni)
        return nr, ni

    lax.fori_loop(0, nk, step2, (ir, ii))


def _s5_tile_fwd(u, bd_re, bd_im, cd_re, cd_im, ar, ai, d, s_re, s_im, nk):
    s_re.set_all(_dot(u, bd_re, _NN))
    s_im.set_all(_dot(u, bd_im, _NN))
    _scan_states(s_re, s_im, ar, ai, nk)
    return _dot(s_re.all(), cd_re, _NN) - _dot(s_im.all(), cd_im, _NN) + d * u


def _s5_specs(L, T):
    lanes = _S5_GT * _S5_P
    u_spec = pl.BlockSpec((L, _LANES), lambda t: (0, t))
    bd_spec = pl.BlockSpec((None, _LANES, lanes), lambda t: (t, 0, 0))
    cd_spec = pl.BlockSpec((None, lanes, _LANES), lambda t: (t, 0, 0))
    lam_spec = pl.BlockSpec((None, 1, lanes), lambda t: (t, 0, 0))
    d_spec = pl.BlockSpec((1, _LANES), lambda t: (0, t))
    return lanes, u_spec, bd_spec, cd_spec, lam_spec, d_spec


def _s5_fwd(u, bd_re, bd_im, cd_re, cd_im, lam_re, lam_im, d):
    L, Wd = u.shape
    T = Wd // _LANES
    nk = L // _SEG
    lanes, u_spec, bd_spec, cd_spec, lam_spec, d_spec = _s5_specs(L, T)

    def body(u_ref, bdr, bdi, cdr, cdi, lr, li, d_ref, g_ref, s_re, s_im):
        s_re, s_im = _Chunked(s_re), _Chunked(s_im)
        ar = jnp.broadcast_to(lr[...], (_SEG, lanes))
        ai = jnp.broadcast_to(li[...], (_SEG, lanes))
        y = _s5_tile_fwd(u_ref[...], bdr[...], bdi[...], cdr[...], cdi[...], ar, ai, d_ref[...], s_re, s_im, nk)
        g_ref[...] = _gelu(y).astype(g_ref.dtype)

    return pl.pallas_call(
        body, name="s5_fwd", grid=(T,),
        in_specs=[u_spec, bd_spec, bd_spec, cd_spec, cd_spec, lam_spec, lam_spec, d_spec],
        out_specs=u_spec, out_shape=jax.ShapeDtypeStruct((L, Wd), _MXU),
        scratch_shapes=[pltpu.VMEM((lanes // _LANES, L, _LANES), F32) for _ in range(2)],
        compiler_params=_cparams(("arbitrary",)),
    )(u, bd_re, bd_im, cd_re, cd_im, lam_re, lam_im, d)


def _s5_bwd(u, dg, bd_re, bd_im, cd_re, cd_im, lam_re, lam_im, d):
    L, Wd = u.shape
    T = Wd // _LANES
    nk = L // _SEG
    lanes, u_spec, bd_spec, cd_spec, lam_spec, d_spec = _s5_specs(L, T)

    def body(u_ref, dg_ref, bdr, bdi, cdr, cdi, lr, li, d_ref,
             du_ref, dbdr, dbdi, dcdr, dcdi, dlr, dli, dd_ref, s_re, s_im, g_re, g_im):
        s_re, s_im, g_re, g_im = _Chunked(s_re), _Chunked(s_im), _Chunked(g_re), _Chunked(g_im)
        ar = jnp.broadcast_to(lr[...], (_SEG, lanes))
        ai = jnp.broadcast_to(li[...], (_SEG, lanes))
        uv, dv = u_ref[...], d_ref[...]
        y = _s5_tile_fwd(uv, bdr[...], bdi[...], cdr[...], cdi[...], ar, ai, dv, s_re, s_im, nk)
        dy = dg_ref[...] * _gelu_grad(y)
        dd_ref[...] = jnp.sum(dy * uv, axis=0, keepdims=True)
        dyb = dy.astype(_MXU)
        dcdr[...] = _dot(dyb, s_re.all(), _TN)
        dcdi[...] = -_dot(dyb, s_im.all(), _TN)
        g_re.set_all(_dot(dyb, cdr[...], _NT))
        g_im.set_all(-_dot(dyb, cdi[...], _NT))

        nai = -ai

        def step(j, s):
            rows = _seg_rows(nk - 1 - j, nk)
            mr, mi = _cmul(ar, nai, s[0], s[1])
            return mr + g_re.rows(rows), mi + g_im.rows(rows)

        zero = jnp.zeros((_SEG, lanes), F32)
        er, ei = lax.fori_loop(0, nk, step, (zero, zero))
        ir, ii = _segment_inits(er, ei, ar, nai, nk, False)

        def acc_lam(gr, gi, pr, pi, acc):
            return acc[0] + gr * pr + gi * pi, acc[1] + gi * pr - gr * pi

        def step2(j, carry):
            s, acc = carry
            k = nk - 1 - j
            rows = _seg_rows(k, nk)
            mr, mi = _cmul(ar, nai, s[0], s[1])
            nr, ni = mr + g_re.rows(rows), mi + g_im.rows(rows)
            g_re.set_rows(rows, nr)
            g_im.set_rows(rows, ni)
            prev = _seg_rows(k - 1, nk)
            return (nr, ni), acc_lam(nr, ni, s_re.rows(prev), s_im.rows(prev), acc)

        (g0r, g0i), acc = lax.fori_loop(0, nk - 1, step2, ((ir, ii), (zero, zero)))
        first = _seg_rows(0, nk)
        mr, mi = _cmul(ar, nai, g0r, g0i)
        nr, ni = mr + g_re.rows(first), mi + g_im.rows(first)
        g_re.set_rows(first, nr)
        g_im.set_rows(first, ni)
        last = _seg_rows(nk - 1, nk)
        acc = acc_lam(nr, ni, _shift_rows(s_re.rows(last), True), _shift_rows(s_im.rows(last), True), acc)
        dlr[...] = jnp.sum(acc[0], axis=0, keepdims=True)
        dli[...] = jnp.sum(acc[1], axis=0, keepdims=True)

        gtr, gti = g_re.all().astype(_MXU), g_im.all().astype(_MXU)
        du = _dot(gtr, bdr[...], _NT) + _dot(gti, bdi[...], _NT) + dy * dv
        du_ref[...] = du.astype(du_ref.dtype)
        ub = uv.astype(_MXU)
        dbdr[...] = _dot(ub, gtr, _TN)
        dbdi[...] = _dot(ub, gti, _TN)

    sd = jax.ShapeDtypeStruct
    big = sd((T, _LANES, lanes), F32)
    return pl.pallas_call(
        body, name="s5_bwd", grid=(T,),
        in_specs=[u_spec, u_spec, bd_spec, bd_spec, cd_spec, cd_spec, lam_spec, lam_spec, d_spec],
        out_specs=[u_spec, bd_spec, bd_spec, bd_spec, bd_spec, lam_spec, lam_spec, d_spec],
        out_shape=[sd((L, Wd), _MXU), big, big, big, big, sd((T, 1, lanes), F32), sd((T, 1, lanes), F32),
                   sd((1, Wd), F32)],
        scratch_shapes=[pltpu.VMEM((lanes // _LANES, L, _LANES), F32) for _ in range(4)],
        compiler_params=_cparams(("arbitrary",)),
    )(u, dg, bd_re, bd_im, cd_re, cd_im, lam_re, lam_im, d)


def _half_tile(R):
    for t in (256, 352, 128):
        if R % t == 0:
            return t
    raise ValueError(R)


def _nat_spec(kind, R, C, tr, layer_axis):
    nr = R // tr
    if kind == "col":
        fn = lambda l, h, i: (h * nr + i, 0)
    else:
        fn = lambda l, h, i: (i, h)
    if layer_axis:
        return pl.BlockSpec((None, tr, C), lambda l, h, i: (l,) + fn(l, h, i))
    return pl.BlockSpec((tr, C), fn)


def _cast_shard(name, w, layer, kind, R, C):
    tr = _half_tile(R)

    def body(w_ref, o_ref):
        o_ref[...] = w_ref[...].astype(o_ref.dtype)

    nat = _nat_spec(kind, R, C, tr, True)
    in_spec = pl.BlockSpec(nat.block_shape, lambda l, h, i: nat.index_map(layer, h, i))
    return pl.pallas_call(
        body, name=name, grid=(1, 2, R // tr), in_specs=[in_spec],
        out_specs=pl.BlockSpec((None, tr, C), lambda l, h, i: (h, i, 0)),
        out_shape=jax.ShapeDtypeStruct((2, R, C), _WIRE),
        compiler_params=_cparams(("arbitrary", "arbitrary", "arbitrary")),
    )(w)


def _adam_math(w, g, m, v):
    m2 = _B1 * m + (1.0 - _B1) * g
    v2 = _B2 * v + (1.0 - _B2) * (g * g)
    m_hat = m2 / (1.0 - _B1 ** _STEP)
    v_hat = v2 / (1.0 - _B2 ** _STEP)
    delta = -_LR * (m_hat / (jnp.sqrt(v_hat) + _AEPS) + _WD * w)
    return delta, m2, v2


def _adamw_big(name, w, m, v, grads, kind, R, C):
    nl = w.shape[0]
    tr = _half_tile(R)
    nr = R // tr

    def body(*refs):
        w_ref, m_ref, v_ref = refs[:3]
        g_refs = refs[3:3 + nl]
        go_ref, d_ref, mo_ref, vo_ref = refs[3 + nl:]
        g = g_refs[0][...]
        for l in range(1, nl):
            g = jnp.where(pl.program_id(0) == l, g_refs[l][...], g)
        delta, m2, v2 = _adam_math(w_ref[...], g, m_ref[...], v_ref[...])
        go_ref[...] = g
        d_ref[...] = delta
        mo_ref[...] = m2
        vo_ref[...] = v2

    nat = _nat_spec(kind, R, C, tr, True)

    def gspec(l0):
        def imap(l, h, i):
            on = l == l0
            return jnp.where(on, h, 1 if l0 < nl - 1 else 0), jnp.where(on, i, nr - 1 if l0 < nl - 1 else 0), 0
        if nl == 1:
            return pl.BlockSpec((None, tr, C), lambda l, h, i: (h, i, 0))
        return pl.BlockSpec((None, tr, C), imap)

    sd = jax.ShapeDtypeStruct(w.shape, F32)
    return pl.pallas_call(
        body, name=name, grid=(nl, 2, nr),
        in_specs=[nat, nat, nat] + [gspec(l) for l in range(nl)],
        out_specs=[nat] * 4, out_shape=[sd] * 4,
        compiler_params=_cparams(("arbitrary", "arbitrary", "arbitrary")),
    )(w, m, v, *grads)


def _adamw_flat(w, g, m, v):
    rows = w.shape[0]
    tr = rows // 8 if rows % 64 == 0 else rows

    def body(w_ref, g_ref, m_ref, v_ref, d_ref, mo_ref, vo_ref):
        delta, m2, v2 = _adam_math(w_ref[...], g_ref[...], m_ref[...], v_ref[...])
        d_ref[...] = delta
        mo_ref[...] = m2
        vo_ref[...] = v2

    spec = pl.BlockSpec((tr, _LANES), lambda i: (i, 0))
    sd = jax.ShapeDtypeStruct(w.shape, F32)
    return pl.pallas_call(
        body, name="adamw_small", grid=(rows // tr,), in_specs=[spec] * 4, out_specs=[spec] * 3,
        out_shape=[sd] * 3, compiler_params=_cparams(("arbitrary",)),
    )(w, g, m, v)


def _place():
    x, y, c = lax.axis_index("x"), lax.axis_index("y"), lax.axis_index("c")
    chips = [(1 - x, y), (x, 1 - y), (1 - x, 1 - y)]
    return x, y, c, 2 * x + y, chips


def _any():
    return pl.BlockSpec(memory_space=pl.ANY)


def _remote(src, dst, ssem, rsem, dev):
    return pltpu.make_async_remote_copy(src_ref=src, dst_ref=dst, send_sem=ssem, recv_sem=rsem,
                                        device_id=dev, device_id_type=_MESH)


def _allgather(name, shards):
    n = len(shards)

    def body(*refs):
        s_refs, g_refs = refs[:n], refs[n:2 * n]
        send1, recv1, send2, recv2, lsem = refs[2 * n:]
        x, y, c, q, chips = _place()
        sib = (x, y, 1 - c)
        local = [pltpu.make_async_copy(s_refs[a], g_refs[a].at[q], lsem.at[a]) for a in range(n)]
        for cp in local:
            cp.start()
        first, passed = [], []
        for a in range(n):
            for j, (rx, ry) in enumerate(chips):
                k = 3 * a + j
                first.append(_remote(s_refs[a].at[c], g_refs[a].at[q, c], send1.at[k], recv1.at[k], (rx, ry, c)))
                land = g_refs[a].at[2 * rx + ry, c]
                passed.append(_remote(land, land, send2.at[k], recv2.at[k], sib))
        for cp in first:
            cp.start()
        for a in range(n):
            for j, (rx, ry) in enumerate(chips):
                k = 3 * a + j
                land = g_refs[a].at[2 * rx + ry, c]
                _remote(land, land, send1.at[k], recv1.at[k], (rx, ry, c)).wait_recv()
                passed[k].start()
        for a in range(n):
            for j, (rx, ry) in enumerate(chips):
                k = 3 * a + j
                other = g_refs[a].at[2 * rx + ry, 1 - c]
                _remote(other, other, send2.at[k], recv2.at[k], sib).wait_recv()
        for cp in first + passed:
            cp.wait_send()
        for cp in local:
            cp.wait()

    sems = [pltpu.SemaphoreType.DMA((3 * n,)) for _ in range(4)] + [pltpu.SemaphoreType.DMA((n,))]
    return pl.pallas_call(
        body, name=name, in_specs=[_any()] * n, out_specs=[_any()] * n,
        out_shape=[jax.ShapeDtypeStruct((4,) + s.shape, s.dtype) for s in shards],
        scratch_shapes=sems,
    )(*shards)


def _swap_halves(name, grads):
    n = len(grads)

    def body(*refs):
        g_refs, t_refs = refs[:n], refs[n:2 * n]
        send, recv = refs[2 * n:]
        x, y, c, _, _ = _place()
        cps = [_remote(g_refs[a].at[1 - c], t_refs[a], send.at[a], recv.at[a], (x, y, 1 - c)) for a in range(n)]
        for cp in cps:
            cp.start()
        for cp in cps:
            cp.wait()

    return pl.pallas_call(
        body, name=name, in_specs=[_any()] * n, out_specs=[_any()] * n,
        out_shape=[jax.ShapeDtypeStruct(g.shape[1:], g.dtype) for g in grads],
        scratch_shapes=[pltpu.SemaphoreType.DMA((n,)), pltpu.SemaphoreType.DMA((n,))],
    )(*grads)


def _chip_sum(name, g, t):
    _, _, R, C = g.shape
    tr = _half_tile(R)

    def body(c_ref, g_ref, t_ref, o_ref):
        o_ref[...] = (g_ref[...].astype(F32) + t_ref[...].astype(F32)).astype(o_ref.dtype)

    c = lax.axis_index("c").astype(jnp.int32).reshape(1)
    gs = pltpu.PrefetchScalarGridSpec(
        num_scalar_prefetch=1, grid=(4, R // tr),
        in_specs=[pl.BlockSpec((None, None, tr, C), lambda r, i, c_ref: (c_ref[0], r, i, 0)),
                  pl.BlockSpec((None, tr, C), lambda r, i, c_ref: (r, i, 0))],
        out_specs=pl.BlockSpec((None, tr, C), lambda r, i, c_ref: (r, i, 0)))
    return pl.pallas_call(
        body, name=name, grid_spec=gs, out_shape=jax.ShapeDtypeStruct((4, R, C), _WIRE),
        compiler_params=_cparams(("arbitrary", "arbitrary")),
    )(c, g, t)


def _scatter_parts(name, parts):
    n = len(parts)

    def body(*refs):
        p_refs, t_refs = refs[:n], refs[n:2 * n]
        send, recv, lsem = refs[2 * n:]
        x, y, c, q, chips = _place()
        local = [pltpu.make_async_copy(p_refs[a].at[q], t_refs[a].at[q], lsem.at[a]) for a in range(n)]
        for cp in local:
            cp.start()
        cps = []
        for a in range(n):
            for j, (rx, ry) in enumerate(chips):
                k = 3 * a + j
                cps.append(_remote(p_refs[a].at[2 * rx + ry], t_refs[a].at[q], send.at[k], recv.at[k], (rx, ry, c)))
        for cp in cps:
            cp.start()
        for a in range(n):
            for j, (rx, ry) in enumerate(chips):
                k = 3 * a + j
                land = t_refs[a].at[2 * rx + ry]
                _remote(land, land, send.at[k], recv.at[k], (rx, ry, c)).wait_recv()
        for cp in cps:
            cp.wait_send()
        for cp in local:
            cp.wait()

    return pl.pallas_call(
        body, name=name, in_specs=[_any()] * n, out_specs=[_any()] * n,
        out_shape=[jax.ShapeDtypeStruct(p.shape, p.dtype) for p in parts],
        scratch_shapes=[pltpu.SemaphoreType.DMA((3 * n,)), pltpu.SemaphoreType.DMA((3 * n,)),
                        pltpu.SemaphoreType.DMA((n,))],
    )(*parts)


def _sum_parts(name, t):
    _, R, C = t.shape
    tr = _half_tile(R)

    def body(t_ref, o_ref):
        acc = t_ref[0].astype(F32)
        for r in range(1, 4):
            acc = acc + t_ref[r].astype(F32)
        o_ref[...] = acc

    return pl.pallas_call(
        body, name=name, grid=(R // tr,),
        in_specs=[pl.BlockSpec((4, tr, C), lambda i: (0, i, 0))],
        out_specs=pl.BlockSpec((tr, C), lambda i: (i, 0)),
        out_shape=jax.ShapeDtypeStruct((R, C), F32),
        compiler_params=_cparams(("arbitrary",)),
    )(t)


def _join_halves(name, halves):
    n = len(halves)

    def body(*refs):
        h_refs, o_refs = refs[:n], refs[n:2 * n]
        send, recv, lsem = refs[2 * n:]
        x, y, c, _, _ = _place()
        local = [pltpu.make_async_copy(h_refs[a], o_refs[a].at[c], lsem.at[a]) for a in range(n)]
        cps = [_remote(h_refs[a], o_refs[a].at[c], send.at[a], recv.at[a], (x, y, 1 - c)) for a in range(n)]
        for cp in local + cps:
            cp.start()
        for a in range(n):
            other = o_refs[a].at[1 - c]
            _remote(other, other, send.at[a], recv.at[a], (x, y, 1 - c)).wait_recv()
        for cp in cps:
            cp.wait_send()
        for cp in local:
            cp.wait()

    return pl.pallas_call(
        body, name=name, in_specs=[_any()] * n, out_specs=[_any()] * n,
        out_shape=[jax.ShapeDtypeStruct((2,) + h.shape, h.dtype) for h in halves],
        scratch_shapes=[pltpu.SemaphoreType.DMA((n,)), pltpu.SemaphoreType.DMA((n,)), pltpu.SemaphoreType.DMA((n,))],
    )(*halves)


def _reduce_scatter(tag, grads):
    swapped = _swap_halves(f"rs_swap_{tag}", grads)
    parts = [_chip_sum(f"rs_chipsum_{tag}_{a}", g, t) for a, (g, t) in enumerate(zip(grads, swapped))]
    landed = _scatter_parts(f"rs_scatter_{tag}", parts)
    halves = [_sum_parts(f"rs_sum_{tag}_{a}", t) for a, t in enumerate(landed)]
    return _join_halves(f"rs_join_{tag}", halves)


def _allreduce_small(name, v):
    _, R, _ = v.shape

    def body(v_ref, o_ref, land, acc, send1, recv1, send2, recv2):
        x, y, c = lax.axis_index("x"), lax.axis_index("y"), lax.axis_index("c")
        me = 4 * x + 2 * y + c
        peers = []
        for k in range(1, 8):
            dx, dy, dc = (k >> 2) & 1, (k >> 1) & 1, k & 1
            px, py, pc = (1 - x if dx else x), (1 - y if dy else y), (1 - c if dc else c)
            peers.append((k, (px, py, pc), 4 * px + 2 * py + pc))
        land[me] = v_ref[me]
        out1 = [_remote(v_ref.at[pid], land.at[me], send1.at[k], recv1.at[k], dev) for k, dev, pid in peers]
        for cp in out1:
            cp.start()
        for k, dev, pid in peers:
            _remote(land.at[pid], land.at[pid], send1.at[k], recv1.at[k], dev).wait_recv()
        total = land[0]
        for j in range(1, 8):
            total = total + land[j]
        acc[...] = total
        o_ref[me] = total
        out2 = [_remote(acc, o_ref.at[me], send2.at[k], recv2.at[k], dev) for k, dev, pid in peers]
        for cp in out2:
            cp.start()
        for k, dev, pid in peers:
            _remote(o_ref.at[pid], o_ref.at[pid], send2.at[k], recv2.at[k], dev).wait_recv()
        for cp in out1 + out2:
            cp.wait_send()

    return pl.pallas_call(
        body, name=name, in_specs=[_vm()], out_specs=_vm(),
        out_shape=jax.ShapeDtypeStruct(v.shape, F32),
        scratch_shapes=[pltpu.VMEM(v.shape, F32), pltpu.VMEM((R, _LANES), F32)]
        + [pltpu.SemaphoreType.DMA((8,)) for _ in range(4)],
        compiler_params=pltpu.CompilerParams(vmem_limit_bytes=_VMEM_LIMIT),
    )(v)


_WEIGHT_NAMES = ['norm_mix', 'norm_ffn', 'norm_ple', 'norm_final', 'gm_w_in', 'gm_ln_g', 'gm_ln_b', 'gm_w_s',
                 'gm_b_s', 'gm_w_out', 's5_w_in', 's5_a_re', 's5_a_im', 's5_log_dt', 's5_b_re', 's5_b_im',
                 's5_c_re', 's5_c_im', 's5_d', 's5_w_out', 'ffn_w1', 'ffn_w3', 'ffn_w2', 'ple_w_gate', 'ple_w_proj']
_BIG = {'gm_w_in': 'col', 'gm_w_out': 'row', 's5_w_in': 'row', 's5_w_out': 'col', 'ffn_w1': 'col',
        'ffn_w3': 'col', 'ffn_w2': 'row', 'ple_w_gate': 'row', 'ple_w_proj': 'col'}


def _rc(kind, shard_shape):
    rows, cols = shard_shape[-2:]
    return (rows // 2, cols) if kind == "col" else (rows, cols // 2)


def _pack(vecs, rows_multiple):
    flat = jnp.concatenate([a.reshape(-1).astype(F32) for a in vecs])
    unit = rows_multiple * _LANES
    pad = (-flat.shape[0]) % unit
    return jnp.pad(flat, (0, pad)).reshape(-1, _LANES)


def _unpack(buf, shapes):
    flat = buf.reshape(-1)
    out, off = [], 0
    for s in shapes:
        n = math.prod(s)
        out.append(flat[off:off + n].reshape(s))
        off += n
    return out


def _ident(accs, ex):
    return accs


def _add_resid(accs, ex):
    return [accs[0] + ex[0]]


def _swiglu_epi(accs, ex):
    a, b = accs
    return [a, b, a * _sig(a) * b]


def _swiglu_bwd_epi(accs, ex):
    df = accs[0]
    a, b = ex[0].astype(F32), ex[1].astype(F32)
    sa = _sig(a)
    return [df * b * (sa * (1.0 + a * (1.0 - sa))), df * (a * sa)]


def _ple_epi(accs, ex):
    gt = _sig(accs[0])
    return [ex[0] + gt * ex[1], gt]


def _glu_epi(accs, ex):
    val, sg = accs[0], _sig(accs[1])
    return [ex[0] + val * sg, val, sg]


def _block_diag(t):
    eye = jnp.eye(_S5_GT, dtype=t.dtype)
    return t[:, :, :, None, :] * eye[None, :, None, :, None]


def _diag_blocks(t, T, a, b):
    d = jnp.diagonal(t.reshape(T, _S5_GT, a, _S5_GT, b), axis1=1, axis2=3)
    return jnp.transpose(d, (0, 3, 1, 2))


def kernel(x, p, norm_mix, norm_ffn, norm_ple, norm_final, gm_w_in, gm_ln_g, gm_ln_b, gm_w_s, gm_b_s, gm_w_out, s5_w_in, s5_a_re, s5_a_im, s5_log_dt, s5_b_re, s5_b_im, s5_c_re, s5_c_im, s5_d, s5_w_out, ffn_w1, ffn_w3, ffn_w2, ple_w_gate, ple_w_proj, loss_target, m_norm_mix, m_norm_ffn, m_norm_ple, m_norm_final, m_gm_w_in, m_gm_ln_g, m_gm_ln_b, m_gm_w_s, m_gm_b_s, m_gm_w_out, m_s5_w_in, m_s5_a_re, m_s5_a_im, m_s5_log_dt, m_s5_b_re, m_s5_b_im, m_s5_c_re, m_s5_c_im, m_s5_d, m_s5_w_out, m_ffn_w1, m_ffn_w3, m_ffn_w2, m_ple_w_gate, m_ple_w_proj, v_norm_mix, v_norm_ffn, v_norm_ple, v_norm_final, v_gm_w_in, v_gm_ln_g, v_gm_ln_b, v_gm_w_s, v_gm_b_s, v_gm_w_out, v_s5_w_in, v_s5_a_re, v_s5_a_im, v_s5_log_dt, v_s5_b_re, v_s5_b_im, v_s5_c_re, v_s5_c_im, v_s5_d, v_s5_w_out, v_ffn_w1, v_ffn_w3, v_ffn_w2, v_ple_w_gate, v_ple_w_proj):
    env = dict(locals())
    w = {n: env[n] for n in _WEIGHT_NAMES}
    mom = {n: env["m_" + n] for n in _WEIGHT_NAMES}
    var = {n: env["v_" + n] for n in _WEIGHT_NAMES}
    xs, tgt = x[0], loss_target[0]
    L, D = xs.shape
    depth = norm_mix.shape[0]
    qx, qy = lax.axis_index("x"), lax.axis_index("y")
    q = 2 * qx + qy

    def gather(tag, items):
        shards = []
        for name, layer in items:
            kind = _BIG[name]
            R, C = _rc(kind, w[name].shape)
            shards.append(_cast_shard(f"cast_{name}{layer}", w[name], layer, kind, R, C))
        full = _allgather(f"ag_{tag}", shards)
        return {it: _W(f, _BIG[it[0]]) for it, f in zip(items, full)}

    W = {}
    W.update(gather("mix0", [("gm_w_in", 0), ("gm_w_out", 0)]))
    for i in range(depth):
        W.update(gather(f"ffn{i}", [("ffn_w1", i), ("ffn_w3", i), ("ffn_w2", i)]))
        W.update(gather(f"ple{i}", [("ple_w_gate", i), ("ple_w_proj", i)]))
    W.update(gather("mix1", [("s5_w_in", 0), ("s5_w_out", 0)]))

    d_slots = jnp.zeros((4, D // 4), F32)
    d_slots = lax.dynamic_update_slice(d_slots, s5_d.astype(F32), (q, 0))
    d_sum = _allreduce_small("ar_s5_d", _pack([d_slots], 64).reshape(8, -1, _LANES))
    d_full = (d_sum.reshape(-1)[:D] * 0.5).reshape(1, D)

    def ffn_fwd(i, xin):
        hf = _rms_fwd(f"rms_ffn{i}", xin, norm_ffn[i:i + 1])
        a, b, f = _mm_nn(f"ffn_up{i}", hf, [W["ffn_w1", i], W["ffn_w3", i]], 512, 1408, ffn_w2.shape[1] * 4,
                         [], [_MXU, _MXU, _MXU], _swiglu_epi)
        xo = _mm_nn(f"ffn_down{i}", f, [W["ffn_w2", i]], 1408, 1024, D, [xin], [F32], _add_resid)[0]
        return xo, (xin, hf, a, b, f)

    def ple_fwd(i, xin):
        hp = _rms_fwd(f"rms_ple{i}", xin, norm_ple[i:i + 1])
        pi = p[i, 0]
        pp = _mm_nn(f"ple_proj{i}", pi, [W["ple_w_proj", i]], 128, 512, D, [], [F32], _ident)[0]
        xo, gt = _mm_nn(f"ple_gate{i}", hp, [W["ple_w_gate", i]], 512, 1024, D, [xin, pp], [F32, _MXU], _ple_epi)
        return xo, (xin, hp, pi, pp, gt)

    h0 = _rms_fwd("rms_mix0", xs, norm_mix[0:1])
    z = _mm_nn("gm_in", h0, [W["gm_w_in", 0]], 512, 1024, 2 * D, [], [F32], _ident)[0]
    bsT = gm_b_s[0].T
    gm_m = _gmlp_fwd(z, gm_ln_g, gm_ln_b, gm_w_s[0], bsT)
    x1 = _mm_nn("gm_out", gm_m, [W["gm_w_out", 0]], 512, 1024, D, [xs], [F32], _add_resid)[0]
    x2, ffn0 = ffn_fwd(0, x1)
    x3, ple0 = ple_fwd(0, x2)

    T = D // _LANES
    lanes = _S5_GT * _S5_P
    a_re, a_im, log_dt = s5_a_re[0], s5_a_im[0], s5_log_dt[0][:, None]
    bT_re, bT_im = jnp.transpose(s5_b_re[0], (2, 0, 1)), jnp.transpose(s5_b_im[0], (2, 0, 1))
    lbr, lbi, BrT, BiT = _s5_prep(a_re, a_im, log_dt, bT_re, bT_im)

    def to_bd(BT):
        t = jnp.transpose(BT.reshape(_S5_C, T, _S5_GT, _S5_P), (1, 2, 0, 3))
        return _block_diag(t).reshape(T, _LANES, lanes).astype(_MXU)

    def to_cd(cw):
        t = jnp.transpose(cw.reshape(T, _S5_GT, _S5_C, _S5_P), (0, 1, 3, 2))
        return _block_diag(t).reshape(T, lanes, _LANES).astype(_MXU)

    bd_re, bd_im = to_bd(BrT), to_bd(BiT)
    cd_re, cd_im = to_cd(s5_c_re[0].astype(F32)), to_cd(s5_c_im[0].astype(F32))
    lam_re, lam_im = lbr.reshape(T, 1, lanes), lbi.reshape(T, 1, lanes)

    h1 = _rms_fwd("rms_mix1", x3, norm_mix[1:2])
    u = _mm_nn("s5_in", h1, [W["s5_w_in", 0]], 512, 1024, D, [], [F32], _ident)[0]
    s5_g = _s5_fwd(u, bd_re, bd_im, cd_re, cd_im, lam_re, lam_im, d_full)
    x4, glu_val, glu_sg = _mm_nn("s5_out", s5_g, [W["s5_w_out", 0], W["s5_w_out", 0]], 512, 1024, D, [x3],
                                 [F32, _MXU, _MXU], _glu_epi, cb_offsets=[0, 2])
    x5, ffn1 = ffn_fwd(1, x4)
    x6, ple1 = ple_fwd(1, x5)

    dx, d_norm_final, loss_rows = _loss_head(x6, norm_final[None], tgt)
    loss = lax.psum(loss_rows[0, 0], ("x", "y", "c"))

    G = {}
    small = {}

    def ple_bwd(i, dxo, saved):
        xin, hp, pi, pp, gt = saved
        dpre, dpp = _ple_bwd_elem(dxo, pp, gt)
        dwg = _mm_tn(f"ple_gate_dw{i}", hp, [dpre], "row", 512, 1024, 512, 1024)[0]
        dwp = _mm_tn(f"ple_proj_dw{i}", pi, [dpp], "col", 128, 512, 128, 512)[0]
        dhp = _mm_nt(f"ple_gate_dx{i}", [dpre], [W["ple_w_gate", i]], 512, 1024, [], [F32], _ident)[0]
        dxin, dg = _rms_bwd(f"rms_ple_bwd{i}", dhp, xin, norm_ple[i:i + 1], dxo)
        return dxin, dg, [dwg, dwp]

    def ffn_bwd(i, dxo, saved):
        xin, hf, a, b, f = saved
        da, db = _mm_nt(f"ffn_down_dx{i}", [dxo], [W["ffn_w2", i]], 1408, 1024, [a, b], [_MXU, _MXU],
                        _swiglu_bwd_epi)
        dw2 = _mm_tn(f"ffn_down_dw{i}", f, [dxo], "row", 1408, 1024, 1408, 1024)[0]
        dw1, dw3 = _mm_tn(f"ffn_up_dw{i}", hf, [da, db], "col", 1024, 1408, 512, 1408)
        dhf = _mm_nt(f"ffn_up_dx{i}", [da, db], [W["ffn_w1", i], W["ffn_w3", i]], 1024, 1408, [], [F32], _ident)[0]
        dxin, dg = _rms_bwd(f"rms_ffn_bwd{i}", dhf, xin, norm_ffn[i:i + 1], dxo)
        return dxin, dg, [dw1, dw3, dw2]

    d_norm_ple, d_norm_ffn, d_norm_mix = [None] * depth, [None] * depth, [None] * depth
    big_layers = {n: [None] * w[n].shape[0] for n in _BIG}

    def reduce_into(tag, names, layer, grads):
        red = _reduce_scatter(tag, grads)
        for n, r in zip(names, red):
            big_layers[n][layer] = r

    dx, d_norm_ple[1], gs = ple_bwd(1, dx, ple1)
    reduce_into("ple1", ["ple_w_gate", "ple_w_proj"], 1, gs)
    dx, d_norm_ffn[1], gs = ffn_bwd(1, dx, ffn1)
    reduce_into("ffn1", ["ffn_w1", "ffn_w3", "ffn_w2"], 1, gs)

    do = _glu_bwd_elem(dx, glu_val, glu_sg)
    dw_s5out = _mm_tn("s5_out_dw", s5_g, [do], "col", 1024, 1024, 512, 1024)[0]
    dgy = _mm_nt("s5_out_dx", [do], [W["s5_w_out", 0]], 1024, 1024, [], [F32], _ident)[0]
    du, dbd_re, dbd_im, dcd_re, dcd_im, dl_re, dl_im, dd = _s5_bwd(u, dgy, bd_re, bd_im, cd_re, cd_im,
                                                                  lam_re, lam_im, d_full)
    dw_s5in = _mm_tn("s5_in_dw", h1, [du], "row", 512, 1024, 512, 1024)[0]
    dh1 = _mm_nt("s5_in_dx", [du], [W["s5_w_in", 0]], 512, 1024, [], [F32], _ident)[0]
    dx, d_norm_mix[1] = _rms_bwd("rms_mix1_bwd", dh1, x3, norm_mix[1:2], dx)
    reduce_into("mix1", ["s5_w_in", "s5_w_out"], 0, [dw_s5in, dw_s5out])

    def from_bd(t):
        return jnp.transpose(_diag_blocks(t, T, _S5_C, _S5_P), (2, 0, 1, 3)).reshape(_S5_C, T * _S5_GT, _S5_P)

    def from_cdT(t):
        return _diag_blocks(t, T, _S5_C, _S5_P).reshape(T * _S5_GT, _S5_C, _S5_P)

    da_re, da_im, dlog_dt, dbT_re, dbT_im = _s5_prep_bwd(
        a_re, a_im, log_dt, bT_re, bT_im, dl_re.reshape(a_re.shape), dl_im.reshape(a_re.shape),
        from_bd(dbd_re), from_bd(dbd_im))
    small["s5_a_re"], small["s5_a_im"], small["s5_log_dt"] = da_re[None], da_im[None], dlog_dt.reshape(1, -1)
    small["s5_b_re"] = jnp.transpose(dbT_re, (1, 2, 0))[None]
    small["s5_b_im"] = jnp.transpose(dbT_im, (1, 2, 0))[None]
    small["s5_c_re"], small["s5_c_im"] = from_cdT(dcd_re)[None], from_cdT(dcd_im)[None]

    dx, d_norm_ple[0], gs = ple_bwd(0, dx, ple0)
    reduce_into("ple0", ["ple_w_gate", "ple_w_proj"], 0, gs)
    dx, d_norm_ffn[0], gs = ffn_bwd(0, dx, ffn0)
    reduce_into("ffn0", ["ffn_w1", "ffn_w3", "ffn_w2"], 0, gs)

    dw_gmout = _mm_tn("gm_out_dw", gm_m, [dx], "row", 512, 1024, 512, 1024)[0]
    dgm = _mm_nt("gm_out_dx", [dx], [W["gm_w_out", 0]], 512, 1024, [], [F32], _ident)[0]
    dz, dws, dbsT, dlng, dlnb = _gmlp_bwd(z, dgm, gm_ln_g, gm_ln_b, gm_w_s[0], bsT)
    dw_gmin = _mm_tn("gm_in_dw", h0, [dz], "col", 1024, 1024, 512, 1024)[0]
    dh0 = _mm_nt("gm_in_dx", [dz], [W["gm_w_in", 0]], 1024, 1024, [], [F32], _ident)[0]
    dx, d_norm_mix[0] = _rms_bwd("rms_mix0_bwd", dh0, xs, norm_mix[0:1], dx)
    reduce_into("mix0", ["gm_w_in", "gm_w_out"], 0, [dw_gmin, dw_gmout])
    grad_x = dx[None]

    small["norm_mix"], small["norm_ffn"] = jnp.concatenate(d_norm_mix), jnp.concatenate(d_norm_ffn)
    small["norm_ple"], small["norm_final"] = jnp.concatenate(d_norm_ple), d_norm_final[0]
    small["gm_ln_g"], small["gm_ln_b"], small["gm_w_s"] = dlng, dlnb, dws[None]
    small["gm_b_s"] = dbsT[:, :_GM_HEADS].T[None]
    small["s5_d"] = dd

    small_names = [n for n in _WEIGHT_NAMES if n not in _BIG]
    packed = _pack([small[n] for n in small_names], 64)
    summed = _allreduce_small("ar_small", packed.reshape(8, -1, _LANES))
    red_small = dict(zip(small_names, _unpack(summed, [small[n].shape for n in small_names])))
    red_small["s5_d"] = lax.dynamic_slice(red_small["s5_d"], (0, q * (D // 4)), (1, D // 4))

    grads, deltas, new_m, new_v = {}, {}, {}, {}
    for n, kind in _BIG.items():
        R, C = _rc(kind, w[n].shape)
        grads[n], deltas[n], new_m[n], new_v[n] = _adamw_big(f"adamw_{n}", w[n], mom[n], var[n], big_layers[n],
                                                             kind, R, C)
    shapes = [w[n].shape for n in small_names]
    dl, mo, vo = _adamw_flat(_pack([w[n] for n in small_names], 64), _pack([red_small[n] for n in small_names], 64),
                             _pack([mom[n] for n in small_names], 64), _pack([var[n] for n in small_names], 64))
    for n, g_, d_, m_, v_ in zip(small_names, [red_small[n] for n in small_names], _unpack(dl, shapes),
                                 _unpack(mo, shapes), _unpack(vo, shapes)):
        grads[n], deltas[n], new_m[n], new_v[n] = g_.reshape(w[n].shape), d_, m_, v_

    return (loss, grad_x, *[grads[n] for n in _WEIGHT_NAMES], *[deltas[n] for n in _WEIGHT_NAMES],
            *[new_m[n] for n in _WEIGHT_NAMES], *[new_v[n] for n in _WEIGHT_NAMES])
```

```python
import functools
import math

import jax
import jax.numpy as jnp
from jax import lax
from jax.experimental import pallas as pl
from jax.experimental.pallas import tpu as pltpu

F32 = jnp.float32
_MXU = jnp.bfloat16
_WIRE = jnp.bfloat16
_EPS = 1e-6
_VMEM_LIMIT = 48 * 1024 * 1024
_LANES = 128
_MESH = pl.DeviceIdType.MESH

_LR, _B1, _B2, _AEPS, _WD, _STEP = 0.001, 0.9, 0.999, 1e-08, 0.01, 10

_GM_CHUNK = 128
_GM_HEADS = 16
_S5_GT = 8
_S5_P = 64
_S5_C = 16

_NN = (((1,), (0,)), ((), ()))
_NT = (((1,), (1,)), ((), ()))
_TN = (((0,), (0,)), ((), ()))


def _cparams(sem):
    return pltpu.CompilerParams(dimension_semantics=sem, vmem_limit_bytes=_VMEM_LIMIT)


def _sig(x):
    return 1.0 / (1.0 + jnp.exp(-x))


_GC = math.sqrt(2.0 / math.pi)


def _gelu(x):
    return 0.5 * x * (1.0 + jnp.tanh(_GC * (x + 0.044715 * (x * x * x))))


def _gelu_grad(x):
    t = jnp.tanh(_GC * (x + 0.044715 * (x * x * x)))
    return 0.5 * (1.0 + t) + 0.5 * x * (1.0 - t * t) * (_GC * (1.0 + 3.0 * 0.044715 * x * x))


def _dot(a, b, dn):
    return lax.dot_general(a.astype(_MXU), b.astype(_MXU), dn, preferred_element_type=F32)


class _W:
    def __init__(self, arr, kind):
        self.a, self.kind = arr, kind
        self.R, self.C = arr.shape[2], arr.shape[3]

    def full_shape(self):
        return (2 * self.R, 4 * self.C) if self.kind == "col" else (4 * self.R, 2 * self.C)


def _part_index(kind, R, C, tr, tc, rb, cb):
    nr, nc = R // tr, C // tc
    if kind == "col":
        return cb // nc, rb // nr, rb % nr, cb % nc
    return rb // nr, cb // nc, rb % nr, cb % nc


def _wspec(w, tr, tc, rb_fn, cb_fn):
    assert w.R % tr == 0 and w.C % tc == 0, (w.R, w.C, tr, tc)

    def imap(i, j, k):
        return _part_index(w.kind, w.R, w.C, tr, tc, rb_fn(i, j, k), cb_fn(i, j, k))

    return pl.BlockSpec((None, None, tr, tc), imap)


def _gspec(kind, R, C, tr, tc):
    assert R % tr == 0 and C % tc == 0, (R, C, tr, tc)

    def imap(i, j, k):
        part, half, rbi, cbi = _part_index(kind, R, C, tr, tc, i, j)
        return half, part, rbi, cbi

    return pl.BlockSpec((None, None, tr, tc), imap)


def _mm(name, grid, a_ops, b_ops, pairs, acc_shape, n_acc, extras, outs, epilogue):
    nk = grid[2]
    na, nb, ne, no = len(a_ops), len(b_ops), len(extras), len(outs)

    def body(*refs):
        a_refs = refs[:na]
        b_refs = refs[na:na + nb]
        e_refs = refs[na + nb:na + nb + ne]
        o_refs = refs[na + nb + ne:na + nb + ne + no]
        acc_refs = refs[na + nb + ne + no:]
        k = pl.program_id(2)

        def products():
            sums = [None] * n_acc
            for ai, bi, ci, dn in pairs:
                d = _dot(a_refs[ai][...], b_refs[bi][...], dn)
                sums[ci] = d if sums[ci] is None else sums[ci] + d
            return sums

        def finish(accs):
            res = epilogue(accs, [e[...] for e in e_refs])
            for o, r in zip(o_refs, res):
                o[...] = r.astype(o.dtype)

        if nk == 1:
            finish(products())
            return

        @pl.when(k == 0)
        def _():
            for acc, d in zip(acc_refs, products()):
                acc[...] = d

        @pl.when(jnp.logical_and(k > 0, k < nk - 1))
        def _():
            for acc, d in zip(acc_refs, products()):
                acc[...] += d

        @pl.when(k == nk - 1)
        def _():
            finish([acc[...] + d for acc, d in zip(acc_refs, products())])

    ops = list(a_ops) + list(b_ops) + list(extras)
    return pl.pallas_call(
        body, name=name, grid=grid,
        in_specs=[s for _, s in ops],
        out_specs=[s for _, s in outs],
        out_shape=[s for s, _ in outs],
        scratch_shapes=[pltpu.VMEM(acc_shape, F32) for _ in range(n_acc if nk > 1 else 0)],
        compiler_params=_cparams(("parallel", "parallel", "arbitrary")),
    )(*[a for a, _ in ops])


def _bs(shape, fn):
    return pl.BlockSpec(shape, fn)


def _tile_m(L):
    return min(L, 512)


def _mm_nn(name, x, ws, tk, tn, n_out, extras, outs_sd, epilogue, tm=None, cb_offsets=None):
    M, K = x.shape
    tm = tm or _tile_m(M)
    grid = (M // tm, n_out // tn, K // tk)
    a_ops = [(x, _bs((tm, tk), lambda i, j, k: (i, k)))]
    cb_offsets = cb_offsets or [0] * len(ws)
    b_ops = [(w.a, _wspec(w, tk, tn, lambda i, j, k: k, (lambda off: lambda i, j, k: j + off)(off)))
             for w, off in zip(ws, cb_offsets)]
    pairs = [(0, bi, bi, _NN) for bi in range(len(ws))]
    mn = _bs((tm, tn), lambda i, j, k: (i, j))
    ex = [(e, mn) for e in extras]
    outs = [(jax.ShapeDtypeStruct((M, n_out), dt), mn) for dt in outs_sd]
    return _mm(name, grid, a_ops, b_ops, pairs, (tm, tn), len(ws), ex, outs, epilogue)


def _mm_nt(name, xs, ws, tn, tk, extras, outs_sd, epilogue, tm=None):
    M, Nw = xs[0].shape
    Kw = ws[0].full_shape()[0]
    tm = tm or _tile_m(M)
    grid = (M // tm, Kw // tn, Nw // tk)
    a_ops = [(x, _bs((tm, tk), lambda i, j, k: (i, k))) for x in xs]
    b_ops = [(w.a, _wspec(w, tn, tk, lambda i, j, k: j, lambda i, j, k: k)) for w in ws]
    pairs = [(i, i, 0, _NT) for i in range(len(ws))]
    mn = _bs((tm, tn), lambda i, j, k: (i, j))
    ex = [(e, mn) for e in extras]
    outs = [(jax.ShapeDtypeStruct((M, Kw), dt), mn) for dt in outs_sd]
    return _mm(name, grid, a_ops, b_ops, pairs, (tm, tn), 1, ex, outs, epilogue)


def _mm_tn(name, x, dys, kind, R, C, tm, tn, tk=None):
    L, Kw = x.shape
    Nw = dys[0].shape[1]
    tk = tk or min(L, 1024)
    grid = (Kw // tm, Nw // tn, L // tk)
    a_ops = [(x, _bs((tk, tm), lambda i, j, k: (k, i)))]
    b_ops = [(dy, _bs((tk, tn), lambda i, j, k: (k, j))) for dy in dys]
    pairs = [(0, bi, bi, _TN) for bi in range(len(dys))]
    gs = _gspec(kind, R, C, tm, tn)
    outs = [(jax.ShapeDtypeStruct((2, 4, R, C), _WIRE), gs) for _ in dys]
    return _mm(name, grid, a_ops, b_ops, pairs, (tm, tn), len(dys), [], outs, lambda accs, ex: accs)


def _row_tile(L):
    return min(L, 256)


def _rowwise(name, body, ins, outs, L, acc_outs=()):
    tr = _row_tile(L)
    n_in, n_out = len(ins), len(outs)

    def kbody(*refs):
        i_refs, o_refs, a_refs = refs[:n_in], refs[n_in:n_in + n_out], refs[n_in + n_out:]
        res, sums = body(*[r[...] for r in i_refs])
        for o, r in zip(o_refs, res):
            o[...] = r.astype(o.dtype)
        if a_refs:
            @pl.when(pl.program_id(0) == 0)
            def _():
                for a in a_refs:
                    a[...] = jnp.zeros(a.shape, F32)
            for a, s in zip(a_refs, sums):
                a[...] += s

    in_specs = []
    for arr, kind in ins:
        if kind == "row":
            in_specs.append(pl.BlockSpec((tr, arr.shape[1]), lambda i: (i, 0)))
        else:
            in_specs.append(pl.BlockSpec(arr.shape, lambda i: (0, 0)))
    out_specs = [pl.BlockSpec((tr, c), lambda i: (i, 0)) for c, _ in outs]
    out_shape = [jax.ShapeDtypeStruct((L, c), dt) for c, dt in outs]
    out_specs += [pl.BlockSpec((1, c), lambda i: (0, 0)) for c in acc_outs]
    out_shape += [jax.ShapeDtypeStruct((1, c), F32) for c in acc_outs]
    return pl.pallas_call(
        kbody, name=name, grid=(L // tr,), in_specs=in_specs, out_specs=out_specs, out_shape=out_shape,
        compiler_params=_cparams(("arbitrary",)),
    )(*[a for a, _ in ins])


def _rms_fwd(name, x, g):
    def body(xv, gv):
        r = lax.rsqrt(jnp.mean(xv * xv, axis=-1, keepdims=True) + _EPS)
        return [xv * r * gv], []
    return _rowwise(name, body, [(x, "row"), (g, "vec")], [(x.shape[1], _MXU)], x.shape[0])[0]


def _rms_bwd(name, dh, x, g, dres):
    def body(dhv, xv, gv, dr):
        r = lax.rsqrt(jnp.mean(xv * xv, axis=-1, keepdims=True) + _EPS)
        xh = xv * r
        dxh = dhv * gv
        dx = r * (dxh - xh * jnp.mean(dxh * xh, axis=-1, keepdims=True))
        return [dr + dx], [jnp.sum(dhv * xh, axis=0, keepdims=True)]
    D = x.shape[1]
    return _rowwise(name, body, [(dh, "row"), (x, "row"), (g, "vec"), (dres, "row")], [(D, F32)], x.shape[0], [D])


def _loss_head(x, g, target):
    D = x.shape[1]

    def body(xv, gv, tv):
        r = lax.rsqrt(jnp.mean(xv * xv, axis=-1, keepdims=True) + _EPS)
        xh = xv * r
        e = xh * gv - tv
        dy = e * (1.0 / D)
        dxh = dy * gv
        dx = r * (dxh - xh * jnp.mean(dxh * xh, axis=-1, keepdims=True))
        row_loss = 0.5 * jnp.mean(e * e, axis=-1, keepdims=True)
        lsum = jnp.sum(row_loss, axis=0, keepdims=True) + jnp.zeros((1, _LANES), F32)
        return [dx], [jnp.sum(dy * xh, axis=0, keepdims=True), lsum]
    return _rowwise("loss_head", body, [(x, "row"), (g, "vec"), (target, "row")], [(D, F32)], x.shape[0], [D, _LANES])


def _ple_bwd_elem(dx, pp, gt):
    def body(dxv, ppv, gtv):
        gt32 = gtv.astype(F32)
        return [dxv * ppv * gt32 * (1.0 - gt32), dxv * gt32], []
    D = dx.shape[1]
    return _rowwise("ple_bwd_elem", body, [(dx, "row"), (pp, "row"), (gt, "row")], [(D, _MXU), (D, _MXU)], dx.shape[0])


def _glu_bwd_elem(dx, val, sg):
    def body(dxv, vv, sv):
        v32, s32 = vv.astype(F32), sv.astype(F32)
        return [jnp.concatenate([dxv * s32, dxv * v32 * s32 * (1.0 - s32)], axis=1)], []
    D = dx.shape[1]
    return _rowwise("glu_bwd_elem", body, [(dx, "row"), (val, "row"), (sg, "row")], [(2 * D, _MXU)], dx.shape[0])[0]


def _gm_common(z, ln_g, ln_b, wc_bf, bsT):
    W = z.shape[1] // 2
    zu, zv = z[:, :W], z[:, W:]
    u, v = _gelu(zu), _gelu(zv)
    mu = jnp.mean(v, axis=-1, keepdims=True)
    vc = v - mu
    rstd = lax.rsqrt(jnp.mean(vc * vc, axis=-1, keepdims=True) + _EPS)
    vh = vc * rstd
    vn = vh * ln_g + ln_b
    vnb = vn.astype(_MXU)
    svs = []
    for h in range(_GM_HEADS):
        sl = slice(h * _LANES, (h + 1) * _LANES)
        svs.append(_dot(wc_bf[h], vnb[:, sl], _NN) + bsT[:, h:h + 1])
    return zu, zv, u, vh, rstd, vnb, svs


def _causal(w):
    t = lax.broadcasted_iota(jnp.int32, w.shape, w.ndim - 2)
    s = lax.broadcasted_iota(jnp.int32, w.shape, w.ndim - 1)
    return jnp.where(s <= t, w, jnp.zeros_like(w))


def _gmlp_fwd(z, ln_g, ln_b, w_s, bsT):
    L, W2 = z.shape
    W = W2 // 2

    def body(z_ref, g_ref, b_ref, ws_ref, bs_ref, m_ref):
        wc = _causal(ws_ref[...]).astype(_MXU)
        _, _, u, _, _, _, svs = _gm_common(z_ref[...], g_ref[...], b_ref[...], wc, bs_ref[...])
        for h in range(_GM_HEADS):
            sl = slice(h * _LANES, (h + 1) * _LANES)
            m_ref[:, sl] = (u[:, sl] * svs[h]).astype(m_ref.dtype)

    return pl.pallas_call(
        body, name="gmlp_fwd", grid=(L // _GM_CHUNK,),
        in_specs=[pl.BlockSpec((_GM_CHUNK, W2), lambda n: (n, 0)),
                  pl.BlockSpec((1, W), lambda n: (0, 0)), pl.BlockSpec((1, W), lambda n: (0, 0)),
                  pl.BlockSpec(w_s.shape, lambda n: (0, 0, 0)), pl.BlockSpec(bsT.shape, lambda n: (0, 0))],
        out_specs=pl.BlockSpec((_GM_CHUNK, W), lambda n: (n, 0)),
        out_shape=jax.ShapeDtypeStruct((L, W), _MXU),
        compiler_params=_cparams(("arbitrary",)),
    )(z, ln_g, ln_b, w_s, bsT)


def _gmlp_bwd(z, dm, ln_g, ln_b, w_s, bsT):
    L, W2 = z.shape
    W = W2 // 2
    T = _GM_CHUNK

    def body(z_ref, dm_ref, g_ref, b_ref, ws_ref, bs_ref, dz_ref, dws_ref, dbs_ref, dg_ref, db_ref):
        @pl.when(pl.program_id(0) == 0)
        def _():
            dws_ref[...] = jnp.zeros(dws_ref.shape, F32)
            dbs_ref[...] = jnp.zeros(dbs_ref.shape, F32)
            dg_ref[...] = jnp.zeros(dg_ref.shape, F32)
            db_ref[...] = jnp.zeros(db_ref.shape, F32)

        wc = _causal(ws_ref[...]).astype(_MXU)
        ln_g_v = g_ref[...]
        zu, zv, u, vh, rstd, vnb, svs = _gm_common(z_ref[...], ln_g_v, b_ref[...], wc, bs_ref[...])
        dmv = dm_ref[...]
        lane = lax.broadcasted_iota(jnp.int32, (T, _LANES), 1)
        dbs = jnp.zeros((T, _LANES), F32)
        dvn_parts = []
        for h in range(_GM_HEADS):
            sl = slice(h * _LANES, (h + 1) * _LANES)
            dsv = dmv[:, sl] * u[:, sl]
            dz_ref[:, sl] = (dmv[:, sl] * svs[h] * _gelu_grad(zu[:, sl])).astype(dz_ref.dtype)
            dbs = dbs + jnp.where(lane == h, jnp.sum(dsv, axis=1, keepdims=True), 0.0)
            dsvb = dsv.astype(_MXU)
            dws_ref[h] += _dot(dsvb, vnb[:, sl], _NT)
            dvn_parts.append(_dot(wc[h], dsvb, _TN))
        dbs_ref[...] += dbs
        dvn = jnp.concatenate(dvn_parts, axis=1)
        dg_ref[...] += jnp.sum(dvn * vh, axis=0, keepdims=True)
        db_ref[...] += jnp.sum(dvn, axis=0, keepdims=True)
        dxh = dvn * ln_g_v
        dv = rstd * (dxh - jnp.mean(dxh, axis=-1, keepdims=True) - vh * jnp.mean(dxh * vh, axis=-1, keepdims=True))
        dz_ref[:, W:] = (dv * _gelu_grad(zv)).astype(dz_ref.dtype)

        @pl.when(pl.program_id(0) == pl.num_programs(0) - 1)
        def _():
            dws_ref[...] = _causal(dws_ref[...])

    return pl.pallas_call(
        body, name="gmlp_bwd", grid=(L // T,),
        in_specs=[pl.BlockSpec((T, W2), lambda n: (n, 0)), pl.BlockSpec((T, W), lambda n: (n, 0)),
                  pl.BlockSpec((1, W), lambda n: (0, 0)), pl.BlockSpec((1, W), lambda n: (0, 0)),
                  pl.BlockSpec(w_s.shape, lambda n: (0, 0, 0)), pl.BlockSpec(bsT.shape, lambda n: (0, 0))],
        out_specs=[pl.BlockSpec((T, W2), lambda n: (n, 0)),
                   pl.BlockSpec(w_s.shape, lambda n: (0, 0, 0)), pl.BlockSpec((T, _LANES), lambda n: (0, 0)),
                   pl.BlockSpec((1, W), lambda n: (0, 0)), pl.BlockSpec((1, W), lambda n: (0, 0))],
        out_shape=[jax.ShapeDtypeStruct((L, W2), _MXU), jax.ShapeDtypeStruct(w_s.shape, F32),
                   jax.ShapeDtypeStruct((T, _LANES), F32),
                   jax.ShapeDtypeStruct((1, W), F32), jax.ShapeDtypeStruct((1, W), F32)],
        compiler_params=_cparams(("arbitrary",)),
    )(z, dm, ln_g, ln_b, w_s, bsT)


def _s5_prep_math(a_re, a_im, log_dt):
    dt = jnp.exp(log_dt)
    xr, xi = a_re * dt, a_im * dt
    e = jnp.exp(xr)
    lbr, lbi = e * jnp.cos(xi), e * jnp.sin(xi)
    dn = a_re * a_re + a_im * a_im
    nr, ni = lbr - 1.0, lbi
    pr, pi = nr * a_re + ni * a_im, ni * a_re - nr * a_im
    return dt, lbr, lbi, dn, nr, ni, pr, pi


def _vm():
    return pl.BlockSpec(memory_space=pltpu.VMEM)


def _s5_prep(a_re, a_im, log_dt, bT_re, bT_im):
    def body(ar_ref, ai_ref, ld_ref, br_ref, bi_ref, lbr_ref, lbi_ref, Br_ref, Bi_ref):
        _, lbr, lbi, dn, _, _, pr, pi = _s5_prep_math(ar_ref[...], ai_ref[...], ld_ref[...])
        cr, ci = pr / dn, pi / dn
        lbr_ref[...] = lbr
        lbi_ref[...] = lbi
        br, bi = br_ref[...], bi_ref[...]
        Br_ref[...] = cr[None] * br - ci[None] * bi
        Bi_ref[...] = cr[None] * bi + ci[None] * br

    sd = jax.ShapeDtypeStruct
    return pl.pallas_call(
        body, name="s5_prep", in_specs=[_vm()] * 5, out_specs=[_vm()] * 4,
        out_shape=[sd(a_re.shape, F32), sd(a_re.shape, F32), sd(bT_re.shape, F32), sd(bT_re.shape, F32)],
    )(a_re, a_im, log_dt, bT_re, bT_im)


def _s5_prep_bwd(a_re, a_im, log_dt, bT_re, bT_im, dlbr_s, dlbi_s, dBr, dBi):
    def body(ar_ref, ai_ref, ld_ref, br_ref, bi_ref, dlr_ref, dli_ref, dBr_ref, dBi_ref,
             dar_ref, dai_ref, dld_ref, dbr_ref, dbi_ref):
        a_re_v, a_im_v = ar_ref[...], ai_ref[...]
        dt, lbr, lbi, dn, nr, ni, pr, pi = _s5_prep_math(a_re_v, a_im_v, ld_ref[...])
        cr, ci = pr / dn, pi / dn
        br, bi, dBr_v, dBi_v = br_ref[...], bi_ref[...], dBr_ref[...], dBi_ref[...]
        dbr_ref[...] = cr[None] * dBr_v + ci[None] * dBi_v
        dbi_ref[...] = cr[None] * dBi_v - ci[None] * dBr_v
        dcr = jnp.sum(br * dBr_v + bi * dBi_v, axis=0)
        dci = jnp.sum(br * dBi_v - bi * dBr_v, axis=0)
        dpr, dpi = dcr / dn, dci / dn
        ddn = -(dcr * pr + dci * pi) / (dn * dn)
        dnr = dpr * a_re_v - dpi * a_im_v
        dni = dpr * a_im_v + dpi * a_re_v
        dlbr = dlr_ref[...] + dnr
        dlbi = dli_ref[...] + dni
        dxr = dlbr * lbr + dlbi * lbi
        dxi = dlbi * lbr - dlbr * lbi
        dar_ref[...] = dpr * nr + dpi * ni + 2.0 * ddn * a_re_v + dxr * dt
        dai_ref[...] = dpr * ni - dpi * nr + 2.0 * ddn * a_im_v + dxi * dt
        dld_ref[...] = jnp.sum(dxr * a_re_v + dxi * a_im_v, axis=1, keepdims=True) * dt

    sd = jax.ShapeDtypeStruct
    return pl.pallas_call(
        body, name="s5_prep_bwd", in_specs=[_vm()] * 9, out_specs=[_vm()] * 5,
        out_shape=[sd(a_re.shape, F32), sd(a_re.shape, F32), sd(log_dt.shape, F32),
                   sd(bT_re.shape, F32), sd(bT_re.shape, F32)],
    )(a_re, a_im, log_dt, bT_re, bT_im, dlbr_s, dlbi_s, dBr, dBi)


def _shift_rows(v, down):
    n = v.shape[0]
    rolled = pltpu.roll(v, 1 if down else n - 1, 0)
    row = lax.broadcasted_iota(jnp.int32, v.shape, 0)
    return jnp.where(row == (0 if down else n - 1), 0.0, rolled)


def _cmul(ar, ai, br, bi):
    return ar * br - ai * bi, ar * bi + ai * br


_SEG = 8
_UNROLL = 8


def _seg_rows(k, nk):
    return pl.ds(k, _SEG, stride=nk)


class _Chunked:
    def __init__(self, ref):
        self.ref, self.n = ref, ref.shape[0]

    def rows(self, rows):
        return jnp.concatenate([self.ref[j, rows, :] for j in range(self.n)], axis=1)

    def set_rows(self, rows, v):
        for j in range(self.n):
            self.ref[j, rows, :] = v[:, j * _LANES:(j + 1) * _LANES]

    def all(self):
        return jnp.concatenate([self.ref[j] for j in range(self.n)], axis=1)

    def set_all(self, v):
        for j in range(self.n):
            self.ref[j] = v[:, j * _LANES:(j + 1) * _LANES]


def _segment_inits(er, ei, ar, ai, nk, down):
    pr, pi = ar, ai
    for _ in range(int(math.log2(nk))):
        pr, pi = _cmul(pr, pi, pr, pi)
    fr, fi = er, ei
    for _ in range(_SEG - 1):
        sr, si = _shift_rows(fr, down), _shift_rows(fi, down)
        mr, mi = _cmul(pr, pi, sr, si)
        fr, fi = er + mr, ei + mi
    return _shift_rows(fr, down), _shift_rows(fi, down)


def _scan_states(x_re, x_im, ar, ai, nk):
    lanes = ar.shape[1]

    def step(k, s):
        rows = _seg_rows(k, nk)
        mr, mi = _cmul(ar, ai, s[0], s[1])
        return mr + x_re.rows(rows), mi + x_im.rows(rows)

    zero = jnp.zeros((_SEG, lanes), F32)
    er, ei = lax.fori_loop(0, nk, step, (zero, zero), unroll=_UNROLL)
    ir, ii = _segment_inits(er, ei, ar, ai, nk, True)

    def step2(k, s):
        rows = _seg_rows(k, nk)
        mr, mi = _cmul(ar, ai, s[0], s[1])
        nr, ni = mr + x_re.rows(rows), mi + x_im.rows(rows)
        x_re.set_rows(rows, nr)
        x_im.set_rows(rows, ni)
        return nr, ni

    lax.fori_loop(0, nk, step2, (ir, ii), unroll=_UNROLL)


def _s5_tile_fwd(u, bd_re, bd_im, cd_re, cd_im, ar, ai, d, s_re, s_im, nk):
    s_re.set_all(_dot(u, bd_re, _NN))
    s_im.set_all(_dot(u, bd_im, _NN))
    _scan_states(s_re, s_im, ar, ai, nk)
    return _dot(s_re.all(), cd_re, _NN) - _dot(s_im.all(), cd_im, _NN) + d * u


def _s5_specs(L, T):
    lanes = _S5_GT * _S5_P
    u_spec = pl.BlockSpec((L, _LANES), lambda t: (0, t))
    bd_spec = pl.BlockSpec((None, _LANES, lanes), lambda t: (t, 0, 0))
    cd_spec = pl.BlockSpec((None, lanes, _LANES), lambda t: (t, 0, 0))
    lam_spec = pl.BlockSpec((None, 1, lanes), lambda t: (t, 0, 0))
    d_spec = pl.BlockSpec((1, _LANES), lambda t: (0, t))
    return lanes, u_spec, bd_spec, cd_spec, lam_spec, d_spec


def _s5_fwd(u, bd_re, bd_im, cd_re, cd_im, lam_re, lam_im, d):
    L, Wd = u.shape
    T = Wd // _LANES
    nk = L // _SEG
    lanes, u_spec, bd_spec, cd_spec, lam_spec, d_spec = _s5_specs(L, T)

    def body(u_ref, bdr, bdi, cdr, cdi, lr, li, d_ref, g_ref, s_re, s_im):
        s_re, s_im = _Chunked(s_re), _Chunked(s_im)
        ar = jnp.broadcast_to(lr[...], (_SEG, lanes))
        ai = jnp.broadcast_to(li[...], (_SEG, lanes))
        y = _s5_tile_fwd(u_ref[...], bdr[...], bdi[...], cdr[...], cdi[...], ar, ai, d_ref[...], s_re, s_im, nk)
        g_ref[...] = _gelu(y).astype(g_ref.dtype)

    return pl.pallas_call(
        body, name="s5_fwd", grid=(T,),
        in_specs=[u_spec, bd_spec, bd_spec, cd_spec, cd_spec, lam_spec, lam_spec, d_spec],
        out_specs=u_spec, out_shape=jax.ShapeDtypeStruct((L, Wd), _MXU),
        scratch_shapes=[pltpu.VMEM((lanes // _LANES, L, _LANES), F32) for _ in range(2)],
        compiler_params=_cparams(("arbitrary",)),
    )(u, bd_re, bd_im, cd_re, cd_im, lam_re, lam_im, d)


def _s5_bwd(u, dg, bd_re, bd_im, cd_re, cd_im, lam_re, lam_im, d):
    L, Wd = u.shape
    T = Wd // _LANES
    nk = L // _SEG
    lanes, u_spec, bd_spec, cd_spec, lam_spec, d_spec = _s5_specs(L, T)

    def body(u_ref, dg_ref, bdr, bdi, cdr, cdi, lr, li, d_ref,
             du_ref, dbdr, dbdi, dcdr, dcdi, dlr, dli, dd_ref, s_re, s_im, g_re, g_im):
        s_re, s_im, g_re, g_im = _Chunked(s_re), _Chunked(s_im), _Chunked(g_re), _Chunked(g_im)
        ar = jnp.broadcast_to(lr[...], (_SEG, lanes))
        ai = jnp.broadcast_to(li[...], (_SEG, lanes))
        uv, dv = u_ref[...], d_ref[...]
        y = _s5_tile_fwd(uv, bdr[...], bdi[...], cdr[...], cdi[...], ar, ai, dv, s_re, s_im, nk)
        dy = dg_ref[...] * _gelu_grad(y)
        dd_ref[...] = jnp.sum(dy * uv, axis=0, keepdims=True)
        dyb = dy.astype(_MXU)
        dcdr[...] = _dot(dyb, s_re.all(), _TN)
        dcdi[...] = -_dot(dyb, s_im.all(), _TN)
        g_re.set_all(_dot(dyb, cdr[...], _NT))
        g_im.set_all(-_dot(dyb, cdi[...], _NT))

        nai = -ai

        def step(j, s):
            rows = _seg_rows(nk - 1 - j, nk)
            mr, mi = _cmul(ar, nai, s[0], s[1])
            return mr + g_re.rows(rows), mi + g_im.rows(rows)

        zero = jnp.zeros((_SEG, lanes), F32)
        er, ei = lax.fori_loop(0, nk, step, (zero, zero), unroll=_UNROLL)
        ir, ii = _segment_inits(er, ei, ar, nai, nk, False)

        def acc_lam(gr, gi, pr, pi, acc):
            return acc[0] + gr * pr + gi * pi, acc[1] + gi * pr - gr * pi

        def step2(j, carry):
            s, acc = carry
            k = nk - 1 - j
            rows = _seg_rows(k, nk)
            mr, mi = _cmul(ar, nai, s[0], s[1])
            nr, ni = mr + g_re.rows(rows), mi + g_im.rows(rows)
            g_re.set_rows(rows, nr)
            g_im.set_rows(rows, ni)
            prev = _seg_rows(k - 1, nk)
            return (nr, ni), acc_lam(nr, ni, s_re.rows(prev), s_im.rows(prev), acc)

        (g0r, g0i), acc = lax.fori_loop(0, nk - 1, step2, ((ir, ii), (zero, zero)), unroll=_UNROLL)
        first = _seg_rows(0, nk)
        mr, mi = _cmul(ar, nai, g0r, g0i)
        nr, ni = mr + g_re.rows(first), mi + g_im.rows(first)
        g_re.set_rows(first, nr)
        g_im.set_rows(first, ni)
        last = _seg_rows(nk - 1, nk)
        acc = acc_lam(nr, ni, _shift_rows(s_re.rows(last), True), _shift_rows(s_im.rows(last), True), acc)
        dlr[...] = jnp.sum(acc[0], axis=0, keepdims=True)
        dli[...] = jnp.sum(acc[1], axis=0, keepdims=True)

        gtr, gti = g_re.all().astype(_MXU), g_im.all().astype(_MXU)
        du = _dot(gtr, bdr[...], _NT) + _dot(gti, bdi[...], _NT) + dy * dv
        du_ref[...] = du.astype(du_ref.dtype)
        ub = uv.astype(_MXU)
        dbdr[...] = _dot(ub, gtr, _TN)
        dbdi[...] = _dot(ub, gti, _TN)

    sd = jax.ShapeDtypeStruct
    big = sd((T, _LANES, lanes), F32)
    return pl.pallas_call(
        body, name="s5_bwd", grid=(T,),
        in_specs=[u_spec, u_spec, bd_spec, bd_spec, cd_spec, cd_spec, lam_spec, lam_spec, d_spec],
        out_specs=[u_spec, bd_spec, bd_spec, bd_spec, bd_spec, lam_spec, lam_spec, d_spec],
        out_shape=[sd((L, Wd), _MXU), big, big, big, big, sd((T, 1, lanes), F32), sd((T, 1, lanes), F32),
                   sd((1, Wd), F32)],
        scratch_shapes=[pltpu.VMEM((lanes // _LANES, L, _LANES), F32) for _ in range(4)],
        compiler_params=_cparams(("arbitrary",)),
    )(u, dg, bd_re, bd_im, cd_re, cd_im, lam_re, lam_im, d)


def _half_tile(R):
    for t in (256, 352, 128):
        if R % t == 0:
            return t
    raise ValueError(R)


def _nat_spec(kind, R, C, tr, layer_axis):
    nr = R // tr
    if kind == "col":
        fn = lambda l, h, i: (h * nr + i, 0)
    else:
        fn = lambda l, h, i: (i, h)
    if layer_axis:
        return pl.BlockSpec((None, tr, C), lambda l, h, i: (l,) + fn(l, h, i))
    return pl.BlockSpec((tr, C), fn)


def _cast_shard(name, w, layer, kind, R, C, q):
    tr = _half_tile(R)
    nr = R // tr

    def body(q_ref, w_ref, o_ref):
        o_ref[...] = w_ref[...].astype(o_ref.dtype)

    if kind == "col":
        in_map = lambda h, i, q_ref: (layer, h * nr + i, 0)
    else:
        in_map = lambda h, i, q_ref: (layer, i, h)
    gs = pltpu.PrefetchScalarGridSpec(
        num_scalar_prefetch=1, grid=(2, nr),
        in_specs=[pl.BlockSpec((None, tr, C), in_map)],
        out_specs=pl.BlockSpec((None, None, tr, C), lambda h, i, q_ref: (q_ref[0], h, i, 0)))
    return pl.pallas_call(
        body, name=name, grid_spec=gs, out_shape=jax.ShapeDtypeStruct((4, 2, R, C), _WIRE),
        compiler_params=_cparams(("arbitrary", "arbitrary")),
    )(q.astype(jnp.int32).reshape(1), w)


def _adam_math(w, g, m, v):
    m2 = _B1 * m + (1.0 - _B1) * g
    v2 = _B2 * v + (1.0 - _B2) * (g * g)
    m_hat = m2 / (1.0 - _B1 ** _STEP)
    v_hat = v2 / (1.0 - _B2 ** _STEP)
    delta = -_LR * (m_hat / (jnp.sqrt(v_hat) + _AEPS) + _WD * w)
    return delta, m2, v2


def _adamw_big(name, w, m, v, grads, kind, R, C):
    nl = w.shape[0]
    tr = _half_tile(R)
    nr = R // tr

    def body(*refs):
        w_ref, m_ref, v_ref = refs[:3]
        g_refs = refs[3:3 + nl]
        go_ref, d_ref, mo_ref, vo_ref = refs[3 + nl:]
        g = g_refs[0][...]
        for l in range(1, nl):
            g = jnp.where(pl.program_id(0) == l, g_refs[l][...], g)
        delta, m2, v2 = _adam_math(w_ref[...], g, m_ref[...], v_ref[...])
        go_ref[...] = g
        d_ref[...] = delta
        mo_ref[...] = m2
        vo_ref[...] = v2

    nat = _nat_spec(kind, R, C, tr, True)

    def gspec(l0):
        def imap(l, h, i):
            on = l == l0
            return jnp.where(on, h, 1 if l0 < nl - 1 else 0), jnp.where(on, i, nr - 1 if l0 < nl - 1 else 0), 0
        if nl == 1:
            return pl.BlockSpec((None, tr, C), lambda l, h, i: (h, i, 0))
        return pl.BlockSpec((None, tr, C), imap)

    sd = jax.ShapeDtypeStruct(w.shape, F32)
    return pl.pallas_call(
        body, name=name, grid=(nl, 2, nr),
        in_specs=[nat, nat, nat] + [gspec(l) for l in range(nl)],
        out_specs=[nat] * 4, out_shape=[sd] * 4,
        compiler_params=_cparams(("arbitrary", "arbitrary", "arbitrary")),
    )(w, m, v, *grads)


def _adamw_flat(w, g, m, v):
    rows = w.shape[0]
    tr = rows // 8 if rows % 64 == 0 else rows

    def body(w_ref, g_ref, m_ref, v_ref, d_ref, mo_ref, vo_ref):
        delta, m2, v2 = _adam_math(w_ref[...], g_ref[...], m_ref[...], v_ref[...])
        d_ref[...] = delta
        mo_ref[...] = m2
        vo_ref[...] = v2

    spec = pl.BlockSpec((tr, _LANES), lambda i: (i, 0))
    sd = jax.ShapeDtypeStruct(w.shape, F32)
    return pl.pallas_call(
        body, name="adamw_small", grid=(rows // tr,), in_specs=[spec] * 4, out_specs=[spec] * 3,
        out_shape=[sd] * 3, compiler_params=_cparams(("arbitrary",)),
    )(w, g, m, v)


def _place():
    x, y, c = lax.axis_index("x"), lax.axis_index("y"), lax.axis_index("c")
    chips = [(1 - x, y), (x, 1 - y), (1 - x, 1 - y)]
    return x, y, c, 2 * x + y, chips


def _any():
    return pl.BlockSpec(memory_space=pl.ANY)


def _remote(src, dst, ssem, rsem, dev):
    return pltpu.make_async_remote_copy(src_ref=src, dst_ref=dst, send_sem=ssem, recv_sem=rsem,
                                        device_id=dev, device_id_type=_MESH)


def _allgather(name, bufs):
    n = len(bufs)

    def body(*refs):
        s_refs, g_refs = refs[:n], refs[n:2 * n]
        send1, recv1, send2, recv2 = refs[2 * n:]
        x, y, c, q, chips = _place()
        sib = (x, y, 1 - c)
        first, passed = [], []
        for a in range(n):
            for j, (rx, ry) in enumerate(chips):
                k = 3 * a + j
                first.append(_remote(s_refs[a].at[q, c], g_refs[a].at[q, c], send1.at[k], recv1.at[k], (rx, ry, c)))
                land = g_refs[a].at[2 * rx + ry, c]
                passed.append(_remote(land, land, send2.at[k], recv2.at[k], sib))
        for cp in first:
            cp.start()
        for a in range(n):
            for j, (rx, ry) in enumerate(chips):
                k = 3 * a + j
                land = g_refs[a].at[2 * rx + ry, c]
                _remote(land, land, send1.at[k], recv1.at[k], (rx, ry, c)).wait_recv()
                passed[k].start()
        for a in range(n):
            for j, (rx, ry) in enumerate(chips):
                k = 3 * a + j
                other = g_refs[a].at[2 * rx + ry, 1 - c]
                _remote(other, other, send2.at[k], recv2.at[k], sib).wait_recv()
        for cp in first + passed:
            cp.wait_send()

    return pl.pallas_call(
        body, name=name, in_specs=[_any()] * n, out_specs=[_any()] * n,
        out_shape=[jax.ShapeDtypeStruct(b.shape, b.dtype) for b in bufs],
        input_output_aliases={a: a for a in range(n)},
        scratch_shapes=[pltpu.SemaphoreType.DMA((3 * n,)) for _ in range(4)],
    )(*bufs)


def _swap_halves(name, grads):
    n = len(grads)

    def body(*refs):
        g_refs, t_refs = refs[:n], refs[n:2 * n]
        send, recv = refs[2 * n:]
        x, y, c, _, _ = _place()
        cps = [_remote(g_refs[a].at[1 - c], t_refs[a], send.at[a], recv.at[a], (x, y, 1 - c)) for a in range(n)]
        for cp in cps:
            cp.start()
        for cp in cps:
            cp.wait()

    return pl.pallas_call(
        body, name=name, in_specs=[_any()] * n, out_specs=[_any()] * n,
        out_shape=[jax.ShapeDtypeStruct(g.shape[1:], g.dtype) for g in grads],
        scratch_shapes=[pltpu.SemaphoreType.DMA((n,)), pltpu.SemaphoreType.DMA((n,))],
    )(*grads)


def _chip_sum(name, g, t):
    _, _, R, C = g.shape
    tr = _half_tile(R)

    def body(c_ref, g_ref, t_ref, o_ref):
        o_ref[...] = (g_ref[...].astype(F32) + t_ref[...].astype(F32)).astype(o_ref.dtype)

    c = lax.axis_index("c").astype(jnp.int32).reshape(1)
    gs = pltpu.PrefetchScalarGridSpec(
        num_scalar_prefetch=1, grid=(4, R // tr),
        in_specs=[pl.BlockSpec((None, None, tr, C), lambda r, i, c_ref: (c_ref[0], r, i, 0)),
                  pl.BlockSpec((None, tr, C), lambda r, i, c_ref: (r, i, 0))],
        out_specs=pl.BlockSpec((None, tr, C), lambda r, i, c_ref: (r, i, 0)))
    return pl.pallas_call(
        body, name=name, grid_spec=gs, out_shape=jax.ShapeDtypeStruct((4, R, C), _WIRE),
        compiler_params=_cparams(("arbitrary", "arbitrary")),
    )(c, g, t)


def _scatter_parts(name, parts):
    n = len(parts)

    def body(*refs):
        p_refs, t_refs = refs[:n], refs[n:2 * n]
        send, recv = refs[2 * n:]
        x, y, c, q, chips = _place()
        cps = []
        for a in range(n):
            for j, (rx, ry) in enumerate(chips):
                k = 3 * a + j
                cps.append(_remote(p_refs[a].at[2 * rx + ry], t_refs[a].at[q], send.at[k], recv.at[k], (rx, ry, c)))
        for cp in cps:
            cp.start()
        for a in range(n):
            for j, (rx, ry) in enumerate(chips):
                k = 3 * a + j
                land = t_refs[a].at[2 * rx + ry]
                _remote(land, land, send.at[k], recv.at[k], (rx, ry, c)).wait_recv()
        for cp in cps:
            cp.wait_send()

    return pl.pallas_call(
        body, name=name, in_specs=[_any()] * n, out_specs=[_any()] * n,
        out_shape=[jax.ShapeDtypeStruct(p.shape, p.dtype) for p in parts],
        scratch_shapes=[pltpu.SemaphoreType.DMA((3 * n,)), pltpu.SemaphoreType.DMA((3 * n,))],
    )(*parts)


def _sum_parts(name, p, t, where):
    _, R, C = t.shape
    tr = _half_tile(R)

    def body(w_ref, p_ref, t0_ref, t1_ref, t2_ref, o_ref):
        o_ref[...] = (p_ref[...].astype(F32) + t0_ref[...].astype(F32)
                      + t1_ref[...].astype(F32) + t2_ref[...].astype(F32))

    def part(slot):
        return pl.BlockSpec((None, tr, C), lambda i, w_ref: (w_ref[slot], i, 0))

    gs = pltpu.PrefetchScalarGridSpec(
        num_scalar_prefetch=1, grid=(R // tr,), in_specs=[part(0), part(1), part(2), part(3)], out_specs=part(4))
    return pl.pallas_call(
        body, name=name, grid_spec=gs, out_shape=jax.ShapeDtypeStruct((2, R, C), F32),
        compiler_params=_cparams(("arbitrary",)),
    )(where, p, t, t, t)


def _join_halves(name, bufs):
    n = len(bufs)

    def body(*refs):
        h_refs, o_refs = refs[:n], refs[n:2 * n]
        send, recv = refs[2 * n:]
        x, y, c, _, _ = _place()
        cps = [_remote(h_refs[a].at[c], o_refs[a].at[c], send.at[a], recv.at[a], (x, y, 1 - c)) for a in range(n)]
        for cp in cps:
            cp.start()
        for a in range(n):
            other = o_refs[a].at[1 - c]
            _remote(other, other, send.at[a], recv.at[a], (x, y, 1 - c)).wait_recv()
        for cp in cps:
            cp.wait_send()

    return pl.pallas_call(
        body, name=name, in_specs=[_any()] * n, out_specs=[_any()] * n,
        out_shape=[jax.ShapeDtypeStruct(b.shape, b.dtype) for b in bufs],
        input_output_aliases={a: a for a in range(n)},
        scratch_shapes=[pltpu.SemaphoreType.DMA((n,)), pltpu.SemaphoreType.DMA((n,))],
    )(*bufs)


def _reduce_scatter(tag, grads):
    x, y, c, q, chips = _place()
    where = jnp.stack([q] + [2 * rx + ry for rx, ry in chips] + [c]).astype(jnp.int32)
    swapped = _swap_halves(f"rs_swap_{tag}", grads)
    parts = [_chip_sum(f"rs_chipsum_{tag}_{a}", g, t) for a, (g, t) in enumerate(zip(grads, swapped))]
    landed = _scatter_parts(f"rs_scatter_{tag}", parts)
    halves = [_sum_parts(f"rs_sum_{tag}_{a}", p, t, where) for a, (p, t) in enumerate(zip(parts, landed))]
    return _join_halves(f"rs_join_{tag}", halves)


def _allreduce_small(name, v):
    _, R, _ = v.shape

    def body(v_ref, o_ref, land, acc, send1, recv1, send2, recv2):
        x, y, c = lax.axis_index("x"), lax.axis_index("y"), lax.axis_index("c")
        me = 4 * x + 2 * y + c
        peers = []
        for k in range(1, 8):
            dx, dy, dc = (k >> 2) & 1, (k >> 1) & 1, k & 1
            px, py, pc = (1 - x if dx else x), (1 - y if dy else y), (1 - c if dc else c)
            peers.append((k, (px, py, pc), 4 * px + 2 * py + pc))
        land[me] = v_ref[me]
        out1 = [_remote(v_ref.at[pid], land.at[me], send1.at[k], recv1.at[k], dev) for k, dev, pid in peers]
        for cp in out1:
            cp.start()
        for k, dev, pid in peers:
            _remote(land.at[pid], land.at[pid], send1.at[k], recv1.at[k], dev).wait_recv()
        total = land[0]
        for j in range(1, 8):
            total = total + land[j]
        acc[...] = total
        o_ref[me] = total
        out2 = [_remote(acc, o_ref.at[me], send2.at[k], recv2.at[k], dev) for k, dev, pid in peers]
        for cp in out2:
            cp.start()
        for k, dev, pid in peers:
            _remote(o_ref.at[pid], o_ref.at[pid], send2.at[k], recv2.at[k], dev).wait_recv()
        for cp in out1 + out2:
            cp.wait_send()

    return pl.pallas_call(
        body, name=name, in_specs=[_vm()], out_specs=_vm(),
        out_shape=jax.ShapeDtypeStruct(v.shape, F32),
        scratch_shapes=[pltpu.VMEM(v.shape, F32), pltpu.VMEM((R, _LANES), F32)]
        + [pltpu.SemaphoreType.DMA((8,)) for _ in range(4)],
        compiler_params=pltpu.CompilerParams(vmem_limit_bytes=_VMEM_LIMIT),
    )(v)


_WEIGHT_NAMES = ['norm_mix', 'norm_ffn', 'norm_ple', 'norm_final', 'gm_w_in', 'gm_ln_g', 'gm_ln_b', 'gm_w_s',
                 'gm_b_s', 'gm_w_out', 's5_w_in', 's5_a_re', 's5_a_im', 's5_log_dt', 's5_b_re', 's5_b_im',
                 's5_c_re', 's5_c_im', 's5_d', 's5_w_out', 'ffn_w1', 'ffn_w3', 'ffn_w2', 'ple_w_gate', 'ple_w_proj']
_BIG = {'gm_w_in': 'col', 'gm_w_out': 'row', 's5_w_in': 'row', 's5_w_out': 'col', 'ffn_w1': 'col',
        'ffn_w3': 'col', 'ffn_w2': 'row', 'ple_w_gate': 'row', 'ple_w_proj': 'col'}


def _rc(kind, shard_shape):
    rows, cols = shard_shape[-2:]
    return (rows // 2, cols) if kind == "col" else (rows, cols // 2)


def _pack(vecs, rows_multiple):
    flat = jnp.concatenate([a.reshape(-1).astype(F32) for a in vecs])
    unit = rows_multiple * _LANES
    pad = (-flat.shape[0]) % unit
    return jnp.pad(flat, (0, pad)).reshape(-1, _LANES)


def _unpack(buf, shapes):
    flat = buf.reshape(-1)
    out, off = [], 0
    for s in shapes:
        n = math.prod(s)
        out.append(flat[off:off + n].reshape(s))
        off += n
    return out


def _ident(accs, ex):
    return accs


def _add_resid(accs, ex):
    return [accs[0] + ex[0]]


def _swiglu_epi(accs, ex):
    a, b = accs
    return [a, b, a * _sig(a) * b]


def _swiglu_bwd_epi(accs, ex):
    df = accs[0]
    a, b = ex[0].astype(F32), ex[1].astype(F32)
    sa = _sig(a)
    return [df * b * (sa * (1.0 + a * (1.0 - sa))), df * (a * sa)]


def _ple_epi(accs, ex):
    gt = _sig(accs[0])
    return [ex[0] + gt * ex[1], gt]


def _glu_epi(accs, ex):
    val, sg = accs[0], _sig(accs[1])
    return [ex[0] + val * sg, val, sg]


def _block_diag(t):
    eye = jnp.eye(_S5_GT, dtype=t.dtype)
    return t[:, :, :, None, :] * eye[None, :, None, :, None]


def _diag_blocks(t, T, a, b):
    d = jnp.diagonal(t.reshape(T, _S5_GT, a, _S5_GT, b), axis1=1, axis2=3)
    return jnp.transpose(d, (0, 3, 1, 2))


def kernel(x, p, norm_mix, norm_ffn, norm_ple, norm_final, gm_w_in, gm_ln_g, gm_ln_b, gm_w_s, gm_b_s, gm_w_out, s5_w_in, s5_a_re, s5_a_im, s5_log_dt, s5_b_re, s5_b_im, s5_c_re, s5_c_im, s5_d, s5_w_out, ffn_w1, ffn_w3, ffn_w2, ple_w_gate, ple_w_proj, loss_target, m_norm_mix, m_norm_ffn, m_norm_ple, m_norm_final, m_gm_w_in, m_gm_ln_g, m_gm_ln_b, m_gm_w_s, m_gm_b_s, m_gm_w_out, m_s5_w_in, m_s5_a_re, m_s5_a_im, m_s5_log_dt, m_s5_b_re, m_s5_b_im, m_s5_c_re, m_s5_c_im, m_s5_d, m_s5_w_out, m_ffn_w1, m_ffn_w3, m_ffn_w2, m_ple_w_gate, m_ple_w_proj, v_norm_mix, v_norm_ffn, v_norm_ple, v_norm_final, v_gm_w_in, v_gm_ln_g, v_gm_ln_b, v_gm_w_s, v_gm_b_s, v_gm_w_out, v_s5_w_in, v_s5_a_re, v_s5_a_im, v_s5_log_dt, v_s5_b_re, v_s5_b_im, v_s5_c_re, v_s5_c_im, v_s5_d, v_s5_w_out, v_ffn_w1, v_ffn_w3, v_ffn_w2, v_ple_w_gate, v_ple_w_proj):
    env = dict(locals())
    w = {n: env[n] for n in _WEIGHT_NAMES}
    mom = {n: env["m_" + n] for n in _WEIGHT_NAMES}
    var = {n: env["v_" + n] for n in _WEIGHT_NAMES}
    xs, tgt = x[0], loss_target[0]
    L, D = xs.shape
    depth = norm_mix.shape[0]
    qx, qy = lax.axis_index("x"), lax.axis_index("y")
    q = 2 * qx + qy

    def gather(tag, items):
        shards = []
        for name, layer in items:
            kind = _BIG[name]
            R, C = _rc(kind, w[name].shape)
            shards.append(_cast_shard(f"cast_{name}{layer}", w[name], layer, kind, R, C, q))
        full = _allgather(f"ag_{tag}", shards)
        return {it: _W(f, _BIG[it[0]]) for it, f in zip(items, full)}

    W = {}
    W.update(gather("mix0", [("gm_w_in", 0), ("gm_w_out", 0)]))
    for i in range(depth):
        W.update(gather(f"ffn{i}", [("ffn_w1", i), ("ffn_w3", i), ("ffn_w2", i)]))
        W.update(gather(f"ple{i}", [("ple_w_gate", i), ("ple_w_proj", i)]))
    W.update(gather("mix1", [("s5_w_in", 0), ("s5_w_out", 0)]))

    d_slots = jnp.zeros((4, D // 4), F32)
    d_slots = lax.dynamic_update_slice(d_slots, s5_d.astype(F32), (q, 0))
    d_sum = _allreduce_small("ar_s5_d", _pack([d_slots], 64).reshape(8, -1, _LANES))
    d_full = (d_sum.reshape(-1)[:D] * 0.5).reshape(1, D)

    def ffn_fwd(i, xin):
        hf = _rms_fwd(f"rms_ffn{i}", xin, norm_ffn[i:i + 1])
        a, b, f = _mm_nn(f"ffn_up{i}", hf, [W["ffn_w1", i], W["ffn_w3", i]], 1024, 1408, ffn_w2.shape[1] * 4,
                         [], [_MXU, _MXU, _MXU], _swiglu_epi)
        xo = _mm_nn(f"ffn_down{i}", f, [W["ffn_w2", i]], 1408, 1024, D, [xin], [F32], _add_resid)[0]
        return xo, (xin, hf, a, b, f)

    def ple_fwd(i, xin):
        hp = _rms_fwd(f"rms_ple{i}", xin, norm_ple[i:i + 1])
        pi = p[i, 0]
        pp = _mm_nn(f"ple_proj{i}", pi, [W["ple_w_proj", i]], 128, 512, D, [], [F32], _ident)[0]
        xo, gt = _mm_nn(f"ple_gate{i}", hp, [W["ple_w_gate", i]], 512, 1024, D, [xin, pp], [F32, _MXU], _ple_epi)
        return xo, (xin, hp, pi, pp, gt)

    h0 = _rms_fwd("rms_mix0", xs, norm_mix[0:1])
    z = _mm_nn("gm_in", h0, [W["gm_w_in", 0]], 1024, 1024, 2 * D, [], [F32], _ident)[0]
    bsT = gm_b_s[0].T
    gm_m = _gmlp_fwd(z, gm_ln_g, gm_ln_b, gm_w_s[0], bsT)
    x1 = _mm_nn("gm_out", gm_m, [W["gm_w_out", 0]], 512, 1024, D, [xs], [F32], _add_resid)[0]
    x2, ffn0 = ffn_fwd(0, x1)
    x3, ple0 = ple_fwd(0, x2)

    T = D // _LANES
    lanes = _S5_GT * _S5_P
    a_re, a_im, log_dt = s5_a_re[0], s5_a_im[0], s5_log_dt[0][:, None]
    bT_re, bT_im = jnp.transpose(s5_b_re[0], (2, 0, 1)), jnp.transpose(s5_b_im[0], (2, 0, 1))
    lbr, lbi, BrT, BiT = _s5_prep(a_re, a_im, log_dt, bT_re, bT_im)

    def to_bd(BT):
        t = jnp.transpose(BT.reshape(_S5_C, T, _S5_GT, _S5_P), (1, 2, 0, 3))
        return _block_diag(t).reshape(T, _LANES, lanes).astype(_MXU)

    def to_cd(cw):
        t = jnp.transpose(cw.reshape(T, _S5_GT, _S5_C, _S5_P), (0, 1, 3, 2))
        return _block_diag(t).reshape(T, lanes, _LANES).astype(_MXU)

    bd_re, bd_im = to_bd(BrT), to_bd(BiT)
    cd_re, cd_im = to_cd(s5_c_re[0].astype(F32)), to_cd(s5_c_im[0].astype(F32))
    lam_re, lam_im = lbr.reshape(T, 1, lanes), lbi.reshape(T, 1, lanes)

    h1 = _rms_fwd("rms_mix1", x3, norm_mix[1:2])
    u = _mm_nn("s5_in", h1, [W["s5_w_in", 0]], 512, 1024, D, [], [F32], _ident)[0]
    s5_g = _s5_fwd(u, bd_re, bd_im, cd_re, cd_im, lam_re, lam_im, d_full)
    x4, glu_val, glu_sg = _mm_nn("s5_out", s5_g, [W["s5_w_out", 0], W["s5_w_out", 0]], 1024, 1024, D, [x3],
                                 [F32, _MXU, _MXU], _glu_epi, cb_offsets=[0, 2])
    x5, ffn1 = ffn_fwd(1, x4)
    x6, ple1 = ple_fwd(1, x5)

    dx, d_norm_final, loss_rows = _loss_head(x6, norm_final[None], tgt)
    loss = lax.psum(loss_rows[0, 0], ("x", "y", "c"))

    G = {}
    small = {}

    def ple_bwd(i, dxo, saved):
        xin, hp, pi, pp, gt = saved
        dpre, dpp = _ple_bwd_elem(dxo, pp, gt)
        dwg = _mm_tn(f"ple_gate_dw{i}", hp, [dpre], "row", 512, 1024, 512, 1024)[0]
        dwp = _mm_tn(f"ple_proj_dw{i}", pi, [dpp], "col", 128, 512, 128, 512)[0]
        dhp = _mm_nt(f"ple_gate_dx{i}", [dpre], [W["ple_w_gate", i]], 512, 1024, [], [F32], _ident)[0]
        dxin, dg = _rms_bwd(f"rms_ple_bwd{i}", dhp, xin, norm_ple[i:i + 1], dxo)
        return dxin, dg, [dwg, dwp]

    def ffn_bwd(i, dxo, saved):
        xin, hf, a, b, f = saved
        da, db = _mm_nt(f"ffn_down_dx{i}", [dxo], [W["ffn_w2", i]], 1408, 1024, [a, b], [_MXU, _MXU],
                        _swiglu_bwd_epi)
        dw2 = _mm_tn(f"ffn_down_dw{i}", f, [dxo], "row", 1408, 1024, 1408, 1024)[0]
        dw1, dw3 = _mm_tn(f"ffn_up_dw{i}", hf, [da, db], "col", 1024, 1408, 512, 1408)
        dhf = _mm_nt(f"ffn_up_dx{i}", [da, db], [W["ffn_w1", i], W["ffn_w3", i]], 1024, 1408, [], [F32], _ident)[0]
        dxin, dg = _rms_bwd(f"rms_ffn_bwd{i}", dhf, xin, norm_ffn[i:i + 1], dxo)
        return dxin, dg, [dw1, dw3, dw2]

    d_norm_ple, d_norm_ffn, d_norm_mix = [None] * depth, [None] * depth, [None] * depth
    big_layers = {n: [None] * w[n].shape[0] for n in _BIG}

    def reduce_into(tag, names, layer, grads):
        red = _reduce_scatter(tag, grads)
        for n, r in zip(names, red):
            big_layers[n][layer] = r

    dx, d_norm_ple[1], gs = ple_bwd(1, dx, ple1)
    reduce_into("ple1", ["ple_w_gate", "ple_w_proj"], 1, gs)
    dx, d_norm_ffn[1], gs = ffn_bwd(1, dx, ffn1)
    reduce_into("ffn1", ["ffn_w1", "ffn_w3", "ffn_w2"], 1, gs)

    do = _glu_bwd_elem(dx, glu_val, glu_sg)
    dw_s5out = _mm_tn("s5_out_dw", s5_g, [do], "col", 1024, 1024, 512, 1024)[0]
    dgy = _mm_nt("s5_out_dx", [do], [W["s5_w_out", 0]], 1024, 1024, [], [F32], _ident)[0]
    du, dbd_re, dbd_im, dcd_re, dcd_im, dl_re, dl_im, dd = _s5_bwd(u, dgy, bd_re, bd_im, cd_re, cd_im,
                                                                  lam_re, lam_im, d_full)
    dw_s5in = _mm_tn("s5_in_dw", h1, [du], "row", 512, 1024, 512, 1024)[0]
    dh1 = _mm_nt("s5_in_dx", [du], [W["s5_w_in", 0]], 512, 1024, [], [F32], _ident)[0]
    dx, d_norm_mix[1] = _rms_bwd("rms_mix1_bwd", dh1, x3, norm_mix[1:2], dx)
    reduce_into("mix1", ["s5_w_in", "s5_w_out"], 0, [dw_s5in, dw_s5out])

    def from_bd(t):
        return jnp.transpose(_diag_blocks(t, T, _S5_C, _S5_P), (2, 0, 1, 3)).reshape(_S5_C, T * _S5_GT, _S5_P)

    def from_cdT(t):
        return _diag_blocks(t, T, _S5_C, _S5_P).reshape(T * _S5_GT, _S5_C, _S5_P)

    da_re, da_im, dlog_dt, dbT_re, dbT_im = _s5_prep_bwd(
        a_re, a_im, log_dt, bT_re, bT_im, dl_re.reshape(a_re.shape), dl_im.reshape(a_re.shape),
        from_bd(dbd_re), from_bd(dbd_im))
    small["s5_a_re"], small["s5_a_im"], small["s5_log_dt"] = da_re[None], da_im[None], dlog_dt.reshape(1, -1)
    small["s5_b_re"] = jnp.transpose(dbT_re, (1, 2, 0))[None]
    small["s5_b_im"] = jnp.transpose(dbT_im, (1, 2, 0))[None]
    small["s5_c_re"], small["s5_c_im"] = from_cdT(dcd_re)[None], from_cdT(dcd_im)[None]

    dx, d_norm_ple[0], gs = ple_bwd(0, dx, ple0)
    reduce_into("ple0", ["ple_w_gate", "ple_w_proj"], 0, gs)
    dx, d_norm_ffn[0], gs = ffn_bwd(0, dx, ffn0)
    reduce_into("ffn0", ["ffn_w1", "ffn_w3", "ffn_w2"], 0, gs)

    dw_gmout = _mm_tn("gm_out_dw", gm_m, [dx], "row", 512, 1024, 512, 1024)[0]
    dgm = _mm_nt("gm_out_dx", [dx], [W["gm_w_out", 0]], 512, 1024, [], [F32], _ident)[0]
    dz, dws, dbsT, dlng, dlnb = _gmlp_bwd(z, dgm, gm_ln_g, gm_ln_b, gm_w_s[0], bsT)
    dw_gmin = _mm_tn("gm_in_dw", h0, [dz], "col", 1024, 1024, 512, 1024)[0]
    dh0 = _mm_nt("gm_in_dx", [dz], [W["gm_w_in", 0]], 1024, 1024, [], [F32], _ident)[0]
    dx, d_norm_mix[0] = _rms_bwd("rms_mix0_bwd", dh0, xs, norm_mix[0:1], dx)
    reduce_into("mix0", ["gm_w_in", "gm_w_out"], 0, [dw_gmin, dw_gmout])
    grad_x = dx[None]

    small["norm_mix"], small["norm_ffn"] = jnp.concatenate(d_norm_mix), jnp.concatenate(d_norm_ffn)
    small["norm_ple"], small["norm_final"] = jnp.concatenate(d_norm_ple), d_norm_final[0]
    small["gm_ln_g"], small["gm_ln_b"], small["gm_w_s"] = dlng, dlnb, dws[None]
    small["gm_b_s"] = dbsT[:, :_GM_HEADS].T[None]
    small["s5_d"] = dd

    small_names = [n for n in _WEIGHT_NAMES if n not in _BIG]
    packed = _pack([small[n] for n in small_names], 64)
    summed = _allreduce_small("ar_small", packed.reshape(8, -1, _LANES))
    red_small = dict(zip(small_names, _unpack(summed, [small[n].shape for n in small_names])))
    red_small["s5_d"] = lax.dynamic_slice(red_small["s5_d"], (0, q * (D // 4)), (1, D // 4))

    grads, deltas, new_m, new_v = {}, {}, {}, {}
    for n, kind in _BIG.items():
        R, C = _rc(kind, w[n].shape)
        grads[n], deltas[n], new_m[n], new_v[n] = _adamw_big(f"adamw_{n}", w[n], mom[n], var[n], big_layers[n],
                                                             kind, R, C)
    shapes = [w[n].shape for n in small_names]
    dl, mo, vo = _adamw_flat(_pack([w[n] for n in small_names], 64), _pack([red_small[n] for n in small_names], 64),
                             _pack([mom[n] for n in small_names], 64), _pack([var[n] for n in small_names], 64))
    for n, g_, d_, m_, v_ in zip(small_names, [red_small[n] for n in small_names], _unpack(dl, shapes),
                                 _unpack(mo, shapes), _unpack(vo, shapes)):
        grads[n], deltas[n], new_m[n], new_v[n] = g_.reshape(w[n].shape), d_, m_, v_

    return (loss, grad_x, *[grads[n] for n in _WEIGHT_NAMES], *[deltas[n] for n in _WEIGHT_NAMES],
            *[new_m[n] for n in _WEIGHT_NAMES], *[new_v[n] for n in _WEIGHT_NAMES])
```

```python
import functools
import math

import jax
import jax.numpy as jnp
from jax import lax
from jax.experimental import pallas as pl
from jax.experimental.pallas import tpu as pltpu
from jax.experimental.pallas import tpu_sc as plsc

F32 = jnp.float32
_MXU = jnp.bfloat16
_WIRE = jnp.bfloat16
_EPS = 1e-6
_VMEM_LIMIT = 48 * 1024 * 1024
_LANES = 128
_MESH = pl.DeviceIdType.MESH

_LR, _B1, _B2, _AEPS, _WD, _STEP = 0.001, 0.9, 0.999, 1e-08, 0.01, 10

_GM_CHUNK = 128
_GM_HEADS = 16
_S5_GT = 8
_S5_P = 64
_S5_C = 16

_NN = (((1,), (0,)), ((), ()))
_NT = (((1,), (1,)), ((), ()))
_TN = (((0,), (0,)), ((), ()))


def _cparams(sem):
    return pltpu.CompilerParams(dimension_semantics=sem, vmem_limit_bytes=_VMEM_LIMIT)


def _sig(x):
    return 1.0 / (1.0 + jnp.exp(-x))


_GC = math.sqrt(2.0 / math.pi)


def _gelu(x):
    return 0.5 * x * (1.0 + jnp.tanh(_GC * (x + 0.044715 * (x * x * x))))


def _gelu_grad(x):
    t = jnp.tanh(_GC * (x + 0.044715 * (x * x * x)))
    return 0.5 * (1.0 + t) + 0.5 * x * (1.0 - t * t) * (_GC * (1.0 + 3.0 * 0.044715 * x * x))


def _dot(a, b, dn):
    return lax.dot_general(a.astype(_MXU), b.astype(_MXU), dn, preferred_element_type=F32)


class _W:
    def __init__(self, arr, kind):
        self.a, self.kind = arr, kind
        self.R, self.C = arr.shape[2], arr.shape[3]

    def full_shape(self):
        return (2 * self.R, 4 * self.C) if self.kind == "col" else (4 * self.R, 2 * self.C)


def _part_index(kind, R, C, tr, tc, rb, cb):
    nr, nc = R // tr, C // tc
    if kind == "col":
        return cb // nc, rb // nr, rb % nr, cb % nc
    return rb // nr, cb // nc, rb % nr, cb % nc


def _wspec(w, tr, tc, rb_fn, cb_fn):
    assert w.R % tr == 0 and w.C % tc == 0, (w.R, w.C, tr, tc)

    def imap(i, j, k):
        return _part_index(w.kind, w.R, w.C, tr, tc, rb_fn(i, j, k), cb_fn(i, j, k))

    return pl.BlockSpec((None, None, tr, tc), imap)


def _gspec(kind, R, C, tr, tc):
    assert R % tr == 0 and C % tc == 0, (R, C, tr, tc)

    def imap(i, j, k):
        part, half, rbi, cbi = _part_index(kind, R, C, tr, tc, i, j)
        return half, part, rbi, cbi

    return pl.BlockSpec((None, None, tr, tc), imap)


def _mm(name, grid, a_ops, b_ops, pairs, acc_shape, n_acc, extras, outs, epilogue):
    nk = grid[2]
    na, nb, ne, no = len(a_ops), len(b_ops), len(extras), len(outs)

    def body(*refs):
        a_refs = refs[:na]
        b_refs = refs[na:na + nb]
        e_refs = refs[na + nb:na + nb + ne]
        o_refs = refs[na + nb + ne:na + nb + ne + no]
        acc_refs = refs[na + nb + ne + no:]
        k = pl.program_id(2)

        def products():
            sums = [None] * n_acc
            for ai, bi, ci, dn in pairs:
                d = _dot(a_refs[ai][...], b_refs[bi][...], dn)
                sums[ci] = d if sums[ci] is None else sums[ci] + d
            return sums

        def finish(accs):
            res = epilogue(accs, [e[...] for e in e_refs])
            for o, r in zip(o_refs, res):
                o[...] = r.astype(o.dtype)

        if nk == 1:
            finish(products())
            return

        @pl.when(k == 0)
        def _():
            for acc, d in zip(acc_refs, products()):
                acc[...] = d

        @pl.when(jnp.logical_and(k > 0, k < nk - 1))
        def _():
            for acc, d in zip(acc_refs, products()):
                acc[...] += d

        @pl.when(k == nk - 1)
        def _():
            finish([acc[...] + d for acc, d in zip(acc_refs, products())])

    ops = list(a_ops) + list(b_ops) + list(extras)
    return pl.pallas_call(
        body, name=name, grid=grid,
        in_specs=[s for _, s in ops],
        out_specs=[s for _, s in outs],
        out_shape=[s for s, _ in outs],
        scratch_shapes=[pltpu.VMEM(acc_shape, F32) for _ in range(n_acc if nk > 1 else 0)],
        compiler_params=_cparams(("parallel", "parallel", "arbitrary")),
    )(*[a for a, _ in ops])


def _bs(shape, fn):
    return pl.BlockSpec(shape, fn)


def _tile_m(L):
    return min(L, 512)


def _mm_nn(name, x, ws, tk, tn, n_out, extras, outs_sd, epilogue, tm=None, cb_offsets=None):
    M, K = x.shape
    tm = tm or _tile_m(M)
    grid = (M // tm, n_out // tn, K // tk)
    a_ops = [(x, _bs((tm, tk), lambda i, j, k: (i, k)))]
    cb_offsets = cb_offsets or [0] * len(ws)
    b_ops = [(w.a, _wspec(w, tk, tn, lambda i, j, k: k, (lambda off: lambda i, j, k: j + off)(off)))
             for w, off in zip(ws, cb_offsets)]
    pairs = [(0, bi, bi, _NN) for bi in range(len(ws))]
    mn = _bs((tm, tn), lambda i, j, k: (i, j))
    ex = [(e, mn) for e in extras]
    outs = [(jax.ShapeDtypeStruct((M, n_out), dt), mn) for dt in outs_sd]
    return _mm(name, grid, a_ops, b_ops, pairs, (tm, tn), len(ws), ex, outs, epilogue)


def _mm_nt(name, xs, ws, tn, tk, extras, outs_sd, epilogue, tm=None):
    M, Nw = xs[0].shape
    Kw = ws[0].full_shape()[0]
    tm = tm or _tile_m(M)
    grid = (M // tm, Kw // tn, Nw // tk)
    a_ops = [(x, _bs((tm, tk), lambda i, j, k: (i, k))) for x in xs]
    b_ops = [(w.a, _wspec(w, tn, tk, lambda i, j, k: j, lambda i, j, k: k)) for w in ws]
    pairs = [(i, i, 0, _NT) for i in range(len(ws))]
    mn = _bs((tm, tn), lambda i, j, k: (i, j))
    ex = [(e, mn) for e in extras]
    outs = [(jax.ShapeDtypeStruct((M, Kw), dt), mn) for dt in outs_sd]
    return _mm(name, grid, a_ops, b_ops, pairs, (tm, tn), 1, ex, outs, epilogue)


def _mm_tn(name, x, dys, kind, R, C, tm, tn, tk=None):
    L, Kw = x.shape
    Nw = dys[0].shape[1]
    tk = tk or min(L, 1024)
    grid = (Kw // tm, Nw // tn, L // tk)
    a_ops = [(x, _bs((tk, tm), lambda i, j, k: (k, i)))]
    b_ops = [(dy, _bs((tk, tn), lambda i, j, k: (k, j))) for dy in dys]
    pairs = [(0, bi, bi, _TN) for bi in range(len(dys))]
    gs = _gspec(kind, R, C, tm, tn)
    outs = [(jax.ShapeDtypeStruct((2, 4, R, C), _WIRE), gs) for _ in dys]
    return _mm(name, grid, a_ops, b_ops, pairs, (tm, tn), len(dys), [], outs, lambda accs, ex: accs)


def _row_tile(L):
    return min(L, 256)


def _rowwise(name, body, ins, outs, L, acc_outs=()):
    tr = _row_tile(L)
    n_in, n_out = len(ins), len(outs)

    def kbody(*refs):
        i_refs, o_refs, a_refs = refs[:n_in], refs[n_in:n_in + n_out], refs[n_in + n_out:]
        res, sums = body(*[r[...] for r in i_refs])
        for o, r in zip(o_refs, res):
            o[...] = r.astype(o.dtype)
        if a_refs:
            @pl.when(pl.program_id(0) == 0)
            def _():
                for a in a_refs:
                    a[...] = jnp.zeros(a.shape, F32)
            for a, s in zip(a_refs, sums):
                a[...] += s

    in_specs = []
    for arr, kind in ins:
        if kind == "row":
            in_specs.append(pl.BlockSpec((tr, arr.shape[1]), lambda i: (i, 0)))
        else:
            in_specs.append(pl.BlockSpec(arr.shape, lambda i: (0, 0)))
    out_specs = [pl.BlockSpec((tr, c), lambda i: (i, 0)) for c, _ in outs]
    out_shape = [jax.ShapeDtypeStruct((L, c), dt) for c, dt in outs]
    out_specs += [pl.BlockSpec((1, c), lambda i: (0, 0)) for c in acc_outs]
    out_shape += [jax.ShapeDtypeStruct((1, c), F32) for c in acc_outs]
    return pl.pallas_call(
        kbody, name=name, grid=(L // tr,), in_specs=in_specs, out_specs=out_specs, out_shape=out_shape,
        compiler_params=_cparams(("arbitrary",)),
    )(*[a for a, _ in ins])


def _rms_fwd(name, x, g):
    def body(xv, gv):
        r = lax.rsqrt(jnp.mean(xv * xv, axis=-1, keepdims=True) + _EPS)
        return [xv * r * gv], []
    return _rowwise(name, body, [(x, "row"), (g, "vec")], [(x.shape[1], _MXU)], x.shape[0])[0]


def _rms_bwd(name, dh, x, g, dres):
    def body(dhv, xv, gv, dr):
        r = lax.rsqrt(jnp.mean(xv * xv, axis=-1, keepdims=True) + _EPS)
        xh = xv * r
        dxh = dhv * gv
        dx = r * (dxh - xh * jnp.mean(dxh * xh, axis=-1, keepdims=True))
        return [dr + dx], [jnp.sum(dhv * xh, axis=0, keepdims=True)]
    D = x.shape[1]
    return _rowwise(name, body, [(dh, "row"), (x, "row"), (g, "vec"), (dres, "row")], [(D, F32)], x.shape[0], [D])


def _loss_head(x, g, target):
    D = x.shape[1]

    def body(xv, gv, tv):
        r = lax.rsqrt(jnp.mean(xv * xv, axis=-1, keepdims=True) + _EPS)
        xh = xv * r
        e = xh * gv - tv
        dy = e * (1.0 / D)
        dxh = dy * gv
        dx = r * (dxh - xh * jnp.mean(dxh * xh, axis=-1, keepdims=True))
        row_loss = 0.5 * jnp.mean(e * e, axis=-1, keepdims=True)
        lsum = jnp.sum(row_loss, axis=0, keepdims=True) + jnp.zeros((1, _LANES), F32)
        return [dx], [jnp.sum(dy * xh, axis=0, keepdims=True), lsum]
    return _rowwise("loss_head", body, [(x, "row"), (g, "vec"), (target, "row")], [(D, F32)], x.shape[0], [D, _LANES])


def _ple_bwd_elem(dx, pp, gt):
    def body(dxv, ppv, gtv):
        gt32 = gtv.astype(F32)
        return [dxv * ppv * gt32 * (1.0 - gt32), dxv * gt32], []
    D = dx.shape[1]
    return _rowwise("ple_bwd_elem", body, [(dx, "row"), (pp, "row"), (gt, "row")], [(D, _MXU), (D, _MXU)], dx.shape[0])


def _glu_bwd_elem(dx, val, sg):
    def body(dxv, vv, sv):
        v32, s32 = vv.astype(F32), sv.astype(F32)
        return [jnp.concatenate([dxv * s32, dxv * v32 * s32 * (1.0 - s32)], axis=1)], []
    D = dx.shape[1]
    return _rowwise("glu_bwd_elem", body, [(dx, "row"), (val, "row"), (sg, "row")], [(2 * D, _MXU)], dx.shape[0])[0]


def _gm_common(z, ln_g, ln_b, wc_bf, bsT):
    W = z.shape[1] // 2
    zu, zv = z[:, :W], z[:, W:]
    u, v = _gelu(zu), _gelu(zv)
    mu = jnp.mean(v, axis=-1, keepdims=True)
    vc = v - mu
    rstd = lax.rsqrt(jnp.mean(vc * vc, axis=-1, keepdims=True) + _EPS)
    vh = vc * rstd
    vn = vh * ln_g + ln_b
    vnb = vn.astype(_MXU)
    svs = []
    for h in range(_GM_HEADS):
        sl = slice(h * _LANES, (h + 1) * _LANES)
        svs.append(_dot(wc_bf[h], vnb[:, sl], _NN) + bsT[:, h:h + 1])
    return zu, zv, u, vh, rstd, vnb, svs


def _causal(w):
    t = lax.broadcasted_iota(jnp.int32, w.shape, w.ndim - 2)
    s = lax.broadcasted_iota(jnp.int32, w.shape, w.ndim - 1)
    return jnp.where(s <= t, w, jnp.zeros_like(w))


def _gmlp_fwd(z, ln_g, ln_b, w_s, bsT):
    L, W2 = z.shape
    W = W2 // 2

    def body(z_ref, g_ref, b_ref, ws_ref, bs_ref, m_ref):
        wc = _causal(ws_ref[...]).astype(_MXU)
        _, _, u, _, _, _, svs = _gm_common(z_ref[...], g_ref[...], b_ref[...], wc, bs_ref[...])
        for h in range(_GM_HEADS):
            sl = slice(h * _LANES, (h + 1) * _LANES)
            m_ref[:, sl] = (u[:, sl] * svs[h]).astype(m_ref.dtype)

    return pl.pallas_call(
        body, name="gmlp_fwd", grid=(L // _GM_CHUNK,),
        in_specs=[pl.BlockSpec((_GM_CHUNK, W2), lambda n: (n, 0)),
                  pl.BlockSpec((1, W), lambda n: (0, 0)), pl.BlockSpec((1, W), lambda n: (0, 0)),
                  pl.BlockSpec(w_s.shape, lambda n: (0, 0, 0)), pl.BlockSpec(bsT.shape, lambda n: (0, 0))],
        out_specs=pl.BlockSpec((_GM_CHUNK, W), lambda n: (n, 0)),
        out_shape=jax.ShapeDtypeStruct((L, W), _MXU),
        compiler_params=_cparams(("arbitrary",)),
    )(z, ln_g, ln_b, w_s, bsT)


def _gmlp_bwd(z, dm, ln_g, ln_b, w_s, bsT):
    L, W2 = z.shape
    W = W2 // 2
    T = _GM_CHUNK

    def body(z_ref, dm_ref, g_ref, b_ref, ws_ref, bs_ref, dz_ref, dws_ref, dbs_ref, dg_ref, db_ref):
        @pl.when(pl.program_id(0) == 0)
        def _():
            dws_ref[...] = jnp.zeros(dws_ref.shape, F32)
            dbs_ref[...] = jnp.zeros(dbs_ref.shape, F32)
            dg_ref[...] = jnp.zeros(dg_ref.shape, F32)
            db_ref[...] = jnp.zeros(db_ref.shape, F32)

        wc = _causal(ws_ref[...]).astype(_MXU)
        ln_g_v = g_ref[...]
        zu, zv, u, vh, rstd, vnb, svs = _gm_common(z_ref[...], ln_g_v, b_ref[...], wc, bs_ref[...])
        dmv = dm_ref[...]
        lane = lax.broadcasted_iota(jnp.int32, (T, _LANES), 1)
        dbs = jnp.zeros((T, _LANES), F32)
        dvn_parts = []
        for h in range(_GM_HEADS):
            sl = slice(h * _LANES, (h + 1) * _LANES)
            dsv = dmv[:, sl] * u[:, sl]
            dz_ref[:, sl] = (dmv[:, sl] * svs[h] * _gelu_grad(zu[:, sl])).astype(dz_ref.dtype)
            dbs = dbs + jnp.where(lane == h, jnp.sum(dsv, axis=1, keepdims=True), 0.0)
            dsvb = dsv.astype(_MXU)
            dws_ref[h] += _dot(dsvb, vnb[:, sl], _NT)
            dvn_parts.append(_dot(wc[h], dsvb, _TN))
        dbs_ref[...] += dbs
        dvn = jnp.concatenate(dvn_parts, axis=1)
        dg_ref[...] += jnp.sum(dvn * vh, axis=0, keepdims=True)
        db_ref[...] += jnp.sum(dvn, axis=0, keepdims=True)
        dxh = dvn * ln_g_v
        dv = rstd * (dxh - jnp.mean(dxh, axis=-1, keepdims=True) - vh * jnp.mean(dxh * vh, axis=-1, keepdims=True))
        dz_ref[:, W:] = (dv * _gelu_grad(zv)).astype(dz_ref.dtype)

        @pl.when(pl.program_id(0) == pl.num_programs(0) - 1)
        def _():
            dws_ref[...] = _causal(dws_ref[...])

    return pl.pallas_call(
        body, name="gmlp_bwd", grid=(L // T,),
        in_specs=[pl.BlockSpec((T, W2), lambda n: (n, 0)), pl.BlockSpec((T, W), lambda n: (n, 0)),
                  pl.BlockSpec((1, W), lambda n: (0, 0)), pl.BlockSpec((1, W), lambda n: (0, 0)),
                  pl.BlockSpec(w_s.shape, lambda n: (0, 0, 0)), pl.BlockSpec(bsT.shape, lambda n: (0, 0))],
        out_specs=[pl.BlockSpec((T, W2), lambda n: (n, 0)),
                   pl.BlockSpec(w_s.shape, lambda n: (0, 0, 0)), pl.BlockSpec((T, _LANES), lambda n: (0, 0)),
                   pl.BlockSpec((1, W), lambda n: (0, 0)), pl.BlockSpec((1, W), lambda n: (0, 0))],
        out_shape=[jax.ShapeDtypeStruct((L, W2), _MXU), jax.ShapeDtypeStruct(w_s.shape, F32),
                   jax.ShapeDtypeStruct((T, _LANES), F32),
                   jax.ShapeDtypeStruct((1, W), F32), jax.ShapeDtypeStruct((1, W), F32)],
        compiler_params=_cparams(("arbitrary",)),
    )(z, dm, ln_g, ln_b, w_s, bsT)


def _s5_prep_math(a_re, a_im, log_dt):
    dt = jnp.exp(log_dt)
    xr, xi = a_re * dt, a_im * dt
    e = jnp.exp(xr)
    lbr, lbi = e * jnp.cos(xi), e * jnp.sin(xi)
    dn = a_re * a_re + a_im * a_im
    nr, ni = lbr - 1.0, lbi
    pr, pi = nr * a_re + ni * a_im, ni * a_re - nr * a_im
    return dt, lbr, lbi, dn, nr, ni, pr, pi


def _vm():
    return pl.BlockSpec(memory_space=pltpu.VMEM)


def _s5_prep(a_re, a_im, log_dt, bT_re, bT_im):
    def body(ar_ref, ai_ref, ld_ref, br_ref, bi_ref, lbr_ref, lbi_ref, Br_ref, Bi_ref):
        _, lbr, lbi, dn, _, _, pr, pi = _s5_prep_math(ar_ref[...], ai_ref[...], ld_ref[...])
        cr, ci = pr / dn, pi / dn
        lbr_ref[...] = lbr
        lbi_ref[...] = lbi
        br, bi = br_ref[...], bi_ref[...]
        Br_ref[...] = cr[None] * br - ci[None] * bi
        Bi_ref[...] = cr[None] * bi + ci[None] * br

    sd = jax.ShapeDtypeStruct
    return pl.pallas_call(
        body, name="s5_prep", in_specs=[_vm()] * 5, out_specs=[_vm()] * 4,
        out_shape=[sd(a_re.shape, F32), sd(a_re.shape, F32), sd(bT_re.shape, F32), sd(bT_re.shape, F32)],
    )(a_re, a_im, log_dt, bT_re, bT_im)


def _s5_prep_bwd(a_re, a_im, log_dt, bT_re, bT_im, dlbr_s, dlbi_s, dBr, dBi):
    def body(ar_ref, ai_ref, ld_ref, br_ref, bi_ref, dlr_ref, dli_ref, dBr_ref, dBi_ref,
             dar_ref, dai_ref, dld_ref, dbr_ref, dbi_ref):
        a_re_v, a_im_v = ar_ref[...], ai_ref[...]
        dt, lbr, lbi, dn, nr, ni, pr, pi = _s5_prep_math(a_re_v, a_im_v, ld_ref[...])
        cr, ci = pr / dn, pi / dn
        br, bi, dBr_v, dBi_v = br_ref[...], bi_ref[...], dBr_ref[...], dBi_ref[...]
        dbr_ref[...] = cr[None] * dBr_v + ci[None] * dBi_v
        dbi_ref[...] = cr[None] * dBi_v - ci[None] * dBr_v
        dcr = jnp.sum(br * dBr_v + bi * dBi_v, axis=0)
        dci = jnp.sum(br * dBi_v - bi * dBr_v, axis=0)
        dpr, dpi = dcr / dn, dci / dn
        ddn = -(dcr * pr + dci * pi) / (dn * dn)
        dnr = dpr * a_re_v - dpi * a_im_v
        dni = dpr * a_im_v + dpi * a_re_v
        dlbr = dlr_ref[...] + dnr
        dlbi = dli_ref[...] + dni
        dxr = dlbr * lbr + dlbi * lbi
        dxi = dlbi * lbr - dlbr * lbi
        dar_ref[...] = dpr * nr + dpi * ni + 2.0 * ddn * a_re_v + dxr * dt
        dai_ref[...] = dpr * ni - dpi * nr + 2.0 * ddn * a_im_v + dxi * dt
        dld_ref[...] = jnp.sum(dxr * a_re_v + dxi * a_im_v, axis=1, keepdims=True) * dt

    sd = jax.ShapeDtypeStruct
    return pl.pallas_call(
        body, name="s5_prep_bwd", in_specs=[_vm()] * 9, out_specs=[_vm()] * 5,
        out_shape=[sd(a_re.shape, F32), sd(a_re.shape, F32), sd(log_dt.shape, F32),
                   sd(bT_re.shape, F32), sd(bT_re.shape, F32)],
    )(a_re, a_im, log_dt, bT_re, bT_im, dlbr_s, dlbi_s, dBr, dBi)


def _shift_rows(v, down):
    n = v.shape[0]
    rolled = pltpu.roll(v, 1 if down else n - 1, 0)
    row = lax.broadcasted_iota(jnp.int32, v.shape, 0)
    return jnp.where(row == (0 if down else n - 1), 0.0, rolled)


def _cmul(ar, ai, br, bi):
    return ar * br - ai * bi, ar * bi + ai * br


_SEG = 8
_UNROLL = 8


def _seg_rows(k):
    if isinstance(k, int):
        return pl.ds(k * _SEG, _SEG)
    return pl.ds(pl.multiple_of(k * _SEG, _SEG), _SEG)


def _unrolled(n, step, init):
    main = n // _UNROLL

    def trip(kk, s):
        for uu in range(_UNROLL):
            s = step(kk * _UNROLL + uu, s)
        return s

    s = lax.fori_loop(0, main, trip, init)
    for r in range(main * _UNROLL, n):
        s = step(r, s)
    return s


def _interleave(src_ref, dst_ref, nk):
    def step(k, carry):
        dst_ref[_seg_rows(k), :] = src_ref[pl.ds(k, _SEG, stride=nk), :]
        return carry
    _unrolled(nk, step, 0)


def _deinterleave(src_ref, dst_ref, nk):
    def step(k, carry):
        dst_ref[pl.ds(k, _SEG, stride=nk), :] = src_ref[_seg_rows(k), :]
        return carry
    _unrolled(nk, step, 0)


def _segment_inits(er, ei, ar, ai, nk, down):
    pr, pi = ar, ai
    for _ in range(int(math.log2(nk))):
        pr, pi = _cmul(pr, pi, pr, pi)
    fr, fi = er, ei
    for _ in range(_SEG - 1):
        sr, si = _shift_rows(fr, down), _shift_rows(fi, down)
        mr, mi = _cmul(pr, pi, sr, si)
        fr, fi = er + mr, ei + mi
    return _shift_rows(fr, down), _shift_rows(fi, down)


def _scan_states(x_re, x_im, ar, ai, nk):
    lanes = ar.shape[1]

    def step(k, s):
        rows = _seg_rows(k)
        mr, mi = _cmul(ar, ai, s[0], s[1])
        return mr + x_re[rows, :], mi + x_im[rows, :]

    zero = jnp.zeros((_SEG, lanes), F32)
    er, ei = _unrolled(nk, step, (zero, zero))
    ir, ii = _segment_inits(er, ei, ar, ai, nk, True)

    def step2(k, s):
        rows = _seg_rows(k)
        mr, mi = _cmul(ar, ai, s[0], s[1])
        nr, ni = mr + x_re[rows, :], mi + x_im[rows, :]
        x_re[rows, :] = nr
        x_im[rows, :] = ni
        return nr, ni

    _unrolled(nk, step2, (ir, ii))


def _s5_tile_fwd(u, bd_re, bd_im, cd_re, cd_im, ar, ai, d, s_re, s_im, nk):
    s_re[...] = _dot(u, bd_re, _NN)
    s_im[...] = _dot(u, bd_im, _NN)
    _scan_states(s_re, s_im, ar, ai, nk)
    return _dot(s_re[...], cd_re, _NN) - _dot(s_im[...], cd_im, _NN) + d * u


def _s5_specs(L, T):
    lanes = _S5_GT * _S5_P
    u_spec = pl.BlockSpec((L, _LANES), lambda t: (0, t))
    bd_spec = pl.BlockSpec((None, _LANES, lanes), lambda t: (t, 0, 0))
    cd_spec = pl.BlockSpec((None, lanes, _LANES), lambda t: (t, 0, 0))
    lam_spec = pl.BlockSpec((None, 1, lanes), lambda t: (t, 0, 0))
    d_spec = pl.BlockSpec((1, _LANES), lambda t: (0, t))
    return lanes, u_spec, bd_spec, cd_spec, lam_spec, d_spec


def _s5_fwd(u, bd_re, bd_im, cd_re, cd_im, lam_re, lam_im, d):
    L, Wd = u.shape
    T = Wd // _LANES
    nk = L // _SEG
    lanes, u_spec, bd_spec, cd_spec, lam_spec, d_spec = _s5_specs(L, T)

    def body(u_ref, bdr, bdi, cdr, cdi, lr, li, d_ref, g_ref, s_re, s_im, up, tmp):
        ar = jnp.broadcast_to(lr[...], (_SEG, lanes))
        ai = jnp.broadcast_to(li[...], (_SEG, lanes))
        _interleave(u_ref, up, nk)
        y = _s5_tile_fwd(up[...], bdr[...], bdi[...], cdr[...], cdi[...], ar, ai, d_ref[...], s_re, s_im, nk)
        up[...] = _gelu(y)
        _deinterleave(up, tmp, nk)
        g_ref[...] = tmp[...].astype(g_ref.dtype)

    return pl.pallas_call(
        body, name="s5_fwd", grid=(T,),
        in_specs=[u_spec, bd_spec, bd_spec, cd_spec, cd_spec, lam_spec, lam_spec, d_spec],
        out_specs=u_spec, out_shape=jax.ShapeDtypeStruct((L, Wd), _MXU),
        scratch_shapes=[pltpu.VMEM((L, lanes), F32) for _ in range(2)]
        + [pltpu.VMEM((L, _LANES), F32) for _ in range(2)],
        compiler_params=_cparams(("arbitrary",)),
    )(u, bd_re, bd_im, cd_re, cd_im, lam_re, lam_im, d)


def _s5_bwd(u, dg, bd_re, bd_im, cd_re, cd_im, lam_re, lam_im, d):
    L, Wd = u.shape
    T = Wd // _LANES
    nk = L // _SEG
    lanes, u_spec, bd_spec, cd_spec, lam_spec, d_spec = _s5_specs(L, T)

    def body(u_ref, dg_ref, bdr, bdi, cdr, cdi, lr, li, d_ref,
             du_ref, dbdr, dbdi, dcdr, dcdi, dlr, dli, dd_ref, s_re, s_im, g_re, g_im, up, dgp, tmp):
        ar = jnp.broadcast_to(lr[...], (_SEG, lanes))
        ai = jnp.broadcast_to(li[...], (_SEG, lanes))
        _interleave(u_ref, up, nk)
        _interleave(dg_ref, dgp, nk)
        uv, dv = up[...], d_ref[...]
        y = _s5_tile_fwd(uv, bdr[...], bdi[...], cdr[...], cdi[...], ar, ai, dv, s_re, s_im, nk)
        dy = dgp[...] * _gelu_grad(y)
        dd_ref[...] = jnp.sum(dy * uv, axis=0, keepdims=True)
        dyb = dy.astype(_MXU)
        dcdr[...] = _dot(dyb, s_re[...], _TN)
        dcdi[...] = -_dot(dyb, s_im[...], _TN)
        g_re[...] = _dot(dyb, cdr[...], _NT)
        g_im[...] = -_dot(dyb, cdi[...], _NT)

        nai = -ai

        def step(j, s):
            rows = _seg_rows(nk - 1 - j)
            mr, mi = _cmul(ar, nai, s[0], s[1])
            return mr + g_re[rows, :], mi + g_im[rows, :]

        zero = jnp.zeros((_SEG, lanes), F32)
        er, ei = _unrolled(nk, step, (zero, zero))
        ir, ii = _segment_inits(er, ei, ar, nai, nk, False)

        def acc_lam(gr, gi, pr, pi, acc):
            return acc[0] + gr * pr + gi * pi, acc[1] + gi * pr - gr * pi

        def step2(j, carry):
            s, acc = carry
            k = nk - 1 - j
            rows = _seg_rows(k)
            mr, mi = _cmul(ar, nai, s[0], s[1])
            nr, ni = mr + g_re[rows, :], mi + g_im[rows, :]
            g_re[rows, :] = nr
            g_im[rows, :] = ni
            prev = _seg_rows(k - 1)
            return (nr, ni), acc_lam(nr, ni, s_re[prev, :], s_im[prev, :], acc)

        (g0r, g0i), acc = _unrolled(nk - 1, step2, ((ir, ii), (zero, zero)))
        first = _seg_rows(0)
        mr, mi = _cmul(ar, nai, g0r, g0i)
        nr, ni = mr + g_re[first, :], mi + g_im[first, :]
        g_re[first, :] = nr
        g_im[first, :] = ni
        last = _seg_rows(nk - 1)
        acc = acc_lam(nr, ni, _shift_rows(s_re[last, :], True), _shift_rows(s_im[last, :], True), acc)
        dlr[...] = jnp.sum(acc[0], axis=0, keepdims=True)
        dli[...] = jnp.sum(acc[1], axis=0, keepdims=True)

        gtr, gti = g_re[...].astype(_MXU), g_im[...].astype(_MXU)
        ub = uv.astype(_MXU)
        dbdr[...] = _dot(ub, gtr, _TN)
        dbdi[...] = _dot(ub, gti, _TN)
        dgp[...] = _dot(gtr, bdr[...], _NT) + _dot(gti, bdi[...], _NT) + dy * dv
        _deinterleave(dgp, tmp, nk)
        du_ref[...] = tmp[...].astype(du_ref.dtype)

    sd = jax.ShapeDtypeStruct
    big = sd((T, _LANES, lanes), F32)
    return pl.pallas_call(
        body, name="s5_bwd", grid=(T,),
        in_specs=[u_spec, u_spec, bd_spec, bd_spec, cd_spec, cd_spec, lam_spec, lam_spec, d_spec],
        out_specs=[u_spec, bd_spec, bd_spec, bd_spec, bd_spec, lam_spec, lam_spec, d_spec],
        out_shape=[sd((L, Wd), _MXU), big, big, big, big, sd((T, 1, lanes), F32), sd((T, 1, lanes), F32),
                   sd((1, Wd), F32)],
        scratch_shapes=[pltpu.VMEM((L, lanes), F32) for _ in range(4)]
        + [pltpu.VMEM((L, _LANES), F32) for _ in range(3)],
        compiler_params=_cparams(("arbitrary",)),
    )(u, dg, bd_re, bd_im, cd_re, cd_im, lam_re, lam_im, d)


def _half_tile(R):
    for t in (256, 352, 128):
        if R % t == 0:
            return t
    raise ValueError(R)


def _cast_shard(name, w, layer, kind, R, C):
    tr = _half_tile(R)
    nr = R // tr

    def body(w_ref, o_ref):
        o_ref[...] = w_ref[...].astype(o_ref.dtype)

    if kind == "col":
        in_map = lambda h, i: (layer, h * nr + i, 0)
    else:
        in_map = lambda h, i: (layer, i, h)
    return pl.pallas_call(
        body, name=name, grid=(2, nr), in_specs=[pl.BlockSpec((None, tr, C), in_map)],
        out_specs=pl.BlockSpec((None, tr, C), lambda h, i: (h, i, 0)),
        out_shape=jax.ShapeDtypeStruct((2, R, C), _WIRE),
        compiler_params=_cparams(("arbitrary", "arbitrary")),
    )(w)


def _adam_math(w, g, m, v):
    m2 = _B1 * m + (1.0 - _B1) * g
    v2 = _B2 * v + (1.0 - _B2) * (g * g)
    m_hat = m2 / (1.0 - _B1 ** _STEP)
    v_hat = v2 / (1.0 - _B2 ** _STEP)
    delta = -_LR * (m_hat / (jnp.sqrt(v_hat) + _AEPS) + _WD * w)
    return delta, m2, v2


def _adamw_big(name, w, m, v, grads, kind, R, C, c):
    nl = w.shape[0]
    tr = _half_tile(R)
    nr = R // tr

    def body(c_ref, *refs):
        w_ref, m_ref, v_ref = refs[:3]
        g_refs = refs[3:3 + 2 * nl]
        go_ref, d_ref, mo_ref, vo_ref = refs[3 + 2 * nl:]
        mine = pl.program_id(1) == c_ref[0]
        g = jnp.where(mine, g_refs[0][...], g_refs[1][...])
        for l in range(1, nl):
            gl = jnp.where(mine, g_refs[2 * l][...], g_refs[2 * l + 1][...])
            g = jnp.where(pl.program_id(0) == l, gl, g)
        delta, m2, v2 = _adam_math(w_ref[...], g, m_ref[...], v_ref[...])
        go_ref[...] = g
        d_ref[...] = delta
        mo_ref[...] = m2
        vo_ref[...] = v2

    if kind == "col":
        nat = pl.BlockSpec((None, tr, C), lambda l, h, i, c_ref: (l, h * nr + i, 0))
    else:
        nat = pl.BlockSpec((None, tr, C), lambda l, h, i, c_ref: (l, i, h))

    def gspec(l0, own):
        def imap(l, h, i, c_ref):
            on = jnp.logical_and(l == l0, (h == c_ref[0]) == own)
            return jnp.where(on, i, 0), 0
        return pl.BlockSpec((tr, C), imap)

    gs = pltpu.PrefetchScalarGridSpec(
        num_scalar_prefetch=1, grid=(nl, 2, nr),
        in_specs=[nat, nat, nat] + [gspec(l, own) for l in range(nl) for own in (True, False)],
        out_specs=[nat] * 4)
    sd = jax.ShapeDtypeStruct(w.shape, F32)
    flat = [g for pair in grads for g in pair]
    return pl.pallas_call(
        body, name=name, grid_spec=gs, out_shape=[sd] * 4,
        compiler_params=_cparams(("arbitrary", "arbitrary", "arbitrary")),
    )(c.astype(jnp.int32).reshape(1), w, m, v, *flat)


def _adamw_flat(w, g, m, v):
    rows = w.shape[0]
    tr = rows // 8 if rows % 64 == 0 else rows

    def body(w_ref, g_ref, m_ref, v_ref, d_ref, mo_ref, vo_ref):
        delta, m2, v2 = _adam_math(w_ref[...], g_ref[...], m_ref[...], v_ref[...])
        d_ref[...] = delta
        mo_ref[...] = m2
        vo_ref[...] = v2

    spec = pl.BlockSpec((tr, _LANES), lambda i: (i, 0))
    sd = jax.ShapeDtypeStruct(w.shape, F32)
    return pl.pallas_call(
        body, name="adamw_small", grid=(rows // tr,), in_specs=[spec] * 4, out_specs=[spec] * 3,
        out_shape=[sd] * 3, compiler_params=_cparams(("arbitrary",)),
    )(w, g, m, v)


def _place():
    x, y, c = lax.axis_index("x"), lax.axis_index("y"), lax.axis_index("c")
    chips = [(1 - x, y), (x, 1 - y), (1 - x, 1 - y)]
    return x, y, c, 2 * x + y, chips


def _any():
    return pl.BlockSpec(memory_space=pl.ANY)


def _remote(src, dst, ssem, rsem, dev):
    return pltpu.make_async_remote_copy(src_ref=src, dst_ref=dst, send_sem=ssem, recv_sem=rsem,
                                        device_id=dev, device_id_type=_MESH)


def _allgather(name, shards):
    n = len(shards)

    def body(*refs):
        s_refs, g_refs = refs[:n], refs[n:2 * n]
        send0, recv0, send1, recv1, send2, recv2 = refs[2 * n:]
        x, y, c, q, chips = _place()
        sib = (x, y, 1 - c)
        _handshake([sib] + [(rx, ry, c) for rx, ry in chips])
        own = [_remote(s_refs[a], g_refs[a].at[q], send0.at[a], recv0.at[a], sib) for a in range(n)]
        first, passed = [], []
        for a in range(n):
            for j, (rx, ry) in enumerate(chips):
                k = 3 * a + j
                first.append(_remote(s_refs[a].at[c], g_refs[a].at[q, c], send1.at[k], recv1.at[k], (rx, ry, c)))
                land = g_refs[a].at[2 * rx + ry, c]
                passed.append(_remote(land, land, send2.at[k], recv2.at[k], sib))
        for cp in first + own:
            cp.start()
        for a in range(n):
            for j, (rx, ry) in enumerate(chips):
                k = 3 * a + j
                land = g_refs[a].at[2 * rx + ry, c]
                _remote(land, land, send1.at[k], recv1.at[k], (rx, ry, c)).wait_recv()
                passed[k].start()
        for a in range(n):
            for j, (rx, ry) in enumerate(chips):
                k = 3 * a + j
                other = g_refs[a].at[2 * rx + ry, 1 - c]
                _remote(other, other, send2.at[k], recv2.at[k], sib).wait_recv()
        for cp in own:
            cp.wait()
        for cp in first + passed:
            cp.wait_send()

    return _sequencer(name, _ID_BOTH, body, shards,
                      [jax.ShapeDtypeStruct((4,) + s.shape, s.dtype) for s in shards], [n, n] + [3 * n] * 4)


def _handshake(peers):
    barrier = pltpu.get_barrier_semaphore()
    for peer in peers:
        pl.semaphore_signal(barrier, inc=1, device_id=peer, device_id_type=_MESH)
    pl.semaphore_wait(barrier, len(peers))


_ID_SIBLING, _ID_CHIPS, _ID_BOTH = 1, 2, 3


def _sequencer(name, collective_id, body, ins, out_types, sem_counts):
    mesh = plsc.ScalarSubcoreMesh(axis_name="seq", num_cores=1)
    return pl.kernel(
        body, name=name, out_type=out_types, mesh=mesh,
        scratch_types=[pltpu.SemaphoreType.DMA((k,)) for k in sem_counts],
        compiler_params=pltpu.CompilerParams(collective_id=collective_id),
    )(*ins)


def _swap_halves(name, grads):
    n = len(grads)

    def body(*refs):
        g_refs, t_refs = refs[:n], refs[n:2 * n]
        send, recv = refs[2 * n:]
        x, y, c, _, _ = _place()
        _handshake([(x, y, 1 - c)])
        cps = [_remote(g_refs[a].at[1 - c], t_refs[a], send.at[a], recv.at[a], (x, y, 1 - c)) for a in range(n)]
        for cp in cps:
            cp.start()
        for cp in cps:
            cp.wait()

    return _sequencer(name, _ID_SIBLING, body, grads,
                      [jax.ShapeDtypeStruct(g.shape[1:], g.dtype) for g in grads], [n, n])


def _chip_sum(name, g, t):
    _, _, R, C = g.shape
    tr = _half_tile(R)

    def body(c_ref, g_ref, t_ref, o_ref):
        o_ref[...] = (g_ref[...].astype(F32) + t_ref[...].astype(F32)).astype(o_ref.dtype)

    c = lax.axis_index("c").astype(jnp.int32).reshape(1)
    gs = pltpu.PrefetchScalarGridSpec(
        num_scalar_prefetch=1, grid=(4, R // tr),
        in_specs=[pl.BlockSpec((None, None, tr, C), lambda r, i, c_ref: (c_ref[0], r, i, 0)),
                  pl.BlockSpec((None, tr, C), lambda r, i, c_ref: (r, i, 0))],
        out_specs=pl.BlockSpec((None, tr, C), lambda r, i, c_ref: (r, i, 0)))
    return pl.pallas_call(
        body, name=name, grid_spec=gs, out_shape=jax.ShapeDtypeStruct((4, R, C), _WIRE),
        compiler_params=_cparams(("arbitrary", "arbitrary")),
    )(c, g, t)


def _scatter_parts(name, parts):
    n = len(parts)

    def body(*refs):
        p_refs, t_refs = refs[:n], refs[n:2 * n]
        send, recv = refs[2 * n:]
        x, y, c, q, chips = _place()
        _handshake([(rx, ry, c) for rx, ry in chips])
        cps = []
        for a in range(n):
            for j, (rx, ry) in enumerate(chips):
                k = 3 * a + j
                cps.append(_remote(p_refs[a].at[2 * rx + ry], t_refs[a].at[q], send.at[k], recv.at[k], (rx, ry, c)))
        for cp in cps:
            cp.start()
        for a in range(n):
            for j, (rx, ry) in enumerate(chips):
                k = 3 * a + j
                land = t_refs[a].at[2 * rx + ry]
                _remote(land, land, send.at[k], recv.at[k], (rx, ry, c)).wait_recv()
        for cp in cps:
            cp.wait_send()

    return _sequencer(name, _ID_CHIPS, body, parts,
                      [jax.ShapeDtypeStruct(p.shape, p.dtype) for p in parts], [3 * n, 3 * n])


def _sum_parts(name, p, t, where):
    _, R, C = t.shape
    tr = _half_tile(R)

    def body(w_ref, p_ref, t0_ref, t1_ref, t2_ref, o_ref):
        o_ref[...] = (p_ref[...].astype(F32) + t0_ref[...].astype(F32)
                      + t1_ref[...].astype(F32) + t2_ref[...].astype(F32))

    def part(slot):
        return pl.BlockSpec((None, tr, C), lambda i, w_ref: (w_ref[slot], i, 0))

    gs = pltpu.PrefetchScalarGridSpec(
        num_scalar_prefetch=1, grid=(R // tr,), in_specs=[part(0), part(1), part(2), part(3)],
        out_specs=pl.BlockSpec((tr, C), lambda i, w_ref: (i, 0)))
    return pl.pallas_call(
        body, name=name, grid_spec=gs, out_shape=jax.ShapeDtypeStruct((R, C), F32),
        compiler_params=_cparams(("arbitrary",)),
    )(where, p, t, t, t)


def _send_halves(name, halves):
    n = len(halves)

    def body(*refs):
        h_refs, o_refs = refs[:n], refs[n:2 * n]
        send, recv = refs[2 * n:]
        x, y, c, _, _ = _place()
        _handshake([(x, y, 1 - c)])
        cps = [_remote(h_refs[a], o_refs[a], send.at[a], recv.at[a], (x, y, 1 - c)) for a in range(n)]
        for cp in cps:
            cp.start()
        for cp in cps:
            cp.wait()

    return _sequencer(name, _ID_SIBLING, body, halves,
                      [jax.ShapeDtypeStruct(h.shape, h.dtype) for h in halves], [n, n])


def _reduce_scatter(tag, grads):
    x, y, c, q, chips = _place()
    where = jnp.stack([q] + [2 * rx + ry for rx, ry in chips]).astype(jnp.int32)
    swapped = _swap_halves(f"rs_swap_{tag}", grads)
    parts = [_chip_sum(f"rs_chipsum_{tag}_{a}", g, t) for a, (g, t) in enumerate(zip(grads, swapped))]
    landed = _scatter_parts(f"rs_scatter_{tag}", parts)
    halves = [_sum_parts(f"rs_sum_{tag}_{a}", p, t, where) for a, (p, t) in enumerate(zip(parts, landed))]
    return list(zip(halves, _send_halves(f"rs_join_{tag}", halves)))


def _allreduce_small(name, v):
    _, R, _ = v.shape

    def body(v_ref, o_ref, land, acc, send1, recv1, send2, recv2):
        x, y, c = lax.axis_index("x"), lax.axis_index("y"), lax.axis_index("c")
        me = 4 * x + 2 * y + c
        peers = []
        for k in range(1, 8):
            dx, dy, dc = (k >> 2) & 1, (k >> 1) & 1, k & 1
            px, py, pc = (1 - x if dx else x), (1 - y if dy else y), (1 - c if dc else c)
            peers.append((k, (px, py, pc), 4 * px + 2 * py + pc))
        land[me] = v_ref[me]
        out1 = [_remote(v_ref.at[pid], land.at[me], send1.at[k], recv1.at[k], dev) for k, dev, pid in peers]
        for cp in out1:
            cp.start()
        for k, dev, pid in peers:
            _remote(land.at[pid], land.at[pid], send1.at[k], recv1.at[k], dev).wait_recv()
        total = land[0]
        for j in range(1, 8):
            total = total + land[j]
        acc[...] = total
        o_ref[me] = total
        out2 = [_remote(acc, o_ref.at[me], send2.at[k], recv2.at[k], dev) for k, dev, pid in peers]
        for cp in out2:
            cp.start()
        for k, dev, pid in peers:
            _remote(o_ref.at[pid], o_ref.at[pid], send2.at[k], recv2.at[k], dev).wait_recv()
        for cp in out1 + out2:
            cp.wait_send()

    return pl.pallas_call(
        body, name=name, in_specs=[_vm()], out_specs=_vm(),
        out_shape=jax.ShapeDtypeStruct(v.shape, F32),
        scratch_shapes=[pltpu.VMEM(v.shape, F32), pltpu.VMEM((R, _LANES), F32)]
        + [pltpu.SemaphoreType.DMA((8,)) for _ in range(4)],
        compiler_params=pltpu.CompilerParams(vmem_limit_bytes=_VMEM_LIMIT),
    )(v)


_WEIGHT_NAMES = ['norm_mix', 'norm_ffn', 'norm_ple', 'norm_final', 'gm_w_in', 'gm_ln_g', 'gm_ln_b', 'gm_w_s',
                 'gm_b_s', 'gm_w_out', 's5_w_in', 's5_a_re', 's5_a_im', 's5_log_dt', 's5_b_re', 's5_b_im',
                 's5_c_re', 's5_c_im', 's5_d', 's5_w_out', 'ffn_w1', 'ffn_w3', 'ffn_w2', 'ple_w_gate', 'ple_w_proj']
_BIG = {'gm_w_in': 'col', 'gm_w_out': 'row', 's5_w_in': 'row', 's5_w_out': 'col', 'ffn_w1': 'col',
        'ffn_w3': 'col', 'ffn_w2': 'row', 'ple_w_gate': 'row', 'ple_w_proj': 'col'}


def _rc(kind, shard_shape):
    rows, cols = shard_shape[-2:]
    return (rows // 2, cols) if kind == "col" else (rows, cols // 2)


def _pack(vecs, rows_multiple):
    flat = jnp.concatenate([a.reshape(-1).astype(F32) for a in vecs])
    unit = rows_multiple * _LANES
    pad = (-flat.shape[0]) % unit
    return jnp.pad(flat, (0, pad)).reshape(-1, _LANES)


def _unpack(buf, shapes):
    flat = buf.reshape(-1)
    out, off = [], 0
    for s in shapes:
        n = math.prod(s)
        out.append(flat[off:off + n].reshape(s))
        off += n
    return out


def _ident(accs, ex):
    return accs


def _add_resid(accs, ex):
    return [accs[0] + ex[0]]


def _swiglu_epi(accs, ex):
    a, b = accs
    return [a, b, a * _sig(a) * b]


def _swiglu_bwd_epi(accs, ex):
    df = accs[0]
    a, b = ex[0].astype(F32), ex[1].astype(F32)
    sa = _sig(a)
    return [df * b * (sa * (1.0 + a * (1.0 - sa))), df * (a * sa)]


def _ple_epi(accs, ex):
    gt = _sig(accs[0])
    return [ex[0] + gt * ex[1], gt]


def _glu_epi(accs, ex):
    val, sg = accs[0], _sig(accs[1])
    return [ex[0] + val * sg, val, sg]


def _block_diag(t):
    eye = jnp.eye(_S5_GT, dtype=t.dtype)
    return t[:, :, :, None, :] * eye[None, :, None, :, None]


def _diag_blocks(t, T, a, b):
    d = jnp.diagonal(t.reshape(T, _S5_GT, a, _S5_GT, b), axis1=1, axis2=3)
    return jnp.transpose(d, (0, 3, 1, 2))


def kernel(x, p, norm_mix, norm_ffn, norm_ple, norm_final, gm_w_in, gm_ln_g, gm_ln_b, gm_w_s, gm_b_s, gm_w_out, s5_w_in, s5_a_re, s5_a_im, s5_log_dt, s5_b_re, s5_b_im, s5_c_re, s5_c_im, s5_d, s5_w_out, ffn_w1, ffn_w3, ffn_w2, ple_w_gate, ple_w_proj, loss_target, m_norm_mix, m_norm_ffn, m_norm_ple, m_norm_final, m_gm_w_in, m_gm_ln_g, m_gm_ln_b, m_gm_w_s, m_gm_b_s, m_gm_w_out, m_s5_w_in, m_s5_a_re, m_s5_a_im, m_s5_log_dt, m_s5_b_re, m_s5_b_im, m_s5_c_re, m_s5_c_im, m_s5_d, m_s5_w_out, m_ffn_w1, m_ffn_w3, m_ffn_w2, m_ple_w_gate, m_ple_w_proj, v_norm_mix, v_norm_ffn, v_norm_ple, v_norm_final, v_gm_w_in, v_gm_ln_g, v_gm_ln_b, v_gm_w_s, v_gm_b_s, v_gm_w_out, v_s5_w_in, v_s5_a_re, v_s5_a_im, v_s5_log_dt, v_s5_b_re, v_s5_b_im, v_s5_c_re, v_s5_c_im, v_s5_d, v_s5_w_out, v_ffn_w1, v_ffn_w3, v_ffn_w2, v_ple_w_gate, v_ple_w_proj):
    env = dict(locals())
    w = {n: env[n] for n in _WEIGHT_NAMES}
    mom = {n: env["m_" + n] for n in _WEIGHT_NAMES}
    var = {n: env["v_" + n] for n in _WEIGHT_NAMES}
    xs, tgt = x[0], loss_target[0]
    L, D = xs.shape
    depth = norm_mix.shape[0]
    qx, qy = lax.axis_index("x"), lax.axis_index("y")
    q = 2 * qx + qy

    def gather(tag, items):
        shards = []
        for name, layer in items:
            kind = _BIG[name]
            R, C = _rc(kind, w[name].shape)
            shards.append(_cast_shard(f"cast_{name}{layer}", w[name], layer, kind, R, C))
        full = _allgather(f"ag_{tag}", shards)
        return {it: _W(f, _BIG[it[0]]) for it, f in zip(items, full)}

    W = {}
    mixers = [[("gm_w_in", 0), ("gm_w_out", 0)], [("s5_w_in", 0), ("s5_w_out", 0)]]
    for i in range(depth):
        W.update(gather(f"mix{i}", mixers[i]))
        W.update(gather(f"ffn{i}", [("ffn_w1", i), ("ffn_w3", i), ("ffn_w2", i)]))
        W.update(gather(f"ple{i}", [("ple_w_gate", i), ("ple_w_proj", i)]))

    d_slots = jnp.zeros((4, D // 4), F32)
    d_slots = lax.dynamic_update_slice(d_slots, s5_d.astype(F32), (q, 0))
    d_sum = _allreduce_small("ar_s5_d", _pack([d_slots], 64).reshape(8, -1, _LANES))
    d_full = (d_sum.reshape(-1)[:D] * 0.5).reshape(1, D)

    def ffn_fwd(i, xin):
        hf = _rms_fwd(f"rms_ffn{i}", xin, norm_ffn[i:i + 1])
        a, b, f = _mm_nn(f"ffn_up{i}", hf, [W["ffn_w1", i], W["ffn_w3", i]], 1024, 1408, ffn_w2.shape[1] * 4,
                         [], [_MXU, _MXU, _MXU], _swiglu_epi)
        xo = _mm_nn(f"ffn_down{i}", f, [W["ffn_w2", i]], 1408, 1024, D, [xin], [F32], _add_resid)[0]
        return xo, (xin, hf, a, b, f)

    def ple_fwd(i, xin):
        hp = _rms_fwd(f"rms_ple{i}", xin, norm_ple[i:i + 1])
        pi = p[i, 0]
        pp = _mm_nn(f"ple_proj{i}", pi, [W["ple_w_proj", i]], 128, 512, D, [], [F32], _ident)[0]
        xo, gt = _mm_nn(f"ple_gate{i}", hp, [W["ple_w_gate", i]], 512, 1024, D, [xin, pp], [F32, _MXU], _ple_epi)
        return xo, (xin, hp, pi, pp, gt)

    h0 = _rms_fwd("rms_mix0", xs, norm_mix[0:1])
    z = _mm_nn("gm_in", h0, [W["gm_w_in", 0]], 1024, 1024, 2 * D, [], [F32], _ident)[0]
    bsT = gm_b_s[0].T
    gm_m = _gmlp_fwd(z, gm_ln_g, gm_ln_b, gm_w_s[0], bsT)
    x1 = _mm_nn("gm_out", gm_m, [W["gm_w_out", 0]], 512, 1024, D, [xs], [F32], _add_resid)[0]
    x2, ffn0 = ffn_fwd(0, x1)
    x3, ple0 = ple_fwd(0, x2)

    T = D // _LANES
    lanes = _S5_GT * _S5_P
    a_re, a_im, log_dt = s5_a_re[0], s5_a_im[0], s5_log_dt[0][:, None]
    bT_re, bT_im = jnp.transpose(s5_b_re[0], (2, 0, 1)), jnp.transpose(s5_b_im[0], (2, 0, 1))
    lbr, lbi, BrT, BiT = _s5_prep(a_re, a_im, log_dt, bT_re, bT_im)

    def to_bd(BT):
        t = jnp.transpose(BT.reshape(_S5_C, T, _S5_GT, _S5_P), (1, 2, 0, 3))
        return _block_diag(t).reshape(T, _LANES, lanes).astype(_MXU)

    def to_cd(cw):
        t = jnp.transpose(cw.reshape(T, _S5_GT, _S5_C, _S5_P), (0, 1, 3, 2))
        return _block_diag(t).reshape(T, lanes, _LANES).astype(_MXU)

    bd_re, bd_im = to_bd(BrT), to_bd(BiT)
    cd_re, cd_im = to_cd(s5_c_re[0].astype(F32)), to_cd(s5_c_im[0].astype(F32))
    lam_re, lam_im = lbr.reshape(T, 1, lanes), lbi.reshape(T, 1, lanes)

    h1 = _rms_fwd("rms_mix1", x3, norm_mix[1:2])
    u = _mm_nn("s5_in", h1, [W["s5_w_in", 0]], 512, 1024, D, [], [F32], _ident)[0]
    s5_g = _s5_fwd(u, bd_re, bd_im, cd_re, cd_im, lam_re, lam_im, d_full)
    x4, glu_val, glu_sg = _mm_nn("s5_out", s5_g, [W["s5_w_out", 0], W["s5_w_out", 0]], 1024, 1024, D, [x3],
                                 [F32, _MXU, _MXU], _glu_epi, cb_offsets=[0, 2])
    x5, ffn1 = ffn_fwd(1, x4)
    x6, ple1 = ple_fwd(1, x5)

    dx, d_norm_final, loss_rows = _loss_head(x6, norm_final[None], tgt)
    loss = lax.psum(loss_rows[0, 0], ("x", "y", "c"))

    G = {}
    small = {}

    def ple_bwd(i, dxo, saved):
        xin, hp, pi, pp, gt = saved
        dpre, dpp = _ple_bwd_elem(dxo, pp, gt)
        dwg = _mm_tn(f"ple_gate_dw{i}", hp, [dpre], "row", 512, 1024, 512, 1024)[0]
        dwp = _mm_tn(f"ple_proj_dw{i}", pi, [dpp], "col", 128, 512, 128, 512)[0]
        dhp = _mm_nt(f"ple_gate_dx{i}", [dpre], [W["ple_w_gate", i]], 512, 1024, [], [F32], _ident)[0]
        dxin, dg = _rms_bwd(f"rms_ple_bwd{i}", dhp, xin, norm_ple[i:i + 1], dxo)
        return dxin, dg, [dwg, dwp]

    def ffn_bwd(i, dxo, saved):
        xin, hf, a, b, f = saved
        da, db = _mm_nt(f"ffn_down_dx{i}", [dxo], [W["ffn_w2", i]], 1408, 1024, [a, b], [_MXU, _MXU],
                        _swiglu_bwd_epi)
        dw2 = _mm_tn(f"ffn_down_dw{i}", f, [dxo], "row", 1408, 1024, 1408, 1024)[0]
        dw1, dw3 = _mm_tn(f"ffn_up_dw{i}", hf, [da, db], "col", 1024, 1408, 512, 1408)
        dhf = _mm_nt(f"ffn_up_dx{i}", [da, db], [W["ffn_w1", i], W["ffn_w3", i]], 1024, 1408, [], [F32], _ident)[0]
        dxin, dg = _rms_bwd(f"rms_ffn_bwd{i}", dhf, xin, norm_ffn[i:i + 1], dxo)
        return dxin, dg, [dw1, dw3, dw2]

    d_norm_ple, d_norm_ffn, d_norm_mix = [None] * depth, [None] * depth, [None] * depth
    big_layers = {n: [None] * w[n].shape[0] for n in _BIG}

    def reduce_into(tag, names, layer, grads):
        red = _reduce_scatter(tag, grads)
        for n, r in zip(names, red):
            big_layers[n][layer] = r

    dx, d_norm_ple[1], gs = ple_bwd(1, dx, ple1)
    reduce_into("ple1", ["ple_w_gate", "ple_w_proj"], 1, gs)
    dx, d_norm_ffn[1], gs = ffn_bwd(1, dx, ffn1)
    reduce_into("ffn1", ["ffn_w1", "ffn_w3", "ffn_w2"], 1, gs)

    do = _glu_bwd_elem(dx, glu_val, glu_sg)
    dw_s5out = _mm_tn("s5_out_dw", s5_g, [do], "col", 1024, 1024, 512, 1024)[0]
    dgy = _mm_nt("s5_out_dx", [do], [W["s5_w_out", 0]], 1024, 1024, [], [F32], _ident)[0]
    du, dbd_re, dbd_im, dcd_re, dcd_im, dl_re, dl_im, dd = _s5_bwd(u, dgy, bd_re, bd_im, cd_re, cd_im,
                                                                  lam_re, lam_im, d_full)
    dw_s5in = _mm_tn("s5_in_dw", h1, [du], "row", 512, 1024, 512, 1024)[0]
    dh1 = _mm_nt("s5_in_dx", [du], [W["s5_w_in", 0]], 512, 1024, [], [F32], _ident)[0]
    dx, d_norm_mix[1] = _rms_bwd("rms_mix1_bwd", dh1, x3, norm_mix[1:2], dx)
    reduce_into("mix1", ["s5_w_in", "s5_w_out"], 0, [dw_s5in, dw_s5out])

    def from_bd(t):
        return jnp.transpose(_diag_blocks(t, T, _S5_C, _S5_P), (2, 0, 1, 3)).reshape(_S5_C, T * _S5_GT, _S5_P)

    def from_cdT(t):
        return _diag_blocks(t, T, _S5_C, _S5_P).reshape(T * _S5_GT, _S5_C, _S5_P)

    da_re, da_im, dlog_dt, dbT_re, dbT_im = _s5_prep_bwd(
        a_re, a_im, log_dt, bT_re, bT_im, dl_re.reshape(a_re.shape), dl_im.reshape(a_re.shape),
        from_bd(dbd_re), from_bd(dbd_im))
    small["s5_a_re"], small["s5_a_im"], small["s5_log_dt"] = da_re[None], da_im[None], dlog_dt.reshape(1, -1)
    small["s5_b_re"] = jnp.transpose(dbT_re, (1, 2, 0))[None]
    small["s5_b_im"] = jnp.transpose(dbT_im, (1, 2, 0))[None]
    small["s5_c_re"], small["s5_c_im"] = from_cdT(dcd_re)[None], from_cdT(dcd_im)[None]

    dx, d_norm_ple[0], gs = ple_bwd(0, dx, ple0)
    reduce_into("ple0", ["ple_w_gate", "ple_w_proj"], 0, gs)
    dx, d_norm_ffn[0], gs = ffn_bwd(0, dx, ffn0)
    reduce_into("ffn0", ["ffn_w1", "ffn_w3", "ffn_w2"], 0, gs)

    dw_gmout = _mm_tn("gm_out_dw", gm_m, [dx], "row", 512, 1024, 512, 1024)[0]
    dgm = _mm_nt("gm_out_dx", [dx], [W["gm_w_out", 0]], 512, 1024, [], [F32], _ident)[0]
    dz, dws, dbsT, dlng, dlnb = _gmlp_bwd(z, dgm, gm_ln_g, gm_ln_b, gm_w_s[0], bsT)
    dw_gmin = _mm_tn("gm_in_dw", h0, [dz], "col", 1024, 1024, 512, 1024)[0]
    dh0 = _mm_nt("gm_in_dx", [dz], [W["gm_w_in", 0]], 1024, 1024, [], [F32], _ident)[0]
    dx, d_norm_mix[0] = _rms_bwd("rms_mix0_bwd", dh0, xs, norm_mix[0:1], dx)
    reduce_into("mix0", ["gm_w_in", "gm_w_out"], 0, [dw_gmin, dw_gmout])
    grad_x = dx[None]

    small["norm_mix"], small["norm_ffn"] = jnp.concatenate(d_norm_mix), jnp.concatenate(d_norm_ffn)
    small["norm_ple"], small["norm_final"] = jnp.concatenate(d_norm_ple), d_norm_final[0]
    small["gm_ln_g"], small["gm_ln_b"], small["gm_w_s"] = dlng, dlnb, dws[None]
    small["gm_b_s"] = dbsT[:, :_GM_HEADS].T[None]
    small["s5_d"] = dd

    small_names = [n for n in _WEIGHT_NAMES if n not in _BIG]
    packed = _pack([small[n] for n in small_names], 64)
    summed = _allreduce_small("ar_small", packed.reshape(8, -1, _LANES))
    red_small = dict(zip(small_names, _unpack(summed, [small[n].shape for n in small_names])))
    red_small["s5_d"] = lax.dynamic_slice(red_small["s5_d"], (0, q * (D // 4)), (1, D // 4))

    grads, deltas, new_m, new_v = {}, {}, {}, {}
    for n, kind in _BIG.items():
        R, C = _rc(kind, w[n].shape)
        grads[n], deltas[n], new_m[n], new_v[n] = _adamw_big(f"adamw_{n}", w[n], mom[n], var[n], big_layers[n],
                                                             kind, R, C, lax.axis_index("c"))
    shapes = [w[n].shape for n in small_names]
    dl, mo, vo = _adamw_flat(_pack([w[n] for n in small_names], 64), _pack([red_small[n] for n in small_names], 64),
                             _pack([mom[n] for n in small_names], 64), _pack([var[n] for n in small_names], 64))
    for n, g_, d_, m_, v_ in zip(small_names, [red_small[n] for n in small_names], _unpack(dl, shapes),
                                 _unpack(mo, shapes), _unpack(vo, shapes)):
        grads[n], deltas[n], new_m[n], new_v[n] = g_.reshape(w[n].shape), d_, m_, v_

    return (loss, grad_x, *[grads[n] for n in _WEIGHT_NAMES], *[deltas[n] for n in _WEIGHT_NAMES],
            *[new_m[n] for n in _WEIGHT_NAMES], *[new_v[n] for n in _WEIGHT_NAMES])
```

```python
import functools
import math

import jax
import jax.numpy as jnp
from jax import lax
from jax.experimental import pallas as pl
from jax.experimental.pallas import tpu as pltpu
from jax.experimental.pallas import tpu_sc as plsc

F32 = jnp.float32
_MXU = jnp.bfloat16
_WIRE = jnp.bfloat16
_EPS = 1e-6
_VMEM_LIMIT = 48 * 1024 * 1024
_LANES = 128
_MESH = pl.DeviceIdType.MESH

_LR, _B1, _B2, _AEPS, _WD, _STEP = 0.001, 0.9, 0.999, 1e-08, 0.01, 10

_GM_CHUNK = 128
_GM_HEADS = 16
_S5_GT = 8
_S5_P = 64
_S5_C = 16

_NN = (((1,), (0,)), ((), ()))
_NT = (((1,), (1,)), ((), ()))
_TN = (((0,), (0,)), ((), ()))


def _cparams(sem):
    return pltpu.CompilerParams(dimension_semantics=sem, vmem_limit_bytes=_VMEM_LIMIT)


def _sig(x):
    return 1.0 / (1.0 + jnp.exp(-x))


_GC = math.sqrt(2.0 / math.pi)


def _gelu(x):
    return 0.5 * x * (1.0 + jnp.tanh(_GC * (x + 0.044715 * (x * x * x))))


def _gelu_grad(x):
    t = jnp.tanh(_GC * (x + 0.044715 * (x * x * x)))
    return 0.5 * (1.0 + t) + 0.5 * x * (1.0 - t * t) * (_GC * (1.0 + 3.0 * 0.044715 * x * x))


def _dot(a, b, dn):
    return lax.dot_general(a.astype(_MXU), b.astype(_MXU), dn, preferred_element_type=F32)


class _W:
    def __init__(self, arr, kind):
        self.a, self.kind = arr, kind
        self.R, self.C = arr.shape[2], arr.shape[3]

    def full_shape(self):
        return (2 * self.R, 4 * self.C) if self.kind == "col" else (4 * self.R, 2 * self.C)


def _part_index(kind, R, C, tr, tc, rb, cb):
    nr, nc = R // tr, C // tc
    if kind == "col":
        return cb // nc, rb // nr, rb % nr, cb % nc
    return rb // nr, cb // nc, rb % nr, cb % nc


def _wspec(w, tr, tc, rb_fn, cb_fn):
    assert w.R % tr == 0 and w.C % tc == 0, (w.R, w.C, tr, tc)

    def imap(i, j, k):
        return _part_index(w.kind, w.R, w.C, tr, tc, rb_fn(i, j, k), cb_fn(i, j, k))

    return pl.BlockSpec((None, None, tr, tc), imap)


def _gspec(kind, R, C, tr, tc):
    assert R % tr == 0 and C % tc == 0, (R, C, tr, tc)

    def imap(i, j, k):
        part, half, rbi, cbi = _part_index(kind, R, C, tr, tc, i, j)
        return half, part, rbi, cbi

    return pl.BlockSpec((None, None, tr, tc), imap)


def _mm(name, grid, a_ops, b_ops, pairs, acc_shape, n_acc, extras, outs, epilogue):
    nk = grid[2]
    na, nb, ne, no = len(a_ops), len(b_ops), len(extras), len(outs)

    def body(*refs):
        a_refs = refs[:na]
        b_refs = refs[na:na + nb]
        e_refs = refs[na + nb:na + nb + ne]
        o_refs = refs[na + nb + ne:na + nb + ne + no]
        acc_refs = refs[na + nb + ne + no:]
        k = pl.program_id(2)

        def products():
            sums = [None] * n_acc
            for ai, bi, ci, dn in pairs:
                d = _dot(a_refs[ai][...], b_refs[bi][...], dn)
                sums[ci] = d if sums[ci] is None else sums[ci] + d
            return sums

        def finish(accs):
            res = epilogue(accs, [e[...] for e in e_refs])
            for o, r in zip(o_refs, res):
                o[...] = r.astype(o.dtype)

        if nk == 1:
            finish(products())
            return

        @pl.when(k == 0)
        def _():
            for acc, d in zip(acc_refs, products()):
                acc[...] = d

        @pl.when(jnp.logical_and(k > 0, k < nk - 1))
        def _():
            for acc, d in zip(acc_refs, products()):
                acc[...] += d

        @pl.when(k == nk - 1)
        def _():
            finish([acc[...] + d for acc, d in zip(acc_refs, products())])

    ops = list(a_ops) + list(b_ops) + list(extras)
    return pl.pallas_call(
        body, name=name, grid=grid,
        in_specs=[s for _, s in ops],
        out_specs=[s for _, s in outs],
        out_shape=[s for s, _ in outs],
        scratch_shapes=[pltpu.VMEM(acc_shape, F32) for _ in range(n_acc if nk > 1 else 0)],
        compiler_params=_cparams(("parallel", "parallel", "arbitrary")),
    )(*[a for a, _ in ops])


def _bs(shape, fn):
    return pl.BlockSpec(shape, fn)


def _tile_m(L):
    return min(L, 512)


def _mm_nn(name, x, ws, tk, tn, n_out, extras, outs_sd, epilogue, tm=None, cb_offsets=None):
    M, K = x.shape
    tm = tm or _tile_m(M)
    grid = (M // tm, n_out // tn, K // tk)
    a_ops = [(x, _bs((tm, tk), lambda i, j, k: (i, k)))]
    cb_offsets = cb_offsets or [0] * len(ws)
    b_ops = [(w.a, _wspec(w, tk, tn, lambda i, j, k: k, (lambda off: lambda i, j, k: j + off)(off)))
             for w, off in zip(ws, cb_offsets)]
    pairs = [(0, bi, bi, _NN) for bi in range(len(ws))]
    mn = _bs((tm, tn), lambda i, j, k: (i, j))
    ex = [(e, mn) for e in extras]
    outs = [(jax.ShapeDtypeStruct((M, n_out), dt), mn) for dt in outs_sd]
    return _mm(name, grid, a_ops, b_ops, pairs, (tm, tn), len(ws), ex, outs, epilogue)


def _mm_nt(name, xs, ws, tn, tk, extras, outs_sd, epilogue, tm=None):
    M, Nw = xs[0].shape
    Kw = ws[0].full_shape()[0]
    tm = tm or _tile_m(M)
    grid = (M // tm, Kw // tn, Nw // tk)
    a_ops = [(x, _bs((tm, tk), lambda i, j, k: (i, k))) for x in xs]
    b_ops = [(w.a, _wspec(w, tn, tk, lambda i, j, k: j, lambda i, j, k: k)) for w in ws]
    pairs = [(i, i, 0, _NT) for i in range(len(ws))]
    mn = _bs((tm, tn), lambda i, j, k: (i, j))
    ex = [(e, mn) for e in extras]
    outs = [(jax.ShapeDtypeStruct((M, Kw), dt), mn) for dt in outs_sd]
    return _mm(name, grid, a_ops, b_ops, pairs, (tm, tn), 1, ex, outs, epilogue)


def _mm_tn(name, x, dys, kind, R, C, tm, tn, tk=None):
    L, Kw = x.shape
    Nw = dys[0].shape[1]
    tk = tk or min(L, 1024)
    grid = (Kw // tm, Nw // tn, L // tk)
    a_ops = [(x, _bs((tk, tm), lambda i, j, k: (k, i)))]
    b_ops = [(dy, _bs((tk, tn), lambda i, j, k: (k, j))) for dy in dys]
    pairs = [(0, bi, bi, _TN) for bi in range(len(dys))]
    gs = _gspec(kind, R, C, tm, tn)
    outs = [(jax.ShapeDtypeStruct((2, 4, R, C), _WIRE), gs) for _ in dys]
    return _mm(name, grid, a_ops, b_ops, pairs, (tm, tn), len(dys), [], outs, lambda accs, ex: accs)


def _row_tile(L):
    return min(L, 256)


def _rowwise(name, body, ins, outs, L, acc_outs=()):
    tr = _row_tile(L)
    n_in, n_out = len(ins), len(outs)

    def kbody(*refs):
        i_refs, o_refs, a_refs = refs[:n_in], refs[n_in:n_in + n_out], refs[n_in + n_out:]
        res, sums = body(*[r[...] for r in i_refs])
        for o, r in zip(o_refs, res):
            o[...] = r.astype(o.dtype)
        if a_refs:
            @pl.when(pl.program_id(0) == 0)
            def _():
                for a in a_refs:
                    a[...] = jnp.zeros(a.shape, F32)
            for a, s in zip(a_refs, sums):
                a[...] += s

    in_specs = []
    for arr, kind in ins:
        if kind == "row":
            in_specs.append(pl.BlockSpec((tr, arr.shape[1]), lambda i: (i, 0)))
        else:
            in_specs.append(pl.BlockSpec(arr.shape, lambda i: (0, 0)))
    out_specs = [pl.BlockSpec((tr, c), lambda i: (i, 0)) for c, _ in outs]
    out_shape = [jax.ShapeDtypeStruct((L, c), dt) for c, dt in outs]
    out_specs += [pl.BlockSpec((1, c), lambda i: (0, 0)) for c in acc_outs]
    out_shape += [jax.ShapeDtypeStruct((1, c), F32) for c in acc_outs]
    return pl.pallas_call(
        kbody, name=name, grid=(L // tr,), in_specs=in_specs, out_specs=out_specs, out_shape=out_shape,
        compiler_params=_cparams(("arbitrary",)),
    )(*[a for a, _ in ins])


def _rms_fwd(name, x, g):
    def body(xv, gv):
        r = lax.rsqrt(jnp.mean(xv * xv, axis=-1, keepdims=True) + _EPS)
        return [xv * r * gv], []
    return _rowwise(name, body, [(x, "row"), (g, "vec")], [(x.shape[1], _MXU)], x.shape[0])[0]


def _rms_bwd(name, dh, x, g, dres):
    def body(dhv, xv, gv, dr):
        r = lax.rsqrt(jnp.mean(xv * xv, axis=-1, keepdims=True) + _EPS)
        xh = xv * r
        dxh = dhv * gv
        dx = r * (dxh - xh * jnp.mean(dxh * xh, axis=-1, keepdims=True))
        return [dr + dx], [jnp.sum(dhv * xh, axis=0, keepdims=True)]
    D = x.shape[1]
    return _rowwise(name, body, [(dh, "row"), (x, "row"), (g, "vec"), (dres, "row")], [(D, F32)], x.shape[0], [D])


def _loss_head(x, g, target):
    D = x.shape[1]

    def body(xv, gv, tv):
        r = lax.rsqrt(jnp.mean(xv * xv, axis=-1, keepdims=True) + _EPS)
        xh = xv * r
        e = xh * gv - tv
        dy = e * (1.0 / D)
        dxh = dy * gv
        dx = r * (dxh - xh * jnp.mean(dxh * xh, axis=-1, keepdims=True))
        row_loss = 0.5 * jnp.mean(e * e, axis=-1, keepdims=True)
        lsum = jnp.sum(row_loss, axis=0, keepdims=True) + jnp.zeros((1, _LANES), F32)
        return [dx], [jnp.sum(dy * xh, axis=0, keepdims=True), lsum]
    return _rowwise("loss_head", body, [(x, "row"), (g, "vec"), (target, "row")], [(D, F32)], x.shape[0], [D, _LANES])


def _ple_bwd_elem(dx, pp, gt):
    def body(dxv, ppv, gtv):
        gt32 = gtv.astype(F32)
        return [dxv * ppv * gt32 * (1.0 - gt32), dxv * gt32], []
    D = dx.shape[1]
    return _rowwise("ple_bwd_elem", body, [(dx, "row"), (pp, "row"), (gt, "row")], [(D, _MXU), (D, _MXU)], dx.shape[0])


def _glu_bwd_elem(dx, val, sg):
    def body(dxv, vv, sv):
        v32, s32 = vv.astype(F32), sv.astype(F32)
        return [jnp.concatenate([dxv * s32, dxv * v32 * s32 * (1.0 - s32)], axis=1)], []
    D = dx.shape[1]
    return _rowwise("glu_bwd_elem", body, [(dx, "row"), (val, "row"), (sg, "row")], [(2 * D, _MXU)], dx.shape[0])[0]


def _gm_common(z, ln_g, ln_b, wc_bf, bsT):
    W = z.shape[1] // 2
    zu, zv = z[:, :W], z[:, W:]
    u, v = _gelu(zu), _gelu(zv)
    mu = jnp.mean(v, axis=-1, keepdims=True)
    vc = v - mu
    rstd = lax.rsqrt(jnp.mean(vc * vc, axis=-1, keepdims=True) + _EPS)
    vh = vc * rstd
    vn = vh * ln_g + ln_b
    vnb = vn.astype(_MXU)
    svs = []
    for h in range(_GM_HEADS):
        sl = slice(h * _LANES, (h + 1) * _LANES)
        svs.append(_dot(wc_bf[h], vnb[:, sl], _NN) + bsT[:, h:h + 1])
    return zu, zv, u, vh, rstd, vnb, svs


def _causal(w):
    t = lax.broadcasted_iota(jnp.int32, w.shape, w.ndim - 2)
    s = lax.broadcasted_iota(jnp.int32, w.shape, w.ndim - 1)
    return jnp.where(s <= t, w, jnp.zeros_like(w))


def _gmlp_fwd(z, ln_g, ln_b, w_s, bsT):
    L, W2 = z.shape
    W = W2 // 2

    def body(z_ref, g_ref, b_ref, ws_ref, bs_ref, m_ref):
        wc = _causal(ws_ref[...]).astype(_MXU)
        _, _, u, _, _, _, svs = _gm_common(z_ref[...], g_ref[...], b_ref[...], wc, bs_ref[...])
        for h in range(_GM_HEADS):
            sl = slice(h * _LANES, (h + 1) * _LANES)
            m_ref[:, sl] = (u[:, sl] * svs[h]).astype(m_ref.dtype)

    return pl.pallas_call(
        body, name="gmlp_fwd", grid=(L // _GM_CHUNK,),
        in_specs=[pl.BlockSpec((_GM_CHUNK, W2), lambda n: (n, 0)),
                  pl.BlockSpec((1, W), lambda n: (0, 0)), pl.BlockSpec((1, W), lambda n: (0, 0)),
                  pl.BlockSpec(w_s.shape, lambda n: (0, 0, 0)), pl.BlockSpec(bsT.shape, lambda n: (0, 0))],
        out_specs=pl.BlockSpec((_GM_CHUNK, W), lambda n: (n, 0)),
        out_shape=jax.ShapeDtypeStruct((L, W), _MXU),
        compiler_params=_cparams(("arbitrary",)),
    )(z, ln_g, ln_b, w_s, bsT)


def _gmlp_bwd(z, dm, ln_g, ln_b, w_s, bsT):
    L, W2 = z.shape
    W = W2 // 2
    T = _GM_CHUNK

    def body(z_ref, dm_ref, g_ref, b_ref, ws_ref, bs_ref, dz_ref, dws_ref, dbs_ref, dg_ref, db_ref):
        @pl.when(pl.program_id(0) == 0)
        def _():
            dws_ref[...] = jnp.zeros(dws_ref.shape, F32)
            dbs_ref[...] = jnp.zeros(dbs_ref.shape, F32)
            dg_ref[...] = jnp.zeros(dg_ref.shape, F32)
            db_ref[...] = jnp.zeros(db_ref.shape, F32)

        wc = _causal(ws_ref[...]).astype(_MXU)
        ln_g_v = g_ref[...]
        zu, zv, u, vh, rstd, vnb, svs = _gm_common(z_ref[...], ln_g_v, b_ref[...], wc, bs_ref[...])
        dmv = dm_ref[...]
        lane = lax.broadcasted_iota(jnp.int32, (T, _LANES), 1)
        dbs = jnp.zeros((T, _LANES), F32)
        dvn_parts = []
        for h in range(_GM_HEADS):
            sl = slice(h * _LANES, (h + 1) * _LANES)
            dsv = dmv[:, sl] * u[:, sl]
            dz_ref[:, sl] = (dmv[:, sl] * svs[h] * _gelu_grad(zu[:, sl])).astype(dz_ref.dtype)
            dbs = dbs + jnp.where(lane == h, jnp.sum(dsv, axis=1, keepdims=True), 0.0)
            dsvb = dsv.astype(_MXU)
            dws_ref[h] += _dot(dsvb, vnb[:, sl], _NT)
            dvn_parts.append(_dot(wc[h], dsvb, _TN))
        dbs_ref[...] += dbs
        dvn = jnp.concatenate(dvn_parts, axis=1)
        dg_ref[...] += jnp.sum(dvn * vh, axis=0, keepdims=True)
        db_ref[...] += jnp.sum(dvn, axis=0, keepdims=True)
        dxh = dvn * ln_g_v
        dv = rstd * (dxh - jnp.mean(dxh, axis=-1, keepdims=True) - vh * jnp.mean(dxh * vh, axis=-1, keepdims=True))
        dz_ref[:, W:] = (dv * _gelu_grad(zv)).astype(dz_ref.dtype)

        @pl.when(pl.program_id(0) == pl.num_programs(0) - 1)
        def _():
            dws_ref[...] = _causal(dws_ref[...])

    return pl.pallas_call(
        body, name="gmlp_bwd", grid=(L // T,),
        in_specs=[pl.BlockSpec((T, W2), lambda n: (n, 0)), pl.BlockSpec((T, W), lambda n: (n, 0)),
                  pl.BlockSpec((1, W), lambda n: (0, 0)), pl.BlockSpec((1, W), lambda n: (0, 0)),
                  pl.BlockSpec(w_s.shape, lambda n: (0, 0, 0)), pl.BlockSpec(bsT.shape, lambda n: (0, 0))],
        out_specs=[pl.BlockSpec((T, W2), lambda n: (n, 0)),
                   pl.BlockSpec(w_s.shape, lambda n: (0, 0, 0)), pl.BlockSpec((T, _LANES), lambda n: (0, 0)),
                   pl.BlockSpec((1, W), lambda n: (0, 0)), pl.BlockSpec((1, W), lambda n: (0, 0))],
        out_shape=[jax.ShapeDtypeStruct((L, W2), _MXU), jax.ShapeDtypeStruct(w_s.shape, F32),
                   jax.ShapeDtypeStruct((T, _LANES), F32),
                   jax.ShapeDtypeStruct((1, W), F32), jax.ShapeDtypeStruct((1, W), F32)],
        compiler_params=_cparams(("arbitrary",)),
    )(z, dm, ln_g, ln_b, w_s, bsT)


def _s5_prep_math(a_re, a_im, log_dt):
    dt = jnp.exp(log_dt)
    xr, xi = a_re * dt, a_im * dt
    e = jnp.exp(xr)
    lbr, lbi = e * jnp.cos(xi), e * jnp.sin(xi)
    dn = a_re * a_re + a_im * a_im
    nr, ni = lbr - 1.0, lbi
    pr, pi = nr * a_re + ni * a_im, ni * a_re - nr * a_im
    return dt, lbr, lbi, dn, nr, ni, pr, pi


def _vm():
    return pl.BlockSpec(memory_space=pltpu.VMEM)


def _s5_prep(a_re, a_im, log_dt, bT_re, bT_im):
    def body(ar_ref, ai_ref, ld_ref, br_ref, bi_ref, lbr_ref, lbi_ref, Br_ref, Bi_ref):
        _, lbr, lbi, dn, _, _, pr, pi = _s5_prep_math(ar_ref[...], ai_ref[...], ld_ref[...])
        cr, ci = pr / dn, pi / dn
        lbr_ref[...] = lbr
        lbi_ref[...] = lbi
        br, bi = br_ref[...], bi_ref[...]
        Br_ref[...] = cr[None] * br - ci[None] * bi
        Bi_ref[...] = cr[None] * bi + ci[None] * br

    sd = jax.ShapeDtypeStruct
    return pl.pallas_call(
        body, name="s5_prep", in_specs=[_vm()] * 5, out_specs=[_vm()] * 4,
        out_shape=[sd(a_re.shape, F32), sd(a_re.shape, F32), sd(bT_re.shape, F32), sd(bT_re.shape, F32)],
    )(a_re, a_im, log_dt, bT_re, bT_im)


def _s5_prep_bwd(a_re, a_im, log_dt, bT_re, bT_im, dlbr_s, dlbi_s, dBr, dBi):
    def body(ar_ref, ai_ref, ld_ref, br_ref, bi_ref, dlr_ref, dli_ref, dBr_ref, dBi_ref,
             dar_ref, dai_ref, dld_ref, dbr_ref, dbi_ref):
        a_re_v, a_im_v = ar_ref[...], ai_ref[...]
        dt, lbr, lbi, dn, nr, ni, pr, pi = _s5_prep_math(a_re_v, a_im_v, ld_ref[...])
        cr, ci = pr / dn, pi / dn
        br, bi, dBr_v, dBi_v = br_ref[...], bi_ref[...], dBr_ref[...], dBi_ref[...]
        dbr_ref[...] = cr[None] * dBr_v + ci[None] * dBi_v
        dbi_ref[...] = cr[None] * dBi_v - ci[None] * dBr_v
        dcr = jnp.sum(br * dBr_v + bi * dBi_v, axis=0)
        dci = jnp.sum(br * dBi_v - bi * dBr_v, axis=0)
        dpr, dpi = dcr / dn, dci / dn
        ddn = -(dcr * pr + dci * pi) / (dn * dn)
        dnr = dpr * a_re_v - dpi * a_im_v
        dni = dpr * a_im_v + dpi * a_re_v
        dlbr = dlr_ref[...] + dnr
        dlbi = dli_ref[...] + dni
        dxr = dlbr * lbr + dlbi * lbi
        dxi = dlbi * lbr - dlbr * lbi
        dar_ref[...] = dpr * nr + dpi * ni + 2.0 * ddn * a_re_v + dxr * dt
        dai_ref[...] = dpr * ni - dpi * nr + 2.0 * ddn * a_im_v + dxi * dt
        dld_ref[...] = jnp.sum(dxr * a_re_v + dxi * a_im_v, axis=1, keepdims=True) * dt

    sd = jax.ShapeDtypeStruct
    return pl.pallas_call(
        body, name="s5_prep_bwd", in_specs=[_vm()] * 9, out_specs=[_vm()] * 5,
        out_shape=[sd(a_re.shape, F32), sd(a_re.shape, F32), sd(log_dt.shape, F32),
                   sd(bT_re.shape, F32), sd(bT_re.shape, F32)],
    )(a_re, a_im, log_dt, bT_re, bT_im, dlbr_s, dlbi_s, dBr, dBi)


def _shift_rows(v, down):
    n = v.shape[0]
    rolled = pltpu.roll(v, 1 if down else n - 1, 0)
    row = lax.broadcasted_iota(jnp.int32, v.shape, 0)
    return jnp.where(row == (0 if down else n - 1), 0.0, rolled)


def _cmul(ar, ai, br, bi):
    return ar * br - ai * bi, ar * bi + ai * br


_SEG = 8
_UNROLL = 8


def _seg_rows(k):
    if isinstance(k, int):
        return pl.ds(k * _SEG, _SEG)
    return pl.ds(pl.multiple_of(k * _SEG, _SEG), _SEG)


def _unrolled(n, step, init):
    main = n // _UNROLL

    def trip(kk, s):
        for uu in range(_UNROLL):
            s = step(kk * _UNROLL + uu, s)
        return s

    s = lax.fori_loop(0, main, trip, init)
    for r in range(main * _UNROLL, n):
        s = step(r, s)
    return s


def _interleave(src_ref, dst_ref, nk):
    def step(k, carry):
        dst_ref[_seg_rows(k), :] = src_ref[pl.ds(k, _SEG, stride=nk), :]
        return carry
    _unrolled(nk, step, 0)


def _deinterleave(src_ref, dst_ref, nk):
    def step(k, carry):
        dst_ref[pl.ds(k, _SEG, stride=nk), :] = src_ref[_seg_rows(k), :]
        return carry
    _unrolled(nk, step, 0)


def _segment_inits(er, ei, ar, ai, nk, down):
    pr, pi = ar, ai
    for _ in range(int(math.log2(nk))):
        pr, pi = _cmul(pr, pi, pr, pi)
    fr, fi = er, ei
    for _ in range(_SEG - 1):
        sr, si = _shift_rows(fr, down), _shift_rows(fi, down)
        mr, mi = _cmul(pr, pi, sr, si)
        fr, fi = er + mr, ei + mi
    return _shift_rows(fr, down), _shift_rows(fi, down)


def _scan_states(x_re, x_im, ar, ai, nk):
    lanes = ar.shape[1]

    def step(k, s):
        rows = _seg_rows(k)
        mr, mi = _cmul(ar, ai, s[0], s[1])
        return mr + x_re[rows, :], mi + x_im[rows, :]

    zero = jnp.zeros((_SEG, lanes), F32)
    er, ei = _unrolled(nk, step, (zero, zero))
    ir, ii = _segment_inits(er, ei, ar, ai, nk, True)

    def step2(k, s):
        rows = _seg_rows(k)
        mr, mi = _cmul(ar, ai, s[0], s[1])
        nr, ni = mr + x_re[rows, :], mi + x_im[rows, :]
        x_re[rows, :] = nr
        x_im[rows, :] = ni
        return nr, ni

    _unrolled(nk, step2, (ir, ii))


def _s5_tile_fwd(u, bd_re, bd_im, cd_re, cd_im, ar, ai, d, s_re, s_im, nk):
    s_re[...] = _dot(u, bd_re, _NN)
    s_im[...] = _dot(u, bd_im, _NN)
    _scan_states(s_re, s_im, ar, ai, nk)
    return _dot(s_re[...], cd_re, _NN) - _dot(s_im[...], cd_im, _NN) + d * u


def _s5_specs(L, T):
    lanes = _S5_GT * _S5_P
    u_spec = pl.BlockSpec((L, _LANES), lambda t: (0, t))
    bd_spec = pl.BlockSpec((None, _LANES, lanes), lambda t: (t, 0, 0))
    cd_spec = pl.BlockSpec((None, lanes, _LANES), lambda t: (t, 0, 0))
    lam_spec = pl.BlockSpec((None, 1, lanes), lambda t: (t, 0, 0))
    d_spec = pl.BlockSpec((1, _LANES), lambda t: (0, t))
    return lanes, u_spec, bd_spec, cd_spec, lam_spec, d_spec


def _s5_fwd(u, bd_re, bd_im, cd_re, cd_im, lam_re, lam_im, d):
    L, Wd = u.shape
    T = Wd // _LANES
    nk = L // _SEG
    lanes, u_spec, bd_spec, cd_spec, lam_spec, d_spec = _s5_specs(L, T)

    def body(u_ref, bdr, bdi, cdr, cdi, lr, li, d_ref, g_ref, s_re, s_im, up, tmp):
        ar = jnp.broadcast_to(lr[...], (_SEG, lanes))
        ai = jnp.broadcast_to(li[...], (_SEG, lanes))
        _interleave(u_ref, up, nk)
        y = _s5_tile_fwd(up[...], bdr[...], bdi[...], cdr[...], cdi[...], ar, ai, d_ref[...], s_re, s_im, nk)
        up[...] = _gelu(y)
        _deinterleave(up, tmp, nk)
        g_ref[...] = tmp[...].astype(g_ref.dtype)

    return pl.pallas_call(
        body, name="s5_fwd", grid=(T,),
        in_specs=[u_spec, bd_spec, bd_spec, cd_spec, cd_spec, lam_spec, lam_spec, d_spec],
        out_specs=u_spec, out_shape=jax.ShapeDtypeStruct((L, Wd), _MXU),
        scratch_shapes=[pltpu.VMEM((L, lanes), F32) for _ in range(2)]
        + [pltpu.VMEM((L, _LANES), F32) for _ in range(2)],
        compiler_params=_cparams(("arbitrary",)),
    )(u, bd_re, bd_im, cd_re, cd_im, lam_re, lam_im, d)


def _s5_bwd(u, dg, bd_re, bd_im, cd_re, cd_im, lam_re, lam_im, d):
    L, Wd = u.shape
    T = Wd // _LANES
    nk = L // _SEG
    lanes, u_spec, bd_spec, cd_spec, lam_spec, d_spec = _s5_specs(L, T)

    def body(u_ref, dg_ref, bdr, bdi, cdr, cdi, lr, li, d_ref,
             du_ref, dbdr, dbdi, dcdr, dcdi, dlr, dli, dd_ref, s_re, s_im, g_re, g_im, up, dgp, tmp):
        ar = jnp.broadcast_to(lr[...], (_SEG, lanes))
        ai = jnp.broadcast_to(li[...], (_SEG, lanes))
        _interleave(u_ref, up, nk)
        _interleave(dg_ref, dgp, nk)
        uv, dv = up[...], d_ref[...]
        y = _s5_tile_fwd(uv, bdr[...], bdi[...], cdr[...], cdi[...], ar, ai, dv, s_re, s_im, nk)
        dy = dgp[...] * _gelu_grad(y)
        dd_ref[...] = jnp.sum(dy * uv, axis=0, keepdims=True)
        dyb = dy.astype(_MXU)
        dcdr[...] = _dot(dyb, s_re[...], _TN)
        dcdi[...] = -_dot(dyb, s_im[...], _TN)
        g_re[...] = _dot(dyb, cdr[...], _NT)
        g_im[...] = -_dot(dyb, cdi[...], _NT)

        nai = -ai

        def step(j, s):
            rows = _seg_rows(nk - 1 - j)
            mr, mi = _cmul(ar, nai, s[0], s[1])
            return mr + g_re[rows, :], mi + g_im[rows, :]

        zero = jnp.zeros((_SEG, lanes), F32)
        er, ei = _unrolled(nk, step, (zero, zero))
        ir, ii = _segment_inits(er, ei, ar, nai, nk, False)

        def acc_lam(gr, gi, pr, pi, acc):
            return acc[0] + gr * pr + gi * pi, acc[1] + gi * pr - gr * pi

        def step2(j, carry):
            s, acc = carry
            k = nk - 1 - j
            rows = _seg_rows(k)
            mr, mi = _cmul(ar, nai, s[0], s[1])
            nr, ni = mr + g_re[rows, :], mi + g_im[rows, :]
            g_re[rows, :] = nr
            g_im[rows, :] = ni
            prev = _seg_rows(k - 1)
            return (nr, ni), acc_lam(nr, ni, s_re[prev, :], s_im[prev, :], acc)

        (g0r, g0i), acc = _unrolled(nk - 1, step2, ((ir, ii), (zero, zero)))
        first = _seg_rows(0)
        mr, mi = _cmul(ar, nai, g0r, g0i)
        nr, ni = mr + g_re[first, :], mi + g_im[first, :]
        g_re[first, :] = nr
        g_im[first, :] = ni
        last = _seg_rows(nk - 1)
        acc = acc_lam(nr, ni, _shift_rows(s_re[last, :], True), _shift_rows(s_im[last, :], True), acc)
        dlr[...] = jnp.sum(acc[0], axis=0, keepdims=True)
        dli[...] = jnp.sum(acc[1], axis=0, keepdims=True)

        gtr, gti = g_re[...].astype(_MXU), g_im[...].astype(_MXU)
        ub = uv.astype(_MXU)
        dbdr[...] = _dot(ub, gtr, _TN)
        dbdi[...] = _dot(ub, gti, _TN)
        dgp[...] = _dot(gtr, bdr[...], _NT) + _dot(gti, bdi[...], _NT) + dy * dv
        _deinterleave(dgp, tmp, nk)
        du_ref[...] = tmp[...].astype(du_ref.dtype)

    sd = jax.ShapeDtypeStruct
    big = sd((T, _LANES, lanes), F32)
    return pl.pallas_call(
        body, name="s5_bwd", grid=(T,),
        in_specs=[u_spec, u_spec, bd_spec, bd_spec, cd_spec, cd_spec, lam_spec, lam_spec, d_spec],
        out_specs=[u_spec, bd_spec, bd_spec, bd_spec, bd_spec, lam_spec, lam_spec, d_spec],
        out_shape=[sd((L, Wd), _MXU), big, big, big, big, sd((T, 1, lanes), F32), sd((T, 1, lanes), F32),
                   sd((1, Wd), F32)],
        scratch_shapes=[pltpu.VMEM((L, lanes), F32) for _ in range(4)]
        + [pltpu.VMEM((L, _LANES), F32) for _ in range(3)],
        compiler_params=_cparams(("arbitrary",)),
    )(u, dg, bd_re, bd_im, cd_re, cd_im, lam_re, lam_im, d)


def _half_tile(R):
    for t in (256, 352, 128):
        if R % t == 0:
            return t
    raise ValueError(R)


def _cast_shard(name, w, layer, kind, R, C):
    tr = _half_tile(R)
    nr = R // tr

    def body(w_ref, o_ref):
        o_ref[...] = w_ref[...].astype(o_ref.dtype)

    if kind == "col":
        in_map = lambda h, i: (layer, h * nr + i, 0)
    else:
        in_map = lambda h, i: (layer, i, h)
    return pl.pallas_call(
        body, name=name, grid=(2, nr), in_specs=[pl.BlockSpec((None, tr, C), in_map)],
        out_specs=pl.BlockSpec((None, tr, C), lambda h, i: (h, i, 0)),
        out_shape=jax.ShapeDtypeStruct((2, R, C), _WIRE),
        compiler_params=_cparams(("arbitrary", "arbitrary")),
    )(w)


def _adam_math(w, g, m, v):
    m2 = _B1 * m + (1.0 - _B1) * g
    v2 = _B2 * v + (1.0 - _B2) * (g * g)
    m_hat = m2 / (1.0 - _B1 ** _STEP)
    v_hat = v2 / (1.0 - _B2 ** _STEP)
    delta = -_LR * (m_hat / (jnp.sqrt(v_hat) + _AEPS) + _WD * w)
    return delta, m2, v2


def _adamw_big(name, w, m, v, layer, pair, kind, R, C, c, after, prev):
    tr = _half_tile(R)
    nr = R // tr

    def body(c_ref, w_ref, m_ref, v_ref, own_ref, other_ref, *rest):
        go_ref, d_ref, mo_ref, vo_ref = rest[-4:]
        g = jnp.where(pl.program_id(0) == c_ref[0], own_ref[...], other_ref[...])
        delta, m2, v2 = _adam_math(w_ref[...], g, m_ref[...], v_ref[...])
        go_ref[...] = g
        d_ref[...] = delta
        mo_ref[...] = m2
        vo_ref[...] = v2

    if kind == "col":
        nat = pl.BlockSpec((None, tr, C), lambda h, i, c_ref: (layer, h * nr + i, 0))
    else:
        nat = pl.BlockSpec((None, tr, C), lambda h, i, c_ref: (layer, i, h))

    def gspec(own):
        return pl.BlockSpec((tr, C), lambda h, i, c_ref: (jnp.where((h == c_ref[0]) == own, i, 0), 0))

    carried = list(prev) if prev is not None else []
    gs = pltpu.PrefetchScalarGridSpec(
        num_scalar_prefetch=1, grid=(2, nr),
        in_specs=[nat, nat, nat, gspec(True), gspec(False), _any()] + [_any()] * len(carried),
        out_specs=[nat] * 4)
    sd = jax.ShapeDtypeStruct(w.shape, F32)
    return pl.pallas_call(
        body, name=name, grid_spec=gs, out_shape=[sd] * 4,
        input_output_aliases={7 + k: k for k in range(len(carried))},
        compiler_params=_cparams(("arbitrary", "arbitrary")),
    )(c.astype(jnp.int32).reshape(1), w, m, v, pair[0], pair[1], after, *carried)


def _adamw_flat(w, g, m, v):
    rows = w.shape[0]
    tr = rows // 8 if rows % 64 == 0 else rows

    def body(w_ref, g_ref, m_ref, v_ref, d_ref, mo_ref, vo_ref):
        delta, m2, v2 = _adam_math(w_ref[...], g_ref[...], m_ref[...], v_ref[...])
        d_ref[...] = delta
        mo_ref[...] = m2
        vo_ref[...] = v2

    spec = pl.BlockSpec((tr, _LANES), lambda i: (i, 0))
    sd = jax.ShapeDtypeStruct(w.shape, F32)
    return pl.pallas_call(
        body, name="adamw_small", grid=(rows // tr,), in_specs=[spec] * 4, out_specs=[spec] * 3,
        out_shape=[sd] * 3, compiler_params=_cparams(("arbitrary",)),
    )(w, g, m, v)


def _place():
    x, y, c = lax.axis_index("x"), lax.axis_index("y"), lax.axis_index("c")
    chips = [(1 - x, y), (x, 1 - y), (1 - x, 1 - y)]
    return x, y, c, 2 * x + y, chips


def _any():
    return pl.BlockSpec(memory_space=pl.ANY)


def _remote(src, dst, ssem, rsem, dev):
    return pltpu.make_async_remote_copy(src_ref=src, dst_ref=dst, send_sem=ssem, recv_sem=rsem,
                                        device_id=dev, device_id_type=_MESH)


def _allgather(name, shards):
    n = len(shards)

    def body(*refs):
        s_refs, g_refs = refs[:n], refs[n:2 * n]
        send0, recv0, send1, recv1, send2, recv2 = refs[2 * n:]
        x, y, c, q, chips = _place()
        sib = (x, y, 1 - c)
        _handshake([sib] + [(rx, ry, c) for rx, ry in chips])
        own = [_remote(s_refs[a], g_refs[a].at[q], send0.at[a], recv0.at[a], sib) for a in range(n)]
        first, passed = [], []
        for a in range(n):
            for j, (rx, ry) in enumerate(chips):
                k = 3 * a + j
                first.append(_remote(s_refs[a].at[c], g_refs[a].at[q, c], send1.at[k], recv1.at[k], (rx, ry, c)))
                land = g_refs[a].at[2 * rx + ry, c]
                passed.append(_remote(land, land, send2.at[k], recv2.at[k], sib))
        for cp in first + own:
            cp.start()
        for a in range(n):
            for j, (rx, ry) in enumerate(chips):
                k = 3 * a + j
                land = g_refs[a].at[2 * rx + ry, c]
                _remote(land, land, send1.at[k], recv1.at[k], (rx, ry, c)).wait_recv()
                passed[k].start()
        for a in range(n):
            for j, (rx, ry) in enumerate(chips):
                k = 3 * a + j
                other = g_refs[a].at[2 * rx + ry, 1 - c]
                _remote(other, other, send2.at[k], recv2.at[k], sib).wait_recv()
        for cp in own:
            cp.wait()
        for cp in first + passed:
            cp.wait_send()

    return _sequencer(name, _ID_BOTH, body, shards,
                      [jax.ShapeDtypeStruct((4,) + s.shape, s.dtype) for s in shards], [n, n] + [3 * n] * 4)


def _handshake(peers):
    barrier = pltpu.get_barrier_semaphore()
    for peer in peers:
        pl.semaphore_signal(barrier, inc=1, device_id=peer, device_id_type=_MESH)
    pl.semaphore_wait(barrier, len(peers))


_ID_SIBLING, _ID_CHIPS, _ID_BOTH = 1, 2, 3


def _sequencer(name, collective_id, body, ins, out_types, sem_counts):
    mesh = plsc.ScalarSubcoreMesh(axis_name="seq", num_cores=1)
    return pl.kernel(
        body, name=name, out_type=out_types, mesh=mesh,
        scratch_types=[pltpu.SemaphoreType.DMA((k,)) for k in sem_counts],
        compiler_params=pltpu.CompilerParams(collective_id=collective_id),
    )(*ins)


def _swap_halves(name, grads):
    n = len(grads)

    def body(*refs):
        g_refs, t_refs = refs[:n], refs[n:2 * n]
        send, recv = refs[2 * n:]
        x, y, c, _, _ = _place()
        _handshake([(x, y, 1 - c)])
        cps = [_remote(g_refs[a].at[1 - c], t_refs[a], send.at[a], recv.at[a], (x, y, 1 - c)) for a in range(n)]
        for cp in cps:
            cp.start()
        for cp in cps:
            cp.wait()

    return _sequencer(name, _ID_SIBLING, body, grads,
                      [jax.ShapeDtypeStruct(g.shape[1:], g.dtype) for g in grads], [n, n])


def _chip_sum(name, g, t, after):
    _, _, R, C = g.shape
    tr = _half_tile(R)

    def body(c_ref, g_ref, t_ref, after_ref, o_ref):
        o_ref[...] = (g_ref[...].astype(F32) + t_ref[...].astype(F32)).astype(o_ref.dtype)

    c = lax.axis_index("c").astype(jnp.int32).reshape(1)
    gs = pltpu.PrefetchScalarGridSpec(
        num_scalar_prefetch=1, grid=(4, R // tr),
        in_specs=[pl.BlockSpec((None, None, tr, C), lambda r, i, c_ref: (c_ref[0], r, i, 0)),
                  pl.BlockSpec((None, tr, C), lambda r, i, c_ref: (r, i, 0)), _any()],
        out_specs=pl.BlockSpec((None, tr, C), lambda r, i, c_ref: (r, i, 0)))
    return pl.pallas_call(
        body, name=name, grid_spec=gs, out_shape=jax.ShapeDtypeStruct((4, R, C), _WIRE),
        compiler_params=_cparams(("arbitrary", "arbitrary")),
    )(c, g, t, after)


def _scatter_parts(name, parts):
    n = len(parts)

    def body(*refs):
        p_refs, t_refs = refs[:n], refs[n:2 * n]
        send, recv = refs[2 * n:]
        x, y, c, q, chips = _place()
        _handshake([(rx, ry, c) for rx, ry in chips])
        cps = []
        for a in range(n):
            for j, (rx, ry) in enumerate(chips):
                k = 3 * a + j
                cps.append(_remote(p_refs[a].at[2 * rx + ry], t_refs[a].at[q], send.at[k], recv.at[k], (rx, ry, c)))
        for cp in cps:
            cp.start()
        for a in range(n):
            for j, (rx, ry) in enumerate(chips):
                k = 3 * a + j
                land = t_refs[a].at[2 * rx + ry]
                _remote(land, land, send.at[k], recv.at[k], (rx, ry, c)).wait_recv()
        for cp in cps:
            cp.wait_send()

    return _sequencer(name, _ID_CHIPS, body, parts,
                      [jax.ShapeDtypeStruct(p.shape, p.dtype) for p in parts], [3 * n, 3 * n])


def _sum_parts(name, p, t, where, after):
    _, R, C = t.shape
    tr = _half_tile(R)

    def body(w_ref, p_ref, t0_ref, t1_ref, t2_ref, after_ref, o_ref):
        o_ref[...] = (p_ref[...].astype(F32) + t0_ref[...].astype(F32)
                      + t1_ref[...].astype(F32) + t2_ref[...].astype(F32))

    def part(slot):
        return pl.BlockSpec((None, tr, C), lambda i, w_ref: (w_ref[slot], i, 0))

    gs = pltpu.PrefetchScalarGridSpec(
        num_scalar_prefetch=1, grid=(R // tr,), in_specs=[part(0), part(1), part(2), part(3), _any()],
        out_specs=pl.BlockSpec((tr, C), lambda i, w_ref: (i, 0)))
    return pl.pallas_call(
        body, name=name, grid_spec=gs, out_shape=jax.ShapeDtypeStruct((R, C), F32),
        compiler_params=_cparams(("arbitrary",)),
    )(where, p, t, t, t, after)


def _send_halves(name, halves):
    n = len(halves)

    def body(*refs):
        h_refs, o_refs = refs[:n], refs[n:2 * n]
        send, recv = refs[2 * n:]
        x, y, c, _, _ = _place()
        _handshake([(x, y, 1 - c)])
        cps = [_remote(h_refs[a], o_refs[a], send.at[a], recv.at[a], (x, y, 1 - c)) for a in range(n)]
        for cp in cps:
            cp.start()
        for cp in cps:
            cp.wait()

    return _sequencer(name, _ID_SIBLING, body, halves,
                      [jax.ShapeDtypeStruct(h.shape, h.dtype) for h in halves], [n, n])


class _Reduction:
    def __init__(self, tag, grads):
        self.tag, self.grads = tag, grads
        self.swapped = _swap_halves(f"rs_swap_{tag}", grads)

    def scatter(self, after):
        self.parts = [_chip_sum(f"rs_chipsum_{self.tag}_{a}", g, t, after)
                      for a, (g, t) in enumerate(zip(self.grads, self.swapped))]
        self.landed = _scatter_parts(f"rs_scatter_{self.tag}", self.parts)

    def finish(self, after):
        x, y, c, q, chips = _place()
        where = jnp.stack([q] + [2 * rx + ry for rx, ry in chips]).astype(jnp.int32)
        halves = [_sum_parts(f"rs_sum_{self.tag}_{a}", p, t, where, after)
                  for a, (p, t) in enumerate(zip(self.parts, self.landed))]
        return list(zip(halves, _send_halves(f"rs_join_{self.tag}", halves)))


def _allreduce_small(name, v):
    _, R, _ = v.shape

    def body(v_ref, o_ref, land, acc, send1, recv1, send2, recv2):
        x, y, c = lax.axis_index("x"), lax.axis_index("y"), lax.axis_index("c")
        me = 4 * x + 2 * y + c
        peers = []
        for k in range(1, 8):
            dx, dy, dc = (k >> 2) & 1, (k >> 1) & 1, k & 1
            px, py, pc = (1 - x if dx else x), (1 - y if dy else y), (1 - c if dc else c)
            peers.append((k, (px, py, pc), 4 * px + 2 * py + pc))
        land[me] = v_ref[me]
        out1 = [_remote(v_ref.at[pid], land.at[me], send1.at[k], recv1.at[k], dev) for k, dev, pid in peers]
        for cp in out1:
            cp.start()
        for k, dev, pid in peers:
            _remote(land.at[pid], land.at[pid], send1.at[k], recv1.at[k], dev).wait_recv()
        total = land[0]
        for j in range(1, 8):
            total = total + land[j]
        acc[...] = total
        o_ref[me] = total
        out2 = [_remote(acc, o_ref.at[me], send2.at[k], recv2.at[k], dev) for k, dev, pid in peers]
        for cp in out2:
            cp.start()
        for k, dev, pid in peers:
            _remote(o_ref.at[pid], o_ref.at[pid], send2.at[k], recv2.at[k], dev).wait_recv()
        for cp in out1 + out2:
            cp.wait_send()

    return pl.pallas_call(
        body, name=name, in_specs=[_vm()], out_specs=_vm(),
        out_shape=jax.ShapeDtypeStruct(v.shape, F32),
        scratch_shapes=[pltpu.VMEM(v.shape, F32), pltpu.VMEM((R, _LANES), F32)]
        + [pltpu.SemaphoreType.DMA((8,)) for _ in range(4)],
        compiler_params=pltpu.CompilerParams(vmem_limit_bytes=_VMEM_LIMIT),
    )(v)


_WEIGHT_NAMES = ['norm_mix', 'norm_ffn', 'norm_ple', 'norm_final', 'gm_w_in', 'gm_ln_g', 'gm_ln_b', 'gm_w_s',
                 'gm_b_s', 'gm_w_out', 's5_w_in', 's5_a_re', 's5_a_im', 's5_log_dt', 's5_b_re', 's5_b_im',
                 's5_c_re', 's5_c_im', 's5_d', 's5_w_out', 'ffn_w1', 'ffn_w3', 'ffn_w2', 'ple_w_gate', 'ple_w_proj']
_BIG = {'gm_w_in': 'col', 'gm_w_out': 'row', 's5_w_in': 'row', 's5_w_out': 'col', 'ffn_w1': 'col',
        'ffn_w3': 'col', 'ffn_w2': 'row', 'ple_w_gate': 'row', 'ple_w_proj': 'col'}


def _rc(kind, shard_shape):
    rows, cols = shard_shape[-2:]
    return (rows // 2, cols) if kind == "col" else (rows, cols // 2)


def _pack(vecs, rows_multiple):
    flat = jnp.concatenate([a.reshape(-1).astype(F32) for a in vecs])
    unit = rows_multiple * _LANES
    pad = (-flat.shape[0]) % unit
    return jnp.pad(flat, (0, pad)).reshape(-1, _LANES)


def _unpack(buf, shapes):
    flat = buf.reshape(-1)
    out, off = [], 0
    for s in shapes:
        n = math.prod(s)
        out.append(flat[off:off + n].reshape(s))
        off += n
    return out


def _ident(accs, ex):
    return accs


def _add_resid(accs, ex):
    return [accs[0] + ex[0]]


def _swiglu_epi(accs, ex):
    a, b = accs
    return [a, b, a * _sig(a) * b]


def _swiglu_bwd_epi(accs, ex):
    df = accs[0]
    a, b = ex[0].astype(F32), ex[1].astype(F32)
    sa = _sig(a)
    return [df * b * (sa * (1.0 + a * (1.0 - sa))), df * (a * sa)]


def _ple_epi(accs, ex):
    gt = _sig(accs[0])
    return [ex[0] + gt * ex[1], gt]


def _glu_epi(accs, ex):
    val, sg = accs[0], _sig(accs[1])
    return [ex[0] + val * sg, val, sg]


def _block_diag(t):
    eye = jnp.eye(_S5_GT, dtype=t.dtype)
    return t[:, :, :, None, :] * eye[None, :, None, :, None]


def _diag_blocks(t, T, a, b):
    d = jnp.diagonal(t.reshape(T, _S5_GT, a, _S5_GT, b), axis1=1, axis2=3)
    return jnp.transpose(d, (0, 3, 1, 2))


def kernel(x, p, norm_mix, norm_ffn, norm_ple, norm_final, gm_w_in, gm_ln_g, gm_ln_b, gm_w_s, gm_b_s, gm_w_out, s5_w_in, s5_a_re, s5_a_im, s5_log_dt, s5_b_re, s5_b_im, s5_c_re, s5_c_im, s5_d, s5_w_out, ffn_w1, ffn_w3, ffn_w2, ple_w_gate, ple_w_proj, loss_target, m_norm_mix, m_norm_ffn, m_norm_ple, m_norm_final, m_gm_w_in, m_gm_ln_g, m_gm_ln_b, m_gm_w_s, m_gm_b_s, m_gm_w_out, m_s5_w_in, m_s5_a_re, m_s5_a_im, m_s5_log_dt, m_s5_b_re, m_s5_b_im, m_s5_c_re, m_s5_c_im, m_s5_d, m_s5_w_out, m_ffn_w1, m_ffn_w3, m_ffn_w2, m_ple_w_gate, m_ple_w_proj, v_norm_mix, v_norm_ffn, v_norm_ple, v_norm_final, v_gm_w_in, v_gm_ln_g, v_gm_ln_b, v_gm_w_s, v_gm_b_s, v_gm_w_out, v_s5_w_in, v_s5_a_re, v_s5_a_im, v_s5_log_dt, v_s5_b_re, v_s5_b_im, v_s5_c_re, v_s5_c_im, v_s5_d, v_s5_w_out, v_ffn_w1, v_ffn_w3, v_ffn_w2, v_ple_w_gate, v_ple_w_proj):
    env = dict(locals())
    w = {n: env[n] for n in _WEIGHT_NAMES}
    mom = {n: env["m_" + n] for n in _WEIGHT_NAMES}
    var = {n: env["v_" + n] for n in _WEIGHT_NAMES}
    xs, tgt = x[0], loss_target[0]
    L, D = xs.shape
    depth = norm_mix.shape[0]
    qx, qy = lax.axis_index("x"), lax.axis_index("y")
    q = 2 * qx + qy

    def gather(tag, items):
        shards = []
        for name, layer in items:
            kind = _BIG[name]
            R, C = _rc(kind, w[name].shape)
            shards.append(_cast_shard(f"cast_{name}{layer}", w[name], layer, kind, R, C))
        full = _allgather(f"ag_{tag}", shards)
        return {it: _W(f, _BIG[it[0]]) for it, f in zip(items, full)}

    W = {}
    mixers = [["gm_w_in", "gm_w_out"], ["s5_w_in", "s5_w_out"]]
    for i in range(depth):
        for n in mixers[i]:
            W.update(gather(f"{n}", [(n, 0)]))
        W.update(gather(f"ffn_up{i}", [("ffn_w1", i), ("ffn_w3", i)]))
        W.update(gather(f"ffn_down{i}", [("ffn_w2", i)]))
        W.update(gather(f"ple{i}", [("ple_w_gate", i), ("ple_w_proj", i)]))

    d_slots = jnp.zeros((4, D // 4), F32)
    d_slots = lax.dynamic_update_slice(d_slots, s5_d.astype(F32), (q, 0))
    d_sum = _allreduce_small("ar_s5_d", _pack([d_slots], 64).reshape(8, -1, _LANES))
    d_full = (d_sum.reshape(-1)[:D] * 0.5).reshape(1, D)

    def ffn_fwd(i, xin):
        hf = _rms_fwd(f"rms_ffn{i}", xin, norm_ffn[i:i + 1])
        a, b, f = _mm_nn(f"ffn_up{i}", hf, [W["ffn_w1", i], W["ffn_w3", i]], 1024, 1408, ffn_w2.shape[1] * 4,
                         [], [_MXU, _MXU, _MXU], _swiglu_epi)
        xo = _mm_nn(f"ffn_down{i}", f, [W["ffn_w2", i]], 1408, 1024, D, [xin], [F32], _add_resid)[0]
        return xo, (xin, hf, a, b, f)

    def ple_fwd(i, xin):
        hp = _rms_fwd(f"rms_ple{i}", xin, norm_ple[i:i + 1])
        pi = p[i, 0]
        pp = _mm_nn(f"ple_proj{i}", pi, [W["ple_w_proj", i]], 128, 512, D, [], [F32], _ident)[0]
        xo, gt = _mm_nn(f"ple_gate{i}", hp, [W["ple_w_gate", i]], 512, 1024, D, [xin, pp], [F32, _MXU], _ple_epi)
        return xo, (xin, hp, pi, pp, gt)

    h0 = _rms_fwd("rms_mix0", xs, norm_mix[0:1])
    z = _mm_nn("gm_in", h0, [W["gm_w_in", 0]], 1024, 1024, 2 * D, [], [F32], _ident)[0]
    bsT = gm_b_s[0].T
    gm_m = _gmlp_fwd(z, gm_ln_g, gm_ln_b, gm_w_s[0], bsT)
    x1 = _mm_nn("gm_out", gm_m, [W["gm_w_out", 0]], 512, 1024, D, [xs], [F32], _add_resid)[0]
    x2, ffn0 = ffn_fwd(0, x1)
    x3, ple0 = ple_fwd(0, x2)

    T = D // _LANES
    lanes = _S5_GT * _S5_P
    a_re, a_im, log_dt = s5_a_re[0], s5_a_im[0], s5_log_dt[0][:, None]
    bT_re, bT_im = jnp.transpose(s5_b_re[0], (2, 0, 1)), jnp.transpose(s5_b_im[0], (2, 0, 1))
    lbr, lbi, BrT, BiT = _s5_prep(a_re, a_im, log_dt, bT_re, bT_im)

    def to_bd(BT):
        t = jnp.transpose(BT.reshape(_S5_C, T, _S5_GT, _S5_P), (1, 2, 0, 3))
        return _block_diag(t).reshape(T, _LANES, lanes).astype(_MXU)

    def to_cd(cw):
        t = jnp.transpose(cw.reshape(T, _S5_GT, _S5_C, _S5_P), (0, 1, 3, 2))
        return _block_diag(t).reshape(T, lanes, _LANES).astype(_MXU)

    bd_re, bd_im = to_bd(BrT), to_bd(BiT)
    cd_re, cd_im = to_cd(s5_c_re[0].astype(F32)), to_cd(s5_c_im[0].astype(F32))
    lam_re, lam_im = lbr.reshape(T, 1, lanes), lbi.reshape(T, 1, lanes)

    h1 = _rms_fwd("rms_mix1", x3, norm_mix[1:2])
    u = _mm_nn("s5_in", h1, [W["s5_w_in", 0]], 512, 1024, D, [], [F32], _ident)[0]
    s5_g = _s5_fwd(u, bd_re, bd_im, cd_re, cd_im, lam_re, lam_im, d_full)
    x4, glu_val, glu_sg = _mm_nn("s5_out", s5_g, [W["s5_w_out", 0], W["s5_w_out", 0]], 1024, 1024, D, [x3],
                                 [F32, _MXU, _MXU], _glu_epi, cb_offsets=[0, 2])
    x5, ffn1 = ffn_fwd(1, x4)
    x6, ple1 = ple_fwd(1, x5)

    dx, d_norm_final, loss_rows = _loss_head(x6, norm_final[None], tgt)
    loss = lax.psum(loss_rows[0, 0], ("x", "y", "c"))

    small = {}

    def ple_bwd(i, dxo, saved):
        xin, hp, pi, pp, gt = saved
        dpre, dpp = _ple_bwd_elem(dxo, pp, gt)
        dwg = _mm_tn(f"ple_gate_dw{i}", hp, [dpre], "row", 512, 1024, 512, 1024)[0]
        dwp = _mm_tn(f"ple_proj_dw{i}", pi, [dpp], "col", 128, 512, 128, 512)[0]
        red = _Reduction(f"ple{i}", [dwg, dwp])
        dhp = _mm_nt(f"ple_gate_dx{i}", [dpre], [W["ple_w_gate", i]], 512, 1024, [], [F32], _ident)[0]
        dxin, dg = _rms_bwd(f"rms_ple_bwd{i}", dhp, xin, norm_ple[i:i + 1], dxo)
        return dxin, dg, red, dpre

    def ffn_bwd(i, dxo, saved):
        xin, hf, a, b, f = saved
        da, db = _mm_nt(f"ffn_down_dx{i}", [dxo], [W["ffn_w2", i]], 1408, 1024, [a, b], [_MXU, _MXU],
                        _swiglu_bwd_epi)
        dw1, dw3 = _mm_tn(f"ffn_up_dw{i}", hf, [da, db], "col", 1024, 1408, 512, 1408)
        dw2 = _mm_tn(f"ffn_down_dw{i}", f, [dxo], "row", 1408, 1024, 1408, 1024)[0]
        red = _Reduction(f"ffn{i}", [dw1, dw3, dw2])
        dhf = _mm_nt(f"ffn_up_dx{i}", [da, db], [W["ffn_w1", i], W["ffn_w3", i]], 1024, 1408, [], [F32], _ident)[0]
        dxin, dg = _rms_bwd(f"rms_ffn_bwd{i}", dhf, xin, norm_ffn[i:i + 1], dxo)
        return dxin, dg, red, da, dhf

    d_norm_ple, d_norm_ffn, d_norm_mix = [None] * depth, [None] * depth, [None] * depth
    reduced = {}

    def keep(names, layer, pairs):
        for n, pr in zip(names, pairs):
            reduced[n, layer] = pr

    ple_names, ffn_names = ["ple_w_gate", "ple_w_proj"], ["ffn_w1", "ffn_w3", "ffn_w2"]

    dx, d_norm_ple[1], r_ple1, _ = ple_bwd(1, dx, ple1)
    dx, d_norm_ffn[1], r_ffn1, da1, _ = ffn_bwd(1, dx, ffn1)
    r_ple1.scatter(da1)

    do = _glu_bwd_elem(dx, glu_val, glu_sg)
    r_ffn1.scatter(do)
    dw_s5out = _mm_tn("s5_out_dw", s5_g, [do], "col", 1024, 1024, 512, 1024)[0]
    dgy = _mm_nt("s5_out_dx", [do], [W["s5_w_out", 0]], 1024, 1024, [], [F32], _ident)[0]
    keep(ple_names, 1, r_ple1.finish(dgy))
    du, dbd_re, dbd_im, dcd_re, dcd_im, dl_re, dl_im, dd = _s5_bwd(u, dgy, bd_re, bd_im, cd_re, cd_im,
                                                                  lam_re, lam_im, d_full)
    dw_s5in = _mm_tn("s5_in_dw", h1, [du], "row", 512, 1024, 512, 1024)[0]
    r_mix1 = _Reduction("mix1", [dw_s5in, dw_s5out])
    dh1 = _mm_nt("s5_in_dx", [du], [W["s5_w_in", 0]], 512, 1024, [], [F32], _ident)[0]
    dx, d_norm_mix[1] = _rms_bwd("rms_mix1_bwd", dh1, x3, norm_mix[1:2], dx)

    def from_bd(t):
        return jnp.transpose(_diag_blocks(t, T, _S5_C, _S5_P), (2, 0, 1, 3)).reshape(_S5_C, T * _S5_GT, _S5_P)

    def from_cdT(t):
        return _diag_blocks(t, T, _S5_C, _S5_P).reshape(T * _S5_GT, _S5_C, _S5_P)

    da_re, da_im, dlog_dt, dbT_re, dbT_im = _s5_prep_bwd(
        a_re, a_im, log_dt, bT_re, bT_im, dl_re.reshape(a_re.shape), dl_im.reshape(a_re.shape),
        from_bd(dbd_re), from_bd(dbd_im))
    small["s5_a_re"], small["s5_a_im"], small["s5_log_dt"] = da_re[None], da_im[None], dlog_dt.reshape(1, -1)
    small["s5_b_re"] = jnp.transpose(dbT_re, (1, 2, 0))[None]
    small["s5_b_im"] = jnp.transpose(dbT_im, (1, 2, 0))[None]
    small["s5_c_re"], small["s5_c_im"] = from_cdT(dcd_re)[None], from_cdT(dcd_im)[None]

    dx, d_norm_ple[0], r_ple0, dpre0 = ple_bwd(0, dx, ple0)
    r_mix1.scatter(dpre0)
    keep(ffn_names, 1, r_ffn1.finish(dx))
    dx, d_norm_ffn[0], r_ffn0, da0, dhf0 = ffn_bwd(0, dx, ffn0)
    r_ple0.scatter(da0)
    keep(["s5_w_in", "s5_w_out"], 0, r_mix1.finish(da0))
    keep(ple_names, 0, r_ple0.finish(dhf0))

    dw_gmout = _mm_tn("gm_out_dw", gm_m, [dx], "row", 512, 1024, 512, 1024)[0]
    dgm = _mm_nt("gm_out_dx", [dx], [W["gm_w_out", 0]], 512, 1024, [], [F32], _ident)[0]
    r_ffn0.scatter(dgm)
    dz, dws, dbsT, dlng, dlnb = _gmlp_bwd(z, dgm, gm_ln_g, gm_ln_b, gm_w_s[0], bsT)
    dw_gmin = _mm_tn("gm_in_dw", h0, [dz], "col", 1024, 1024, 512, 1024)[0]
    r_mix0 = _Reduction("mix0", [dw_gmin, dw_gmout])
    dh0 = _mm_nt("gm_in_dx", [dz], [W["gm_w_in", 0]], 1024, 1024, [], [F32], _ident)[0]
    dx, d_norm_mix[0] = _rms_bwd("rms_mix0_bwd", dh0, xs, norm_mix[0:1], dx)
    r_mix0.scatter(dx)
    grad_x = dx[None]

    small["norm_mix"], small["norm_ffn"] = jnp.concatenate(d_norm_mix), jnp.concatenate(d_norm_ffn)
    small["norm_ple"], small["norm_final"] = jnp.concatenate(d_norm_ple), d_norm_final[0]
    small["gm_ln_g"], small["gm_ln_b"], small["gm_w_s"] = dlng, dlnb, dws[None]
    small["gm_b_s"] = dbsT[:, :_GM_HEADS].T[None]
    small["s5_d"] = dd

    small_names = [n for n in _WEIGHT_NAMES if n not in _BIG]
    packed = _pack([small[n] for n in small_names], 64)
    summed = _allreduce_small("ar_small", packed.reshape(8, -1, _LANES))
    red_small = dict(zip(small_names, _unpack(summed, [small[n].shape for n in small_names])))
    red_small["s5_d"] = lax.dynamic_slice(red_small["s5_d"], (0, q * (D // 4)), (1, D // 4))

    grads, deltas, new_m, new_v = {}, {}, {}, {}
    my_c = lax.axis_index("c")
    token = summed

    def adamw(n, layer, prev):
        nonlocal token
        kind = _BIG[n]
        R, C = _rc(kind, w[n].shape)
        outs = _adamw_big(f"adamw_{n}{layer}", w[n], mom[n], var[n], layer, reduced[n, layer], kind, R, C,
                          my_c, token, prev)
        token = outs[1]
        return outs

    late = {}
    for n in ffn_names + ple_names:
        late[n] = adamw(n, 1, None)
    for n in ["s5_w_in", "s5_w_out"]:
        grads[n], deltas[n], new_m[n], new_v[n] = adamw(n, 0, None)
    for n in ple_names:
        grads[n], deltas[n], new_m[n], new_v[n] = adamw(n, 0, late[n])
    keep(ffn_names, 0, r_ffn0.finish(token))
    for n in ffn_names:
        grads[n], deltas[n], new_m[n], new_v[n] = adamw(n, 0, late[n])
    keep(["gm_w_in", "gm_w_out"], 0, r_mix0.finish(token))
    for n in ["gm_w_in", "gm_w_out"]:
        grads[n], deltas[n], new_m[n], new_v[n] = adamw(n, 0, None)
    shapes = [w[n].shape for n in small_names]
    dl, mo, vo = _adamw_flat(_pack([w[n] for n in small_names], 64), _pack([red_small[n] for n in small_names], 64),
                             _pack([mom[n] for n in small_names], 64), _pack([var[n] for n in small_names], 64))
    for n, g_, d_, m_, v_ in zip(small_names, [red_small[n] for n in small_names], _unpack(dl, shapes),
                                 _unpack(mo, shapes), _unpack(vo, shapes)):
        grads[n], deltas[n], new_m[n], new_v[n] = g_.reshape(w[n].shape), d_, m_, v_

    return (loss, grad_x, *[grads[n] for n in _WEIGHT_NAMES], *[deltas[n] for n in _WEIGHT_NAMES],
            *[new_m[n] for n in _WEIGHT_NAMES], *[new_v[n] for n in _WEIGHT_NAMES])
```

```python
import functools
import math

import jax
import jax.numpy as jnp
from jax import lax
from jax.experimental import pallas as pl
from jax.experimental.pallas import tpu as pltpu
from jax.experimental.pallas import tpu_sc as plsc

F32 = jnp.float32
_MXU = jnp.bfloat16
_WIRE = jnp.bfloat16
_EPS = 1e-6
_VMEM_LIMIT = 48 * 1024 * 1024
_LANES = 128
_MESH = pl.DeviceIdType.MESH

_LR, _B1, _B2, _AEPS, _WD, _STEP = 0.001, 0.9, 0.999, 1e-08, 0.01, 10

_GM_CHUNK = 128
_GM_HEADS = 16
_S5_GT = 8
_S5_P = 64
_S5_C = 16

_NN = (((1,), (0,)), ((), ()))
_NT = (((1,), (1,)), ((), ()))
_TN = (((0,), (0,)), ((), ()))


def _cparams(sem):
    return pltpu.CompilerParams(dimension_semantics=sem, vmem_limit_bytes=_VMEM_LIMIT)


def _sig(x):
    return 1.0 / (1.0 + jnp.exp(-x))


_GC = math.sqrt(2.0 / math.pi)


def _gelu(x):
    return 0.5 * x * (1.0 + jnp.tanh(_GC * (x + 0.044715 * (x * x * x))))


def _gelu_grad(x):
    t = jnp.tanh(_GC * (x + 0.044715 * (x * x * x)))
    return 0.5 * (1.0 + t) + 0.5 * x * (1.0 - t * t) * (_GC * (1.0 + 3.0 * 0.044715 * x * x))


def _dot(a, b, dn):
    return lax.dot_general(a.astype(_MXU), b.astype(_MXU), dn, preferred_element_type=F32)


class _W:
    def __init__(self, arr, kind):
        self.a, self.kind = arr, kind
        self.R, self.C = arr.shape[2], arr.shape[3]

    def full_shape(self):
        return (2 * self.R, 4 * self.C) if self.kind == "col" else (4 * self.R, 2 * self.C)


def _part_index(kind, R, C, tr, tc, rb, cb):
    nr, nc = R // tr, C // tc
    if kind == "col":
        return cb // nc, rb // nr, rb % nr, cb % nc
    return rb // nr, cb // nc, rb % nr, cb % nc


def _wspec(w, tr, tc, rb_fn, cb_fn):
    assert w.R % tr == 0 and w.C % tc == 0, (w.R, w.C, tr, tc)

    def imap(i, j, k):
        return _part_index(w.kind, w.R, w.C, tr, tc, rb_fn(i, j, k), cb_fn(i, j, k))

    return pl.BlockSpec((None, None, tr, tc), imap)


def _gspec(kind, R, C, tr, tc):
    assert R % tr == 0 and C % tc == 0, (R, C, tr, tc)

    def imap(i, j, k):
        part, half, rbi, cbi = _part_index(kind, R, C, tr, tc, i, j)
        return half, part, rbi, cbi

    return pl.BlockSpec((None, None, tr, tc), imap)


def _mm(name, grid, a_ops, b_ops, pairs, acc_shape, n_acc, extras, outs, epilogue):
    nk = grid[2]
    na, nb, ne, no = len(a_ops), len(b_ops), len(extras), len(outs)

    def body(*refs):
        a_refs = refs[:na]
        b_refs = refs[na:na + nb]
        e_refs = refs[na + nb:na + nb + ne]
        o_refs = refs[na + nb + ne:na + nb + ne + no]
        acc_refs = refs[na + nb + ne + no:]
        k = pl.program_id(2)

        def products():
            sums = [None] * n_acc
            for ai, bi, ci, dn in pairs:
                d = _dot(a_refs[ai][...], b_refs[bi][...], dn)
                sums[ci] = d if sums[ci] is None else sums[ci] + d
            return sums

        def finish(accs):
            res = epilogue(accs, [e[...] for e in e_refs])
            for o, r in zip(o_refs, res):
                o[...] = r.astype(o.dtype)

        if nk == 1:
            finish(products())
            return

        @pl.when(k == 0)
        def _():
            for acc, d in zip(acc_refs, products()):
                acc[...] = d

        @pl.when(jnp.logical_and(k > 0, k < nk - 1))
        def _():
            for acc, d in zip(acc_refs, products()):
                acc[...] += d

        @pl.when(k == nk - 1)
        def _():
            finish([acc[...] + d for acc, d in zip(acc_refs, products())])

    ops = list(a_ops) + list(b_ops) + list(extras)
    return pl.pallas_call(
        body, name=name, grid=grid,
        in_specs=[s for _, s in ops],
        out_specs=[s for _, s in outs],
        out_shape=[s for s, _ in outs],
        scratch_shapes=[pltpu.VMEM(acc_shape, F32) for _ in range(n_acc if nk > 1 else 0)],
        compiler_params=_cparams(("parallel", "parallel", "arbitrary")),
    )(*[a for a, _ in ops])


def _bs(shape, fn):
    return pl.BlockSpec(shape, fn)


def _tile_m(L):
    return min(L, 512)


def _mm_nn(name, x, ws, tk, tn, n_out, extras, outs_sd, epilogue, tm=None, cb_offsets=None):
    M, K = x.shape
    tm = min(M, tm or _tile_m(M))
    grid = (M // tm, n_out // tn, K // tk)
    a_ops = [(x, _bs((tm, tk), lambda i, j, k: (i, k)))]
    cb_offsets = cb_offsets or [0] * len(ws)
    b_ops = [(w.a, _wspec(w, tk, tn, lambda i, j, k: k, (lambda off: lambda i, j, k: j + off)(off)))
             for w, off in zip(ws, cb_offsets)]
    pairs = [(0, bi, bi, _NN) for bi in range(len(ws))]
    mn = _bs((tm, tn), lambda i, j, k: (i, j))
    ex = [(e, mn) for e in extras]
    outs = [(jax.ShapeDtypeStruct((M, n_out), dt), mn) for dt in outs_sd]
    return _mm(name, grid, a_ops, b_ops, pairs, (tm, tn), len(ws), ex, outs, epilogue)


def _mm_nt(name, xs, ws, tn, tk, extras, outs_sd, epilogue, tm=None):
    M, Nw = xs[0].shape
    Kw = ws[0].full_shape()[0]
    tm = min(M, tm or _tile_m(M))
    grid = (M // tm, Kw // tn, Nw // tk)
    a_ops = [(x, _bs((tm, tk), lambda i, j, k: (i, k))) for x in xs]
    b_ops = [(w.a, _wspec(w, tn, tk, lambda i, j, k: j, lambda i, j, k: k)) for w in ws]
    pairs = [(i, i, 0, _NT) for i in range(len(ws))]
    mn = _bs((tm, tn), lambda i, j, k: (i, j))
    ex = [(e, mn) for e in extras]
    outs = [(jax.ShapeDtypeStruct((M, Kw), dt), mn) for dt in outs_sd]
    return _mm(name, grid, a_ops, b_ops, pairs, (tm, tn), 1, ex, outs, epilogue)


def _mm_tn(name, x, dys, kind, R, C, tm, tn, tk=None):
    L, Kw = x.shape
    Nw = dys[0].shape[1]
    tk = tk or min(L, 1024)
    grid = (Kw // tm, Nw // tn, L // tk)
    a_ops = [(x, _bs((tk, tm), lambda i, j, k: (k, i)))]
    b_ops = [(dy, _bs((tk, tn), lambda i, j, k: (k, j))) for dy in dys]
    pairs = [(0, bi, bi, _TN) for bi in range(len(dys))]
    gs = _gspec(kind, R, C, tm, tn)
    outs = [(jax.ShapeDtypeStruct((2, 4, R, C), _WIRE), gs) for _ in dys]
    return _mm(name, grid, a_ops, b_ops, pairs, (tm, tn), len(dys), [], outs, lambda accs, ex: accs)


def _row_tile(L):
    return min(L, 256)


def _rowwise(name, body, ins, outs, L, acc_outs=()):
    tr = _row_tile(L)
    n_in, n_out = len(ins), len(outs)

    def kbody(*refs):
        i_refs, o_refs, a_refs = refs[:n_in], refs[n_in:n_in + n_out], refs[n_in + n_out:]
        res, sums = body(*[r[...] for r in i_refs])
        for o, r in zip(o_refs, res):
            o[...] = r.astype(o.dtype)
        if a_refs:
            @pl.when(pl.program_id(0) == 0)
            def _():
                for a in a_refs:
                    a[...] = jnp.zeros(a.shape, F32)
            for a, s in zip(a_refs, sums):
                a[...] += s

    in_specs = []
    for arr, kind in ins:
        if kind == "row":
            in_specs.append(pl.BlockSpec((tr, arr.shape[1]), lambda i: (i, 0)))
        else:
            in_specs.append(pl.BlockSpec(arr.shape, lambda i: (0, 0)))
    out_specs = [pl.BlockSpec((tr, c), lambda i: (i, 0)) for c, _ in outs]
    out_shape = [jax.ShapeDtypeStruct((L, c), dt) for c, dt in outs]
    out_specs += [pl.BlockSpec((1, c), lambda i: (0, 0)) for c in acc_outs]
    out_shape += [jax.ShapeDtypeStruct((1, c), F32) for c in acc_outs]
    return pl.pallas_call(
        kbody, name=name, grid=(L // tr,), in_specs=in_specs, out_specs=out_specs, out_shape=out_shape,
        compiler_params=_cparams(("arbitrary",)),
    )(*[a for a, _ in ins])


def _rms_fwd(name, x, g):
    def body(xv, gv):
        r = lax.rsqrt(jnp.mean(xv * xv, axis=-1, keepdims=True) + _EPS)
        return [xv * r * gv], []
    return _rowwise(name, body, [(x, "row"), (g, "vec")], [(x.shape[1], _MXU)], x.shape[0])[0]


def _rms_bwd(name, dh, x, g, dres):
    def body(dhv, xv, gv, dr):
        r = lax.rsqrt(jnp.mean(xv * xv, axis=-1, keepdims=True) + _EPS)
        xh = xv * r
        dxh = dhv * gv
        dx = r * (dxh - xh * jnp.mean(dxh * xh, axis=-1, keepdims=True))
        return [dr + dx], [jnp.sum(dhv * xh, axis=0, keepdims=True)]
    D = x.shape[1]
    return _rowwise(name, body, [(dh, "row"), (x, "row"), (g, "vec"), (dres, "row")], [(D, F32)], x.shape[0], [D])


def _loss_head(x, g, target):
    D = x.shape[1]

    def body(xv, gv, tv):
        r = lax.rsqrt(jnp.mean(xv * xv, axis=-1, keepdims=True) + _EPS)
        xh = xv * r
        e = xh * gv - tv
        dy = e * (1.0 / D)
        dxh = dy * gv
        dx = r * (dxh - xh * jnp.mean(dxh * xh, axis=-1, keepdims=True))
        row_loss = 0.5 * jnp.mean(e * e, axis=-1, keepdims=True)
        lsum = jnp.sum(row_loss, axis=0, keepdims=True) + jnp.zeros((1, _LANES), F32)
        return [dx], [jnp.sum(dy * xh, axis=0, keepdims=True), lsum]
    return _rowwise("loss_head", body, [(x, "row"), (g, "vec"), (target, "row")], [(D, F32)], x.shape[0], [D, _LANES])


def _ple_bwd_elem(dx, pp, gt):
    def body(dxv, ppv, gtv):
        gt32 = gtv.astype(F32)
        return [dxv * ppv * gt32 * (1.0 - gt32), dxv * gt32], []
    D = dx.shape[1]
    return _rowwise("ple_bwd_elem", body, [(dx, "row"), (pp, "row"), (gt, "row")], [(D, _MXU), (D, _MXU)], dx.shape[0])


def _glu_bwd_elem(dx, val, sg):
    def body(dxv, vv, sv):
        v32, s32 = vv.astype(F32), sv.astype(F32)
        return [jnp.concatenate([dxv * s32, dxv * v32 * s32 * (1.0 - s32)], axis=1)], []
    D = dx.shape[1]
    return _rowwise("glu_bwd_elem", body, [(dx, "row"), (val, "row"), (sg, "row")], [(2 * D, _MXU)], dx.shape[0])[0]


def _gm_common(z, ln_g, ln_b, wc_bf, bsT):
    W = z.shape[1] // 2
    zu, zv = z[:, :W], z[:, W:]
    u, v = _gelu(zu), _gelu(zv)
    mu = jnp.mean(v, axis=-1, keepdims=True)
    vc = v - mu
    rstd = lax.rsqrt(jnp.mean(vc * vc, axis=-1, keepdims=True) + _EPS)
    vh = vc * rstd
    vn = vh * ln_g + ln_b
    vnb = vn.astype(_MXU)
    svs = []
    for h in range(_GM_HEADS):
        sl = slice(h * _LANES, (h + 1) * _LANES)
        svs.append(_dot(wc_bf[h], vnb[:, sl], _NN) + bsT[:, h:h + 1])
    return zu, zv, u, vh, rstd, vnb, svs


def _causal(w):
    t = lax.broadcasted_iota(jnp.int32, w.shape, w.ndim - 2)
    s = lax.broadcasted_iota(jnp.int32, w.shape, w.ndim - 1)
    return jnp.where(s <= t, w, jnp.zeros_like(w))


def _gmlp_fwd(z, ln_g, ln_b, w_s, bsT):
    L, W2 = z.shape
    W = W2 // 2

    def body(z_ref, g_ref, b_ref, ws_ref, bs_ref, m_ref):
        wc = _causal(ws_ref[...]).astype(_MXU)
        _, _, u, _, _, _, svs = _gm_common(z_ref[...], g_ref[...], b_ref[...], wc, bs_ref[...])
        for h in range(_GM_HEADS):
            sl = slice(h * _LANES, (h + 1) * _LANES)
            m_ref[:, sl] = (u[:, sl] * svs[h]).astype(m_ref.dtype)

    return pl.pallas_call(
        body, name="gmlp_fwd", grid=(L // _GM_CHUNK,),
        in_specs=[pl.BlockSpec((_GM_CHUNK, W2), lambda n: (n, 0)),
                  pl.BlockSpec((1, W), lambda n: (0, 0)), pl.BlockSpec((1, W), lambda n: (0, 0)),
                  pl.BlockSpec(w_s.shape, lambda n: (0, 0, 0)), pl.BlockSpec(bsT.shape, lambda n: (0, 0))],
        out_specs=pl.BlockSpec((_GM_CHUNK, W), lambda n: (n, 0)),
        out_shape=jax.ShapeDtypeStruct((L, W), _MXU),
        compiler_params=_cparams(("arbitrary",)),
    )(z, ln_g, ln_b, w_s, bsT)


def _gmlp_bwd(z, dm, ln_g, ln_b, w_s, bsT):
    L, W2 = z.shape
    W = W2 // 2
    T = _GM_CHUNK

    def body(z_ref, dm_ref, g_ref, b_ref, ws_ref, bs_ref, dz_ref, dws_ref, dbs_ref, dg_ref, db_ref):
        @pl.when(pl.program_id(0) == 0)
        def _():
            dws_ref[...] = jnp.zeros(dws_ref.shape, F32)
            dbs_ref[...] = jnp.zeros(dbs_ref.shape, F32)
            dg_ref[...] = jnp.zeros(dg_ref.shape, F32)
            db_ref[...] = jnp.zeros(db_ref.shape, F32)

        wc = _causal(ws_ref[...]).astype(_MXU)
        ln_g_v = g_ref[...]
        zu, zv, u, vh, rstd, vnb, svs = _gm_common(z_ref[...], ln_g_v, b_ref[...], wc, bs_ref[...])
        dmv = dm_ref[...]
        lane = lax.broadcasted_iota(jnp.int32, (T, _LANES), 1)
        dbs = jnp.zeros((T, _LANES), F32)
        dvn_parts = []
        for h in range(_GM_HEADS):
            sl = slice(h * _LANES, (h + 1) * _LANES)
            dsv = dmv[:, sl] * u[:, sl]
            dz_ref[:, sl] = (dmv[:, sl] * svs[h] * _gelu_grad(zu[:, sl])).astype(dz_ref.dtype)
            dbs = dbs + jnp.where(lane == h, jnp.sum(dsv, axis=1, keepdims=True), 0.0)
            dsvb = dsv.astype(_MXU)
            dws_ref[h] += _dot(dsvb, vnb[:, sl], _NT)
            dvn_parts.append(_dot(wc[h], dsvb, _TN))
        dbs_ref[...] += dbs
        dvn = jnp.concatenate(dvn_parts, axis=1)
        dg_ref[...] += jnp.sum(dvn * vh, axis=0, keepdims=True)
        db_ref[...] += jnp.sum(dvn, axis=0, keepdims=True)
        dxh = dvn * ln_g_v
        dv = rstd * (dxh - jnp.mean(dxh, axis=-1, keepdims=True) - vh * jnp.mean(dxh * vh, axis=-1, keepdims=True))
        dz_ref[:, W:] = (dv * _gelu_grad(zv)).astype(dz_ref.dtype)

        @pl.when(pl.program_id(0) == pl.num_programs(0) - 1)
        def _():
            dws_ref[...] = _causal(dws_ref[...])

    return pl.pallas_call(
        body, name="gmlp_bwd", grid=(L // T,),
        in_specs=[pl.BlockSpec((T, W2), lambda n: (n, 0)), pl.BlockSpec((T, W), lambda n: (n, 0)),
                  pl.BlockSpec((1, W), lambda n: (0, 0)), pl.BlockSpec((1, W), lambda n: (0, 0)),
                  pl.BlockSpec(w_s.shape, lambda n: (0, 0, 0)), pl.BlockSpec(bsT.shape, lambda n: (0, 0))],
        out_specs=[pl.BlockSpec((T, W2), lambda n: (n, 0)),
                   pl.BlockSpec(w_s.shape, lambda n: (0, 0, 0)), pl.BlockSpec((T, _LANES), lambda n: (0, 0)),
                   pl.BlockSpec((1, W), lambda n: (0, 0)), pl.BlockSpec((1, W), lambda n: (0, 0))],
        out_shape=[jax.ShapeDtypeStruct((L, W2), _MXU), jax.ShapeDtypeStruct(w_s.shape, F32),
                   jax.ShapeDtypeStruct((T, _LANES), F32),
                   jax.ShapeDtypeStruct((1, W), F32), jax.ShapeDtypeStruct((1, W), F32)],
        compiler_params=_cparams(("arbitrary",)),
    )(z, dm, ln_g, ln_b, w_s, bsT)


def _s5_prep_math(a_re, a_im, log_dt):
    dt = jnp.exp(log_dt)
    xr, xi = a_re * dt, a_im * dt
    e = jnp.exp(xr)
    lbr, lbi = e * jnp.cos(xi), e * jnp.sin(xi)
    dn = a_re * a_re + a_im * a_im
    nr, ni = lbr - 1.0, lbi
    pr, pi = nr * a_re + ni * a_im, ni * a_re - nr * a_im
    return dt, lbr, lbi, dn, nr, ni, pr, pi


def _vm():
    return pl.BlockSpec(memory_space=pltpu.VMEM)


def _s5_prep(a_re, a_im, log_dt, b_re, b_im):
    def body(ar_ref, ai_ref, ld_ref, br_ref, bi_ref, lbr_ref, lbi_ref, Br_ref, Bi_ref):
        _, lbr, lbi, dn, _, _, pr, pi = _s5_prep_math(ar_ref[...], ai_ref[...], ld_ref[...])
        cr, ci = (pr / dn)[:, None, :], (pi / dn)[:, None, :]
        lbr_ref[...] = lbr
        lbi_ref[...] = lbi
        br, bi = br_ref[...], bi_ref[...]
        Br_ref[...] = cr * br - ci * bi
        Bi_ref[...] = cr * bi + ci * br

    sd = jax.ShapeDtypeStruct
    return pl.pallas_call(
        body, name="s5_prep", in_specs=[_vm()] * 5, out_specs=[_vm()] * 4,
        out_shape=[sd(a_re.shape, F32), sd(a_re.shape, F32), sd(b_re.shape, F32), sd(b_re.shape, F32)],
    )(a_re, a_im, log_dt, b_re, b_im)


def _s5_prep_bwd(a_re, a_im, log_dt, b_re, b_im, dlbr_s, dlbi_s, dBr, dBi):
    def body(ar_ref, ai_ref, ld_ref, br_ref, bi_ref, dlr_ref, dli_ref, dBr_ref, dBi_ref,
             dar_ref, dai_ref, dld_ref, dbr_ref, dbi_ref):
        a_re_v, a_im_v = ar_ref[...], ai_ref[...]
        dt, lbr, lbi, dn, nr, ni, pr, pi = _s5_prep_math(a_re_v, a_im_v, ld_ref[...])
        cr, ci = (pr / dn)[:, None, :], (pi / dn)[:, None, :]
        br, bi, dBr_v, dBi_v = br_ref[...], bi_ref[...], dBr_ref[...], dBi_ref[...]
        dbr_ref[...] = cr * dBr_v + ci * dBi_v
        dbi_ref[...] = cr * dBi_v - ci * dBr_v
        dcr = jnp.sum(br * dBr_v + bi * dBi_v, axis=1)
        dci = jnp.sum(br * dBi_v - bi * dBr_v, axis=1)
        dpr, dpi = dcr / dn, dci / dn
        ddn = -(dcr * pr + dci * pi) / (dn * dn)
        dnr = dpr * a_re_v - dpi * a_im_v
        dni = dpr * a_im_v + dpi * a_re_v
        dlbr = dlr_ref[...] + dnr
        dlbi = dli_ref[...] + dni
        dxr = dlbr * lbr + dlbi * lbi
        dxi = dlbi * lbr - dlbr * lbi
        dar_ref[...] = dpr * nr + dpi * ni + 2.0 * ddn * a_re_v + dxr * dt
        dai_ref[...] = dpr * ni - dpi * nr + 2.0 * ddn * a_im_v + dxi * dt
        dld_ref[...] = jnp.sum(dxr * a_re_v + dxi * a_im_v, axis=0, keepdims=True) * dt

    sd = jax.ShapeDtypeStruct
    return pl.pallas_call(
        body, name="s5_prep_bwd", in_specs=[_vm()] * 9, out_specs=[_vm()] * 5,
        out_shape=[sd(a_re.shape, F32), sd(a_re.shape, F32), sd(log_dt.shape, F32),
                   sd(b_re.shape, F32), sd(b_re.shape, F32)],
    )(a_re, a_im, log_dt, b_re, b_im, dlbr_s, dlbi_s, dBr, dBi)


def _shift_rows(v, down):
    n = v.shape[0]
    rolled = pltpu.roll(v, 1 if down else n - 1, 0)
    row = lax.broadcasted_iota(jnp.int32, v.shape, 0)
    return jnp.where(row == (0 if down else n - 1), 0.0, rolled)


def _cmul(ar, ai, br, bi):
    return ar * br - ai * bi, ar * bi + ai * br


_SEG = 8
_UNROLL = 8


def _seg_rows(k):
    if isinstance(k, int):
        return pl.ds(k * _SEG, _SEG)
    return pl.ds(pl.multiple_of(k * _SEG, _SEG), _SEG)


def _unrolled(n, step, init):
    main = n // _UNROLL

    def trip(kk, s):
        for uu in range(_UNROLL):
            s = step(kk * _UNROLL + uu, s)
        return s

    s = lax.fori_loop(0, main, trip, init)
    for r in range(main * _UNROLL, n):
        s = step(r, s)
    return s


def _interleave(src_ref, dst_ref, nk):
    def step(k, carry):
        dst_ref[_seg_rows(k), :] = src_ref[pl.ds(k, _SEG, stride=nk), :]
        return carry
    _unrolled(nk, step, 0)


def _deinterleave(src_ref, dst_ref, nk):
    def step(k, carry):
        dst_ref[pl.ds(k, _SEG, stride=nk), :] = src_ref[_seg_rows(k), :]
        return carry
    _unrolled(nk, step, 0)


def _segment_inits(er, ei, ar, ai, nk, down):
    pr, pi = ar, ai
    for _ in range(int(math.log2(nk))):
        pr, pi = _cmul(pr, pi, pr, pi)
    fr, fi = er, ei
    for _ in range(_SEG - 1):
        sr, si = _shift_rows(fr, down), _shift_rows(fi, down)
        mr, mi = _cmul(pr, pi, sr, si)
        fr, fi = er + mr, ei + mi
    return _shift_rows(fr, down), _shift_rows(fi, down)


def _scan_states(x_re, x_im, ar, ai, nk):
    lanes = ar.shape[1]

    def step(k, s):
        rows = _seg_rows(k)
        mr, mi = _cmul(ar, ai, s[0], s[1])
        return mr + x_re[rows, :], mi + x_im[rows, :]

    zero = jnp.zeros((_SEG, lanes), F32)
    er, ei = _unrolled(nk, step, (zero, zero))
    ir, ii = _segment_inits(er, ei, ar, ai, nk, True)

    def step2(k, s):
        rows = _seg_rows(k)
        mr, mi = _cmul(ar, ai, s[0], s[1])
        nr, ni = mr + x_re[rows, :], mi + x_im[rows, :]
        x_re[rows, :] = nr
        x_im[rows, :] = ni
        return nr, ni

    _unrolled(nk, step2, (ir, ii))


def _s5_tile_fwd(u, bd_re, bd_im, cd_re, cd_im, ar, ai, d, s_re, s_im, nk):
    s_re[...] = _dot(u, bd_re, _NN)
    s_im[...] = _dot(u, bd_im, _NN)
    _scan_states(s_re, s_im, ar, ai, nk)
    return _dot(s_re[...], cd_re, _NN) - _dot(s_im[...], cd_im, _NN) + d * u


def _s5_specs(L, T):
    lanes = _S5_GT * _S5_P
    u_spec = pl.BlockSpec((L, _LANES), lambda t: (0, t))
    bd_spec = pl.BlockSpec((None, _LANES, lanes), lambda t: (t, 0, 0))
    cd_spec = pl.BlockSpec((None, lanes, _LANES), lambda t: (t, 0, 0))
    lam_spec = pl.BlockSpec((None, 1, lanes), lambda t: (t, 0, 0))
    d_spec = pl.BlockSpec((1, _LANES), lambda t: (0, t))
    return lanes, u_spec, bd_spec, cd_spec, lam_spec, d_spec


def _s5_fwd(u, bd_re, bd_im, cd_re, cd_im, lam_re, lam_im, d):
    L, Wd = u.shape
    T = Wd // _LANES
    nk = L // _SEG
    lanes, u_spec, bd_spec, cd_spec, lam_spec, d_spec = _s5_specs(L, T)

    def body(u_ref, bdr, bdi, cdr, cdi, lr, li, d_ref, g_ref, s_re, s_im, up, tmp):
        ar = jnp.broadcast_to(lr[...], (_SEG, lanes))
        ai = jnp.broadcast_to(li[...], (_SEG, lanes))
        _interleave(u_ref, up, nk)
        y = _s5_tile_fwd(up[...], bdr[...], bdi[...], cdr[...], cdi[...], ar, ai, d_ref[...], s_re, s_im, nk)
        up[...] = _gelu(y)
        _deinterleave(up, tmp, nk)
        g_ref[...] = tmp[...].astype(g_ref.dtype)

    return pl.pallas_call(
        body, name="s5_fwd", grid=(T,),
        in_specs=[u_spec, bd_spec, bd_spec, cd_spec, cd_spec, lam_spec, lam_spec, d_spec],
        out_specs=u_spec, out_shape=jax.ShapeDtypeStruct((L, Wd), _MXU),
        scratch_shapes=[pltpu.VMEM((L, lanes), F32) for _ in range(2)]
        + [pltpu.VMEM((L, _LANES), F32) for _ in range(2)],
        compiler_params=_cparams(("arbitrary",)),
    )(u, bd_re, bd_im, cd_re, cd_im, lam_re, lam_im, d)


def _s5_bwd(u, dg, bd_re, bd_im, cd_re, cd_im, lam_re, lam_im, d):
    L, Wd = u.shape
    T = Wd // _LANES
    nk = L // _SEG
    lanes, u_spec, bd_spec, cd_spec, lam_spec, d_spec = _s5_specs(L, T)

    def body(u_ref, dg_ref, bdr, bdi, cdr, cdi, lr, li, d_ref,
             du_ref, dbdr, dbdi, dcdr, dcdi, dlr, dli, dd_ref, s_re, s_im, g_re, g_im, up, dgp, tmp):
        ar = jnp.broadcast_to(lr[...], (_SEG, lanes))
        ai = jnp.broadcast_to(li[...], (_SEG, lanes))
        _interleave(u_ref, up, nk)
        _interleave(dg_ref, dgp, nk)
        uv, dv = up[...], d_ref[...]
        y = _s5_tile_fwd(uv, bdr[...], bdi[...], cdr[...], cdi[...], ar, ai, dv, s_re, s_im, nk)
        dy = dgp[...] * _gelu_grad(y)
        dd_ref[...] = jnp.sum(dy * uv, axis=0, keepdims=True)
        dyb = dy.astype(_MXU)
        dcdr[...] = _dot(dyb, s_re[...], _TN)
        dcdi[...] = -_dot(dyb, s_im[...], _TN)
        g_re[...] = _dot(dyb, cdr[...], _NT)
        g_im[...] = -_dot(dyb, cdi[...], _NT)

        nai = -ai

        def step(j, s):
            rows = _seg_rows(nk - 1 - j)
            mr, mi = _cmul(ar, nai, s[0], s[1])
            return mr + g_re[rows, :], mi + g_im[rows, :]

        zero = jnp.zeros((_SEG, lanes), F32)
        er, ei = _unrolled(nk, step, (zero, zero))
        ir, ii = _segment_inits(er, ei, ar, nai, nk, False)

        def acc_lam(gr, gi, pr, pi, acc):
            return acc[0] + gr * pr + gi * pi, acc[1] + gi * pr - gr * pi

        def step2(j, carry):
            s, acc = carry
            k = nk - 1 - j
            rows = _seg_rows(k)
            mr, mi = _cmul(ar, nai, s[0], s[1])
            nr, ni = mr + g_re[rows, :], mi + g_im[rows, :]
            g_re[rows, :] = nr
            g_im[rows, :] = ni
            prev = _seg_rows(k - 1)
            return (nr, ni), acc_lam(nr, ni, s_re[prev, :], s_im[prev, :], acc)

        (g0r, g0i), acc = _unrolled(nk - 1, step2, ((ir, ii), (zero, zero)))
        first = _seg_rows(0)
        mr, mi = _cmul(ar, nai, g0r, g0i)
        nr, ni = mr + g_re[first, :], mi + g_im[first, :]
        g_re[first, :] = nr
        g_im[first, :] = ni
        last = _seg_rows(nk - 1)
        acc = acc_lam(nr, ni, _shift_rows(s_re[last, :], True), _shift_rows(s_im[last, :], True), acc)
        dlr[...] = jnp.sum(acc[0], axis=0, keepdims=True)
        dli[...] = jnp.sum(acc[1], axis=0, keepdims=True)

        gtr, gti = g_re[...].astype(_MXU), g_im[...].astype(_MXU)
        ub = uv.astype(_MXU)
        dbdr[...] = _dot(ub, gtr, _TN)
        dbdi[...] = _dot(ub, gti, _TN)
        dgp[...] = _dot(gtr, bdr[...], _NT) + _dot(gti, bdi[...], _NT) + dy * dv
        _deinterleave(dgp, tmp, nk)
        du_ref[...] = tmp[...].astype(du_ref.dtype)

    sd = jax.ShapeDtypeStruct
    big = sd((T, _LANES, lanes), F32)
    return pl.pallas_call(
        body, name="s5_bwd", grid=(T,),
        in_specs=[u_spec, u_spec, bd_spec, bd_spec, cd_spec, cd_spec, lam_spec, lam_spec, d_spec],
        out_specs=[u_spec, bd_spec, bd_spec, bd_spec, bd_spec, lam_spec, lam_spec, d_spec],
        out_shape=[sd((L, Wd), _MXU), big, big, big, big, sd((T, 1, lanes), F32), sd((T, 1, lanes), F32),
                   sd((1, Wd), F32)],
        scratch_shapes=[pltpu.VMEM((L, lanes), F32) for _ in range(4)]
        + [pltpu.VMEM((L, _LANES), F32) for _ in range(3)],
        compiler_params=_cparams(("arbitrary",)),
    )(u, dg, bd_re, bd_im, cd_re, cd_im, lam_re, lam_im, d)


def _half_tile(R, few_arrays=False):
    for t in ((512, 704, 128) if few_arrays else (256, 352, 128)):
        if R % t == 0:
            return t
    raise ValueError(R)


def _cast_shard(name, w, layer, kind, R, C):
    tr = _half_tile(R, True)
    nr = R // tr

    def body(w_ref, o_ref):
        o_ref[...] = w_ref[...].astype(o_ref.dtype)

    if kind == "col":
        in_map = lambda h, i: (layer, h * nr + i, 0)
    else:
        in_map = lambda h, i: (layer, i, h)
    return pl.pallas_call(
        body, name=name, grid=(2, nr), in_specs=[pl.BlockSpec((None, tr, C), in_map)],
        out_specs=pl.BlockSpec((None, tr, C), lambda h, i: (h, i, 0)),
        out_shape=jax.ShapeDtypeStruct((2, R, C), _WIRE),
        compiler_params=_cparams(("arbitrary", "arbitrary")),
    )(w)


def _adam_math(w, g, m, v):
    m2 = _B1 * m + (1.0 - _B1) * g
    v2 = _B2 * v + (1.0 - _B2) * (g * g)
    m_hat = m2 / (1.0 - _B1 ** _STEP)
    v_hat = v2 / (1.0 - _B2 ** _STEP)
    delta = -_LR * (m_hat / (jnp.sqrt(v_hat) + _AEPS) + _WD * w)
    return delta, m2, v2


def _adamw_big(name, w, m, v, layer, pair, kind, R, C, c, after, prev):
    tr = _half_tile(R)
    nr = R // tr

    def body(c_ref, w_ref, m_ref, v_ref, own_ref, other_ref, *rest):
        go_ref, d_ref, mo_ref, vo_ref = rest[-4:]
        g = jnp.where(pl.program_id(0) == c_ref[0], own_ref[...], other_ref[...])
        delta, m2, v2 = _adam_math(w_ref[...], g, m_ref[...], v_ref[...])
        go_ref[...] = g
        d_ref[...] = delta
        mo_ref[...] = m2
        vo_ref[...] = v2

    if kind == "col":
        nat = pl.BlockSpec((None, tr, C), lambda h, i, c_ref: (layer, h * nr + i, 0))
    else:
        nat = pl.BlockSpec((None, tr, C), lambda h, i, c_ref: (layer, i, h))

    def gspec(own):
        return pl.BlockSpec((tr, C), lambda h, i, c_ref: (jnp.where((h == c_ref[0]) == own, i, 0), 0))

    carried = list(prev) if prev is not None else []
    gs = pltpu.PrefetchScalarGridSpec(
        num_scalar_prefetch=1, grid=(2, nr),
        in_specs=[nat, nat, nat, gspec(True), gspec(False), _any()] + [_any()] * len(carried),
        out_specs=[nat] * 4)
    sd = jax.ShapeDtypeStruct(w.shape, F32)
    return pl.pallas_call(
        body, name=name, grid_spec=gs, out_shape=[sd] * 4,
        input_output_aliases={7 + k: k for k in range(len(carried))},
        compiler_params=_cparams(("arbitrary", "arbitrary")),
    )(c.astype(jnp.int32).reshape(1), w, m, v, pair[0], pair[1], after, *carried)


def _adamw_flat(w, g, m, v):
    rows = w.shape[0]
    tr = rows // 8 if rows % 64 == 0 else rows

    def body(w_ref, g_ref, m_ref, v_ref, d_ref, mo_ref, vo_ref):
        delta, m2, v2 = _adam_math(w_ref[...], g_ref[...], m_ref[...], v_ref[...])
        d_ref[...] = delta
        mo_ref[...] = m2
        vo_ref[...] = v2

    spec = pl.BlockSpec((tr, _LANES), lambda i: (i, 0))
    sd = jax.ShapeDtypeStruct(w.shape, F32)
    return pl.pallas_call(
        body, name="adamw_small", grid=(rows // tr,), in_specs=[spec] * 4, out_specs=[spec] * 3,
        out_shape=[sd] * 3, compiler_params=_cparams(("arbitrary",)),
    )(w, g, m, v)


def _place():
    x, y, c = lax.axis_index("x"), lax.axis_index("y"), lax.axis_index("c")
    chips = [(1 - x, y), (x, 1 - y), (1 - x, 1 - y)]
    return x, y, c, 2 * x + y, chips


def _any():
    return pl.BlockSpec(memory_space=pl.ANY)


def _remote(src, dst, ssem, rsem, dev):
    return pltpu.make_async_remote_copy(src_ref=src, dst_ref=dst, send_sem=ssem, recv_sem=rsem,
                                        device_id=dev, device_id_type=_MESH)


def _allgather(name, shards):
    n = len(shards)

    def body(*refs):
        s_refs, g_refs = refs[:n], refs[n:2 * n]
        send0, recv0, send1, recv1, send2, recv2 = refs[2 * n:]
        x, y, c, q, chips = _place()
        sib = (x, y, 1 - c)
        _handshake([sib] + [(rx, ry, c) for rx, ry in chips])
        own = [_remote(s_refs[a], g_refs[a].at[q], send0.at[a], recv0.at[a], sib) for a in range(n)]
        first, passed = [], []
        for a in range(n):
            for j, (rx, ry) in enumerate(chips):
                k = 3 * a + j
                first.append(_remote(s_refs[a].at[c], g_refs[a].at[q, c], send1.at[k], recv1.at[k], (rx, ry, c)))
                land = g_refs[a].at[2 * rx + ry, c]
                passed.append(_remote(land, land, send2.at[k], recv2.at[k], sib))
        for cp in first + own:
            cp.start()
        for a in range(n):
            for j, (rx, ry) in enumerate(chips):
                k = 3 * a + j
                land = g_refs[a].at[2 * rx + ry, c]
                _remote(land, land, send1.at[k], recv1.at[k], (rx, ry, c)).wait_recv()
                passed[k].start()
        for a in range(n):
            for j, (rx, ry) in enumerate(chips):
                k = 3 * a + j
                other = g_refs[a].at[2 * rx + ry, 1 - c]
                _remote(other, other, send2.at[k], recv2.at[k], sib).wait_recv()
        for cp in own:
            cp.wait()
        for cp in first + passed:
            cp.wait_send()

    return _sequencer(name, _ID_BOTH, body, shards,
                      [jax.ShapeDtypeStruct((4,) + s.shape, s.dtype) for s in shards], [n, n] + [3 * n] * 4)


def _handshake(peers):
    barrier = pltpu.get_barrier_semaphore()
    for peer in peers:
        pl.semaphore_signal(barrier, inc=1, device_id=peer, device_id_type=_MESH)
    pl.semaphore_wait(barrier, len(peers))


_ID_SIBLING, _ID_CHIPS, _ID_BOTH = 1, 2, 3


def _sequencer(name, collective_id, body, ins, out_types, sem_counts):
    mesh = plsc.ScalarSubcoreMesh(axis_name="seq", num_cores=1)
    return pl.kernel(
        body, name=name, out_type=out_types, mesh=mesh,
        scratch_types=[pltpu.SemaphoreType.DMA((k,)) for k in sem_counts],
        compiler_params=pltpu.CompilerParams(collective_id=collective_id),
    )(*ins)


def _swap_halves(name, grads):
    n = len(grads)

    def body(*refs):
        g_refs, t_refs = refs[:n], refs[n:2 * n]
        send, recv = refs[2 * n:]
        x, y, c, _, _ = _place()
        _handshake([(x, y, 1 - c)])
        cps = [_remote(g_refs[a].at[1 - c], t_refs[a], send.at[a], recv.at[a], (x, y, 1 - c)) for a in range(n)]
        for cp in cps:
            cp.start()
        for cp in cps:
            cp.wait()

    return _sequencer(name, _ID_SIBLING, body, grads,
                      [jax.ShapeDtypeStruct(g.shape[1:], g.dtype) for g in grads], [n, n])


def _chip_sum(name, g, t, after):
    _, _, R, C = g.shape
    tr = _half_tile(R, True)

    def body(c_ref, g_ref, t_ref, after_ref, o_ref):
        o_ref[...] = (g_ref[...].astype(F32) + t_ref[...].astype(F32)).astype(o_ref.dtype)

    c = lax.axis_index("c").astype(jnp.int32).reshape(1)
    gs = pltpu.PrefetchScalarGridSpec(
        num_scalar_prefetch=1, grid=(4, R // tr),
        in_specs=[pl.BlockSpec((None, None, tr, C), lambda r, i, c_ref: (c_ref[0], r, i, 0)),
                  pl.BlockSpec((None, tr, C), lambda r, i, c_ref: (r, i, 0)), _any()],
        out_specs=pl.BlockSpec((None, tr, C), lambda r, i, c_ref: (r, i, 0)))
    return pl.pallas_call(
        body, name=name, grid_spec=gs, out_shape=jax.ShapeDtypeStruct((4, R, C), _WIRE),
        compiler_params=_cparams(("arbitrary", "arbitrary")),
    )(c, g, t, after)


def _scatter_parts(name, parts):
    n = len(parts)

    def body(*refs):
        p_refs, t_refs = refs[:n], refs[n:2 * n]
        send, recv = refs[2 * n:]
        x, y, c, q, chips = _place()
        _handshake([(rx, ry, c) for rx, ry in chips])
        cps = []
        for a in range(n):
            for j, (rx, ry) in enumerate(chips):
                k = 3 * a + j
                cps.append(_remote(p_refs[a].at[2 * rx + ry], t_refs[a].at[q], send.at[k], recv.at[k], (rx, ry, c)))
        for cp in cps:
            cp.start()
        for a in range(n):
            for j, (rx, ry) in enumerate(chips):
                k = 3 * a + j
                land = t_refs[a].at[2 * rx + ry]
                _remote(land, land, send.at[k], recv.at[k], (rx, ry, c)).wait_recv()
        for cp in cps:
            cp.wait_send()

    return _sequencer(name, _ID_CHIPS, body, parts,
                      [jax.ShapeDtypeStruct(p.shape, p.dtype) for p in parts], [3 * n, 3 * n])


def _sum_parts(name, p, t, where, after):
    _, R, C = t.shape
    tr = _half_tile(R, True)

    def body(w_ref, p_ref, t0_ref, t1_ref, t2_ref, after_ref, o_ref):
        o_ref[...] = (p_ref[...].astype(F32) + t0_ref[...].astype(F32)
                      + t1_ref[...].astype(F32) + t2_ref[...].astype(F32))

    def part(slot):
        return pl.BlockSpec((None, tr, C), lambda i, w_ref: (w_ref[slot], i, 0))

    gs = pltpu.PrefetchScalarGridSpec(
        num_scalar_prefetch=1, grid=(R // tr,), in_specs=[part(0), part(1), part(2), part(3), _any()],
        out_specs=pl.BlockSpec((tr, C), lambda i, w_ref: (i, 0)))
    return pl.pallas_call(
        body, name=name, grid_spec=gs, out_shape=jax.ShapeDtypeStruct((R, C), F32),
        compiler_params=_cparams(("arbitrary",)),
    )(where, p, t, t, t, after)


def _send_halves(name, halves):
    n = len(halves)

    def body(*refs):
        h_refs, o_refs = refs[:n], refs[n:2 * n]
        send, recv = refs[2 * n:]
        x, y, c, _, _ = _place()
        _handshake([(x, y, 1 - c)])
        cps = [_remote(h_refs[a], o_refs[a], send.at[a], recv.at[a], (x, y, 1 - c)) for a in range(n)]
        for cp in cps:
            cp.start()
        for cp in cps:
            cp.wait()

    return _sequencer(name, _ID_SIBLING, body, halves,
                      [jax.ShapeDtypeStruct(h.shape, h.dtype) for h in halves], [n, n])


class _Reduction:
    def __init__(self, tag, grads):
        self.tag, self.grads = tag, grads
        self.swapped = _swap_halves(f"rs_swap_{tag}", grads)

    def scatter(self, after):
        self.parts = [_chip_sum(f"rs_chipsum_{self.tag}_{a}", g, t, after)
                      for a, (g, t) in enumerate(zip(self.grads, self.swapped))]
        self.landed = _scatter_parts(f"rs_scatter_{self.tag}", self.parts)

    def finish(self, after):
        x, y, c, q, chips = _place()
        where = jnp.stack([q] + [2 * rx + ry for rx, ry in chips]).astype(jnp.int32)
        halves = [_sum_parts(f"rs_sum_{self.tag}_{a}", p, t, where, after)
                  for a, (p, t) in enumerate(zip(self.parts, self.landed))]
        return list(zip(halves, _send_halves(f"rs_join_{self.tag}", halves)))


def _allreduce_small(name, v):
    _, R, _ = v.shape

    def body(v_ref, o_ref, land, acc, send1, recv1, send2, recv2):
        x, y, c = lax.axis_index("x"), lax.axis_index("y"), lax.axis_index("c")
        me = 4 * x + 2 * y + c
        peers = []
        for k in range(1, 8):
            dx, dy, dc = (k >> 2) & 1, (k >> 1) & 1, k & 1
            px, py, pc = (1 - x if dx else x), (1 - y if dy else y), (1 - c if dc else c)
            peers.append((k, (px, py, pc), 4 * px + 2 * py + pc))
        land[me] = v_ref[me]
        out1 = [_remote(v_ref.at[pid], land.at[me], send1.at[k], recv1.at[k], dev) for k, dev, pid in peers]
        for cp in out1:
            cp.start()
        for k, dev, pid in peers:
            _remote(land.at[pid], land.at[pid], send1.at[k], recv1.at[k], dev).wait_recv()
        total = land[0]
        for j in range(1, 8):
            total = total + land[j]
        acc[...] = total
        o_ref[me] = total
        out2 = [_remote(acc, o_ref.at[me], send2.at[k], recv2.at[k], dev) for k, dev, pid in peers]
        for cp in out2:
            cp.start()
        for k, dev, pid in peers:
            _remote(o_ref.at[pid], o_ref.at[pid], send2.at[k], recv2.at[k], dev).wait_recv()
        for cp in out1 + out2:
            cp.wait_send()

    return pl.pallas_call(
        body, name=name, in_specs=[_vm()], out_specs=_vm(),
        out_shape=jax.ShapeDtypeStruct(v.shape, F32),
        scratch_shapes=[pltpu.VMEM(v.shape, F32), pltpu.VMEM((R, _LANES), F32)]
        + [pltpu.SemaphoreType.DMA((8,)) for _ in range(4)],
        compiler_params=pltpu.CompilerParams(vmem_limit_bytes=_VMEM_LIMIT),
    )(v)


_WEIGHT_NAMES = ['norm_mix', 'norm_ffn', 'norm_ple', 'norm_final', 'gm_w_in', 'gm_ln_g', 'gm_ln_b', 'gm_w_s',
                 'gm_b_s', 'gm_w_out', 's5_w_in', 's5_a_re', 's5_a_im', 's5_log_dt', 's5_b_re', 's5_b_im',
                 's5_c_re', 's5_c_im', 's5_d', 's5_w_out', 'ffn_w1', 'ffn_w3', 'ffn_w2', 'ple_w_gate', 'ple_w_proj']
_BIG = {'gm_w_in': 'col', 'gm_w_out': 'row', 's5_w_in': 'row', 's5_w_out': 'col', 'ffn_w1': 'col',
        'ffn_w3': 'col', 'ffn_w2': 'row', 'ple_w_gate': 'row', 'ple_w_proj': 'col'}


_VIEW = {'s5_a_re': (0, 2, 1), 's5_a_im': (0, 2, 1), 's5_b_re': (0, 2, 3, 1), 's5_b_im': (0, 2, 3, 1),
         's5_c_re': (0, 2, 3, 1), 's5_c_im': (0, 2, 3, 1)}


def _to_view(name, a):
    return jnp.transpose(a, _VIEW[name]) if name in _VIEW else a


def _from_view(name, a):
    if name not in _VIEW:
        return a
    perm = _VIEW[name]
    return jnp.transpose(a, [perm.index(i) for i in range(len(perm))])


def _rc(kind, shard_shape):
    rows, cols = shard_shape[-2:]
    return (rows // 2, cols) if kind == "col" else (rows, cols // 2)


def _pack(vecs, rows_multiple):
    flat = jnp.concatenate([a.reshape(-1).astype(F32) for a in vecs])
    unit = rows_multiple * _LANES
    pad = (-flat.shape[0]) % unit
    return jnp.pad(flat, (0, pad)).reshape(-1, _LANES)


def _unpack(buf, shapes):
    flat = buf.reshape(-1)
    out, off = [], 0
    for s in shapes:
        n = math.prod(s)
        out.append(flat[off:off + n].reshape(s))
        off += n
    return out


def _ident(accs, ex):
    return accs


def _add_resid(accs, ex):
    return [accs[0] + ex[0]]


def _swiglu_epi(accs, ex):
    a, b = accs
    return [a, b, a * _sig(a) * b]


def _swiglu_bwd_epi(accs, ex):
    df = accs[0]
    a, b = ex[0].astype(F32), ex[1].astype(F32)
    sa = _sig(a)
    return [df * b * (sa * (1.0 + a * (1.0 - sa))), df * (a * sa)]


def _ple_epi(accs, ex):
    gt = _sig(accs[0])
    return [ex[0] + gt * ex[1], gt]


def _glu_epi(accs, ex):
    val, sg = accs[0], _sig(accs[1])
    return [ex[0] + val * sg, val, sg]


def _block_diag(t):
    eye = jnp.eye(_S5_GT, dtype=t.dtype)
    return t[:, :, :, None, :] * eye[None, :, None, :, None]


def _diag_blocks(t, T, a, b):
    d = jnp.diagonal(t.reshape(T, _S5_GT, a, _S5_GT, b), axis1=1, axis2=3)
    return jnp.transpose(d, (0, 3, 1, 2))


def kernel(x, p, norm_mix, norm_ffn, norm_ple, norm_final, gm_w_in, gm_ln_g, gm_ln_b, gm_w_s, gm_b_s, gm_w_out, s5_w_in, s5_a_re, s5_a_im, s5_log_dt, s5_b_re, s5_b_im, s5_c_re, s5_c_im, s5_d, s5_w_out, ffn_w1, ffn_w3, ffn_w2, ple_w_gate, ple_w_proj, loss_target, m_norm_mix, m_norm_ffn, m_norm_ple, m_norm_final, m_gm_w_in, m_gm_ln_g, m_gm_ln_b, m_gm_w_s, m_gm_b_s, m_gm_w_out, m_s5_w_in, m_s5_a_re, m_s5_a_im, m_s5_log_dt, m_s5_b_re, m_s5_b_im, m_s5_c_re, m_s5_c_im, m_s5_d, m_s5_w_out, m_ffn_w1, m_ffn_w3, m_ffn_w2, m_ple_w_gate, m_ple_w_proj, v_norm_mix, v_norm_ffn, v_norm_ple, v_norm_final, v_gm_w_in, v_gm_ln_g, v_gm_ln_b, v_gm_w_s, v_gm_b_s, v_gm_w_out, v_s5_w_in, v_s5_a_re, v_s5_a_im, v_s5_log_dt, v_s5_b_re, v_s5_b_im, v_s5_c_re, v_s5_c_im, v_s5_d, v_s5_w_out, v_ffn_w1, v_ffn_w3, v_ffn_w2, v_ple_w_gate, v_ple_w_proj):
    env = dict(locals())
    w = {n: env[n] for n in _WEIGHT_NAMES}
    mom = {n: env["m_" + n] for n in _WEIGHT_NAMES}
    var = {n: env["v_" + n] for n in _WEIGHT_NAMES}
    xs, tgt = x[0], loss_target[0]
    L, D = xs.shape
    depth = norm_mix.shape[0]
    qx, qy = lax.axis_index("x"), lax.axis_index("y")
    q = 2 * qx + qy

    def gather(tag, items):
        shards = []
        for name, layer in items:
            kind = _BIG[name]
            R, C = _rc(kind, w[name].shape)
            shards.append(_cast_shard(f"cast_{name}{layer}", w[name], layer, kind, R, C))
        full = _allgather(f"ag_{tag}", shards)
        return {it: _W(f, _BIG[it[0]]) for it, f in zip(items, full)}

    W = {}
    mixers = [["gm_w_in", "gm_w_out"], ["s5_w_in", "s5_w_out"]]
    for i in range(depth):
        for n in mixers[i]:
            W.update(gather(f"{n}", [(n, 0)]))
        W.update(gather(f"ffn_up{i}", [("ffn_w1", i), ("ffn_w3", i)]))
        W.update(gather(f"ffn_down{i}", [("ffn_w2", i)]))
        W.update(gather(f"ple{i}", [("ple_w_gate", i), ("ple_w_proj", i)]))

    d_slots = jnp.zeros((4, D // 4), F32)
    d_slots = lax.dynamic_update_slice(d_slots, s5_d.astype(F32), (q, 0))
    d_sum = _allreduce_small("ar_s5_d", _pack([d_slots], 64).reshape(8, -1, _LANES))
    d_full = (d_sum.reshape(-1)[:D] * 0.5).reshape(1, D)

    def ffn_fwd(i, xin):
        hf = _rms_fwd(f"rms_ffn{i}", xin, norm_ffn[i:i + 1])
        a, b, f = _mm_nn(f"ffn_up{i}", hf, [W["ffn_w1", i], W["ffn_w3", i]], 1024, 1408, ffn_w2.shape[1] * 4,
                         [], [_MXU, _MXU, _MXU], _swiglu_epi)
        xo = _mm_nn(f"ffn_down{i}", f, [W["ffn_w2", i]], 1408, 1024, D, [xin], [F32], _add_resid, tm=1024)[0]
        return xo, (xin, hf, a, b, f)

    def ple_fwd(i, xin):
        hp = _rms_fwd(f"rms_ple{i}", xin, norm_ple[i:i + 1])
        pi = p[i, 0]
        pp = _mm_nn(f"ple_proj{i}", pi, [W["ple_w_proj", i]], 128, 512, D, [], [F32], _ident)[0]
        xo, gt = _mm_nn(f"ple_gate{i}", hp, [W["ple_w_gate", i]], 512, 1024, D, [xin, pp], [F32, _MXU], _ple_epi,
                        tm=1024)
        return xo, (xin, hp, pi, pp, gt)

    h0 = _rms_fwd("rms_mix0", xs, norm_mix[0:1])
    z = _mm_nn("gm_in", h0, [W["gm_w_in", 0]], 1024, 1024, 2 * D, [], [F32], _ident)[0]
    bsT = gm_b_s[0].T
    gm_m = _gmlp_fwd(z, gm_ln_g, gm_ln_b, gm_w_s[0], bsT)
    x1 = _mm_nn("gm_out", gm_m, [W["gm_w_out", 0]], 512, 1024, D, [xs], [F32], _add_resid, tm=1024)[0]
    x2, ffn0 = ffn_fwd(0, x1)
    x3, ple0 = ple_fwd(0, x2)

    T = D // _LANES
    lanes = _S5_GT * _S5_P
    sv = {n: _to_view(n, w[n])[0] for n in _VIEW}
    a_re, a_im, log_dt = sv["s5_a_re"], sv["s5_a_im"], s5_log_dt
    lbr, lbi, Bbar_re, Bbar_im = _s5_prep(a_re, a_im, log_dt, sv["s5_b_re"], sv["s5_b_im"])

    def to_bd(B):
        t = jnp.transpose(B.reshape(_S5_P, _S5_C, T, _S5_GT), (2, 3, 1, 0))
        return _block_diag(t).reshape(T, _LANES, lanes).astype(_MXU)

    def to_cd(cw):
        t = jnp.transpose(cw.reshape(_S5_C, _S5_P, T, _S5_GT), (2, 3, 1, 0))
        return _block_diag(t).reshape(T, lanes, _LANES).astype(_MXU)

    def to_lam(v):
        return jnp.transpose(v).reshape(T, 1, lanes)

    bd_re, bd_im = to_bd(Bbar_re), to_bd(Bbar_im)
    cd_re, cd_im = to_cd(sv["s5_c_re"]), to_cd(sv["s5_c_im"])
    lam_re, lam_im = to_lam(lbr), to_lam(lbi)

    h1 = _rms_fwd("rms_mix1", x3, norm_mix[1:2])
    u = _mm_nn("s5_in", h1, [W["s5_w_in", 0]], 512, 1024, D, [], [F32], _ident, tm=1024)[0]
    s5_g = _s5_fwd(u, bd_re, bd_im, cd_re, cd_im, lam_re, lam_im, d_full)
    x4, glu_val, glu_sg = _mm_nn("s5_out", s5_g, [W["s5_w_out", 0], W["s5_w_out", 0]], 1024, 1024, D, [x3],
                                 [F32, _MXU, _MXU], _glu_epi, cb_offsets=[0, 2])
    x5, ffn1 = ffn_fwd(1, x4)
    x6, ple1 = ple_fwd(1, x5)

    dx, d_norm_final, loss_rows = _loss_head(x6, norm_final[None], tgt)
    loss = lax.psum(loss_rows[0, 0], ("x", "y", "c"))

    small = {}

    def ple_bwd(i, dxo, saved):
        xin, hp, pi, pp, gt = saved
        dpre, dpp = _ple_bwd_elem(dxo, pp, gt)
        dwg = _mm_tn(f"ple_gate_dw{i}", hp, [dpre], "row", 512, 1024, 512, 1024)[0]
        dwp = _mm_tn(f"ple_proj_dw{i}", pi, [dpp], "col", 128, 512, 128, 512)[0]
        red = _Reduction(f"ple{i}", [dwg, dwp])
        dhp = _mm_nt(f"ple_gate_dx{i}", [dpre], [W["ple_w_gate", i]], 512, 1024, [], [F32], _ident, tm=2048)[0]
        dxin, dg = _rms_bwd(f"rms_ple_bwd{i}", dhp, xin, norm_ple[i:i + 1], dxo)
        return dxin, dg, red, dpre

    def ffn_bwd(i, dxo, saved):
        xin, hf, a, b, f = saved
        da, db = _mm_nt(f"ffn_down_dx{i}", [dxo], [W["ffn_w2", i]], 1408, 1024, [a, b], [_MXU, _MXU],
                        _swiglu_bwd_epi)
        dw1, dw3 = _mm_tn(f"ffn_up_dw{i}", hf, [da, db], "col", 1024, 1408, 1024, 1408)
        dw2 = _mm_tn(f"ffn_down_dw{i}", f, [dxo], "row", 1408, 1024, 1408, 1024)[0]
        red = _Reduction(f"ffn{i}", [dw1, dw3, dw2])
        dhf = _mm_nt(f"ffn_up_dx{i}", [da, db], [W["ffn_w1", i], W["ffn_w3", i]], 1024, 1408, [], [F32], _ident,
                     tm=1024)[0]
        dxin, dg = _rms_bwd(f"rms_ffn_bwd{i}", dhf, xin, norm_ffn[i:i + 1], dxo)
        return dxin, dg, red, da, dhf

    d_norm_ple, d_norm_ffn, d_norm_mix = [None] * depth, [None] * depth, [None] * depth
    reduced = {}

    def keep(names, layer, pairs):
        for n, pr in zip(names, pairs):
            reduced[n, layer] = pr

    ple_names, ffn_names = ["ple_w_gate", "ple_w_proj"], ["ffn_w1", "ffn_w3", "ffn_w2"]

    dx, d_norm_ple[1], r_ple1, _ = ple_bwd(1, dx, ple1)
    dx, d_norm_ffn[1], r_ffn1, da1, _ = ffn_bwd(1, dx, ffn1)
    r_ple1.scatter(da1)

    do = _glu_bwd_elem(dx, glu_val, glu_sg)
    r_ffn1.scatter(do)
    dw_s5out = _mm_tn("s5_out_dw", s5_g, [do], "col", 1024, 1024, 1024, 1024)[0]
    dgy = _mm_nt("s5_out_dx", [do], [W["s5_w_out", 0]], 1024, 1024, [], [F32], _ident, tm=1024)[0]
    keep(ple_names, 1, r_ple1.finish(dgy))
    du, dbd_re, dbd_im, dcd_re, dcd_im, dl_re, dl_im, dd = _s5_bwd(u, dgy, bd_re, bd_im, cd_re, cd_im,
                                                                  lam_re, lam_im, d_full)
    dw_s5in = _mm_tn("s5_in_dw", h1, [du], "row", 512, 1024, 512, 1024)[0]
    r_mix1 = _Reduction("mix1", [dw_s5in, dw_s5out])
    dh1 = _mm_nt("s5_in_dx", [du], [W["s5_w_in", 0]], 512, 1024, [], [F32], _ident, tm=2048)[0]
    dx, d_norm_mix[1] = _rms_bwd("rms_mix1_bwd", dh1, x3, norm_mix[1:2], dx)

    def from_bd(t):
        return jnp.transpose(_diag_blocks(t, T, _S5_C, _S5_P), (3, 2, 0, 1)).reshape(_S5_P, _S5_C, T * _S5_GT)

    def from_cdT(t):
        return jnp.transpose(_diag_blocks(t, T, _S5_C, _S5_P), (2, 3, 0, 1)).reshape(_S5_C, _S5_P, T * _S5_GT)

    def from_lam(t):
        return jnp.transpose(t.reshape(T * _S5_GT, _S5_P))

    da_re, da_im, dlog_dt, db_re, db_im = _s5_prep_bwd(
        a_re, a_im, log_dt, sv["s5_b_re"], sv["s5_b_im"], from_lam(dl_re), from_lam(dl_im),
        from_bd(dbd_re), from_bd(dbd_im))
    small["s5_a_re"], small["s5_a_im"], small["s5_log_dt"] = da_re[None], da_im[None], dlog_dt
    small["s5_b_re"], small["s5_b_im"] = db_re[None], db_im[None]
    small["s5_c_re"], small["s5_c_im"] = from_cdT(dcd_re)[None], from_cdT(dcd_im)[None]

    dx, d_norm_ple[0], r_ple0, dpre0 = ple_bwd(0, dx, ple0)
    r_mix1.scatter(dpre0)
    keep(ffn_names, 1, r_ffn1.finish(dx))
    dx, d_norm_ffn[0], r_ffn0, da0, dhf0 = ffn_bwd(0, dx, ffn0)
    r_ple0.scatter(da0)
    keep(["s5_w_in", "s5_w_out"], 0, r_mix1.finish(da0))
    keep(ple_names, 0, r_ple0.finish(dhf0))

    dw_gmout = _mm_tn("gm_out_dw", gm_m, [dx], "row", 512, 1024, 512, 1024)[0]
    dgm = _mm_nt("gm_out_dx", [dx], [W["gm_w_out", 0]], 512, 1024, [], [F32], _ident, tm=2048)[0]
    r_ffn0.scatter(dgm)
    dz, dws, dbsT, dlng, dlnb = _gmlp_bwd(z, dgm, gm_ln_g, gm_ln_b, gm_w_s[0], bsT)
    dw_gmin = _mm_tn("gm_in_dw", h0, [dz], "col", 1024, 1024, 1024, 1024)[0]
    r_mix0 = _Reduction("mix0", [dw_gmin, dw_gmout])
    dh0 = _mm_nt("gm_in_dx", [dz], [W["gm_w_in", 0]], 1024, 1024, [], [F32], _ident, tm=1024)[0]
    dx, d_norm_mix[0] = _rms_bwd("rms_mix0_bwd", dh0, xs, norm_mix[0:1], dx)
    r_mix0.scatter(dx)
    grad_x = dx[None]

    small["norm_mix"], small["norm_ffn"] = jnp.concatenate(d_norm_mix), jnp.concatenate(d_norm_ffn)
    small["norm_ple"], small["norm_final"] = jnp.concatenate(d_norm_ple), d_norm_final[0]
    small["gm_ln_g"], small["gm_ln_b"], small["gm_w_s"] = dlng, dlnb, dws[None]
    small["gm_b_s"] = dbsT[:, :_GM_HEADS].T[None]
    small["s5_d"] = dd

    small_names = [n for n in _WEIGHT_NAMES if n not in _BIG]
    packed = _pack([small[n] for n in small_names], 64)
    summed = _allreduce_small("ar_small", packed.reshape(8, -1, _LANES))
    red_small = dict(zip(small_names, _unpack(summed, [small[n].shape for n in small_names])))
    red_small["s5_d"] = lax.dynamic_slice(red_small["s5_d"], (0, q * (D // 4)), (1, D // 4))

    grads, deltas, new_m, new_v = {}, {}, {}, {}
    my_c = lax.axis_index("c")
    token = summed

    def adamw(n, layer, prev):
        nonlocal token
        kind = _BIG[n]
        R, C = _rc(kind, w[n].shape)
        outs = _adamw_big(f"adamw_{n}{layer}", w[n], mom[n], var[n], layer, reduced[n, layer], kind, R, C,
                          my_c, token, prev)
        token = outs[1]
        return outs

    late = {}
    for n in ffn_names + ple_names:
        late[n] = adamw(n, 1, None)
    for n in ["s5_w_in", "s5_w_out"]:
        grads[n], deltas[n], new_m[n], new_v[n] = adamw(n, 0, None)
    for n in ple_names:
        grads[n], deltas[n], new_m[n], new_v[n] = adamw(n, 0, late[n])
    keep(ffn_names, 0, r_ffn0.finish(token))
    for n in ffn_names:
        grads[n], deltas[n], new_m[n], new_v[n] = adamw(n, 0, late[n])
    keep(["gm_w_in", "gm_w_out"], 0, r_mix0.finish(token))
    for n in ["gm_w_in", "gm_w_out"]:
        grads[n], deltas[n], new_m[n], new_v[n] = adamw(n, 0, None)
    def views(src):
        return [_to_view(n, src[n]) for n in small_names]

    shapes = [v_.shape for v_ in views(w)]
    dl, mo, vo = _adamw_flat(_pack(views(w), 64), _pack([red_small[n] for n in small_names], 64),
                             _pack(views(mom), 64), _pack(views(var), 64))
    for n, g_, d_, m_, v_ in zip(small_names, [red_small[n] for n in small_names], _unpack(dl, shapes),
                                 _unpack(mo, shapes), _unpack(vo, shapes)):
        grads[n], deltas[n], new_m[n], new_v[n] = [_from_view(n, t_).reshape(w[n].shape) for t_ in (g_, d_, m_, v_)]

    return (loss, grad_x, *[grads[n] for n in _WEIGHT_NAMES], *[deltas[n] for n in _WEIGHT_NAMES],
            *[new_m[n] for n in _WEIGHT_NAMES], *[new_v[n] for n in _WEIGHT_NAMES])
```

```python
import functools
import math

import jax
import jax.numpy as jnp
from jax import lax
from jax.experimental import pallas as pl
from jax.experimental.pallas import tpu as pltpu
from jax.experimental.pallas import tpu_sc as plsc

F32 = jnp.float32
_MXU = jnp.bfloat16
_WIRE = jnp.bfloat16
_EPS = 1e-6
_VMEM_LIMIT = 48 * 1024 * 1024
_LANES = 128
_MESH = pl.DeviceIdType.MESH

_LR, _B1, _B2, _AEPS, _WD, _STEP = 0.001, 0.9, 0.999, 1e-08, 0.01, 10

_GM_CHUNK = 128
_GM_HEADS = 16
_S5_GT = 8
_S5_P = 64
_S5_C = 16

_NN = (((1,), (0,)), ((), ()))
_NT = (((1,), (1,)), ((), ()))
_TN = (((0,), (0,)), ((), ()))


def _cparams(sem):
    return pltpu.CompilerParams(dimension_semantics=sem, vmem_limit_bytes=_VMEM_LIMIT)


def _sig(x):
    return 1.0 / (1.0 + jnp.exp(-x))


_GC = math.sqrt(2.0 / math.pi)


def _gelu(x):
    return 0.5 * x * (1.0 + jnp.tanh(_GC * (x + 0.044715 * (x * x * x))))


def _gelu_grad(x):
    t = jnp.tanh(_GC * (x + 0.044715 * (x * x * x)))
    return 0.5 * (1.0 + t) + 0.5 * x * (1.0 - t * t) * (_GC * (1.0 + 3.0 * 0.044715 * x * x))


def _dot(a, b, dn):
    return lax.dot_general(a.astype(_MXU), b.astype(_MXU), dn, preferred_element_type=F32)


class _W:
    def __init__(self, arr, kind):
        self.a, self.kind = arr, kind
        self.R, self.C = arr.shape[2], arr.shape[3]

    def full_shape(self):
        return (2 * self.R, 4 * self.C) if self.kind == "col" else (4 * self.R, 2 * self.C)


def _part_index(kind, R, C, tr, tc, rb, cb):
    nr, nc = R // tr, C // tc
    if kind == "col":
        return cb // nc, rb // nr, rb % nr, cb % nc
    return rb // nr, cb // nc, rb % nr, cb % nc


def _wspec(w, tr, tc, rb_fn, cb_fn):
    assert w.R % tr == 0 and w.C % tc == 0, (w.R, w.C, tr, tc)

    def imap(i, j, k):
        return _part_index(w.kind, w.R, w.C, tr, tc, rb_fn(i, j, k), cb_fn(i, j, k))

    return pl.BlockSpec((None, None, tr, tc), imap)


def _gspec(kind, R, C, tr, tc):
    assert R % tr == 0 and C % tc == 0, (R, C, tr, tc)

    def imap(i, j, k):
        part, half, rbi, cbi = _part_index(kind, R, C, tr, tc, i, j)
        return half, part, rbi, cbi

    return pl.BlockSpec((None, None, tr, tc), imap)


def _mm(name, grid, a_ops, b_ops, pairs, acc_shape, n_acc, extras, outs, epilogue):
    nk = grid[2]
    na, nb, ne, no = len(a_ops), len(b_ops), len(extras), len(outs)

    def body(*refs):
        a_refs = refs[:na]
        b_refs = refs[na:na + nb]
        e_refs = refs[na + nb:na + nb + ne]
        o_refs = refs[na + nb + ne:na + nb + ne + no]
        acc_refs = refs[na + nb + ne + no:]
        k = pl.program_id(2)

        def products():
            sums = [None] * n_acc
            for ai, bi, ci, dn in pairs:
                d = _dot(a_refs[ai][...], b_refs[bi][...], dn)
                sums[ci] = d if sums[ci] is None else sums[ci] + d
            return sums

        def finish(accs):
            res = epilogue(accs, [e[...] for e in e_refs])
            for o, r in zip(o_refs, res):
                o[...] = r.astype(o.dtype)

        if nk == 1:
            finish(products())
            return

        @pl.when(k == 0)
        def _():
            for acc, d in zip(acc_refs, products()):
                acc[...] = d

        @pl.when(jnp.logical_and(k > 0, k < nk - 1))
        def _():
            for acc, d in zip(acc_refs, products()):
                acc[...] += d

        @pl.when(k == nk - 1)
        def _():
            finish([acc[...] + d for acc, d in zip(acc_refs, products())])

    ops = list(a_ops) + list(b_ops) + list(extras)
    return pl.pallas_call(
        body, name=name, grid=grid,
        in_specs=[s for _, s in ops],
        out_specs=[s for _, s in outs],
        out_shape=[s for s, _ in outs],
        scratch_shapes=[pltpu.VMEM(acc_shape, F32) for _ in range(n_acc if nk > 1 else 0)],
        compiler_params=_cparams(("parallel", "parallel", "arbitrary")),
    )(*[a for a, _ in ops])


def _bs(shape, fn):
    return pl.BlockSpec(shape, fn)


def _tile_m(L):
    return min(L, 512)


def _mm_nn(name, x, ws, tk, tn, n_out, extras, outs_sd, epilogue, tm=None, cb_offsets=None):
    M, K = x.shape
    tm = min(M, tm or _tile_m(M))
    grid = (M // tm, n_out // tn, K // tk)
    a_ops = [(x, _bs((tm, tk), lambda i, j, k: (i, k)))]
    cb_offsets = cb_offsets or [0] * len(ws)
    b_ops = [(w.a, _wspec(w, tk, tn, lambda i, j, k: k, (lambda off: lambda i, j, k: j + off)(off)))
             for w, off in zip(ws, cb_offsets)]
    pairs = [(0, bi, bi, _NN) for bi in range(len(ws))]
    mn = _bs((tm, tn), lambda i, j, k: (i, j))
    ex = [(e, mn) for e in extras]
    outs = [(jax.ShapeDtypeStruct((M, n_out), dt), mn) for dt in outs_sd]
    return _mm(name, grid, a_ops, b_ops, pairs, (tm, tn), len(ws), ex, outs, epilogue)


def _mm_nt(name, xs, ws, tn, tk, extras, outs_sd, epilogue, tm=None):
    M, Nw = xs[0].shape
    Kw = ws[0].full_shape()[0]
    tm = min(M, tm or _tile_m(M))
    grid = (M // tm, Kw // tn, Nw // tk)
    a_ops = [(x, _bs((tm, tk), lambda i, j, k: (i, k))) for x in xs]
    b_ops = [(w.a, _wspec(w, tn, tk, lambda i, j, k: j, lambda i, j, k: k)) for w in ws]
    pairs = [(i, i, 0, _NT) for i in range(len(ws))]
    mn = _bs((tm, tn), lambda i, j, k: (i, j))
    ex = [(e, mn) for e in extras]
    outs = [(jax.ShapeDtypeStruct((M, Kw), dt), mn) for dt in outs_sd]
    return _mm(name, grid, a_ops, b_ops, pairs, (tm, tn), 1, ex, outs, epilogue)


def _mm_tn(name, x, dys, kind, R, C, tm, tn, tk=None):
    L, Kw = x.shape
    Nw = dys[0].shape[1]
    tk = tk or min(L, 1024)
    grid = (Kw // tm, Nw // tn, L // tk)
    a_ops = [(x, _bs((tk, tm), lambda i, j, k: (k, i)))]
    b_ops = [(dy, _bs((tk, tn), lambda i, j, k: (k, j))) for dy in dys]
    pairs = [(0, bi, bi, _TN) for bi in range(len(dys))]
    gs = _gspec(kind, R, C, tm, tn)
    outs = [(jax.ShapeDtypeStruct((2, 4, R, C), _WIRE), gs) for _ in dys]
    return _mm(name, grid, a_ops, b_ops, pairs, (tm, tn), len(dys), [], outs, lambda accs, ex: accs)


def _row_tile(L):
    return min(L, 256)


def _rowwise(name, body, ins, outs, L, acc_outs=()):
    tr = _row_tile(L)
    n_in, n_out = len(ins), len(outs)

    def kbody(*refs):
        i_refs, o_refs, a_refs = refs[:n_in], refs[n_in:n_in + n_out], refs[n_in + n_out:]
        res, sums = body(*[r[...] for r in i_refs])
        for o, r in zip(o_refs, res):
            o[...] = r.astype(o.dtype)
        if a_refs:
            @pl.when(pl.program_id(0) == 0)
            def _():
                for a in a_refs:
                    a[...] = jnp.zeros(a.shape, F32)
            for a, s in zip(a_refs, sums):
                a[...] += s

    in_specs = []
    for arr, kind in ins:
        if kind == "row":
            in_specs.append(pl.BlockSpec((tr, arr.shape[1]), lambda i: (i, 0)))
        else:
            in_specs.append(pl.BlockSpec(arr.shape, lambda i: (0, 0)))
    out_specs = [pl.BlockSpec((tr, c), lambda i: (i, 0)) for c, _ in outs]
    out_shape = [jax.ShapeDtypeStruct((L, c), dt) for c, dt in outs]
    out_specs += [pl.BlockSpec((1, c), lambda i: (0, 0)) for c in acc_outs]
    out_shape += [jax.ShapeDtypeStruct((1, c), F32) for c in acc_outs]
    return pl.pallas_call(
        kbody, name=name, grid=(L // tr,), in_specs=in_specs, out_specs=out_specs, out_shape=out_shape,
        compiler_params=_cparams(("arbitrary",)),
    )(*[a for a, _ in ins])


def _rms_fwd(name, x, g):
    def body(xv, gv):
        r = lax.rsqrt(jnp.mean(xv * xv, axis=-1, keepdims=True) + _EPS)
        return [xv * r * gv], []
    return _rowwise(name, body, [(x, "row"), (g, "vec")], [(x.shape[1], _MXU)], x.shape[0])[0]


def _rms_bwd(name, dh, x, g, dres):
    def body(dhv, xv, gv, dr):
        r = lax.rsqrt(jnp.mean(xv * xv, axis=-1, keepdims=True) + _EPS)
        xh = xv * r
        dxh = dhv * gv
        dx = r * (dxh - xh * jnp.mean(dxh * xh, axis=-1, keepdims=True))
        return [dr + dx], [jnp.sum(dhv * xh, axis=0, keepdims=True)]
    D = x.shape[1]
    return _rowwise(name, body, [(dh, "row"), (x, "row"), (g, "vec"), (dres, "row")], [(D, F32)], x.shape[0], [D])


def _loss_head(x, g, target):
    D = x.shape[1]

    def body(xv, gv, tv):
        r = lax.rsqrt(jnp.mean(xv * xv, axis=-1, keepdims=True) + _EPS)
        xh = xv * r
        e = xh * gv - tv
        dy = e * (1.0 / D)
        dxh = dy * gv
        dx = r * (dxh - xh * jnp.mean(dxh * xh, axis=-1, keepdims=True))
        row_loss = 0.5 * jnp.mean(e * e, axis=-1, keepdims=True)
        lsum = jnp.sum(row_loss, axis=0, keepdims=True) + jnp.zeros((1, _LANES), F32)
        return [dx], [jnp.sum(dy * xh, axis=0, keepdims=True), lsum]
    return _rowwise("loss_head", body, [(x, "row"), (g, "vec"), (target, "row")], [(D, F32)], x.shape[0], [D, _LANES])


def _ple_bwd_elem(dx, pp, gt):
    def body(dxv, ppv, gtv):
        gt32 = gtv.astype(F32)
        return [dxv * ppv * gt32 * (1.0 - gt32), dxv * gt32], []
    D = dx.shape[1]
    return _rowwise("ple_bwd_elem", body, [(dx, "row"), (pp, "row"), (gt, "row")], [(D, _MXU), (D, _MXU)], dx.shape[0])


def _glu_bwd_elem(dx, val, sg):
    def body(dxv, vv, sv):
        v32, s32 = vv.astype(F32), sv.astype(F32)
        return [jnp.concatenate([dxv * s32, dxv * v32 * s32 * (1.0 - s32)], axis=1)], []
    D = dx.shape[1]
    return _rowwise("glu_bwd_elem", body, [(dx, "row"), (val, "row"), (sg, "row")], [(2 * D, _MXU)], dx.shape[0])[0]


def _gm_common(z, ln_g, ln_b, wc_bf, bsT):
    W = z.shape[1] // 2
    zu, zv = z[:, :W], z[:, W:]
    u, v = _gelu(zu), _gelu(zv)
    mu = jnp.mean(v, axis=-1, keepdims=True)
    vc = v - mu
    rstd = lax.rsqrt(jnp.mean(vc * vc, axis=-1, keepdims=True) + _EPS)
    vh = vc * rstd
    vn = vh * ln_g + ln_b
    vnb = vn.astype(_MXU)
    svs = []
    for h in range(_GM_HEADS):
        sl = slice(h * _LANES, (h + 1) * _LANES)
        svs.append(_dot(wc_bf[h], vnb[:, sl], _NN) + bsT[:, h:h + 1])
    return zu, zv, u, vh, rstd, vnb, svs


def _causal(w):
    t = lax.broadcasted_iota(jnp.int32, w.shape, w.ndim - 2)
    s = lax.broadcasted_iota(jnp.int32, w.shape, w.ndim - 1)
    return jnp.where(s <= t, w, jnp.zeros_like(w))


def _gmlp_fwd(z, ln_g, ln_b, w_s, bsT):
    L, W2 = z.shape
    W = W2 // 2

    def body(z_ref, g_ref, b_ref, ws_ref, bs_ref, m_ref):
        wc = _causal(ws_ref[...]).astype(_MXU)
        _, _, u, _, _, _, svs = _gm_common(z_ref[...], g_ref[...], b_ref[...], wc, bs_ref[...])
        for h in range(_GM_HEADS):
            sl = slice(h * _LANES, (h + 1) * _LANES)
            m_ref[:, sl] = (u[:, sl] * svs[h]).astype(m_ref.dtype)

    return pl.pallas_call(
        body, name="gmlp_fwd", grid=(L // _GM_CHUNK,),
        in_specs=[pl.BlockSpec((_GM_CHUNK, W2), lambda n: (n, 0)),
                  pl.BlockSpec((1, W), lambda n: (0, 0)), pl.BlockSpec((1, W), lambda n: (0, 0)),
                  pl.BlockSpec(w_s.shape, lambda n: (0, 0, 0)), pl.BlockSpec(bsT.shape, lambda n: (0, 0))],
        out_specs=pl.BlockSpec((_GM_CHUNK, W), lambda n: (n, 0)),
        out_shape=jax.ShapeDtypeStruct((L, W), _MXU),
        compiler_params=_cparams(("arbitrary",)),
    )(z, ln_g, ln_b, w_s, bsT)


def _gmlp_bwd(z, dm, ln_g, ln_b, w_s, bsT):
    L, W2 = z.shape
    W = W2 // 2
    T = _GM_CHUNK

    def body(z_ref, dm_ref, g_ref, b_ref, ws_ref, bs_ref, dz_ref, dws_ref, dbs_ref, dg_ref, db_ref):
        @pl.when(pl.program_id(0) == 0)
        def _():
            dws_ref[...] = jnp.zeros(dws_ref.shape, F32)
            dbs_ref[...] = jnp.zeros(dbs_ref.shape, F32)
            dg_ref[...] = jnp.zeros(dg_ref.shape, F32)
            db_ref[...] = jnp.zeros(db_ref.shape, F32)

        wc = _causal(ws_ref[...]).astype(_MXU)
        ln_g_v = g_ref[...]
        zu, zv, u, vh, rstd, vnb, svs = _gm_common(z_ref[...], ln_g_v, b_ref[...], wc, bs_ref[...])
        dmv = dm_ref[...]
        lane = lax.broadcasted_iota(jnp.int32, (T, _LANES), 1)
        dbs = jnp.zeros((T, _LANES), F32)
        dvn_parts = []
        for h in range(_GM_HEADS):
            sl = slice(h * _LANES, (h + 1) * _LANES)
            dsv = dmv[:, sl] * u[:, sl]
            dz_ref[:, sl] = (dmv[:, sl] * svs[h] * _gelu_grad(zu[:, sl])).astype(dz_ref.dtype)
            dbs = dbs + jnp.where(lane == h, jnp.sum(dsv, axis=1, keepdims=True), 0.0)
            dsvb = dsv.astype(_MXU)
            dws_ref[h] += _dot(dsvb, vnb[:, sl], _NT)
            dvn_parts.append(_dot(wc[h], dsvb, _TN))
        dbs_ref[...] += dbs
        dvn = jnp.concatenate(dvn_parts, axis=1)
        dg_ref[...] += jnp.sum(dvn * vh, axis=0, keepdims=True)
        db_ref[...] += jnp.sum(dvn, axis=0, keepdims=True)
        dxh = dvn * ln_g_v
        dv = rstd * (dxh - jnp.mean(dxh, axis=-1, keepdims=True) - vh * jnp.mean(dxh * vh, axis=-1, keepdims=True))
        dz_ref[:, W:] = (dv * _gelu_grad(zv)).astype(dz_ref.dtype)

        @pl.when(pl.program_id(0) == pl.num_programs(0) - 1)
        def _():
            dws_ref[...] = _causal(dws_ref[...])

    return pl.pallas_call(
        body, name="gmlp_bwd", grid=(L // T,),
        in_specs=[pl.BlockSpec((T, W2), lambda n: (n, 0)), pl.BlockSpec((T, W), lambda n: (n, 0)),
                  pl.BlockSpec((1, W), lambda n: (0, 0)), pl.BlockSpec((1, W), lambda n: (0, 0)),
                  pl.BlockSpec(w_s.shape, lambda n: (0, 0, 0)), pl.BlockSpec(bsT.shape, lambda n: (0, 0))],
        out_specs=[pl.BlockSpec((T, W2), lambda n: (n, 0)),
                   pl.BlockSpec(w_s.shape, lambda n: (0, 0, 0)), pl.BlockSpec((T, _LANES), lambda n: (0, 0)),
                   pl.BlockSpec((1, W), lambda n: (0, 0)), pl.BlockSpec((1, W), lambda n: (0, 0))],
        out_shape=[jax.ShapeDtypeStruct((L, W2), _MXU), jax.ShapeDtypeStruct(w_s.shape, F32),
                   jax.ShapeDtypeStruct((T, _LANES), F32),
                   jax.ShapeDtypeStruct((1, W), F32), jax.ShapeDtypeStruct((1, W), F32)],
        compiler_params=_cparams(("arbitrary",)),
    )(z, dm, ln_g, ln_b, w_s, bsT)


def _s5_prep_math(a_re, a_im, log_dt):
    dt = jnp.exp(log_dt)
    xr, xi = a_re * dt, a_im * dt
    e = jnp.exp(xr)
    lbr, lbi = e * jnp.cos(xi), e * jnp.sin(xi)
    dn = a_re * a_re + a_im * a_im
    nr, ni = lbr - 1.0, lbi
    pr, pi = nr * a_re + ni * a_im, ni * a_re - nr * a_im
    return dt, lbr, lbi, dn, nr, ni, pr, pi


def _vm():
    return pl.BlockSpec(memory_space=pltpu.VMEM)


def _s5_prep(a_re, a_im, log_dt, b_re, b_im):
    def body(ar_ref, ai_ref, ld_ref, br_ref, bi_ref, lbr_ref, lbi_ref, Br_ref, Bi_ref):
        _, lbr, lbi, dn, _, _, pr, pi = _s5_prep_math(ar_ref[...], ai_ref[...], ld_ref[...])
        cr, ci = (pr / dn)[:, None, :], (pi / dn)[:, None, :]
        lbr_ref[...] = lbr
        lbi_ref[...] = lbi
        br, bi = br_ref[...], bi_ref[...]
        Br_ref[...] = cr * br - ci * bi
        Bi_ref[...] = cr * bi + ci * br

    sd = jax.ShapeDtypeStruct
    return pl.pallas_call(
        body, name="s5_prep", in_specs=[_vm()] * 5, out_specs=[_vm()] * 4,
        out_shape=[sd(a_re.shape, F32), sd(a_re.shape, F32), sd(b_re.shape, F32), sd(b_re.shape, F32)],
    )(a_re, a_im, log_dt, b_re, b_im)


def _s5_prep_bwd(a_re, a_im, log_dt, b_re, b_im, dlbr_s, dlbi_s, dBr, dBi):
    def body(ar_ref, ai_ref, ld_ref, br_ref, bi_ref, dlr_ref, dli_ref, dBr_ref, dBi_ref,
             dar_ref, dai_ref, dld_ref, dbr_ref, dbi_ref):
        a_re_v, a_im_v = ar_ref[...], ai_ref[...]
        dt, lbr, lbi, dn, nr, ni, pr, pi = _s5_prep_math(a_re_v, a_im_v, ld_ref[...])
        cr, ci = (pr / dn)[:, None, :], (pi / dn)[:, None, :]
        br, bi, dBr_v, dBi_v = br_ref[...], bi_ref[...], dBr_ref[...], dBi_ref[...]
        dbr_ref[...] = cr * dBr_v + ci * dBi_v
        dbi_ref[...] = cr * dBi_v - ci * dBr_v
        dcr = jnp.sum(br * dBr_v + bi * dBi_v, axis=1)
        dci = jnp.sum(br * dBi_v - bi * dBr_v, axis=1)
        dpr, dpi = dcr / dn, dci / dn
        ddn = -(dcr * pr + dci * pi) / (dn * dn)
        dnr = dpr * a_re_v - dpi * a_im_v
        dni = dpr * a_im_v + dpi * a_re_v
        dlbr = dlr_ref[...] + dnr
        dlbi = dli_ref[...] + dni
        dxr = dlbr * lbr + dlbi * lbi
        dxi = dlbi * lbr - dlbr * lbi
        dar_ref[...] = dpr * nr + dpi * ni + 2.0 * ddn * a_re_v + dxr * dt
        dai_ref[...] = dpr * ni - dpi * nr + 2.0 * ddn * a_im_v + dxi * dt
        dld_ref[...] = jnp.sum(dxr * a_re_v + dxi * a_im_v, axis=0, keepdims=True) * dt

    sd = jax.ShapeDtypeStruct
    return pl.pallas_call(
        body, name="s5_prep_bwd", in_specs=[_vm()] * 9, out_specs=[_vm()] * 5,
        out_shape=[sd(a_re.shape, F32), sd(a_re.shape, F32), sd(log_dt.shape, F32),
                   sd(b_re.shape, F32), sd(b_re.shape, F32)],
    )(a_re, a_im, log_dt, b_re, b_im, dlbr_s, dlbi_s, dBr, dBi)


def _shift_rows(v, down):
    n = v.shape[0]
    rolled = pltpu.roll(v, 1 if down else n - 1, 0)
    row = lax.broadcasted_iota(jnp.int32, v.shape, 0)
    return jnp.where(row == (0 if down else n - 1), 0.0, rolled)


def _cmul(ar, ai, br, bi):
    return ar * br - ai * bi, ar * bi + ai * br


_SEG = 8
_UNROLL = 8


def _seg_rows(k):
    if isinstance(k, int):
        return pl.ds(k * _SEG, _SEG)
    return pl.ds(pl.multiple_of(k * _SEG, _SEG), _SEG)


def _unrolled(n, step, init):
    main = n // _UNROLL

    def trip(kk, s):
        for uu in range(_UNROLL):
            s = step(kk * _UNROLL + uu, s)
        return s

    s = lax.fori_loop(0, main, trip, init)
    for r in range(main * _UNROLL, n):
        s = step(r, s)
    return s


def _interleave(src_ref, dst_ref, nk):
    def step(k, carry):
        dst_ref[_seg_rows(k), :] = src_ref[pl.ds(k, _SEG, stride=nk), :]
        return carry
    _unrolled(nk, step, 0)


def _deinterleave(src_ref, dst_ref, nk):
    def step(k, carry):
        dst_ref[pl.ds(k, _SEG, stride=nk), :] = src_ref[_seg_rows(k), :]
        return carry
    _unrolled(nk, step, 0)


def _segment_inits(er, ei, ar, ai, nk, down):
    pr, pi = ar, ai
    for _ in range(int(math.log2(nk))):
        pr, pi = _cmul(pr, pi, pr, pi)
    fr, fi = er, ei
    for _ in range(_SEG - 1):
        sr, si = _shift_rows(fr, down), _shift_rows(fi, down)
        mr, mi = _cmul(pr, pi, sr, si)
        fr, fi = er + mr, ei + mi
    return _shift_rows(fr, down), _shift_rows(fi, down)


def _scan_states(x_re, x_im, ar, ai, nk):
    lanes = ar.shape[1]

    def step(k, s):
        rows = _seg_rows(k)
        mr, mi = _cmul(ar, ai, s[0], s[1])
        return mr + x_re[rows, :], mi + x_im[rows, :]

    zero = jnp.zeros((_SEG, lanes), F32)
    er, ei = _unrolled(nk, step, (zero, zero))
    ir, ii = _segment_inits(er, ei, ar, ai, nk, True)

    def step2(k, s):
        rows = _seg_rows(k)
        mr, mi = _cmul(ar, ai, s[0], s[1])
        nr, ni = mr + x_re[rows, :], mi + x_im[rows, :]
        x_re[rows, :] = nr
        x_im[rows, :] = ni
        return nr, ni

    _unrolled(nk, step2, (ir, ii))


def _s5_tile_fwd(u, bd_re, bd_im, cd_re, cd_im, ar, ai, d, s_re, s_im, nk):
    s_re[...] = _dot(u, bd_re, _NN)
    s_im[...] = _dot(u, bd_im, _NN)
    _scan_states(s_re, s_im, ar, ai, nk)
    return _dot(s_re[...], cd_re, _NN) - _dot(s_im[...], cd_im, _NN) + d * u


def _s5_specs(L, T):
    lanes = _S5_GT * _S5_P
    u_spec = pl.BlockSpec((L, _LANES), lambda t: (0, t))
    bd_spec = pl.BlockSpec((None, _LANES, lanes), lambda t: (t, 0, 0))
    cd_spec = pl.BlockSpec((None, lanes, _LANES), lambda t: (t, 0, 0))
    lam_spec = pl.BlockSpec((None, 1, lanes), lambda t: (t, 0, 0))
    d_spec = pl.BlockSpec((1, _LANES), lambda t: (0, t))
    return lanes, u_spec, bd_spec, cd_spec, lam_spec, d_spec


def _s5_fwd(u, bd_re, bd_im, cd_re, cd_im, lam_re, lam_im, d):
    L, Wd = u.shape
    T = Wd // _LANES
    nk = L // _SEG
    lanes, u_spec, bd_spec, cd_spec, lam_spec, d_spec = _s5_specs(L, T)

    def body(u_ref, bdr, bdi, cdr, cdi, lr, li, d_ref, g_ref, s_re, s_im, up, tmp):
        ar = jnp.broadcast_to(lr[...], (_SEG, lanes))
        ai = jnp.broadcast_to(li[...], (_SEG, lanes))
        _interleave(u_ref, up, nk)
        y = _s5_tile_fwd(up[...], bdr[...], bdi[...], cdr[...], cdi[...], ar, ai, d_ref[...], s_re, s_im, nk)
        up[...] = _gelu(y)
        _deinterleave(up, tmp, nk)
        g_ref[...] = tmp[...].astype(g_ref.dtype)

    return pl.pallas_call(
        body, name="s5_fwd", grid=(T,),
        in_specs=[u_spec, bd_spec, bd_spec, cd_spec, cd_spec, lam_spec, lam_spec, d_spec],
        out_specs=u_spec, out_shape=jax.ShapeDtypeStruct((L, Wd), _MXU),
        scratch_shapes=[pltpu.VMEM((L, lanes), F32) for _ in range(2)]
        + [pltpu.VMEM((L, _LANES), F32) for _ in range(2)],
        compiler_params=_cparams(("arbitrary",)),
    )(u, bd_re, bd_im, cd_re, cd_im, lam_re, lam_im, d)


def _s5_bwd(u, dg, bd_re, bd_im, cd_re, cd_im, lam_re, lam_im, d):
    L, Wd = u.shape
    T = Wd // _LANES
    nk = L // _SEG
    lanes, u_spec, bd_spec, cd_spec, lam_spec, d_spec = _s5_specs(L, T)

    def body(u_ref, dg_ref, bdr, bdi, cdr, cdi, lr, li, d_ref,
             du_ref, dbdr, dbdi, dcdr, dcdi, dlr, dli, dd_ref, s_re, s_im, g_re, g_im, up, dgp, tmp):
        ar = jnp.broadcast_to(lr[...], (_SEG, lanes))
        ai = jnp.broadcast_to(li[...], (_SEG, lanes))
        _interleave(u_ref, up, nk)
        _interleave(dg_ref, dgp, nk)
        uv, dv = up[...], d_ref[...]
        y = _s5_tile_fwd(uv, bdr[...], bdi[...], cdr[...], cdi[...], ar, ai, dv, s_re, s_im, nk)
        dy = dgp[...] * _gelu_grad(y)
        dd_ref[...] = jnp.sum(dy * uv, axis=0, keepdims=True)
        dyb = dy.astype(_MXU)
        dcdr[...] = _dot(dyb, s_re[...], _TN)
        dcdi[...] = -_dot(dyb, s_im[...], _TN)
        g_re[...] = _dot(dyb, cdr[...], _NT)
        g_im[...] = -_dot(dyb, cdi[...], _NT)

        nai = -ai

        def step(j, s):
            rows = _seg_rows(nk - 1 - j)
            mr, mi = _cmul(ar, nai, s[0], s[1])
            return mr + g_re[rows, :], mi + g_im[rows, :]

        zero = jnp.zeros((_SEG, lanes), F32)
        er, ei = _unrolled(nk, step, (zero, zero))
        ir, ii = _segment_inits(er, ei, ar, nai, nk, False)

        def acc_lam(gr, gi, pr, pi, acc):
            return acc[0] + gr * pr + gi * pi, acc[1] + gi * pr - gr * pi

        def step2(j, carry):
            s, acc = carry
            k = nk - 1 - j
            rows = _seg_rows(k)
            mr, mi = _cmul(ar, nai, s[0], s[1])
            nr, ni = mr + g_re[rows, :], mi + g_im[rows, :]
            g_re[rows, :] = nr
            g_im[rows, :] = ni
            prev = _seg_rows(k - 1)
            return (nr, ni), acc_lam(nr, ni, s_re[prev, :], s_im[prev, :], acc)

        (g0r, g0i), acc = _unrolled(nk - 1, step2, ((ir, ii), (zero, zero)))
        first = _seg_rows(0)
        mr, mi = _cmul(ar, nai, g0r, g0i)
        nr, ni = mr + g_re[first, :], mi + g_im[first, :]
        g_re[first, :] = nr
        g_im[first, :] = ni
        last = _seg_rows(nk - 1)
        acc = acc_lam(nr, ni, _shift_rows(s_re[last, :], True), _shift_rows(s_im[last, :], True), acc)
        dlr[...] = jnp.sum(acc[0], axis=0, keepdims=True)
        dli[...] = jnp.sum(acc[1], axis=0, keepdims=True)

        gtr, gti = g_re[...].astype(_MXU), g_im[...].astype(_MXU)
        ub = uv.astype(_MXU)
        dbdr[...] = _dot(ub, gtr, _TN)
        dbdi[...] = _dot(ub, gti, _TN)
        dgp[...] = _dot(gtr, bdr[...], _NT) + _dot(gti, bdi[...], _NT) + dy * dv
        _deinterleave(dgp, tmp, nk)
        du_ref[...] = tmp[...].astype(du_ref.dtype)

    sd = jax.ShapeDtypeStruct
    big = sd((T, _LANES, lanes), F32)
    return pl.pallas_call(
        body, name="s5_bwd", grid=(T,),
        in_specs=[u_spec, u_spec, bd_spec, bd_spec, cd_spec, cd_spec, lam_spec, lam_spec, d_spec],
        out_specs=[u_spec, bd_spec, bd_spec, bd_spec, bd_spec, lam_spec, lam_spec, d_spec],
        out_shape=[sd((L, Wd), _MXU), big, big, big, big, sd((T, 1, lanes), F32), sd((T, 1, lanes), F32),
                   sd((1, Wd), F32)],
        scratch_shapes=[pltpu.VMEM((L, lanes), F32) for _ in range(4)]
        + [pltpu.VMEM((L, _LANES), F32) for _ in range(3)],
        compiler_params=_cparams(("arbitrary",)),
    )(u, dg, bd_re, bd_im, cd_re, cd_im, lam_re, lam_im, d)


def _half_tile(R, few_arrays=False):
    for t in ((512, 704, 128) if few_arrays else (256, 352, 128)):
        if R % t == 0:
            return t
    raise ValueError(R)


def _cast_shard(name, w, layer, kind, R, C):
    tr = _half_tile(R, True)
    nr = R // tr

    def body(w_ref, o_ref):
        o_ref[...] = w_ref[...].astype(o_ref.dtype)

    if kind == "col":
        in_map = lambda h, i: (layer, h * nr + i, 0)
    else:
        in_map = lambda h, i: (layer, i, h)
    return pl.pallas_call(
        body, name=name, grid=(2, nr), in_specs=[pl.BlockSpec((None, tr, C), in_map)],
        out_specs=pl.BlockSpec((None, tr, C), lambda h, i: (h, i, 0)),
        out_shape=jax.ShapeDtypeStruct((2, R, C), _WIRE),
        compiler_params=_cparams(("arbitrary", "arbitrary")),
    )(w)


def _adam_math(w, g, m, v):
    m2 = _B1 * m + (1.0 - _B1) * g
    v2 = _B2 * v + (1.0 - _B2) * (g * g)
    m_hat = m2 / (1.0 - _B1 ** _STEP)
    v_hat = v2 / (1.0 - _B2 ** _STEP)
    delta = -_LR * (m_hat / (jnp.sqrt(v_hat) + _AEPS) + _WD * w)
    return delta, m2, v2


def _adamw_big(name, w, m, v, layer, pair, kind, R, C, c, after, prev):
    tr = _half_tile(R)
    nr = R // tr

    def body(c_ref, w_ref, m_ref, v_ref, own_ref, other_ref, *rest):
        go_ref, d_ref, mo_ref, vo_ref = rest[-4:]
        g = jnp.where(pl.program_id(0) == c_ref[0], own_ref[...], other_ref[...])
        delta, m2, v2 = _adam_math(w_ref[...], g, m_ref[...], v_ref[...])
        go_ref[...] = g
        d_ref[...] = delta
        mo_ref[...] = m2
        vo_ref[...] = v2

    if kind == "col":
        nat = pl.BlockSpec((None, tr, C), lambda h, i, c_ref: (layer, h * nr + i, 0))
    else:
        nat = pl.BlockSpec((None, tr, C), lambda h, i, c_ref: (layer, i, h))

    def gspec(own):
        return pl.BlockSpec((tr, C), lambda h, i, c_ref: (jnp.where((h == c_ref[0]) == own, i, 0), 0))

    carried = list(prev) if prev is not None else []
    gs = pltpu.PrefetchScalarGridSpec(
        num_scalar_prefetch=1, grid=(2, nr),
        in_specs=[nat, nat, nat, gspec(True), gspec(False), _any()] + [_any()] * len(carried),
        out_specs=[nat] * 4)
    sd = jax.ShapeDtypeStruct(w.shape, F32)
    return pl.pallas_call(
        body, name=name, grid_spec=gs, out_shape=[sd] * 4,
        input_output_aliases={7 + k: k for k in range(len(carried))},
        compiler_params=_cparams(("arbitrary", "arbitrary")),
    )(c.astype(jnp.int32).reshape(1), w, m, v, pair[0], pair[1], after, *carried)


def _adamw_flat(w, g, m, v):
    rows = w.shape[0]
    tr = rows // 8 if rows % 64 == 0 else rows

    def body(w_ref, g_ref, m_ref, v_ref, d_ref, mo_ref, vo_ref):
        delta, m2, v2 = _adam_math(w_ref[...], g_ref[...], m_ref[...], v_ref[...])
        d_ref[...] = delta
        mo_ref[...] = m2
        vo_ref[...] = v2

    spec = pl.BlockSpec((tr, _LANES), lambda i: (i, 0))
    sd = jax.ShapeDtypeStruct(w.shape, F32)
    return pl.pallas_call(
        body, name="adamw_small", grid=(rows // tr,), in_specs=[spec] * 4, out_specs=[spec] * 3,
        out_shape=[sd] * 3, compiler_params=_cparams(("arbitrary",)),
    )(w, g, m, v)


def _place():
    x, y, c = lax.axis_index("x"), lax.axis_index("y"), lax.axis_index("c")
    chips = [(1 - x, y), (x, 1 - y), (1 - x, 1 - y)]
    return x, y, c, 2 * x + y, chips


def _any():
    return pl.BlockSpec(memory_space=pl.ANY)


def _remote(src, dst, ssem, rsem, dev):
    return pltpu.make_async_remote_copy(src_ref=src, dst_ref=dst, send_sem=ssem, recv_sem=rsem,
                                        device_id=dev, device_id_type=_MESH)


def _allgather(name, shards):
    n = len(shards)

    def body(*refs):
        s_refs, g_refs = refs[:n], refs[n:2 * n]
        send0, recv0, send1, recv1, send2, recv2 = refs[2 * n:]
        x, y, c, q, _ = _place()
        sib, xn, yn = (x, y, 1 - c), (1 - x, y, c), (x, 1 - y, c)
        qx, qy, qd = 2 * (1 - x) + y, 2 * x + (1 - y), 2 * (1 - x) + (1 - y)
        _handshake([sib, xn, yn])
        own = [_remote(s_refs[a], g_refs[a].at[q], send0.at[a], recv0.at[a], sib) for a in range(n)]

        def pieces(a):
            g, half = g_refs[a], s_refs[a].shape[1] // 2
            return [g.at[qx, c], g.at[qy, c], g.at[qd, c, pl.ds(0, half)], g.at[qd, c, pl.ds(half, half)]]

        def relayed(a):
            g, half = g_refs[a], s_refs[a].shape[1] // 2
            return [(g.at[qx, c, pl.ds(0, half)], yn), (g.at[qy, c, pl.ds(half, half)], xn)]

        first = []
        for a in range(n):
            first.append(_remote(s_refs[a].at[c], g_refs[a].at[q, c], send1.at[4 * a], recv1.at[4 * a], xn))
            first.append(_remote(s_refs[a].at[c], g_refs[a].at[q, c], send1.at[4 * a + 1], recv1.at[4 * a + 1], yn))
        for cp in first + own:
            cp.start()
        later = []
        for a in range(n):
            land = pieces(a)
            for j in range(4):
                k = 4 * a + j
                _remote(land[j], land[j], send1.at[k], recv1.at[k], xn).wait_recv()
                if j < 2:
                    src, to = relayed(a)[j]
                    cp = _remote(src, src, send1.at[k + 2], recv1.at[k + 2], to)
                    cp.start()
                    later.append(cp)
                cp = _remote(land[j], land[j], send2.at[k], recv2.at[k], sib)
                cp.start()
                later.append(cp)
        for a in range(n):
            g, half = g_refs[a], s_refs[a].shape[1] // 2
            theirs = [g.at[qx, 1 - c], g.at[qy, 1 - c], g.at[qd, 1 - c, pl.ds(0, half)],
                      g.at[qd, 1 - c, pl.ds(half, half)]]
            for j in range(4):
                _remote(theirs[j], theirs[j], send2.at[4 * a + j], recv2.at[4 * a + j], sib).wait_recv()
        for cp in own:
            cp.wait()
        for cp in first + later:
            cp.wait_send()

    return _sequencer(name, _ID_GATHER, body, shards,
                      [jax.ShapeDtypeStruct((4,) + s.shape, s.dtype) for s in shards], [n, n] + [4 * n] * 4)


def _handshake(peers):
    barrier = pltpu.get_barrier_semaphore()
    for peer in peers:
        pl.semaphore_signal(barrier, inc=1, device_id=peer, device_id_type=_MESH)
    pl.semaphore_wait(barrier, len(peers))


_ID_SIBLING, _ID_CHIPS, _ID_GATHER, _ID_ALL = 1, 2, 3, 4


def _sequencer(name, collective_id, body, ins, out_types, sem_counts):
    mesh = plsc.ScalarSubcoreMesh(axis_name="seq", num_cores=1)
    return pl.kernel(
        body, name=name, out_type=out_types, mesh=mesh,
        scratch_types=[pltpu.SemaphoreType.DMA((k,)) for k in sem_counts],
        compiler_params=pltpu.CompilerParams(collective_id=collective_id),
    )(*ins)


def _swap_halves(name, grads):
    n = len(grads)

    def body(*refs):
        g_refs, t_refs = refs[:n], refs[n:2 * n]
        send, recv = refs[2 * n:]
        x, y, c, _, _ = _place()
        _handshake([(x, y, 1 - c)])
        cps = [_remote(g_refs[a].at[1 - c], t_refs[a], send.at[a], recv.at[a], (x, y, 1 - c)) for a in range(n)]
        for cp in cps:
            cp.start()
        for cp in cps:
            cp.wait()

    return _sequencer(name, _ID_SIBLING, body, grads,
                      [jax.ShapeDtypeStruct(g.shape[1:], g.dtype) for g in grads], [n, n])


def _chip_sum(name, g, t, after):
    _, _, R, C = g.shape
    tr = _half_tile(R, True)

    def body(c_ref, g_ref, t_ref, after_ref, o_ref):
        o_ref[...] = (g_ref[...].astype(F32) + t_ref[...].astype(F32)).astype(o_ref.dtype)

    c = lax.axis_index("c").astype(jnp.int32).reshape(1)
    gs = pltpu.PrefetchScalarGridSpec(
        num_scalar_prefetch=1, grid=(4, R // tr),
        in_specs=[pl.BlockSpec((None, None, tr, C), lambda r, i, c_ref: (c_ref[0], r, i, 0)),
                  pl.BlockSpec((None, tr, C), lambda r, i, c_ref: (r, i, 0)), _any()],
        out_specs=pl.BlockSpec((None, tr, C), lambda r, i, c_ref: (r, i, 0)))
    return pl.pallas_call(
        body, name=name, grid_spec=gs, out_shape=jax.ShapeDtypeStruct((4, R, C), _WIRE),
        compiler_params=_cparams(("arbitrary", "arbitrary")),
    )(c, g, t, after)


def _scatter_parts(name, parts):
    n = len(parts)

    def body(*refs):
        p_refs, t_refs = refs[:n], refs[n:2 * n]
        send, recv = refs[2 * n:]
        x, y, c, q, chips = _place()
        _handshake([(rx, ry, c) for rx, ry in chips])
        cps = []
        for a in range(n):
            for j, (rx, ry) in enumerate(chips):
                k = 3 * a + j
                cps.append(_remote(p_refs[a].at[2 * rx + ry], t_refs[a].at[q], send.at[k], recv.at[k], (rx, ry, c)))
        for cp in cps:
            cp.start()
        for a in range(n):
            for j, (rx, ry) in enumerate(chips):
                k = 3 * a + j
                land = t_refs[a].at[2 * rx + ry]
                _remote(land, land, send.at[k], recv.at[k], (rx, ry, c)).wait_recv()
        for cp in cps:
            cp.wait_send()

    return _sequencer(name, _ID_CHIPS, body, parts,
                      [jax.ShapeDtypeStruct(p.shape, p.dtype) for p in parts], [3 * n, 3 * n])


def _sum_parts(name, p, t, where, after):
    _, R, C = t.shape
    tr = _half_tile(R, True)

    def body(w_ref, p_ref, t0_ref, t1_ref, t2_ref, after_ref, o_ref):
        o_ref[...] = (p_ref[...].astype(F32) + t0_ref[...].astype(F32)
                      + t1_ref[...].astype(F32) + t2_ref[...].astype(F32))

    def part(slot):
        return pl.BlockSpec((None, tr, C), lambda i, w_ref: (w_ref[slot], i, 0))

    gs = pltpu.PrefetchScalarGridSpec(
        num_scalar_prefetch=1, grid=(R // tr,), in_specs=[part(0), part(1), part(2), part(3), _any()],
        out_specs=pl.BlockSpec((tr, C), lambda i, w_ref: (i, 0)))
    return pl.pallas_call(
        body, name=name, grid_spec=gs, out_shape=jax.ShapeDtypeStruct((R, C), F32),
        compiler_params=_cparams(("arbitrary",)),
    )(where, p, t, t, t, after)


def _send_halves(name, halves):
    n = len(halves)

    def body(*refs):
        h_refs, o_refs = refs[:n], refs[n:2 * n]
        send, recv = refs[2 * n:]
        x, y, c, _, _ = _place()
        _handshake([(x, y, 1 - c)])
        cps = [_remote(h_refs[a], o_refs[a], send.at[a], recv.at[a], (x, y, 1 - c)) for a in range(n)]
        for cp in cps:
            cp.start()
        for cp in cps:
            cp.wait()

    return _sequencer(name, _ID_SIBLING, body, halves,
                      [jax.ShapeDtypeStruct(h.shape, h.dtype) for h in halves], [n, n])


class _Reduction:
    def __init__(self, tag, grads):
        self.tag, self.grads = tag, grads
        self.swapped = _swap_halves(f"rs_swap_{tag}", grads)

    def scatter(self, after):
        self.parts = [_chip_sum(f"rs_chipsum_{self.tag}_{a}", g, t, after)
                      for a, (g, t) in enumerate(zip(self.grads, self.swapped))]
        self.landed = _scatter_parts(f"rs_scatter_{self.tag}", self.parts)

    def finish(self, after):
        x, y, c, q, chips = _place()
        where = jnp.stack([q] + [2 * rx + ry for rx, ry in chips]).astype(jnp.int32)
        halves = [_sum_parts(f"rs_sum_{self.tag}_{a}", p, t, where, after)
                  for a, (p, t) in enumerate(zip(self.parts, self.landed))]
        return list(zip(halves, _send_halves(f"rs_join_{self.tag}", halves)))


def _allreduce_small(name, v):
    _, R, _ = v.shape

    def body(v_ref, o_ref, land, acc, send1, recv1, send2, recv2):
        x, y, c = lax.axis_index("x"), lax.axis_index("y"), lax.axis_index("c")
        me = 4 * x + 2 * y + c
        peers = []
        for k in range(1, 8):
            dx, dy, dc = (k >> 2) & 1, (k >> 1) & 1, k & 1
            px, py, pc = (1 - x if dx else x), (1 - y if dy else y), (1 - c if dc else c)
            peers.append((k, (px, py, pc), 4 * px + 2 * py + pc))
        land[me] = v_ref[me]
        out1 = [_remote(v_ref.at[pid], land.at[me], send1.at[k], recv1.at[k], dev) for k, dev, pid in peers]
        for cp in out1:
            cp.start()
        for k, dev, pid in peers:
            _remote(land.at[pid], land.at[pid], send1.at[k], recv1.at[k], dev).wait_recv()
        total = land[0]
        for j in range(1, 8):
            total = total + land[j]
        acc[...] = total
        o_ref[me] = total
        out2 = [_remote(acc, o_ref.at[me], send2.at[k], recv2.at[k], dev) for k, dev, pid in peers]
        for cp in out2:
            cp.start()
        for k, dev, pid in peers:
            _remote(o_ref.at[pid], o_ref.at[pid], send2.at[k], recv2.at[k], dev).wait_recv()
        for cp in out1 + out2:
            cp.wait_send()

    return pl.pallas_call(
        body, name=name, in_specs=[_vm()], out_specs=_vm(),
        out_shape=jax.ShapeDtypeStruct(v.shape, F32),
        scratch_shapes=[pltpu.VMEM(v.shape, F32), pltpu.VMEM((R, _LANES), F32)]
        + [pltpu.SemaphoreType.DMA((8,)) for _ in range(4)],
        compiler_params=pltpu.CompilerParams(vmem_limit_bytes=_VMEM_LIMIT),
    )(v)


def _all_peers():
    x, y, c = lax.axis_index("x"), lax.axis_index("y"), lax.axis_index("c")
    peers = []
    for k in range(1, 8):
        px, py, pc = (1 - x if k & 4 else x), (1 - y if k & 2 else y), (1 - c if k & 1 else c)
        peers.append((k, (px, py, pc), 4 * px + 2 * py + pc))
    return 4 * x + 2 * y + c, peers


def _exchange_slices(name, v):
    def body(v_ref, land, send, recv):
        me, peers = _all_peers()
        _handshake([dev for _, dev, _ in peers])
        cps = [_remote(v_ref.at[pid], land.at[me], send.at[k], recv.at[k], dev) for k, dev, pid in peers]
        for cp in cps:
            cp.start()
        for k, dev, pid in peers:
            _remote(land.at[pid], land.at[pid], send.at[k], recv.at[k], dev).wait_recv()
        for cp in cps:
            cp.wait_send()

    return _sequencer(name, _ID_ALL, body, [v], [jax.ShapeDtypeStruct(v.shape, v.dtype)], [8, 8])[0]


def _sum_slices(name, v, landed, after):
    _, R, _ = v.shape

    def body(v_ref, land_ref, after_ref, o_ref):
        me, peers = _all_peers()
        acc = v_ref[me]
        for _, _, pid in peers:
            acc = acc + land_ref[pid]
        o_ref[...] = acc

    return pl.pallas_call(
        body, name=name, in_specs=[_vm(), _vm(), _any()], out_specs=_vm(),
        out_shape=jax.ShapeDtypeStruct((R, _LANES), F32),
        compiler_params=pltpu.CompilerParams(vmem_limit_bytes=_VMEM_LIMIT),
    )(v, landed, after)


def _broadcast_slices(name, s):
    def body(s_ref, out, send, recv):
        me, peers = _all_peers()
        _handshake([dev for _, dev, _ in peers])
        cps = [_remote(s_ref, out.at[me], send.at[k], recv.at[k], dev) for k, dev, pid in peers]
        for cp in cps:
            cp.start()
        for k, dev, pid in peers:
            _remote(out.at[pid], out.at[pid], send.at[k], recv.at[k], dev).wait_recv()
        for cp in cps:
            cp.wait_send()

    return _sequencer(name, _ID_ALL, body, [s], [jax.ShapeDtypeStruct((8,) + s.shape, s.dtype)], [8, 8])[0]


_WEIGHT_NAMES = ['norm_mix', 'norm_ffn', 'norm_ple', 'norm_final', 'gm_w_in', 'gm_ln_g', 'gm_ln_b', 'gm_w_s',
                 'gm_b_s', 'gm_w_out', 's5_w_in', 's5_a_re', 's5_a_im', 's5_log_dt', 's5_b_re', 's5_b_im',
                 's5_c_re', 's5_c_im', 's5_d', 's5_w_out', 'ffn_w1', 'ffn_w3', 'ffn_w2', 'ple_w_gate', 'ple_w_proj']
_BIG = {'gm_w_in': 'col', 'gm_w_out': 'row', 's5_w_in': 'row', 's5_w_out': 'col', 'ffn_w1': 'col',
        'ffn_w3': 'col', 'ffn_w2': 'row', 'ple_w_gate': 'row', 'ple_w_proj': 'col'}


_VIEW = {'s5_a_re': (0, 2, 1), 's5_a_im': (0, 2, 1), 's5_b_re': (0, 2, 3, 1), 's5_b_im': (0, 2, 3, 1),
         's5_c_re': (0, 2, 3, 1), 's5_c_im': (0, 2, 3, 1)}


def _to_view(name, a):
    return jnp.transpose(a, _VIEW[name]) if name in _VIEW else a


def _from_view(name, a):
    if name not in _VIEW:
        return a
    perm = _VIEW[name]
    return jnp.transpose(a, [perm.index(i) for i in range(len(perm))])


def _rc(kind, shard_shape):
    rows, cols = shard_shape[-2:]
    return (rows // 2, cols) if kind == "col" else (rows, cols // 2)


def _pack(vecs, rows_multiple):
    flat = jnp.concatenate([a.reshape(-1).astype(F32) for a in vecs])
    unit = rows_multiple * _LANES
    pad = (-flat.shape[0]) % unit
    return jnp.pad(flat, (0, pad)).reshape(-1, _LANES)


def _unpack(buf, shapes):
    flat = buf.reshape(-1)
    out, off = [], 0
    for s in shapes:
        n = math.prod(s)
        out.append(flat[off:off + n].reshape(s))
        off += n
    return out


def _ident(accs, ex):
    return accs


def _add_resid(accs, ex):
    return [accs[0] + ex[0]]


def _swiglu_epi(accs, ex):
    a, b = accs
    return [a, b, a * _sig(a) * b]


def _swiglu_bwd_epi(accs, ex):
    df = accs[0]
    a, b = ex[0].astype(F32), ex[1].astype(F32)
    sa = _sig(a)
    return [df * b * (sa * (1.0 + a * (1.0 - sa))), df * (a * sa)]


def _ple_epi(accs, ex):
    gt = _sig(accs[0])
    return [ex[0] + gt * ex[1], gt]


def _glu_epi(accs, ex):
    val, sg = accs[0], _sig(accs[1])
    return [ex[0] + val * sg, val, sg]


def _block_diag(t):
    eye = jnp.eye(_S5_GT, dtype=t.dtype)
    return t[:, :, :, None, :] * eye[None, :, None, :, None]


def _diag_blocks(t, T, a, b):
    d = jnp.diagonal(t.reshape(T, _S5_GT, a, _S5_GT, b), axis1=1, axis2=3)
    return jnp.transpose(d, (0, 3, 1, 2))


def kernel(x, p, norm_mix, norm_ffn, norm_ple, norm_final, gm_w_in, gm_ln_g, gm_ln_b, gm_w_s, gm_b_s, gm_w_out, s5_w_in, s5_a_re, s5_a_im, s5_log_dt, s5_b_re, s5_b_im, s5_c_re, s5_c_im, s5_d, s5_w_out, ffn_w1, ffn_w3, ffn_w2, ple_w_gate, ple_w_proj, loss_target, m_norm_mix, m_norm_ffn, m_norm_ple, m_norm_final, m_gm_w_in, m_gm_ln_g, m_gm_ln_b, m_gm_w_s, m_gm_b_s, m_gm_w_out, m_s5_w_in, m_s5_a_re, m_s5_a_im, m_s5_log_dt, m_s5_b_re, m_s5_b_im, m_s5_c_re, m_s5_c_im, m_s5_d, m_s5_w_out, m_ffn_w1, m_ffn_w3, m_ffn_w2, m_ple_w_gate, m_ple_w_proj, v_norm_mix, v_norm_ffn, v_norm_ple, v_norm_final, v_gm_w_in, v_gm_ln_g, v_gm_ln_b, v_gm_w_s, v_gm_b_s, v_gm_w_out, v_s5_w_in, v_s5_a_re, v_s5_a_im, v_s5_log_dt, v_s5_b_re, v_s5_b_im, v_s5_c_re, v_s5_c_im, v_s5_d, v_s5_w_out, v_ffn_w1, v_ffn_w3, v_ffn_w2, v_ple_w_gate, v_ple_w_proj):
    env = dict(locals())
    w = {n: env[n] for n in _WEIGHT_NAMES}
    mom = {n: env["m_" + n] for n in _WEIGHT_NAMES}
    var = {n: env["v_" + n] for n in _WEIGHT_NAMES}
    xs, tgt = x[0], loss_target[0]
    L, D = xs.shape
    depth = norm_mix.shape[0]
    qx, qy = lax.axis_index("x"), lax.axis_index("y")
    q = 2 * qx + qy

    def gather(tag, items):
        shards = []
        for name, layer in items:
            kind = _BIG[name]
            R, C = _rc(kind, w[name].shape)
            shards.append(_cast_shard(f"cast_{name}{layer}", w[name], layer, kind, R, C))
        full = _allgather(f"ag_{tag}", shards)
        return {it: _W(f, _BIG[it[0]]) for it, f in zip(items, full)}

    W = {}
    mixers = [["gm_w_in", "gm_w_out"], ["s5_w_in", "s5_w_out"]]
    for i in range(depth):
        for n in mixers[i]:
            W.update(gather(f"{n}", [(n, 0)]))
        W.update(gather(f"ffn_up{i}", [("ffn_w1", i), ("ffn_w3", i)]))
        W.update(gather(f"ffn_down{i}", [("ffn_w2", i)]))
        W.update(gather(f"ple{i}", [("ple_w_gate", i), ("ple_w_proj", i)]))

    d_slots = jnp.zeros((4, D // 4), F32)
    d_slots = lax.dynamic_update_slice(d_slots, s5_d.astype(F32), (q, 0))
    d_sum = _allreduce_small("ar_s5_d", _pack([d_slots], 64).reshape(8, -1, _LANES))
    d_full = (d_sum.reshape(-1)[:D] * 0.5).reshape(1, D)

    def ffn_fwd(i, xin):
        hf = _rms_fwd(f"rms_ffn{i}", xin, norm_ffn[i:i + 1])
        a, b, f = _mm_nn(f"ffn_up{i}", hf, [W["ffn_w1", i], W["ffn_w3", i]], 1024, 1408, ffn_w2.shape[1] * 4,
                         [], [_MXU, _MXU, _MXU], _swiglu_epi)
        xo = _mm_nn(f"ffn_down{i}", f, [W["ffn_w2", i]], 1408, 1024, D, [xin], [F32], _add_resid, tm=1024)[0]
        return xo, (xin, hf, a, b, f)

    def ple_fwd(i, xin):
        hp = _rms_fwd(f"rms_ple{i}", xin, norm_ple[i:i + 1])
        pi = p[i, 0]
        pp = _mm_nn(f"ple_proj{i}", pi, [W["ple_w_proj", i]], 128, 512, D, [], [F32], _ident)[0]
        xo, gt = _mm_nn(f"ple_gate{i}", hp, [W["ple_w_gate", i]], 512, 1024, D, [xin, pp], [F32, _MXU], _ple_epi,
                        tm=1024)
        return xo, (xin, hp, pi, pp, gt)

    h0 = _rms_fwd("rms_mix0", xs, norm_mix[0:1])
    z = _mm_nn("gm_in", h0, [W["gm_w_in", 0]], 1024, 1024, 2 * D, [], [F32], _ident)[0]
    bsT = gm_b_s[0].T
    gm_m = _gmlp_fwd(z, gm_ln_g, gm_ln_b, gm_w_s[0], bsT)
    x1 = _mm_nn("gm_out", gm_m, [W["gm_w_out", 0]], 512, 1024, D, [xs], [F32], _add_resid, tm=1024)[0]
    x2, ffn0 = ffn_fwd(0, x1)
    x3, ple0 = ple_fwd(0, x2)

    T = D // _LANES
    lanes = _S5_GT * _S5_P
    sv = {n: _to_view(n, w[n])[0] for n in _VIEW}
    a_re, a_im, log_dt = sv["s5_a_re"], sv["s5_a_im"], s5_log_dt
    lbr, lbi, Bbar_re, Bbar_im = _s5_prep(a_re, a_im, log_dt, sv["s5_b_re"], sv["s5_b_im"])

    def to_bd(B):
        t = jnp.transpose(B.reshape(_S5_P, _S5_C, T, _S5_GT), (2, 3, 1, 0))
        return _block_diag(t).reshape(T, _LANES, lanes).astype(_MXU)

    def to_cd(cw):
        t = jnp.transpose(cw.reshape(_S5_C, _S5_P, T, _S5_GT), (2, 3, 1, 0))
        return _block_diag(t).reshape(T, lanes, _LANES).astype(_MXU)

    def to_lam(v):
        return jnp.transpose(v).reshape(T, 1, lanes)

    bd_re, bd_im = to_bd(Bbar_re), to_bd(Bbar_im)
    cd_re, cd_im = to_cd(sv["s5_c_re"]), to_cd(sv["s5_c_im"])
    lam_re, lam_im = to_lam(lbr), to_lam(lbi)

    h1 = _rms_fwd("rms_mix1", x3, norm_mix[1:2])
    u = _mm_nn("s5_in", h1, [W["s5_w_in", 0]], 512, 1024, D, [], [F32], _ident, tm=1024)[0]
    s5_g = _s5_fwd(u, bd_re, bd_im, cd_re, cd_im, lam_re, lam_im, d_full)
    x4, glu_val, glu_sg = _mm_nn("s5_out", s5_g, [W["s5_w_out", 0], W["s5_w_out", 0]], 1024, 1024, D, [x3],
                                 [F32, _MXU, _MXU], _glu_epi, cb_offsets=[0, 2])
    x5, ffn1 = ffn_fwd(1, x4)
    x6, ple1 = ple_fwd(1, x5)

    dx, d_norm_final, loss_rows = _loss_head(x6, norm_final[None], tgt)
    loss = lax.psum(loss_rows[0, 0], ("x", "y", "c"))

    small = {}

    def ple_bwd(i, dxo, saved):
        xin, hp, pi, pp, gt = saved
        dpre, dpp = _ple_bwd_elem(dxo, pp, gt)
        dwg = _mm_tn(f"ple_gate_dw{i}", hp, [dpre], "row", 512, 1024, 512, 1024)[0]
        dwp = _mm_tn(f"ple_proj_dw{i}", pi, [dpp], "col", 128, 512, 128, 512)[0]
        red = _Reduction(f"ple{i}", [dwg, dwp])
        dhp = _mm_nt(f"ple_gate_dx{i}", [dpre], [W["ple_w_gate", i]], 512, 1024, [], [F32], _ident, tm=2048)[0]
        dxin, dg = _rms_bwd(f"rms_ple_bwd{i}", dhp, xin, norm_ple[i:i + 1], dxo)
        return dxin, dg, red, dpre

    def ffn_bwd(i, dxo, saved):
        xin, hf, a, b, f = saved
        da, db = _mm_nt(f"ffn_down_dx{i}", [dxo], [W["ffn_w2", i]], 1408, 1024, [a, b], [_MXU, _MXU],
                        _swiglu_bwd_epi)
        dw1, dw3 = _mm_tn(f"ffn_up_dw{i}", hf, [da, db], "col", 1024, 1408, 1024, 1408)
        dw2 = _mm_tn(f"ffn_down_dw{i}", f, [dxo], "row", 1408, 1024, 1408, 1024)[0]
        red = _Reduction(f"ffn{i}", [dw1, dw3, dw2])
        dhf = _mm_nt(f"ffn_up_dx{i}", [da, db], [W["ffn_w1", i], W["ffn_w3", i]], 1024, 1408, [], [F32], _ident,
                     tm=1024)[0]
        dxin, dg = _rms_bwd(f"rms_ffn_bwd{i}", dhf, xin, norm_ffn[i:i + 1], dxo)
        return dxin, dg, red, da, dhf

    d_norm_ple, d_norm_ffn, d_norm_mix = [None] * depth, [None] * depth, [None] * depth
    reduced = {}

    def keep(names, layer, pairs):
        for n, pr in zip(names, pairs):
            reduced[n, layer] = pr

    ple_names, ffn_names = ["ple_w_gate", "ple_w_proj"], ["ffn_w1", "ffn_w3", "ffn_w2"]

    dx, d_norm_ple[1], r_ple1, _ = ple_bwd(1, dx, ple1)
    dx, d_norm_ffn[1], r_ffn1, da1, _ = ffn_bwd(1, dx, ffn1)
    r_ple1.scatter(da1)

    do = _glu_bwd_elem(dx, glu_val, glu_sg)
    r_ffn1.scatter(do)
    dw_s5out = _mm_tn("s5_out_dw", s5_g, [do], "col", 1024, 1024, 1024, 1024)[0]
    dgy = _mm_nt("s5_out_dx", [do], [W["s5_w_out", 0]], 1024, 1024, [], [F32], _ident, tm=1024)[0]
    keep(ple_names, 1, r_ple1.finish(dgy))
    du, dbd_re, dbd_im, dcd_re, dcd_im, dl_re, dl_im, dd = _s5_bwd(u, dgy, bd_re, bd_im, cd_re, cd_im,
                                                                  lam_re, lam_im, d_full)
    dw_s5in = _mm_tn("s5_in_dw", h1, [du], "row", 512, 1024, 512, 1024)[0]
    r_mix1 = _Reduction("mix1", [dw_s5in, dw_s5out])
    dh1 = _mm_nt("s5_in_dx", [du], [W["s5_w_in", 0]], 512, 1024, [], [F32], _ident, tm=2048)[0]
    dx, d_norm_mix[1] = _rms_bwd("rms_mix1_bwd", dh1, x3, norm_mix[1:2], dx)

    def from_bd(t):
        return jnp.transpose(_diag_blocks(t, T, _S5_C, _S5_P), (3, 2, 0, 1)).reshape(_S5_P, _S5_C, T * _S5_GT)

    def from_cdT(t):
        return jnp.transpose(_diag_blocks(t, T, _S5_C, _S5_P), (2, 3, 0, 1)).reshape(_S5_C, _S5_P, T * _S5_GT)

    def from_lam(t):
        return jnp.transpose(t.reshape(T * _S5_GT, _S5_P))

    da_re, da_im, dlog_dt, db_re, db_im = _s5_prep_bwd(
        a_re, a_im, log_dt, sv["s5_b_re"], sv["s5_b_im"], from_lam(dl_re), from_lam(dl_im),
        from_bd(dbd_re), from_bd(dbd_im))
    small["s5_a_re"], small["s5_a_im"], small["s5_log_dt"] = da_re[None], da_im[None], dlog_dt
    small["s5_b_re"], small["s5_b_im"] = db_re[None], db_im[None]
    small["s5_c_re"], small["s5_c_im"] = from_cdT(dcd_re)[None], from_cdT(dcd_im)[None]

    dx, d_norm_ple[0], r_ple0, dpre0 = ple_bwd(0, dx, ple0)
    r_mix1.scatter(dpre0)
    keep(ffn_names, 1, r_ffn1.finish(dx))
    dx, d_norm_ffn[0], r_ffn0, da0, dhf0 = ffn_bwd(0, dx, ffn0)
    r_ple0.scatter(da0)
    keep(["s5_w_in", "s5_w_out"], 0, r_mix1.finish(dhf0))
    keep(ple_names, 0, r_ple0.finish(dhf0))

    dw_gmout = _mm_tn("gm_out_dw", gm_m, [dx], "row", 512, 1024, 512, 1024)[0]
    dgm = _mm_nt("gm_out_dx", [dx], [W["gm_w_out", 0]], 512, 1024, [], [F32], _ident, tm=2048)[0]
    r_ffn0.scatter(dgm)
    dz, dws, dbsT, dlng, dlnb = _gmlp_bwd(z, dgm, gm_ln_g, gm_ln_b, gm_w_s[0], bsT)
    dw_gmin = _mm_tn("gm_in_dw", h0, [dz], "col", 1024, 1024, 1024, 1024)[0]
    r_mix0 = _Reduction("mix0", [dw_gmin, dw_gmout])
    dh0 = _mm_nt("gm_in_dx", [dz], [W["gm_w_in", 0]], 1024, 1024, [], [F32], _ident, tm=1024)[0]
    dx, d_norm_mix[0] = _rms_bwd("rms_mix0_bwd", dh0, xs, norm_mix[0:1], dx)
    r_mix0.scatter(dx)
    grad_x = dx[None]

    small["norm_mix"], small["norm_ffn"] = jnp.concatenate(d_norm_mix), jnp.concatenate(d_norm_ffn)
    small["norm_ple"], small["norm_final"] = jnp.concatenate(d_norm_ple), d_norm_final[0]
    small["gm_ln_g"], small["gm_ln_b"], small["gm_w_s"] = dlng, dlnb, dws[None]
    small["gm_b_s"] = dbsT[:, :_GM_HEADS].T[None]
    small["s5_d"] = dd

    small_names = [n for n in _WEIGHT_NAMES if n not in _BIG]
    packed = _pack([small[n] for n in small_names], 64).reshape(8, -1, _LANES)
    landed = _exchange_slices("ar_small_in", packed)
    grads, deltas, new_m, new_v = {}, {}, {}, {}
    my_c = lax.axis_index("c")
    token = dx

    def adamw(n, layer, prev):
        nonlocal token
        kind = _BIG[n]
        R, C = _rc(kind, w[n].shape)
        outs = _adamw_big(f"adamw_{n}{layer}", w[n], mom[n], var[n], layer, reduced[n, layer], kind, R, C,
                          my_c, token, prev)
        token = outs[1]
        return outs

    late = {}
    for n in ffn_names:
        late[n] = adamw(n, 1, None)
    mine = _sum_slices("ar_small_sum", packed, landed, token)
    spread = _broadcast_slices("ar_small_out", mine)
    for n in ple_names:
        late[n] = adamw(n, 1, None)
    for n in ["s5_w_in", "s5_w_out"]:
        grads[n], deltas[n], new_m[n], new_v[n] = adamw(n, 0, None)
    for n in ple_names:
        grads[n], deltas[n], new_m[n], new_v[n] = adamw(n, 0, late[n])
    summed = lax.dynamic_update_slice(spread, mine[None], (4 * qx + 2 * qy + my_c, 0, 0))
    red_small = dict(zip(small_names, _unpack(summed, [small[n].shape for n in small_names])))
    red_small["s5_d"] = lax.dynamic_slice(red_small["s5_d"], (0, q * (D // 4)), (1, D // 4))
    keep(ffn_names, 0, r_ffn0.finish(token))
    for n in ffn_names:
        grads[n], deltas[n], new_m[n], new_v[n] = adamw(n, 0, late[n])
    keep(["gm_w_in", "gm_w_out"], 0, r_mix0.finish(token))
    for n in ["gm_w_in", "gm_w_out"]:
        grads[n], deltas[n], new_m[n], new_v[n] = adamw(n, 0, None)
    def views(src):
        return [_to_view(n, src[n]) for n in small_names]

    shapes = [v_.shape for v_ in views(w)]
    dl, mo, vo = _adamw_flat(_pack(views(w), 64), _pack([red_small[n] for n in small_names], 64),
                             _pack(views(mom), 64), _pack(views(var), 64))
    for n, g_, d_, m_, v_ in zip(small_names, [red_small[n] for n in small_names], _unpack(dl, shapes),
                                 _unpack(mo, shapes), _unpack(vo, shapes)):
        grads[n], deltas[n], new_m[n], new_v[n] = [_from_view(n, t_).reshape(w[n].shape) for t_ in (g_, d_, m_, v_)]

    return (loss, grad_x, *[grads[n] for n in _WEIGHT_NAMES], *[deltas[n] for n in _WEIGHT_NAMES],
            *[new_m[n] for n in _WEIGHT_NAMES], *[new_v[n] for n in _WEIGHT_NAMES])
```

```python
import functools
import math

import jax
import jax.numpy as jnp
from jax import lax
from jax.experimental import pallas as pl
from jax.experimental.pallas import tpu as pltpu
from jax.experimental.pallas import tpu_sc as plsc

F32 = jnp.float32
_MXU = jnp.bfloat16
_WIRE = jnp.bfloat16
_EPS = 1e-6
_VMEM_LIMIT = 56 * 1024 * 1024
_LANES = 128
_MESH = pl.DeviceIdType.MESH

_LR, _B1, _B2, _AEPS, _WD, _STEP = 0.001, 0.9, 0.999, 1e-08, 0.01, 10

_GM_CHUNK = 128
_GM_HEADS = 16
_S5_GT = 8
_S5_P = 64
_S5_C = 16

_NN = (((1,), (0,)), ((), ()))
_NT = (((1,), (1,)), ((), ()))
_TN = (((0,), (0,)), ((), ()))


def _cparams(sem):
    return pltpu.CompilerParams(dimension_semantics=sem, vmem_limit_bytes=_VMEM_LIMIT)


def _sig(x):
    return 1.0 / (1.0 + jnp.exp(-x))


_GC = math.sqrt(2.0 / math.pi)


def _gelu(x):
    return 0.5 * x * (1.0 + jnp.tanh(_GC * (x + 0.044715 * (x * x * x))))


def _gelu_grad(x):
    t = jnp.tanh(_GC * (x + 0.044715 * (x * x * x)))
    return 0.5 * (1.0 + t) + 0.5 * x * (1.0 - t * t) * (_GC * (1.0 + 3.0 * 0.044715 * x * x))


def _dot(a, b, dn):
    return lax.dot_general(a.astype(_MXU), b.astype(_MXU), dn, preferred_element_type=F32)


class _W:
    def __init__(self, arr, kind):
        self.a, self.kind = arr, kind
        self.R, self.C = arr.shape[2], arr.shape[3]

    def full_shape(self):
        return (2 * self.R, 4 * self.C) if self.kind == "col" else (4 * self.R, 2 * self.C)


def _part_index(kind, R, C, tr, tc, rb, cb):
    nr, nc = R // tr, C // tc
    if kind == "col":
        return cb // nc, rb // nr, rb % nr, cb % nc
    return rb // nr, cb // nc, rb % nr, cb % nc


def _wspec(w, tr, tc, rb_fn, cb_fn):
    assert w.R % tr == 0 and w.C % tc == 0, (w.R, w.C, tr, tc)

    def imap(i, j, k):
        return _part_index(w.kind, w.R, w.C, tr, tc, rb_fn(i, j, k), cb_fn(i, j, k))

    return pl.BlockSpec((None, None, tr, tc), imap)


def _gspec(kind, R, C, tr, tc):
    assert R % tr == 0 and C % tc == 0, (R, C, tr, tc)

    def imap(i, j, k):
        part, half, rbi, cbi = _part_index(kind, R, C, tr, tc, i, j)
        return half, part, rbi, cbi

    return pl.BlockSpec((None, None, tr, tc), imap)


def _mm(name, grid, a_ops, b_ops, pairs, acc_shape, n_acc, extras, outs, epilogue):
    nk = grid[2]
    na, nb, ne, no = len(a_ops), len(b_ops), len(extras), len(outs)

    def body(*refs):
        a_refs = refs[:na]
        b_refs = refs[na:na + nb]
        e_refs = refs[na + nb:na + nb + ne]
        o_refs = refs[na + nb + ne:na + nb + ne + no]
        acc_refs = refs[na + nb + ne + no:]
        k = pl.program_id(2)

        def products():
            sums = [None] * n_acc
            for ai, bi, ci, dn in pairs:
                d = _dot(a_refs[ai][...], b_refs[bi][...], dn)
                sums[ci] = d if sums[ci] is None else sums[ci] + d
            return sums

        def finish(accs):
            res = epilogue(accs, [e[...] for e in e_refs])
            for o, r in zip(o_refs, res):
                o[...] = r.astype(o.dtype)

        if nk == 1:
            finish(products())
            return

        @pl.when(k == 0)
        def _():
            for acc, d in zip(acc_refs, products()):
                acc[...] = d

        @pl.when(jnp.logical_and(k > 0, k < nk - 1))
        def _():
            for acc, d in zip(acc_refs, products()):
                acc[...] += d

        @pl.when(k == nk - 1)
        def _():
            finish([acc[...] + d for acc, d in zip(acc_refs, products())])

    ops = list(a_ops) + list(b_ops) + list(extras)
    return pl.pallas_call(
        body, name=name, grid=grid,
        in_specs=[s for _, s in ops],
        out_specs=[s for _, s in outs],
        out_shape=[s for s, _ in outs],
        scratch_shapes=[pltpu.VMEM(acc_shape, F32) for _ in range(n_acc if nk > 1 else 0)],
        compiler_params=_cparams(("parallel", "parallel", "arbitrary")),
    )(*[a for a, _ in ops])


def _bs(shape, fn):
    return pl.BlockSpec(shape, fn)


def _tile_m(L):
    return min(L, 512)


def _mm_nn(name, x, ws, tk, tn, n_out, extras, outs_sd, epilogue, tm=None, cb_offsets=None):
    M, K = x.shape
    tm = min(M, tm or _tile_m(M))
    grid = (M // tm, n_out // tn, K // tk)
    a_ops = [(x, _bs((tm, tk), lambda i, j, k: (i, k)))]
    cb_offsets = cb_offsets or [0] * len(ws)
    b_ops = [(w.a, _wspec(w, tk, tn, lambda i, j, k: k, (lambda off: lambda i, j, k: j + off)(off)))
             for w, off in zip(ws, cb_offsets)]
    pairs = [(0, bi, bi, _NN) for bi in range(len(ws))]
    mn = _bs((tm, tn), lambda i, j, k: (i, j))
    ex = [(e, mn) for e in extras]
    outs = [(jax.ShapeDtypeStruct((M, n_out), dt), mn) for dt in outs_sd]
    return _mm(name, grid, a_ops, b_ops, pairs, (tm, tn), len(ws), ex, outs, epilogue)


def _mm_nt(name, xs, ws, tn, tk, extras, outs_sd, epilogue, tm=None):
    M, Nw = xs[0].shape
    Kw = ws[0].full_shape()[0]
    tm = min(M, tm or _tile_m(M))
    grid = (M // tm, Kw // tn, Nw // tk)
    a_ops = [(x, _bs((tm, tk), lambda i, j, k: (i, k))) for x in xs]
    b_ops = [(w.a, _wspec(w, tn, tk, lambda i, j, k: j, lambda i, j, k: k)) for w in ws]
    pairs = [(i, i, 0, _NT) for i in range(len(ws))]
    mn = _bs((tm, tn), lambda i, j, k: (i, j))
    ex = [(e, mn) for e in extras]
    outs = [(jax.ShapeDtypeStruct((M, Kw), dt), mn) for dt in outs_sd]
    return _mm(name, grid, a_ops, b_ops, pairs, (tm, tn), 1, ex, outs, epilogue)


def _mm_tn(name, x, dys, kind, R, C, tm, tn, tk=None):
    L, Kw = x.shape
    Nw = dys[0].shape[1]
    tk = tk or min(L, 1024)
    grid = (Kw // tm, Nw // tn, L // tk)
    a_ops = [(x, _bs((tk, tm), lambda i, j, k: (k, i)))]
    b_ops = [(dy, _bs((tk, tn), lambda i, j, k: (k, j))) for dy in dys]
    pairs = [(0, bi, bi, _TN) for bi in range(len(dys))]
    gs = _gspec(kind, R, C, tm, tn)
    outs = [(jax.ShapeDtypeStruct((2, 4, R, C), _WIRE), gs) for _ in dys]
    return _mm(name, grid, a_ops, b_ops, pairs, (tm, tn), len(dys), [], outs, lambda accs, ex: accs)


def _row_tile(L):
    return min(L, 256)


def _rowwise(name, body, ins, outs, L, acc_outs=()):
    tr = _row_tile(L)
    n_in, n_out = len(ins), len(outs)

    def kbody(*refs):
        i_refs, o_refs, a_refs = refs[:n_in], refs[n_in:n_in + n_out], refs[n_in + n_out:]
        res, sums = body(*[r[...] for r in i_refs])
        for o, r in zip(o_refs, res):
            o[...] = r.astype(o.dtype)
        if a_refs:
            @pl.when(pl.program_id(0) == 0)
            def _():
                for a in a_refs:
                    a[...] = jnp.zeros(a.shape, F32)
            for a, s in zip(a_refs, sums):
                a[...] += s

    in_specs = []
    for arr, kind in ins:
        if kind == "row":
            in_specs.append(pl.BlockSpec((tr, arr.shape[1]), lambda i: (i, 0)))
        else:
            in_specs.append(pl.BlockSpec(arr.shape, lambda i: (0, 0)))
    out_specs = [pl.BlockSpec((tr, c), lambda i: (i, 0)) for c, _ in outs]
    out_shape = [jax.ShapeDtypeStruct((L, c), dt) for c, dt in outs]
    out_specs += [pl.BlockSpec((1, c), lambda i: (0, 0)) for c in acc_outs]
    out_shape += [jax.ShapeDtypeStruct((1, c), F32) for c in acc_outs]
    return pl.pallas_call(
        kbody, name=name, grid=(L // tr,), in_specs=in_specs, out_specs=out_specs, out_shape=out_shape,
        compiler_params=_cparams(("arbitrary",)),
    )(*[a for a, _ in ins])


def _rms_fwd(name, x, g):
    def body(xv, gv):
        r = lax.rsqrt(jnp.mean(xv * xv, axis=-1, keepdims=True) + _EPS)
        return [xv * r * gv], []
    return _rowwise(name, body, [(x, "row"), (g, "vec")], [(x.shape[1], _MXU)], x.shape[0])[0]


def _rms_bwd(name, dh, x, g, dres):
    def body(dhv, xv, gv, dr):
        r = lax.rsqrt(jnp.mean(xv * xv, axis=-1, keepdims=True) + _EPS)
        xh = xv * r
        dxh = dhv * gv
        dx = r * (dxh - xh * jnp.mean(dxh * xh, axis=-1, keepdims=True))
        return [dr + dx], [jnp.sum(dhv * xh, axis=0, keepdims=True)]
    D = x.shape[1]
    return _rowwise(name, body, [(dh, "row"), (x, "row"), (g, "vec"), (dres, "row")], [(D, F32)], x.shape[0], [D])


def _loss_head(x, g, target):
    D = x.shape[1]

    def body(xv, gv, tv):
        r = lax.rsqrt(jnp.mean(xv * xv, axis=-1, keepdims=True) + _EPS)
        xh = xv * r
        e = xh * gv - tv
        dy = e * (1.0 / D)
        dxh = dy * gv
        dx = r * (dxh - xh * jnp.mean(dxh * xh, axis=-1, keepdims=True))
        row_loss = 0.5 * jnp.mean(e * e, axis=-1, keepdims=True)
        lsum = jnp.sum(row_loss, axis=0, keepdims=True) + jnp.zeros((1, _LANES), F32)
        return [dx], [jnp.sum(dy * xh, axis=0, keepdims=True), lsum]
    return _rowwise("loss_head", body, [(x, "row"), (g, "vec"), (target, "row")], [(D, F32)], x.shape[0], [D, _LANES])


def _ple_bwd_elem(dx, pp, gt):
    def body(dxv, ppv, gtv):
        gt32 = gtv.astype(F32)
        return [dxv * ppv * gt32 * (1.0 - gt32), dxv * gt32], []
    D = dx.shape[1]
    return _rowwise("ple_bwd_elem", body, [(dx, "row"), (pp, "row"), (gt, "row")], [(D, _MXU), (D, _MXU)], dx.shape[0])


def _glu_bwd_elem(dx, val, sg):
    def body(dxv, vv, sv):
        v32, s32 = vv.astype(F32), sv.astype(F32)
        return [jnp.concatenate([dxv * s32, dxv * v32 * s32 * (1.0 - s32)], axis=1)], []
    D = dx.shape[1]
    return _rowwise("glu_bwd_elem", body, [(dx, "row"), (val, "row"), (sg, "row")], [(2 * D, _MXU)], dx.shape[0])[0]


def _gm_common(z, ln_g, ln_b, wc_bf, bsT):
    W = z.shape[1] // 2
    zu, zv = z[:, :W], z[:, W:]
    u, v = _gelu(zu), _gelu(zv)
    mu = jnp.mean(v, axis=-1, keepdims=True)
    vc = v - mu
    rstd = lax.rsqrt(jnp.mean(vc * vc, axis=-1, keepdims=True) + _EPS)
    vh = vc * rstd
    vn = vh * ln_g + ln_b
    vnb = vn.astype(_MXU)
    svs = []
    for h in range(_GM_HEADS):
        sl = slice(h * _LANES, (h + 1) * _LANES)
        svs.append(_dot(wc_bf[h], vnb[:, sl], _NN) + bsT[:, h:h + 1])
    return zu, zv, u, vh, rstd, vnb, svs


def _causal(w):
    t = lax.broadcasted_iota(jnp.int32, w.shape, w.ndim - 2)
    s = lax.broadcasted_iota(jnp.int32, w.shape, w.ndim - 1)
    return jnp.where(s <= t, w, jnp.zeros_like(w))


def _gmlp_fwd(z, ln_g, ln_b, w_s, bsT):
    L, W2 = z.shape
    W = W2 // 2

    def body(z_ref, g_ref, b_ref, ws_ref, bs_ref, m_ref):
        wc = _causal(ws_ref[...]).astype(_MXU)
        _, _, u, _, _, _, svs = _gm_common(z_ref[...], g_ref[...], b_ref[...], wc, bs_ref[...])
        for h in range(_GM_HEADS):
            sl = slice(h * _LANES, (h + 1) * _LANES)
            m_ref[:, sl] = (u[:, sl] * svs[h]).astype(m_ref.dtype)

    return pl.pallas_call(
        body, name="gmlp_fwd", grid=(L // _GM_CHUNK,),
        in_specs=[pl.BlockSpec((_GM_CHUNK, W2), lambda n: (n, 0)),
                  pl.BlockSpec((1, W), lambda n: (0, 0)), pl.BlockSpec((1, W), lambda n: (0, 0)),
                  pl.BlockSpec(w_s.shape, lambda n: (0, 0, 0)), pl.BlockSpec(bsT.shape, lambda n: (0, 0))],
        out_specs=pl.BlockSpec((_GM_CHUNK, W), lambda n: (n, 0)),
        out_shape=jax.ShapeDtypeStruct((L, W), _MXU),
        compiler_params=_cparams(("arbitrary",)),
    )(z, ln_g, ln_b, w_s, bsT)


def _gmlp_bwd(z, dm, ln_g, ln_b, w_s, bsT):
    L, W2 = z.shape
    W = W2 // 2
    T = _GM_CHUNK

    def body(z_ref, dm_ref, g_ref, b_ref, ws_ref, bs_ref, dz_ref, dws_ref, dbs_ref, dg_ref, db_ref):
        @pl.when(pl.program_id(0) == 0)
        def _():
            dws_ref[...] = jnp.zeros(dws_ref.shape, F32)
            dbs_ref[...] = jnp.zeros(dbs_ref.shape, F32)
            dg_ref[...] = jnp.zeros(dg_ref.shape, F32)
            db_ref[...] = jnp.zeros(db_ref.shape, F32)

        wc = _causal(ws_ref[...]).astype(_MXU)
        ln_g_v = g_ref[...]
        zu, zv, u, vh, rstd, vnb, svs = _gm_common(z_ref[...], ln_g_v, b_ref[...], wc, bs_ref[...])
        dmv = dm_ref[...]
        lane = lax.broadcasted_iota(jnp.int32, (T, _LANES), 1)
        dbs = jnp.zeros((T, _LANES), F32)
        dvn_parts = []
        for h in range(_GM_HEADS):
            sl = slice(h * _LANES, (h + 1) * _LANES)
            dsv = dmv[:, sl] * u[:, sl]
            dz_ref[:, sl] = (dmv[:, sl] * svs[h] * _gelu_grad(zu[:, sl])).astype(dz_ref.dtype)
            dbs = dbs + jnp.where(lane == h, jnp.sum(dsv, axis=1, keepdims=True), 0.0)
            dsvb = dsv.astype(_MXU)
            dws_ref[h] += _dot(dsvb, vnb[:, sl], _NT)
            dvn_parts.append(_dot(wc[h], dsvb, _TN))
        dbs_ref[...] += dbs
        dvn = jnp.concatenate(dvn_parts, axis=1)
        dg_ref[...] += jnp.sum(dvn * vh, axis=0, keepdims=True)
        db_ref[...] += jnp.sum(dvn, axis=0, keepdims=True)
        dxh = dvn * ln_g_v
        dv = rstd * (dxh - jnp.mean(dxh, axis=-1, keepdims=True) - vh * jnp.mean(dxh * vh, axis=-1, keepdims=True))
        dz_ref[:, W:] = (dv * _gelu_grad(zv)).astype(dz_ref.dtype)

        @pl.when(pl.program_id(0) == pl.num_programs(0) - 1)
        def _():
            dws_ref[...] = _causal(dws_ref[...])

    return pl.pallas_call(
        body, name="gmlp_bwd", grid=(L // T,),
        in_specs=[pl.BlockSpec((T, W2), lambda n: (n, 0)), pl.BlockSpec((T, W), lambda n: (n, 0)),
                  pl.BlockSpec((1, W), lambda n: (0, 0)), pl.BlockSpec((1, W), lambda n: (0, 0)),
                  pl.BlockSpec(w_s.shape, lambda n: (0, 0, 0)), pl.BlockSpec(bsT.shape, lambda n: (0, 0))],
        out_specs=[pl.BlockSpec((T, W2), lambda n: (n, 0)),
                   pl.BlockSpec(w_s.shape, lambda n: (0, 0, 0)), pl.BlockSpec((T, _LANES), lambda n: (0, 0)),
                   pl.BlockSpec((1, W), lambda n: (0, 0)), pl.BlockSpec((1, W), lambda n: (0, 0))],
        out_shape=[jax.ShapeDtypeStruct((L, W2), _MXU), jax.ShapeDtypeStruct(w_s.shape, F32),
                   jax.ShapeDtypeStruct((T, _LANES), F32),
                   jax.ShapeDtypeStruct((1, W), F32), jax.ShapeDtypeStruct((1, W), F32)],
        compiler_params=_cparams(("arbitrary",)),
    )(z, dm, ln_g, ln_b, w_s, bsT)


def _s5_prep_math(a_re, a_im, log_dt):
    dt = jnp.exp(log_dt)
    xr, xi = a_re * dt, a_im * dt
    e = jnp.exp(xr)
    lbr, lbi = e * jnp.cos(xi), e * jnp.sin(xi)
    dn = a_re * a_re + a_im * a_im
    nr, ni = lbr - 1.0, lbi
    pr, pi = nr * a_re + ni * a_im, ni * a_re - nr * a_im
    return dt, lbr, lbi, dn, nr, ni, pr, pi


def _vm():
    return pl.BlockSpec(memory_space=pltpu.VMEM)


def _s5_prep(a_re, a_im, log_dt, b_re, b_im):
    def body(ar_ref, ai_ref, ld_ref, br_ref, bi_ref, lbr_ref, lbi_ref, Br_ref, Bi_ref):
        _, lbr, lbi, dn, _, _, pr, pi = _s5_prep_math(ar_ref[...], ai_ref[...], ld_ref[...])
        cr, ci = (pr / dn)[:, None, :], (pi / dn)[:, None, :]
        lbr_ref[...] = lbr
        lbi_ref[...] = lbi
        br, bi = br_ref[...], bi_ref[...]
        Br_ref[...] = cr * br - ci * bi
        Bi_ref[...] = cr * bi + ci * br

    sd = jax.ShapeDtypeStruct
    return pl.pallas_call(
        body, name="s5_prep", in_specs=[_vm()] * 5, out_specs=[_vm()] * 4,
        out_shape=[sd(a_re.shape, F32), sd(a_re.shape, F32), sd(b_re.shape, F32), sd(b_re.shape, F32)],
    )(a_re, a_im, log_dt, b_re, b_im)


def _s5_prep_bwd(a_re, a_im, log_dt, b_re, b_im, dlbr_s, dlbi_s, dBr, dBi):
    def body(ar_ref, ai_ref, ld_ref, br_ref, bi_ref, dlr_ref, dli_ref, dBr_ref, dBi_ref,
             dar_ref, dai_ref, dld_ref, dbr_ref, dbi_ref):
        a_re_v, a_im_v = ar_ref[...], ai_ref[...]
        dt, lbr, lbi, dn, nr, ni, pr, pi = _s5_prep_math(a_re_v, a_im_v, ld_ref[...])
        cr, ci = (pr / dn)[:, None, :], (pi / dn)[:, None, :]
        br, bi, dBr_v, dBi_v = br_ref[...], bi_ref[...], dBr_ref[...], dBi_ref[...]
        dbr_ref[...] = cr * dBr_v + ci * dBi_v
        dbi_ref[...] = cr * dBi_v - ci * dBr_v
        dcr = jnp.sum(br * dBr_v + bi * dBi_v, axis=1)
        dci = jnp.sum(br * dBi_v - bi * dBr_v, axis=1)
        dpr, dpi = dcr / dn, dci / dn
        ddn = -(dcr * pr + dci * pi) / (dn * dn)
        dnr = dpr * a_re_v - dpi * a_im_v
        dni = dpr * a_im_v + dpi * a_re_v
        dlbr = dlr_ref[...] + dnr
        dlbi = dli_ref[...] + dni
        dxr = dlbr * lbr + dlbi * lbi
        dxi = dlbi * lbr - dlbr * lbi
        dar_ref[...] = dpr * nr + dpi * ni + 2.0 * ddn * a_re_v + dxr * dt
        dai_ref[...] = dpr * ni - dpi * nr + 2.0 * ddn * a_im_v + dxi * dt
        dld_ref[...] = jnp.sum(dxr * a_re_v + dxi * a_im_v, axis=0, keepdims=True) * dt

    sd = jax.ShapeDtypeStruct
    return pl.pallas_call(
        body, name="s5_prep_bwd", in_specs=[_vm()] * 9, out_specs=[_vm()] * 5,
        out_shape=[sd(a_re.shape, F32), sd(a_re.shape, F32), sd(log_dt.shape, F32),
                   sd(b_re.shape, F32), sd(b_re.shape, F32)],
    )(a_re, a_im, log_dt, b_re, b_im, dlbr_s, dlbi_s, dBr, dBi)


def _shift_rows(v, down):
    n = v.shape[0]
    rolled = pltpu.roll(v, 1 if down else n - 1, 0)
    row = lax.broadcasted_iota(jnp.int32, v.shape, 0)
    return jnp.where(row == (0 if down else n - 1), 0.0, rolled)


def _cmul(ar, ai, br, bi):
    return ar * br - ai * bi, ar * bi + ai * br


_SEG = 8
_UNROLL = 8


def _seg_rows(k):
    if isinstance(k, int):
        return pl.ds(k * _SEG, _SEG)
    return pl.ds(pl.multiple_of(k * _SEG, _SEG), _SEG)


def _unrolled(n, step, init):
    main = n // _UNROLL

    def trip(kk, s):
        for uu in range(_UNROLL):
            s = step(kk * _UNROLL + uu, s)
        return s

    s = lax.fori_loop(0, main, trip, init)
    for r in range(main * _UNROLL, n):
        s = step(r, s)
    return s


def _interleave(src_ref, dst_ref, nk):
    def step(k, carry):
        dst_ref[_seg_rows(k), :] = src_ref[pl.ds(k, _SEG, stride=nk), :]
        return carry
    _unrolled(nk, step, 0)


def _deinterleave(src_ref, dst_ref, nk):
    def step(k, carry):
        dst_ref[pl.ds(k, _SEG, stride=nk), :] = src_ref[_seg_rows(k), :]
        return carry
    _unrolled(nk, step, 0)


def _segment_inits(er, ei, ar, ai, nk, down):
    pr, pi = ar, ai
    for _ in range(int(math.log2(nk))):
        pr, pi = _cmul(pr, pi, pr, pi)
    fr, fi = er, ei
    for _ in range(_SEG - 1):
        sr, si = _shift_rows(fr, down), _shift_rows(fi, down)
        mr, mi = _cmul(pr, pi, sr, si)
        fr, fi = er + mr, ei + mi
    return _shift_rows(fr, down), _shift_rows(fi, down)


def _scan_states(x_re, x_im, ar, ai, nk):
    lanes = ar.shape[1]

    def step(k, s):
        rows = _seg_rows(k)
        mr, mi = _cmul(ar, ai, s[0], s[1])
        return mr + x_re[rows, :], mi + x_im[rows, :]

    zero = jnp.zeros((_SEG, lanes), F32)
    er, ei = _unrolled(nk, step, (zero, zero))
    ir, ii = _segment_inits(er, ei, ar, ai, nk, True)

    def step2(k, s):
        rows = _seg_rows(k)
        mr, mi = _cmul(ar, ai, s[0], s[1])
        nr, ni = mr + x_re[rows, :], mi + x_im[rows, :]
        x_re[rows, :] = nr
        x_im[rows, :] = ni
        return nr, ni

    _unrolled(nk, step2, (ir, ii))


def _s5_tile_fwd(u, bd_re, bd_im, cd_re, cd_im, ar, ai, d, s_re, s_im, nk):
    s_re[...] = _dot(u, bd_re, _NN)
    s_im[...] = _dot(u, bd_im, _NN)
    _scan_states(s_re, s_im, ar, ai, nk)
    return _dot(s_re[...], cd_re, _NN) - _dot(s_im[...], cd_im, _NN) + d * u


def _s5_specs(L, T):
    lanes = _S5_GT * _S5_P
    u_spec = pl.BlockSpec((L, _LANES), lambda t: (0, t))
    bd_spec = pl.BlockSpec((None, _S5_GT, _S5_C, _S5_P), lambda t: (t, 0, 0, 0))
    cd_spec = pl.BlockSpec((None, _S5_GT, _S5_P, _S5_C), lambda t: (t, 0, 0, 0))
    lam_spec = pl.BlockSpec((None, 1, lanes), lambda t: (t, 0, 0))
    d_spec = pl.BlockSpec((1, _LANES), lambda t: (0, t))
    return lanes, u_spec, bd_spec, cd_spec, lam_spec, d_spec


def _fill_block_diag(dst_ref, blocks_ref):
    _, a, b = blocks_ref.shape
    dst_ref[...] = jnp.zeros(dst_ref.shape, F32)
    for g in range(_S5_GT):
        dst_ref[g * a:(g + 1) * a, g * b:(g + 1) * b] = blocks_ref[g]


def _take_block_diag(dst_ref, v):
    _, a, b = dst_ref.shape
    for g in range(_S5_GT):
        dst_ref[g] = v[g * a:(g + 1) * a, g * b:(g + 1) * b]


def _s5_dense(bdr, bdi, cdr, cdi, dense):
    for src, dst in zip((bdr, bdi, cdr, cdi), dense):
        _fill_block_diag(dst, src)
    return [dst[...] for dst in dense]


def _s5_dense_scratch(lanes):
    return [pltpu.VMEM((_LANES, lanes), F32), pltpu.VMEM((_LANES, lanes), F32),
            pltpu.VMEM((lanes, _LANES), F32), pltpu.VMEM((lanes, _LANES), F32)]


def _s5_fwd(u, bd_re, bd_im, cd_re, cd_im, lam_re, lam_im, d):
    L, Wd = u.shape
    T = Wd // _LANES
    nk = L // _SEG
    lanes, u_spec, bd_spec, cd_spec, lam_spec, d_spec = _s5_specs(L, T)

    def body(u_ref, bdr, bdi, cdr, cdi, lr, li, d_ref, g_ref, s_re, s_im, up, tmp, *dense):
        ar = jnp.broadcast_to(lr[...], (_SEG, lanes))
        ai = jnp.broadcast_to(li[...], (_SEG, lanes))
        bd_re_v, bd_im_v, cd_re_v, cd_im_v = _s5_dense(bdr, bdi, cdr, cdi, dense)
        _interleave(u_ref, up, nk)
        y = _s5_tile_fwd(up[...], bd_re_v, bd_im_v, cd_re_v, cd_im_v, ar, ai, d_ref[...], s_re, s_im, nk)
        up[...] = _gelu(y)
        _deinterleave(up, tmp, nk)
        g_ref[...] = tmp[...].astype(g_ref.dtype)

    return pl.pallas_call(
        body, name="s5_fwd", grid=(T,),
        in_specs=[u_spec, bd_spec, bd_spec, cd_spec, cd_spec, lam_spec, lam_spec, d_spec],
        out_specs=u_spec, out_shape=jax.ShapeDtypeStruct((L, Wd), _MXU),
        scratch_shapes=[pltpu.VMEM((L, lanes), F32) for _ in range(2)]
        + [pltpu.VMEM((L, _LANES), F32) for _ in range(2)] + _s5_dense_scratch(lanes),
        compiler_params=_cparams(("arbitrary",)),
    )(u, bd_re, bd_im, cd_re, cd_im, lam_re, lam_im, d)


def _s5_bwd(u, dg, bd_re, bd_im, cd_re, cd_im, lam_re, lam_im, d):
    L, Wd = u.shape
    T = Wd // _LANES
    nk = L // _SEG
    lanes, u_spec, bd_spec, cd_spec, lam_spec, d_spec = _s5_specs(L, T)

    def body(u_ref, dg_ref, bdr, bdi, cdr, cdi, lr, li, d_ref,
             du_ref, dbdr, dbdi, dcdr, dcdi, dlr, dli, dd_ref, s_re, s_im, g_re, g_im, up, dgp, tmp, *dense):
        ar = jnp.broadcast_to(lr[...], (_SEG, lanes))
        ai = jnp.broadcast_to(li[...], (_SEG, lanes))
        bd_re_v, bd_im_v, cd_re_v, cd_im_v = _s5_dense(bdr, bdi, cdr, cdi, dense)
        _interleave(u_ref, up, nk)
        _interleave(dg_ref, dgp, nk)
        uv, dv = up[...], d_ref[...]
        y = _s5_tile_fwd(uv, bd_re_v, bd_im_v, cd_re_v, cd_im_v, ar, ai, dv, s_re, s_im, nk)
        dy = dgp[...] * _gelu_grad(y)
        dd_ref[...] = jnp.sum(dy * uv, axis=0, keepdims=True)
        dyb = dy.astype(_MXU)
        _take_block_diag(dcdr, _dot(dyb, s_re[...], _TN))
        _take_block_diag(dcdi, -_dot(dyb, s_im[...], _TN))
        g_re[...] = _dot(dyb, cd_re_v, _NT)
        g_im[...] = -_dot(dyb, cd_im_v, _NT)

        nai = -ai

        def step(j, s):
            rows = _seg_rows(nk - 1 - j)
            mr, mi = _cmul(ar, nai, s[0], s[1])
            return mr + g_re[rows, :], mi + g_im[rows, :]

        zero = jnp.zeros((_SEG, lanes), F32)
        er, ei = _unrolled(nk, step, (zero, zero))
        ir, ii = _segment_inits(er, ei, ar, nai, nk, False)

        def acc_lam(gr, gi, pr, pi, acc):
            return acc[0] + gr * pr + gi * pi, acc[1] + gi * pr - gr * pi

        def step2(j, carry):
            s, acc = carry
            k = nk - 1 - j
            rows = _seg_rows(k)
            mr, mi = _cmul(ar, nai, s[0], s[1])
            nr, ni = mr + g_re[rows, :], mi + g_im[rows, :]
            g_re[rows, :] = nr
            g_im[rows, :] = ni
            prev = _seg_rows(k - 1)
            return (nr, ni), acc_lam(nr, ni, s_re[prev, :], s_im[prev, :], acc)

        (g0r, g0i), acc = _unrolled(nk - 1, step2, ((ir, ii), (zero, zero)))
        first = _seg_rows(0)
        mr, mi = _cmul(ar, nai, g0r, g0i)
        nr, ni = mr + g_re[first, :], mi + g_im[first, :]
        g_re[first, :] = nr
        g_im[first, :] = ni
        last = _seg_rows(nk - 1)
        acc = acc_lam(nr, ni, _shift_rows(s_re[last, :], True), _shift_rows(s_im[last, :], True), acc)
        dlr[...] = jnp.sum(acc[0], axis=0, keepdims=True)
        dli[...] = jnp.sum(acc[1], axis=0, keepdims=True)

        gtr, gti = g_re[...].astype(_MXU), g_im[...].astype(_MXU)
        ub = uv.astype(_MXU)
        _take_block_diag(dbdr, _dot(ub, gtr, _TN))
        _take_block_diag(dbdi, _dot(ub, gti, _TN))
        dgp[...] = _dot(gtr, bd_re_v, _NT) + _dot(gti, bd_im_v, _NT) + dy * dv
        _deinterleave(dgp, tmp, nk)
        du_ref[...] = tmp[...].astype(du_ref.dtype)

    sd = jax.ShapeDtypeStruct
    big = sd((T, _S5_GT, _S5_C, _S5_P), F32)
    return pl.pallas_call(
        body, name="s5_bwd", grid=(T,),
        in_specs=[u_spec, u_spec, bd_spec, bd_spec, cd_spec, cd_spec, lam_spec, lam_spec, d_spec],
        out_specs=[u_spec, bd_spec, bd_spec, bd_spec, bd_spec, lam_spec, lam_spec, d_spec],
        out_shape=[sd((L, Wd), _MXU), big, big, big, big, sd((T, 1, lanes), F32), sd((T, 1, lanes), F32),
                   sd((1, Wd), F32)],
        scratch_shapes=[pltpu.VMEM((L, lanes), F32) for _ in range(4)]
        + [pltpu.VMEM((L, _LANES), F32) for _ in range(3)] + _s5_dense_scratch(lanes),
        compiler_params=_cparams(("arbitrary",)),
    )(u, dg, bd_re, bd_im, cd_re, cd_im, lam_re, lam_im, d)


def _half_tile(R, few_arrays=False):
    for t in ((512, 704, 128) if few_arrays else (256, 352, 128)):
        if R % t == 0:
            return t
    raise ValueError(R)


def _cast_shard(name, w, layer, kind, R, C):
    tr = _half_tile(R, True)
    nr = R // tr

    def body(w_ref, o_ref):
        o_ref[...] = w_ref[...].astype(o_ref.dtype)

    if kind == "col":
        in_map = lambda h, i: (layer, h * nr + i, 0)
    else:
        in_map = lambda h, i: (layer, i, h)
    return pl.pallas_call(
        body, name=name, grid=(2, nr), in_specs=[pl.BlockSpec((None, tr, C), in_map)],
        out_specs=pl.BlockSpec((None, tr, C), lambda h, i: (h, i, 0)),
        out_shape=jax.ShapeDtypeStruct((2, R, C), _WIRE),
        compiler_params=_cparams(("arbitrary", "arbitrary")),
    )(w)


def _adam_math(w, g, m, v):
    m2 = _B1 * m + (1.0 - _B1) * g
    v2 = _B2 * v + (1.0 - _B2) * (g * g)
    m_hat = m2 / (1.0 - _B1 ** _STEP)
    v_hat = v2 / (1.0 - _B2 ** _STEP)
    delta = -_LR * (m_hat / (jnp.sqrt(v_hat) + _AEPS) + _WD * w)
    return delta, m2, v2


def _adamw_big(name, w, m, v, layer, pair, kind, R, C, c, after, prev):
    tr = _half_tile(R, few_arrays=C <= 1024 and R % 512 == 0)
    nr = R // tr

    def body(c_ref, w_ref, m_ref, v_ref, own_ref, other_ref, *rest):
        go_ref, d_ref, mo_ref, vo_ref = rest[-4:]
        g = jnp.where(pl.program_id(0) == c_ref[0], own_ref[...], other_ref[...])
        delta, m2, v2 = _adam_math(w_ref[...], g, m_ref[...], v_ref[...])
        go_ref[...] = g
        d_ref[...] = delta
        mo_ref[...] = m2
        vo_ref[...] = v2

    if kind == "col":
        nat = pl.BlockSpec((None, tr, C), lambda h, i, c_ref: (layer, h * nr + i, 0))
    else:
        nat = pl.BlockSpec((None, tr, C), lambda h, i, c_ref: (layer, i, h))

    def gspec(own):
        return pl.BlockSpec((tr, C), lambda h, i, c_ref: (jnp.where((h == c_ref[0]) == own, i, 0), 0))

    carried = list(prev) if prev is not None else []
    gs = pltpu.PrefetchScalarGridSpec(
        num_scalar_prefetch=1, grid=(2, nr),
        in_specs=[nat, nat, nat, gspec(True), gspec(False), _any()] + [_any()] * len(carried),
        out_specs=[nat] * 4)
    sd = jax.ShapeDtypeStruct(w.shape, F32)
    return pl.pallas_call(
        body, name=name, grid_spec=gs, out_shape=[sd] * 4,
        input_output_aliases={7 + k: k for k in range(len(carried))},
        compiler_params=_cparams(("arbitrary", "arbitrary")),
    )(c.astype(jnp.int32).reshape(1), w, m, v, pair[0], pair[1], after, *carried)


def _adamw_flat(w, g, m, v):
    rows = w.shape[0]
    tr = rows // 8 if rows % 64 == 0 else rows

    def body(w_ref, g_ref, m_ref, v_ref, d_ref, mo_ref, vo_ref):
        delta, m2, v2 = _adam_math(w_ref[...], g_ref[...], m_ref[...], v_ref[...])
        d_ref[...] = delta
        mo_ref[...] = m2
        vo_ref[...] = v2

    spec = pl.BlockSpec((tr, _LANES), lambda i: (i, 0))
    sd = jax.ShapeDtypeStruct(w.shape, F32)
    return pl.pallas_call(
        body, name="adamw_small", grid=(rows // tr,), in_specs=[spec] * 4, out_specs=[spec] * 3,
        out_shape=[sd] * 3, compiler_params=_cparams(("arbitrary",)),
    )(w, g, m, v)


def _place():
    x, y, c = lax.axis_index("x"), lax.axis_index("y"), lax.axis_index("c")
    chips = [(1 - x, y), (x, 1 - y), (1 - x, 1 - y)]
    return x, y, c, 2 * x + y, chips


def _any():
    return pl.BlockSpec(memory_space=pl.ANY)


def _remote(src, dst, ssem, rsem, dev):
    return pltpu.make_async_remote_copy(src_ref=src, dst_ref=dst, send_sem=ssem, recv_sem=rsem,
                                        device_id=dev, device_id_type=_MESH)


def _allgather(name, shards):
    n = len(shards)

    def body(*refs):
        s_refs, g_refs = refs[:n], refs[n:2 * n]
        send0, recv0, send1, recv1, send2, recv2 = refs[2 * n:]
        x, y, c, q, _ = _place()
        sib, xn, yn = (x, y, 1 - c), (1 - x, y, c), (x, 1 - y, c)
        qx, qy, qd = 2 * (1 - x) + y, 2 * x + (1 - y), 2 * (1 - x) + (1 - y)
        _handshake([sib, xn, yn])
        own = [_remote(s_refs[a], g_refs[a].at[q], send0.at[a], recv0.at[a], sib) for a in range(n)]

        def pieces(a):
            g, half = g_refs[a], s_refs[a].shape[1] // 2
            return [g.at[qx, c], g.at[qy, c], g.at[qd, c, pl.ds(0, half)], g.at[qd, c, pl.ds(half, half)]]

        def relayed(a):
            g, half = g_refs[a], s_refs[a].shape[1] // 2
            return [(g.at[qx, c, pl.ds(0, half)], yn), (g.at[qy, c, pl.ds(half, half)], xn)]

        first = []
        for a in range(n):
            first.append(_remote(s_refs[a].at[c], g_refs[a].at[q, c], send1.at[4 * a], recv1.at[4 * a], xn))
            first.append(_remote(s_refs[a].at[c], g_refs[a].at[q, c], send1.at[4 * a + 1], recv1.at[4 * a + 1], yn))
        for cp in first + own:
            cp.start()
        later = []
        for a in range(n):
            land = pieces(a)
            for j in range(4):
                k = 4 * a + j
                _remote(land[j], land[j], send1.at[k], recv1.at[k], xn).wait_recv()
                if j < 2:
                    src, to = relayed(a)[j]
                    cp = _remote(src, src, send1.at[k + 2], recv1.at[k + 2], to)
                    cp.start()
                    later.append(cp)
                cp = _remote(land[j], land[j], send2.at[k], recv2.at[k], sib)
                cp.start()
                later.append(cp)
        for a in range(n):
            g, half = g_refs[a], s_refs[a].shape[1] // 2
            theirs = [g.at[qx, 1 - c], g.at[qy, 1 - c], g.at[qd, 1 - c, pl.ds(0, half)],
                      g.at[qd, 1 - c, pl.ds(half, half)]]
            for j in range(4):
                _remote(theirs[j], theirs[j], send2.at[4 * a + j], recv2.at[4 * a + j], sib).wait_recv()
        for cp in own:
            cp.wait()
        for cp in first + later:
            cp.wait_send()

    return _sequencer(name, _ID_GATHER, body, shards,
                      [jax.ShapeDtypeStruct((4,) + s.shape, s.dtype) for s in shards], [n, n] + [4 * n] * 4)


def _handshake(peers):
    barrier = pltpu.get_barrier_semaphore()
    for peer in peers:
        pl.semaphore_signal(barrier, inc=1, device_id=peer, device_id_type=_MESH)
    pl.semaphore_wait(barrier, len(peers))


_ID_SIBLING, _ID_CHIPS, _ID_GATHER, _ID_ALL = 1, 2, 3, 4


def _sequencer(name, collective_id, body, ins, out_types, sem_counts):
    mesh = plsc.ScalarSubcoreMesh(axis_name="seq", num_cores=1)
    return pl.kernel(
        body, name=name, out_type=out_types, mesh=mesh,
        scratch_types=[pltpu.SemaphoreType.DMA((k,)) for k in sem_counts],
        compiler_params=pltpu.CompilerParams(collective_id=collective_id),
    )(*ins)


def _swap_halves(name, grads):
    n = len(grads)

    def body(*refs):
        g_refs, t_refs = refs[:n], refs[n:2 * n]
        send, recv = refs[2 * n:]
        x, y, c, _, _ = _place()
        _handshake([(x, y, 1 - c)])
        cps = [_remote(g_refs[a].at[1 - c], t_refs[a], send.at[a], recv.at[a], (x, y, 1 - c)) for a in range(n)]
        for cp in cps:
            cp.start()
        for cp in cps:
            cp.wait()

    return _sequencer(name, _ID_SIBLING, body, grads,
                      [jax.ShapeDtypeStruct(g.shape[1:], g.dtype) for g in grads], [n, n])


def _chip_sum(name, g, t, after):
    _, _, R, C = g.shape
    tr = _half_tile(R, True)

    def body(c_ref, g_ref, t_ref, after_ref, o_ref):
        o_ref[...] = (g_ref[...].astype(F32) + t_ref[...].astype(F32)).astype(o_ref.dtype)

    c = lax.axis_index("c").astype(jnp.int32).reshape(1)
    gs = pltpu.PrefetchScalarGridSpec(
        num_scalar_prefetch=1, grid=(4, R // tr),
        in_specs=[pl.BlockSpec((None, None, tr, C), lambda r, i, c_ref: (c_ref[0], r, i, 0)),
                  pl.BlockSpec((None, tr, C), lambda r, i, c_ref: (r, i, 0)), _any()],
        out_specs=pl.BlockSpec((None, tr, C), lambda r, i, c_ref: (r, i, 0)))
    return pl.pallas_call(
        body, name=name, grid_spec=gs, out_shape=jax.ShapeDtypeStruct((4, R, C), _WIRE),
        compiler_params=_cparams(("arbitrary", "arbitrary")),
    )(c, g, t, after)


def _scatter_parts(name, parts):
    n = len(parts)

    def body(*refs):
        p_refs, t_refs = refs[:n], refs[n:2 * n]
        send, recv = refs[2 * n:]
        x, y, c, q, chips = _place()
        _handshake([(rx, ry, c) for rx, ry in chips])
        cps = []
        for a in range(n):
            for j, (rx, ry) in enumerate(chips):
                k = 3 * a + j
                cps.append(_remote(p_refs[a].at[2 * rx + ry], t_refs[a].at[q], send.at[k], recv.at[k], (rx, ry, c)))
        for cp in cps:
            cp.start()
        for a in range(n):
            for j, (rx, ry) in enumerate(chips):
                k = 3 * a + j
                land = t_refs[a].at[2 * rx + ry]
                _remote(land, land, send.at[k], recv.at[k], (rx, ry, c)).wait_recv()
        for cp in cps:
            cp.wait_send()

    return _sequencer(name, _ID_CHIPS, body, parts,
                      [jax.ShapeDtypeStruct(p.shape, p.dtype) for p in parts], [3 * n, 3 * n])


def _sum_parts(name, p, t, where, after):
    _, R, C = t.shape
    tr = _half_tile(R, True)

    def body(w_ref, p_ref, t0_ref, t1_ref, t2_ref, after_ref, o_ref):
        o_ref[...] = (p_ref[...].astype(F32) + t0_ref[...].astype(F32)
                      + t1_ref[...].astype(F32) + t2_ref[...].astype(F32))

    def part(slot):
        return pl.BlockSpec((None, tr, C), lambda i, w_ref: (w_ref[slot], i, 0))

    gs = pltpu.PrefetchScalarGridSpec(
        num_scalar_prefetch=1, grid=(R // tr,), in_specs=[part(0), part(1), part(2), part(3), _any()],
        out_specs=pl.BlockSpec((tr, C), lambda i, w_ref: (i, 0)))
    return pl.pallas_call(
        body, name=name, grid_spec=gs, out_shape=jax.ShapeDtypeStruct((R, C), F32),
        compiler_params=_cparams(("arbitrary",)),
    )(where, p, t, t, t, after)


def _send_halves(name, halves):
    n = len(halves)

    def body(*refs):
        h_refs, o_refs = refs[:n], refs[n:2 * n]
        send, recv = refs[2 * n:]
        x, y, c, _, _ = _place()
        _handshake([(x, y, 1 - c)])
        cps = [_remote(h_refs[a], o_refs[a], send.at[a], recv.at[a], (x, y, 1 - c)) for a in range(n)]
        for cp in cps:
            cp.start()
        for cp in cps:
            cp.wait()

    return _sequencer(name, _ID_SIBLING, body, halves,
                      [jax.ShapeDtypeStruct(h.shape, h.dtype) for h in halves], [n, n])


class _Reduction:
    def __init__(self, tag, grads):
        self.tag, self.grads = tag, grads
        self.swapped = _swap_halves(f"rs_swap_{tag}", grads)

    def scatter(self, after):
        self.parts = [_chip_sum(f"rs_chipsum_{self.tag}_{a}", g, t, after)
                      for a, (g, t) in enumerate(zip(self.grads, self.swapped))]
        self.landed = _scatter_parts(f"rs_scatter_{self.tag}", self.parts)

    def finish(self, after):
        x, y, c, q, chips = _place()
        where = jnp.stack([q] + [2 * rx + ry for rx, ry in chips]).astype(jnp.int32)
        halves = [_sum_parts(f"rs_sum_{self.tag}_{a}", p, t, where, after)
                  for a, (p, t) in enumerate(zip(self.parts, self.landed))]
        return list(zip(halves, _send_halves(f"rs_join_{self.tag}", halves)))


def _allreduce_small(name, v):
    _, R, _ = v.shape

    def body(v_ref, o_ref, land, acc, send1, recv1, send2, recv2):
        x, y, c = lax.axis_index("x"), lax.axis_index("y"), lax.axis_index("c")
        me = 4 * x + 2 * y + c
        peers = []
        for k in range(1, 8):
            dx, dy, dc = (k >> 2) & 1, (k >> 1) & 1, k & 1
            px, py, pc = (1 - x if dx else x), (1 - y if dy else y), (1 - c if dc else c)
            peers.append((k, (px, py, pc), 4 * px + 2 * py + pc))
        land[me] = v_ref[me]
        out1 = [_remote(v_ref.at[pid], land.at[me], send1.at[k], recv1.at[k], dev) for k, dev, pid in peers]
        for cp in out1:
            cp.start()
        for k, dev, pid in peers:
            _remote(land.at[pid], land.at[pid], send1.at[k], recv1.at[k], dev).wait_recv()
        total = land[0]
        for j in range(1, 8):
            total = total + land[j]
        acc[...] = total
        o_ref[me] = total
        out2 = [_remote(acc, o_ref.at[me], send2.at[k], recv2.at[k], dev) for k, dev, pid in peers]
        for cp in out2:
            cp.start()
        for k, dev, pid in peers:
            _remote(o_ref.at[pid], o_ref.at[pid], send2.at[k], recv2.at[k], dev).wait_recv()
        for cp in out1 + out2:
            cp.wait_send()

    return pl.pallas_call(
        body, name=name, in_specs=[_vm()], out_specs=_vm(),
        out_shape=jax.ShapeDtypeStruct(v.shape, F32),
        scratch_shapes=[pltpu.VMEM(v.shape, F32), pltpu.VMEM((R, _LANES), F32)]
        + [pltpu.SemaphoreType.DMA((8,)) for _ in range(4)],
        compiler_params=pltpu.CompilerParams(vmem_limit_bytes=_VMEM_LIMIT),
    )(v)


def _all_peers():
    x, y, c = lax.axis_index("x"), lax.axis_index("y"), lax.axis_index("c")
    peers = []
    for k in range(1, 8):
        px, py, pc = (1 - x if k & 4 else x), (1 - y if k & 2 else y), (1 - c if k & 1 else c)
        peers.append((k, (px, py, pc), 4 * px + 2 * py + pc))
    return 4 * x + 2 * y + c, peers


def _exchange_slices(name, v):
    def body(v_ref, land, send, recv):
        me, peers = _all_peers()
        _handshake([dev for _, dev, _ in peers])
        cps = [_remote(v_ref.at[pid], land.at[me], send.at[k], recv.at[k], dev) for k, dev, pid in peers]
        for cp in cps:
            cp.start()
        for k, dev, pid in peers:
            _remote(land.at[pid], land.at[pid], send.at[k], recv.at[k], dev).wait_recv()
        for cp in cps:
            cp.wait_send()

    return _sequencer(name, _ID_ALL, body, [v], [jax.ShapeDtypeStruct(v.shape, v.dtype)], [8, 8])[0]


def _sum_slices(name, v, landed, after):
    _, R, _ = v.shape

    def body(v_ref, land_ref, after_ref, o_ref):
        me, peers = _all_peers()
        acc = v_ref[me]
        for _, _, pid in peers:
            acc = acc + land_ref[pid]
        o_ref[...] = acc

    return pl.pallas_call(
        body, name=name, in_specs=[_vm(), _vm(), _any()], out_specs=_vm(),
        out_shape=jax.ShapeDtypeStruct((R, _LANES), F32),
        compiler_params=pltpu.CompilerParams(vmem_limit_bytes=_VMEM_LIMIT),
    )(v, landed, after)


def _broadcast_slices(name, s):
    def body(s_ref, out, send, recv):
        me, peers = _all_peers()
        _handshake([dev for _, dev, _ in peers])
        cps = [_remote(s_ref, out.at[me], send.at[k], recv.at[k], dev) for k, dev, pid in peers]
        for cp in cps:
            cp.start()
        for k, dev, pid in peers:
            _remote(out.at[pid], out.at[pid], send.at[k], recv.at[k], dev).wait_recv()
        for cp in cps:
            cp.wait_send()

    return _sequencer(name, _ID_ALL, body, [s], [jax.ShapeDtypeStruct((8,) + s.shape, s.dtype)], [8, 8])[0]


_WEIGHT_NAMES = ['norm_mix', 'norm_ffn', 'norm_ple', 'norm_final', 'gm_w_in', 'gm_ln_g', 'gm_ln_b', 'gm_w_s',
                 'gm_b_s', 'gm_w_out', 's5_w_in', 's5_a_re', 's5_a_im', 's5_log_dt', 's5_b_re', 's5_b_im',
                 's5_c_re', 's5_c_im', 's5_d', 's5_w_out', 'ffn_w1', 'ffn_w3', 'ffn_w2', 'ple_w_gate', 'ple_w_proj']
_BIG = {'gm_w_in': 'col', 'gm_w_out': 'row', 's5_w_in': 'row', 's5_w_out': 'col', 'ffn_w1': 'col',
        'ffn_w3': 'col', 'ffn_w2': 'row', 'ple_w_gate': 'row', 'ple_w_proj': 'col'}


_VIEW = {'s5_a_re': (0, 2, 1), 's5_a_im': (0, 2, 1), 's5_b_re': (0, 2, 3, 1), 's5_b_im': (0, 2, 3, 1),
         's5_c_re': (0, 2, 3, 1), 's5_c_im': (0, 2, 3, 1)}


def _to_view(name, a):
    return jnp.transpose(a, _VIEW[name]) if name in _VIEW else a


def _from_view(name, a):
    if name not in _VIEW:
        return a
    perm = _VIEW[name]
    return jnp.transpose(a, [perm.index(i) for i in range(len(perm))])


def _rc(kind, shard_shape):
    rows, cols = shard_shape[-2:]
    return (rows // 2, cols) if kind == "col" else (rows, cols // 2)


def _pack(vecs, rows_multiple):
    flat = jnp.concatenate([a.reshape(-1).astype(F32) for a in vecs])
    unit = rows_multiple * _LANES
    pad = (-flat.shape[0]) % unit
    return jnp.pad(flat, (0, pad)).reshape(-1, _LANES)


def _unpack(buf, shapes):
    flat = buf.reshape(-1)
    out, off = [], 0
    for s in shapes:
        n = math.prod(s)
        out.append(flat[off:off + n].reshape(s))
        off += n
    return out


def _ident(accs, ex):
    return accs


def _add_resid(accs, ex):
    return [accs[0] + ex[0]]


def _swiglu_epi(accs, ex):
    a, b = accs
    return [a, b, a * _sig(a) * b]


def _swiglu_bwd_epi(accs, ex):
    df = accs[0]
    a, b = ex[0].astype(F32), ex[1].astype(F32)
    sa = _sig(a)
    return [df * b * (sa * (1.0 + a * (1.0 - sa))), df * (a * sa)]


def _ple_epi(accs, ex):
    gt = _sig(accs[0])
    return [ex[0] + gt * ex[1], gt]


def _glu_epi(accs, ex):
    val, sg = accs[0], _sig(accs[1])
    return [ex[0] + val * sg, val, sg]


def kernel(x, p, norm_mix, norm_ffn, norm_ple, norm_final, gm_w_in, gm_ln_g, gm_ln_b, gm_w_s, gm_b_s, gm_w_out, s5_w_in, s5_a_re, s5_a_im, s5_log_dt, s5_b_re, s5_b_im, s5_c_re, s5_c_im, s5_d, s5_w_out, ffn_w1, ffn_w3, ffn_w2, ple_w_gate, ple_w_proj, loss_target, m_norm_mix, m_norm_ffn, m_norm_ple, m_norm_final, m_gm_w_in, m_gm_ln_g, m_gm_ln_b, m_gm_w_s, m_gm_b_s, m_gm_w_out, m_s5_w_in, m_s5_a_re, m_s5_a_im, m_s5_log_dt, m_s5_b_re, m_s5_b_im, m_s5_c_re, m_s5_c_im, m_s5_d, m_s5_w_out, m_ffn_w1, m_ffn_w3, m_ffn_w2, m_ple_w_gate, m_ple_w_proj, v_norm_mix, v_norm_ffn, v_norm_ple, v_norm_final, v_gm_w_in, v_gm_ln_g, v_gm_ln_b, v_gm_w_s, v_gm_b_s, v_gm_w_out, v_s5_w_in, v_s5_a_re, v_s5_a_im, v_s5_log_dt, v_s5_b_re, v_s5_b_im, v_s5_c_re, v_s5_c_im, v_s5_d, v_s5_w_out, v_ffn_w1, v_ffn_w3, v_ffn_w2, v_ple_w_gate, v_ple_w_proj):
    env = dict(locals())
    w = {n: env[n] for n in _WEIGHT_NAMES}
    mom = {n: env["m_" + n] for n in _WEIGHT_NAMES}
    var = {n: env["v_" + n] for n in _WEIGHT_NAMES}
    xs, tgt = x[0], loss_target[0]
    L, D = xs.shape
    depth = norm_mix.shape[0]
    qx, qy = lax.axis_index("x"), lax.axis_index("y")
    q = 2 * qx + qy

    def gather(tag, items):
        shards = []
        for name, layer in items:
            kind = _BIG[name]
            R, C = _rc(kind, w[name].shape)
            shards.append(_cast_shard(f"cast_{name}{layer}", w[name], layer, kind, R, C))
        full = _allgather(f"ag_{tag}", shards)
        return {it: _W(f, _BIG[it[0]]) for it, f in zip(items, full)}

    W = {}
    mixers = [["gm_w_in", "gm_w_out"], ["s5_w_in", "s5_w_out"]]
    for i in range(depth):
        for n in mixers[i]:
            W.update(gather(f"{n}", [(n, 0)]))
        W.update(gather(f"ffn_up{i}", [("ffn_w1", i), ("ffn_w3", i)]))
        W.update(gather(f"ffn_down{i}", [("ffn_w2", i)]))
        W.update(gather(f"ple{i}", [("ple_w_gate", i), ("ple_w_proj", i)]))

    d_slots = jnp.zeros((4, D // 4), F32)
    d_slots = lax.dynamic_update_slice(d_slots, s5_d.astype(F32), (q, 0))
    d_sum = _allreduce_small("ar_s5_d", _pack([d_slots], 64).reshape(8, -1, _LANES))
    d_full = (d_sum.reshape(-1)[:D] * 0.5).reshape(1, D)

    def ffn_fwd(i, xin):
        hf = _rms_fwd(f"rms_ffn{i}", xin, norm_ffn[i:i + 1])
        a, b, f = _mm_nn(f"ffn_up{i}", hf, [W["ffn_w1", i], W["ffn_w3", i]], 1024, 1408, ffn_w2.shape[1] * 4,
                         [], [_MXU, _MXU, _MXU], _swiglu_epi, tm=1024)
        xo = _mm_nn(f"ffn_down{i}", f, [W["ffn_w2", i]], 1408, 1024, D, [xin], [F32], _add_resid, tm=1024)[0]
        return xo, (xin, hf, a, b, f)

    def ple_fwd(i, xin):
        hp = _rms_fwd(f"rms_ple{i}", xin, norm_ple[i:i + 1])
        pi = p[i, 0]
        pp = _mm_nn(f"ple_proj{i}", pi, [W["ple_w_proj", i]], 128, 512, D, [], [F32], _ident)[0]
        xo, gt = _mm_nn(f"ple_gate{i}", hp, [W["ple_w_gate", i]], 512, 1024, D, [xin, pp], [F32, _MXU], _ple_epi,
                        tm=1024)
        return xo, (xin, hp, pi, pp, gt)

    h0 = _rms_fwd("rms_mix0", xs, norm_mix[0:1])
    z = _mm_nn("gm_in", h0, [W["gm_w_in", 0]], 1024, 1024, 2 * D, [], [F32], _ident)[0]
    bsT = gm_b_s[0].T
    gm_m = _gmlp_fwd(z, gm_ln_g, gm_ln_b, gm_w_s[0], bsT)
    x1 = _mm_nn("gm_out", gm_m, [W["gm_w_out", 0]], 512, 1024, D, [xs], [F32], _add_resid, tm=1024)[0]
    x2, ffn0 = ffn_fwd(0, x1)
    x3, ple0 = ple_fwd(0, x2)

    T = D // _LANES
    lanes = _S5_GT * _S5_P
    sv = {n: _to_view(n, w[n])[0] for n in _VIEW}
    a_re, a_im, log_dt = sv["s5_a_re"], sv["s5_a_im"], s5_log_dt
    lbr, lbi, Bbar_re, Bbar_im = _s5_prep(a_re, a_im, log_dt, sv["s5_b_re"], sv["s5_b_im"])

    def to_bd(B):
        return jnp.transpose(B.reshape(_S5_P, _S5_C, T, _S5_GT), (2, 3, 1, 0))

    def to_cd(cw):
        return jnp.transpose(cw.reshape(_S5_C, _S5_P, T, _S5_GT), (2, 3, 1, 0))

    def to_lam(v):
        return jnp.transpose(v).reshape(T, 1, lanes)

    bd_re, bd_im = to_bd(Bbar_re), to_bd(Bbar_im)
    cd_re, cd_im = to_cd(sv["s5_c_re"]), to_cd(sv["s5_c_im"])
    lam_re, lam_im = to_lam(lbr), to_lam(lbi)

    h1 = _rms_fwd("rms_mix1", x3, norm_mix[1:2])
    u = _mm_nn("s5_in", h1, [W["s5_w_in", 0]], 512, 1024, D, [], [F32], _ident, tm=1024)[0]
    s5_g = _s5_fwd(u, bd_re, bd_im, cd_re, cd_im, lam_re, lam_im, d_full)
    x4, glu_val, glu_sg = _mm_nn("s5_out", s5_g, [W["s5_w_out", 0], W["s5_w_out", 0]], 1024, 1024, D, [x3],
                                 [F32, _MXU, _MXU], _glu_epi, cb_offsets=[0, 2])
    x5, ffn1 = ffn_fwd(1, x4)
    x6, ple1 = ple_fwd(1, x5)

    dx, d_norm_final, loss_rows = _loss_head(x6, norm_final[None], tgt)
    loss = lax.psum(loss_rows[0, 0], ("x", "y", "c"))

    small = {}

    def ple_bwd(i, dxo, saved):
        xin, hp, pi, pp, gt = saved
        dpre, dpp = _ple_bwd_elem(dxo, pp, gt)
        dwg = _mm_tn(f"ple_gate_dw{i}", hp, [dpre], "row", 512, 1024, 512, 1024)[0]
        dwp = _mm_tn(f"ple_proj_dw{i}", pi, [dpp], "col", 128, 512, 128, 512)[0]
        red = _Reduction(f"ple{i}", [dwg, dwp])
        dhp = _mm_nt(f"ple_gate_dx{i}", [dpre], [W["ple_w_gate", i]], 512, 1024, [], [F32], _ident, tm=2048)[0]
        dxin, dg = _rms_bwd(f"rms_ple_bwd{i}", dhp, xin, norm_ple[i:i + 1], dxo)
        return dxin, dg, red, dpre

    def ffn_bwd(i, dxo, saved):
        xin, hf, a, b, f = saved
        da, db = _mm_nt(f"ffn_down_dx{i}", [dxo], [W["ffn_w2", i]], 1408, 1024, [a, b], [_MXU, _MXU],
                        _swiglu_bwd_epi)
        dw1, dw3 = _mm_tn(f"ffn_up_dw{i}", hf, [da, db], "col", 1024, 1408, 1024, 1408)
        dw2 = _mm_tn(f"ffn_down_dw{i}", f, [dxo], "row", 1408, 1024, 1408, 1024)[0]
        red = _Reduction(f"ffn{i}", [dw1, dw3, dw2])
        dhf = _mm_nt(f"ffn_up_dx{i}", [da, db], [W["ffn_w1", i], W["ffn_w3", i]], 1024, 1408, [], [F32], _ident,
                     tm=1024)[0]
        dxin, dg = _rms_bwd(f"rms_ffn_bwd{i}", dhf, xin, norm_ffn[i:i + 1], dxo)
        return dxin, dg, red, da, dhf

    d_norm_ple, d_norm_ffn, d_norm_mix = [None] * depth, [None] * depth, [None] * depth
    reduced = {}

    def keep(names, layer, pairs):
        for n, pr in zip(names, pairs):
            reduced[n, layer] = pr

    ple_names, ffn_names = ["ple_w_gate", "ple_w_proj"], ["ffn_w1", "ffn_w3", "ffn_w2"]

    dx, d_norm_ple[1], r_ple1, _ = ple_bwd(1, dx, ple1)
    dx, d_norm_ffn[1], r_ffn1, da1, _ = ffn_bwd(1, dx, ffn1)
    r_ple1.scatter(da1)

    do = _glu_bwd_elem(dx, glu_val, glu_sg)
    r_ffn1.scatter(do)
    dw_s5out = _mm_tn("s5_out_dw", s5_g, [do], "col", 1024, 1024, 1024, 1024)[0]
    dgy = _mm_nt("s5_out_dx", [do], [W["s5_w_out", 0]], 1024, 1024, [], [F32], _ident, tm=1024)[0]
    keep(ple_names, 1, r_ple1.finish(dgy))
    du, dbd_re, dbd_im, dcd_re, dcd_im, dl_re, dl_im, dd = _s5_bwd(u, dgy, bd_re, bd_im, cd_re, cd_im,
                                                                  lam_re, lam_im, d_full)
    dw_s5in = _mm_tn("s5_in_dw", h1, [du], "row", 512, 1024, 512, 1024)[0]
    r_mix1 = _Reduction("mix1", [dw_s5in, dw_s5out])
    dh1 = _mm_nt("s5_in_dx", [du], [W["s5_w_in", 0]], 512, 1024, [], [F32], _ident, tm=2048)[0]
    dx, d_norm_mix[1] = _rms_bwd("rms_mix1_bwd", dh1, x3, norm_mix[1:2], dx)

    def from_bd(t):
        return jnp.transpose(t, (3, 2, 0, 1)).reshape(_S5_P, _S5_C, T * _S5_GT)

    def from_cdT(t):
        return jnp.transpose(t, (2, 3, 0, 1)).reshape(_S5_C, _S5_P, T * _S5_GT)

    def from_lam(t):
        return jnp.transpose(t.reshape(T * _S5_GT, _S5_P))

    da_re, da_im, dlog_dt, db_re, db_im = _s5_prep_bwd(
        a_re, a_im, log_dt, sv["s5_b_re"], sv["s5_b_im"], from_lam(dl_re), from_lam(dl_im),
        from_bd(dbd_re), from_bd(dbd_im))
    small["s5_a_re"], small["s5_a_im"], small["s5_log_dt"] = da_re[None], da_im[None], dlog_dt
    small["s5_b_re"], small["s5_b_im"] = db_re[None], db_im[None]
    small["s5_c_re"], small["s5_c_im"] = from_cdT(dcd_re)[None], from_cdT(dcd_im)[None]

    dx, d_norm_ple[0], r_ple0, dpre0 = ple_bwd(0, dx, ple0)
    r_mix1.scatter(dpre0)
    keep(ffn_names, 1, r_ffn1.finish(dx))
    dx, d_norm_ffn[0], r_ffn0, da0, dhf0 = ffn_bwd(0, dx, ffn0)
    r_ple0.scatter(da0)
    keep(["s5_w_in", "s5_w_out"], 0, r_mix1.finish(dhf0))
    keep(ple_names, 0, r_ple0.finish(dhf0))

    dw_gmout = _mm_tn("gm_out_dw", gm_m, [dx], "row", 512, 1024, 512, 1024)[0]
    dgm = _mm_nt("gm_out_dx", [dx], [W["gm_w_out", 0]], 512, 1024, [], [F32], _ident, tm=2048)[0]
    r_ffn0.scatter(dgm)
    dz, dws, dbsT, dlng, dlnb = _gmlp_bwd(z, dgm, gm_ln_g, gm_ln_b, gm_w_s[0], bsT)
    dw_gmin = _mm_tn("gm_in_dw", h0, [dz], "col", 1024, 1024, 1024, 1024)[0]
    r_mix0 = _Reduction("mix0", [dw_gmin, dw_gmout])
    dh0 = _mm_nt("gm_in_dx", [dz], [W["gm_w_in", 0]], 1024, 1024, [], [F32], _ident, tm=1024)[0]
    dx, d_norm_mix[0] = _rms_bwd("rms_mix0_bwd", dh0, xs, norm_mix[0:1], dx)
    r_mix0.scatter(dx)
    grad_x = dx[None]

    small["norm_mix"], small["norm_ffn"] = jnp.concatenate(d_norm_mix), jnp.concatenate(d_norm_ffn)
    small["norm_ple"], small["norm_final"] = jnp.concatenate(d_norm_ple), d_norm_final[0]
    small["gm_ln_g"], small["gm_ln_b"], small["gm_w_s"] = dlng, dlnb, dws[None]
    small["gm_b_s"] = dbsT[:, :_GM_HEADS].T[None]
    small["s5_d"] = dd

    small_names = [n for n in _WEIGHT_NAMES if n not in _BIG]
    packed = _pack([small[n] for n in small_names], 64).reshape(8, -1, _LANES)
    landed = _exchange_slices("ar_small_in", packed)
    grads, deltas, new_m, new_v = {}, {}, {}, {}
    my_c = lax.axis_index("c")
    token = dx

    def adamw(n, layer, prev):
        nonlocal token
        kind = _BIG[n]
        R, C = _rc(kind, w[n].shape)
        outs = _adamw_big(f"adamw_{n}{layer}", w[n], mom[n], var[n], layer, reduced[n, layer], kind, R, C,
                          my_c, token, prev)
        token = outs[1]
        return outs

    late = {}
    for n in ffn_names:
        late[n] = adamw(n, 1, None)
    mine = _sum_slices("ar_small_sum", packed, landed, token)
    spread = _broadcast_slices("ar_small_out", mine)
    for n in ple_names:
        late[n] = adamw(n, 1, None)
    for n in ["s5_w_in", "s5_w_out"]:
        grads[n], deltas[n], new_m[n], new_v[n] = adamw(n, 0, None)
    for n in ple_names:
        grads[n], deltas[n], new_m[n], new_v[n] = adamw(n, 0, late[n])
    summed = lax.dynamic_update_slice(spread, mine[None], (4 * qx + 2 * qy + my_c, 0, 0))
    red_small = dict(zip(small_names, _unpack(summed, [small[n].shape for n in small_names])))
    red_small["s5_d"] = lax.dynamic_slice(red_small["s5_d"], (0, q * (D // 4)), (1, D // 4))
    keep(ffn_names, 0, r_ffn0.finish(token))
    for n in ffn_names:
        grads[n], deltas[n], new_m[n], new_v[n] = adamw(n, 0, late[n])
    keep(["gm_w_in", "gm_w_out"], 0, r_mix0.finish(token))
    for n in ["gm_w_in", "gm_w_out"]:
        grads[n], deltas[n], new_m[n], new_v[n] = adamw(n, 0, None)
    def views(src):
        return [_to_view(n, src[n]) for n in small_names]

    shapes = [v_.shape for v_ in views(w)]
    dl, mo, vo = _adamw_flat(_pack(views(w), 64), _pack([red_small[n] for n in small_names], 64),
                             _pack(views(mom), 64), _pack(views(var), 64))
    for n, g_, d_, m_, v_ in zip(small_names, [red_small[n] for n in small_names], _unpack(dl, shapes),
                                 _unpack(mo, shapes), _unpack(vo, shapes)):
        grads[n], deltas[n], new_m[n], new_v[n] = [_from_view(n, t_).reshape(w[n].shape) for t_ in (g_, d_, m_, v_)]

    return (loss, grad_x, *[grads[n] for n in _WEIGHT_NAMES], *[deltas[n] for n in _WEIGHT_NAMES],
            *[new_m[n] for n in _WEIGHT_NAMES], *[new_v[n] for n in _WEIGHT_NAMES])
```

```python
import functools
import math

import jax
import jax.numpy as jnp
from jax import lax
from jax.experimental import pallas as pl
from jax.experimental.pallas import tpu as pltpu
from jax.experimental.pallas import tpu_sc as plsc

F32 = jnp.float32
_MXU = jnp.bfloat16
_WIRE = jnp.bfloat16
_EPS = 1e-6
_VMEM_LIMIT = 56 * 1024 * 1024
_LANES = 128
_MESH = pl.DeviceIdType.MESH

_LR, _B1, _B2, _AEPS, _WD, _STEP = 0.001, 0.9, 0.999, 1e-08, 0.01, 10

_GM_CHUNK = 128
_GM_HEADS = 16
_S5_GT = 8
_S5_P = 64
_S5_C = 16

_NN = (((1,), (0,)), ((), ()))
_NT = (((1,), (1,)), ((), ()))
_TN = (((0,), (0,)), ((), ()))


def _cparams(sem):
    return pltpu.CompilerParams(dimension_semantics=sem, vmem_limit_bytes=_VMEM_LIMIT)


def _sig(x):
    return 1.0 / (1.0 + jnp.exp(-x))


_GC = math.sqrt(2.0 / math.pi)


def _gelu(x):
    return 0.5 * x * (1.0 + jnp.tanh(_GC * (x + 0.044715 * (x * x * x))))


def _gelu_grad(x):
    t = jnp.tanh(_GC * (x + 0.044715 * (x * x * x)))
    return 0.5 * (1.0 + t) + 0.5 * x * (1.0 - t * t) * (_GC * (1.0 + 3.0 * 0.044715 * x * x))


def _dot(a, b, dn):
    return lax.dot_general(a.astype(_MXU), b.astype(_MXU), dn, preferred_element_type=F32)


class _W:
    def __init__(self, arr, kind):
        self.a, self.kind = arr, kind
        self.R, self.C = arr.shape[2], arr.shape[3]

    def full_shape(self):
        return (2 * self.R, 4 * self.C) if self.kind == "col" else (4 * self.R, 2 * self.C)


def _part_index(kind, R, C, tr, tc, rb, cb):
    nr, nc = R // tr, C // tc
    if kind == "col":
        return cb // nc, rb // nr, rb % nr, cb % nc
    return rb // nr, cb // nc, rb % nr, cb % nc


def _wspec(w, tr, tc, rb_fn, cb_fn):
    assert w.R % tr == 0 and w.C % tc == 0, (w.R, w.C, tr, tc)

    def imap(i, j, k):
        return _part_index(w.kind, w.R, w.C, tr, tc, rb_fn(i, j, k), cb_fn(i, j, k))

    return pl.BlockSpec((None, None, tr, tc), imap)


def _gspec(kind, R, C, tr, tc):
    assert R % tr == 0 and C % tc == 0, (R, C, tr, tc)

    def imap(i, j, k):
        part, half, rbi, cbi = _part_index(kind, R, C, tr, tc, i, j)
        return half, part, rbi, cbi

    return pl.BlockSpec((None, None, tr, tc), imap)


def _mm(name, grid, a_ops, b_ops, pairs, acc_shape, n_acc, extras, outs, epilogue):
    nk = grid[2]
    na, nb, ne, no = len(a_ops), len(b_ops), len(extras), len(outs)

    def body(*refs):
        a_refs = refs[:na]
        b_refs = refs[na:na + nb]
        e_refs = refs[na + nb:na + nb + ne]
        o_refs = refs[na + nb + ne:na + nb + ne + no]
        acc_refs = refs[na + nb + ne + no:]
        k = pl.program_id(2)

        def products():
            sums = [None] * n_acc
            for ai, bi, ci, dn in pairs:
                d = _dot(a_refs[ai][...], b_refs[bi][...], dn)
                sums[ci] = d if sums[ci] is None else sums[ci] + d
            return sums

        def finish(accs):
            res = epilogue(accs, [e[...] for e in e_refs])
            for o, r in zip(o_refs, res):
                o[...] = r.astype(o.dtype)

        if nk == 1:
            finish(products())
            return

        @pl.when(k == 0)
        def _():
            for acc, d in zip(acc_refs, products()):
                acc[...] = d

        @pl.when(jnp.logical_and(k > 0, k < nk - 1))
        def _():
            for acc, d in zip(acc_refs, products()):
                acc[...] += d

        @pl.when(k == nk - 1)
        def _():
            finish([acc[...] + d for acc, d in zip(acc_refs, products())])

    ops = list(a_ops) + list(b_ops) + list(extras)
    return pl.pallas_call(
        body, name=name, grid=grid,
        in_specs=[s for _, s in ops],
        out_specs=[s for _, s in outs],
        out_shape=[s for s, _ in outs],
        scratch_shapes=[pltpu.VMEM(acc_shape, F32) for _ in range(n_acc if nk > 1 else 0)],
        compiler_params=_cparams(("parallel", "parallel", "arbitrary")),
    )(*[a for a, _ in ops])


def _bs(shape, fn):
    return pl.BlockSpec(shape, fn)


def _tile_m(L):
    return min(L, 512)


def _mm_nn(name, x, ws, tk, tn, n_out, extras, outs_sd, epilogue, tm=None, cb_offsets=None):
    M, K = x.shape
    tm = min(M, tm or _tile_m(M))
    grid = (M // tm, n_out // tn, K // tk)
    a_ops = [(x, _bs((tm, tk), lambda i, j, k: (i, k)))]
    cb_offsets = cb_offsets or [0] * len(ws)
    b_ops = [(w.a, _wspec(w, tk, tn, lambda i, j, k: k, (lambda off: lambda i, j, k: j + off)(off)))
             for w, off in zip(ws, cb_offsets)]
    pairs = [(0, bi, bi, _NN) for bi in range(len(ws))]
    mn = _bs((tm, tn), lambda i, j, k: (i, j))
    ex = [(e, mn) for e in extras]
    outs = [(jax.ShapeDtypeStruct((M, n_out), dt), mn) for dt in outs_sd]
    return _mm(name, grid, a_ops, b_ops, pairs, (tm, tn), len(ws), ex, outs, epilogue)


def _mm_nt(name, xs, ws, tn, tk, extras, outs_sd, epilogue, tm=None):
    M, Nw = xs[0].shape
    Kw = ws[0].full_shape()[0]
    tm = min(M, tm or _tile_m(M))
    grid = (M // tm, Kw // tn, Nw // tk)
    a_ops = [(x, _bs((tm, tk), lambda i, j, k: (i, k))) for x in xs]
    b_ops = [(w.a, _wspec(w, tn, tk, lambda i, j, k: j, lambda i, j, k: k)) for w in ws]
    pairs = [(i, i, 0, _NT) for i in range(len(ws))]
    mn = _bs((tm, tn), lambda i, j, k: (i, j))
    ex = [(e, mn) for e in extras]
    outs = [(jax.ShapeDtypeStruct((M, Kw), dt), mn) for dt in outs_sd]
    return _mm(name, grid, a_ops, b_ops, pairs, (tm, tn), 1, ex, outs, epilogue)


def _mm_tn(name, x, dys, kind, R, C, tm, tn, tk=None):
    L, Kw = x.shape
    Nw = dys[0].shape[1]
    tk = tk or min(L, 1024)
    grid = (Kw // tm, Nw // tn, L // tk)
    a_ops = [(x, _bs((tk, tm), lambda i, j, k: (k, i)))]
    b_ops = [(dy, _bs((tk, tn), lambda i, j, k: (k, j))) for dy in dys]
    pairs = [(0, bi, bi, _TN) for bi in range(len(dys))]
    gs = _gspec(kind, R, C, tm, tn)
    outs = [(jax.ShapeDtypeStruct((2, 4, R, C), _WIRE), gs) for _ in dys]
    return _mm(name, grid, a_ops, b_ops, pairs, (tm, tn), len(dys), [], outs, lambda accs, ex: accs)


def _row_tile(L):
    return min(L, 256)


def _rowwise(name, body, ins, outs, L, acc_outs=()):
    tr = _row_tile(L)
    n_in, n_out = len(ins), len(outs)

    def kbody(*refs):
        i_refs, o_refs, a_refs = refs[:n_in], refs[n_in:n_in + n_out], refs[n_in + n_out:]
        res, sums = body(*[r[...] for r in i_refs])
        for o, r in zip(o_refs, res):
            o[...] = r.astype(o.dtype)
        if a_refs:
            @pl.when(pl.program_id(0) == 0)
            def _():
                for a in a_refs:
                    a[...] = jnp.zeros(a.shape, F32)
            for a, s in zip(a_refs, sums):
                a[...] += s

    in_specs = []
    for arr, kind in ins:
        if kind == "row":
            in_specs.append(pl.BlockSpec((tr, arr.shape[1]), lambda i: (i, 0)))
        else:
            in_specs.append(pl.BlockSpec(arr.shape, lambda i: (0, 0)))
    out_specs = [pl.BlockSpec((tr, c), lambda i: (i, 0)) for c, _ in outs]
    out_shape = [jax.ShapeDtypeStruct((L, c), dt) for c, dt in outs]
    out_specs += [pl.BlockSpec((1, c), lambda i: (0, 0)) for c in acc_outs]
    out_shape += [jax.ShapeDtypeStruct((1, c), F32) for c in acc_outs]
    return pl.pallas_call(
        kbody, name=name, grid=(L // tr,), in_specs=in_specs, out_specs=out_specs, out_shape=out_shape,
        compiler_params=_cparams(("arbitrary",)),
    )(*[a for a, _ in ins])


def _rms_fwd(name, x, g):
    def body(xv, gv):
        r = lax.rsqrt(jnp.mean(xv * xv, axis=-1, keepdims=True) + _EPS)
        return [xv * r * gv], []
    return _rowwise(name, body, [(x, "row"), (g, "vec")], [(x.shape[1], _MXU)], x.shape[0])[0]


def _rms_bwd(name, dh, x, g, dres):
    def body(dhv, xv, gv, dr):
        r = lax.rsqrt(jnp.mean(xv * xv, axis=-1, keepdims=True) + _EPS)
        xh = xv * r
        dxh = dhv * gv
        dx = dr + r * (dxh - xh * jnp.mean(dxh * xh, axis=-1, keepdims=True))
        return [dx, dx], [jnp.sum(dhv * xh, axis=0, keepdims=True)]
    D = x.shape[1]
    return _rowwise(name, body, [(dh, "row"), (x, "row"), (g, "vec"), (dres, "row")], [(D, F32), (D, _MXU)],
                    x.shape[0], [D])


def _loss_head(x, g, target):
    D = x.shape[1]

    def body(xv, gv, tv):
        r = lax.rsqrt(jnp.mean(xv * xv, axis=-1, keepdims=True) + _EPS)
        xh = xv * r
        e = xh * gv - tv
        dy = e * (1.0 / D)
        dxh = dy * gv
        dx = r * (dxh - xh * jnp.mean(dxh * xh, axis=-1, keepdims=True))
        row_loss = 0.5 * jnp.mean(e * e, axis=-1, keepdims=True)
        lsum = jnp.sum(row_loss, axis=0, keepdims=True) + jnp.zeros((1, _LANES), F32)
        return [dx], [jnp.sum(dy * xh, axis=0, keepdims=True), lsum]
    return _rowwise("loss_head", body, [(x, "row"), (g, "vec"), (target, "row")], [(D, F32)], x.shape[0], [D, _LANES])


def _ple_bwd_elem(dx, pp, gt):
    def body(dxv, ppv, gtv):
        gt32 = gtv.astype(F32)
        return [dxv * ppv * gt32 * (1.0 - gt32), dxv * gt32], []
    D = dx.shape[1]
    return _rowwise("ple_bwd_elem", body, [(dx, "row"), (pp, "row"), (gt, "row")], [(D, _MXU), (D, _MXU)], dx.shape[0])


def _glu_bwd_elem(dx, val, sg):
    def body(dxv, vv, sv):
        v32, s32 = vv.astype(F32), sv.astype(F32)
        return [jnp.concatenate([dxv * s32, dxv * v32 * s32 * (1.0 - s32)], axis=1)], []
    D = dx.shape[1]
    return _rowwise("glu_bwd_elem", body, [(dx, "row"), (val, "row"), (sg, "row")], [(2 * D, _MXU)], dx.shape[0])[0]


def _gm_common(z, ln_g, ln_b, wc_bf, bsT):
    W = z.shape[1] // 2
    zu, zv = z[:, :W], z[:, W:]
    u, v = _gelu(zu), _gelu(zv)
    mu = jnp.mean(v, axis=-1, keepdims=True)
    vc = v - mu
    rstd = lax.rsqrt(jnp.mean(vc * vc, axis=-1, keepdims=True) + _EPS)
    vh = vc * rstd
    vn = vh * ln_g + ln_b
    vnb = vn.astype(_MXU)
    svs = []
    for h in range(_GM_HEADS):
        sl = slice(h * _LANES, (h + 1) * _LANES)
        svs.append(_dot(wc_bf[h], vnb[:, sl], _NN) + bsT[:, h:h + 1])
    return zu, zv, u, vh, rstd, vnb, svs


def _causal(w):
    t = lax.broadcasted_iota(jnp.int32, w.shape, w.ndim - 2)
    s = lax.broadcasted_iota(jnp.int32, w.shape, w.ndim - 1)
    return jnp.where(s <= t, w, jnp.zeros_like(w))


def _gmlp_fwd(z, ln_g, ln_b, w_s, bsT):
    L, W2 = z.shape
    W = W2 // 2

    def body(z_ref, g_ref, b_ref, ws_ref, bs_ref, m_ref):
        wc = _causal(ws_ref[...]).astype(_MXU)
        _, _, u, _, _, _, svs = _gm_common(z_ref[...], g_ref[...], b_ref[...], wc, bs_ref[...])
        for h in range(_GM_HEADS):
            sl = slice(h * _LANES, (h + 1) * _LANES)
            m_ref[:, sl] = (u[:, sl] * svs[h]).astype(m_ref.dtype)

    return pl.pallas_call(
        body, name="gmlp_fwd", grid=(L // _GM_CHUNK,),
        in_specs=[pl.BlockSpec((_GM_CHUNK, W2), lambda n: (n, 0)),
                  pl.BlockSpec((1, W), lambda n: (0, 0)), pl.BlockSpec((1, W), lambda n: (0, 0)),
                  pl.BlockSpec(w_s.shape, lambda n: (0, 0, 0)), pl.BlockSpec(bsT.shape, lambda n: (0, 0))],
        out_specs=pl.BlockSpec((_GM_CHUNK, W), lambda n: (n, 0)),
        out_shape=jax.ShapeDtypeStruct((L, W), _MXU),
        compiler_params=_cparams(("arbitrary",)),
    )(z, ln_g, ln_b, w_s, bsT)


def _gmlp_bwd(z, dm, ln_g, ln_b, w_s, bsT):
    L, W2 = z.shape
    W = W2 // 2
    T = _GM_CHUNK

    def body(z_ref, dm_ref, g_ref, b_ref, ws_ref, bs_ref, dz_ref, dws_ref, dbs_ref, dg_ref, db_ref):
        @pl.when(pl.program_id(0) == 0)
        def _():
            dws_ref[...] = jnp.zeros(dws_ref.shape, F32)
            dbs_ref[...] = jnp.zeros(dbs_ref.shape, F32)
            dg_ref[...] = jnp.zeros(dg_ref.shape, F32)
            db_ref[...] = jnp.zeros(db_ref.shape, F32)

        wc = _causal(ws_ref[...]).astype(_MXU)
        ln_g_v = g_ref[...]
        zu, zv, u, vh, rstd, vnb, svs = _gm_common(z_ref[...], ln_g_v, b_ref[...], wc, bs_ref[...])
        dmv = dm_ref[...]
        lane = lax.broadcasted_iota(jnp.int32, (T, _LANES), 1)
        dbs = jnp.zeros((T, _LANES), F32)
        dvn_parts = []
        for h in range(_GM_HEADS):
            sl = slice(h * _LANES, (h + 1) * _LANES)
            dsv = dmv[:, sl] * u[:, sl]
            dz_ref[:, sl] = (dmv[:, sl] * svs[h] * _gelu_grad(zu[:, sl])).astype(dz_ref.dtype)
            dbs = dbs + jnp.where(lane == h, jnp.sum(dsv, axis=1, keepdims=True), 0.0)
            dsvb = dsv.astype(_MXU)
            dws_ref[h] += _dot(dsvb, vnb[:, sl], _NT)
            dvn_parts.append(_dot(wc[h], dsvb, _TN))
        dbs_ref[...] += dbs
        dvn = jnp.concatenate(dvn_parts, axis=1)
        dg_ref[...] += jnp.sum(dvn * vh, axis=0, keepdims=True)
        db_ref[...] += jnp.sum(dvn, axis=0, keepdims=True)
        dxh = dvn * ln_g_v
        dv = rstd * (dxh - jnp.mean(dxh, axis=-1, keepdims=True) - vh * jnp.mean(dxh * vh, axis=-1, keepdims=True))
        dz_ref[:, W:] = (dv * _gelu_grad(zv)).astype(dz_ref.dtype)

        @pl.when(pl.program_id(0) == pl.num_programs(0) - 1)
        def _():
            dws_ref[...] = _causal(dws_ref[...])

    return pl.pallas_call(
        body, name="gmlp_bwd", grid=(L // T,),
        in_specs=[pl.BlockSpec((T, W2), lambda n: (n, 0)), pl.BlockSpec((T, W), lambda n: (n, 0)),
                  pl.BlockSpec((1, W), lambda n: (0, 0)), pl.BlockSpec((1, W), lambda n: (0, 0)),
                  pl.BlockSpec(w_s.shape, lambda n: (0, 0, 0)), pl.BlockSpec(bsT.shape, lambda n: (0, 0))],
        out_specs=[pl.BlockSpec((T, W2), lambda n: (n, 0)),
                   pl.BlockSpec(w_s.shape, lambda n: (0, 0, 0)), pl.BlockSpec((T, _LANES), lambda n: (0, 0)),
                   pl.BlockSpec((1, W), lambda n: (0, 0)), pl.BlockSpec((1, W), lambda n: (0, 0))],
        out_shape=[jax.ShapeDtypeStruct((L, W2), _MXU), jax.ShapeDtypeStruct(w_s.shape, F32),
                   jax.ShapeDtypeStruct((T, _LANES), F32),
                   jax.ShapeDtypeStruct((1, W), F32), jax.ShapeDtypeStruct((1, W), F32)],
        compiler_params=_cparams(("arbitrary",)),
    )(z, dm, ln_g, ln_b, w_s, bsT)


def _s5_prep_math(a_re, a_im, log_dt):
    dt = jnp.exp(log_dt)
    xr, xi = a_re * dt, a_im * dt
    e = jnp.exp(xr)
    lbr, lbi = e * jnp.cos(xi), e * jnp.sin(xi)
    dn = a_re * a_re + a_im * a_im
    nr, ni = lbr - 1.0, lbi
    pr, pi = nr * a_re + ni * a_im, ni * a_re - nr * a_im
    return dt, lbr, lbi, dn, nr, ni, pr, pi


def _vm():
    return pl.BlockSpec(memory_space=pltpu.VMEM)


def _s5_prep(a_re, a_im, log_dt, b_re, b_im):
    def body(ar_ref, ai_ref, ld_ref, br_ref, bi_ref, lbr_ref, lbi_ref, Br_ref, Bi_ref):
        _, lbr, lbi, dn, _, _, pr, pi = _s5_prep_math(ar_ref[...], ai_ref[...], ld_ref[...])
        cr, ci = (pr / dn)[:, None, :], (pi / dn)[:, None, :]
        lbr_ref[...] = lbr
        lbi_ref[...] = lbi
        br, bi = br_ref[...], bi_ref[...]
        Br_ref[...] = cr * br - ci * bi
        Bi_ref[...] = cr * bi + ci * br

    sd = jax.ShapeDtypeStruct
    return pl.pallas_call(
        body, name="s5_prep", in_specs=[_vm()] * 5, out_specs=[_vm()] * 4,
        out_shape=[sd(a_re.shape, F32), sd(a_re.shape, F32), sd(b_re.shape, F32), sd(b_re.shape, F32)],
    )(a_re, a_im, log_dt, b_re, b_im)


def _s5_prep_bwd(a_re, a_im, log_dt, b_re, b_im, dlbr_s, dlbi_s, dBr, dBi):
    def body(ar_ref, ai_ref, ld_ref, br_ref, bi_ref, dlr_ref, dli_ref, dBr_ref, dBi_ref,
             dar_ref, dai_ref, dld_ref, dbr_ref, dbi_ref):
        a_re_v, a_im_v = ar_ref[...], ai_ref[...]
        dt, lbr, lbi, dn, nr, ni, pr, pi = _s5_prep_math(a_re_v, a_im_v, ld_ref[...])
        cr, ci = (pr / dn)[:, None, :], (pi / dn)[:, None, :]
        br, bi, dBr_v, dBi_v = br_ref[...], bi_ref[...], dBr_ref[...], dBi_ref[...]
        dbr_ref[...] = cr * dBr_v + ci * dBi_v
        dbi_ref[...] = cr * dBi_v - ci * dBr_v
        dcr = jnp.sum(br * dBr_v + bi * dBi_v, axis=1)
        dci = jnp.sum(br * dBi_v - bi * dBr_v, axis=1)
        dpr, dpi = dcr / dn, dci / dn
        ddn = -(dcr * pr + dci * pi) / (dn * dn)
        dnr = dpr * a_re_v - dpi * a_im_v
        dni = dpr * a_im_v + dpi * a_re_v
        dlbr = dlr_ref[...] + dnr
        dlbi = dli_ref[...] + dni
        dxr = dlbr * lbr + dlbi * lbi
        dxi = dlbi * lbr - dlbr * lbi
        dar_ref[...] = dpr * nr + dpi * ni + 2.0 * ddn * a_re_v + dxr * dt
        dai_ref[...] = dpr * ni - dpi * nr + 2.0 * ddn * a_im_v + dxi * dt
        dld_ref[...] = jnp.sum(dxr * a_re_v + dxi * a_im_v, axis=0, keepdims=True) * dt

    sd = jax.ShapeDtypeStruct
    return pl.pallas_call(
        body, name="s5_prep_bwd", in_specs=[_vm()] * 9, out_specs=[_vm()] * 5,
        out_shape=[sd(a_re.shape, F32), sd(a_re.shape, F32), sd(log_dt.shape, F32),
                   sd(b_re.shape, F32), sd(b_re.shape, F32)],
    )(a_re, a_im, log_dt, b_re, b_im, dlbr_s, dlbi_s, dBr, dBi)


def _shift_rows(v, down):
    n = v.shape[0]
    rolled = pltpu.roll(v, 1 if down else n - 1, 0)
    row = lax.broadcasted_iota(jnp.int32, v.shape, 0)
    return jnp.where(row == (0 if down else n - 1), 0.0, rolled)


def _cmul(ar, ai, br, bi):
    return ar * br - ai * bi, ar * bi + ai * br


_SEG = 8
_UNROLL = 8


def _seg_rows(k):
    if isinstance(k, int):
        return pl.ds(k * _SEG, _SEG)
    return pl.ds(pl.multiple_of(k * _SEG, _SEG), _SEG)


def _unrolled(n, step, init):
    main = n // _UNROLL

    def trip(kk, s):
        for uu in range(_UNROLL):
            s = step(kk * _UNROLL + uu, s)
        return s

    s = lax.fori_loop(0, main, trip, init)
    for r in range(main * _UNROLL, n):
        s = step(r, s)
    return s


def _interleave(src_ref, dst_ref, nk):
    def step(k, carry):
        dst_ref[_seg_rows(k), :] = src_ref[pl.ds(k, _SEG, stride=nk), :]
        return carry
    _unrolled(nk, step, 0)


def _deinterleave(src_ref, dst_ref, nk):
    def step(k, carry):
        dst_ref[pl.ds(k, _SEG, stride=nk), :] = src_ref[_seg_rows(k), :]
        return carry
    _unrolled(nk, step, 0)


def _segment_inits(er, ei, ar, ai, nk, down):
    pr, pi = ar, ai
    for _ in range(int(math.log2(nk))):
        pr, pi = _cmul(pr, pi, pr, pi)
    fr, fi = er, ei
    for _ in range(_SEG - 1):
        sr, si = _shift_rows(fr, down), _shift_rows(fi, down)
        mr, mi = _cmul(pr, pi, sr, si)
        fr, fi = er + mr, ei + mi
    return _shift_rows(fr, down), _shift_rows(fi, down)


def _scan_states(x_re, x_im, ar, ai, nk):
    lanes = ar.shape[1]

    def step(k, s):
        rows = _seg_rows(k)
        mr, mi = _cmul(ar, ai, s[0], s[1])
        return mr + x_re[rows, :], mi + x_im[rows, :]

    zero = jnp.zeros((_SEG, lanes), F32)
    er, ei = _unrolled(nk, step, (zero, zero))
    ir, ii = _segment_inits(er, ei, ar, ai, nk, True)

    def step2(k, s):
        rows = _seg_rows(k)
        mr, mi = _cmul(ar, ai, s[0], s[1])
        nr, ni = mr + x_re[rows, :], mi + x_im[rows, :]
        x_re[rows, :] = nr
        x_im[rows, :] = ni
        return nr, ni

    _unrolled(nk, step2, (ir, ii))


def _s5_tile_fwd(u, bd_re, bd_im, cd_re, cd_im, ar, ai, d, s_re, s_im, nk):
    s_re[...] = _dot(u, bd_re, _NN)
    s_im[...] = _dot(u, bd_im, _NN)
    _scan_states(s_re, s_im, ar, ai, nk)
    return _dot(s_re[...], cd_re, _NN) - _dot(s_im[...], cd_im, _NN) + d * u


def _s5_specs(L, T):
    lanes = _S5_GT * _S5_P
    u_spec = pl.BlockSpec((L, _LANES), lambda t: (0, t))
    bd_spec = pl.BlockSpec((None, _S5_GT, _S5_C, _S5_P), lambda t: (t, 0, 0, 0))
    cd_spec = pl.BlockSpec((None, _S5_GT, _S5_P, _S5_C), lambda t: (t, 0, 0, 0))
    lam_spec = pl.BlockSpec((None, 1, lanes), lambda t: (t, 0, 0))
    d_spec = pl.BlockSpec((1, _LANES), lambda t: (0, t))
    return lanes, u_spec, bd_spec, cd_spec, lam_spec, d_spec


def _fill_block_diag(dst_ref, blocks_ref):
    _, a, b = blocks_ref.shape
    dst_ref[...] = jnp.zeros(dst_ref.shape, F32)
    for g in range(_S5_GT):
        dst_ref[g * a:(g + 1) * a, g * b:(g + 1) * b] = blocks_ref[g]


def _take_block_diag(dst_ref, v):
    _, a, b = dst_ref.shape
    for g in range(_S5_GT):
        dst_ref[g] = v[g * a:(g + 1) * a, g * b:(g + 1) * b]


def _s5_dense(bdr, bdi, cdr, cdi, dense):
    for src, dst in zip((bdr, bdi, cdr, cdi), dense):
        _fill_block_diag(dst, src)
    return [dst[...] for dst in dense]


def _s5_dense_scratch(lanes):
    return [pltpu.VMEM((_LANES, lanes), F32), pltpu.VMEM((_LANES, lanes), F32),
            pltpu.VMEM((lanes, _LANES), F32), pltpu.VMEM((lanes, _LANES), F32)]


def _s5_fwd(u, bd_re, bd_im, cd_re, cd_im, lam_re, lam_im, d):
    L, Wd = u.shape
    T = Wd // _LANES
    nk = L // _SEG
    lanes, u_spec, bd_spec, cd_spec, lam_spec, d_spec = _s5_specs(L, T)

    def body(u_ref, bdr, bdi, cdr, cdi, lr, li, d_ref, g_ref, s_re, s_im, up, tmp, *dense):
        ar = jnp.broadcast_to(lr[...], (_SEG, lanes))
        ai = jnp.broadcast_to(li[...], (_SEG, lanes))
        bd_re_v, bd_im_v, cd_re_v, cd_im_v = _s5_dense(bdr, bdi, cdr, cdi, dense)
        _interleave(u_ref, up, nk)
        y = _s5_tile_fwd(up[...], bd_re_v, bd_im_v, cd_re_v, cd_im_v, ar, ai, d_ref[...], s_re, s_im, nk)
        up[...] = _gelu(y)
        _deinterleave(up, tmp, nk)
        g_ref[...] = tmp[...].astype(g_ref.dtype)

    return pl.pallas_call(
        body, name="s5_fwd", grid=(T,),
        in_specs=[u_spec, bd_spec, bd_spec, cd_spec, cd_spec, lam_spec, lam_spec, d_spec],
        out_specs=u_spec, out_shape=jax.ShapeDtypeStruct((L, Wd), _MXU),
        scratch_shapes=[pltpu.VMEM((L, lanes), F32) for _ in range(2)]
        + [pltpu.VMEM((L, _LANES), F32) for _ in range(2)] + _s5_dense_scratch(lanes),
        compiler_params=_cparams(("arbitrary",)),
    )(u, bd_re, bd_im, cd_re, cd_im, lam_re, lam_im, d)


def _s5_bwd(u, dg, bd_re, bd_im, cd_re, cd_im, lam_re, lam_im, d):
    L, Wd = u.shape
    T = Wd // _LANES
    nk = L // _SEG
    lanes, u_spec, bd_spec, cd_spec, lam_spec, d_spec = _s5_specs(L, T)

    def body(u_ref, dg_ref, bdr, bdi, cdr, cdi, lr, li, d_ref,
             du_ref, dbdr, dbdi, dcdr, dcdi, dlr, dli, dd_ref, s_re, s_im, g_re, g_im, up, dgp, tmp, *dense):
        ar = jnp.broadcast_to(lr[...], (_SEG, lanes))
        ai = jnp.broadcast_to(li[...], (_SEG, lanes))
        bd_re_v, bd_im_v, cd_re_v, cd_im_v = _s5_dense(bdr, bdi, cdr, cdi, dense)
        _interleave(u_ref, up, nk)
        _interleave(dg_ref, dgp, nk)
        uv, dv = up[...], d_ref[...]
        y = _s5_tile_fwd(uv, bd_re_v, bd_im_v, cd_re_v, cd_im_v, ar, ai, dv, s_re, s_im, nk)
        dy = dgp[...] * _gelu_grad(y)
        dd_ref[...] = jnp.sum(dy * uv, axis=0, keepdims=True)
        dyb = dy.astype(_MXU)
        _take_block_diag(dcdr, _dot(dyb, s_re[...], _TN))
        _take_block_diag(dcdi, -_dot(dyb, s_im[...], _TN))
        g_re[...] = _dot(dyb, cd_re_v, _NT)
        g_im[...] = -_dot(dyb, cd_im_v, _NT)

        nai = -ai

        def step(j, s):
            rows = _seg_rows(nk - 1 - j)
            mr, mi = _cmul(ar, nai, s[0], s[1])
            return mr + g_re[rows, :], mi + g_im[rows, :]

        zero = jnp.zeros((_SEG, lanes), F32)
        er, ei = _unrolled(nk, step, (zero, zero))
        ir, ii = _segment_inits(er, ei, ar, nai, nk, False)

        def acc_lam(gr, gi, pr, pi, acc):
            return acc[0] + gr * pr + gi * pi, acc[1] + gi * pr - gr * pi

        def step2(j, carry):
            s, acc = carry
            k = nk - 1 - j
            rows = _seg_rows(k)
            mr, mi = _cmul(ar, nai, s[0], s[1])
            nr, ni = mr + g_re[rows, :], mi + g_im[rows, :]
            g_re[rows, :] = nr
            g_im[rows, :] = ni
            prev = _seg_rows(k - 1)
            return (nr, ni), acc_lam(nr, ni, s_re[prev, :], s_im[prev, :], acc)

        (g0r, g0i), acc = _unrolled(nk - 1, step2, ((ir, ii), (zero, zero)))
        first = _seg_rows(0)
        mr, mi = _cmul(ar, nai, g0r, g0i)
        nr, ni = mr + g_re[first, :], mi + g_im[first, :]
        g_re[first, :] = nr
        g_im[first, :] = ni
        last = _seg_rows(nk - 1)
        acc = acc_lam(nr, ni, _shift_rows(s_re[last, :], True), _shift_rows(s_im[last, :], True), acc)
        dlr[...] = jnp.sum(acc[0], axis=0, keepdims=True)
        dli[...] = jnp.sum(acc[1], axis=0, keepdims=True)

        gtr, gti = g_re[...].astype(_MXU), g_im[...].astype(_MXU)
        ub = uv.astype(_MXU)
        _take_block_diag(dbdr, _dot(ub, gtr, _TN))
        _take_block_diag(dbdi, _dot(ub, gti, _TN))
        dgp[...] = _dot(gtr, bd_re_v, _NT) + _dot(gti, bd_im_v, _NT) + dy * dv
        _deinterleave(dgp, tmp, nk)
        du_ref[...] = tmp[...].astype(du_ref.dtype)

    sd = jax.ShapeDtypeStruct
    big = sd((T, _S5_GT, _S5_C, _S5_P), F32)
    return pl.pallas_call(
        body, name="s5_bwd", grid=(T,),
        in_specs=[u_spec, u_spec, bd_spec, bd_spec, cd_spec, cd_spec, lam_spec, lam_spec, d_spec],
        out_specs=[u_spec, bd_spec, bd_spec, bd_spec, bd_spec, lam_spec, lam_spec, d_spec],
        out_shape=[sd((L, Wd), _MXU), big, big, big, big, sd((T, 1, lanes), F32), sd((T, 1, lanes), F32),
                   sd((1, Wd), F32)],
        scratch_shapes=[pltpu.VMEM((L, lanes), F32) for _ in range(4)]
        + [pltpu.VMEM((L, _LANES), F32) for _ in range(3)] + _s5_dense_scratch(lanes),
        compiler_params=_cparams(("arbitrary",)),
    )(u, dg, bd_re, bd_im, cd_re, cd_im, lam_re, lam_im, d)


def _half_tile(R, few_arrays=False):
    for t in ((512, 704, 128) if few_arrays else (256, 352, 128)):
        if R % t == 0:
            return t
    raise ValueError(R)


def _cast_shard(name, w, layer, kind, R, C):
    tr = _half_tile(R, True)
    nr = R // tr

    def body(w_ref, o_ref):
        o_ref[...] = w_ref[...].astype(o_ref.dtype)

    if kind == "col":
        in_map = lambda h, i: (layer, h * nr + i, 0)
    else:
        in_map = lambda h, i: (layer, i, h)
    return pl.pallas_call(
        body, name=name, grid=(2, nr), in_specs=[pl.BlockSpec((None, tr, C), in_map)],
        out_specs=pl.BlockSpec((None, tr, C), lambda h, i: (h, i, 0)),
        out_shape=jax.ShapeDtypeStruct((2, R, C), _WIRE),
        compiler_params=_cparams(("arbitrary", "arbitrary")),
    )(w)


def _adam_math(w, g, m, v):
    m2 = _B1 * m + (1.0 - _B1) * g
    v2 = _B2 * v + (1.0 - _B2) * (g * g)
    m_hat = m2 / (1.0 - _B1 ** _STEP)
    v_hat = v2 / (1.0 - _B2 ** _STEP)
    delta = -_LR * (m_hat / (jnp.sqrt(v_hat) + _AEPS) + _WD * w)
    return delta, m2, v2


def _adamw_big(name, w, m, v, layer, pair, kind, R, C, c, after, prev):
    tr = _half_tile(R, few_arrays=C <= 1024 and R % 512 == 0)
    nr = R // tr

    def body(c_ref, w_ref, m_ref, v_ref, own_ref, other_ref, *rest):
        go_ref, d_ref, mo_ref, vo_ref = rest[-4:]
        g = jnp.where(pl.program_id(0) == c_ref[0], own_ref[...], other_ref[...])
        delta, m2, v2 = _adam_math(w_ref[...], g, m_ref[...], v_ref[...])
        go_ref[...] = g
        d_ref[...] = delta
        mo_ref[...] = m2
        vo_ref[...] = v2

    if kind == "col":
        nat = pl.BlockSpec((None, tr, C), lambda h, i, c_ref: (layer, h * nr + i, 0))
    else:
        nat = pl.BlockSpec((None, tr, C), lambda h, i, c_ref: (layer, i, h))

    def gspec(own):
        return pl.BlockSpec((tr, C), lambda h, i, c_ref: (jnp.where((h == c_ref[0]) == own, i, 0), 0))

    carried = list(prev) if prev is not None else []
    gs = pltpu.PrefetchScalarGridSpec(
        num_scalar_prefetch=1, grid=(2, nr),
        in_specs=[nat, nat, nat, gspec(True), gspec(False), _any()] + [_any()] * len(carried),
        out_specs=[nat] * 4)
    sd = jax.ShapeDtypeStruct(w.shape, F32)
    return pl.pallas_call(
        body, name=name, grid_spec=gs, out_shape=[sd] * 4,
        input_output_aliases={7 + k: k for k in range(len(carried))},
        compiler_params=_cparams(("arbitrary", "arbitrary")),
    )(c.astype(jnp.int32).reshape(1), w, m, v, pair[0], pair[1], after, *carried)


def _adamw_flat(w, g, m, v):
    rows = w.shape[0]
    tr = rows // 8 if rows % 64 == 0 else rows

    def body(w_ref, g_ref, m_ref, v_ref, d_ref, mo_ref, vo_ref):
        delta, m2, v2 = _adam_math(w_ref[...], g_ref[...], m_ref[...], v_ref[...])
        d_ref[...] = delta
        mo_ref[...] = m2
        vo_ref[...] = v2

    spec = pl.BlockSpec((tr, _LANES), lambda i: (i, 0))
    sd = jax.ShapeDtypeStruct(w.shape, F32)
    return pl.pallas_call(
        body, name="adamw_small", grid=(rows // tr,), in_specs=[spec] * 4, out_specs=[spec] * 3,
        out_shape=[sd] * 3, compiler_params=_cparams(("arbitrary",)),
    )(w, g, m, v)


def _place():
    x, y, c = lax.axis_index("x"), lax.axis_index("y"), lax.axis_index("c")
    chips = [(1 - x, y), (x, 1 - y), (1 - x, 1 - y)]
    return x, y, c, 2 * x + y, chips


def _any():
    return pl.BlockSpec(memory_space=pl.ANY)


def _remote(src, dst, ssem, rsem, dev):
    return pltpu.make_async_remote_copy(src_ref=src, dst_ref=dst, send_sem=ssem, recv_sem=rsem,
                                        device_id=dev, device_id_type=_MESH)


def _allgather(name, shards):
    n = len(shards)

    def body(*refs):
        s_refs, g_refs = refs[:n], refs[n:2 * n]
        send0, recv0, send1, recv1, send2, recv2 = refs[2 * n:]
        x, y, c, q, _ = _place()
        sib, xn, yn = (x, y, 1 - c), (1 - x, y, c), (x, 1 - y, c)
        qx, qy, qd = 2 * (1 - x) + y, 2 * x + (1 - y), 2 * (1 - x) + (1 - y)
        _handshake([sib, xn, yn])
        own = [_remote(s_refs[a], g_refs[a].at[q], send0.at[a], recv0.at[a], sib) for a in range(n)]

        def pieces(a):
            g, half = g_refs[a], s_refs[a].shape[1] // 2
            return [g.at[qx, c], g.at[qy, c], g.at[qd, c, pl.ds(0, half)], g.at[qd, c, pl.ds(half, half)]]

        def relayed(a):
            g, half = g_refs[a], s_refs[a].shape[1] // 2
            return [(g.at[qx, c, pl.ds(0, half)], yn), (g.at[qy, c, pl.ds(half, half)], xn)]

        first = []
        for a in range(n):
            first.append(_remote(s_refs[a].at[c], g_refs[a].at[q, c], send1.at[4 * a], recv1.at[4 * a], xn))
            first.append(_remote(s_refs[a].at[c], g_refs[a].at[q, c], send1.at[4 * a + 1], recv1.at[4 * a + 1], yn))
        for cp in first + own:
            cp.start()
        later = []
        for a in range(n):
            land = pieces(a)
            for j in range(4):
                k = 4 * a + j
                _remote(land[j], land[j], send1.at[k], recv1.at[k], xn).wait_recv()
                if j < 2:
                    src, to = relayed(a)[j]
                    cp = _remote(src, src, send1.at[k + 2], recv1.at[k + 2], to)
                    cp.start()
                    later.append(cp)
                cp = _remote(land[j], land[j], send2.at[k], recv2.at[k], sib)
                cp.start()
                later.append(cp)
        for a in range(n):
            g, half = g_refs[a], s_refs[a].shape[1] // 2
            theirs = [g.at[qx, 1 - c], g.at[qy, 1 - c], g.at[qd, 1 - c, pl.ds(0, half)],
                      g.at[qd, 1 - c, pl.ds(half, half)]]
            for j in range(4):
                _remote(theirs[j], theirs[j], send2.at[4 * a + j], recv2.at[4 * a + j], sib).wait_recv()
        for cp in own:
            cp.wait()
        for cp in first + later:
            cp.wait_send()

    return _sequencer(name, _ID_GATHER, body, shards,
                      [jax.ShapeDtypeStruct((4,) + s.shape, s.dtype) for s in shards], [n, n] + [4 * n] * 4)


def _handshake(peers):
    barrier = pltpu.get_barrier_semaphore()
    for peer in peers:
        pl.semaphore_signal(barrier, inc=1, device_id=peer, device_id_type=_MESH)
    pl.semaphore_wait(barrier, len(peers))


_ID_SIBLING, _ID_CHIPS, _ID_GATHER, _ID_ALL = 1, 2, 3, 4


def _sequencer(name, collective_id, body, ins, out_types, sem_counts):
    mesh = plsc.ScalarSubcoreMesh(axis_name="seq", num_cores=1)
    return pl.kernel(
        body, name=name, out_type=out_types, mesh=mesh,
        scratch_types=[pltpu.SemaphoreType.DMA((k,)) for k in sem_counts],
        compiler_params=pltpu.CompilerParams(collective_id=collective_id),
    )(*ins)


def _swap_halves(name, grads):
    n = len(grads)

    def body(*refs):
        g_refs, t_refs = refs[:n], refs[n:2 * n]
        send, recv = refs[2 * n:]
        x, y, c, _, _ = _place()
        _handshake([(x, y, 1 - c)])
        cps = [_remote(g_refs[a].at[1 - c], t_refs[a], send.at[a], recv.at[a], (x, y, 1 - c)) for a in range(n)]
        for cp in cps:
            cp.start()
        for cp in cps:
            cp.wait()

    return _sequencer(name, _ID_SIBLING, body, grads,
                      [jax.ShapeDtypeStruct(g.shape[1:], g.dtype) for g in grads], [n, n])


def _chip_sum(name, g, t, after):
    _, _, R, C = g.shape
    tr = _half_tile(R, True)

    def body(c_ref, g_ref, t_ref, after_ref, o_ref):
        o_ref[...] = (g_ref[...].astype(F32) + t_ref[...].astype(F32)).astype(o_ref.dtype)

    c = lax.axis_index("c").astype(jnp.int32).reshape(1)
    gs = pltpu.PrefetchScalarGridSpec(
        num_scalar_prefetch=1, grid=(4, R // tr),
        in_specs=[pl.BlockSpec((None, None, tr, C), lambda r, i, c_ref: (c_ref[0], r, i, 0)),
                  pl.BlockSpec((None, tr, C), lambda r, i, c_ref: (r, i, 0)), _any()],
        out_specs=pl.BlockSpec((None, tr, C), lambda r, i, c_ref: (r, i, 0)))
    return pl.pallas_call(
        body, name=name, grid_spec=gs, out_shape=jax.ShapeDtypeStruct((4, R, C), _WIRE),
        compiler_params=_cparams(("arbitrary", "arbitrary")),
    )(c, g, t, after)


def _scatter_parts(name, parts):
    n = len(parts)

    def body(*refs):
        p_refs, t_refs = refs[:n], refs[n:2 * n]
        send, recv = refs[2 * n:]
        x, y, c, q, chips = _place()
        _handshake([(rx, ry, c) for rx, ry in chips])
        cps = []
        for a in range(n):
            for j, (rx, ry) in enumerate(chips):
                k = 3 * a + j
                cps.append(_remote(p_refs[a].at[2 * rx + ry], t_refs[a].at[q], send.at[k], recv.at[k], (rx, ry, c)))
        for cp in cps:
            cp.start()
        for a in range(n):
            for j, (rx, ry) in enumerate(chips):
                k = 3 * a + j
                land = t_refs[a].at[2 * rx + ry]
                _remote(land, land, send.at[k], recv.at[k], (rx, ry, c)).wait_recv()
        for cp in cps:
            cp.wait_send()

    return _sequencer(name, _ID_CHIPS, body, parts,
                      [jax.ShapeDtypeStruct(p.shape, p.dtype) for p in parts], [3 * n, 3 * n])


def _sum_parts(name, p, t, where, after):
    _, R, C = t.shape
    tr = _half_tile(R, True)

    def body(w_ref, p_ref, t0_ref, t1_ref, t2_ref, after_ref, o_ref):
        o_ref[...] = (p_ref[...].astype(F32) + t0_ref[...].astype(F32)
                      + t1_ref[...].astype(F32) + t2_ref[...].astype(F32))

    def part(slot):
        return pl.BlockSpec((None, tr, C), lambda i, w_ref: (w_ref[slot], i, 0))

    gs = pltpu.PrefetchScalarGridSpec(
        num_scalar_prefetch=1, grid=(R // tr,), in_specs=[part(0), part(1), part(2), part(3), _any()],
        out_specs=pl.BlockSpec((tr, C), lambda i, w_ref: (i, 0)))
    return pl.pallas_call(
        body, name=name, grid_spec=gs, out_shape=jax.ShapeDtypeStruct((R, C), F32),
        compiler_params=_cparams(("arbitrary",)),
    )(where, p, t, t, t, after)


def _send_halves(name, halves):
    n = len(halves)

    def body(*refs):
        h_refs, o_refs = refs[:n], refs[n:2 * n]
        send, recv = refs[2 * n:]
        x, y, c, _, _ = _place()
        _handshake([(x, y, 1 - c)])
        cps = [_remote(h_refs[a], o_refs[a], send.at[a], recv.at[a], (x, y, 1 - c)) for a in range(n)]
        for cp in cps:
            cp.start()
        for cp in cps:
            cp.wait()

    return _sequencer(name, _ID_SIBLING, body, halves,
                      [jax.ShapeDtypeStruct(h.shape, h.dtype) for h in halves], [n, n])


class _Order:
    def __init__(self):
        self.tok = None

    def tie(self, x):
        return x if self.tok is None else lax.optimization_barrier((x, self.tok))[0]

    def done(self, outs):
        self.tok = outs[0]
        return outs


class _Reduction:
    def __init__(self, tag, grads):
        self.tag, self.grads = tag, grads
        self.swapped = _swap_halves(f"rs_swap_{tag}", grads)

    def scatter(self, seq):
        self.parts = [seq.done([_chip_sum(f"rs_chipsum_{self.tag}_{a}", g, t, seq.tok)])[0]
                      for a, (g, t) in enumerate(zip(self.grads, self.swapped))]
        self.landed = _scatter_parts(f"rs_scatter_{self.tag}", self.parts)

    def finish(self, seq):
        x, y, c, q, chips = _place()
        where = jnp.stack([q] + [2 * rx + ry for rx, ry in chips]).astype(jnp.int32)
        halves = [seq.done([_sum_parts(f"rs_sum_{self.tag}_{a}", p, t, where, seq.tok)])[0]
                  for a, (p, t) in enumerate(zip(self.parts, self.landed))]
        return list(zip(halves, _send_halves(f"rs_join_{self.tag}", halves)))


def _allreduce_small(name, v):
    _, R, _ = v.shape

    def body(v_ref, o_ref, land, acc, send1, recv1, send2, recv2):
        x, y, c = lax.axis_index("x"), lax.axis_index("y"), lax.axis_index("c")
        me = 4 * x + 2 * y + c
        peers = []
        for k in range(1, 8):
            dx, dy, dc = (k >> 2) & 1, (k >> 1) & 1, k & 1
            px, py, pc = (1 - x if dx else x), (1 - y if dy else y), (1 - c if dc else c)
            peers.append((k, (px, py, pc), 4 * px + 2 * py + pc))
        land[me] = v_ref[me]
        out1 = [_remote(v_ref.at[pid], land.at[me], send1.at[k], recv1.at[k], dev) for k, dev, pid in peers]
        for cp in out1:
            cp.start()
        for k, dev, pid in peers:
            _remote(land.at[pid], land.at[pid], send1.at[k], recv1.at[k], dev).wait_recv()
        total = land[0]
        for j in range(1, 8):
            total = total + land[j]
        acc[...] = total
        o_ref[me] = total
        out2 = [_remote(acc, o_ref.at[me], send2.at[k], recv2.at[k], dev) for k, dev, pid in peers]
        for cp in out2:
            cp.start()
        for k, dev, pid in peers:
            _remote(o_ref.at[pid], o_ref.at[pid], send2.at[k], recv2.at[k], dev).wait_recv()
        for cp in out1 + out2:
            cp.wait_send()

    return pl.pallas_call(
        body, name=name, in_specs=[_vm()], out_specs=_vm(),
        out_shape=jax.ShapeDtypeStruct(v.shape, F32),
        scratch_shapes=[pltpu.VMEM(v.shape, F32), pltpu.VMEM((R, _LANES), F32)]
        + [pltpu.SemaphoreType.DMA((8,)) for _ in range(4)],
        compiler_params=pltpu.CompilerParams(vmem_limit_bytes=_VMEM_LIMIT),
    )(v)


def _all_peers():
    x, y, c = lax.axis_index("x"), lax.axis_index("y"), lax.axis_index("c")
    peers = []
    for k in range(1, 8):
        px, py, pc = (1 - x if k & 4 else x), (1 - y if k & 2 else y), (1 - c if k & 1 else c)
        peers.append((k, (px, py, pc), 4 * px + 2 * py + pc))
    return 4 * x + 2 * y + c, peers


def _exchange_slices(name, v):
    def body(v_ref, land, send, recv):
        me, peers = _all_peers()
        _handshake([dev for _, dev, _ in peers])
        cps = [_remote(v_ref.at[pid], land.at[me], send.at[k], recv.at[k], dev) for k, dev, pid in peers]
        for cp in cps:
            cp.start()
        for k, dev, pid in peers:
            _remote(land.at[pid], land.at[pid], send.at[k], recv.at[k], dev).wait_recv()
        for cp in cps:
            cp.wait_send()

    return _sequencer(name, _ID_ALL, body, [v], [jax.ShapeDtypeStruct(v.shape, v.dtype)], [8, 8])[0]


def _sum_slices(name, v, landed, after):
    _, R, _ = v.shape

    def body(v_ref, land_ref, after_ref, o_ref):
        me, peers = _all_peers()
        acc = v_ref[me]
        for _, _, pid in peers:
            acc = acc + land_ref[pid]
        o_ref[...] = acc

    return pl.pallas_call(
        body, name=name, in_specs=[_vm(), _vm(), _any()], out_specs=_vm(),
        out_shape=jax.ShapeDtypeStruct((R, _LANES), F32),
        compiler_params=pltpu.CompilerParams(vmem_limit_bytes=_VMEM_LIMIT),
    )(v, landed, after)


def _broadcast_slices(name, s):
    def body(s_ref, out, send, recv):
        me, peers = _all_peers()
        _handshake([dev for _, dev, _ in peers])
        cps = [_remote(s_ref, out.at[me], send.at[k], recv.at[k], dev) for k, dev, pid in peers]
        for cp in cps:
            cp.start()
        for k, dev, pid in peers:
            _remote(out.at[pid], out.at[pid], send.at[k], recv.at[k], dev).wait_recv()
        for cp in cps:
            cp.wait_send()

    return _sequencer(name, _ID_ALL, body, [s], [jax.ShapeDtypeStruct((8,) + s.shape, s.dtype)], [8, 8])[0]


_WEIGHT_NAMES = ['norm_mix', 'norm_ffn', 'norm_ple', 'norm_final', 'gm_w_in', 'gm_ln_g', 'gm_ln_b', 'gm_w_s',
                 'gm_b_s', 'gm_w_out', 's5_w_in', 's5_a_re', 's5_a_im', 's5_log_dt', 's5_b_re', 's5_b_im',
                 's5_c_re', 's5_c_im', 's5_d', 's5_w_out', 'ffn_w1', 'ffn_w3', 'ffn_w2', 'ple_w_gate', 'ple_w_proj']
_BIG = {'gm_w_in': 'col', 'gm_w_out': 'row', 's5_w_in': 'row', 's5_w_out': 'col', 'ffn_w1': 'col',
        'ffn_w3': 'col', 'ffn_w2': 'row', 'ple_w_gate': 'row', 'ple_w_proj': 'col'}


_VIEW = {'s5_a_re': (0, 2, 1), 's5_a_im': (0, 2, 1), 's5_b_re': (0, 2, 3, 1), 's5_b_im': (0, 2, 3, 1),
         's5_c_re': (0, 2, 3, 1), 's5_c_im': (0, 2, 3, 1)}


def _to_view(name, a):
    return jnp.transpose(a, _VIEW[name]) if name in _VIEW else a


def _from_view(name, a):
    if name not in _VIEW:
        return a
    perm = _VIEW[name]
    return jnp.transpose(a, [perm.index(i) for i in range(len(perm))])


def _rc(kind, shard_shape):
    rows, cols = shard_shape[-2:]
    return (rows // 2, cols) if kind == "col" else (rows, cols // 2)


def _pack(vecs, rows_multiple):
    flat = jnp.concatenate([a.reshape(-1).astype(F32) for a in vecs])
    unit = rows_multiple * _LANES
    pad = (-flat.shape[0]) % unit
    return jnp.pad(flat, (0, pad)).reshape(-1, _LANES)


def _unpack(buf, shapes):
    flat = buf.reshape(-1)
    out, off = [], 0
    for s in shapes:
        n = math.prod(s)
        out.append(flat[off:off + n].reshape(s))
        off += n
    return out


def _ident(accs, ex):
    return accs


def _add_resid(accs, ex):
    return [accs[0] + ex[0]]


def _swiglu_epi(accs, ex):
    a, b = accs
    return [a, b, a * _sig(a) * b]


def _swiglu_bwd_epi(accs, ex):
    df = accs[0]
    a, b = ex[0].astype(F32), ex[1].astype(F32)
    sa = _sig(a)
    return [df * b * (sa * (1.0 + a * (1.0 - sa))), df * (a * sa)]


def _ple_epi(accs, ex):
    gt = _sig(accs[0])
    return [ex[0] + gt * ex[1], gt]


def _glu_epi(accs, ex):
    val, sg = accs[0], _sig(accs[1])
    return [ex[0] + val * sg, val, sg]


def kernel(x, p, norm_mix, norm_ffn, norm_ple, norm_final, gm_w_in, gm_ln_g, gm_ln_b, gm_w_s, gm_b_s, gm_w_out, s5_w_in, s5_a_re, s5_a_im, s5_log_dt, s5_b_re, s5_b_im, s5_c_re, s5_c_im, s5_d, s5_w_out, ffn_w1, ffn_w3, ffn_w2, ple_w_gate, ple_w_proj, loss_target, m_norm_mix, m_norm_ffn, m_norm_ple, m_norm_final, m_gm_w_in, m_gm_ln_g, m_gm_ln_b, m_gm_w_s, m_gm_b_s, m_gm_w_out, m_s5_w_in, m_s5_a_re, m_s5_a_im, m_s5_log_dt, m_s5_b_re, m_s5_b_im, m_s5_c_re, m_s5_c_im, m_s5_d, m_s5_w_out, m_ffn_w1, m_ffn_w3, m_ffn_w2, m_ple_w_gate, m_ple_w_proj, v_norm_mix, v_norm_ffn, v_norm_ple, v_norm_final, v_gm_w_in, v_gm_ln_g, v_gm_ln_b, v_gm_w_s, v_gm_b_s, v_gm_w_out, v_s5_w_in, v_s5_a_re, v_s5_a_im, v_s5_log_dt, v_s5_b_re, v_s5_b_im, v_s5_c_re, v_s5_c_im, v_s5_d, v_s5_w_out, v_ffn_w1, v_ffn_w3, v_ffn_w2, v_ple_w_gate, v_ple_w_proj):
    env = dict(locals())
    w = {n: env[n] for n in _WEIGHT_NAMES}
    mom = {n: env["m_" + n] for n in _WEIGHT_NAMES}
    var = {n: env["v_" + n] for n in _WEIGHT_NAMES}
    xs, tgt = x[0], loss_target[0]
    L, D = xs.shape
    depth = norm_mix.shape[0]
    qx, qy = lax.axis_index("x"), lax.axis_index("y")
    q = 2 * qx + qy

    def gather(tag, items):
        shards = []
        for name, layer in items:
            kind = _BIG[name]
            R, C = _rc(kind, w[name].shape)
            shards.append(_cast_shard(f"cast_{name}{layer}", w[name], layer, kind, R, C))
        full = _allgather(f"ag_{tag}", shards)
        return {it: _W(f, _BIG[it[0]]) for it, f in zip(items, full)}

    W = {}
    mixers = [["gm_w_in", "gm_w_out"], ["s5_w_in", "s5_w_out"]]
    for i in range(depth):
        for n in mixers[i]:
            W.update(gather(f"{n}", [(n, 0)]))
        W.update(gather(f"ffn_up{i}", [("ffn_w1", i), ("ffn_w3", i)]))
        W.update(gather(f"ffn_down{i}", [("ffn_w2", i)]))
        W.update(gather(f"ple{i}", [("ple_w_gate", i), ("ple_w_proj", i)]))

    d_slots = jnp.zeros((4, D // 4), F32)
    d_slots = lax.dynamic_update_slice(d_slots, s5_d.astype(F32), (q, 0))
    d_sum = _allreduce_small("ar_s5_d", _pack([d_slots], 64).reshape(8, -1, _LANES))
    d_full = (d_sum.reshape(-1)[:D] * 0.5).reshape(1, D)

    def ffn_fwd(i, xin):
        hf = _rms_fwd(f"rms_ffn{i}", xin, norm_ffn[i:i + 1])
        a, b, f = _mm_nn(f"ffn_up{i}", hf, [W["ffn_w1", i], W["ffn_w3", i]], 1024, 1408, ffn_w2.shape[1] * 4,
                         [], [_MXU, _MXU, _MXU], _swiglu_epi, tm=1024)
        xo = _mm_nn(f"ffn_down{i}", f, [W["ffn_w2", i]], 1408, 1024, D, [xin], [F32], _add_resid, tm=1024)[0]
        return xo, (xin, hf, a, b, f)

    def ple_fwd(i, xin):
        hp = _rms_fwd(f"rms_ple{i}", xin, norm_ple[i:i + 1])
        pi = p[i, 0]
        pp = _mm_nn(f"ple_proj{i}", pi, [W["ple_w_proj", i]], 128, 512, D, [], [F32], _ident, tm=2048)[0]
        xo, gt = _mm_nn(f"ple_gate{i}", hp, [W["ple_w_gate", i]], 512, 1024, D, [xin, pp], [F32, _MXU], _ple_epi,
                        tm=1024)
        return xo, (xin, hp, pi, pp, gt)

    h0 = _rms_fwd("rms_mix0", xs, norm_mix[0:1])
    z = _mm_nn("gm_in", h0, [W["gm_w_in", 0]], 1024, 1024, 2 * D, [], [F32], _ident, tm=1024)[0]
    bsT = gm_b_s[0].T
    gm_m = _gmlp_fwd(z, gm_ln_g, gm_ln_b, gm_w_s[0], bsT)
    x1 = _mm_nn("gm_out", gm_m, [W["gm_w_out", 0]], 512, 1024, D, [xs], [F32], _add_resid, tm=1024)[0]
    x2, ffn0 = ffn_fwd(0, x1)
    x3, ple0 = ple_fwd(0, x2)

    T = D // _LANES
    lanes = _S5_GT * _S5_P
    sv = {n: _to_view(n, w[n])[0] for n in _VIEW}
    a_re, a_im, log_dt = sv["s5_a_re"], sv["s5_a_im"], s5_log_dt
    lbr, lbi, Bbar_re, Bbar_im = _s5_prep(a_re, a_im, log_dt, sv["s5_b_re"], sv["s5_b_im"])

    def to_bd(B):
        return jnp.transpose(B.reshape(_S5_P, _S5_C, T, _S5_GT), (2, 3, 1, 0))

    def to_cd(cw):
        return jnp.transpose(cw.reshape(_S5_C, _S5_P, T, _S5_GT), (2, 3, 1, 0))

    def to_lam(v):
        return jnp.transpose(v).reshape(T, 1, lanes)

    bd_re, bd_im = to_bd(Bbar_re), to_bd(Bbar_im)
    cd_re, cd_im = to_cd(sv["s5_c_re"]), to_cd(sv["s5_c_im"])
    lam_re, lam_im = to_lam(lbr), to_lam(lbi)

    h1 = _rms_fwd("rms_mix1", x3, norm_mix[1:2])
    u = _mm_nn("s5_in", h1, [W["s5_w_in", 0]], 512, 1024, D, [], [F32], _ident, tm=1024)[0]
    s5_g = _s5_fwd(u, bd_re, bd_im, cd_re, cd_im, lam_re, lam_im, d_full)
    x4, glu_val, glu_sg = _mm_nn("s5_out", s5_g, [W["s5_w_out", 0], W["s5_w_out", 0]], 1024, 1024, D, [x3],
                                 [F32, _MXU, _MXU], _glu_epi, tm=1024, cb_offsets=[0, 2])
    x5, ffn1 = ffn_fwd(1, x4)
    x6, ple1 = ple_fwd(1, x5)

    dx, d_norm_final, loss_rows = _loss_head(x6, norm_final[None], tgt)
    loss = lax.psum(loss_rows[0, 0], ("x", "y", "c"))

    small = {}

    seq = _Order()
    tie, done = seq.tie, seq.done
    d_norm_ple, d_norm_ffn, d_norm_mix = [None] * depth, [None] * depth, [None] * depth
    reduced = {}

    def keep(names, layer, pairs):
        for n, pr in zip(names, pairs):
            reduced[n, layer] = pr

    ple_names, up_names, down_names = ["ple_w_gate", "ple_w_proj"], ["ffn_w1", "ffn_w3"], ["ffn_w2"]

    def ple_bwd(i, dxo, saved):
        xin, hp, pi, pp, gt = saved
        dpre, dpp = done(_ple_bwd_elem(tie(dxo), pp, gt))
        dwg = done(_mm_tn(f"ple_gate_dw{i}", tie(hp), [dpre], "row", 512, 1024, 512, 1024))[0]
        dwp = done(_mm_tn(f"ple_proj_dw{i}", tie(pi), [dpp], "col", 128, 512, 128, 512))[0]
        red = _Reduction(f"ple{i}", [dwg, dwp])
        dhp = done(_mm_nt(f"ple_gate_dx{i}", [tie(dpre)], [W["ple_w_gate", i]], 512, 1024, [], [F32], _ident,
                          tm=2048))[0]
        dxin, dxin_mxu, dg = done(_rms_bwd(f"rms_ple_bwd{i}", tie(dhp), xin, norm_ple[i:i + 1], dxo))
        return dxin, dxin_mxu, dg, red

    def ffn_bwd(i, dxo, dxo_mxu, saved, before_up, before_end):
        xin, hf, a, b, f = saved
        dw2 = done(_mm_tn(f"ffn_down_dw{i}", tie(f), [dxo_mxu], "row", 1408, 1024, 1408, 1024))[0]
        r_down = _Reduction(f"ffd{i}", [dw2])
        da, db = done(_mm_nt(f"ffn_down_dx{i}", [tie(dxo_mxu)], [W["ffn_w2", i]], 1408, 1024, [a, b], [_MXU, _MXU],
                             _swiglu_bwd_epi, tm=1024))
        for step in before_up:
            step()
        r_down.scatter(seq)
        dw1, dw3 = done(_mm_tn(f"ffn_up_dw{i}", tie(hf), [da, db], "col", 1024, 1408, 1024, 1408))
        r_up = _Reduction(f"ffu{i}", [dw1, dw3])
        dhf = done(_mm_nt(f"ffn_up_dx{i}", [tie(da), db], [W["ffn_w1", i], W["ffn_w3", i]], 1024, 1408, [], [F32],
                          _ident, tm=1024))[0]
        dxin, dxin_mxu, dg = done(_rms_bwd(f"rms_ffn_bwd{i}", tie(dhf), xin, norm_ffn[i:i + 1], dxo))
        for step in before_end:
            step()
        r_up.scatter(seq)
        return dxin, dxin_mxu, dg, r_down, r_up

    dx, dx_mxu, d_norm_ple[1], r_ple1 = ple_bwd(1, dx, ple1)
    dx, _, d_norm_ffn[1], r_down1, r_up1 = ffn_bwd(1, dx, dx_mxu, ffn1, [lambda: r_ple1.scatter(seq)], [])

    do = done([_glu_bwd_elem(tie(dx), glu_val, glu_sg)])[0]
    dw_s5out = done(_mm_tn("s5_out_dw", tie(s5_g), [do], "col", 1024, 1024, 1024, 1024))[0]
    r_s5out = _Reduction("s5out", [dw_s5out])
    dgy = done(_mm_nt("s5_out_dx", [tie(do)], [W["s5_w_out", 0]], 1024, 1024, [], [F32], _ident, tm=1024))[0]
    keep(ple_names, 1, r_ple1.finish(seq))
    keep(down_names, 1, r_down1.finish(seq))
    du, dbd_re, dbd_im, dcd_re, dcd_im, dl_re, dl_im, dd = done(_s5_bwd(tie(u), dgy, bd_re, bd_im, cd_re, cd_im,
                                                                       lam_re, lam_im, d_full))
    r_s5out.scatter(seq)
    dw_s5in = done(_mm_tn("s5_in_dw", tie(h1), [du], "row", 512, 1024, 512, 1024))[0]
    r_s5in = _Reduction("s5in", [dw_s5in])
    dh1 = done(_mm_nt("s5_in_dx", [tie(du)], [W["s5_w_in", 0]], 512, 1024, [], [F32], _ident, tm=2048))[0]
    dx, _, d_norm_mix[1] = done(_rms_bwd("rms_mix1_bwd", tie(dh1), x3, norm_mix[1:2], dx))
    keep(up_names, 1, r_up1.finish(seq))
    r_s5in.scatter(seq)

    def from_bd(t):
        return jnp.transpose(t, (3, 2, 0, 1)).reshape(_S5_P, _S5_C, T * _S5_GT)

    def from_cdT(t):
        return jnp.transpose(t, (2, 3, 0, 1)).reshape(_S5_C, _S5_P, T * _S5_GT)

    def from_lam(t):
        return jnp.transpose(t.reshape(T * _S5_GT, _S5_P))

    da_re, da_im, dlog_dt, db_re, db_im = _s5_prep_bwd(
        a_re, a_im, log_dt, sv["s5_b_re"], sv["s5_b_im"], from_lam(dl_re), from_lam(dl_im),
        from_bd(dbd_re), from_bd(dbd_im))
    small["s5_a_re"], small["s5_a_im"], small["s5_log_dt"] = da_re[None], da_im[None], dlog_dt
    small["s5_b_re"], small["s5_b_im"] = db_re[None], db_im[None]
    small["s5_c_re"], small["s5_c_im"] = from_cdT(dcd_re)[None], from_cdT(dcd_im)[None]

    dx, dx_mxu, d_norm_ple[0], r_ple0 = ple_bwd(0, dx, ple0)
    keep(["s5_w_out"], 0, r_s5out.finish(seq))
    dx, dx_mxu, d_norm_ffn[0], r_down0, r_up0 = ffn_bwd(
        0, dx, dx_mxu, ffn0, [lambda: r_ple0.scatter(seq), lambda: keep(["s5_w_in"], 0, r_s5in.finish(seq))],
        [lambda: keep(ple_names, 0, r_ple0.finish(seq))])

    dw_gmout = done(_mm_tn("gm_out_dw", tie(gm_m), [dx_mxu], "row", 512, 1024, 512, 1024))[0]
    r_gmout = _Reduction("gmout", [dw_gmout])
    dgm = done(_mm_nt("gm_out_dx", [tie(dx_mxu)], [W["gm_w_out", 0]], 512, 1024, [], [F32], _ident, tm=2048))[0]
    dz, dws, dbsT, dlng, dlnb = done(_gmlp_bwd(tie(z), dgm, gm_ln_g, gm_ln_b, gm_w_s[0], bsT))
    keep(down_names, 0, r_down0.finish(seq))
    r_gmout.scatter(seq)
    dw_gmin = done(_mm_tn("gm_in_dw", tie(h0), [dz], "col", 1024, 1024, 1024, 1024))[0]
    r_gmin = _Reduction("gmin", [dw_gmin])
    dh0 = done(_mm_nt("gm_in_dx", [tie(dz)], [W["gm_w_in", 0]], 1024, 1024, [], [F32], _ident, tm=1024))[0]
    dx, _, d_norm_mix[0] = done(_rms_bwd("rms_mix0_bwd", tie(dh0), xs, norm_mix[0:1], dx))
    r_gmin.scatter(seq)
    grad_x = dx[None]

    small["norm_mix"], small["norm_ffn"] = jnp.concatenate(d_norm_mix), jnp.concatenate(d_norm_ffn)
    small["norm_ple"], small["norm_final"] = jnp.concatenate(d_norm_ple), d_norm_final[0]
    small["gm_ln_g"], small["gm_ln_b"], small["gm_w_s"] = dlng, dlnb, dws[None]
    small["gm_b_s"] = dbsT[:, :_GM_HEADS].T[None]
    small["s5_d"] = dd

    small_names = [n for n in _WEIGHT_NAMES if n not in _BIG]
    packed = _pack([small[n] for n in small_names], 64).reshape(8, -1, _LANES)
    landed = _exchange_slices("ar_small_in", packed)
    grads, deltas, new_m, new_v = {}, {}, {}, {}
    my_c = lax.axis_index("c")

    def adamw(n, layer, prev):
        kind = _BIG[n]
        R, C = _rc(kind, w[n].shape)
        return done(_adamw_big(f"adamw_{n}{layer}", w[n], mom[n], var[n], layer, reduced[n, layer], kind, R, C,
                               my_c, seq.tok, prev))

    late = {}
    for n in down_names + up_names:
        late[n] = adamw(n, 1, None)
    mine = done([_sum_slices("ar_small_sum", packed, landed, seq.tok)])[0]
    spread = _broadcast_slices("ar_small_out", mine)
    for n in ple_names:
        late[n] = adamw(n, 1, None)
    for n in ["s5_w_in", "s5_w_out"]:
        grads[n], deltas[n], new_m[n], new_v[n] = adamw(n, 0, None)
    for n in ple_names + down_names:
        grads[n], deltas[n], new_m[n], new_v[n] = adamw(n, 0, late[n])
    summed = lax.dynamic_update_slice(spread, mine[None], (4 * qx + 2 * qy + my_c, 0, 0))
    red_small = dict(zip(small_names, _unpack(summed, [small[n].shape for n in small_names])))
    red_small["s5_d"] = lax.dynamic_slice(red_small["s5_d"], (0, q * (D // 4)), (1, D // 4))
    keep(up_names, 0, r_up0.finish(seq))
    keep(["gm_w_out"], 0, r_gmout.finish(seq))
    for n in up_names + ["gm_w_out"]:
        grads[n], deltas[n], new_m[n], new_v[n] = adamw(n, 0, late.get(n))
    keep(["gm_w_in"], 0, r_gmin.finish(seq))
    grads["gm_w_in"], deltas["gm_w_in"], new_m["gm_w_in"], new_v["gm_w_in"] = adamw("gm_w_in", 0, None)

    def views(src):
        return [_to_view(n, src[n]) for n in small_names]

    shapes = [v_.shape for v_ in views(w)]
    dl, mo, vo = _adamw_flat(_pack(views(w), 64), _pack([red_small[n] for n in small_names], 64),
                             _pack(views(mom), 64), _pack(views(var), 64))
    for n, g_, d_, m_, v_ in zip(small_names, [red_small[n] for n in small_names], _unpack(dl, shapes),
                                 _unpack(mo, shapes), _unpack(vo, shapes)):
        grads[n], deltas[n], new_m[n], new_v[n] = [_from_view(n, t_).reshape(w[n].shape) for t_ in (g_, d_, m_, v_)]

    return (loss, grad_x, *[grads[n] for n in _WEIGHT_NAMES], *[deltas[n] for n in _WEIGHT_NAMES],
            *[new_m[n] for n in _WEIGHT_NAMES], *[new_v[n] for n in _WEIGHT_NAMES])
```

```python
import functools
import math

import jax
import jax.numpy as jnp
from jax import lax
from jax.experimental import pallas as pl
from jax.experimental.pallas import tpu as pltpu
from jax.experimental.pallas import tpu_sc as plsc

F32 = jnp.float32
_MXU = jnp.bfloat16
_WIRE = jnp.bfloat16
_EPS = 1e-6
_VMEM_LIMIT = 56 * 1024 * 1024
_LANES = 128
_MESH = pl.DeviceIdType.MESH

_LR, _B1, _B2, _AEPS, _WD, _STEP = 0.001, 0.9, 0.999, 1e-08, 0.01, 10

_GM_CHUNK = 128
_GM_HEADS = 16
_S5_GT = 8
_S5_P = 64
_S5_C = 16

_NN = (((1,), (0,)), ((), ()))
_NT = (((1,), (1,)), ((), ()))
_TN = (((0,), (0,)), ((), ()))


def _cparams(sem):
    return pltpu.CompilerParams(dimension_semantics=sem, vmem_limit_bytes=_VMEM_LIMIT)


def _sig(x):
    return 1.0 / (1.0 + jnp.exp(-x))


_GC = math.sqrt(2.0 / math.pi)


def _gelu(x):
    return 0.5 * x * (1.0 + jnp.tanh(_GC * (x + 0.044715 * (x * x * x))))


def _gelu_grad(x):
    t = jnp.tanh(_GC * (x + 0.044715 * (x * x * x)))
    return 0.5 * (1.0 + t) + 0.5 * x * (1.0 - t * t) * (_GC * (1.0 + 3.0 * 0.044715 * x * x))


def _dot(a, b, dn):
    return lax.dot_general(a.astype(_MXU), b.astype(_MXU), dn, preferred_element_type=F32)


class _W:
    def __init__(self, arr, kind):
        self.a, self.kind = arr, kind
        self.R, self.C = arr.shape[2], arr.shape[3]

    def full_shape(self):
        return (2 * self.R, 4 * self.C) if self.kind == "col" else (4 * self.R, 2 * self.C)


def _part_index(kind, R, C, tr, tc, rb, cb):
    nr, nc = R // tr, C // tc
    if kind == "col":
        return cb // nc, rb // nr, rb % nr, cb % nc
    return rb // nr, cb // nc, rb % nr, cb % nc


def _wspec(w, tr, tc, rb_fn, cb_fn):
    assert w.R % tr == 0 and w.C % tc == 0, (w.R, w.C, tr, tc)

    def imap(i, j, k):
        return _part_index(w.kind, w.R, w.C, tr, tc, rb_fn(i, j, k), cb_fn(i, j, k))

    return pl.BlockSpec((None, None, tr, tc), imap)


def _gspec(kind, R, C, tr, tc):
    assert R % tr == 0 and C % tc == 0, (R, C, tr, tc)

    def imap(i, j, k):
        part, half, rbi, cbi = _part_index(kind, R, C, tr, tc, i, j)
        return half, part, rbi, cbi

    return pl.BlockSpec((None, None, tr, tc), imap)


def _mm(name, grid, a_ops, b_ops, pairs, acc_shape, n_acc, extras, outs, epilogue):
    nk = grid[2]
    na, nb, ne, no = len(a_ops), len(b_ops), len(extras), len(outs)

    def body(*refs):
        a_refs = refs[:na]
        b_refs = refs[na:na + nb]
        e_refs = refs[na + nb:na + nb + ne]
        o_refs = refs[na + nb + ne:na + nb + ne + no]
        acc_refs = refs[na + nb + ne + no:]
        k = pl.program_id(2)

        def products():
            sums = [None] * n_acc
            for ai, bi, ci, dn in pairs:
                d = _dot(a_refs[ai][...], b_refs[bi][...], dn)
                sums[ci] = d if sums[ci] is None else sums[ci] + d
            return sums

        def finish(accs):
            res = epilogue(accs, [e[...] for e in e_refs])
            for o, r in zip(o_refs, res):
                o[...] = r.astype(o.dtype)

        if nk == 1:
            finish(products())
            return

        @pl.when(k == 0)
        def _():
            for acc, d in zip(acc_refs, products()):
                acc[...] = d

        @pl.when(jnp.logical_and(k > 0, k < nk - 1))
        def _():
            for acc, d in zip(acc_refs, products()):
                acc[...] += d

        @pl.when(k == nk - 1)
        def _():
            finish([acc[...] + d for acc, d in zip(acc_refs, products())])

    ops = list(a_ops) + list(b_ops) + list(extras)
    return pl.pallas_call(
        body, name=name, grid=grid,
        in_specs=[s for _, s in ops],
        out_specs=[s for _, s in outs],
        out_shape=[s for s, _ in outs],
        scratch_shapes=[pltpu.VMEM(acc_shape, F32) for _ in range(n_acc if nk > 1 else 0)],
        compiler_params=_cparams(("parallel", "parallel", "arbitrary")),
    )(*[a for a, _ in ops])


def _bs(shape, fn):
    return pl.BlockSpec(shape, fn)


def _tile_m(L):
    return min(L, 512)


def _mm_nn(name, x, ws, tk, tn, n_out, extras, outs_sd, epilogue, tm=None, cb_offsets=None):
    M, K = x.shape
    tm = min(M, tm or _tile_m(M))
    grid = (M // tm, n_out // tn, K // tk)
    a_ops = [(x, _bs((tm, tk), lambda i, j, k: (i, k)))]
    cb_offsets = cb_offsets or [0] * len(ws)
    b_ops = [(w.a, _wspec(w, tk, tn, lambda i, j, k: k, (lambda off: lambda i, j, k: j + off)(off)))
             for w, off in zip(ws, cb_offsets)]
    pairs = [(0, bi, bi, _NN) for bi in range(len(ws))]
    mn = _bs((tm, tn), lambda i, j, k: (i, j))
    ex = [(e, mn) for e in extras]
    outs = [(jax.ShapeDtypeStruct((M, n_out), dt), mn) for dt in outs_sd]
    return _mm(name, grid, a_ops, b_ops, pairs, (tm, tn), len(ws), ex, outs, epilogue)


def _mm_nt(name, xs, ws, tn, tk, extras, outs_sd, epilogue, tm=None):
    M, Nw = xs[0].shape
    Kw = ws[0].full_shape()[0]
    tm = min(M, tm or _tile_m(M))
    grid = (M // tm, Kw // tn, Nw // tk)
    a_ops = [(x, _bs((tm, tk), lambda i, j, k: (i, k))) for x in xs]
    b_ops = [(w.a, _wspec(w, tn, tk, lambda i, j, k: j, lambda i, j, k: k)) for w in ws]
    pairs = [(i, i, 0, _NT) for i in range(len(ws))]
    mn = _bs((tm, tn), lambda i, j, k: (i, j))
    ex = [(e, mn) for e in extras]
    outs = [(jax.ShapeDtypeStruct((M, Kw), dt), mn) for dt in outs_sd]
    return _mm(name, grid, a_ops, b_ops, pairs, (tm, tn), 1, ex, outs, epilogue)


def _mm_tn(name, x, dys, kind, R, C, tm, tn, tk=None):
    L, Kw = x.shape
    Nw = dys[0].shape[1]
    tk = tk or min(L, 1024)
    grid = (Kw // tm, Nw // tn, L // tk)
    a_ops = [(x, _bs((tk, tm), lambda i, j, k: (k, i)))]
    b_ops = [(dy, _bs((tk, tn), lambda i, j, k: (k, j))) for dy in dys]
    pairs = [(0, bi, bi, _TN) for bi in range(len(dys))]
    gs = _gspec(kind, R, C, tm, tn)
    outs = [(jax.ShapeDtypeStruct((2, 4, R, C), _WIRE), gs) for _ in dys]
    return _mm(name, grid, a_ops, b_ops, pairs, (tm, tn), len(dys), [], outs, lambda accs, ex: accs)


def _row_tile(L):
    return min(L, 256)


def _rowwise(name, body, ins, outs, L, acc_outs=()):
    tr = _row_tile(L)
    n_in, n_out = len(ins), len(outs)

    def kbody(*refs):
        i_refs, o_refs, a_refs = refs[:n_in], refs[n_in:n_in + n_out], refs[n_in + n_out:]
        res, sums = body(*[r[...] for r in i_refs])
        for o, r in zip(o_refs, res):
            o[...] = r.astype(o.dtype)
        if a_refs:
            @pl.when(pl.program_id(0) == 0)
            def _():
                for a in a_refs:
                    a[...] = jnp.zeros(a.shape, F32)
            for a, s in zip(a_refs, sums):
                a[...] += s

    in_specs = []
    for arr, kind in ins:
        if kind == "row":
            in_specs.append(pl.BlockSpec((tr, arr.shape[1]), lambda i: (i, 0)))
        else:
            in_specs.append(pl.BlockSpec(arr.shape, lambda i: (0, 0)))
    out_specs = [pl.BlockSpec((tr, c), lambda i: (i, 0)) for c, _ in outs]
    out_shape = [jax.ShapeDtypeStruct((L, c), dt) for c, dt in outs]
    out_specs += [pl.BlockSpec((1, c), lambda i: (0, 0)) for c in acc_outs]
    out_shape += [jax.ShapeDtypeStruct((1, c), F32) for c in acc_outs]
    return pl.pallas_call(
        kbody, name=name, grid=(L // tr,), in_specs=in_specs, out_specs=out_specs, out_shape=out_shape,
        compiler_params=_cparams(("arbitrary",)),
    )(*[a for a, _ in ins])


def _rms_fwd(name, x, g):
    def body(xv, gv):
        r = lax.rsqrt(jnp.mean(xv * xv, axis=-1, keepdims=True) + _EPS)
        return [xv * r * gv], []
    return _rowwise(name, body, [(x, "row"), (g, "vec")], [(x.shape[1], _MXU)], x.shape[0])[0]


def _rms_bwd(name, dh, x, g, dres):
    def body(dhv, xv, gv, dr):
        r = lax.rsqrt(jnp.mean(xv * xv, axis=-1, keepdims=True) + _EPS)
        xh = xv * r
        dxh = dhv * gv
        dx = dr + r * (dxh - xh * jnp.mean(dxh * xh, axis=-1, keepdims=True))
        return [dx, dx], [jnp.sum(dhv * xh, axis=0, keepdims=True)]
    D = x.shape[1]
    return _rowwise(name, body, [(dh, "row"), (x, "row"), (g, "vec"), (dres, "row")], [(D, F32), (D, _MXU)],
                    x.shape[0], [D])


def _loss_head(x, g, target):
    D = x.shape[1]

    def body(xv, gv, tv):
        r = lax.rsqrt(jnp.mean(xv * xv, axis=-1, keepdims=True) + _EPS)
        xh = xv * r
        e = xh * gv - tv
        dy = e * (1.0 / D)
        dxh = dy * gv
        dx = r * (dxh - xh * jnp.mean(dxh * xh, axis=-1, keepdims=True))
        row_loss = 0.5 * jnp.mean(e * e, axis=-1, keepdims=True)
        lsum = jnp.sum(row_loss, axis=0, keepdims=True) + jnp.zeros((1, _LANES), F32)
        return [dx], [jnp.sum(dy * xh, axis=0, keepdims=True), lsum]
    return _rowwise("loss_head", body, [(x, "row"), (g, "vec"), (target, "row")], [(D, F32)], x.shape[0], [D, _LANES])


def _ple_bwd_elem(dx, pp, gt):
    def body(dxv, ppv, gtv):
        gt32 = gtv.astype(F32)
        return [dxv * ppv * gt32 * (1.0 - gt32), dxv * gt32], []
    D = dx.shape[1]
    return _rowwise("ple_bwd_elem", body, [(dx, "row"), (pp, "row"), (gt, "row")], [(D, _MXU), (D, _MXU)], dx.shape[0])


def _glu_bwd_elem(dx, val, sg):
    def body(dxv, vv, sv):
        v32, s32 = vv.astype(F32), sv.astype(F32)
        return [jnp.concatenate([dxv * s32, dxv * v32 * s32 * (1.0 - s32)], axis=1)], []
    D = dx.shape[1]
    return _rowwise("glu_bwd_elem", body, [(dx, "row"), (val, "row"), (sg, "row")], [(2 * D, _MXU)], dx.shape[0])[0]


def _gm_common(z, ln_g, ln_b, wc_bf, bsT):
    W = z.shape[1] // 2
    zu, zv = z[:, :W], z[:, W:]
    u, v = _gelu(zu), _gelu(zv)
    mu = jnp.mean(v, axis=-1, keepdims=True)
    vc = v - mu
    rstd = lax.rsqrt(jnp.mean(vc * vc, axis=-1, keepdims=True) + _EPS)
    vh = vc * rstd
    vn = vh * ln_g + ln_b
    vnb = vn.astype(_MXU)
    svs = []
    for h in range(_GM_HEADS):
        sl = slice(h * _LANES, (h + 1) * _LANES)
        svs.append(_dot(wc_bf[h], vnb[:, sl], _NN) + bsT[:, h:h + 1])
    return zu, zv, u, vh, rstd, vnb, svs


def _causal(w):
    t = lax.broadcasted_iota(jnp.int32, w.shape, w.ndim - 2)
    s = lax.broadcasted_iota(jnp.int32, w.shape, w.ndim - 1)
    return jnp.where(s <= t, w, jnp.zeros_like(w))


def _gmlp_fwd(z, ln_g, ln_b, w_s, bsT):
    L, W2 = z.shape
    W = W2 // 2

    def body(z_ref, g_ref, b_ref, ws_ref, bs_ref, m_ref):
        wc = _causal(ws_ref[...]).astype(_MXU)
        _, _, u, _, _, _, svs = _gm_common(z_ref[...], g_ref[...], b_ref[...], wc, bs_ref[...])
        for h in range(_GM_HEADS):
            sl = slice(h * _LANES, (h + 1) * _LANES)
            m_ref[:, sl] = (u[:, sl] * svs[h]).astype(m_ref.dtype)

    return pl.pallas_call(
        body, name="gmlp_fwd", grid=(L // _GM_CHUNK,),
        in_specs=[pl.BlockSpec((_GM_CHUNK, W2), lambda n: (n, 0)),
                  pl.BlockSpec((1, W), lambda n: (0, 0)), pl.BlockSpec((1, W), lambda n: (0, 0)),
                  pl.BlockSpec(w_s.shape, lambda n: (0, 0, 0)), pl.BlockSpec(bsT.shape, lambda n: (0, 0))],
        out_specs=pl.BlockSpec((_GM_CHUNK, W), lambda n: (n, 0)),
        out_shape=jax.ShapeDtypeStruct((L, W), _MXU),
        compiler_params=_cparams(("arbitrary",)),
    )(z, ln_g, ln_b, w_s, bsT)


def _gmlp_bwd(z, dm, ln_g, ln_b, w_s, bsT):
    L, W2 = z.shape
    W = W2 // 2
    T = _GM_CHUNK

    def body(z_ref, dm_ref, g_ref, b_ref, ws_ref, bs_ref, dz_ref, dws_ref, dbs_ref, dg_ref, db_ref):
        @pl.when(pl.program_id(0) == 0)
        def _():
            dws_ref[...] = jnp.zeros(dws_ref.shape, F32)
            dbs_ref[...] = jnp.zeros(dbs_ref.shape, F32)
            dg_ref[...] = jnp.zeros(dg_ref.shape, F32)
            db_ref[...] = jnp.zeros(db_ref.shape, F32)

        wc = _causal(ws_ref[...]).astype(_MXU)
        ln_g_v = g_ref[...]
        zu, zv, u, vh, rstd, vnb, svs = _gm_common(z_ref[...], ln_g_v, b_ref[...], wc, bs_ref[...])
        dmv = dm_ref[...]
        lane = lax.broadcasted_iota(jnp.int32, (T, _LANES), 1)
        dbs = jnp.zeros((T, _LANES), F32)
        dvn_parts = []
        for h in range(_GM_HEADS):
            sl = slice(h * _LANES, (h + 1) * _LANES)
            dsv = dmv[:, sl] * u[:, sl]
            dz_ref[:, sl] = (dmv[:, sl] * svs[h] * _gelu_grad(zu[:, sl])).astype(dz_ref.dtype)
            dbs = dbs + jnp.where(lane == h, jnp.sum(dsv, axis=1, keepdims=True), 0.0)
            dsvb = dsv.astype(_MXU)
            dws_ref[h] += _dot(dsvb, vnb[:, sl], _NT)
            dvn_parts.append(_dot(wc[h], dsvb, _TN))
        dbs_ref[...] += dbs
        dvn = jnp.concatenate(dvn_parts, axis=1)
        dg_ref[...] += jnp.sum(dvn * vh, axis=0, keepdims=True)
        db_ref[...] += jnp.sum(dvn, axis=0, keepdims=True)
        dxh = dvn * ln_g_v
        dv = rstd * (dxh - jnp.mean(dxh, axis=-1, keepdims=True) - vh * jnp.mean(dxh * vh, axis=-1, keepdims=True))
        dz_ref[:, W:] = (dv * _gelu_grad(zv)).astype(dz_ref.dtype)

        @pl.when(pl.program_id(0) == pl.num_programs(0) - 1)
        def _():
            dws_ref[...] = _causal(dws_ref[...])

    return pl.pallas_call(
        body, name="gmlp_bwd", grid=(L // T,),
        in_specs=[pl.BlockSpec((T, W2), lambda n: (n, 0)), pl.BlockSpec((T, W), lambda n: (n, 0)),
                  pl.BlockSpec((1, W), lambda n: (0, 0)), pl.BlockSpec((1, W), lambda n: (0, 0)),
                  pl.BlockSpec(w_s.shape, lambda n: (0, 0, 0)), pl.BlockSpec(bsT.shape, lambda n: (0, 0))],
        out_specs=[pl.BlockSpec((T, W2), lambda n: (n, 0)),
                   pl.BlockSpec(w_s.shape, lambda n: (0, 0, 0)), pl.BlockSpec((T, _LANES), lambda n: (0, 0)),
                   pl.BlockSpec((1, W), lambda n: (0, 0)), pl.BlockSpec((1, W), lambda n: (0, 0))],
        out_shape=[jax.ShapeDtypeStruct((L, W2), _MXU), jax.ShapeDtypeStruct(w_s.shape, F32),
                   jax.ShapeDtypeStruct((T, _LANES), F32),
                   jax.ShapeDtypeStruct((1, W), F32), jax.ShapeDtypeStruct((1, W), F32)],
        compiler_params=_cparams(("arbitrary",)),
    )(z, dm, ln_g, ln_b, w_s, bsT)


def _s5_prep_math(a_re, a_im, log_dt):
    dt = jnp.exp(log_dt)
    xr, xi = a_re * dt, a_im * dt
    e = jnp.exp(xr)
    lbr, lbi = e * jnp.cos(xi), e * jnp.sin(xi)
    dn = a_re * a_re + a_im * a_im
    nr, ni = lbr - 1.0, lbi
    pr, pi = nr * a_re + ni * a_im, ni * a_re - nr * a_im
    return dt, lbr, lbi, dn, nr, ni, pr, pi


def _vm():
    return pl.BlockSpec(memory_space=pltpu.VMEM)


def _s5_prep(a_re, a_im, log_dt, b_re, b_im):
    def body(ar_ref, ai_ref, ld_ref, br_ref, bi_ref, lbr_ref, lbi_ref, Br_ref, Bi_ref):
        _, lbr, lbi, dn, _, _, pr, pi = _s5_prep_math(ar_ref[...], ai_ref[...], ld_ref[...])
        cr, ci = (pr / dn)[:, None, :], (pi / dn)[:, None, :]
        lbr_ref[...] = lbr
        lbi_ref[...] = lbi
        br, bi = br_ref[...], bi_ref[...]
        Br_ref[...] = cr * br - ci * bi
        Bi_ref[...] = cr * bi + ci * br

    sd = jax.ShapeDtypeStruct
    return pl.pallas_call(
        body, name="s5_prep", in_specs=[_vm()] * 5, out_specs=[_vm()] * 4,
        out_shape=[sd(a_re.shape, F32), sd(a_re.shape, F32), sd(b_re.shape, F32), sd(b_re.shape, F32)],
    )(a_re, a_im, log_dt, b_re, b_im)


def _s5_prep_bwd(a_re, a_im, log_dt, b_re, b_im, dlbr_s, dlbi_s, dBr, dBi):
    def body(ar_ref, ai_ref, ld_ref, br_ref, bi_ref, dlr_ref, dli_ref, dBr_ref, dBi_ref,
             dar_ref, dai_ref, dld_ref, dbr_ref, dbi_ref):
        a_re_v, a_im_v = ar_ref[...], ai_ref[...]
        dt, lbr, lbi, dn, nr, ni, pr, pi = _s5_prep_math(a_re_v, a_im_v, ld_ref[...])
        cr, ci = (pr / dn)[:, None, :], (pi / dn)[:, None, :]
        br, bi, dBr_v, dBi_v = br_ref[...], bi_ref[...], dBr_ref[...], dBi_ref[...]
        dbr_ref[...] = cr * dBr_v + ci * dBi_v
        dbi_ref[...] = cr * dBi_v - ci * dBr_v
        dcr = jnp.sum(br * dBr_v + bi * dBi_v, axis=1)
        dci = jnp.sum(br * dBi_v - bi * dBr_v, axis=1)
        dpr, dpi = dcr / dn, dci / dn
        ddn = -(dcr * pr + dci * pi) / (dn * dn)
        dnr = dpr * a_re_v - dpi * a_im_v
        dni = dpr * a_im_v + dpi * a_re_v
        dlbr = dlr_ref[...] + dnr
        dlbi = dli_ref[...] + dni
        dxr = dlbr * lbr + dlbi * lbi
        dxi = dlbi * lbr - dlbr * lbi
        dar_ref[...] = dpr * nr + dpi * ni + 2.0 * ddn * a_re_v + dxr * dt
        dai_ref[...] = dpr * ni - dpi * nr + 2.0 * ddn * a_im_v + dxi * dt
        dld_ref[...] = jnp.sum(dxr * a_re_v + dxi * a_im_v, axis=0, keepdims=True) * dt

    sd = jax.ShapeDtypeStruct
    return pl.pallas_call(
        body, name="s5_prep_bwd", in_specs=[_vm()] * 9, out_specs=[_vm()] * 5,
        out_shape=[sd(a_re.shape, F32), sd(a_re.shape, F32), sd(log_dt.shape, F32),
                   sd(b_re.shape, F32), sd(b_re.shape, F32)],
    )(a_re, a_im, log_dt, b_re, b_im, dlbr_s, dlbi_s, dBr, dBi)


def _shift_rows(v, down):
    n = v.shape[0]
    rolled = pltpu.roll(v, 1 if down else n - 1, 0)
    row = lax.broadcasted_iota(jnp.int32, v.shape, 0)
    return jnp.where(row == (0 if down else n - 1), 0.0, rolled)


def _cmul(ar, ai, br, bi):
    return ar * br - ai * bi, ar * bi + ai * br


_SEG = 8
_UNROLL = 8


def _seg_rows(k):
    if isinstance(k, int):
        return pl.ds(k * _SEG, _SEG)
    return pl.ds(pl.multiple_of(k * _SEG, _SEG), _SEG)


def _unrolled(n, step, init):
    main = n // _UNROLL

    def trip(kk, s):
        for uu in range(_UNROLL):
            s = step(kk * _UNROLL + uu, s)
        return s

    s = lax.fori_loop(0, main, trip, init)
    for r in range(main * _UNROLL, n):
        s = step(r, s)
    return s


def _interleave(src_ref, dst_ref, nk):
    def step(k, carry):
        dst_ref[_seg_rows(k), :] = src_ref[pl.ds(k, _SEG, stride=nk), :]
        return carry
    _unrolled(nk, step, 0)


def _deinterleave(src_ref, dst_ref, nk):
    def step(k, carry):
        dst_ref[pl.ds(k, _SEG, stride=nk), :] = src_ref[_seg_rows(k), :]
        return carry
    _unrolled(nk, step, 0)


def _segment_inits(er, ei, ar, ai, nk, down):
    pr, pi = ar, ai
    for _ in range(int(math.log2(nk))):
        pr, pi = _cmul(pr, pi, pr, pi)
    fr, fi = er, ei
    for _ in range(_SEG - 1):
        sr, si = _shift_rows(fr, down), _shift_rows(fi, down)
        mr, mi = _cmul(pr, pi, sr, si)
        fr, fi = er + mr, ei + mi
    return _shift_rows(fr, down), _shift_rows(fi, down)


def _scan_states(x_re, x_im, ar, ai, nk):
    lanes = ar.shape[1]

    def step(k, s):
        rows = _seg_rows(k)
        mr, mi = _cmul(ar, ai, s[0], s[1])
        return mr + x_re[rows, :], mi + x_im[rows, :]

    zero = jnp.zeros((_SEG, lanes), F32)
    er, ei = _unrolled(nk, step, (zero, zero))
    ir, ii = _segment_inits(er, ei, ar, ai, nk, True)

    def step2(k, s):
        rows = _seg_rows(k)
        mr, mi = _cmul(ar, ai, s[0], s[1])
        nr, ni = mr + x_re[rows, :], mi + x_im[rows, :]
        x_re[rows, :] = nr
        x_im[rows, :] = ni
        return nr, ni

    _unrolled(nk, step2, (ir, ii))


def _s5_tile_fwd(u, bd_re, bd_im, cd_re, cd_im, ar, ai, d, s_re, s_im, nk):
    s_re[...] = _dot(u, bd_re, _NN)
    s_im[...] = _dot(u, bd_im, _NN)
    _scan_states(s_re, s_im, ar, ai, nk)
    return _dot(s_re[...], cd_re, _NN) - _dot(s_im[...], cd_im, _NN) + d * u


def _s5_specs(L, T):
    lanes = _S5_GT * _S5_P
    u_spec = pl.BlockSpec((L, _LANES), lambda t: (0, t))
    bd_spec = pl.BlockSpec((None, _S5_GT, _S5_C, _S5_P), lambda t: (t, 0, 0, 0))
    cd_spec = pl.BlockSpec((None, _S5_GT, _S5_P, _S5_C), lambda t: (t, 0, 0, 0))
    lam_spec = pl.BlockSpec((None, 1, lanes), lambda t: (t, 0, 0))
    d_spec = pl.BlockSpec((1, _LANES), lambda t: (0, t))
    return lanes, u_spec, bd_spec, cd_spec, lam_spec, d_spec


def _fill_block_diag(dst_ref, blocks_ref):
    _, a, b = blocks_ref.shape
    dst_ref[...] = jnp.zeros(dst_ref.shape, F32)
    for g in range(_S5_GT):
        dst_ref[g * a:(g + 1) * a, g * b:(g + 1) * b] = blocks_ref[g]


def _take_block_diag(dst_ref, v):
    _, a, b = dst_ref.shape
    for g in range(_S5_GT):
        dst_ref[g] = v[g * a:(g + 1) * a, g * b:(g + 1) * b]


def _s5_dense(bdr, bdi, cdr, cdi, dense):
    for src, dst in zip((bdr, bdi, cdr, cdi), dense):
        _fill_block_diag(dst, src)
    return [dst[...] for dst in dense]


def _s5_dense_scratch(lanes):
    return [pltpu.VMEM((_LANES, lanes), F32), pltpu.VMEM((_LANES, lanes), F32),
            pltpu.VMEM((lanes, _LANES), F32), pltpu.VMEM((lanes, _LANES), F32)]


def _s5_fwd(u, bd_re, bd_im, cd_re, cd_im, lam_re, lam_im, d):
    L, Wd = u.shape
    T = Wd // _LANES
    nk = L // _SEG
    lanes, u_spec, bd_spec, cd_spec, lam_spec, d_spec = _s5_specs(L, T)

    def body(u_ref, bdr, bdi, cdr, cdi, lr, li, d_ref, g_ref, s_re, s_im, up, tmp, *dense):
        ar = jnp.broadcast_to(lr[...], (_SEG, lanes))
        ai = jnp.broadcast_to(li[...], (_SEG, lanes))
        bd_re_v, bd_im_v, cd_re_v, cd_im_v = _s5_dense(bdr, bdi, cdr, cdi, dense)
        _interleave(u_ref, up, nk)
        y = _s5_tile_fwd(up[...], bd_re_v, bd_im_v, cd_re_v, cd_im_v, ar, ai, d_ref[...], s_re, s_im, nk)
        up[...] = _gelu(y)
        _deinterleave(up, tmp, nk)
        g_ref[...] = tmp[...].astype(g_ref.dtype)

    return pl.pallas_call(
        body, name="s5_fwd", grid=(T,),
        in_specs=[u_spec, bd_spec, bd_spec, cd_spec, cd_spec, lam_spec, lam_spec, d_spec],
        out_specs=u_spec, out_shape=jax.ShapeDtypeStruct((L, Wd), _MXU),
        scratch_shapes=[pltpu.VMEM((L, lanes), F32) for _ in range(2)]
        + [pltpu.VMEM((L, _LANES), F32) for _ in range(2)] + _s5_dense_scratch(lanes),
        compiler_params=_cparams(("arbitrary",)),
    )(u, bd_re, bd_im, cd_re, cd_im, lam_re, lam_im, d)


def _s5_bwd(u, dg, bd_re, bd_im, cd_re, cd_im, lam_re, lam_im, d):
    L, Wd = u.shape
    T = Wd // _LANES
    nk = L // _SEG
    lanes, u_spec, bd_spec, cd_spec, lam_spec, d_spec = _s5_specs(L, T)

    def body(u_ref, dg_ref, bdr, bdi, cdr, cdi, lr, li, d_ref,
             du_ref, dbdr, dbdi, dcdr, dcdi, dlr, dli, dd_ref, s_re, s_im, g_re, g_im, up, dgp, tmp, *dense):
        ar = jnp.broadcast_to(lr[...], (_SEG, lanes))
        ai = jnp.broadcast_to(li[...], (_SEG, lanes))
        bd_re_v, bd_im_v, cd_re_v, cd_im_v = _s5_dense(bdr, bdi, cdr, cdi, dense)
        _interleave(u_ref, up, nk)
        _interleave(dg_ref, dgp, nk)
        uv, dv = up[...], d_ref[...]
        y = _s5_tile_fwd(uv, bd_re_v, bd_im_v, cd_re_v, cd_im_v, ar, ai, dv, s_re, s_im, nk)
        dy = dgp[...] * _gelu_grad(y)
        dd_ref[...] = jnp.sum(dy * uv, axis=0, keepdims=True)
        dyb = dy.astype(_MXU)
        _take_block_diag(dcdr, _dot(dyb, s_re[...], _TN))
        _take_block_diag(dcdi, -_dot(dyb, s_im[...], _TN))
        g_re[...] = _dot(dyb, cd_re_v, _NT)
        g_im[...] = -_dot(dyb, cd_im_v, _NT)

        nai = -ai

        def step(j, s):
            rows = _seg_rows(nk - 1 - j)
            mr, mi = _cmul(ar, nai, s[0], s[1])
            return mr + g_re[rows, :], mi + g_im[rows, :]

        zero = jnp.zeros((_SEG, lanes), F32)
        er, ei = _unrolled(nk, step, (zero, zero))
        ir, ii = _segment_inits(er, ei, ar, nai, nk, False)

        def acc_lam(gr, gi, pr, pi, acc):
            return acc[0] + gr * pr + gi * pi, acc[1] + gi * pr - gr * pi

        def step2(j, carry):
            s, acc = carry
            k = nk - 1 - j
            rows = _seg_rows(k)
            mr, mi = _cmul(ar, nai, s[0], s[1])
            nr, ni = mr + g_re[rows, :], mi + g_im[rows, :]
            g_re[rows, :] = nr
            g_im[rows, :] = ni
            prev = _seg_rows(k - 1)
            return (nr, ni), acc_lam(nr, ni, s_re[prev, :], s_im[prev, :], acc)

        (g0r, g0i), acc = _unrolled(nk - 1, step2, ((ir, ii), (zero, zero)))
        first = _seg_rows(0)
        mr, mi = _cmul(ar, nai, g0r, g0i)
        nr, ni = mr + g_re[first, :], mi + g_im[first, :]
        g_re[first, :] = nr
        g_im[first, :] = ni
        last = _seg_rows(nk - 1)
        acc = acc_lam(nr, ni, _shift_rows(s_re[last, :], True), _shift_rows(s_im[last, :], True), acc)
        dlr[...] = jnp.sum(acc[0], axis=0, keepdims=True)
        dli[...] = jnp.sum(acc[1], axis=0, keepdims=True)

        gtr, gti = g_re[...].astype(_MXU), g_im[...].astype(_MXU)
        ub = uv.astype(_MXU)
        _take_block_diag(dbdr, _dot(ub, gtr, _TN))
        _take_block_diag(dbdi, _dot(ub, gti, _TN))
        dgp[...] = _dot(gtr, bd_re_v, _NT) + _dot(gti, bd_im_v, _NT) + dy * dv
        _deinterleave(dgp, tmp, nk)
        du_ref[...] = tmp[...].astype(du_ref.dtype)

    sd = jax.ShapeDtypeStruct
    big = sd((T, _S5_GT, _S5_C, _S5_P), F32)
    return pl.pallas_call(
        body, name="s5_bwd", grid=(T,),
        in_specs=[u_spec, u_spec, bd_spec, bd_spec, cd_spec, cd_spec, lam_spec, lam_spec, d_spec],
        out_specs=[u_spec, bd_spec, bd_spec, bd_spec, bd_spec, lam_spec, lam_spec, d_spec],
        out_shape=[sd((L, Wd), _MXU), big, big, big, big, sd((T, 1, lanes), F32), sd((T, 1, lanes), F32),
                   sd((1, Wd), F32)],
        scratch_shapes=[pltpu.VMEM((L, lanes), F32) for _ in range(4)]
        + [pltpu.VMEM((L, _LANES), F32) for _ in range(3)] + _s5_dense_scratch(lanes),
        compiler_params=_cparams(("arbitrary",)),
    )(u, dg, bd_re, bd_im, cd_re, cd_im, lam_re, lam_im, d)


def _half_tile(R, few_arrays=False):
    for t in ((512, 704, 128) if few_arrays else (256, 352, 128)):
        if R % t == 0:
            return t
    raise ValueError(R)


def _cast_shard(name, w, layer, kind, R, C):
    tr = _half_tile(R, True)
    nr = R // tr

    def body(w_ref, o_ref):
        o_ref[...] = w_ref[...].astype(o_ref.dtype)

    if kind == "col":
        in_map = lambda h, i: (layer, h * nr + i, 0)
    else:
        in_map = lambda h, i: (layer, i, h)
    return pl.pallas_call(
        body, name=name, grid=(2, nr), in_specs=[pl.BlockSpec((None, tr, C), in_map)],
        out_specs=pl.BlockSpec((None, tr, C), lambda h, i: (h, i, 0)),
        out_shape=jax.ShapeDtypeStruct((2, R, C), _WIRE),
        compiler_params=_cparams(("arbitrary", "arbitrary")),
    )(w)


def _adam_math(w, g, m, v):
    m2 = _B1 * m + (1.0 - _B1) * g
    v2 = _B2 * v + (1.0 - _B2) * (g * g)
    m_hat = m2 / (1.0 - _B1 ** _STEP)
    v_hat = v2 / (1.0 - _B2 ** _STEP)
    delta = -_LR * (m_hat / (jnp.sqrt(v_hat) + _AEPS) + _WD * w)
    return delta, m2, v2


def _adamw_big(name, w, m, v, layer, pair, kind, R, C, c, after, prev):
    tr = _half_tile(R, few_arrays=C <= 1024 and R % 512 == 0)
    nr = R // tr

    def body(c_ref, w_ref, m_ref, v_ref, own_ref, other_ref, *rest):
        go_ref, d_ref, mo_ref, vo_ref = rest[-4:]
        g = jnp.where(pl.program_id(0) == c_ref[0], own_ref[...], other_ref[...])
        delta, m2, v2 = _adam_math(w_ref[...], g, m_ref[...], v_ref[...])
        go_ref[...] = g
        d_ref[...] = delta
        mo_ref[...] = m2
        vo_ref[...] = v2

    if kind == "col":
        nat = pl.BlockSpec((None, tr, C), lambda h, i, c_ref: (layer, h * nr + i, 0))
    else:
        nat = pl.BlockSpec((None, tr, C), lambda h, i, c_ref: (layer, i, h))

    def gspec(own):
        return pl.BlockSpec((tr, C), lambda h, i, c_ref: (jnp.where((h == c_ref[0]) == own, i, 0), 0))

    carried = list(prev) if prev is not None else []
    gs = pltpu.PrefetchScalarGridSpec(
        num_scalar_prefetch=1, grid=(2, nr),
        in_specs=[nat, nat, nat, gspec(True), gspec(False), _any()] + [_any()] * len(carried),
        out_specs=[nat] * 4)
    sd = jax.ShapeDtypeStruct(w.shape, F32)
    return pl.pallas_call(
        body, name=name, grid_spec=gs, out_shape=[sd] * 4,
        input_output_aliases={7 + k: k for k in range(len(carried))},
        compiler_params=_cparams(("arbitrary", "arbitrary")),
    )(c.astype(jnp.int32).reshape(1), w, m, v, pair[0], pair[1], after, *carried)


def _adamw_flat(w, g, m, v):
    rows = w.shape[0]
    tr = rows // 8 if rows % 64 == 0 else rows

    def body(w_ref, g_ref, m_ref, v_ref, d_ref, mo_ref, vo_ref):
        delta, m2, v2 = _adam_math(w_ref[...], g_ref[...], m_ref[...], v_ref[...])
        d_ref[...] = delta
        mo_ref[...] = m2
        vo_ref[...] = v2

    spec = pl.BlockSpec((tr, _LANES), lambda i: (i, 0))
    sd = jax.ShapeDtypeStruct(w.shape, F32)
    return pl.pallas_call(
        body, name="adamw_small", grid=(rows // tr,), in_specs=[spec] * 4, out_specs=[spec] * 3,
        out_shape=[sd] * 3, compiler_params=_cparams(("arbitrary",)),
    )(w, g, m, v)


def _place():
    x, y, c = lax.axis_index("x"), lax.axis_index("y"), lax.axis_index("c")
    chips = [(1 - x, y), (x, 1 - y), (1 - x, 1 - y)]
    return x, y, c, 2 * x + y, chips


def _any():
    return pl.BlockSpec(memory_space=pl.ANY)


def _remote(src, dst, ssem, rsem, dev):
    return pltpu.make_async_remote_copy(src_ref=src, dst_ref=dst, send_sem=ssem, recv_sem=rsem,
                                        device_id=dev, device_id_type=_MESH)


def _allgather(name, shards):
    n = len(shards)

    def body(*refs):
        s_refs, g_refs = refs[:n], refs[n:2 * n]
        send0, recv0, send1, recv1, send2, recv2 = refs[2 * n:]
        x, y, c, q, _ = _place()
        sib, xn, yn = (x, y, 1 - c), (1 - x, y, c), (x, 1 - y, c)
        qx, qy, qd = 2 * (1 - x) + y, 2 * x + (1 - y), 2 * (1 - x) + (1 - y)
        _handshake([sib, xn, yn])
        own = [_remote(s_refs[a], g_refs[a].at[q], send0.at[a], recv0.at[a], sib) for a in range(n)]

        def pieces(a):
            g, half = g_refs[a], s_refs[a].shape[1] // 2
            return [g.at[qx, c], g.at[qy, c], g.at[qd, c, pl.ds(0, half)], g.at[qd, c, pl.ds(half, half)]]

        def relayed(a):
            g, half = g_refs[a], s_refs[a].shape[1] // 2
            return [(g.at[qx, c, pl.ds(0, half)], yn), (g.at[qy, c, pl.ds(half, half)], xn)]

        first = []
        for a in range(n):
            first.append(_remote(s_refs[a].at[c], g_refs[a].at[q, c], send1.at[4 * a], recv1.at[4 * a], xn))
            first.append(_remote(s_refs[a].at[c], g_refs[a].at[q, c], send1.at[4 * a + 1], recv1.at[4 * a + 1], yn))
        for cp in first + own:
            cp.start()
        later = []
        for a in range(n):
            land = pieces(a)
            for j in range(4):
                k = 4 * a + j
                _remote(land[j], land[j], send1.at[k], recv1.at[k], xn).wait_recv()
                if j < 2:
                    src, to = relayed(a)[j]
                    cp = _remote(src, src, send1.at[k + 2], recv1.at[k + 2], to)
                    cp.start()
                    later.append(cp)
                cp = _remote(land[j], land[j], send2.at[k], recv2.at[k], sib)
                cp.start()
                later.append(cp)
        for a in range(n):
            g, half = g_refs[a], s_refs[a].shape[1] // 2
            theirs = [g.at[qx, 1 - c], g.at[qy, 1 - c], g.at[qd, 1 - c, pl.ds(0, half)],
                      g.at[qd, 1 - c, pl.ds(half, half)]]
            for j in range(4):
                _remote(theirs[j], theirs[j], send2.at[4 * a + j], recv2.at[4 * a + j], sib).wait_recv()
        for cp in own:
            cp.wait()
        for cp in first + later:
            cp.wait_send()

    return _sequencer(name, _ID_GATHER, body, shards,
                      [jax.ShapeDtypeStruct((4,) + s.shape, s.dtype) for s in shards], [n, n] + [4 * n] * 4)


def _handshake(peers):
    barrier = pltpu.get_barrier_semaphore()
    for peer in peers:
        pl.semaphore_signal(barrier, inc=1, device_id=peer, device_id_type=_MESH)
    pl.semaphore_wait(barrier, len(peers))


_ID_SIBLING, _ID_CHIPS, _ID_GATHER, _ID_ALL = 1, 2, 3, 4


def _sequencer(name, collective_id, body, ins, out_types, sem_counts):
    mesh = plsc.ScalarSubcoreMesh(axis_name="seq", num_cores=1)
    moved = sum(math.prod(o.shape) * jnp.dtype(o.dtype).itemsize for o in out_types)
    return pl.kernel(
        body, name=name, out_type=out_types, mesh=mesh,
        scratch_types=[pltpu.SemaphoreType.DMA((k,)) for k in sem_counts],
        compiler_params=pltpu.CompilerParams(collective_id=collective_id),
        cost_estimate=pl.CostEstimate(flops=0, transcendentals=0, bytes_accessed=2 * moved,
                                      remote_bytes_transferred=moved),
    )(*ins)


def _swap_halves(name, grads):
    n = len(grads)

    def body(*refs):
        g_refs, t_refs = refs[:n], refs[n:2 * n]
        send, recv = refs[2 * n:]
        x, y, c, _, _ = _place()
        _handshake([(x, y, 1 - c)])
        cps = [_remote(g_refs[a].at[1 - c], t_refs[a], send.at[a], recv.at[a], (x, y, 1 - c)) for a in range(n)]
        for cp in cps:
            cp.start()
        for cp in cps:
            cp.wait()

    return _sequencer(name, _ID_SIBLING, body, grads,
                      [jax.ShapeDtypeStruct(g.shape[1:], g.dtype) for g in grads], [n, n])


def _chip_sum(name, g, t, after):
    _, _, R, C = g.shape
    tr = _half_tile(R, True)

    def body(c_ref, g_ref, t_ref, after_ref, o_ref):
        o_ref[...] = (g_ref[...].astype(F32) + t_ref[...].astype(F32)).astype(o_ref.dtype)

    c = lax.axis_index("c").astype(jnp.int32).reshape(1)
    gs = pltpu.PrefetchScalarGridSpec(
        num_scalar_prefetch=1, grid=(4, R // tr),
        in_specs=[pl.BlockSpec((None, None, tr, C), lambda r, i, c_ref: (c_ref[0], r, i, 0)),
                  pl.BlockSpec((None, tr, C), lambda r, i, c_ref: (r, i, 0)), _any()],
        out_specs=pl.BlockSpec((None, tr, C), lambda r, i, c_ref: (r, i, 0)))
    return pl.pallas_call(
        body, name=name, grid_spec=gs, out_shape=jax.ShapeDtypeStruct((4, R, C), _WIRE),
        compiler_params=_cparams(("arbitrary", "arbitrary")),
    )(c, g, t, after)


def _scatter_parts(name, parts):
    n = len(parts)

    def body(*refs):
        p_refs, t_refs = refs[:n], refs[n:2 * n]
        send, recv = refs[2 * n:]
        x, y, c, q, chips = _place()
        _handshake([(rx, ry, c) for rx, ry in chips])
        cps = []
        for a in range(n):
            for j, (rx, ry) in enumerate(chips):
                k = 3 * a + j
                cps.append(_remote(p_refs[a].at[2 * rx + ry], t_refs[a].at[q], send.at[k], recv.at[k], (rx, ry, c)))
        for cp in cps:
            cp.start()
        for a in range(n):
            for j, (rx, ry) in enumerate(chips):
                k = 3 * a + j
                land = t_refs[a].at[2 * rx + ry]
                _remote(land, land, send.at[k], recv.at[k], (rx, ry, c)).wait_recv()
        for cp in cps:
            cp.wait_send()

    return _sequencer(name, _ID_CHIPS, body, parts,
                      [jax.ShapeDtypeStruct(p.shape, p.dtype) for p in parts], [3 * n, 3 * n])


def _sum_parts(name, p, t, where, after):
    _, R, C = t.shape
    tr = _half_tile(R, True)

    def body(w_ref, p_ref, t0_ref, t1_ref, t2_ref, after_ref, o_ref):
        o_ref[...] = (p_ref[...].astype(F32) + t0_ref[...].astype(F32)
                      + t1_ref[...].astype(F32) + t2_ref[...].astype(F32))

    def part(slot):
        return pl.BlockSpec((None, tr, C), lambda i, w_ref: (w_ref[slot], i, 0))

    gs = pltpu.PrefetchScalarGridSpec(
        num_scalar_prefetch=1, grid=(R // tr,), in_specs=[part(0), part(1), part(2), part(3), _any()],
        out_specs=pl.BlockSpec((tr, C), lambda i, w_ref: (i, 0)))
    return pl.pallas_call(
        body, name=name, grid_spec=gs, out_shape=jax.ShapeDtypeStruct((R, C), F32),
        compiler_params=_cparams(("arbitrary",)),
    )(where, p, t, t, t, after)


def _send_halves(name, halves):
    n = len(halves)

    def body(*refs):
        h_refs, o_refs = refs[:n], refs[n:2 * n]
        send, recv = refs[2 * n:]
        x, y, c, _, _ = _place()
        _handshake([(x, y, 1 - c)])
        cps = [_remote(h_refs[a], o_refs[a], send.at[a], recv.at[a], (x, y, 1 - c)) for a in range(n)]
        for cp in cps:
            cp.start()
        for cp in cps:
            cp.wait()

    return _sequencer(name, _ID_SIBLING, body, halves,
                      [jax.ShapeDtypeStruct(h.shape, h.dtype) for h in halves], [n, n])


class _Order:
    def __init__(self):
        self.tok = None

    def tie(self, x):
        return x if self.tok is None else lax.optimization_barrier((x, self.tok))[0]

    def done(self, outs):
        self.tok = outs[0]
        return outs


class _Reduction:
    def __init__(self, tag, grads):
        self.tag, self.grads = tag, grads
        self.swapped = _swap_halves(f"rs_swap_{tag}", grads)

    def scatter(self, seq):
        self.parts = [seq.done([_chip_sum(f"rs_chipsum_{self.tag}_{a}", g, t, seq.tok)])[0]
                      for a, (g, t) in enumerate(zip(self.grads, self.swapped))]
        self.landed = _scatter_parts(f"rs_scatter_{self.tag}", self.parts)

    def finish(self, seq):
        x, y, c, q, chips = _place()
        where = jnp.stack([q] + [2 * rx + ry for rx, ry in chips]).astype(jnp.int32)
        halves = [seq.done([_sum_parts(f"rs_sum_{self.tag}_{a}", p, t, where, seq.tok)])[0]
                  for a, (p, t) in enumerate(zip(self.parts, self.landed))]
        return list(zip(halves, _send_halves(f"rs_join_{self.tag}", halves)))


def _allreduce_small(name, v):
    _, R, _ = v.shape

    def body(v_ref, o_ref, land, acc, send1, recv1, send2, recv2):
        x, y, c = lax.axis_index("x"), lax.axis_index("y"), lax.axis_index("c")
        me = 4 * x + 2 * y + c
        peers = []
        for k in range(1, 8):
            dx, dy, dc = (k >> 2) & 1, (k >> 1) & 1, k & 1
            px, py, pc = (1 - x if dx else x), (1 - y if dy else y), (1 - c if dc else c)
            peers.append((k, (px, py, pc), 4 * px + 2 * py + pc))
        land[me] = v_ref[me]
        out1 = [_remote(v_ref.at[pid], land.at[me], send1.at[k], recv1.at[k], dev) for k, dev, pid in peers]
        for cp in out1:
            cp.start()
        for k, dev, pid in peers:
            _remote(land.at[pid], land.at[pid], send1.at[k], recv1.at[k], dev).wait_recv()
        total = land[0]
        for j in range(1, 8):
            total = total + land[j]
        acc[...] = total
        o_ref[me] = total
        out2 = [_remote(acc, o_ref.at[me], send2.at[k], recv2.at[k], dev) for k, dev, pid in peers]
        for cp in out2:
            cp.start()
        for k, dev, pid in peers:
            _remote(o_ref.at[pid], o_ref.at[pid], send2.at[k], recv2.at[k], dev).wait_recv()
        for cp in out1 + out2:
            cp.wait_send()

    return pl.pallas_call(
        body, name=name, in_specs=[_vm()], out_specs=_vm(),
        out_shape=jax.ShapeDtypeStruct(v.shape, F32),
        scratch_shapes=[pltpu.VMEM(v.shape, F32), pltpu.VMEM((R, _LANES), F32)]
        + [pltpu.SemaphoreType.DMA((8,)) for _ in range(4)],
        compiler_params=pltpu.CompilerParams(vmem_limit_bytes=_VMEM_LIMIT),
    )(v)


def _all_peers():
    x, y, c = lax.axis_index("x"), lax.axis_index("y"), lax.axis_index("c")
    peers = []
    for k in range(1, 8):
        px, py, pc = (1 - x if k & 4 else x), (1 - y if k & 2 else y), (1 - c if k & 1 else c)
        peers.append((k, (px, py, pc), 4 * px + 2 * py + pc))
    return 4 * x + 2 * y + c, peers


def _exchange_slices(name, v):
    def body(v_ref, land, send, recv):
        me, peers = _all_peers()
        _handshake([dev for _, dev, _ in peers])
        cps = [_remote(v_ref.at[pid], land.at[me], send.at[k], recv.at[k], dev) for k, dev, pid in peers]
        for cp in cps:
            cp.start()
        for k, dev, pid in peers:
            _remote(land.at[pid], land.at[pid], send.at[k], recv.at[k], dev).wait_recv()
        for cp in cps:
            cp.wait_send()

    return _sequencer(name, _ID_ALL, body, [v], [jax.ShapeDtypeStruct(v.shape, v.dtype)], [8, 8])[0]


def _sum_slices(name, v, landed, after):
    _, R, _ = v.shape

    def body(v_ref, land_ref, after_ref, o_ref):
        me, peers = _all_peers()
        acc = v_ref[me]
        for _, _, pid in peers:
            acc = acc + land_ref[pid]
        o_ref[...] = acc

    return pl.pallas_call(
        body, name=name, in_specs=[_vm(), _vm(), _any()], out_specs=_vm(),
        out_shape=jax.ShapeDtypeStruct((R, _LANES), F32),
        compiler_params=pltpu.CompilerParams(vmem_limit_bytes=_VMEM_LIMIT),
    )(v, landed, after)


def _broadcast_slices(name, s):
    def body(s_ref, out, send, recv):
        me, peers = _all_peers()
        _handshake([dev for _, dev, _ in peers])
        cps = [_remote(s_ref, out.at[me], send.at[k], recv.at[k], dev) for k, dev, pid in peers]
        for cp in cps:
            cp.start()
        for k, dev, pid in peers:
            _remote(out.at[pid], out.at[pid], send.at[k], recv.at[k], dev).wait_recv()
        for cp in cps:
            cp.wait_send()

    return _sequencer(name, _ID_ALL, body, [s], [jax.ShapeDtypeStruct((8,) + s.shape, s.dtype)], [8, 8])[0]


_WEIGHT_NAMES = ['norm_mix', 'norm_ffn', 'norm_ple', 'norm_final', 'gm_w_in', 'gm_ln_g', 'gm_ln_b', 'gm_w_s',
                 'gm_b_s', 'gm_w_out', 's5_w_in', 's5_a_re', 's5_a_im', 's5_log_dt', 's5_b_re', 's5_b_im',
                 's5_c_re', 's5_c_im', 's5_d', 's5_w_out', 'ffn_w1', 'ffn_w3', 'ffn_w2', 'ple_w_gate', 'ple_w_proj']
_BIG = {'gm_w_in': 'col', 'gm_w_out': 'row', 's5_w_in': 'row', 's5_w_out': 'col', 'ffn_w1': 'col',
        'ffn_w3': 'col', 'ffn_w2': 'row', 'ple_w_gate': 'row', 'ple_w_proj': 'col'}


_VIEW = {'s5_a_re': (0, 2, 1), 's5_a_im': (0, 2, 1), 's5_b_re': (0, 2, 3, 1), 's5_b_im': (0, 2, 3, 1),
         's5_c_re': (0, 2, 3, 1), 's5_c_im': (0, 2, 3, 1)}


def _to_view(name, a):
    return jnp.transpose(a, _VIEW[name]) if name in _VIEW else a


def _from_view(name, a):
    if name not in _VIEW:
        return a
    perm = _VIEW[name]
    return jnp.transpose(a, [perm.index(i) for i in range(len(perm))])


def _rc(kind, shard_shape):
    rows, cols = shard_shape[-2:]
    return (rows // 2, cols) if kind == "col" else (rows, cols // 2)


def _pack(vecs, rows_multiple):
    flat = jnp.concatenate([a.reshape(-1).astype(F32) for a in vecs])
    unit = rows_multiple * _LANES
    pad = (-flat.shape[0]) % unit
    return jnp.pad(flat, (0, pad)).reshape(-1, _LANES)


def _unpack(buf, shapes):
    flat = buf.reshape(-1)
    out, off = [], 0
    for s in shapes:
        n = math.prod(s)
        out.append(flat[off:off + n].reshape(s))
        off += n
    return out


def _ident(accs, ex):
    return accs


def _add_resid(accs, ex):
    return [accs[0] + ex[0]]


def _swiglu_epi(accs, ex):
    a, b = accs
    return [a, b, a * _sig(a) * b]


def _swiglu_bwd_epi(accs, ex):
    df = accs[0]
    a, b = ex[0].astype(F32), ex[1].astype(F32)
    sa = _sig(a)
    return [df * b * (sa * (1.0 + a * (1.0 - sa))), df * (a * sa)]


def _ple_epi(accs, ex):
    gt = _sig(accs[0])
    return [ex[0] + gt * ex[1], gt]


def _glu_epi(accs, ex):
    val, sg = accs[0], _sig(accs[1])
    return [ex[0] + val * sg, val, sg]


def kernel(x, p, norm_mix, norm_ffn, norm_ple, norm_final, gm_w_in, gm_ln_g, gm_ln_b, gm_w_s, gm_b_s, gm_w_out, s5_w_in, s5_a_re, s5_a_im, s5_log_dt, s5_b_re, s5_b_im, s5_c_re, s5_c_im, s5_d, s5_w_out, ffn_w1, ffn_w3, ffn_w2, ple_w_gate, ple_w_proj, loss_target, m_norm_mix, m_norm_ffn, m_norm_ple, m_norm_final, m_gm_w_in, m_gm_ln_g, m_gm_ln_b, m_gm_w_s, m_gm_b_s, m_gm_w_out, m_s5_w_in, m_s5_a_re, m_s5_a_im, m_s5_log_dt, m_s5_b_re, m_s5_b_im, m_s5_c_re, m_s5_c_im, m_s5_d, m_s5_w_out, m_ffn_w1, m_ffn_w3, m_ffn_w2, m_ple_w_gate, m_ple_w_proj, v_norm_mix, v_norm_ffn, v_norm_ple, v_norm_final, v_gm_w_in, v_gm_ln_g, v_gm_ln_b, v_gm_w_s, v_gm_b_s, v_gm_w_out, v_s5_w_in, v_s5_a_re, v_s5_a_im, v_s5_log_dt, v_s5_b_re, v_s5_b_im, v_s5_c_re, v_s5_c_im, v_s5_d, v_s5_w_out, v_ffn_w1, v_ffn_w3, v_ffn_w2, v_ple_w_gate, v_ple_w_proj):
    env = dict(locals())
    w = {n: env[n] for n in _WEIGHT_NAMES}
    mom = {n: env["m_" + n] for n in _WEIGHT_NAMES}
    var = {n: env["v_" + n] for n in _WEIGHT_NAMES}
    xs, tgt = x[0], loss_target[0]
    L, D = xs.shape
    depth = norm_mix.shape[0]
    qx, qy = lax.axis_index("x"), lax.axis_index("y")
    q = 2 * qx + qy

    def gather(tag, items):
        shards = []
        for name, layer in items:
            kind = _BIG[name]
            R, C = _rc(kind, w[name].shape)
            shards.append(_cast_shard(f"cast_{name}{layer}", w[name], layer, kind, R, C))
        full = _allgather(f"ag_{tag}", shards)
        return {it: _W(f, _BIG[it[0]]) for it, f in zip(items, full)}

    W = {}
    mixers = [["gm_w_in", "gm_w_out"], ["s5_w_in", "s5_w_out"]]
    for i in range(depth):
        for n in mixers[i]:
            W.update(gather(f"{n}", [(n, 0)]))
        W.update(gather(f"ffn_up{i}", [("ffn_w1", i), ("ffn_w3", i)]))
        W.update(gather(f"ffn_down{i}", [("ffn_w2", i)]))
        W.update(gather(f"ple{i}", [("ple_w_gate", i), ("ple_w_proj", i)]))

    d_slots = jnp.zeros((4, D // 4), F32)
    d_slots = lax.dynamic_update_slice(d_slots, s5_d.astype(F32), (q, 0))
    d_sum = _allreduce_small("ar_s5_d", _pack([d_slots], 64).reshape(8, -1, _LANES))
    d_full = (d_sum.reshape(-1)[:D] * 0.5).reshape(1, D)

    def ffn_fwd(i, xin):
        hf = _rms_fwd(f"rms_ffn{i}", xin, norm_ffn[i:i + 1])
        a, b, f = _mm_nn(f"ffn_up{i}", hf, [W["ffn_w1", i], W["ffn_w3", i]], 1024, 1408, ffn_w2.shape[1] * 4,
                         [], [_MXU, _MXU, _MXU], _swiglu_epi, tm=1024)
        xo = _mm_nn(f"ffn_down{i}", f, [W["ffn_w2", i]], 1408, 1024, D, [xin], [F32], _add_resid, tm=1024)[0]
        return xo, (xin, hf, a, b, f)

    def ple_fwd(i, xin):
        hp = _rms_fwd(f"rms_ple{i}", xin, norm_ple[i:i + 1])
        pi = p[i, 0]
        pp = _mm_nn(f"ple_proj{i}", pi, [W["ple_w_proj", i]], 128, 512, D, [], [F32], _ident, tm=2048)[0]
        xo, gt = _mm_nn(f"ple_gate{i}", hp, [W["ple_w_gate", i]], 512, 1024, D, [xin, pp], [F32, _MXU], _ple_epi,
                        tm=1024)
        return xo, (xin, hp, pi, pp, gt)

    h0 = _rms_fwd("rms_mix0", xs, norm_mix[0:1])
    z = _mm_nn("gm_in", h0, [W["gm_w_in", 0]], 1024, 1024, 2 * D, [], [F32], _ident, tm=1024)[0]
    bsT = gm_b_s[0].T
    gm_m = _gmlp_fwd(z, gm_ln_g, gm_ln_b, gm_w_s[0], bsT)
    x1 = _mm_nn("gm_out", gm_m, [W["gm_w_out", 0]], 512, 1024, D, [xs], [F32], _add_resid, tm=1024)[0]
    x2, ffn0 = ffn_fwd(0, x1)
    x3, ple0 = ple_fwd(0, x2)

    T = D // _LANES
    lanes = _S5_GT * _S5_P
    sv = {n: _to_view(n, w[n])[0] for n in _VIEW}
    a_re, a_im, log_dt = sv["s5_a_re"], sv["s5_a_im"], s5_log_dt
    lbr, lbi, Bbar_re, Bbar_im = _s5_prep(a_re, a_im, log_dt, sv["s5_b_re"], sv["s5_b_im"])

    def to_bd(B):
        return jnp.transpose(B.reshape(_S5_P, _S5_C, T, _S5_GT), (2, 3, 1, 0))

    def to_cd(cw):
        return jnp.transpose(cw.reshape(_S5_C, _S5_P, T, _S5_GT), (2, 3, 1, 0))

    def to_lam(v):
        return jnp.transpose(v).reshape(T, 1, lanes)

    bd_re, bd_im = to_bd(Bbar_re), to_bd(Bbar_im)
    cd_re, cd_im = to_cd(sv["s5_c_re"]), to_cd(sv["s5_c_im"])
    lam_re, lam_im = to_lam(lbr), to_lam(lbi)

    h1 = _rms_fwd("rms_mix1", x3, norm_mix[1:2])
    u = _mm_nn("s5_in", h1, [W["s5_w_in", 0]], 512, 1024, D, [], [F32], _ident, tm=1024)[0]
    s5_g = _s5_fwd(u, bd_re, bd_im, cd_re, cd_im, lam_re, lam_im, d_full)
    x4, glu_val, glu_sg = _mm_nn("s5_out", s5_g, [W["s5_w_out", 0], W["s5_w_out", 0]], 1024, 1024, D, [x3],
                                 [F32, _MXU, _MXU], _glu_epi, tm=1024, cb_offsets=[0, 2])
    x5, ffn1 = ffn_fwd(1, x4)
    x6, ple1 = ple_fwd(1, x5)

    dx, d_norm_final, loss_rows = _loss_head(x6, norm_final[None], tgt)
    loss = lax.psum(loss_rows[0, 0], ("x", "y", "c"))

    small = {}

    seq = _Order()
    tie, done = seq.tie, seq.done
    d_norm_ple, d_norm_ffn, d_norm_mix = [None] * depth, [None] * depth, [None] * depth
    reduced = {}

    def keep(names, layer, pairs):
        for n, pr in zip(names, pairs):
            reduced[n, layer] = pr

    ple_names, up_names, down_names = ["ple_w_gate", "ple_w_proj"], ["ffn_w1", "ffn_w3"], ["ffn_w2"]

    def ple_bwd(i, dxo, saved):
        xin, hp, pi, pp, gt = saved
        dpre, dpp = done(_ple_bwd_elem(tie(dxo), pp, gt))
        dwg = done(_mm_tn(f"ple_gate_dw{i}", tie(hp), [dpre], "row", 512, 1024, 512, 1024))[0]
        dwp = done(_mm_tn(f"ple_proj_dw{i}", tie(pi), [dpp], "col", 128, 512, 128, 512))[0]
        red = _Reduction(f"ple{i}", [dwg, dwp])
        dhp = done(_mm_nt(f"ple_gate_dx{i}", [tie(dpre)], [W["ple_w_gate", i]], 512, 1024, [], [F32], _ident,
                          tm=2048))[0]
        dxin, dxin_mxu, dg = done(_rms_bwd(f"rms_ple_bwd{i}", tie(dhp), xin, norm_ple[i:i + 1], dxo))
        return dxin, dxin_mxu, dg, red

    def ffn_bwd(i, dxo, dxo_mxu, saved, before_up):
        xin, hf, a, b, f = saved
        dw2 = done(_mm_tn(f"ffn_down_dw{i}", tie(f), [dxo_mxu], "row", 1408, 1024, 1408, 1024))[0]
        r_down = _Reduction(f"ffd{i}", [dw2])
        da, db = done(_mm_nt(f"ffn_down_dx{i}", [tie(dxo_mxu)], [W["ffn_w2", i]], 1408, 1024, [a, b], [_MXU, _MXU],
                             _swiglu_bwd_epi, tm=1024))
        for step in before_up:
            step()
        r_down.scatter(seq)
        dw1, dw3 = done(_mm_tn(f"ffn_up_dw{i}", tie(hf), [da, db], "col", 1024, 1408, 1024, 1408))
        r_up = _Reduction(f"ffu{i}", [dw1, dw3])
        dhf = done(_mm_nt(f"ffn_up_dx{i}", [tie(da), db], [W["ffn_w1", i], W["ffn_w3", i]], 1024, 1408, [], [F32],
                          _ident, tm=1024))[0]
        dxin, dxin_mxu, dg = done(_rms_bwd(f"rms_ffn_bwd{i}", tie(dhf), xin, norm_ffn[i:i + 1], dxo))
        r_up.scatter(seq)
        return dxin, dxin_mxu, dg, r_down, r_up

    dx, dx_mxu, d_norm_ple[1], r_ple1 = ple_bwd(1, dx, ple1)
    dx, _, d_norm_ffn[1], r_down1, r_up1 = ffn_bwd(1, dx, dx_mxu, ffn1, [lambda: r_ple1.scatter(seq)])

    do = done([_glu_bwd_elem(tie(dx), glu_val, glu_sg)])[0]
    dw_s5out = done(_mm_tn("s5_out_dw", tie(s5_g), [do], "col", 1024, 1024, 1024, 1024))[0]
    r_s5out = _Reduction("s5out", [dw_s5out])
    dgy = done(_mm_nt("s5_out_dx", [tie(do)], [W["s5_w_out", 0]], 1024, 1024, [], [F32], _ident, tm=1024))[0]
    keep(ple_names, 1, r_ple1.finish(seq))
    keep(down_names, 1, r_down1.finish(seq))
    du, dbd_re, dbd_im, dcd_re, dcd_im, dl_re, dl_im, dd = done(_s5_bwd(tie(u), dgy, bd_re, bd_im, cd_re, cd_im,
                                                                       lam_re, lam_im, d_full))
    r_s5out.scatter(seq)
    dw_s5in = done(_mm_tn("s5_in_dw", tie(h1), [du], "row", 512, 1024, 512, 1024))[0]
    r_s5in = _Reduction("s5in", [dw_s5in])
    dh1 = done(_mm_nt("s5_in_dx", [tie(du)], [W["s5_w_in", 0]], 512, 1024, [], [F32], _ident, tm=2048))[0]
    dx, _, d_norm_mix[1] = done(_rms_bwd("rms_mix1_bwd", tie(dh1), x3, norm_mix[1:2], dx))
    keep(up_names, 1, r_up1.finish(seq))
    r_s5in.scatter(seq)

    def from_bd(t):
        return jnp.transpose(t, (3, 2, 0, 1)).reshape(_S5_P, _S5_C, T * _S5_GT)

    def from_cdT(t):
        return jnp.transpose(t, (2, 3, 0, 1)).reshape(_S5_C, _S5_P, T * _S5_GT)

    def from_lam(t):
        return jnp.transpose(t.reshape(T * _S5_GT, _S5_P))

    da_re, da_im, dlog_dt, db_re, db_im = _s5_prep_bwd(
        a_re, a_im, log_dt, sv["s5_b_re"], sv["s5_b_im"], from_lam(dl_re), from_lam(dl_im),
        from_bd(dbd_re), from_bd(dbd_im))
    small["s5_a_re"], small["s5_a_im"], small["s5_log_dt"] = da_re[None], da_im[None], dlog_dt
    small["s5_b_re"], small["s5_b_im"] = db_re[None], db_im[None]
    small["s5_c_re"], small["s5_c_im"] = from_cdT(dcd_re)[None], from_cdT(dcd_im)[None]

    dx, dx_mxu, d_norm_ple[0], r_ple0 = ple_bwd(0, dx, ple0)
    keep(["s5_w_out"], 0, r_s5out.finish(seq))
    xin0, hf0, a0, b0, f0 = ffn0
    da0, db0 = done(_mm_nt("ffn_down_dx0", [tie(dx_mxu)], [W["ffn_w2", 0]], 1408, 1024, [a0, b0], [_MXU, _MXU],
                           _swiglu_bwd_epi, tm=1024))
    r_ple0.scatter(seq)
    dw1, dw3 = done(_mm_tn("ffn_up_dw0", tie(hf0), [da0, db0], "col", 1024, 1408, 1024, 1408))
    r_up0 = _Reduction("ffu0", [dw1, dw3])
    keep(["s5_w_in"], 0, r_s5in.finish(seq))
    dw2 = done(_mm_tn("ffn_down_dw0", tie(f0), [dx_mxu], "row", 1408, 1024, 1408, 1024))[0]
    r_down0 = _Reduction("ffd0", [dw2])
    r_up0.scatter(seq)
    dhf0 = done(_mm_nt("ffn_up_dx0", [tie(da0), db0], [W["ffn_w1", 0], W["ffn_w3", 0]], 1024, 1408, [], [F32], _ident,
                       tm=1024))[0]
    dx, dx_mxu, d_norm_ffn[0] = done(_rms_bwd("rms_ffn_bwd0", tie(dhf0), xin0, norm_ffn[0:1], dx))
    keep(ple_names, 0, r_ple0.finish(seq))
    r_down0.scatter(seq)

    dw_gmout = done(_mm_tn("gm_out_dw", tie(gm_m), [dx_mxu], "row", 512, 1024, 512, 1024))[0]
    r_gmout = _Reduction("gmout", [dw_gmout])
    dgm = done(_mm_nt("gm_out_dx", [tie(dx_mxu)], [W["gm_w_out", 0]], 512, 1024, [], [F32], _ident, tm=2048))[0]
    dz, dws, dbsT, dlng, dlnb = done(_gmlp_bwd(tie(z), dgm, gm_ln_g, gm_ln_b, gm_w_s[0], bsT))
    dw_gmin = done(_mm_tn("gm_in_dw", tie(h0), [dz], "col", 1024, 1024, 1024, 1024))[0]
    r_gmin = _Reduction("gmin", [dw_gmin])
    dh0 = done(_mm_nt("gm_in_dx", [tie(dz)], [W["gm_w_in", 0]], 1024, 1024, [], [F32], _ident, tm=1024))[0]
    dx, _, d_norm_mix[0] = done(_rms_bwd("rms_mix0_bwd", tie(dh0), xs, norm_mix[0:1], dx))
    grad_x = dx[None]

    small["norm_mix"], small["norm_ffn"] = jnp.concatenate(d_norm_mix), jnp.concatenate(d_norm_ffn)
    small["norm_ple"], small["norm_final"] = jnp.concatenate(d_norm_ple), d_norm_final[0]
    small["gm_ln_g"], small["gm_ln_b"], small["gm_w_s"] = dlng, dlnb, dws[None]
    small["gm_b_s"] = dbsT[:, :_GM_HEADS].T[None]
    small["s5_d"] = dd

    small_names = [n for n in _WEIGHT_NAMES if n not in _BIG]
    packed = _pack([small[n] for n in small_names], 64).reshape(8, -1, _LANES)
    grads, deltas, new_m, new_v = {}, {}, {}, {}
    my_c = lax.axis_index("c")

    def adamw(n, layer, prev):
        kind = _BIG[n]
        R, C = _rc(kind, w[n].shape)
        return done(_adamw_big(f"adamw_{n}{layer}", w[n], mom[n], var[n], layer, reduced[n, layer], kind, R, C,
                               my_c, seq.tok, prev))

    def adamw_last(names):
        for n in names:
            grads[n], deltas[n], new_m[n], new_v[n] = adamw(n, 0, late.get(n))

    late = {}
    for n in down_names + up_names + ple_names:
        late[n] = adamw(n, 1, None)
    keep(up_names, 0, r_up0.finish(seq))
    r_gmout.scatter(seq)
    r_gmin.scatter(seq)
    landed = _exchange_slices("ar_small_in", packed)
    adamw_last(["s5_w_in", "s5_w_out"] + ple_names)
    keep(down_names, 0, r_down0.finish(seq))
    adamw_last(up_names)
    keep(["gm_w_out"], 0, r_gmout.finish(seq))
    mine = done([_sum_slices("ar_small_sum", packed, landed, seq.tok)])[0]
    spread = _broadcast_slices("ar_small_out", mine)
    keep(["gm_w_in"], 0, r_gmin.finish(seq))
    adamw_last(down_names + ["gm_w_out", "gm_w_in"])
    summed = lax.dynamic_update_slice(spread, mine[None], (4 * qx + 2 * qy + my_c, 0, 0))
    red_small = dict(zip(small_names, _unpack(summed, [small[n].shape for n in small_names])))
    red_small["s5_d"] = lax.dynamic_slice(red_small["s5_d"], (0, q * (D // 4)), (1, D // 4))

    def views(src):
        return [_to_view(n, src[n]) for n in small_names]

    shapes = [v_.shape for v_ in views(w)]
    dl, mo, vo = _adamw_flat(_pack(views(w), 64), _pack([red_small[n] for n in small_names], 64),
                             _pack(views(mom), 64), _pack(views(var), 64))
    for n, g_, d_, m_, v_ in zip(small_names, [red_small[n] for n in small_names], _unpack(dl, shapes),
                                 _unpack(mo, shapes), _unpack(vo, shapes)):
        grads[n], deltas[n], new_m[n], new_v[n] = [_from_view(n, t_).reshape(w[n].shape) for t_ in (g_, d_, m_, v_)]

    return (loss, grad_x, *[grads[n] for n in _WEIGHT_NAMES], *[deltas[n] for n in _WEIGHT_NAMES],
            *[new_m[n] for n in _WEIGHT_NAMES], *[new_v[n] for n in _WEIGHT_NAMES])
```

```python
import functools
import math

import jax
import jax.numpy as jnp
from jax import lax
from jax.experimental import pallas as pl
from jax.experimental.pallas import tpu as pltpu
from jax.experimental.pallas import tpu_sc as plsc

F32 = jnp.float32
_MXU = jnp.bfloat16
_WIRE = jnp.bfloat16
_EPS = 1e-6
_VMEM_LIMIT = 56 * 1024 * 1024
_LANES = 128
_MESH = pl.DeviceIdType.MESH

_LR, _B1, _B2, _AEPS, _WD, _STEP = 0.001, 0.9, 0.999, 1e-08, 0.01, 10

_GM_CHUNK = 128
_GM_HEADS = 16
_S5_GT = 8
_S5_P = 64
_S5_C = 16

_NN = (((1,), (0,)), ((), ()))
_NT = (((1,), (1,)), ((), ()))
_TN = (((0,), (0,)), ((), ()))


def _cparams(sem):
    return pltpu.CompilerParams(dimension_semantics=sem, vmem_limit_bytes=_VMEM_LIMIT)


def _sig(x):
    return 1.0 / (1.0 + jnp.exp(-x))


_GC = math.sqrt(2.0 / math.pi)


def _gelu(x):
    return 0.5 * x * (1.0 + jnp.tanh(_GC * (x + 0.044715 * (x * x * x))))


def _gelu_grad(x):
    t = jnp.tanh(_GC * (x + 0.044715 * (x * x * x)))
    return 0.5 * (1.0 + t) + 0.5 * x * (1.0 - t * t) * (_GC * (1.0 + 3.0 * 0.044715 * x * x))


def _dot(a, b, dn):
    return lax.dot_general(a.astype(_MXU), b.astype(_MXU), dn, preferred_element_type=F32)


class _W:
    def __init__(self, arr, kind):
        self.a, self.kind = arr, kind
        self.R, self.C = arr.shape[2], arr.shape[3]

    def full_shape(self):
        return (2 * self.R, 4 * self.C) if self.kind == "col" else (4 * self.R, 2 * self.C)


def _part_index(kind, R, C, tr, tc, rb, cb):
    nr, nc = R // tr, C // tc
    if kind == "col":
        return cb // nc, rb // nr, rb % nr, cb % nc
    return rb // nr, cb // nc, rb % nr, cb % nc


def _wspec(w, tr, tc, rb_fn, cb_fn):
    assert w.R % tr == 0 and w.C % tc == 0, (w.R, w.C, tr, tc)

    def imap(i, j, k):
        return _part_index(w.kind, w.R, w.C, tr, tc, rb_fn(i, j, k), cb_fn(i, j, k))

    return pl.BlockSpec((None, None, tr, tc), imap)


def _gspec(kind, R, C, tr, tc):
    assert R % tr == 0 and C % tc == 0, (R, C, tr, tc)

    def imap(i, j, k):
        part, half, rbi, cbi = _part_index(kind, R, C, tr, tc, i, j)
        return half, part, rbi, cbi

    return pl.BlockSpec((None, None, tr, tc), imap)


def _mm(name, grid, a_ops, b_ops, pairs, acc_shape, n_acc, extras, outs, epilogue):
    nk = grid[2]
    na, nb, ne, no = len(a_ops), len(b_ops), len(extras), len(outs)

    def body(*refs):
        a_refs = refs[:na]
        b_refs = refs[na:na + nb]
        e_refs = refs[na + nb:na + nb + ne]
        o_refs = refs[na + nb + ne:na + nb + ne + no]
        acc_refs = refs[na + nb + ne + no:]
        k = pl.program_id(2)

        def products():
            sums = [None] * n_acc
            for ai, bi, ci, dn in pairs:
                d = _dot(a_refs[ai][...], b_refs[bi][...], dn)
                sums[ci] = d if sums[ci] is None else sums[ci] + d
            return sums

        def finish(accs):
            res = epilogue(accs, [e[...] for e in e_refs])
            for o, r in zip(o_refs, res):
                o[...] = r.astype(o.dtype)

        if nk == 1:
            finish(products())
            return

        @pl.when(k == 0)
        def _():
            for acc, d in zip(acc_refs, products()):
                acc[...] = d

        @pl.when(jnp.logical_and(k > 0, k < nk - 1))
        def _():
            for acc, d in zip(acc_refs, products()):
                acc[...] += d

        @pl.when(k == nk - 1)
        def _():
            finish([acc[...] + d for acc, d in zip(acc_refs, products())])

    ops = list(a_ops) + list(b_ops) + list(extras)
    return pl.pallas_call(
        body, name=name, grid=grid,
        in_specs=[s for _, s in ops],
        out_specs=[s for _, s in outs],
        out_shape=[s for s, _ in outs],
        scratch_shapes=[pltpu.VMEM(acc_shape, F32) for _ in range(n_acc if nk > 1 else 0)],
        compiler_params=_cparams(("parallel", "parallel", "arbitrary")),
    )(*[a for a, _ in ops])


def _bs(shape, fn):
    return pl.BlockSpec(shape, fn)


def _tile_m(L):
    return min(L, 512)


def _mm_nn(name, x, ws, tk, tn, n_out, extras, outs_sd, epilogue, tm=None, cb_offsets=None):
    M, K = x.shape
    tm = min(M, tm or _tile_m(M))
    grid = (M // tm, n_out // tn, K // tk)
    a_ops = [(x, _bs((tm, tk), lambda i, j, k: (i, k)))]
    cb_offsets = cb_offsets or [0] * len(ws)
    b_ops = [(w.a, _wspec(w, tk, tn, lambda i, j, k: k, (lambda off: lambda i, j, k: j + off)(off)))
             for w, off in zip(ws, cb_offsets)]
    pairs = [(0, bi, bi, _NN) for bi in range(len(ws))]
    mn = _bs((tm, tn), lambda i, j, k: (i, j))
    ex = [(e, mn) for e in extras]
    outs = [(jax.ShapeDtypeStruct((M, n_out), dt), mn) for dt in outs_sd]
    return _mm(name, grid, a_ops, b_ops, pairs, (tm, tn), len(ws), ex, outs, epilogue)


def _mm_nt(name, xs, ws, tn, tk, extras, outs_sd, epilogue, tm=None):
    M, Nw = xs[0].shape
    Kw = ws[0].full_shape()[0]
    tm = min(M, tm or _tile_m(M))
    grid = (M // tm, Kw // tn, Nw // tk)
    a_ops = [(x, _bs((tm, tk), lambda i, j, k: (i, k))) for x in xs]
    b_ops = [(w.a, _wspec(w, tn, tk, lambda i, j, k: j, lambda i, j, k: k)) for w in ws]
    pairs = [(i, i, 0, _NT) for i in range(len(ws))]
    mn = _bs((tm, tn), lambda i, j, k: (i, j))
    ex = [(e, mn) for e in extras]
    outs = [(jax.ShapeDtypeStruct((M, Kw), dt), mn) for dt in outs_sd]
    return _mm(name, grid, a_ops, b_ops, pairs, (tm, tn), 1, ex, outs, epilogue)


def _mm_tn(name, x, dys, kind, R, C, tm, tn, tk=None):
    L, Kw = x.shape
    Nw = dys[0].shape[1]
    tk = tk or min(L, 1024)
    grid = (Kw // tm, Nw // tn, L // tk)
    a_ops = [(x, _bs((tk, tm), lambda i, j, k: (k, i)))]
    b_ops = [(dy, _bs((tk, tn), lambda i, j, k: (k, j))) for dy in dys]
    pairs = [(0, bi, bi, _TN) for bi in range(len(dys))]
    gs = _gspec(kind, R, C, tm, tn)
    outs = [(jax.ShapeDtypeStruct((2, 4, R, C), _WIRE), gs) for _ in dys]
    return _mm(name, grid, a_ops, b_ops, pairs, (tm, tn), len(dys), [], outs, lambda accs, ex: accs)


def _row_tile(L):
    return min(L, 256)


def _rowwise(name, body, ins, outs, L, acc_outs=()):
    tr = _row_tile(L)
    n_in, n_out = len(ins), len(outs)

    def kbody(*refs):
        i_refs, o_refs, a_refs = refs[:n_in], refs[n_in:n_in + n_out], refs[n_in + n_out:]
        res, sums = body(*[r[...] for r in i_refs])
        for o, r in zip(o_refs, res):
            o[...] = r.astype(o.dtype)
        if a_refs:
            @pl.when(pl.program_id(0) == 0)
            def _():
                for a in a_refs:
                    a[...] = jnp.zeros(a.shape, F32)
            for a, s in zip(a_refs, sums):
                a[...] += s

    in_specs = []
    for arr, kind in ins:
        if kind == "row":
            in_specs.append(pl.BlockSpec((tr, arr.shape[1]), lambda i: (i, 0)))
        else:
            in_specs.append(pl.BlockSpec(arr.shape, lambda i: (0, 0)))
    out_specs = [pl.BlockSpec((tr, c), lambda i: (i, 0)) for c, _ in outs]
    out_shape = [jax.ShapeDtypeStruct((L, c), dt) for c, dt in outs]
    out_specs += [pl.BlockSpec((1, c), lambda i: (0, 0)) for c in acc_outs]
    out_shape += [jax.ShapeDtypeStruct((1, c), F32) for c in acc_outs]
    return pl.pallas_call(
        kbody, name=name, grid=(L // tr,), in_specs=in_specs, out_specs=out_specs, out_shape=out_shape,
        compiler_params=_cparams(("arbitrary",)),
    )(*[a for a, _ in ins])


def _rms_fwd(name, x, g):
    def body(xv, gv):
        r = lax.rsqrt(jnp.mean(xv * xv, axis=-1, keepdims=True) + _EPS)
        return [xv * r * gv], []
    return _rowwise(name, body, [(x, "row"), (g, "vec")], [(x.shape[1], _MXU)], x.shape[0])[0]


def _rms_bwd(name, dh, x, g, dres):
    def body(dhv, xv, gv, dr):
        r = lax.rsqrt(jnp.mean(xv * xv, axis=-1, keepdims=True) + _EPS)
        xh = xv * r
        dxh = dhv * gv
        dx = dr + r * (dxh - xh * jnp.mean(dxh * xh, axis=-1, keepdims=True))
        return [dx, dx], [jnp.sum(dhv * xh, axis=0, keepdims=True)]
    D = x.shape[1]
    return _rowwise(name, body, [(dh, "row"), (x, "row"), (g, "vec"), (dres, "row")], [(D, F32), (D, _MXU)],
                    x.shape[0], [D])


def _loss_head(x, g, target):
    D = x.shape[1]

    def body(xv, gv, tv):
        r = lax.rsqrt(jnp.mean(xv * xv, axis=-1, keepdims=True) + _EPS)
        xh = xv * r
        e = xh * gv - tv
        dy = e * (1.0 / D)
        dxh = dy * gv
        dx = r * (dxh - xh * jnp.mean(dxh * xh, axis=-1, keepdims=True))
        row_loss = 0.5 * jnp.mean(e * e, axis=-1, keepdims=True)
        lsum = jnp.sum(row_loss, axis=0, keepdims=True) + jnp.zeros((1, _LANES), F32)
        return [dx], [jnp.sum(dy * xh, axis=0, keepdims=True), lsum]
    return _rowwise("loss_head", body, [(x, "row"), (g, "vec"), (target, "row")], [(D, F32)], x.shape[0], [D, _LANES])


def _ple_bwd_elem(dx, pp, gt):
    def body(dxv, ppv, gtv):
        gt32 = gtv.astype(F32)
        return [dxv * ppv * gt32 * (1.0 - gt32), dxv * gt32], []
    D = dx.shape[1]
    return _rowwise("ple_bwd_elem", body, [(dx, "row"), (pp, "row"), (gt, "row")], [(D, _MXU), (D, _MXU)], dx.shape[0])


def _glu_bwd_elem(dx, val, sg):
    def body(dxv, vv, sv):
        v32, s32 = vv.astype(F32), sv.astype(F32)
        return [jnp.concatenate([dxv * s32, dxv * v32 * s32 * (1.0 - s32)], axis=1)], []
    D = dx.shape[1]
    return _rowwise("glu_bwd_elem", body, [(dx, "row"), (val, "row"), (sg, "row")], [(2 * D, _MXU)], dx.shape[0])[0]


def _gm_common(z, ln_g, ln_b, wc_bf, bsT):
    W = z.shape[1] // 2
    zu, zv = z[:, :W], z[:, W:]
    u, v = _gelu(zu), _gelu(zv)
    mu = jnp.mean(v, axis=-1, keepdims=True)
    vc = v - mu
    rstd = lax.rsqrt(jnp.mean(vc * vc, axis=-1, keepdims=True) + _EPS)
    vh = vc * rstd
    vn = vh * ln_g + ln_b
    vnb = vn.astype(_MXU)
    svs = []
    for h in range(_GM_HEADS):
        sl = slice(h * _LANES, (h + 1) * _LANES)
        svs.append(_dot(wc_bf[h], vnb[:, sl], _NN) + bsT[:, h:h + 1])
    return zu, zv, u, vh, rstd, vnb, svs


def _causal(w):
    t = lax.broadcasted_iota(jnp.int32, w.shape, w.ndim - 2)
    s = lax.broadcasted_iota(jnp.int32, w.shape, w.ndim - 1)
    return jnp.where(s <= t, w, jnp.zeros_like(w))


def _gmlp_fwd(z, ln_g, ln_b, w_s, bsT):
    L, W2 = z.shape
    W = W2 // 2

    def body(z_ref, g_ref, b_ref, ws_ref, bs_ref, m_ref):
        wc = _causal(ws_ref[...]).astype(_MXU)
        _, _, u, _, _, _, svs = _gm_common(z_ref[...], g_ref[...], b_ref[...], wc, bs_ref[...])
        for h in range(_GM_HEADS):
            sl = slice(h * _LANES, (h + 1) * _LANES)
            m_ref[:, sl] = (u[:, sl] * svs[h]).astype(m_ref.dtype)

    return pl.pallas_call(
        body, name="gmlp_fwd", grid=(L // _GM_CHUNK,),
        in_specs=[pl.BlockSpec((_GM_CHUNK, W2), lambda n: (n, 0)),
                  pl.BlockSpec((1, W), lambda n: (0, 0)), pl.BlockSpec((1, W), lambda n: (0, 0)),
                  pl.BlockSpec(w_s.shape, lambda n: (0, 0, 0)), pl.BlockSpec(bsT.shape, lambda n: (0, 0))],
        out_specs=pl.BlockSpec((_GM_CHUNK, W), lambda n: (n, 0)),
        out_shape=jax.ShapeDtypeStruct((L, W), _MXU),
        compiler_params=_cparams(("arbitrary",)),
    )(z, ln_g, ln_b, w_s, bsT)


def _gmlp_bwd(z, dm, ln_g, ln_b, w_s, bsT):
    L, W2 = z.shape
    W = W2 // 2
    T = _GM_CHUNK

    def body(z_ref, dm_ref, g_ref, b_ref, ws_ref, bs_ref, dz_ref, dws_ref, dbs_ref, dg_ref, db_ref):
        @pl.when(pl.program_id(0) == 0)
        def _():
            dws_ref[...] = jnp.zeros(dws_ref.shape, F32)
            dbs_ref[...] = jnp.zeros(dbs_ref.shape, F32)
            dg_ref[...] = jnp.zeros(dg_ref.shape, F32)
            db_ref[...] = jnp.zeros(db_ref.shape, F32)

        wc = _causal(ws_ref[...]).astype(_MXU)
        ln_g_v = g_ref[...]
        zu, zv, u, vh, rstd, vnb, svs = _gm_common(z_ref[...], ln_g_v, b_ref[...], wc, bs_ref[...])
        dmv = dm_ref[...]
        lane = lax.broadcasted_iota(jnp.int32, (T, _LANES), 1)
        dbs = jnp.zeros((T, _LANES), F32)
        dvn_parts = []
        for h in range(_GM_HEADS):
            sl = slice(h * _LANES, (h + 1) * _LANES)
            dsv = dmv[:, sl] * u[:, sl]
            dz_ref[:, sl] = (dmv[:, sl] * svs[h] * _gelu_grad(zu[:, sl])).astype(dz_ref.dtype)
            dbs = dbs + jnp.where(lane == h, jnp.sum(dsv, axis=1, keepdims=True), 0.0)
            dsvb = dsv.astype(_MXU)
            dws_ref[h] += _dot(dsvb, vnb[:, sl], _NT)
            dvn_parts.append(_dot(wc[h], dsvb, _TN))
        dbs_ref[...] += dbs
        dvn = jnp.concatenate(dvn_parts, axis=1)
        dg_ref[...] += jnp.sum(dvn * vh, axis=0, keepdims=True)
        db_ref[...] += jnp.sum(dvn, axis=0, keepdims=True)
        dxh = dvn * ln_g_v
        dv = rstd * (dxh - jnp.mean(dxh, axis=-1, keepdims=True) - vh * jnp.mean(dxh * vh, axis=-1, keepdims=True))
        dz_ref[:, W:] = (dv * _gelu_grad(zv)).astype(dz_ref.dtype)

        @pl.when(pl.program_id(0) == pl.num_programs(0) - 1)
        def _():
            dws_ref[...] = _causal(dws_ref[...])

    return pl.pallas_call(
        body, name="gmlp_bwd", grid=(L // T,),
        in_specs=[pl.BlockSpec((T, W2), lambda n: (n, 0)), pl.BlockSpec((T, W), lambda n: (n, 0)),
                  pl.BlockSpec((1, W), lambda n: (0, 0)), pl.BlockSpec((1, W), lambda n: (0, 0)),
                  pl.BlockSpec(w_s.shape, lambda n: (0, 0, 0)), pl.BlockSpec(bsT.shape, lambda n: (0, 0))],
        out_specs=[pl.BlockSpec((T, W2), lambda n: (n, 0)),
                   pl.BlockSpec(w_s.shape, lambda n: (0, 0, 0)), pl.BlockSpec((T, _LANES), lambda n: (0, 0)),
                   pl.BlockSpec((1, W), lambda n: (0, 0)), pl.BlockSpec((1, W), lambda n: (0, 0))],
        out_shape=[jax.ShapeDtypeStruct((L, W2), _MXU), jax.ShapeDtypeStruct(w_s.shape, F32),
                   jax.ShapeDtypeStruct((T, _LANES), F32),
                   jax.ShapeDtypeStruct((1, W), F32), jax.ShapeDtypeStruct((1, W), F32)],
        compiler_params=_cparams(("arbitrary",)),
    )(z, dm, ln_g, ln_b, w_s, bsT)


def _s5_prep_math(a_re, a_im, log_dt):
    dt = jnp.exp(log_dt)
    xr, xi = a_re * dt, a_im * dt
    e = jnp.exp(xr)
    lbr, lbi = e * jnp.cos(xi), e * jnp.sin(xi)
    dn = a_re * a_re + a_im * a_im
    nr, ni = lbr - 1.0, lbi
    pr, pi = nr * a_re + ni * a_im, ni * a_re - nr * a_im
    return dt, lbr, lbi, dn, nr, ni, pr, pi


def _vm():
    return pl.BlockSpec(memory_space=pltpu.VMEM)


def _s5_prep(a_re, a_im, log_dt, b_re, b_im):
    def body(ar_ref, ai_ref, ld_ref, br_ref, bi_ref, lbr_ref, lbi_ref, Br_ref, Bi_ref):
        _, lbr, lbi, dn, _, _, pr, pi = _s5_prep_math(ar_ref[...], ai_ref[...], ld_ref[...])
        cr, ci = (pr / dn)[:, None, :], (pi / dn)[:, None, :]
        lbr_ref[...] = lbr
        lbi_ref[...] = lbi
        br, bi = br_ref[...], bi_ref[...]
        Br_ref[...] = cr * br - ci * bi
        Bi_ref[...] = cr * bi + ci * br

    sd = jax.ShapeDtypeStruct
    return pl.pallas_call(
        body, name="s5_prep", in_specs=[_vm()] * 5, out_specs=[_vm()] * 4,
        out_shape=[sd(a_re.shape, F32), sd(a_re.shape, F32), sd(b_re.shape, F32), sd(b_re.shape, F32)],
    )(a_re, a_im, log_dt, b_re, b_im)


def _s5_prep_bwd(a_re, a_im, log_dt, b_re, b_im, dlbr_s, dlbi_s, dBr, dBi):
    def body(ar_ref, ai_ref, ld_ref, br_ref, bi_ref, dlr_ref, dli_ref, dBr_ref, dBi_ref,
             dar_ref, dai_ref, dld_ref, dbr_ref, dbi_ref):
        a_re_v, a_im_v = ar_ref[...], ai_ref[...]
        dt, lbr, lbi, dn, nr, ni, pr, pi = _s5_prep_math(a_re_v, a_im_v, ld_ref[...])
        cr, ci = (pr / dn)[:, None, :], (pi / dn)[:, None, :]
        br, bi, dBr_v, dBi_v = br_ref[...], bi_ref[...], dBr_ref[...], dBi_ref[...]
        dbr_ref[...] = cr * dBr_v + ci * dBi_v
        dbi_ref[...] = cr * dBi_v - ci * dBr_v
        dcr = jnp.sum(br * dBr_v + bi * dBi_v, axis=1)
        dci = jnp.sum(br * dBi_v - bi * dBr_v, axis=1)
        dpr, dpi = dcr / dn, dci / dn
        ddn = -(dcr * pr + dci * pi) / (dn * dn)
        dnr = dpr * a_re_v - dpi * a_im_v
        dni = dpr * a_im_v + dpi * a_re_v
        dlbr = dlr_ref[...] + dnr
        dlbi = dli_ref[...] + dni
        dxr = dlbr * lbr + dlbi * lbi
        dxi = dlbi * lbr - dlbr * lbi
        dar_ref[...] = dpr * nr + dpi * ni + 2.0 * ddn * a_re_v + dxr * dt
        dai_ref[...] = dpr * ni - dpi * nr + 2.0 * ddn * a_im_v + dxi * dt
        dld_ref[...] = jnp.sum(dxr * a_re_v + dxi * a_im_v, axis=0, keepdims=True) * dt

    sd = jax.ShapeDtypeStruct
    return pl.pallas_call(
        body, name="s5_prep_bwd", in_specs=[_vm()] * 9, out_specs=[_vm()] * 5,
        out_shape=[sd(a_re.shape, F32), sd(a_re.shape, F32), sd(log_dt.shape, F32),
                   sd(b_re.shape, F32), sd(b_re.shape, F32)],
    )(a_re, a_im, log_dt, b_re, b_im, dlbr_s, dlbi_s, dBr, dBi)


def _shift_rows(v, down):
    n = v.shape[0]
    rolled = pltpu.roll(v, 1 if down else n - 1, 0)
    row = lax.broadcasted_iota(jnp.int32, v.shape, 0)
    return jnp.where(row == (0 if down else n - 1), 0.0, rolled)


def _cmul(ar, ai, br, bi):
    return ar * br - ai * bi, ar * bi + ai * br


_SEG = 8
_UNROLL = 8


def _seg_rows(k):
    if isinstance(k, int):
        return pl.ds(k * _SEG, _SEG)
    return pl.ds(pl.multiple_of(k * _SEG, _SEG), _SEG)


def _unrolled(n, step, init):
    main = n // _UNROLL

    def trip(kk, s):
        for uu in range(_UNROLL):
            s = step(kk * _UNROLL + uu, s)
        return s

    s = lax.fori_loop(0, main, trip, init)
    for r in range(main * _UNROLL, n):
        s = step(r, s)
    return s


def _interleave(src_ref, dst_ref, nk):
    def step(k, carry):
        dst_ref[_seg_rows(k), :] = src_ref[pl.ds(k, _SEG, stride=nk), :]
        return carry
    _unrolled(nk, step, 0)


def _deinterleave(src_ref, dst_ref, nk):
    def step(k, carry):
        dst_ref[pl.ds(k, _SEG, stride=nk), :] = src_ref[_seg_rows(k), :]
        return carry
    _unrolled(nk, step, 0)


def _segment_inits(er, ei, ar, ai, nk, down):
    pr, pi = ar, ai
    for _ in range(int(math.log2(nk))):
        pr, pi = _cmul(pr, pi, pr, pi)
    fr, fi = er, ei
    for _ in range(_SEG - 1):
        sr, si = _shift_rows(fr, down), _shift_rows(fi, down)
        mr, mi = _cmul(pr, pi, sr, si)
        fr, fi = er + mr, ei + mi
    return _shift_rows(fr, down), _shift_rows(fi, down)


def _scan_states(x_re, x_im, ar, ai, nk):
    lanes = ar.shape[1]

    def step(k, s):
        rows = _seg_rows(k)
        mr, mi = _cmul(ar, ai, s[0], s[1])
        return mr + x_re[rows, :], mi + x_im[rows, :]

    zero = jnp.zeros((_SEG, lanes), F32)
    er, ei = _unrolled(nk, step, (zero, zero))
    ir, ii = _segment_inits(er, ei, ar, ai, nk, True)

    def step2(k, s):
        rows = _seg_rows(k)
        mr, mi = _cmul(ar, ai, s[0], s[1])
        nr, ni = mr + x_re[rows, :], mi + x_im[rows, :]
        x_re[rows, :] = nr
        x_im[rows, :] = ni
        return nr, ni

    _unrolled(nk, step2, (ir, ii))


def _s5_tile_fwd(u, bd_re, bd_im, cd_re, cd_im, ar, ai, d, s_re, s_im, nk):
    s_re[...] = _dot(u, bd_re, _NN)
    s_im[...] = _dot(u, bd_im, _NN)
    _scan_states(s_re, s_im, ar, ai, nk)
    return _dot(s_re[...], cd_re, _NN) - _dot(s_im[...], cd_im, _NN) + d * u


def _s5_specs(L, T):
    lanes = _S5_GT * _S5_P
    u_spec = pl.BlockSpec((L, _LANES), lambda t: (0, t))
    bd_spec = pl.BlockSpec((None, _S5_GT, _S5_C, _S5_P), lambda t: (t, 0, 0, 0))
    cd_spec = pl.BlockSpec((None, _S5_GT, _S5_P, _S5_C), lambda t: (t, 0, 0, 0))
    lam_spec = pl.BlockSpec((None, 1, lanes), lambda t: (t, 0, 0))
    d_spec = pl.BlockSpec((1, _LANES), lambda t: (0, t))
    return lanes, u_spec, bd_spec, cd_spec, lam_spec, d_spec


def _fill_block_diag(dst_ref, blocks_ref):
    _, a, b = blocks_ref.shape
    dst_ref[...] = jnp.zeros(dst_ref.shape, F32)
    for g in range(_S5_GT):
        dst_ref[g * a:(g + 1) * a, g * b:(g + 1) * b] = blocks_ref[g]


def _take_block_diag(dst_ref, v):
    _, a, b = dst_ref.shape
    for g in range(_S5_GT):
        dst_ref[g] = v[g * a:(g + 1) * a, g * b:(g + 1) * b]


def _s5_dense(bdr, bdi, cdr, cdi, dense):
    for src, dst in zip((bdr, bdi, cdr, cdi), dense):
        _fill_block_diag(dst, src)
    return [dst[...] for dst in dense]


def _s5_dense_scratch(lanes):
    return [pltpu.VMEM((_LANES, lanes), F32), pltpu.VMEM((_LANES, lanes), F32),
            pltpu.VMEM((lanes, _LANES), F32), pltpu.VMEM((lanes, _LANES), F32)]


def _s5_fwd(u, bd_re, bd_im, cd_re, cd_im, lam_re, lam_im, d):
    L, Wd = u.shape
    T = Wd // _LANES
    nk = L // _SEG
    lanes, u_spec, bd_spec, cd_spec, lam_spec, d_spec = _s5_specs(L, T)

    def body(u_ref, bdr, bdi, cdr, cdi, lr, li, d_ref, g_ref, s_re, s_im, up, tmp, *dense):
        ar = jnp.broadcast_to(lr[...], (_SEG, lanes))
        ai = jnp.broadcast_to(li[...], (_SEG, lanes))
        bd_re_v, bd_im_v, cd_re_v, cd_im_v = _s5_dense(bdr, bdi, cdr, cdi, dense)
        _interleave(u_ref, up, nk)
        y = _s5_tile_fwd(up[...], bd_re_v, bd_im_v, cd_re_v, cd_im_v, ar, ai, d_ref[...], s_re, s_im, nk)
        up[...] = _gelu(y)
        _deinterleave(up, tmp, nk)
        g_ref[...] = tmp[...].astype(g_ref.dtype)

    return pl.pallas_call(
        body, name="s5_fwd", grid=(T,),
        in_specs=[u_spec, bd_spec, bd_spec, cd_spec, cd_spec, lam_spec, lam_spec, d_spec],
        out_specs=u_spec, out_shape=jax.ShapeDtypeStruct((L, Wd), _MXU),
        scratch_shapes=[pltpu.VMEM((L, lanes), F32) for _ in range(2)]
        + [pltpu.VMEM((L, _LANES), F32) for _ in range(2)] + _s5_dense_scratch(lanes),
        compiler_params=_cparams(("arbitrary",)),
    )(u, bd_re, bd_im, cd_re, cd_im, lam_re, lam_im, d)


def _s5_bwd(u, dg, bd_re, bd_im, cd_re, cd_im, lam_re, lam_im, d):
    L, Wd = u.shape
    T = Wd // _LANES
    nk = L // _SEG
    lanes, u_spec, bd_spec, cd_spec, lam_spec, d_spec = _s5_specs(L, T)

    def body(u_ref, dg_ref, bdr, bdi, cdr, cdi, lr, li, d_ref,
             du_ref, dbdr, dbdi, dcdr, dcdi, dlr, dli, dd_ref, s_re, s_im, g_re, g_im, up, dgp, tmp, *dense):
        ar = jnp.broadcast_to(lr[...], (_SEG, lanes))
        ai = jnp.broadcast_to(li[...], (_SEG, lanes))
        bd_re_v, bd_im_v, cd_re_v, cd_im_v = _s5_dense(bdr, bdi, cdr, cdi, dense)
        _interleave(u_ref, up, nk)
        _interleave(dg_ref, dgp, nk)
        uv, dv = up[...], d_ref[...]
        y = _s5_tile_fwd(uv, bd_re_v, bd_im_v, cd_re_v, cd_im_v, ar, ai, dv, s_re, s_im, nk)
        dy = dgp[...] * _gelu_grad(y)
        dd_ref[...] = jnp.sum(dy * uv, axis=0, keepdims=True)
        dyb = dy.astype(_MXU)
        _take_block_diag(dcdr, _dot(dyb, s_re[...], _TN))
        _take_block_diag(dcdi, -_dot(dyb, s_im[...], _TN))
        g_re[...] = _dot(dyb, cd_re_v, _NT)
        g_im[...] = -_dot(dyb, cd_im_v, _NT)

        nai = -ai

        def step(j, s):
            rows = _seg_rows(nk - 1 - j)
            mr, mi = _cmul(ar, nai, s[0], s[1])
            return mr + g_re[rows, :], mi + g_im[rows, :]

        zero = jnp.zeros((_SEG, lanes), F32)
        er, ei = _unrolled(nk, step, (zero, zero))
        ir, ii = _segment_inits(er, ei, ar, nai, nk, False)

        def acc_lam(gr, gi, pr, pi, acc):
            return acc[0] + gr * pr + gi * pi, acc[1] + gi * pr - gr * pi

        def step2(j, carry):
            s, acc = carry
            k = nk - 1 - j
            rows = _seg_rows(k)
            mr, mi = _cmul(ar, nai, s[0], s[1])
            nr, ni = mr + g_re[rows, :], mi + g_im[rows, :]
            g_re[rows, :] = nr
            g_im[rows, :] = ni
            prev = _seg_rows(k - 1)
            return (nr, ni), acc_lam(nr, ni, s_re[prev, :], s_im[prev, :], acc)

        (g0r, g0i), acc = _unrolled(nk - 1, step2, ((ir, ii), (zero, zero)))
        first = _seg_rows(0)
        mr, mi = _cmul(ar, nai, g0r, g0i)
        nr, ni = mr + g_re[first, :], mi + g_im[first, :]
        g_re[first, :] = nr
        g_im[first, :] = ni
        last = _seg_rows(nk - 1)
        acc = acc_lam(nr, ni, _shift_rows(s_re[last, :], True), _shift_rows(s_im[last, :], True), acc)
        dlr[...] = jnp.sum(acc[0], axis=0, keepdims=True)
        dli[...] = jnp.sum(acc[1], axis=0, keepdims=True)

        gtr, gti = g_re[...].astype(_MXU), g_im[...].astype(_MXU)
        ub = uv.astype(_MXU)
        _take_block_diag(dbdr, _dot(ub, gtr, _TN))
        _take_block_diag(dbdi, _dot(ub, gti, _TN))
        dgp[...] = _dot(gtr, bd_re_v, _NT) + _dot(gti, bd_im_v, _NT) + dy * dv
        _deinterleave(dgp, tmp, nk)
        du_ref[...] = tmp[...].astype(du_ref.dtype)

    sd = jax.ShapeDtypeStruct
    big = sd((T, _S5_GT, _S5_C, _S5_P), F32)
    return pl.pallas_call(
        body, name="s5_bwd", grid=(T,),
        in_specs=[u_spec, u_spec, bd_spec, bd_spec, cd_spec, cd_spec, lam_spec, lam_spec, d_spec],
        out_specs=[u_spec, bd_spec, bd_spec, bd_spec, bd_spec, lam_spec, lam_spec, d_spec],
        out_shape=[sd((L, Wd), _MXU), big, big, big, big, sd((T, 1, lanes), F32), sd((T, 1, lanes), F32),
                   sd((1, Wd), F32)],
        scratch_shapes=[pltpu.VMEM((L, lanes), F32) for _ in range(4)]
        + [pltpu.VMEM((L, _LANES), F32) for _ in range(3)] + _s5_dense_scratch(lanes),
        compiler_params=_cparams(("arbitrary",)),
    )(u, dg, bd_re, bd_im, cd_re, cd_im, lam_re, lam_im, d)


def _half_tile(R, few_arrays=False):
    for t in ((512, 704, 128) if few_arrays else (256, 352, 128)):
        if R % t == 0:
            return t
    raise ValueError(R)


def _cast_shard(name, w, layer, kind, R, C):
    tr = _half_tile(R, True)
    nr = R // tr

    def body(w_ref, o_ref):
        o_ref[...] = w_ref[...].astype(o_ref.dtype)

    if kind == "col":
        in_map = lambda h, i: (layer, h * nr + i, 0)
    else:
        in_map = lambda h, i: (layer, i, h)
    return pl.pallas_call(
        body, name=name, grid=(2, nr), in_specs=[pl.BlockSpec((None, tr, C), in_map)],
        out_specs=pl.BlockSpec((None, tr, C), lambda h, i: (h, i, 0)),
        out_shape=jax.ShapeDtypeStruct((2, R, C), _WIRE),
        compiler_params=_cparams(("arbitrary", "arbitrary")),
    )(w)


def _adam_math(w, g, m, v):
    m2 = _B1 * m + (1.0 - _B1) * g
    v2 = _B2 * v + (1.0 - _B2) * (g * g)
    m_hat = m2 / (1.0 - _B1 ** _STEP)
    v_hat = v2 / (1.0 - _B2 ** _STEP)
    delta = -_LR * (m_hat / (jnp.sqrt(v_hat) + _AEPS) + _WD * w)
    return delta, m2, v2


def _adamw_big(name, w, m, v, layer, pair, kind, R, C, c, after, prev):
    tr = _half_tile(R, few_arrays=C <= 1024 and R % 512 == 0)
    nr = R // tr

    def body(c_ref, w_ref, m_ref, v_ref, own_ref, other_ref, *rest):
        go_ref, d_ref, mo_ref, vo_ref = rest[-4:]
        g = jnp.where(pl.program_id(0) == c_ref[0], own_ref[...], other_ref[...])
        delta, m2, v2 = _adam_math(w_ref[...], g, m_ref[...], v_ref[...])
        go_ref[...] = g
        d_ref[...] = delta
        mo_ref[...] = m2
        vo_ref[...] = v2

    if kind == "col":
        nat = pl.BlockSpec((None, tr, C), lambda h, i, c_ref: (layer, h * nr + i, 0))
    else:
        nat = pl.BlockSpec((None, tr, C), lambda h, i, c_ref: (layer, i, h))

    def gspec(own):
        return pl.BlockSpec((tr, C), lambda h, i, c_ref: (jnp.where((h == c_ref[0]) == own, i, 0), 0))

    carried = list(prev) if prev is not None else []
    gs = pltpu.PrefetchScalarGridSpec(
        num_scalar_prefetch=1, grid=(2, nr),
        in_specs=[nat, nat, nat, gspec(True), gspec(False), _any()] + [_any()] * len(carried),
        out_specs=[nat] * 4)
    sd = jax.ShapeDtypeStruct(w.shape, F32)
    return pl.pallas_call(
        body, name=name, grid_spec=gs, out_shape=[sd] * 4,
        input_output_aliases={7 + k: k for k in range(len(carried))},
        compiler_params=_cparams(("arbitrary", "arbitrary")),
    )(c.astype(jnp.int32).reshape(1), w, m, v, pair[0], pair[1], after, *carried)


def _adamw_flat(w, g, m, v):
    rows = w.shape[0]
    tr = rows // 8 if rows % 64 == 0 else rows

    def body(w_ref, g_ref, m_ref, v_ref, d_ref, mo_ref, vo_ref):
        delta, m2, v2 = _adam_math(w_ref[...], g_ref[...], m_ref[...], v_ref[...])
        d_ref[...] = delta
        mo_ref[...] = m2
        vo_ref[...] = v2

    spec = pl.BlockSpec((tr, _LANES), lambda i: (i, 0))
    sd = jax.ShapeDtypeStruct(w.shape, F32)
    return pl.pallas_call(
        body, name="adamw_small", grid=(rows // tr,), in_specs=[spec] * 4, out_specs=[spec] * 3,
        out_shape=[sd] * 3, compiler_params=_cparams(("arbitrary",)),
    )(w, g, m, v)


def _place():
    x, y, c = lax.axis_index("x"), lax.axis_index("y"), lax.axis_index("c")
    chips = [(1 - x, y), (x, 1 - y), (1 - x, 1 - y)]
    return x, y, c, 2 * x + y, chips


def _any():
    return pl.BlockSpec(memory_space=pl.ANY)


def _remote(src, dst, ssem, rsem, dev):
    return pltpu.make_async_remote_copy(src_ref=src, dst_ref=dst, send_sem=ssem, recv_sem=rsem,
                                        device_id=dev, device_id_type=_MESH)


def _allgather(name, shards):
    n = len(shards)

    def body(*refs):
        s_refs, g_refs = refs[:n], refs[n:2 * n]
        send0, recv0, send1, recv1, send2, recv2 = refs[2 * n:]
        x, y, c, q, _ = _place()
        sib, xn, yn = (x, y, 1 - c), (1 - x, y, c), (x, 1 - y, c)
        qx, qy, qd = 2 * (1 - x) + y, 2 * x + (1 - y), 2 * (1 - x) + (1 - y)
        _handshake([sib, xn, yn])
        own = [_remote(s_refs[a], g_refs[a].at[q], send0.at[a], recv0.at[a], sib) for a in range(n)]

        def pieces(a):
            g, half = g_refs[a], s_refs[a].shape[1] // 2
            return [g.at[qx, c], g.at[qy, c], g.at[qd, c, pl.ds(0, half)], g.at[qd, c, pl.ds(half, half)]]

        def relayed(a):
            g, half = g_refs[a], s_refs[a].shape[1] // 2
            return [(g.at[qx, c, pl.ds(0, half)], yn), (g.at[qy, c, pl.ds(half, half)], xn)]

        first = []
        for a in range(n):
            first.append(_remote(s_refs[a].at[c], g_refs[a].at[q, c], send1.at[4 * a], recv1.at[4 * a], xn))
            first.append(_remote(s_refs[a].at[c], g_refs[a].at[q, c], send1.at[4 * a + 1], recv1.at[4 * a + 1], yn))
        for cp in first + own:
            cp.start()
        later = []
        for a in range(n):
            land = pieces(a)
            for j in range(4):
                k = 4 * a + j
                _remote(land[j], land[j], send1.at[k], recv1.at[k], xn).wait_recv()
                if j < 2:
                    src, to = relayed(a)[j]
                    cp = _remote(src, src, send1.at[k + 2], recv1.at[k + 2], to)
                    cp.start()
                    later.append(cp)
                cp = _remote(land[j], land[j], send2.at[k], recv2.at[k], sib)
                cp.start()
                later.append(cp)
        for a in range(n):
            g, half = g_refs[a], s_refs[a].shape[1] // 2
            theirs = [g.at[qx, 1 - c], g.at[qy, 1 - c], g.at[qd, 1 - c, pl.ds(0, half)],
                      g.at[qd, 1 - c, pl.ds(half, half)]]
            for j in range(4):
                _remote(theirs[j], theirs[j], send2.at[4 * a + j], recv2.at[4 * a + j], sib).wait_recv()
        for cp in own:
            cp.wait()
        for cp in first + later:
            cp.wait_send()

    return _sequencer(name, _ID_GATHER, body, shards,
                      [jax.ShapeDtypeStruct((4,) + s.shape, s.dtype) for s in shards], [n, n] + [4 * n] * 4)


def _handshake(peers):
    barrier = pltpu.get_barrier_semaphore()
    for peer in peers:
        pl.semaphore_signal(barrier, inc=1, device_id=peer, device_id_type=_MESH)
    pl.semaphore_wait(barrier, len(peers))


_ID_SIBLING, _ID_CHIPS, _ID_GATHER, _ID_ALL = 1, 2, 3, 4


def _sequencer(name, collective_id, body, ins, out_types, sem_counts):
    mesh = plsc.ScalarSubcoreMesh(axis_name="seq", num_cores=1)
    moved = sum(math.prod(o.shape) * jnp.dtype(o.dtype).itemsize for o in out_types)
    return pl.kernel(
        body, name=name, out_type=out_types, mesh=mesh,
        scratch_types=[pltpu.SemaphoreType.DMA((k,)) for k in sem_counts],
        compiler_params=pltpu.CompilerParams(collective_id=collective_id),
        cost_estimate=pl.CostEstimate(flops=0, transcendentals=0, bytes_accessed=2 * moved,
                                      remote_bytes_transferred=moved),
    )(*ins)


def _swap_halves(name, grads):
    n = len(grads)

    def body(*refs):
        g_refs, t_refs = refs[:n], refs[n:2 * n]
        send, recv = refs[2 * n:]
        x, y, c, _, _ = _place()
        _handshake([(x, y, 1 - c)])
        cps = [_remote(g_refs[a].at[1 - c], t_refs[a], send.at[a], recv.at[a], (x, y, 1 - c)) for a in range(n)]
        for cp in cps:
            cp.start()
        for cp in cps:
            cp.wait()

    return _sequencer(name, _ID_SIBLING, body, grads,
                      [jax.ShapeDtypeStruct(g.shape[1:], g.dtype) for g in grads], [n, n])


def _chip_sum(name, g, t, after):
    _, _, R, C = g.shape
    tr = _half_tile(R, True)

    def body(c_ref, g_ref, t_ref, after_ref, o_ref):
        o_ref[...] = (g_ref[...].astype(F32) + t_ref[...].astype(F32)).astype(o_ref.dtype)

    c = lax.axis_index("c").astype(jnp.int32).reshape(1)
    gs = pltpu.PrefetchScalarGridSpec(
        num_scalar_prefetch=1, grid=(4, R // tr),
        in_specs=[pl.BlockSpec((None, None, tr, C), lambda r, i, c_ref: (c_ref[0], r, i, 0)),
                  pl.BlockSpec((None, tr, C), lambda r, i, c_ref: (r, i, 0)), _any()],
        out_specs=pl.BlockSpec((None, tr, C), lambda r, i, c_ref: (r, i, 0)))
    return pl.pallas_call(
        body, name=name, grid_spec=gs, out_shape=jax.ShapeDtypeStruct((4, R, C), _WIRE),
        compiler_params=_cparams(("arbitrary", "arbitrary")),
    )(c, g, t, after)


def _scatter_parts(name, parts):
    n = len(parts)

    def body(*refs):
        p_refs, t_refs = refs[:n], refs[n:2 * n]
        send, recv = refs[2 * n:]
        x, y, c, q, chips = _place()
        _handshake([(rx, ry, c) for rx, ry in chips])
        cps = []
        for a in range(n):
            for j, (rx, ry) in enumerate(chips):
                k = 3 * a + j
                cps.append(_remote(p_refs[a].at[2 * rx + ry], t_refs[a].at[q], send.at[k], recv.at[k], (rx, ry, c)))
        for cp in cps:
            cp.start()
        for a in range(n):
            for j, (rx, ry) in enumerate(chips):
                k = 3 * a + j
                land = t_refs[a].at[2 * rx + ry]
                _remote(land, land, send.at[k], recv.at[k], (rx, ry, c)).wait_recv()
        for cp in cps:
            cp.wait_send()

    return _sequencer(name, _ID_CHIPS, body, parts,
                      [jax.ShapeDtypeStruct(p.shape, p.dtype) for p in parts], [3 * n, 3 * n])


def _sum_parts(name, p, t, where, after):
    _, R, C = t.shape
    tr = _half_tile(R, True)

    def body(w_ref, p_ref, t0_ref, t1_ref, t2_ref, after_ref, o_ref):
        o_ref[...] = (p_ref[...].astype(F32) + t0_ref[...].astype(F32)
                      + t1_ref[...].astype(F32) + t2_ref[...].astype(F32))

    def part(slot):
        return pl.BlockSpec((None, tr, C), lambda i, w_ref: (w_ref[slot], i, 0))

    gs = pltpu.PrefetchScalarGridSpec(
        num_scalar_prefetch=1, grid=(R // tr,), in_specs=[part(0), part(1), part(2), part(3), _any()],
        out_specs=pl.BlockSpec((tr, C), lambda i, w_ref: (i, 0)))
    return pl.pallas_call(
        body, name=name, grid_spec=gs, out_shape=jax.ShapeDtypeStruct((R, C), F32),
        compiler_params=_cparams(("arbitrary",)),
    )(where, p, t, t, t, after)


def _send_halves(name, halves):
    n = len(halves)

    def body(*refs):
        h_refs, o_refs = refs[:n], refs[n:2 * n]
        send, recv = refs[2 * n:]
        x, y, c, _, _ = _place()
        _handshake([(x, y, 1 - c)])
        cps = [_remote(h_refs[a], o_refs[a], send.at[a], recv.at[a], (x, y, 1 - c)) for a in range(n)]
        for cp in cps:
            cp.start()
        for cp in cps:
            cp.wait()

    return _sequencer(name, _ID_SIBLING, body, halves,
                      [jax.ShapeDtypeStruct(h.shape, h.dtype) for h in halves], [n, n])


class _Order:
    def __init__(self):
        self.tok = None

    def tie(self, x):
        return x if self.tok is None else lax.optimization_barrier((x, self.tok))[0]

    def done(self, outs):
        self.tok = outs[0]
        return outs


class _Reduction:
    def __init__(self, tag, grads):
        self.tag, self.grads = tag, grads
        self.swapped = _swap_halves(f"rs_swap_{tag}", grads)

    def scatter(self, seq):
        self.parts = [seq.done([_chip_sum(f"rs_chipsum_{self.tag}_{a}", g, t, seq.tok)])[0]
                      for a, (g, t) in enumerate(zip(self.grads, self.swapped))]
        self.landed = _scatter_parts(f"rs_scatter_{self.tag}", self.parts)

    def finish(self, seq):
        x, y, c, q, chips = _place()
        where = jnp.stack([q] + [2 * rx + ry for rx, ry in chips]).astype(jnp.int32)
        halves = [seq.done([_sum_parts(f"rs_sum_{self.tag}_{a}", p, t, where, seq.tok)])[0]
                  for a, (p, t) in enumerate(zip(self.parts, self.landed))]
        return list(zip(halves, _send_halves(f"rs_join_{self.tag}", halves)))


def _allreduce_small(name, v):
    _, R, _ = v.shape

    def body(v_ref, o_ref, land, acc, send1, recv1, send2, recv2):
        x, y, c = lax.axis_index("x"), lax.axis_index("y"), lax.axis_index("c")
        me = 4 * x + 2 * y + c
        peers = []
        for k in range(1, 8):
            dx, dy, dc = (k >> 2) & 1, (k >> 1) & 1, k & 1
            px, py, pc = (1 - x if dx else x), (1 - y if dy else y), (1 - c if dc else c)
            peers.append((k, (px, py, pc), 4 * px + 2 * py + pc))
        land[me] = v_ref[me]
        out1 = [_remote(v_ref.at[pid], land.at[me], send1.at[k], recv1.at[k], dev) for k, dev, pid in peers]
        for cp in out1:
            cp.start()
        for k, dev, pid in peers:
            _remote(land.at[pid], land.at[pid], send1.at[k], recv1.at[k], dev).wait_recv()
        total = land[0]
        for j in range(1, 8):
            total = total + land[j]
        acc[...] = total
        o_ref[me] = total
        out2 = [_remote(acc, o_ref.at[me], send2.at[k], recv2.at[k], dev) for k, dev, pid in peers]
        for cp in out2:
            cp.start()
        for k, dev, pid in peers:
            _remote(o_ref.at[pid], o_ref.at[pid], send2.at[k], recv2.at[k], dev).wait_recv()
        for cp in out1 + out2:
            cp.wait_send()

    return pl.pallas_call(
        body, name=name, in_specs=[_vm()], out_specs=_vm(),
        out_shape=jax.ShapeDtypeStruct(v.shape, F32),
        scratch_shapes=[pltpu.VMEM(v.shape, F32), pltpu.VMEM((R, _LANES), F32)]
        + [pltpu.SemaphoreType.DMA((8,)) for _ in range(4)],
        compiler_params=pltpu.CompilerParams(vmem_limit_bytes=_VMEM_LIMIT),
    )(v)


def _all_peers():
    x, y, c = lax.axis_index("x"), lax.axis_index("y"), lax.axis_index("c")
    peers = []
    for k in range(1, 8):
        px, py, pc = (1 - x if k & 4 else x), (1 - y if k & 2 else y), (1 - c if k & 1 else c)
        peers.append((k, (px, py, pc), 4 * px + 2 * py + pc))
    return 4 * x + 2 * y + c, peers


def _exchange_slices(name, v):
    def body(v_ref, land, send, recv):
        me, peers = _all_peers()
        _handshake([dev for _, dev, _ in peers])
        cps = [_remote(v_ref.at[pid], land.at[me], send.at[k], recv.at[k], dev) for k, dev, pid in peers]
        for cp in cps:
            cp.start()
        for k, dev, pid in peers:
            _remote(land.at[pid], land.at[pid], send.at[k], recv.at[k], dev).wait_recv()
        for cp in cps:
            cp.wait_send()

    return _sequencer(name, _ID_ALL, body, [v], [jax.ShapeDtypeStruct(v.shape, v.dtype)], [8, 8])[0]


def _sum_slices(name, v, landed, after):
    _, R, _ = v.shape

    def body(v_ref, land_ref, after_ref, o_ref):
        me, peers = _all_peers()
        acc = v_ref[me]
        for _, _, pid in peers:
            acc = acc + land_ref[pid]
        o_ref[...] = acc

    return pl.pallas_call(
        body, name=name, in_specs=[_vm(), _vm(), _any()], out_specs=_vm(),
        out_shape=jax.ShapeDtypeStruct((R, _LANES), F32),
        compiler_params=pltpu.CompilerParams(vmem_limit_bytes=_VMEM_LIMIT),
    )(v, landed, after)


def _broadcast_slices(name, s):
    def body(s_ref, out, send, recv):
        me, peers = _all_peers()
        _handshake([dev for _, dev, _ in peers])
        cps = [_remote(s_ref, out.at[me], send.at[k], recv.at[k], dev) for k, dev, pid in peers]
        for cp in cps:
            cp.start()
        for k, dev, pid in peers:
            _remote(out.at[pid], out.at[pid], send.at[k], recv.at[k], dev).wait_recv()
        for cp in cps:
            cp.wait_send()

    return _sequencer(name, _ID_ALL, body, [s], [jax.ShapeDtypeStruct((8,) + s.shape, s.dtype)], [8, 8])[0]


_WEIGHT_NAMES = ['norm_mix', 'norm_ffn', 'norm_ple', 'norm_final', 'gm_w_in', 'gm_ln_g', 'gm_ln_b', 'gm_w_s',
                 'gm_b_s', 'gm_w_out', 's5_w_in', 's5_a_re', 's5_a_im', 's5_log_dt', 's5_b_re', 's5_b_im',
                 's5_c_re', 's5_c_im', 's5_d', 's5_w_out', 'ffn_w1', 'ffn_w3', 'ffn_w2', 'ple_w_gate', 'ple_w_proj']
_BIG = {'gm_w_in': 'col', 'gm_w_out': 'row', 's5_w_in': 'row', 's5_w_out': 'col', 'ffn_w1': 'col',
        'ffn_w3': 'col', 'ffn_w2': 'row', 'ple_w_gate': 'row', 'ple_w_proj': 'col'}


_VIEW = {'s5_a_re': (0, 2, 1), 's5_a_im': (0, 2, 1), 's5_b_re': (0, 2, 3, 1), 's5_b_im': (0, 2, 3, 1),
         's5_c_re': (0, 2, 3, 1), 's5_c_im': (0, 2, 3, 1)}


def _to_view(name, a):
    return jnp.transpose(a, _VIEW[name]) if name in _VIEW else a


def _from_view(name, a):
    if name not in _VIEW:
        return a
    perm = _VIEW[name]
    return jnp.transpose(a, [perm.index(i) for i in range(len(perm))])


def _rc(kind, shard_shape):
    rows, cols = shard_shape[-2:]
    return (rows // 2, cols) if kind == "col" else (rows, cols // 2)


def _pack(vecs, rows_multiple):
    flat = jnp.concatenate([a.reshape(-1).astype(F32) for a in vecs])
    unit = rows_multiple * _LANES
    pad = (-flat.shape[0]) % unit
    return jnp.pad(flat, (0, pad)).reshape(-1, _LANES)


def _unpack(buf, shapes):
    flat = buf.reshape(-1)
    out, off = [], 0
    for s in shapes:
        n = math.prod(s)
        out.append(flat[off:off + n].reshape(s))
        off += n
    return out


def _ident(accs, ex):
    return accs


def _add_resid(accs, ex):
    return [accs[0] + ex[0]]


def _swiglu_epi(accs, ex):
    a, b = accs
    return [a, b, a * _sig(a) * b]


def _swiglu_bwd_epi(accs, ex):
    df = accs[0]
    a, b = ex[0].astype(F32), ex[1].astype(F32)
    sa = _sig(a)
    return [df * b * (sa * (1.0 + a * (1.0 - sa))), df * (a * sa)]


def _ple_epi(accs, ex):
    gt = _sig(accs[0])
    return [ex[0] + gt * ex[1], gt]


def _glu_epi(accs, ex):
    val, sg = accs[0], _sig(accs[1])
    return [ex[0] + val * sg, val, sg]


def kernel(x, p, norm_mix, norm_ffn, norm_ple, norm_final, gm_w_in, gm_ln_g, gm_ln_b, gm_w_s, gm_b_s, gm_w_out, s5_w_in, s5_a_re, s5_a_im, s5_log_dt, s5_b_re, s5_b_im, s5_c_re, s5_c_im, s5_d, s5_w_out, ffn_w1, ffn_w3, ffn_w2, ple_w_gate, ple_w_proj, loss_target, m_norm_mix, m_norm_ffn, m_norm_ple, m_norm_final, m_gm_w_in, m_gm_ln_g, m_gm_ln_b, m_gm_w_s, m_gm_b_s, m_gm_w_out, m_s5_w_in, m_s5_a_re, m_s5_a_im, m_s5_log_dt, m_s5_b_re, m_s5_b_im, m_s5_c_re, m_s5_c_im, m_s5_d, m_s5_w_out, m_ffn_w1, m_ffn_w3, m_ffn_w2, m_ple_w_gate, m_ple_w_proj, v_norm_mix, v_norm_ffn, v_norm_ple, v_norm_final, v_gm_w_in, v_gm_ln_g, v_gm_ln_b, v_gm_w_s, v_gm_b_s, v_gm_w_out, v_s5_w_in, v_s5_a_re, v_s5_a_im, v_s5_log_dt, v_s5_b_re, v_s5_b_im, v_s5_c_re, v_s5_c_im, v_s5_d, v_s5_w_out, v_ffn_w1, v_ffn_w3, v_ffn_w2, v_ple_w_gate, v_ple_w_proj):
    env = dict(locals())
    w = {n: env[n] for n in _WEIGHT_NAMES}
    mom = {n: env["m_" + n] for n in _WEIGHT_NAMES}
    var = {n: env["v_" + n] for n in _WEIGHT_NAMES}
    xs, tgt = x[0], loss_target[0]
    L, D = xs.shape
    depth = norm_mix.shape[0]
    qx, qy = lax.axis_index("x"), lax.axis_index("y")
    q = 2 * qx + qy

    def gather(tag, items):
        shards = []
        for name, layer in items:
            kind = _BIG[name]
            R, C = _rc(kind, w[name].shape)
            shards.append(_cast_shard(f"cast_{name}{layer}", w[name], layer, kind, R, C))
        full = _allgather(f"ag_{tag}", shards)
        return {it: _W(f, _BIG[it[0]]) for it, f in zip(items, full)}

    W = {}
    mixers = [["gm_w_in", "gm_w_out"], ["s5_w_in", "s5_w_out"]]
    for i in range(depth):
        for n in mixers[i]:
            W.update(gather(f"{n}", [(n, 0)]))
        W.update(gather(f"ffn_up{i}", [("ffn_w1", i), ("ffn_w3", i)]))
        W.update(gather(f"ffn_down{i}", [("ffn_w2", i)]))
        W.update(gather(f"ple{i}", [("ple_w_gate", i), ("ple_w_proj", i)]))

    d_slots = jnp.zeros((4, D // 4), F32)
    d_slots = lax.dynamic_update_slice(d_slots, s5_d.astype(F32), (q, 0))
    d_sum = _allreduce_small("ar_s5_d", _pack([d_slots], 64).reshape(8, -1, _LANES))
    d_full = (d_sum.reshape(-1)[:D] * 0.5).reshape(1, D)

    def ffn_fwd(i, xin):
        hf = _rms_fwd(f"rms_ffn{i}", xin, norm_ffn[i:i + 1])
        a, b, f = _mm_nn(f"ffn_up{i}", hf, [W["ffn_w1", i], W["ffn_w3", i]], 1024, 1408, ffn_w2.shape[1] * 4,
                         [], [_MXU, _MXU, _MXU], _swiglu_epi, tm=1024)
        xo = _mm_nn(f"ffn_down{i}", f, [W["ffn_w2", i]], 1408, 1024, D, [xin], [F32], _add_resid, tm=1024)[0]
        return xo, (xin, hf, a, b, f)

    def ple_fwd(i, xin):
        hp = _rms_fwd(f"rms_ple{i}", xin, norm_ple[i:i + 1])
        pi = lax.optimization_barrier((p[i, 0], hp))[0]
        pp = _mm_nn(f"ple_proj{i}", pi, [W["ple_w_proj", i]], 128, 512, D, [], [F32], _ident, tm=2048)[0]
        xo, gt = _mm_nn(f"ple_gate{i}", hp, [W["ple_w_gate", i]], 512, 1024, D, [xin, pp], [F32, _MXU], _ple_epi,
                        tm=1024)
        return xo, (xin, hp, pi, pp, gt)

    h0 = _rms_fwd("rms_mix0", xs, norm_mix[0:1])
    z = _mm_nn("gm_in", h0, [W["gm_w_in", 0]], 1024, 1024, 2 * D, [], [F32], _ident, tm=1024)[0]
    bsT = gm_b_s[0].T
    gm_m = _gmlp_fwd(z, gm_ln_g, gm_ln_b, gm_w_s[0], bsT)
    x1 = _mm_nn("gm_out", gm_m, [W["gm_w_out", 0]], 512, 1024, D, [xs], [F32], _add_resid, tm=1024)[0]
    x2, ffn0 = ffn_fwd(0, x1)
    x3, ple0 = ple_fwd(0, x2)

    T = D // _LANES
    lanes = _S5_GT * _S5_P
    sv = {n: _to_view(n, w[n])[0] for n in _VIEW}
    a_re, a_im, log_dt = sv["s5_a_re"], sv["s5_a_im"], s5_log_dt
    lbr, lbi, Bbar_re, Bbar_im = _s5_prep(a_re, a_im, log_dt, sv["s5_b_re"], sv["s5_b_im"])

    def to_bd(B):
        return jnp.transpose(B.reshape(_S5_P, _S5_C, T, _S5_GT), (2, 3, 1, 0))

    def to_cd(cw):
        return jnp.transpose(cw.reshape(_S5_C, _S5_P, T, _S5_GT), (2, 3, 1, 0))

    def to_lam(v):
        return jnp.transpose(v).reshape(T, 1, lanes)

    bd_re, bd_im = to_bd(Bbar_re), to_bd(Bbar_im)
    cd_re, cd_im = to_cd(sv["s5_c_re"]), to_cd(sv["s5_c_im"])
    lam_re, lam_im = to_lam(lbr), to_lam(lbi)

    h1 = _rms_fwd("rms_mix1", x3, norm_mix[1:2])
    u = _mm_nn("s5_in", h1, [W["s5_w_in", 0]], 512, 1024, D, [], [F32], _ident, tm=1024)[0]
    s5_g = _s5_fwd(u, bd_re, bd_im, cd_re, cd_im, lam_re, lam_im, d_full)
    x4, glu_val, glu_sg = _mm_nn("s5_out", s5_g, [W["s5_w_out", 0], W["s5_w_out", 0]], 1024, 1024, D, [x3],
                                 [F32, _MXU, _MXU], _glu_epi, tm=1024, cb_offsets=[0, 2])
    x5, ffn1 = ffn_fwd(1, x4)
    x6, ple1 = ple_fwd(1, x5)

    dx, d_norm_final, loss_rows = _loss_head(x6, norm_final[None], tgt)

    small = {}

    seq = _Order()
    tie, done = seq.tie, seq.done
    d_norm_ple, d_norm_ffn, d_norm_mix = [None] * depth, [None] * depth, [None] * depth
    reduced = {}

    def keep(names, layer, pairs):
        for n, pr in zip(names, pairs):
            reduced[n, layer] = pr

    ple_names, up_names, down_names = ["ple_w_gate", "ple_w_proj"], ["ffn_w1", "ffn_w3"], ["ffn_w2"]

    def ple_bwd(i, dxo, saved):
        xin, hp, pi, pp, gt = saved
        dpre, dpp = done(_ple_bwd_elem(tie(dxo), pp, gt))
        dwg = done(_mm_tn(f"ple_gate_dw{i}", tie(hp), [dpre], "row", 512, 1024, 512, 1024))[0]
        dwp = done(_mm_tn(f"ple_proj_dw{i}", tie(pi), [dpp], "col", 128, 512, 128, 512))[0]
        red = _Reduction(f"ple{i}", [dwg, dwp])
        dhp = done(_mm_nt(f"ple_gate_dx{i}", [tie(dpre)], [W["ple_w_gate", i]], 512, 1024, [], [F32], _ident,
                          tm=2048))[0]
        dxin, dxin_mxu, dg = done(_rms_bwd(f"rms_ple_bwd{i}", tie(dhp), xin, norm_ple[i:i + 1], dxo))
        return dxin, dxin_mxu, dg, red

    def ffn_bwd(i, dxo, dxo_mxu, saved, before_up):
        xin, hf, a, b, f = saved
        dw2 = done(_mm_tn(f"ffn_down_dw{i}", tie(f), [dxo_mxu], "row", 1408, 1024, 1408, 1024))[0]
        r_down = _Reduction(f"ffd{i}", [dw2])
        da, db = done(_mm_nt(f"ffn_down_dx{i}", [tie(dxo_mxu)], [W["ffn_w2", i]], 1408, 1024, [a, b], [_MXU, _MXU],
                             _swiglu_bwd_epi, tm=1024))
        for step in before_up:
            step()
        r_down.scatter(seq)
        dw1, dw3 = done(_mm_tn(f"ffn_up_dw{i}", tie(hf), [da, db], "col", 1024, 1408, 1024, 1408))
        r_up = _Reduction(f"ffu{i}", [dw1, dw3])
        dhf = done(_mm_nt(f"ffn_up_dx{i}", [tie(da), db], [W["ffn_w1", i], W["ffn_w3", i]], 1024, 1408, [], [F32],
                          _ident, tm=1024))[0]
        dxin, dxin_mxu, dg = done(_rms_bwd(f"rms_ffn_bwd{i}", tie(dhf), xin, norm_ffn[i:i + 1], dxo))
        r_up.scatter(seq)
        return dxin, dxin_mxu, dg, r_down, r_up

    dx, dx_mxu, d_norm_ple[1], r_ple1 = ple_bwd(1, dx, ple1)
    dx, _, d_norm_ffn[1], r_down1, r_up1 = ffn_bwd(1, dx, dx_mxu, ffn1, [lambda: r_ple1.scatter(seq)])

    do = done([_glu_bwd_elem(tie(dx), glu_val, glu_sg)])[0]
    dw_s5out = done(_mm_tn("s5_out_dw", tie(s5_g), [do], "col", 1024, 1024, 1024, 1024))[0]
    r_s5out = _Reduction("s5out", [dw_s5out])
    dgy = done(_mm_nt("s5_out_dx", [tie(do)], [W["s5_w_out", 0]], 1024, 1024, [], [F32], _ident, tm=1024))[0]
    keep(ple_names, 1, r_ple1.finish(seq))
    keep(down_names, 1, r_down1.finish(seq))
    du, dbd_re, dbd_im, dcd_re, dcd_im, dl_re, dl_im, dd = done(_s5_bwd(tie(u), dgy, bd_re, bd_im, cd_re, cd_im,
                                                                       lam_re, lam_im, d_full))
    r_s5out.scatter(seq)
    dw_s5in = done(_mm_tn("s5_in_dw", tie(h1), [du], "row", 512, 1024, 512, 1024))[0]
    r_s5in = _Reduction("s5in", [dw_s5in])
    dh1 = done(_mm_nt("s5_in_dx", [tie(du)], [W["s5_w_in", 0]], 512, 1024, [], [F32], _ident, tm=2048))[0]
    dx, _, d_norm_mix[1] = done(_rms_bwd("rms_mix1_bwd", tie(dh1), x3, norm_mix[1:2], dx))
    keep(up_names, 1, r_up1.finish(seq))
    r_s5in.scatter(seq)

    def from_bd(t):
        return jnp.transpose(t, (3, 2, 0, 1)).reshape(_S5_P, _S5_C, T * _S5_GT)

    def from_cdT(t):
        return jnp.transpose(t, (2, 3, 0, 1)).reshape(_S5_C, _S5_P, T * _S5_GT)

    def from_lam(t):
        return jnp.transpose(t.reshape(T * _S5_GT, _S5_P))

    da_re, da_im, dlog_dt, db_re, db_im = _s5_prep_bwd(
        a_re, a_im, log_dt, sv["s5_b_re"], sv["s5_b_im"], from_lam(dl_re), from_lam(dl_im),
        from_bd(dbd_re), from_bd(dbd_im))
    small["s5_a_re"], small["s5_a_im"], small["s5_log_dt"] = da_re[None], da_im[None], dlog_dt
    small["s5_b_re"], small["s5_b_im"] = db_re[None], db_im[None]
    small["s5_c_re"], small["s5_c_im"] = from_cdT(dcd_re)[None], from_cdT(dcd_im)[None]

    dx, dx_mxu, d_norm_ple[0], r_ple0 = ple_bwd(0, dx, ple0)
    keep(["s5_w_out"], 0, r_s5out.finish(seq))
    xin0, hf0, a0, b0, f0 = ffn0
    da0, db0 = done(_mm_nt("ffn_down_dx0", [tie(dx_mxu)], [W["ffn_w2", 0]], 1408, 1024, [a0, b0], [_MXU, _MXU],
                           _swiglu_bwd_epi, tm=1024))
    r_ple0.scatter(seq)
    dw1, dw3 = done(_mm_tn("ffn_up_dw0", tie(hf0), [da0, db0], "col", 1024, 1408, 1024, 1408))
    r_up0 = _Reduction("ffu0", [dw1, dw3])
    keep(["s5_w_in"], 0, r_s5in.finish(seq))
    dw2 = done(_mm_tn("ffn_down_dw0", tie(f0), [dx_mxu], "row", 1408, 1024, 1408, 1024))[0]
    r_down0 = _Reduction("ffd0", [dw2])
    r_up0.scatter(seq)
    dhf0 = done(_mm_nt("ffn_up_dx0", [tie(da0), db0], [W["ffn_w1", 0], W["ffn_w3", 0]], 1024, 1408, [], [F32], _ident,
                       tm=1024))[0]
    dx, dx_mxu, d_norm_ffn[0] = done(_rms_bwd("rms_ffn_bwd0", tie(dhf0), xin0, norm_ffn[0:1], dx))
    keep(ple_names, 0, r_ple0.finish(seq))
    r_down0.scatter(seq)

    dw_gmout = done(_mm_tn("gm_out_dw", tie(gm_m), [dx_mxu], "row", 512, 1024, 512, 1024))[0]
    r_gmout = _Reduction("gmout", [dw_gmout])
    dgm = done(_mm_nt("gm_out_dx", [tie(dx_mxu)], [W["gm_w_out", 0]], 512, 1024, [], [F32], _ident, tm=2048))[0]
    dz, dws, dbsT, dlng, dlnb = done(_gmlp_bwd(tie(z), dgm, gm_ln_g, gm_ln_b, gm_w_s[0], bsT))
    dw_gmin = done(_mm_tn("gm_in_dw", tie(h0), [dz], "col", 1024, 1024, 1024, 1024))[0]
    r_gmin = _Reduction("gmin", [dw_gmin])
    dh0 = done(_mm_nt("gm_in_dx", [tie(dz)], [W["gm_w_in", 0]], 1024, 1024, [], [F32], _ident, tm=1024))[0]
    dx, _, d_norm_mix[0] = done(_rms_bwd("rms_mix0_bwd", tie(dh0), xs, norm_mix[0:1], dx))
    grad_x = dx[None]

    small["norm_mix"], small["norm_ffn"] = jnp.concatenate(d_norm_mix), jnp.concatenate(d_norm_ffn)
    small["norm_ple"], small["norm_final"] = jnp.concatenate(d_norm_ple), d_norm_final[0]
    small["gm_ln_g"], small["gm_ln_b"], small["gm_w_s"] = dlng, dlnb, dws[None]
    small["gm_b_s"] = dbsT[:, :_GM_HEADS].T[None]
    small["s5_d"] = dd

    small_names = [n for n in _WEIGHT_NAMES if n not in _BIG]
    packed = _pack([small[n] for n in small_names] + [loss_rows[:, :1]], 64).reshape(8, -1, _LANES)
    grads, deltas, new_m, new_v = {}, {}, {}, {}
    my_c = lax.axis_index("c")

    def adamw(n, layer, prev):
        kind = _BIG[n]
        R, C = _rc(kind, w[n].shape)
        return done(_adamw_big(f"adamw_{n}{layer}", w[n], mom[n], var[n], layer, reduced[n, layer], kind, R, C,
                               my_c, seq.tok, prev))

    def adamw_last(names):
        for n in names:
            grads[n], deltas[n], new_m[n], new_v[n] = adamw(n, 0, late.get(n))

    late = {}
    for n in down_names + up_names + ple_names:
        late[n] = adamw(n, 1, None)
    keep(up_names, 0, r_up0.finish(seq))
    r_gmout.scatter(seq)
    r_gmin.scatter(seq)
    landed = _exchange_slices("ar_small_in", packed)
    adamw_last(["s5_w_in", "s5_w_out"] + ple_names)
    keep(down_names, 0, r_down0.finish(seq))
    adamw_last(up_names)
    keep(["gm_w_out"], 0, r_gmout.finish(seq))
    mine = done([_sum_slices("ar_small_sum", packed, landed, seq.tok)])[0]
    spread = _broadcast_slices("ar_small_out", mine)
    keep(["gm_w_in"], 0, r_gmin.finish(seq))
    adamw_last(down_names + ["gm_w_out", "gm_w_in"])
    summed = lax.dynamic_update_slice(spread, mine[None], (4 * qx + 2 * qy + my_c, 0, 0))
    *red_list, loss_sum = _unpack(summed, [small[n].shape for n in small_names] + [(1, 1)])
    red_small, loss = dict(zip(small_names, red_list)), loss_sum.reshape(())
    red_small["s5_d"] = lax.dynamic_slice(red_small["s5_d"], (0, q * (D // 4)), (1, D // 4))

    def views(src):
        return [_to_view(n, src[n]) for n in small_names]

    shapes = [v_.shape for v_ in views(w)]
    dl, mo, vo = _adamw_flat(_pack(views(w), 64), _pack([red_small[n] for n in small_names], 64),
                             _pack(views(mom), 64), _pack(views(var), 64))
    for n, g_, d_, m_, v_ in zip(small_names, [red_small[n] for n in small_names], _unpack(dl, shapes),
                                 _unpack(mo, shapes), _unpack(vo, shapes)):
        grads[n], deltas[n], new_m[n], new_v[n] = [_from_view(n, t_).reshape(w[n].shape) for t_ in (g_, d_, m_, v_)]

    return (loss, grad_x, *[grads[n] for n in _WEIGHT_NAMES], *[deltas[n] for n in _WEIGHT_NAMES],
            *[new_m[n] for n in _WEIGHT_NAMES], *[new_v[n] for n in _WEIGHT_NAMES])
```

```python
import functools
import math

import jax
import jax.numpy as jnp
from jax import lax
from jax.experimental import pallas as pl
from jax.experimental.pallas import tpu as pltpu
from jax.experimental.pallas import tpu_sc as plsc

F32 = jnp.float32
_MXU = jnp.bfloat16
_WIRE = jnp.bfloat16
_EPS = 1e-6
_VMEM_LIMIT = 56 * 1024 * 1024
_LANES = 128
_MESH = pl.DeviceIdType.MESH

_LR, _B1, _B2, _AEPS, _WD, _STEP = 0.001, 0.9, 0.999, 1e-08, 0.01, 10

_GM_CHUNK = 128
_GM_HEADS = 16
_S5_GT = 8
_S5_P = 64
_S5_C = 16

_NN = (((1,), (0,)), ((), ()))
_NT = (((1,), (1,)), ((), ()))
_TN = (((0,), (0,)), ((), ()))


def _cparams(sem):
    return pltpu.CompilerParams(dimension_semantics=sem, vmem_limit_bytes=_VMEM_LIMIT)


def _sig(x):
    return 0.5 * jnp.tanh(0.5 * x) + 0.5


_GC = math.sqrt(2.0 / math.pi)


def _gelu(x):
    return 0.5 * x * (1.0 + jnp.tanh(_GC * (x + 0.044715 * (x * x * x))))


def _gelu_grad(x):
    t = jnp.tanh(_GC * (x + 0.044715 * (x * x * x)))
    return 0.5 * (1.0 + t) + 0.5 * x * (1.0 - t * t) * (_GC * (1.0 + 3.0 * 0.044715 * x * x))


def _dot(a, b, dn):
    return lax.dot_general(a.astype(_MXU), b.astype(_MXU), dn, preferred_element_type=F32)


class _W:
    def __init__(self, arr, kind):
        self.a, self.kind = arr, kind
        self.R, self.C = arr.shape[2], arr.shape[3]

    def full_shape(self):
        return (2 * self.R, 4 * self.C) if self.kind == "col" else (4 * self.R, 2 * self.C)


def _part_index(kind, R, C, tr, tc, rb, cb):
    nr, nc = R // tr, C // tc
    if kind == "col":
        return cb // nc, rb // nr, rb % nr, cb % nc
    return rb // nr, cb // nc, rb % nr, cb % nc


def _wspec(w, tr, tc, rb_fn, cb_fn):
    assert w.R % tr == 0 and w.C % tc == 0, (w.R, w.C, tr, tc)

    def imap(i, j, k):
        return _part_index(w.kind, w.R, w.C, tr, tc, rb_fn(i, j, k), cb_fn(i, j, k))

    return pl.BlockSpec((None, None, tr, tc), imap)


def _gspec(kind, R, C, tr, tc):
    assert R % tr == 0 and C % tc == 0, (R, C, tr, tc)

    def imap(i, j, k):
        part, half, rbi, cbi = _part_index(kind, R, C, tr, tc, i, j)
        return half, part, rbi, cbi

    return pl.BlockSpec((None, None, tr, tc), imap)


def _mm(name, grid, a_ops, b_ops, pairs, acc_shape, n_acc, extras, outs, epilogue):
    nk = grid[2]
    na, nb, ne, no = len(a_ops), len(b_ops), len(extras), len(outs)

    def body(*refs):
        a_refs = refs[:na]
        b_refs = refs[na:na + nb]
        e_refs = refs[na + nb:na + nb + ne]
        o_refs = refs[na + nb + ne:na + nb + ne + no]
        acc_refs = refs[na + nb + ne + no:]
        k = pl.program_id(2)

        def products():
            sums = [None] * n_acc
            for ai, bi, ci, dn in pairs:
                d = _dot(a_refs[ai][...], b_refs[bi][...], dn)
                sums[ci] = d if sums[ci] is None else sums[ci] + d
            return sums

        def finish(accs):
            res = epilogue(accs, [e[...] for e in e_refs])
            for o, r in zip(o_refs, res):
                o[...] = r.astype(o.dtype)

        if nk == 1:
            finish(products())
            return

        @pl.when(k == 0)
        def _():
            for acc, d in zip(acc_refs, products()):
                acc[...] = d

        @pl.when(jnp.logical_and(k > 0, k < nk - 1))
        def _():
            for acc, d in zip(acc_refs, products()):
                acc[...] += d

        @pl.when(k == nk - 1)
        def _():
            finish([acc[...] + d for acc, d in zip(acc_refs, products())])

    ops = list(a_ops) + list(b_ops) + list(extras)
    return pl.pallas_call(
        body, name=name, grid=grid,
        in_specs=[s for _, s in ops],
        out_specs=[s for _, s in outs],
        out_shape=[s for s, _ in outs],
        scratch_shapes=[pltpu.VMEM(acc_shape, F32) for _ in range(n_acc if nk > 1 else 0)],
        compiler_params=_cparams(("parallel", "parallel", "arbitrary")),
    )(*[a for a, _ in ops])


def _bs(shape, fn):
    return pl.BlockSpec(shape, fn)


def _tile_m(L):
    return min(L, 512)


def _mm_nn(name, x, ws, tk, tn, n_out, extras, outs_sd, epilogue, tm=None, cb_offsets=None):
    M, K = x.shape
    tm = min(M, tm or _tile_m(M))
    grid = (M // tm, n_out // tn, K // tk)
    a_ops = [(x, _bs((tm, tk), lambda i, j, k: (i, k)))]
    cb_offsets = cb_offsets or [0] * len(ws)
    b_ops = [(w.a, _wspec(w, tk, tn, lambda i, j, k: k, (lambda off: lambda i, j, k: j + off)(off)))
             for w, off in zip(ws, cb_offsets)]
    pairs = [(0, bi, bi, _NN) for bi in range(len(ws))]
    mn = _bs((tm, tn), lambda i, j, k: (i, j))
    ex = [(e, mn) for e in extras]
    outs = [(jax.ShapeDtypeStruct((M, n_out), dt), mn) for dt in outs_sd]
    return _mm(name, grid, a_ops, b_ops, pairs, (tm, tn), len(ws), ex, outs, epilogue)


def _mm_nt(name, xs, ws, tn, tk, extras, outs_sd, epilogue, tm=None):
    M, Nw = xs[0].shape
    Kw = ws[0].full_shape()[0]
    tm = min(M, tm or _tile_m(M))
    grid = (M // tm, Kw // tn, Nw // tk)
    a_ops = [(x, _bs((tm, tk), lambda i, j, k: (i, k))) for x in xs]
    b_ops = [(w.a, _wspec(w, tn, tk, lambda i, j, k: j, lambda i, j, k: k)) for w in ws]
    pairs = [(i, i, 0, _NT) for i in range(len(ws))]
    mn = _bs((tm, tn), lambda i, j, k: (i, j))
    ex = [(e, mn) for e in extras]
    outs = [(jax.ShapeDtypeStruct((M, Kw), dt), mn) for dt in outs_sd]
    return _mm(name, grid, a_ops, b_ops, pairs, (tm, tn), 1, ex, outs, epilogue)


def _mm_tn(name, x, dys, kind, R, C, tm, tn, tk=None):
    L, Kw = x.shape
    Nw = dys[0].shape[1]
    tk = tk or min(L, 1024)
    grid = (Kw // tm, Nw // tn, L // tk)
    a_ops = [(x, _bs((tk, tm), lambda i, j, k: (k, i)))]
    b_ops = [(dy, _bs((tk, tn), lambda i, j, k: (k, j))) for dy in dys]
    pairs = [(0, bi, bi, _TN) for bi in range(len(dys))]
    gs = _gspec(kind, R, C, tm, tn)
    outs = [(jax.ShapeDtypeStruct((2, 4, R, C), _WIRE), gs) for _ in dys]
    return _mm(name, grid, a_ops, b_ops, pairs, (tm, tn), len(dys), [], outs, lambda accs, ex: accs)


def _row_tile(L):
    return min(L, 256)


def _rowwise(name, body, ins, outs, L, acc_outs=()):
    tr = _row_tile(L)
    n_in, n_out = len(ins), len(outs)

    def kbody(*refs):
        i_refs, o_refs, a_refs = refs[:n_in], refs[n_in:n_in + n_out], refs[n_in + n_out:]
        res, sums = body(*[r[...] for r in i_refs])
        for o, r in zip(o_refs, res):
            o[...] = r.astype(o.dtype)
        if a_refs:
            @pl.when(pl.program_id(0) == 0)
            def _():
                for a in a_refs:
                    a[...] = jnp.zeros(a.shape, F32)
            for a, s in zip(a_refs, sums):
                a[...] += s

    in_specs = []
    for arr, kind in ins:
        if kind == "row":
            in_specs.append(pl.BlockSpec((tr, arr.shape[1]), lambda i: (i, 0)))
        else:
            in_specs.append(pl.BlockSpec(arr.shape, lambda i: (0, 0)))
    out_specs = [pl.BlockSpec((tr, c), lambda i: (i, 0)) for c, _ in outs]
    out_shape = [jax.ShapeDtypeStruct((L, c), dt) for c, dt in outs]
    out_specs += [pl.BlockSpec((1, c), lambda i: (0, 0)) for c in acc_outs]
    out_shape += [jax.ShapeDtypeStruct((1, c), F32) for c in acc_outs]
    return pl.pallas_call(
        kbody, name=name, grid=(L // tr,), in_specs=in_specs, out_specs=out_specs, out_shape=out_shape,
        compiler_params=_cparams(("arbitrary",)),
    )(*[a for a, _ in ins])


def _rms_fwd(name, x, g):
    def body(xv, gv):
        r = lax.rsqrt(jnp.mean(xv * xv, axis=-1, keepdims=True) + _EPS)
        return [xv * r * gv], []
    return _rowwise(name, body, [(x, "row"), (g, "vec")], [(x.shape[1], _MXU)], x.shape[0])[0]


def _rms_bwd(name, dh, x, g, dres):
    def body(dhv, xv, gv, dr):
        r = lax.rsqrt(jnp.mean(xv * xv, axis=-1, keepdims=True) + _EPS)
        xh = xv * r
        dxh = dhv * gv
        dx = dr + r * (dxh - xh * jnp.mean(dxh * xh, axis=-1, keepdims=True))
        return [dx, dx], [jnp.sum(dhv * xh, axis=0, keepdims=True)]
    D = x.shape[1]
    return _rowwise(name, body, [(dh, "row"), (x, "row"), (g, "vec"), (dres, "row")], [(D, F32), (D, _MXU)],
                    x.shape[0], [D])


def _loss_head(x, g, target):
    D = x.shape[1]

    def body(xv, gv, tv):
        r = lax.rsqrt(jnp.mean(xv * xv, axis=-1, keepdims=True) + _EPS)
        xh = xv * r
        e = xh * gv - tv
        dy = e * (1.0 / D)
        dxh = dy * gv
        dx = r * (dxh - xh * jnp.mean(dxh * xh, axis=-1, keepdims=True))
        row_loss = 0.5 * jnp.mean(e * e, axis=-1, keepdims=True)
        lsum = jnp.sum(row_loss, axis=0, keepdims=True) + jnp.zeros((1, _LANES), F32)
        return [dx], [jnp.sum(dy * xh, axis=0, keepdims=True), lsum]
    return _rowwise("loss_head", body, [(x, "row"), (g, "vec"), (target, "row")], [(D, F32)], x.shape[0], [D, _LANES])


def _ple_bwd_elem(dx, pp, gt):
    def body(dxv, ppv, gtv):
        gt32 = gtv.astype(F32)
        return [dxv * ppv * gt32 * (1.0 - gt32), dxv * gt32], []
    D = dx.shape[1]
    return _rowwise("ple_bwd_elem", body, [(dx, "row"), (pp, "row"), (gt, "row")], [(D, _MXU), (D, _MXU)], dx.shape[0])


def _glu_bwd_elem(dx, val, sg):
    def body(dxv, vv, sv):
        v32, s32 = vv.astype(F32), sv.astype(F32)
        return [jnp.concatenate([dxv * s32, dxv * v32 * s32 * (1.0 - s32)], axis=1)], []
    D = dx.shape[1]
    return _rowwise("glu_bwd_elem", body, [(dx, "row"), (val, "row"), (sg, "row")], [(2 * D, _MXU)], dx.shape[0])[0]


def _gm_common(z, ln_g, ln_b, wc_bf, bsT):
    W = z.shape[1] // 2
    zu, zv = z[:, :W], z[:, W:]
    u, v = _gelu(zu), _gelu(zv)
    mu = jnp.mean(v, axis=-1, keepdims=True)
    vc = v - mu
    rstd = lax.rsqrt(jnp.mean(vc * vc, axis=-1, keepdims=True) + _EPS)
    vh = vc * rstd
    vn = vh * ln_g + ln_b
    vnb = vn.astype(_MXU)
    svs = []
    for h in range(_GM_HEADS):
        sl = slice(h * _LANES, (h + 1) * _LANES)
        svs.append(_dot(wc_bf[h], vnb[:, sl], _NN) + bsT[:, h:h + 1])
    return zu, zv, u, vh, rstd, vnb, svs


def _causal(w):
    t = lax.broadcasted_iota(jnp.int32, w.shape, w.ndim - 2)
    s = lax.broadcasted_iota(jnp.int32, w.shape, w.ndim - 1)
    return jnp.where(s <= t, w, jnp.zeros_like(w))


def _gmlp_fwd(z, ln_g, ln_b, w_s, bsT):
    L, W2 = z.shape
    W = W2 // 2

    def body(z_ref, g_ref, b_ref, ws_ref, bs_ref, m_ref):
        wc = _causal(ws_ref[...]).astype(_MXU)
        _, _, u, _, _, _, svs = _gm_common(z_ref[...], g_ref[...], b_ref[...], wc, bs_ref[...])
        for h in range(_GM_HEADS):
            sl = slice(h * _LANES, (h + 1) * _LANES)
            m_ref[:, sl] = (u[:, sl] * svs[h]).astype(m_ref.dtype)

    return pl.pallas_call(
        body, name="gmlp_fwd", grid=(L // _GM_CHUNK,),
        in_specs=[pl.BlockSpec((_GM_CHUNK, W2), lambda n: (n, 0)),
                  pl.BlockSpec((1, W), lambda n: (0, 0)), pl.BlockSpec((1, W), lambda n: (0, 0)),
                  pl.BlockSpec(w_s.shape, lambda n: (0, 0, 0)), pl.BlockSpec(bsT.shape, lambda n: (0, 0))],
        out_specs=pl.BlockSpec((_GM_CHUNK, W), lambda n: (n, 0)),
        out_shape=jax.ShapeDtypeStruct((L, W), _MXU),
        compiler_params=_cparams(("arbitrary",)),
    )(z, ln_g, ln_b, w_s, bsT)


def _gmlp_bwd(z, dm, ln_g, ln_b, w_s, bsT):
    L, W2 = z.shape
    W = W2 // 2
    T = _GM_CHUNK

    def body(z_ref, dm_ref, g_ref, b_ref, ws_ref, bs_ref, dz_ref, dws_ref, dbs_ref, dg_ref, db_ref):
        @pl.when(pl.program_id(0) == 0)
        def _():
            dws_ref[...] = jnp.zeros(dws_ref.shape, F32)
            dbs_ref[...] = jnp.zeros(dbs_ref.shape, F32)
            dg_ref[...] = jnp.zeros(dg_ref.shape, F32)
            db_ref[...] = jnp.zeros(db_ref.shape, F32)

        wc = _causal(ws_ref[...]).astype(_MXU)
        ln_g_v = g_ref[...]
        zu, zv, u, vh, rstd, vnb, svs = _gm_common(z_ref[...], ln_g_v, b_ref[...], wc, bs_ref[...])
        dmv = dm_ref[...]
        lane = lax.broadcasted_iota(jnp.int32, (T, _LANES), 1)
        dbs = jnp.zeros((T, _LANES), F32)
        dvn_parts = []
        for h in range(_GM_HEADS):
            sl = slice(h * _LANES, (h + 1) * _LANES)
            dsv = dmv[:, sl] * u[:, sl]
            dz_ref[:, sl] = (dmv[:, sl] * svs[h] * _gelu_grad(zu[:, sl])).astype(dz_ref.dtype)
            dbs = dbs + jnp.where(lane == h, jnp.sum(dsv, axis=1, keepdims=True), 0.0)
            dsvb = dsv.astype(_MXU)
            dws_ref[h] += _dot(dsvb, vnb[:, sl], _NT)
            dvn_parts.append(_dot(wc[h], dsvb, _TN))
        dbs_ref[...] += dbs
        dvn = jnp.concatenate(dvn_parts, axis=1)
        dg_ref[...] += jnp.sum(dvn * vh, axis=0, keepdims=True)
        db_ref[...] += jnp.sum(dvn, axis=0, keepdims=True)
        dxh = dvn * ln_g_v
        dv = rstd * (dxh - jnp.mean(dxh, axis=-1, keepdims=True) - vh * jnp.mean(dxh * vh, axis=-1, keepdims=True))
        dz_ref[:, W:] = (dv * _gelu_grad(zv)).astype(dz_ref.dtype)

        @pl.when(pl.program_id(0) == pl.num_programs(0) - 1)
        def _():
            dws_ref[...] = _causal(dws_ref[...])

    return pl.pallas_call(
        body, name="gmlp_bwd", grid=(L // T,),
        in_specs=[pl.BlockSpec((T, W2), lambda n: (n, 0)), pl.BlockSpec((T, W), lambda n: (n, 0)),
                  pl.BlockSpec((1, W), lambda n: (0, 0)), pl.BlockSpec((1, W), lambda n: (0, 0)),
                  pl.BlockSpec(w_s.shape, lambda n: (0, 0, 0)), pl.BlockSpec(bsT.shape, lambda n: (0, 0))],
        out_specs=[pl.BlockSpec((T, W2), lambda n: (n, 0)),
                   pl.BlockSpec(w_s.shape, lambda n: (0, 0, 0)), pl.BlockSpec((T, _LANES), lambda n: (0, 0)),
                   pl.BlockSpec((1, W), lambda n: (0, 0)), pl.BlockSpec((1, W), lambda n: (0, 0))],
        out_shape=[jax.ShapeDtypeStruct((L, W2), _MXU), jax.ShapeDtypeStruct(w_s.shape, F32),
                   jax.ShapeDtypeStruct((T, _LANES), F32),
                   jax.ShapeDtypeStruct((1, W), F32), jax.ShapeDtypeStruct((1, W), F32)],
        compiler_params=_cparams(("arbitrary",)),
    )(z, dm, ln_g, ln_b, w_s, bsT)


def _s5_prep_math(a_re, a_im, log_dt):
    dt = jnp.exp(log_dt)
    xr, xi = a_re * dt, a_im * dt
    e = jnp.exp(xr)
    lbr, lbi = e * jnp.cos(xi), e * jnp.sin(xi)
    dn = a_re * a_re + a_im * a_im
    nr, ni = lbr - 1.0, lbi
    pr, pi = nr * a_re + ni * a_im, ni * a_re - nr * a_im
    return dt, lbr, lbi, dn, nr, ni, pr, pi


def _vm():
    return pl.BlockSpec(memory_space=pltpu.VMEM)


def _s5_prep(a_re, a_im, log_dt, b_re, b_im):
    def body(ar_ref, ai_ref, ld_ref, br_ref, bi_ref, lbr_ref, lbi_ref, Br_ref, Bi_ref):
        _, lbr, lbi, dn, _, _, pr, pi = _s5_prep_math(ar_ref[...], ai_ref[...], ld_ref[...])
        cr, ci = (pr / dn)[:, None, :], (pi / dn)[:, None, :]
        lbr_ref[...] = lbr
        lbi_ref[...] = lbi
        br, bi = br_ref[...], bi_ref[...]
        Br_ref[...] = cr * br - ci * bi
        Bi_ref[...] = cr * bi + ci * br

    sd = jax.ShapeDtypeStruct
    return pl.pallas_call(
        body, name="s5_prep", in_specs=[_vm()] * 5, out_specs=[_vm()] * 4,
        out_shape=[sd(a_re.shape, F32), sd(a_re.shape, F32), sd(b_re.shape, F32), sd(b_re.shape, F32)],
    )(a_re, a_im, log_dt, b_re, b_im)


def _s5_prep_bwd(a_re, a_im, log_dt, b_re, b_im, dlbr_s, dlbi_s, dBr, dBi):
    def body(ar_ref, ai_ref, ld_ref, br_ref, bi_ref, dlr_ref, dli_ref, dBr_ref, dBi_ref,
             dar_ref, dai_ref, dld_ref, dbr_ref, dbi_ref):
        a_re_v, a_im_v = ar_ref[...], ai_ref[...]
        dt, lbr, lbi, dn, nr, ni, pr, pi = _s5_prep_math(a_re_v, a_im_v, ld_ref[...])
        cr, ci = (pr / dn)[:, None, :], (pi / dn)[:, None, :]
        br, bi, dBr_v, dBi_v = br_ref[...], bi_ref[...], dBr_ref[...], dBi_ref[...]
        dbr_ref[...] = cr * dBr_v + ci * dBi_v
        dbi_ref[...] = cr * dBi_v - ci * dBr_v
        dcr = jnp.sum(br * dBr_v + bi * dBi_v, axis=1)
        dci = jnp.sum(br * dBi_v - bi * dBr_v, axis=1)
        dpr, dpi = dcr / dn, dci / dn
        ddn = -(dcr * pr + dci * pi) / (dn * dn)
        dnr = dpr * a_re_v - dpi * a_im_v
        dni = dpr * a_im_v + dpi * a_re_v
        dlbr = dlr_ref[...] + dnr
        dlbi = dli_ref[...] + dni
        dxr = dlbr * lbr + dlbi * lbi
        dxi = dlbi * lbr - dlbr * lbi
        dar_ref[...] = dpr * nr + dpi * ni + 2.0 * ddn * a_re_v + dxr * dt
        dai_ref[...] = dpr * ni - dpi * nr + 2.0 * ddn * a_im_v + dxi * dt
        dld_ref[...] = jnp.sum(dxr * a_re_v + dxi * a_im_v, axis=0, keepdims=True) * dt

    sd = jax.ShapeDtypeStruct
    return pl.pallas_call(
        body, name="s5_prep_bwd", in_specs=[_vm()] * 9, out_specs=[_vm()] * 5,
        out_shape=[sd(a_re.shape, F32), sd(a_re.shape, F32), sd(log_dt.shape, F32),
                   sd(b_re.shape, F32), sd(b_re.shape, F32)],
    )(a_re, a_im, log_dt, b_re, b_im, dlbr_s, dlbi_s, dBr, dBi)


def _shift_rows(v, down):
    n = v.shape[0]
    rolled = pltpu.roll(v, 1 if down else n - 1, 0)
    row = lax.broadcasted_iota(jnp.int32, v.shape, 0)
    return jnp.where(row == (0 if down else n - 1), 0.0, rolled)


def _cmul(ar, ai, br, bi):
    return ar * br - ai * bi, ar * bi + ai * br


_SEG = 8
_UNROLL = 8


def _seg_rows(k):
    if isinstance(k, int):
        return pl.ds(k * _SEG, _SEG)
    return pl.ds(pl.multiple_of(k * _SEG, _SEG), _SEG)


def _unrolled(n, step, init):
    main = n // _UNROLL

    def trip(kk, s):
        for uu in range(_UNROLL):
            s = step(kk * _UNROLL + uu, s)
        return s

    s = lax.fori_loop(0, main, trip, init)
    for r in range(main * _UNROLL, n):
        s = step(r, s)
    return s


def _interleave(src_ref, dst_ref, nk):
    def step(k, carry):
        dst_ref[_seg_rows(k), :] = src_ref[pl.ds(k, _SEG, stride=nk), :]
        return carry
    _unrolled(nk, step, 0)


def _deinterleave(src_ref, dst_ref, nk):
    def step(k, carry):
        dst_ref[pl.ds(k, _SEG, stride=nk), :] = src_ref[_seg_rows(k), :]
        return carry
    _unrolled(nk, step, 0)


def _segment_inits(er, ei, ar, ai, nk, down):
    pr, pi = ar, ai
    for _ in range(int(math.log2(nk))):
        pr, pi = _cmul(pr, pi, pr, pi)
    fr, fi = er, ei
    for _ in range(_SEG - 1):
        sr, si = _shift_rows(fr, down), _shift_rows(fi, down)
        mr, mi = _cmul(pr, pi, sr, si)
        fr, fi = er + mr, ei + mi
    return _shift_rows(fr, down), _shift_rows(fi, down)


def _scan_states(x_re, x_im, ar, ai, nk):
    lanes = ar.shape[1]

    def step(k, s):
        rows = _seg_rows(k)
        mr, mi = _cmul(ar, ai, s[0], s[1])
        return mr + x_re[rows, :], mi + x_im[rows, :]

    zero = jnp.zeros((_SEG, lanes), F32)
    er, ei = _unrolled(nk, step, (zero, zero))
    ir, ii = _segment_inits(er, ei, ar, ai, nk, True)

    def step2(k, s):
        rows = _seg_rows(k)
        mr, mi = _cmul(ar, ai, s[0], s[1])
        nr, ni = mr + x_re[rows, :], mi + x_im[rows, :]
        x_re[rows, :] = nr
        x_im[rows, :] = ni
        return nr, ni

    _unrolled(nk, step2, (ir, ii))


def _s5_tile_fwd(u, bd_re, bd_im, cd_re, cd_im, ar, ai, d, s_re, s_im, nk):
    s_re[...] = _dot(u, bd_re, _NN)
    s_im[...] = _dot(u, bd_im, _NN)
    _scan_states(s_re, s_im, ar, ai, nk)
    return _dot(s_re[...], cd_re, _NN) - _dot(s_im[...], cd_im, _NN) + d * u


def _s5_specs(L, T):
    lanes = _S5_GT * _S5_P
    u_spec = pl.BlockSpec((L, _LANES), lambda t: (0, t))
    bd_spec = pl.BlockSpec((None, _S5_GT, _S5_C, _S5_P), lambda t: (t, 0, 0, 0))
    cd_spec = pl.BlockSpec((None, _S5_GT, _S5_P, _S5_C), lambda t: (t, 0, 0, 0))
    lam_spec = pl.BlockSpec((None, 1, lanes), lambda t: (t, 0, 0))
    d_spec = pl.BlockSpec((1, _LANES), lambda t: (0, t))
    return lanes, u_spec, bd_spec, cd_spec, lam_spec, d_spec


def _fill_block_diag(dst_ref, blocks_ref):
    _, a, b = blocks_ref.shape
    dst_ref[...] = jnp.zeros(dst_ref.shape, F32)
    for g in range(_S5_GT):
        dst_ref[g * a:(g + 1) * a, g * b:(g + 1) * b] = blocks_ref[g]


def _take_block_diag(dst_ref, v):
    _, a, b = dst_ref.shape
    for g in range(_S5_GT):
        dst_ref[g] = v[g * a:(g + 1) * a, g * b:(g + 1) * b]


def _s5_dense(bdr, bdi, cdr, cdi, dense):
    for src, dst in zip((bdr, bdi, cdr, cdi), dense):
        _fill_block_diag(dst, src)
    return [dst[...] for dst in dense]


def _s5_dense_scratch(lanes):
    return [pltpu.VMEM((_LANES, lanes), F32), pltpu.VMEM((_LANES, lanes), F32),
            pltpu.VMEM((lanes, _LANES), F32), pltpu.VMEM((lanes, _LANES), F32)]


def _s5_fwd(u, bd_re, bd_im, cd_re, cd_im, lam_re, lam_im, d):
    L, Wd = u.shape
    T = Wd // _LANES
    nk = L // _SEG
    lanes, u_spec, bd_spec, cd_spec, lam_spec, d_spec = _s5_specs(L, T)
    s_spec = pl.BlockSpec((L, lanes), lambda t: (0, t))

    def body(u_ref, bdr, bdi, cdr, cdi, lr, li, d_ref, g_ref, s_re, s_im, up, tmp, *dense):
        ar = jnp.broadcast_to(lr[...], (_SEG, lanes))
        ai = jnp.broadcast_to(li[...], (_SEG, lanes))
        bd_re_v, bd_im_v, cd_re_v, cd_im_v = _s5_dense(bdr, bdi, cdr, cdi, dense)
        _interleave(u_ref, up, nk)
        y = _s5_tile_fwd(up[...], bd_re_v, bd_im_v, cd_re_v, cd_im_v, ar, ai, d_ref[...], s_re, s_im, nk)
        up[...] = _gelu(y)
        _deinterleave(up, tmp, nk)
        g_ref[...] = tmp[...].astype(g_ref.dtype)

    return pl.pallas_call(
        body, name="s5_fwd", grid=(T,),
        in_specs=[u_spec, bd_spec, bd_spec, cd_spec, cd_spec, lam_spec, lam_spec, d_spec],
        out_specs=[u_spec, s_spec, s_spec],
        out_shape=[jax.ShapeDtypeStruct((L, Wd), _MXU), jax.ShapeDtypeStruct((L, T * lanes), F32),
                   jax.ShapeDtypeStruct((L, T * lanes), F32)],
        scratch_shapes=[pltpu.VMEM((L, _LANES), F32) for _ in range(2)] + _s5_dense_scratch(lanes),
        compiler_params=_cparams(("arbitrary",)),
    )(u, bd_re, bd_im, cd_re, cd_im, lam_re, lam_im, d)


def _s5_bwd(u, dg, states_re, states_im, bd_re, bd_im, cd_re, cd_im, lam_re, lam_im, d):
    L, Wd = u.shape
    T = Wd // _LANES
    nk = L // _SEG
    lanes, u_spec, bd_spec, cd_spec, lam_spec, d_spec = _s5_specs(L, T)
    s_spec = pl.BlockSpec((L, lanes), lambda t: (0, t))

    def body(u_ref, dg_ref, s_re, s_im, bdr, bdi, cdr, cdi, lr, li, d_ref,
             du_ref, dbdr, dbdi, dcdr, dcdi, dlr, dli, dd_ref, g_re, g_im, up, dgp, tmp, *dense):
        ar = jnp.broadcast_to(lr[...], (_SEG, lanes))
        ai = jnp.broadcast_to(li[...], (_SEG, lanes))
        bd_re_v, bd_im_v, cd_re_v, cd_im_v = _s5_dense(bdr, bdi, cdr, cdi, dense)
        _interleave(u_ref, up, nk)
        _interleave(dg_ref, dgp, nk)
        uv, dv = up[...], d_ref[...]
        y = _dot(s_re[...], cd_re_v, _NN) - _dot(s_im[...], cd_im_v, _NN) + dv * uv
        dy = dgp[...] * _gelu_grad(y)
        dd_ref[...] = jnp.sum(dy * uv, axis=0, keepdims=True)
        dyb = dy.astype(_MXU)
        _take_block_diag(dcdr, _dot(dyb, s_re[...], _TN))
        _take_block_diag(dcdi, -_dot(dyb, s_im[...], _TN))
        g_re[...] = _dot(dyb, cd_re_v, _NT)
        g_im[...] = -_dot(dyb, cd_im_v, _NT)

        nai = -ai

        def step(j, s):
            rows = _seg_rows(nk - 1 - j)
            mr, mi = _cmul(ar, nai, s[0], s[1])
            return mr + g_re[rows, :], mi + g_im[rows, :]

        zero = jnp.zeros((_SEG, lanes), F32)
        er, ei = _unrolled(nk, step, (zero, zero))
        ir, ii = _segment_inits(er, ei, ar, nai, nk, False)

        def acc_lam(gr, gi, pr, pi, acc):
            return acc[0] + gr * pr + gi * pi, acc[1] + gi * pr - gr * pi

        def step2(j, carry):
            s, acc = carry
            k = nk - 1 - j
            rows = _seg_rows(k)
            mr, mi = _cmul(ar, nai, s[0], s[1])
            nr, ni = mr + g_re[rows, :], mi + g_im[rows, :]
            g_re[rows, :] = nr
            g_im[rows, :] = ni
            prev = _seg_rows(k - 1)
            return (nr, ni), acc_lam(nr, ni, s_re[prev, :], s_im[prev, :], acc)

        (g0r, g0i), acc = _unrolled(nk - 1, step2, ((ir, ii), (zero, zero)))
        first = _seg_rows(0)
        mr, mi = _cmul(ar, nai, g0r, g0i)
        nr, ni = mr + g_re[first, :], mi + g_im[first, :]
        g_re[first, :] = nr
        g_im[first, :] = ni
        last = _seg_rows(nk - 1)
        acc = acc_lam(nr, ni, _shift_rows(s_re[last, :], True), _shift_rows(s_im[last, :], True), acc)
        dlr[...] = jnp.sum(acc[0], axis=0, keepdims=True)
        dli[...] = jnp.sum(acc[1], axis=0, keepdims=True)

        gtr, gti = g_re[...].astype(_MXU), g_im[...].astype(_MXU)
        ub = uv.astype(_MXU)
        _take_block_diag(dbdr, _dot(ub, gtr, _TN))
        _take_block_diag(dbdi, _dot(ub, gti, _TN))
        dgp[...] = _dot(gtr, bd_re_v, _NT) + _dot(gti, bd_im_v, _NT) + dy * dv
        _deinterleave(dgp, tmp, nk)
        du_ref[...] = tmp[...].astype(du_ref.dtype)

    sd = jax.ShapeDtypeStruct
    big = sd((T, _S5_GT, _S5_C, _S5_P), F32)
    return pl.pallas_call(
        body, name="s5_bwd", grid=(T,),
        in_specs=[u_spec, u_spec, s_spec, s_spec, bd_spec, bd_spec, cd_spec, cd_spec, lam_spec, lam_spec, d_spec],
        out_specs=[u_spec, bd_spec, bd_spec, bd_spec, bd_spec, lam_spec, lam_spec, d_spec],
        out_shape=[sd((L, Wd), _MXU), big, big, big, big, sd((T, 1, lanes), F32), sd((T, 1, lanes), F32),
                   sd((1, Wd), F32)],
        scratch_shapes=[pltpu.VMEM((L, lanes), F32) for _ in range(2)]
        + [pltpu.VMEM((L, _LANES), F32) for _ in range(3)] + _s5_dense_scratch(lanes),
        compiler_params=_cparams(("arbitrary",)),
    )(u, dg, states_re, states_im, bd_re, bd_im, cd_re, cd_im, lam_re, lam_im, d)


def _half_tile(R, few_arrays=False):
    for t in ((512, 704, 128) if few_arrays else (256, 352, 128)):
        if R % t == 0:
            return t
    raise ValueError(R)


def _cast_shard(name, w, layer, kind, R, C):
    tr = _half_tile(R, True)
    nr = R // tr

    def body(w_ref, o_ref):
        o_ref[...] = w_ref[...].astype(o_ref.dtype)

    if kind == "col":
        in_map = lambda h, i: (layer, h * nr + i, 0)
    else:
        in_map = lambda h, i: (layer, i, h)
    return pl.pallas_call(
        body, name=name, grid=(2, nr), in_specs=[pl.BlockSpec((None, tr, C), in_map)],
        out_specs=pl.BlockSpec((None, tr, C), lambda h, i: (h, i, 0)),
        out_shape=jax.ShapeDtypeStruct((2, R, C), _WIRE),
        compiler_params=_cparams(("arbitrary", "arbitrary")),
    )(w)


def _adam_math(w, g, m, v):
    m2 = _B1 * m + (1.0 - _B1) * g
    v2 = _B2 * v + (1.0 - _B2) * (g * g)
    m_hat = m2 / (1.0 - _B1 ** _STEP)
    v_hat = v2 / (1.0 - _B2 ** _STEP)
    delta = -_LR * (m_hat / (jnp.sqrt(v_hat) + _AEPS) + _WD * w)
    return delta, m2, v2


def _adamw_big(name, w, m, v, layer, pair, kind, R, C, c, after, prev):
    tr = _half_tile(R, few_arrays=C <= 1024 and R % 512 == 0)
    nr = R // tr

    def body(c_ref, w_ref, m_ref, v_ref, own_ref, other_ref, *rest):
        go_ref, d_ref, mo_ref, vo_ref = rest[-4:]
        g = jnp.where(pl.program_id(0) == c_ref[0], own_ref[...], other_ref[...])
        delta, m2, v2 = _adam_math(w_ref[...], g, m_ref[...], v_ref[...])
        go_ref[...] = g
        d_ref[...] = delta
        mo_ref[...] = m2
        vo_ref[...] = v2

    if kind == "col":
        nat = pl.BlockSpec((None, tr, C), lambda h, i, c_ref: (layer, h * nr + i, 0))
    else:
        nat = pl.BlockSpec((None, tr, C), lambda h, i, c_ref: (layer, i, h))

    def gspec(own):
        return pl.BlockSpec((tr, C), lambda h, i, c_ref: (jnp.where((h == c_ref[0]) == own, i, 0), 0))

    carried = list(prev) if prev is not None else []
    gs = pltpu.PrefetchScalarGridSpec(
        num_scalar_prefetch=1, grid=(2, nr),
        in_specs=[nat, nat, nat, gspec(True), gspec(False), _any()] + [_any()] * len(carried),
        out_specs=[nat] * 4)
    sd = jax.ShapeDtypeStruct(w.shape, F32)
    return pl.pallas_call(
        body, name=name, grid_spec=gs, out_shape=[sd] * 4,
        input_output_aliases={7 + k: k for k in range(len(carried))},
        compiler_params=_cparams(("arbitrary", "arbitrary")),
    )(c.astype(jnp.int32).reshape(1), w, m, v, pair[0], pair[1], after, *carried)


def _adamw_flat(w, g, m, v):
    rows = w.shape[0]
    tr = rows // 8 if rows % 64 == 0 else rows

    def body(w_ref, g_ref, m_ref, v_ref, d_ref, mo_ref, vo_ref):
        delta, m2, v2 = _adam_math(w_ref[...], g_ref[...], m_ref[...], v_ref[...])
        d_ref[...] = delta
        mo_ref[...] = m2
        vo_ref[...] = v2

    spec = pl.BlockSpec((tr, _LANES), lambda i: (i, 0))
    sd = jax.ShapeDtypeStruct(w.shape, F32)
    return pl.pallas_call(
        body, name="adamw_small", grid=(rows // tr,), in_specs=[spec] * 4, out_specs=[spec] * 3,
        out_shape=[sd] * 3, compiler_params=_cparams(("arbitrary",)),
    )(w, g, m, v)


def _place():
    x, y, c = lax.axis_index("x"), lax.axis_index("y"), lax.axis_index("c")
    chips = [(1 - x, y), (x, 1 - y), (1 - x, 1 - y)]
    return x, y, c, 2 * x + y, chips


def _any():
    return pl.BlockSpec(memory_space=pl.ANY)


def _remote(src, dst, ssem, rsem, dev):
    return pltpu.make_async_remote_copy(src_ref=src, dst_ref=dst, send_sem=ssem, recv_sem=rsem,
                                        device_id=dev, device_id_type=_MESH)


def _allgather(name, shards):
    n = len(shards)

    def body(*refs):
        s_refs, g_refs = refs[:n], refs[n:2 * n]
        send0, recv0, send1, recv1, send2, recv2 = refs[2 * n:]
        x, y, c, q, _ = _place()
        sib, xn, yn = (x, y, 1 - c), (1 - x, y, c), (x, 1 - y, c)
        qx, qy, qd = 2 * (1 - x) + y, 2 * x + (1 - y), 2 * (1 - x) + (1 - y)
        _handshake([sib, xn, yn])
        own = [_remote(s_refs[a], g_refs[a].at[q], send0.at[a], recv0.at[a], sib) for a in range(n)]

        def pieces(a):
            g, half = g_refs[a], s_refs[a].shape[1] // 2
            return [g.at[qx, c], g.at[qy, c], g.at[qd, c, pl.ds(0, half)], g.at[qd, c, pl.ds(half, half)]]

        def relayed(a):
            g, half = g_refs[a], s_refs[a].shape[1] // 2
            return [(g.at[qx, c, pl.ds(0, half)], yn), (g.at[qy, c, pl.ds(half, half)], xn)]

        first = []
        for a in range(n):
            first.append(_remote(s_refs[a].at[c], g_refs[a].at[q, c], send1.at[4 * a], recv1.at[4 * a], xn))
            first.append(_remote(s_refs[a].at[c], g_refs[a].at[q, c], send1.at[4 * a + 1], recv1.at[4 * a + 1], yn))
        for cp in first + own:
            cp.start()
        later = []
        for a in range(n):
            land = pieces(a)
            for j in range(4):
                k = 4 * a + j
                _remote(land[j], land[j], send1.at[k], recv1.at[k], xn).wait_recv()
                if j < 2:
                    src, to = relayed(a)[j]
                    cp = _remote(src, src, send1.at[k + 2], recv1.at[k + 2], to)
                    cp.start()
                    later.append(cp)
                cp = _remote(land[j], land[j], send2.at[k], recv2.at[k], sib)
                cp.start()
                later.append(cp)
        for a in range(n):
            g, half = g_refs[a], s_refs[a].shape[1] // 2
            theirs = [g.at[qx, 1 - c], g.at[qy, 1 - c], g.at[qd, 1 - c, pl.ds(0, half)],
                      g.at[qd, 1 - c, pl.ds(half, half)]]
            for j in range(4):
                _remote(theirs[j], theirs[j], send2.at[4 * a + j], recv2.at[4 * a + j], sib).wait_recv()
        for cp in own:
            cp.wait()
        for cp in first + later:
            cp.wait_send()

    return _sequencer(name, _ID_GATHER, body, shards,
                      [jax.ShapeDtypeStruct((4,) + s.shape, s.dtype) for s in shards], [n, n] + [4 * n] * 4)


def _handshake(peers):
    barrier = pltpu.get_barrier_semaphore()
    for peer in peers:
        pl.semaphore_signal(barrier, inc=1, device_id=peer, device_id_type=_MESH)
    pl.semaphore_wait(barrier, len(peers))


_ID_SIBLING, _ID_CHIPS, _ID_GATHER, _ID_ALL = 1, 2, 3, 4


def _sequencer(name, collective_id, body, ins, out_types, sem_counts):
    mesh = plsc.ScalarSubcoreMesh(axis_name="seq", num_cores=1)
    moved = sum(math.prod(o.shape) * jnp.dtype(o.dtype).itemsize for o in out_types)
    return pl.kernel(
        body, name=name, out_type=out_types, mesh=mesh,
        scratch_types=[pltpu.SemaphoreType.DMA((k,)) for k in sem_counts],
        compiler_params=pltpu.CompilerParams(collective_id=collective_id),
        cost_estimate=pl.CostEstimate(flops=0, transcendentals=0, bytes_accessed=2 * moved,
                                      remote_bytes_transferred=moved),
    )(*ins)


def _swap_halves(name, grads):
    n = len(grads)

    def body(*refs):
        g_refs, t_refs = refs[:n], refs[n:2 * n]
        send, recv = refs[2 * n:]
        x, y, c, _, _ = _place()
        _handshake([(x, y, 1 - c)])
        cps = [_remote(g_refs[a].at[1 - c], t_refs[a], send.at[a], recv.at[a], (x, y, 1 - c)) for a in range(n)]
        for cp in cps:
            cp.start()
        for cp in cps:
            cp.wait()

    return _sequencer(name, _ID_SIBLING, body, grads,
                      [jax.ShapeDtypeStruct(g.shape[1:], g.dtype) for g in grads], [n, n])


def _chip_sum(name, g, t, after):
    _, _, R, C = g.shape
    tr = _half_tile(R, True)

    def body(c_ref, g_ref, t_ref, after_ref, o_ref):
        o_ref[...] = (g_ref[...].astype(F32) + t_ref[...].astype(F32)).astype(o_ref.dtype)

    c = lax.axis_index("c").astype(jnp.int32).reshape(1)
    gs = pltpu.PrefetchScalarGridSpec(
        num_scalar_prefetch=1, grid=(4, R // tr),
        in_specs=[pl.BlockSpec((None, None, tr, C), lambda r, i, c_ref: (c_ref[0], r, i, 0)),
                  pl.BlockSpec((None, tr, C), lambda r, i, c_ref: (r, i, 0)), _any()],
        out_specs=pl.BlockSpec((None, tr, C), lambda r, i, c_ref: (r, i, 0)))
    return pl.pallas_call(
        body, name=name, grid_spec=gs, out_shape=jax.ShapeDtypeStruct((4, R, C), _WIRE),
        compiler_params=_cparams(("arbitrary", "arbitrary")),
    )(c, g, t, after)


def _scatter_parts(name, parts):
    n = len(parts)

    def body(*refs):
        p_refs, t_refs = refs[:n], refs[n:2 * n]
        send, recv = refs[2 * n:]
        x, y, c, q, chips = _place()
        _handshake([(rx, ry, c) for rx, ry in chips])
        cps = []
        for a in range(n):
            for j, (rx, ry) in enumerate(chips):
                k = 3 * a + j
                cps.append(_remote(p_refs[a].at[2 * rx + ry], t_refs[a].at[q], send.at[k], recv.at[k], (rx, ry, c)))
        for cp in cps:
            cp.start()
        for a in range(n):
            for j, (rx, ry) in enumerate(chips):
                k = 3 * a + j
                land = t_refs[a].at[2 * rx + ry]
                _remote(land, land, send.at[k], recv.at[k], (rx, ry, c)).wait_recv()
        for cp in cps:
            cp.wait_send()

    return _sequencer(name, _ID_CHIPS, body, parts,
                      [jax.ShapeDtypeStruct(p.shape, p.dtype) for p in parts], [3 * n, 3 * n])


def _sum_parts(name, p, t, where, after):
    _, R, C = t.shape
    tr = _half_tile(R, True)

    def body(w_ref, p_ref, t0_ref, t1_ref, t2_ref, after_ref, o_ref):
        o_ref[...] = (p_ref[...].astype(F32) + t0_ref[...].astype(F32)
                      + t1_ref[...].astype(F32) + t2_ref[...].astype(F32))

    def part(slot):
        return pl.BlockSpec((None, tr, C), lambda i, w_ref: (w_ref[slot], i, 0))

    gs = pltpu.PrefetchScalarGridSpec(
        num_scalar_prefetch=1, grid=(R // tr,), in_specs=[part(0), part(1), part(2), part(3), _any()],
        out_specs=pl.BlockSpec((tr, C), lambda i, w_ref: (i, 0)))
    return pl.pallas_call(
        body, name=name, grid_spec=gs, out_shape=jax.ShapeDtypeStruct((R, C), F32),
        compiler_params=_cparams(("arbitrary",)),
    )(where, p, t, t, t, after)


def _send_halves(name, halves):
    n = len(halves)

    def body(*refs):
        h_refs, o_refs = refs[:n], refs[n:2 * n]
        send, recv = refs[2 * n:]
        x, y, c, _, _ = _place()
        _handshake([(x, y, 1 - c)])
        cps = [_remote(h_refs[a], o_refs[a], send.at[a], recv.at[a], (x, y, 1 - c)) for a in range(n)]
        for cp in cps:
            cp.start()
        for cp in cps:
            cp.wait()

    return _sequencer(name, _ID_SIBLING, body, halves,
                      [jax.ShapeDtypeStruct(h.shape, h.dtype) for h in halves], [n, n])


class _Order:
    def __init__(self):
        self.tok = None

    def tie(self, x):
        return x if self.tok is None else lax.optimization_barrier((x, self.tok))[0]

    def done(self, outs):
        self.tok = outs[0]
        return outs


class _Reduction:
    def __init__(self, tag, grads):
        self.tag, self.grads = tag, grads
        self.swapped = _swap_halves(f"rs_swap_{tag}", grads)

    def scatter(self, seq):
        self.parts = [seq.done([_chip_sum(f"rs_chipsum_{self.tag}_{a}", g, t, seq.tok)])[0]
                      for a, (g, t) in enumerate(zip(self.grads, self.swapped))]
        self.landed = _scatter_parts(f"rs_scatter_{self.tag}", self.parts)

    def finish(self, seq):
        x, y, c, q, chips = _place()
        where = jnp.stack([q] + [2 * rx + ry for rx, ry in chips]).astype(jnp.int32)
        halves = [seq.done([_sum_parts(f"rs_sum_{self.tag}_{a}", p, t, where, seq.tok)])[0]
                  for a, (p, t) in enumerate(zip(self.parts, self.landed))]
        return list(zip(halves, _send_halves(f"rs_join_{self.tag}", halves)))


def _allreduce_small(name, v):
    _, R, _ = v.shape

    def body(v_ref, o_ref, land, acc, send1, recv1, send2, recv2):
        x, y, c = lax.axis_index("x"), lax.axis_index("y"), lax.axis_index("c")
        me = 4 * x + 2 * y + c
        peers = []
        for k in range(1, 8):
            dx, dy, dc = (k >> 2) & 1, (k >> 1) & 1, k & 1
            px, py, pc = (1 - x if dx else x), (1 - y if dy else y), (1 - c if dc else c)
            peers.append((k, (px, py, pc), 4 * px + 2 * py + pc))
        land[me] = v_ref[me]
        out1 = [_remote(v_ref.at[pid], land.at[me], send1.at[k], recv1.at[k], dev) for k, dev, pid in peers]
        for cp in out1:
            cp.start()
        for k, dev, pid in peers:
            _remote(land.at[pid], land.at[pid], send1.at[k], recv1.at[k], dev).wait_recv()
        total = land[0]
        for j in range(1, 8):
            total = total + land[j]
        acc[...] = total
        o_ref[me] = total
        out2 = [_remote(acc, o_ref.at[me], send2.at[k], recv2.at[k], dev) for k, dev, pid in peers]
        for cp in out2:
            cp.start()
        for k, dev, pid in peers:
            _remote(o_ref.at[pid], o_ref.at[pid], send2.at[k], recv2.at[k], dev).wait_recv()
        for cp in out1 + out2:
            cp.wait_send()

    return pl.pallas_call(
        body, name=name, in_specs=[_vm()], out_specs=_vm(),
        out_shape=jax.ShapeDtypeStruct(v.shape, F32),
        scratch_shapes=[pltpu.VMEM(v.shape, F32), pltpu.VMEM((R, _LANES), F32)]
        + [pltpu.SemaphoreType.DMA((8,)) for _ in range(4)],
        compiler_params=pltpu.CompilerParams(vmem_limit_bytes=_VMEM_LIMIT),
    )(v)


def _all_peers():
    x, y, c = lax.axis_index("x"), lax.axis_index("y"), lax.axis_index("c")
    peers = []
    for k in range(1, 8):
        px, py, pc = (1 - x if k & 4 else x), (1 - y if k & 2 else y), (1 - c if k & 1 else c)
        peers.append((k, (px, py, pc), 4 * px + 2 * py + pc))
    return 4 * x + 2 * y + c, peers


def _exchange_slices(name, v):
    def body(v_ref, land, send, recv):
        me, peers = _all_peers()
        _handshake([dev for _, dev, _ in peers])
        cps = [_remote(v_ref.at[pid], land.at[me], send.at[k], recv.at[k], dev) for k, dev, pid in peers]
        for cp in cps:
            cp.start()
        for k, dev, pid in peers:
            _remote(land.at[pid], land.at[pid], send.at[k], recv.at[k], dev).wait_recv()
        for cp in cps:
            cp.wait_send()

    return _sequencer(name, _ID_ALL, body, [v], [jax.ShapeDtypeStruct(v.shape, v.dtype)], [8, 8])[0]


def _sum_slices(name, v, landed, after):
    _, R, _ = v.shape

    def body(v_ref, land_ref, after_ref, o_ref):
        me, peers = _all_peers()
        acc = v_ref[me]
        for _, _, pid in peers:
            acc = acc + land_ref[pid]
        o_ref[...] = acc

    return pl.pallas_call(
        body, name=name, in_specs=[_vm(), _vm(), _any()], out_specs=_vm(),
        out_shape=jax.ShapeDtypeStruct((R, _LANES), F32),
        compiler_params=pltpu.CompilerParams(vmem_limit_bytes=_VMEM_LIMIT),
    )(v, landed, after)


def _broadcast_slices(name, s):
    def body(s_ref, out, send, recv):
        me, peers = _all_peers()
        _handshake([dev for _, dev, _ in peers])
        cps = [_remote(s_ref, out.at[me], send.at[k], recv.at[k], dev) for k, dev, pid in peers]
        for cp in cps:
            cp.start()
        for k, dev, pid in peers:
            _remote(out.at[pid], out.at[pid], send.at[k], recv.at[k], dev).wait_recv()
        for cp in cps:
            cp.wait_send()

    return _sequencer(name, _ID_ALL, body, [s], [jax.ShapeDtypeStruct((8,) + s.shape, s.dtype)], [8, 8])[0]


_WEIGHT_NAMES = ['norm_mix', 'norm_ffn', 'norm_ple', 'norm_final', 'gm_w_in', 'gm_ln_g', 'gm_ln_b', 'gm_w_s',
                 'gm_b_s', 'gm_w_out', 's5_w_in', 's5_a_re', 's5_a_im', 's5_log_dt', 's5_b_re', 's5_b_im',
                 's5_c_re', 's5_c_im', 's5_d', 's5_w_out', 'ffn_w1', 'ffn_w3', 'ffn_w2', 'ple_w_gate', 'ple_w_proj']
_BIG = {'gm_w_in': 'col', 'gm_w_out': 'row', 's5_w_in': 'row', 's5_w_out': 'col', 'ffn_w1': 'col',
        'ffn_w3': 'col', 'ffn_w2': 'row', 'ple_w_gate': 'row', 'ple_w_proj': 'col'}


_VIEW = {'s5_a_re': (0, 2, 1), 's5_a_im': (0, 2, 1), 's5_b_re': (0, 2, 3, 1), 's5_b_im': (0, 2, 3, 1),
         's5_c_re': (0, 2, 3, 1), 's5_c_im': (0, 2, 3, 1)}


def _to_view(name, a):
    return jnp.transpose(a, _VIEW[name]) if name in _VIEW else a


def _from_view(name, a):
    if name not in _VIEW:
        return a
    perm = _VIEW[name]
    return jnp.transpose(a, [perm.index(i) for i in range(len(perm))])


def _rc(kind, shard_shape):
    rows, cols = shard_shape[-2:]
    return (rows // 2, cols) if kind == "col" else (rows, cols // 2)


def _pack(vecs, rows_multiple):
    flat = jnp.concatenate([a.reshape(-1).astype(F32) for a in vecs])
    unit = rows_multiple * _LANES
    pad = (-flat.shape[0]) % unit
    return jnp.pad(flat, (0, pad)).reshape(-1, _LANES)


def _unpack(buf, shapes):
    flat = buf.reshape(-1)
    out, off = [], 0
    for s in shapes:
        n = math.prod(s)
        out.append(flat[off:off + n].reshape(s))
        off += n
    return out


def _ident(accs, ex):
    return accs


def _add_resid(accs, ex):
    return [accs[0] + ex[0]]


def _swiglu_epi(accs, ex):
    a, b = accs
    return [a, b, a * _sig(a) * b]


def _swiglu_bwd_epi(accs, ex):
    df = accs[0]
    a, b = ex[0].astype(F32), ex[1].astype(F32)
    sa = _sig(a)
    return [df * b * (sa * (1.0 + a * (1.0 - sa))), df * (a * sa)]


def _ple_epi(accs, ex):
    gt = _sig(accs[0])
    return [ex[0] + gt * ex[1], gt]


def _glu_epi(accs, ex):
    val, sg = accs[0], _sig(accs[1])
    return [ex[0] + val * sg, val, sg]


def kernel(x, p, norm_mix, norm_ffn, norm_ple, norm_final, gm_w_in, gm_ln_g, gm_ln_b, gm_w_s, gm_b_s, gm_w_out, s5_w_in, s5_a_re, s5_a_im, s5_log_dt, s5_b_re, s5_b_im, s5_c_re, s5_c_im, s5_d, s5_w_out, ffn_w1, ffn_w3, ffn_w2, ple_w_gate, ple_w_proj, loss_target, m_norm_mix, m_norm_ffn, m_norm_ple, m_norm_final, m_gm_w_in, m_gm_ln_g, m_gm_ln_b, m_gm_w_s, m_gm_b_s, m_gm_w_out, m_s5_w_in, m_s5_a_re, m_s5_a_im, m_s5_log_dt, m_s5_b_re, m_s5_b_im, m_s5_c_re, m_s5_c_im, m_s5_d, m_s5_w_out, m_ffn_w1, m_ffn_w3, m_ffn_w2, m_ple_w_gate, m_ple_w_proj, v_norm_mix, v_norm_ffn, v_norm_ple, v_norm_final, v_gm_w_in, v_gm_ln_g, v_gm_ln_b, v_gm_w_s, v_gm_b_s, v_gm_w_out, v_s5_w_in, v_s5_a_re, v_s5_a_im, v_s5_log_dt, v_s5_b_re, v_s5_b_im, v_s5_c_re, v_s5_c_im, v_s5_d, v_s5_w_out, v_ffn_w1, v_ffn_w3, v_ffn_w2, v_ple_w_gate, v_ple_w_proj):
    env = dict(locals())
    w = {n: env[n] for n in _WEIGHT_NAMES}
    mom = {n: env["m_" + n] for n in _WEIGHT_NAMES}
    var = {n: env["v_" + n] for n in _WEIGHT_NAMES}
    xs, tgt = x[0], loss_target[0]
    L, D = xs.shape
    depth = norm_mix.shape[0]
    qx, qy = lax.axis_index("x"), lax.axis_index("y")
    q = 2 * qx + qy

    def gather(tag, items):
        shards = []
        for name, layer in items:
            kind = _BIG[name]
            R, C = _rc(kind, w[name].shape)
            shards.append(_cast_shard(f"cast_{name}{layer}", w[name], layer, kind, R, C))
        full = _allgather(f"ag_{tag}", shards)
        return {it: _W(f, _BIG[it[0]]) for it, f in zip(items, full)}

    W = {}
    mixers = [["gm_w_in", "gm_w_out"], ["s5_w_in", "s5_w_out"]]
    for i in range(depth):
        for n in mixers[i]:
            W.update(gather(f"{n}", [(n, 0)]))
        W.update(gather(f"ffn_up{i}", [("ffn_w1", i), ("ffn_w3", i)]))
        W.update(gather(f"ffn_down{i}", [("ffn_w2", i)]))
        W.update(gather(f"ple{i}", [("ple_w_gate", i), ("ple_w_proj", i)]))

    d_slots = jnp.zeros((4, D // 4), F32)
    d_slots = lax.dynamic_update_slice(d_slots, s5_d.astype(F32), (q, 0))
    d_sum = _allreduce_small("ar_s5_d", _pack([d_slots], 64).reshape(8, -1, _LANES))
    d_full = (d_sum.reshape(-1)[:D] * 0.5).reshape(1, D)

    def ffn_fwd(i, xin):
        hf = _rms_fwd(f"rms_ffn{i}", xin, norm_ffn[i:i + 1])
        a, b, f = _mm_nn(f"ffn_up{i}", hf, [W["ffn_w1", i], W["ffn_w3", i]], 1024, 1408, ffn_w2.shape[1] * 4,
                         [], [_MXU, _MXU, _MXU], _swiglu_epi, tm=1024)
        xo = _mm_nn(f"ffn_down{i}", f, [W["ffn_w2", i]], 1408, 1024, D, [xin], [F32], _add_resid, tm=1024)[0]
        return xo, (xin, hf, a, b, f)

    def ple_fwd(i, xin):
        hp = _rms_fwd(f"rms_ple{i}", xin, norm_ple[i:i + 1])
        pi = lax.optimization_barrier((p[i, 0], hp))[0]
        pp = _mm_nn(f"ple_proj{i}", pi, [W["ple_w_proj", i]], 128, 512, D, [], [F32], _ident, tm=2048)[0]
        xo, gt = _mm_nn(f"ple_gate{i}", hp, [W["ple_w_gate", i]], 512, 1024, D, [xin, pp], [F32, _MXU], _ple_epi,
                        tm=1024)
        return xo, (xin, hp, pi, pp, gt)

    h0 = _rms_fwd("rms_mix0", xs, norm_mix[0:1])
    z = _mm_nn("gm_in", h0, [W["gm_w_in", 0]], 1024, 1024, 2 * D, [], [F32], _ident, tm=1024)[0]
    bsT = gm_b_s[0].T
    gm_m = _gmlp_fwd(z, gm_ln_g, gm_ln_b, gm_w_s[0], bsT)
    x1 = _mm_nn("gm_out", gm_m, [W["gm_w_out", 0]], 512, 1024, D, [xs], [F32], _add_resid, tm=1024)[0]
    x2, ffn0 = ffn_fwd(0, x1)
    x3, ple0 = ple_fwd(0, x2)

    T = D // _LANES
    lanes = _S5_GT * _S5_P
    sv = {n: _to_view(n, w[n])[0] for n in _VIEW}
    a_re, a_im, log_dt = sv["s5_a_re"], sv["s5_a_im"], s5_log_dt
    lbr, lbi, Bbar_re, Bbar_im = _s5_prep(a_re, a_im, log_dt, sv["s5_b_re"], sv["s5_b_im"])

    def to_bd(B):
        return jnp.transpose(B.reshape(_S5_P, _S5_C, T, _S5_GT), (2, 3, 1, 0))

    def to_cd(cw):
        return jnp.transpose(cw.reshape(_S5_C, _S5_P, T, _S5_GT), (2, 3, 1, 0))

    def to_lam(v):
        return jnp.transpose(v).reshape(T, 1, lanes)

    bd_re, bd_im = to_bd(Bbar_re), to_bd(Bbar_im)
    cd_re, cd_im = to_cd(sv["s5_c_re"]), to_cd(sv["s5_c_im"])
    lam_re, lam_im = to_lam(lbr), to_lam(lbi)

    h1 = _rms_fwd("rms_mix1", x3, norm_mix[1:2])
    u = _mm_nn("s5_in", h1, [W["s5_w_in", 0]], 512, 1024, D, [], [F32], _ident, tm=1024)[0]
    s5_g, s5_re, s5_im = _s5_fwd(u, bd_re, bd_im, cd_re, cd_im, lam_re, lam_im, d_full)
    x4, glu_val, glu_sg = _mm_nn("s5_out", s5_g, [W["s5_w_out", 0], W["s5_w_out", 0]], 1024, 1024, D, [x3],
                                 [F32, _MXU, _MXU], _glu_epi, tm=1024, cb_offsets=[0, 2])
    x5, ffn1 = ffn_fwd(1, x4)
    x6, ple1 = ple_fwd(1, x5)

    dx, d_norm_final, loss_rows = _loss_head(x6, norm_final[None], tgt)

    small = {}

    seq = _Order()
    tie, done = seq.tie, seq.done
    d_norm_ple, d_norm_ffn, d_norm_mix = [None] * depth, [None] * depth, [None] * depth
    reduced = {}

    def keep(names, layer, pairs):
        for n, pr in zip(names, pairs):
            reduced[n, layer] = pr

    ple_names, up_names, down_names = ["ple_w_gate", "ple_w_proj"], ["ffn_w1", "ffn_w3"], ["ffn_w2"]

    def ple_bwd(i, dxo, saved):
        xin, hp, pi, pp, gt = saved
        dpre, dpp = done(_ple_bwd_elem(tie(dxo), pp, gt))
        dwg = done(_mm_tn(f"ple_gate_dw{i}", tie(hp), [dpre], "row", 512, 1024, 512, 1024))[0]
        dwp = done(_mm_tn(f"ple_proj_dw{i}", tie(pi), [dpp], "col", 128, 512, 128, 512))[0]
        red = _Reduction(f"ple{i}", [dwg, dwp])
        dhp = done(_mm_nt(f"ple_gate_dx{i}", [tie(dpre)], [W["ple_w_gate", i]], 512, 1024, [], [F32], _ident,
                          tm=2048))[0]
        dxin, dxin_mxu, dg = done(_rms_bwd(f"rms_ple_bwd{i}", tie(dhp), xin, norm_ple[i:i + 1], dxo))
        return dxin, dxin_mxu, dg, red

    def ffn_bwd(i, dxo, dxo_mxu, saved, before_up):
        xin, hf, a, b, f = saved
        dw2 = done(_mm_tn(f"ffn_down_dw{i}", tie(f), [dxo_mxu], "row", 1408, 1024, 1408, 1024))[0]
        r_down = _Reduction(f"ffd{i}", [dw2])
        da, db = done(_mm_nt(f"ffn_down_dx{i}", [tie(dxo_mxu)], [W["ffn_w2", i]], 1408, 1024, [a, b], [_MXU, _MXU],
                             _swiglu_bwd_epi, tm=1024))
        for step in before_up:
            step()
        r_down.scatter(seq)
        dw1, dw3 = done(_mm_tn(f"ffn_up_dw{i}", tie(hf), [da, db], "col", 1024, 1408, 1024, 1408))
        r_up = _Reduction(f"ffu{i}", [dw1, dw3])
        dhf = done(_mm_nt(f"ffn_up_dx{i}", [tie(da), db], [W["ffn_w1", i], W["ffn_w3", i]], 1024, 1408, [], [F32],
                          _ident, tm=1024))[0]
        dxin, dxin_mxu, dg = done(_rms_bwd(f"rms_ffn_bwd{i}", tie(dhf), xin, norm_ffn[i:i + 1], dxo))
        r_up.scatter(seq)
        return dxin, dxin_mxu, dg, r_down, r_up

    dx, dx_mxu, d_norm_ple[1], r_ple1 = ple_bwd(1, dx, ple1)
    dx, _, d_norm_ffn[1], r_down1, r_up1 = ffn_bwd(1, dx, dx_mxu, ffn1, [lambda: r_ple1.scatter(seq)])

    do = done([_glu_bwd_elem(tie(dx), glu_val, glu_sg)])[0]
    dw_s5out = done(_mm_tn("s5_out_dw", tie(s5_g), [do], "col", 1024, 1024, 1024, 1024))[0]
    r_s5out = _Reduction("s5out", [dw_s5out])
    dgy = done(_mm_nt("s5_out_dx", [tie(do)], [W["s5_w_out", 0]], 1024, 1024, [], [F32], _ident, tm=1024))[0]
    keep(ple_names, 1, r_ple1.finish(seq))
    keep(down_names, 1, r_down1.finish(seq))
    du, dbd_re, dbd_im, dcd_re, dcd_im, dl_re, dl_im, dd = done(_s5_bwd(
        tie(u), dgy, s5_re, s5_im, bd_re, bd_im, cd_re, cd_im, lam_re, lam_im, d_full))
    r_s5out.scatter(seq)
    dw_s5in = done(_mm_tn("s5_in_dw", tie(h1), [du], "row", 512, 1024, 512, 1024))[0]
    r_s5in = _Reduction("s5in", [dw_s5in])
    dh1 = done(_mm_nt("s5_in_dx", [tie(du)], [W["s5_w_in", 0]], 512, 1024, [], [F32], _ident, tm=2048))[0]
    dx, _, d_norm_mix[1] = done(_rms_bwd("rms_mix1_bwd", tie(dh1), x3, norm_mix[1:2], dx))
    keep(up_names, 1, r_up1.finish(seq))
    r_s5in.scatter(seq)

    def from_bd(t):
        return jnp.transpose(t, (3, 2, 0, 1)).reshape(_S5_P, _S5_C, T * _S5_GT)

    def from_cdT(t):
        return jnp.transpose(t, (2, 3, 0, 1)).reshape(_S5_C, _S5_P, T * _S5_GT)

    def from_lam(t):
        return jnp.transpose(t.reshape(T * _S5_GT, _S5_P))

    da_re, da_im, dlog_dt, db_re, db_im = _s5_prep_bwd(
        a_re, a_im, log_dt, sv["s5_b_re"], sv["s5_b_im"], from_lam(dl_re), from_lam(dl_im),
        from_bd(dbd_re), from_bd(dbd_im))
    small["s5_a_re"], small["s5_a_im"], small["s5_log_dt"] = da_re[None], da_im[None], dlog_dt
    small["s5_b_re"], small["s5_b_im"] = db_re[None], db_im[None]
    small["s5_c_re"], small["s5_c_im"] = from_cdT(dcd_re)[None], from_cdT(dcd_im)[None]

    dx, dx_mxu, d_norm_ple[0], r_ple0 = ple_bwd(0, dx, ple0)
    keep(["s5_w_out"], 0, r_s5out.finish(seq))
    xin0, hf0, a0, b0, f0 = ffn0
    da0, db0 = done(_mm_nt("ffn_down_dx0", [tie(dx_mxu)], [W["ffn_w2", 0]], 1408, 1024, [a0, b0], [_MXU, _MXU],
                           _swiglu_bwd_epi, tm=1024))
    r_ple0.scatter(seq)
    dw1, dw3 = done(_mm_tn("ffn_up_dw0", tie(hf0), [da0, db0], "col", 1024, 1408, 1024, 1408))
    r_up0 = _Reduction("ffu0", [dw1, dw3])
    keep(["s5_w_in"], 0, r_s5in.finish(seq))
    dw2 = done(_mm_tn("ffn_down_dw0", tie(f0), [dx_mxu], "row", 1408, 1024, 1408, 1024))[0]
    r_down0 = _Reduction("ffd0", [dw2])
    r_up0.scatter(seq)
    dhf0 = done(_mm_nt("ffn_up_dx0", [tie(da0), db0], [W["ffn_w1", 0], W["ffn_w3", 0]], 1024, 1408, [], [F32], _ident,
                       tm=1024))[0]
    dx, dx_mxu, d_norm_ffn[0] = done(_rms_bwd("rms_ffn_bwd0", tie(dhf0), xin0, norm_ffn[0:1], dx))
    keep(ple_names, 0, r_ple0.finish(seq))
    r_down0.scatter(seq)

    dw_gmout = done(_mm_tn("gm_out_dw", tie(gm_m), [dx_mxu], "row", 512, 1024, 512, 1024))[0]
    r_gmout = _Reduction("gmout", [dw_gmout])
    dgm = done(_mm_nt("gm_out_dx", [tie(dx_mxu)], [W["gm_w_out", 0]], 512, 1024, [], [F32], _ident, tm=2048))[0]
    dz, dws, dbsT, dlng, dlnb = done(_gmlp_bwd(tie(z), dgm, gm_ln_g, gm_ln_b, gm_w_s[0], bsT))
    dw_gmin = done(_mm_tn("gm_in_dw", tie(h0), [dz], "col", 1024, 1024, 1024, 1024))[0]
    r_gmin = _Reduction("gmin", [dw_gmin])
    dh0 = done(_mm_nt("gm_in_dx", [tie(dz)], [W["gm_w_in", 0]], 1024, 1024, [], [F32], _ident, tm=1024))[0]
    dx, _, d_norm_mix[0] = done(_rms_bwd("rms_mix0_bwd", tie(dh0), xs, norm_mix[0:1], dx))
    grad_x = dx[None]

    small["norm_mix"], small["norm_ffn"] = jnp.concatenate(d_norm_mix), jnp.concatenate(d_norm_ffn)
    small["norm_ple"], small["norm_final"] = jnp.concatenate(d_norm_ple), d_norm_final[0]
    small["gm_ln_g"], small["gm_ln_b"], small["gm_w_s"] = dlng, dlnb, dws[None]
    small["gm_b_s"] = dbsT[:, :_GM_HEADS].T[None]
    small["s5_d"] = dd

    small_names = [n for n in _WEIGHT_NAMES if n not in _BIG]
    packed = _pack([small[n] for n in small_names] + [loss_rows[:, :1]], 64).reshape(8, -1, _LANES)
    grads, deltas, new_m, new_v = {}, {}, {}, {}
    my_c = lax.axis_index("c")

    def adamw(n, layer, prev):
        kind = _BIG[n]
        R, C = _rc(kind, w[n].shape)
        return done(_adamw_big(f"adamw_{n}{layer}", w[n], mom[n], var[n], layer, reduced[n, layer], kind, R, C,
                               my_c, seq.tok, prev))

    def adamw_last(names):
        for n in names:
            grads[n], deltas[n], new_m[n], new_v[n] = adamw(n, 0, late.get(n))

    late = {}
    for n in down_names + up_names + ple_names:
        late[n] = adamw(n, 1, None)
    keep(up_names, 0, r_up0.finish(seq))
    r_gmout.scatter(seq)
    r_gmin.scatter(seq)
    landed = _exchange_slices("ar_small_in", packed)
    adamw_last(["s5_w_in", "s5_w_out"] + ple_names)
    keep(down_names, 0, r_down0.finish(seq))
    adamw_last(up_names)
    keep(["gm_w_out"], 0, r_gmout.finish(seq))
    mine = done([_sum_slices("ar_small_sum", packed, landed, seq.tok)])[0]
    spread = _broadcast_slices("ar_small_out", mine)
    keep(["gm_w_in"], 0, r_gmin.finish(seq))
    adamw_last(down_names + ["gm_w_out", "gm_w_in"])
    summed = lax.dynamic_update_slice(spread, mine[None], (4 * qx + 2 * qy + my_c, 0, 0))
    *red_list, loss_sum = _unpack(summed, [small[n].shape for n in small_names] + [(1, 1)])
    red_small, loss = dict(zip(small_names, red_list)), loss_sum.reshape(())
    red_small["s5_d"] = lax.dynamic_slice(red_small["s5_d"], (0, q * (D // 4)), (1, D // 4))

    def views(src):
        return [_to_view(n, src[n]) for n in small_names]

    shapes = [v_.shape for v_ in views(w)]
    dl, mo, vo = _adamw_flat(_pack(views(w), 64), _pack([red_small[n] for n in small_names], 64),
                             _pack(views(mom), 64), _pack(views(var), 64))
    for n, g_, d_, m_, v_ in zip(small_names, [red_small[n] for n in small_names], _unpack(dl, shapes),
                                 _unpack(mo, shapes), _unpack(vo, shapes)):
        grads[n], deltas[n], new_m[n], new_v[n] = [_from_view(n, t_).reshape(w[n].shape) for t_ in (g_, d_, m_, v_)]

    return (loss, grad_x, *[grads[n] for n in _WEIGHT_NAMES], *[deltas[n] for n in _WEIGHT_NAMES],
            *[new_m[n] for n in _WEIGHT_NAMES], *[new_v[n] for n in _WEIGHT_NAMES])
```

```python
import math

import jax
import jax.numpy as jnp
from jax import lax
from jax.experimental import pallas as pl
from jax.experimental.pallas import tpu as pltpu
from jax.experimental.pallas import tpu_sc as plsc

F32 = jnp.float32
_MXU = jnp.bfloat16
_WIRE = jnp.bfloat16
_EPS = 1e-6
_VMEM_LIMIT = 56 * 1024 * 1024
_LANES = 128
_MESH = pl.DeviceIdType.MESH

_LR, _B1, _B2, _AEPS, _WD, _STEP = 0.001, 0.9, 0.999, 1e-08, 0.01, 10

_GM_CHUNK = 128
_GM_HEADS = 16
_S5_GT = 8
_S5_P = 64
_S5_C = 16

_NN = (((1,), (0,)), ((), ()))
_NT = (((1,), (1,)), ((), ()))
_TN = (((0,), (0,)), ((), ()))


def _cparams(sem):
    return pltpu.CompilerParams(dimension_semantics=sem, vmem_limit_bytes=_VMEM_LIMIT)


def _sig(x):
    return 0.5 * jnp.tanh(0.5 * x) + 0.5


_GC = math.sqrt(2.0 / math.pi)


def _gelu(x):
    return 0.5 * x * (1.0 + jnp.tanh(_GC * (x + 0.044715 * (x * x * x))))


def _gelu_grad(x):
    t = jnp.tanh(_GC * (x + 0.044715 * (x * x * x)))
    return 0.5 * (1.0 + t) + 0.5 * x * (1.0 - t * t) * (_GC * (1.0 + 3.0 * 0.044715 * x * x))


def _dot(a, b, dn):
    return lax.dot_general(a.astype(_MXU), b.astype(_MXU), dn, preferred_element_type=F32)


class _W:
    def __init__(self, arr, kind):
        self.a, self.kind = arr, kind
        self.R, self.C = arr.shape[2], arr.shape[3]

    def full_shape(self):
        return (2 * self.R, 4 * self.C) if self.kind == "col" else (4 * self.R, 2 * self.C)


def _part_index(kind, R, C, tr, tc, rb, cb):
    nr, nc = R // tr, C // tc
    if kind == "col":
        return cb // nc, rb // nr, rb % nr, cb % nc
    return rb // nr, cb // nc, rb % nr, cb % nc


def _wspec(w, tr, tc, rb_fn, cb_fn):
    assert w.R % tr == 0 and w.C % tc == 0, (w.R, w.C, tr, tc)

    def imap(i, j, k):
        return _part_index(w.kind, w.R, w.C, tr, tc, rb_fn(i, j, k), cb_fn(i, j, k))

    return pl.BlockSpec((None, None, tr, tc), imap)


def _gspec(kind, R, C, tr, tc):
    assert R % tr == 0 and C % tc == 0, (R, C, tr, tc)

    def imap(i, j, k):
        part, half, rbi, cbi = _part_index(kind, R, C, tr, tc, i, j)
        return half, part, rbi, cbi

    return pl.BlockSpec((None, None, tr, tc), imap)


def _mm(name, grid, a_ops, b_ops, pairs, acc_shape, n_acc, extras, outs, epilogue):
    nk = grid[2]
    na, nb, ne, no = len(a_ops), len(b_ops), len(extras), len(outs)

    def body(*refs):
        a_refs = refs[:na]
        b_refs = refs[na:na + nb]
        e_refs = refs[na + nb:na + nb + ne]
        o_refs = refs[na + nb + ne:na + nb + ne + no]
        acc_refs = refs[na + nb + ne + no:]
        k = pl.program_id(2)

        def products():
            sums = [None] * n_acc
            for ai, bi, ci, dn in pairs:
                d = _dot(a_refs[ai][...], b_refs[bi][...], dn)
                sums[ci] = d if sums[ci] is None else sums[ci] + d
            return sums

        def finish(accs):
            res = epilogue(accs, [e[...] for e in e_refs])
            for o, r in zip(o_refs, res):
                o[...] = r.astype(o.dtype)

        if nk == 1:
            finish(products())
            return

        @pl.when(k == 0)
        def _():
            for acc, d in zip(acc_refs, products()):
                acc[...] = d

        @pl.when(jnp.logical_and(k > 0, k < nk - 1))
        def _():
            for acc, d in zip(acc_refs, products()):
                acc[...] += d

        @pl.when(k == nk - 1)
        def _():
            finish([acc[...] + d for acc, d in zip(acc_refs, products())])

    ops = list(a_ops) + list(b_ops) + list(extras)
    return pl.pallas_call(
        body, name=name, grid=grid,
        in_specs=[s for _, s in ops],
        out_specs=[s for _, s in outs],
        out_shape=[s for s, _ in outs],
        scratch_shapes=[pltpu.VMEM(acc_shape, F32) for _ in range(n_acc if nk > 1 else 0)],
        compiler_params=_cparams(("parallel", "parallel", "arbitrary")),
    )(*[a for a, _ in ops])


def _bs(shape, fn):
    return pl.BlockSpec(shape, fn)


def _tile_m(L):
    return min(L, 512)


def _mm_nn(name, x, ws, tk, tn, n_out, extras, outs_sd, epilogue, tm=None, cb_offsets=None):
    M, K = x.shape
    tm = min(M, tm or _tile_m(M))
    grid = (M // tm, n_out // tn, K // tk)
    a_ops = [(x, _bs((tm, tk), lambda i, j, k: (i, k)))]
    cb_offsets = cb_offsets or [0] * len(ws)
    b_ops = [(w.a, _wspec(w, tk, tn, lambda i, j, k: k, (lambda off: lambda i, j, k: j + off)(off)))
             for w, off in zip(ws, cb_offsets)]
    pairs = [(0, bi, bi, _NN) for bi in range(len(ws))]
    mn = _bs((tm, tn), lambda i, j, k: (i, j))
    ex = [(e, mn) for e in extras]
    outs = [(jax.ShapeDtypeStruct((M, n_out), dt), mn) for dt in outs_sd]
    return _mm(name, grid, a_ops, b_ops, pairs, (tm, tn), len(ws), ex, outs, epilogue)


def _mm_nt(name, xs, ws, tn, tk, extras, outs_sd, epilogue, tm=None):
    M, Nw = xs[0].shape
    Kw = ws[0].full_shape()[0]
    tm = min(M, tm or _tile_m(M))
    grid = (M // tm, Kw // tn, Nw // tk)
    a_ops = [(x, _bs((tm, tk), lambda i, j, k: (i, k))) for x in xs]
    b_ops = [(w.a, _wspec(w, tn, tk, lambda i, j, k: j, lambda i, j, k: k)) for w in ws]
    pairs = [(i, i, 0, _NT) for i in range(len(ws))]
    mn = _bs((tm, tn), lambda i, j, k: (i, j))
    ex = [(e, mn) for e in extras]
    outs = [(jax.ShapeDtypeStruct((M, Kw), dt), mn) for dt in outs_sd]
    return _mm(name, grid, a_ops, b_ops, pairs, (tm, tn), 1, ex, outs, epilogue)


def _mm_tn(name, x, dys, kind, R, C, tm, tn, tk=None):
    L, Kw = x.shape
    Nw = dys[0].shape[1]
    tk = tk or min(L, 1024)
    grid = (Kw // tm, Nw // tn, L // tk)
    a_ops = [(x, _bs((tk, tm), lambda i, j, k: (k, i)))]
    b_ops = [(dy, _bs((tk, tn), lambda i, j, k: (k, j))) for dy in dys]
    pairs = [(0, bi, bi, _TN) for bi in range(len(dys))]
    gs = _gspec(kind, R, C, tm, tn)
    outs = [(jax.ShapeDtypeStruct((2, 4, R, C), _WIRE), gs) for _ in dys]
    return _mm(name, grid, a_ops, b_ops, pairs, (tm, tn), len(dys), [], outs, lambda accs, ex: accs)


def _row_tile(L):
    return min(L, 256)


def _rowwise(name, body, ins, outs, L, acc_outs=()):
    tr = _row_tile(L)
    n_in, n_out = len(ins), len(outs)

    def kbody(*refs):
        i_refs, o_refs, a_refs = refs[:n_in], refs[n_in:n_in + n_out], refs[n_in + n_out:]
        res, sums = body(*[r[...] for r in i_refs])
        for o, r in zip(o_refs, res):
            o[...] = r.astype(o.dtype)
        if a_refs:
            @pl.when(pl.program_id(0) == 0)
            def _():
                for a in a_refs:
                    a[...] = jnp.zeros(a.shape, F32)
            for a, s in zip(a_refs, sums):
                a[...] += s

    in_specs = []
    for arr, kind in ins:
        if kind == "row":
            in_specs.append(pl.BlockSpec((tr, arr.shape[1]), lambda i: (i, 0)))
        else:
            in_specs.append(pl.BlockSpec(arr.shape, lambda i: (0, 0)))
    out_specs = [pl.BlockSpec((tr, c), lambda i: (i, 0)) for c, _ in outs]
    out_shape = [jax.ShapeDtypeStruct((L, c), dt) for c, dt in outs]
    out_specs += [pl.BlockSpec((1, c), lambda i: (0, 0)) for c in acc_outs]
    out_shape += [jax.ShapeDtypeStruct((1, c), F32) for c in acc_outs]
    return pl.pallas_call(
        kbody, name=name, grid=(L // tr,), in_specs=in_specs, out_specs=out_specs, out_shape=out_shape,
        compiler_params=_cparams(("arbitrary",)),
    )(*[a for a, _ in ins])


def _rms_fwd(name, x, g):
    def body(xv, gv):
        r = lax.rsqrt(jnp.mean(xv * xv, axis=-1, keepdims=True) + _EPS)
        return [xv * r * gv], []
    return _rowwise(name, body, [(x, "row"), (g, "vec")], [(x.shape[1], _MXU)], x.shape[0])[0]


def _rms_bwd(name, dh, x, g, dres):
    def body(dhv, xv, gv, dr):
        r = lax.rsqrt(jnp.mean(xv * xv, axis=-1, keepdims=True) + _EPS)
        xh = xv * r
        dxh = dhv * gv
        dx = dr + r * (dxh - xh * jnp.mean(dxh * xh, axis=-1, keepdims=True))
        return [dx, dx], [jnp.sum(dhv * xh, axis=0, keepdims=True)]
    D = x.shape[1]
    return _rowwise(name, body, [(dh, "row"), (x, "row"), (g, "vec"), (dres, "row")], [(D, F32), (D, _MXU)],
                    x.shape[0], [D])


def _loss_head(x, g, target):
    D = x.shape[1]

    def body(xv, gv, tv):
        r = lax.rsqrt(jnp.mean(xv * xv, axis=-1, keepdims=True) + _EPS)
        xh = xv * r
        e = xh * gv - tv
        dy = e * (1.0 / D)
        dxh = dy * gv
        dx = r * (dxh - xh * jnp.mean(dxh * xh, axis=-1, keepdims=True))
        row_loss = 0.5 * jnp.mean(e * e, axis=-1, keepdims=True)
        lsum = jnp.sum(row_loss, axis=0, keepdims=True) + jnp.zeros((1, _LANES), F32)
        return [dx], [jnp.sum(dy * xh, axis=0, keepdims=True), lsum]
    return _rowwise("loss_head", body, [(x, "row"), (g, "vec"), (target, "row")], [(D, F32)], x.shape[0], [D, _LANES])


def _ple_bwd_elem(dx, pp, gt):
    def body(dxv, ppv, gtv):
        gt32 = gtv.astype(F32)
        return [dxv * ppv * gt32 * (1.0 - gt32), dxv * gt32], []
    D = dx.shape[1]
    return _rowwise("ple_bwd_elem", body, [(dx, "row"), (pp, "row"), (gt, "row")], [(D, _MXU), (D, _MXU)], dx.shape[0])


def _glu_bwd_elem(dx, val, sg):
    def body(dxv, vv, sv):
        v32, s32 = vv.astype(F32), sv.astype(F32)
        return [jnp.concatenate([dxv * s32, dxv * v32 * s32 * (1.0 - s32)], axis=1)], []
    D = dx.shape[1]
    return _rowwise("glu_bwd_elem", body, [(dx, "row"), (val, "row"), (sg, "row")], [(2 * D, _MXU)], dx.shape[0])[0]


def _gm_common(z, ln_g, ln_b, wc_bf, bsT):
    W = z.shape[1] // 2
    zu, zv = z[:, :W], z[:, W:]
    u, v = _gelu(zu), _gelu(zv)
    mu = jnp.mean(v, axis=-1, keepdims=True)
    vc = v - mu
    rstd = lax.rsqrt(jnp.mean(vc * vc, axis=-1, keepdims=True) + _EPS)
    vh = vc * rstd
    vn = vh * ln_g + ln_b
    vnb = vn.astype(_MXU)
    svs = []
    for h in range(_GM_HEADS):
        sl = slice(h * _LANES, (h + 1) * _LANES)
        svs.append(_dot(wc_bf[h], vnb[:, sl], _NN) + bsT[:, h:h + 1])
    return zu, zv, u, vh, rstd, vnb, svs


def _causal(w):
    t = lax.broadcasted_iota(jnp.int32, w.shape, w.ndim - 2)
    s = lax.broadcasted_iota(jnp.int32, w.shape, w.ndim - 1)
    return jnp.where(s <= t, w, jnp.zeros_like(w))


def _gmlp_fwd(z, ln_g, ln_b, w_s, bsT):
    L, W2 = z.shape
    W = W2 // 2

    def body(z_ref, g_ref, b_ref, ws_ref, bs_ref, m_ref):
        wc = _causal(ws_ref[...]).astype(_MXU)
        _, _, u, _, _, _, svs = _gm_common(z_ref[...], g_ref[...], b_ref[...], wc, bs_ref[...])
        for h in range(_GM_HEADS):
            sl = slice(h * _LANES, (h + 1) * _LANES)
            m_ref[:, sl] = (u[:, sl] * svs[h]).astype(m_ref.dtype)

    return pl.pallas_call(
        body, name="gmlp_fwd", grid=(L // _GM_CHUNK,),
        in_specs=[pl.BlockSpec((_GM_CHUNK, W2), lambda n: (n, 0)),
                  pl.BlockSpec((1, W), lambda n: (0, 0)), pl.BlockSpec((1, W), lambda n: (0, 0)),
                  pl.BlockSpec(w_s.shape, lambda n: (0, 0, 0)), pl.BlockSpec(bsT.shape, lambda n: (0, 0))],
        out_specs=pl.BlockSpec((_GM_CHUNK, W), lambda n: (n, 0)),
        out_shape=jax.ShapeDtypeStruct((L, W), _MXU),
        compiler_params=_cparams(("arbitrary",)),
    )(z, ln_g, ln_b, w_s, bsT)


def _gmlp_bwd(z, dm, ln_g, ln_b, w_s, bsT):
    L, W2 = z.shape
    W = W2 // 2
    T = _GM_CHUNK

    def body(z_ref, dm_ref, g_ref, b_ref, ws_ref, bs_ref, dz_ref, dws_ref, dbs_ref, dg_ref, db_ref):
        @pl.when(pl.program_id(0) == 0)
        def _():
            dws_ref[...] = jnp.zeros(dws_ref.shape, F32)
            dbs_ref[...] = jnp.zeros(dbs_ref.shape, F32)
            dg_ref[...] = jnp.zeros(dg_ref.shape, F32)
            db_ref[...] = jnp.zeros(db_ref.shape, F32)

        wc = _causal(ws_ref[...]).astype(_MXU)
        ln_g_v = g_ref[...]
        zu, zv, u, vh, rstd, vnb, svs = _gm_common(z_ref[...], ln_g_v, b_ref[...], wc, bs_ref[...])
        dmv = dm_ref[...]
        lane = lax.broadcasted_iota(jnp.int32, (T, _LANES), 1)
        dbs = jnp.zeros((T, _LANES), F32)
        dvn_parts = []
        for h in range(_GM_HEADS):
            sl = slice(h * _LANES, (h + 1) * _LANES)
            dsv = dmv[:, sl] * u[:, sl]
            dz_ref[:, sl] = (dmv[:, sl] * svs[h] * _gelu_grad(zu[:, sl])).astype(dz_ref.dtype)
            dbs = dbs + jnp.where(lane == h, jnp.sum(dsv, axis=1, keepdims=True), 0.0)
            dsvb = dsv.astype(_MXU)
            dws_ref[h] += _dot(dsvb, vnb[:, sl], _NT)
            dvn_parts.append(_dot(wc[h], dsvb, _TN))
        dbs_ref[...] += dbs
        dvn = jnp.concatenate(dvn_parts, axis=1)
        dg_ref[...] += jnp.sum(dvn * vh, axis=0, keepdims=True)
        db_ref[...] += jnp.sum(dvn, axis=0, keepdims=True)
        dxh = dvn * ln_g_v
        dv = rstd * (dxh - jnp.mean(dxh, axis=-1, keepdims=True) - vh * jnp.mean(dxh * vh, axis=-1, keepdims=True))
        dz_ref[:, W:] = (dv * _gelu_grad(zv)).astype(dz_ref.dtype)

        @pl.when(pl.program_id(0) == pl.num_programs(0) - 1)
        def _():
            dws_ref[...] = _causal(dws_ref[...])

    return pl.pallas_call(
        body, name="gmlp_bwd", grid=(L // T,),
        in_specs=[pl.BlockSpec((T, W2), lambda n: (n, 0)), pl.BlockSpec((T, W), lambda n: (n, 0)),
                  pl.BlockSpec((1, W), lambda n: (0, 0)), pl.BlockSpec((1, W), lambda n: (0, 0)),
                  pl.BlockSpec(w_s.shape, lambda n: (0, 0, 0)), pl.BlockSpec(bsT.shape, lambda n: (0, 0))],
        out_specs=[pl.BlockSpec((T, W2), lambda n: (n, 0)),
                   pl.BlockSpec(w_s.shape, lambda n: (0, 0, 0)), pl.BlockSpec((T, _LANES), lambda n: (0, 0)),
                   pl.BlockSpec((1, W), lambda n: (0, 0)), pl.BlockSpec((1, W), lambda n: (0, 0))],
        out_shape=[jax.ShapeDtypeStruct((L, W2), _MXU), jax.ShapeDtypeStruct(w_s.shape, F32),
                   jax.ShapeDtypeStruct((T, _LANES), F32),
                   jax.ShapeDtypeStruct((1, W), F32), jax.ShapeDtypeStruct((1, W), F32)],
        compiler_params=_cparams(("arbitrary",)),
    )(z, dm, ln_g, ln_b, w_s, bsT)


def _s5_prep_math(a_re, a_im, log_dt):
    dt = jnp.exp(log_dt)
    xr, xi = a_re * dt, a_im * dt
    e = jnp.exp(xr)
    lbr, lbi = e * jnp.cos(xi), e * jnp.sin(xi)
    dn = a_re * a_re + a_im * a_im
    nr, ni = lbr - 1.0, lbi
    pr, pi = nr * a_re + ni * a_im, ni * a_re - nr * a_im
    return dt, lbr, lbi, dn, nr, ni, pr, pi


def _vm():
    return pl.BlockSpec(memory_space=pltpu.VMEM)


def _s5_prep(a_re, a_im, log_dt, b_re, b_im):
    def body(ar_ref, ai_ref, ld_ref, br_ref, bi_ref, lbr_ref, lbi_ref, Br_ref, Bi_ref):
        _, lbr, lbi, dn, _, _, pr, pi = _s5_prep_math(ar_ref[...], ai_ref[...], ld_ref[...])
        cr, ci = (pr / dn)[:, None, :], (pi / dn)[:, None, :]
        lbr_ref[...] = lbr
        lbi_ref[...] = lbi
        br, bi = br_ref[...], bi_ref[...]
        Br_ref[...] = cr * br - ci * bi
        Bi_ref[...] = cr * bi + ci * br

    sd = jax.ShapeDtypeStruct
    return pl.pallas_call(
        body, name="s5_prep", in_specs=[_vm()] * 5, out_specs=[_vm()] * 4,
        out_shape=[sd(a_re.shape, F32), sd(a_re.shape, F32), sd(b_re.shape, F32), sd(b_re.shape, F32)],
    )(a_re, a_im, log_dt, b_re, b_im)


def _s5_prep_bwd(a_re, a_im, log_dt, b_re, b_im, dlbr_s, dlbi_s, dBr, dBi):
    def body(ar_ref, ai_ref, ld_ref, br_ref, bi_ref, dlr_ref, dli_ref, dBr_ref, dBi_ref,
             dar_ref, dai_ref, dld_ref, dbr_ref, dbi_ref):
        a_re_v, a_im_v = ar_ref[...], ai_ref[...]
        dt, lbr, lbi, dn, nr, ni, pr, pi = _s5_prep_math(a_re_v, a_im_v, ld_ref[...])
        cr, ci = (pr / dn)[:, None, :], (pi / dn)[:, None, :]
        br, bi, dBr_v, dBi_v = br_ref[...], bi_ref[...], dBr_ref[...], dBi_ref[...]
        dbr_ref[...] = cr * dBr_v + ci * dBi_v
        dbi_ref[...] = cr * dBi_v - ci * dBr_v
        dcr = jnp.sum(br * dBr_v + bi * dBi_v, axis=1)
        dci = jnp.sum(br * dBi_v - bi * dBr_v, axis=1)
        dpr, dpi = dcr / dn, dci / dn
        ddn = -(dcr * pr + dci * pi) / (dn * dn)
        dnr = dpr * a_re_v - dpi * a_im_v
        dni = dpr * a_im_v + dpi * a_re_v
        dlbr = dlr_ref[...] + dnr
        dlbi = dli_ref[...] + dni
        dxr = dlbr * lbr + dlbi * lbi
        dxi = dlbi * lbr - dlbr * lbi
        dar_ref[...] = dpr * nr + dpi * ni + 2.0 * ddn * a_re_v + dxr * dt
        dai_ref[...] = dpr * ni - dpi * nr + 2.0 * ddn * a_im_v + dxi * dt
        dld_ref[...] = jnp.sum(dxr * a_re_v + dxi * a_im_v, axis=0, keepdims=True) * dt

    sd = jax.ShapeDtypeStruct
    return pl.pallas_call(
        body, name="s5_prep_bwd", in_specs=[_vm()] * 9, out_specs=[_vm()] * 5,
        out_shape=[sd(a_re.shape, F32), sd(a_re.shape, F32), sd(log_dt.shape, F32),
                   sd(b_re.shape, F32), sd(b_re.shape, F32)],
    )(a_re, a_im, log_dt, b_re, b_im, dlbr_s, dlbi_s, dBr, dBi)


def _shift_rows(v, down):
    n = v.shape[0]
    rolled = pltpu.roll(v, 1 if down else n - 1, 0)
    row = lax.broadcasted_iota(jnp.int32, v.shape, 0)
    return jnp.where(row == (0 if down else n - 1), 0.0, rolled)


def _cmul(ar, ai, br, bi):
    return ar * br - ai * bi, ar * bi + ai * br


_SEG = 8
_UNROLL = 8


def _seg_rows(k):
    if isinstance(k, int):
        return pl.ds(k * _SEG, _SEG)
    return pl.ds(pl.multiple_of(k * _SEG, _SEG), _SEG)


def _unrolled(n, step, init):
    main = n // _UNROLL

    def trip(kk, s):
        for uu in range(_UNROLL):
            s = step(kk * _UNROLL + uu, s)
        return s

    s = lax.fori_loop(0, main, trip, init)
    for r in range(main * _UNROLL, n):
        s = step(r, s)
    return s


def _interleave(src_ref, dst_ref, nk):
    def step(k, carry):
        dst_ref[_seg_rows(k), :] = src_ref[pl.ds(k, _SEG, stride=nk), :]
        return carry
    _unrolled(nk, step, 0)


def _deinterleave(src_ref, dst_ref, nk):
    def step(k, carry):
        dst_ref[pl.ds(k, _SEG, stride=nk), :] = src_ref[_seg_rows(k), :]
        return carry
    _unrolled(nk, step, 0)


def _segment_inits(er, ei, ar, ai, nk, down):
    pr, pi = ar, ai
    for _ in range(int(math.log2(nk))):
        pr, pi = _cmul(pr, pi, pr, pi)
    fr, fi = er, ei
    for _ in range(_SEG - 1):
        sr, si = _shift_rows(fr, down), _shift_rows(fi, down)
        mr, mi = _cmul(pr, pi, sr, si)
        fr, fi = er + mr, ei + mi
    return _shift_rows(fr, down), _shift_rows(fi, down)


def _scan_states(x_re, x_im, ar, ai, nk):
    lanes = ar.shape[1]

    def step(k, s):
        rows = _seg_rows(k)
        mr, mi = _cmul(ar, ai, s[0], s[1])
        return mr + x_re[rows, :], mi + x_im[rows, :]

    zero = jnp.zeros((_SEG, lanes), F32)
    er, ei = _unrolled(nk, step, (zero, zero))
    ir, ii = _segment_inits(er, ei, ar, ai, nk, True)

    def step2(k, s):
        rows = _seg_rows(k)
        mr, mi = _cmul(ar, ai, s[0], s[1])
        nr, ni = mr + x_re[rows, :], mi + x_im[rows, :]
        x_re[rows, :] = nr
        x_im[rows, :] = ni
        return nr, ni

    _unrolled(nk, step2, (ir, ii))


def _s5_tile_fwd(u, bd_re, bd_im, cd_re, cd_im, ar, ai, d, s_re, s_im, nk):
    s_re[...] = _dot(u, bd_re, _NN)
    s_im[...] = _dot(u, bd_im, _NN)
    _scan_states(s_re, s_im, ar, ai, nk)
    return _dot(s_re[...], cd_re, _NN) - _dot(s_im[...], cd_im, _NN) + d * u


def _s5_specs(L, T):
    lanes = _S5_GT * _S5_P
    u_spec = pl.BlockSpec((L, _LANES), lambda t: (0, t))
    bd_spec = pl.BlockSpec((None, _S5_GT, _S5_C, _S5_P), lambda t: (t, 0, 0, 0))
    cd_spec = pl.BlockSpec((None, _S5_GT, _S5_P, _S5_C), lambda t: (t, 0, 0, 0))
    lam_spec = pl.BlockSpec((None, 1, lanes), lambda t: (t, 0, 0))
    d_spec = pl.BlockSpec((1, _LANES), lambda t: (0, t))
    return lanes, u_spec, bd_spec, cd_spec, lam_spec, d_spec


def _fill_block_diag(dst_ref, blocks_ref):
    _, a, b = blocks_ref.shape
    dst_ref[...] = jnp.zeros(dst_ref.shape, F32)
    for g in range(_S5_GT):
        dst_ref[g * a:(g + 1) * a, g * b:(g + 1) * b] = blocks_ref[g]


def _take_block_diag(dst_ref, v):
    _, a, b = dst_ref.shape
    for g in range(_S5_GT):
        dst_ref[g] = v[g * a:(g + 1) * a, g * b:(g + 1) * b]


def _s5_dense(bdr, bdi, cdr, cdi, dense):
    for src, dst in zip((bdr, bdi, cdr, cdi), dense):
        _fill_block_diag(dst, src)
    return [dst[...] for dst in dense]


def _s5_dense_scratch(lanes):
    return [pltpu.VMEM((_LANES, lanes), F32), pltpu.VMEM((_LANES, lanes), F32),
            pltpu.VMEM((lanes, _LANES), F32), pltpu.VMEM((lanes, _LANES), F32)]


def _s5_fwd(u, bd_re, bd_im, cd_re, cd_im, lam_re, lam_im, d):
    L, Wd = u.shape
    T = Wd // _LANES
    nk = L // _SEG
    lanes, u_spec, bd_spec, cd_spec, lam_spec, d_spec = _s5_specs(L, T)
    s_spec = pl.BlockSpec((L, lanes), lambda t: (0, t))

    def body(u_ref, bdr, bdi, cdr, cdi, lr, li, d_ref, g_ref, s_re, s_im, up, tmp, *dense):
        ar = jnp.broadcast_to(lr[...], (_SEG, lanes))
        ai = jnp.broadcast_to(li[...], (_SEG, lanes))
        bd_re_v, bd_im_v, cd_re_v, cd_im_v = _s5_dense(bdr, bdi, cdr, cdi, dense)
        _interleave(u_ref, up, nk)
        y = _s5_tile_fwd(up[...], bd_re_v, bd_im_v, cd_re_v, cd_im_v, ar, ai, d_ref[...], s_re, s_im, nk)
        up[...] = _gelu(y)
        _deinterleave(up, tmp, nk)
        g_ref[...] = tmp[...].astype(g_ref.dtype)

    return pl.pallas_call(
        body, name="s5_fwd", grid=(T,),
        in_specs=[u_spec, bd_spec, bd_spec, cd_spec, cd_spec, lam_spec, lam_spec, d_spec],
        out_specs=[u_spec, s_spec, s_spec],
        out_shape=[jax.ShapeDtypeStruct((L, Wd), _MXU), jax.ShapeDtypeStruct((L, T * lanes), F32),
                   jax.ShapeDtypeStruct((L, T * lanes), F32)],
        scratch_shapes=[pltpu.VMEM((L, _LANES), F32) for _ in range(2)] + _s5_dense_scratch(lanes),
        compiler_params=_cparams(("arbitrary",)),
    )(u, bd_re, bd_im, cd_re, cd_im, lam_re, lam_im, d)


def _s5_bwd(u, dg, states_re, states_im, bd_re, bd_im, cd_re, cd_im, lam_re, lam_im, d):
    L, Wd = u.shape
    T = Wd // _LANES
    nk = L // _SEG
    lanes, u_spec, bd_spec, cd_spec, lam_spec, d_spec = _s5_specs(L, T)
    s_spec = pl.BlockSpec((L, lanes), lambda t: (0, t))

    def body(u_ref, dg_ref, s_re, s_im, bdr, bdi, cdr, cdi, lr, li, d_ref,
             du_ref, dbdr, dbdi, dcdr, dcdi, dlr, dli, dd_ref, g_re, g_im, up, dgp, tmp, *dense):
        ar = jnp.broadcast_to(lr[...], (_SEG, lanes))
        ai = jnp.broadcast_to(li[...], (_SEG, lanes))
        bd_re_v, bd_im_v, cd_re_v, cd_im_v = _s5_dense(bdr, bdi, cdr, cdi, dense)
        _interleave(u_ref, up, nk)
        _interleave(dg_ref, dgp, nk)
        uv, dv = up[...], d_ref[...]
        y = _dot(s_re[...], cd_re_v, _NN) - _dot(s_im[...], cd_im_v, _NN) + dv * uv
        dy = dgp[...] * _gelu_grad(y)
        dd_ref[...] = jnp.sum(dy * uv, axis=0, keepdims=True)
        dyb = dy.astype(_MXU)
        _take_block_diag(dcdr, _dot(dyb, s_re[...], _TN))
        _take_block_diag(dcdi, -_dot(dyb, s_im[...], _TN))
        g_re[...] = _dot(dyb, cd_re_v, _NT)
        g_im[...] = -_dot(dyb, cd_im_v, _NT)

        nai = -ai

        def step(j, s):
            rows = _seg_rows(nk - 1 - j)
            mr, mi = _cmul(ar, nai, s[0], s[1])
            return mr + g_re[rows, :], mi + g_im[rows, :]

        zero = jnp.zeros((_SEG, lanes), F32)
        er, ei = _unrolled(nk, step, (zero, zero))
        ir, ii = _segment_inits(er, ei, ar, nai, nk, False)

        def acc_lam(gr, gi, pr, pi, acc):
            return acc[0] + gr * pr + gi * pi, acc[1] + gi * pr - gr * pi

        def step2(j, carry):
            s, acc = carry
            k = nk - 1 - j
            rows = _seg_rows(k)
            mr, mi = _cmul(ar, nai, s[0], s[1])
            nr, ni = mr + g_re[rows, :], mi + g_im[rows, :]
            g_re[rows, :] = nr
            g_im[rows, :] = ni
            prev = _seg_rows(k - 1)
            return (nr, ni), acc_lam(nr, ni, s_re[prev, :], s_im[prev, :], acc)

        (g0r, g0i), acc = _unrolled(nk - 1, step2, ((ir, ii), (zero, zero)))
        first = _seg_rows(0)
        mr, mi = _cmul(ar, nai, g0r, g0i)
        nr, ni = mr + g_re[first, :], mi + g_im[first, :]
        g_re[first, :] = nr
        g_im[first, :] = ni
        last = _seg_rows(nk - 1)
        acc = acc_lam(nr, ni, _shift_rows(s_re[last, :], True), _shift_rows(s_im[last, :], True), acc)
        dlr[...] = jnp.sum(acc[0], axis=0, keepdims=True)
        dli[...] = jnp.sum(acc[1], axis=0, keepdims=True)

        gtr, gti = g_re[...].astype(_MXU), g_im[...].astype(_MXU)
        ub = uv.astype(_MXU)
        _take_block_diag(dbdr, _dot(ub, gtr, _TN))
        _take_block_diag(dbdi, _dot(ub, gti, _TN))
        dgp[...] = _dot(gtr, bd_re_v, _NT) + _dot(gti, bd_im_v, _NT) + dy * dv
        _deinterleave(dgp, tmp, nk)
        du_ref[...] = tmp[...].astype(du_ref.dtype)

    sd = jax.ShapeDtypeStruct
    big = sd((T, _S5_GT, _S5_C, _S5_P), F32)
    return pl.pallas_call(
        body, name="s5_bwd", grid=(T,),
        in_specs=[u_spec, u_spec, s_spec, s_spec, bd_spec, bd_spec, cd_spec, cd_spec, lam_spec, lam_spec, d_spec],
        out_specs=[u_spec, bd_spec, bd_spec, bd_spec, bd_spec, lam_spec, lam_spec, d_spec],
        out_shape=[sd((L, Wd), _MXU), big, big, big, big, sd((T, 1, lanes), F32), sd((T, 1, lanes), F32),
                   sd((1, Wd), F32)],
        scratch_shapes=[pltpu.VMEM((L, lanes), F32) for _ in range(2)]
        + [pltpu.VMEM((L, _LANES), F32) for _ in range(3)] + _s5_dense_scratch(lanes),
        compiler_params=_cparams(("arbitrary",)),
    )(u, dg, states_re, states_im, bd_re, bd_im, cd_re, cd_im, lam_re, lam_im, d)


def _half_tile(R, few_arrays=False):
    for t in ((512, 704, 128) if few_arrays else (256, 352, 128)):
        if R % t == 0:
            return t
    raise ValueError(R)


def _cast_shard(name, w, layer, kind, R, C):
    tr = _half_tile(R, True)
    nr = R // tr

    def body(w_ref, o_ref):
        o_ref[...] = w_ref[...].astype(o_ref.dtype)

    if kind == "col":
        in_map = lambda h, i: (layer, h * nr + i, 0)
    else:
        in_map = lambda h, i: (layer, i, h)
    return pl.pallas_call(
        body, name=name, grid=(2, nr), in_specs=[pl.BlockSpec((None, tr, C), in_map)],
        out_specs=pl.BlockSpec((None, tr, C), lambda h, i: (h, i, 0)),
        out_shape=jax.ShapeDtypeStruct((2, R, C), _WIRE),
        compiler_params=_cparams(("arbitrary", "arbitrary")),
    )(w)


def _adam_math(w, g, m, v):
    m2 = _B1 * m + (1.0 - _B1) * g
    v2 = _B2 * v + (1.0 - _B2) * (g * g)
    m_hat = m2 / (1.0 - _B1 ** _STEP)
    v_hat = v2 / (1.0 - _B2 ** _STEP)
    delta = -_LR * (m_hat / (jnp.sqrt(v_hat) + _AEPS) + _WD * w)
    return delta, m2, v2


def _adamw_big(name, w, m, v, layer, pair, kind, R, C, c, after, prev):
    tr = _half_tile(R, few_arrays=C <= 1024 and R % 512 == 0)
    nr = R // tr

    def body(c_ref, w_ref, m_ref, v_ref, own_ref, other_ref, *rest):
        go_ref, d_ref, mo_ref, vo_ref = rest[-4:]
        g = jnp.where(pl.program_id(0) == c_ref[0], own_ref[...], other_ref[...])
        delta, m2, v2 = _adam_math(w_ref[...], g, m_ref[...], v_ref[...])
        go_ref[...] = g
        d_ref[...] = delta
        mo_ref[...] = m2
        vo_ref[...] = v2

    if kind == "col":
        nat = pl.BlockSpec((None, tr, C), lambda h, i, c_ref: (layer, h * nr + i, 0))
    else:
        nat = pl.BlockSpec((None, tr, C), lambda h, i, c_ref: (layer, i, h))

    def gspec(own):
        return pl.BlockSpec((tr, C), lambda h, i, c_ref: (jnp.where((h == c_ref[0]) == own, i, 0), 0))

    carried = list(prev) if prev is not None else []
    gs = pltpu.PrefetchScalarGridSpec(
        num_scalar_prefetch=1, grid=(2, nr),
        in_specs=[nat, nat, nat, gspec(True), gspec(False), _any()] + [_any()] * len(carried),
        out_specs=[nat] * 4)
    sd = jax.ShapeDtypeStruct(w.shape, F32)
    return pl.pallas_call(
        body, name=name, grid_spec=gs, out_shape=[sd] * 4,
        input_output_aliases={7 + k: k for k in range(len(carried))},
        compiler_params=_cparams(("arbitrary", "arbitrary")),
    )(c, w, m, v, pair[0], pair[1], after, *carried)


def _adamw_flat(w, g, m, v):
    rows = w.shape[0]
    tr = rows // 8 if rows % 64 == 0 else rows

    def body(w_ref, g_ref, m_ref, v_ref, d_ref, mo_ref, vo_ref):
        delta, m2, v2 = _adam_math(w_ref[...], g_ref[...], m_ref[...], v_ref[...])
        d_ref[...] = delta
        mo_ref[...] = m2
        vo_ref[...] = v2

    spec = pl.BlockSpec((tr, _LANES), lambda i: (i, 0))
    sd = jax.ShapeDtypeStruct(w.shape, F32)
    return pl.pallas_call(
        body, name="adamw_small", grid=(rows // tr,), in_specs=[spec] * 4, out_specs=[spec] * 3,
        out_shape=[sd] * 3, compiler_params=_cparams(("arbitrary",)),
    )(w, g, m, v)


def _place():
    x, y, c = lax.axis_index("x"), lax.axis_index("y"), lax.axis_index("c")
    chips = [(1 - x, y), (x, 1 - y), (1 - x, 1 - y)]
    return x, y, c, 2 * x + y, chips


def _any():
    return pl.BlockSpec(memory_space=pl.ANY)


def _remote(src, dst, ssem, rsem, dev):
    return pltpu.make_async_remote_copy(src_ref=src, dst_ref=dst, send_sem=ssem, recv_sem=rsem,
                                        device_id=dev, device_id_type=_MESH)


def _allgather(name, shards):
    n = len(shards)

    def body(*refs):
        s_refs, g_refs = refs[:n], refs[n:2 * n]
        send0, recv0, send1, recv1, send2, recv2 = refs[2 * n:]
        x, y, c, q, _ = _place()
        sib, xn, yn = (x, y, 1 - c), (1 - x, y, c), (x, 1 - y, c)
        qx, qy, qd = 2 * (1 - x) + y, 2 * x + (1 - y), 2 * (1 - x) + (1 - y)
        _handshake([sib, xn, yn])
        own = [_remote(s_refs[a], g_refs[a].at[q], send0.at[a], recv0.at[a], sib) for a in range(n)]

        def pieces(a):
            g, half = g_refs[a], s_refs[a].shape[1] // 2
            return [g.at[qx, c], g.at[qy, c], g.at[qd, c, pl.ds(0, half)], g.at[qd, c, pl.ds(half, half)]]

        def relayed(a):
            g, half = g_refs[a], s_refs[a].shape[1] // 2
            return [(g.at[qx, c, pl.ds(0, half)], yn), (g.at[qy, c, pl.ds(half, half)], xn)]

        first = []
        for a in range(n):
            first.append(_remote(s_refs[a].at[c], g_refs[a].at[q, c], send1.at[4 * a], recv1.at[4 * a], xn))
            first.append(_remote(s_refs[a].at[c], g_refs[a].at[q, c], send1.at[4 * a + 1], recv1.at[4 * a + 1], yn))
        for cp in first + own:
            cp.start()
        later = []
        for a in range(n):
            land = pieces(a)
            for j in range(4):
                k = 4 * a + j
                _remote(land[j], land[j], send1.at[k], recv1.at[k], xn).wait_recv()
                if j < 2:
                    src, to = relayed(a)[j]
                    cp = _remote(src, src, send1.at[k + 2], recv1.at[k + 2], to)
                    cp.start()
                    later.append(cp)
                cp = _remote(land[j], land[j], send2.at[k], recv2.at[k], sib)
                cp.start()
                later.append(cp)
        for a in range(n):
            g, half = g_refs[a], s_refs[a].shape[1] // 2
            theirs = [g.at[qx, 1 - c], g.at[qy, 1 - c], g.at[qd, 1 - c, pl.ds(0, half)],
                      g.at[qd, 1 - c, pl.ds(half, half)]]
            for j in range(4):
                _remote(theirs[j], theirs[j], send2.at[4 * a + j], recv2.at[4 * a + j], sib).wait_recv()
        for cp in own:
            cp.wait()
        for cp in first + later:
            cp.wait_send()

    return _sequencer(name, _ID_GATHER, body, shards,
                      [jax.ShapeDtypeStruct((4,) + s.shape, s.dtype) for s in shards], [n, n] + [4 * n] * 4)


def _handshake(peers):
    barrier = pltpu.get_barrier_semaphore()
    for peer in peers:
        pl.semaphore_signal(barrier, inc=1, device_id=peer, device_id_type=_MESH)
    pl.semaphore_wait(barrier, len(peers))


_ID_SIBLING, _ID_CHIPS, _ID_GATHER, _ID_ALL = 1, 2, 3, 4


def _sequencer(name, collective_id, body, ins, out_types, sem_counts):
    mesh = plsc.ScalarSubcoreMesh(axis_name="seq", num_cores=1)
    moved = sum(math.prod(o.shape) * jnp.dtype(o.dtype).itemsize for o in out_types)
    return pl.kernel(
        body, name=name, out_type=out_types, mesh=mesh,
        scratch_types=[pltpu.SemaphoreType.DMA((k,)) for k in sem_counts],
        compiler_params=pltpu.CompilerParams(collective_id=collective_id),
        cost_estimate=pl.CostEstimate(flops=0, transcendentals=0, bytes_accessed=2 * moved,
                                      remote_bytes_transferred=moved),
    )(*ins)


def _swap_halves(name, grads):
    n = len(grads)

    def body(*refs):
        g_refs, t_refs = refs[:n], refs[n:2 * n]
        send, recv = refs[2 * n:]
        x, y, c, _, _ = _place()
        _handshake([(x, y, 1 - c)])
        cps = [_remote(g_refs[a].at[1 - c], t_refs[a], send.at[a], recv.at[a], (x, y, 1 - c)) for a in range(n)]
        for cp in cps:
            cp.start()
        for cp in cps:
            cp.wait()

    return _sequencer(name, _ID_SIBLING, body, grads,
                      [jax.ShapeDtypeStruct(g.shape[1:], g.dtype) for g in grads], [n, n])


def _chip_sum(name, g, t, after, core):
    _, _, R, C = g.shape
    tr = _half_tile(R, True)

    def body(c_ref, g_ref, t_ref, after_ref, o_ref):
        o_ref[...] = (g_ref[...].astype(F32) + t_ref[...].astype(F32)).astype(o_ref.dtype)

    gs = pltpu.PrefetchScalarGridSpec(
        num_scalar_prefetch=1, grid=(4, R // tr),
        in_specs=[pl.BlockSpec((None, None, tr, C), lambda r, i, c_ref: (c_ref[0], r, i, 0)),
                  pl.BlockSpec((None, tr, C), lambda r, i, c_ref: (r, i, 0)), _any()],
        out_specs=pl.BlockSpec((None, tr, C), lambda r, i, c_ref: (r, i, 0)))
    return pl.pallas_call(
        body, name=name, grid_spec=gs, out_shape=jax.ShapeDtypeStruct((4, R, C), _WIRE),
        compiler_params=_cparams(("arbitrary", "arbitrary")),
    )(core, g, t, after)


def _scatter_parts(name, parts):
    n = len(parts)

    def body(*refs):
        p_refs, t_refs = refs[:n], refs[n:2 * n]
        send, recv = refs[2 * n:]
        x, y, c, q, chips = _place()
        _handshake([(rx, ry, c) for rx, ry in chips])
        cps = []
        for a in range(n):
            for j, (rx, ry) in enumerate(chips):
                k = 3 * a + j
                cps.append(_remote(p_refs[a].at[2 * rx + ry], t_refs[a].at[q], send.at[k], recv.at[k], (rx, ry, c)))
        for cp in cps:
            cp.start()
        for a in range(n):
            for j, (rx, ry) in enumerate(chips):
                k = 3 * a + j
                land = t_refs[a].at[2 * rx + ry]
                _remote(land, land, send.at[k], recv.at[k], (rx, ry, c)).wait_recv()
        for cp in cps:
            cp.wait_send()

    return _sequencer(name, _ID_CHIPS, body, parts,
                      [jax.ShapeDtypeStruct(p.shape, p.dtype) for p in parts], [3 * n, 3 * n])


def _sum_parts(name, p, t, where, after):
    _, R, C = t.shape
    tr = _half_tile(R, True)

    def body(w_ref, p_ref, t0_ref, t1_ref, t2_ref, after_ref, o_ref):
        o_ref[...] = (p_ref[...].astype(F32) + t0_ref[...].astype(F32)
                      + t1_ref[...].astype(F32) + t2_ref[...].astype(F32))

    def part(slot):
        return pl.BlockSpec((None, tr, C), lambda i, w_ref: (w_ref[slot], i, 0))

    gs = pltpu.PrefetchScalarGridSpec(
        num_scalar_prefetch=1, grid=(R // tr,), in_specs=[part(0), part(1), part(2), part(3), _any()],
        out_specs=pl.BlockSpec((tr, C), lambda i, w_ref: (i, 0)))
    return pl.pallas_call(
        body, name=name, grid_spec=gs, out_shape=jax.ShapeDtypeStruct((R, C), F32),
        compiler_params=_cparams(("arbitrary",)),
    )(where, p, t, t, t, after)


def _send_halves(name, halves):
    n = len(halves)

    def body(*refs):
        h_refs, o_refs = refs[:n], refs[n:2 * n]
        send, recv = refs[2 * n:]
        x, y, c, _, _ = _place()
        _handshake([(x, y, 1 - c)])
        cps = [_remote(h_refs[a], o_refs[a], send.at[a], recv.at[a], (x, y, 1 - c)) for a in range(n)]
        for cp in cps:
            cp.start()
        for cp in cps:
            cp.wait()

    return _sequencer(name, _ID_SIBLING, body, halves,
                      [jax.ShapeDtypeStruct(h.shape, h.dtype) for h in halves], [n, n])


class _Order:
    def __init__(self):
        self.tok = None
        x, y, c, q, chips = _place()
        self.core = c.astype(jnp.int32).reshape(1)
        self.where = jnp.stack([q] + [2 * rx + ry for rx, ry in chips]).astype(jnp.int32)

    def tie(self, x):
        return x if self.tok is None else lax.optimization_barrier((x, self.tok))[0]

    def done(self, outs):
        self.tok = outs[0]
        return outs


class _Reduction:
    def __init__(self, tag, grads):
        self.tag, self.grads = tag, grads
        self.swapped = _swap_halves(f"rs_swap_{tag}", grads)

    def scatter(self, seq):
        self.parts = [seq.done([_chip_sum(f"rs_chipsum_{self.tag}_{a}", g, t, seq.tok, seq.core)])[0]
                      for a, (g, t) in enumerate(zip(self.grads, self.swapped))]
        self.landed = _scatter_parts(f"rs_scatter_{self.tag}", self.parts)

    def finish(self, seq):
        halves = [seq.done([_sum_parts(f"rs_sum_{self.tag}_{a}", p, t, seq.where, seq.tok)])[0]
                  for a, (p, t) in enumerate(zip(self.parts, self.landed))]
        return list(zip(halves, _send_halves(f"rs_join_{self.tag}", halves)))


def _allreduce_small(name, v):
    _, R, _ = v.shape

    def body(v_ref, o_ref, land, acc, send1, recv1, send2, recv2):
        x, y, c = lax.axis_index("x"), lax.axis_index("y"), lax.axis_index("c")
        me = 4 * x + 2 * y + c
        peers = []
        for k in range(1, 8):
            dx, dy, dc = (k >> 2) & 1, (k >> 1) & 1, k & 1
            px, py, pc = (1 - x if dx else x), (1 - y if dy else y), (1 - c if dc else c)
            peers.append((k, (px, py, pc), 4 * px + 2 * py + pc))
        land[me] = v_ref[me]
        out1 = [_remote(v_ref.at[pid], land.at[me], send1.at[k], recv1.at[k], dev) for k, dev, pid in peers]
        for cp in out1:
            cp.start()
        for k, dev, pid in peers:
            _remote(land.at[pid], land.at[pid], send1.at[k], recv1.at[k], dev).wait_recv()
        total = land[0]
        for j in range(1, 8):
            total = total + land[j]
        acc[...] = total
        o_ref[me] = total
        out2 = [_remote(acc, o_ref.at[me], send2.at[k], recv2.at[k], dev) for k, dev, pid in peers]
        for cp in out2:
            cp.start()
        for k, dev, pid in peers:
            _remote(o_ref.at[pid], o_ref.at[pid], send2.at[k], recv2.at[k], dev).wait_recv()
        for cp in out1 + out2:
            cp.wait_send()

    return pl.pallas_call(
        body, name=name, in_specs=[_vm()], out_specs=_vm(),
        out_shape=jax.ShapeDtypeStruct(v.shape, F32),
        scratch_shapes=[pltpu.VMEM(v.shape, F32), pltpu.VMEM((R, _LANES), F32)]
        + [pltpu.SemaphoreType.DMA((8,)) for _ in range(4)],
        compiler_params=pltpu.CompilerParams(vmem_limit_bytes=_VMEM_LIMIT),
    )(v)


def _all_peers():
    x, y, c = lax.axis_index("x"), lax.axis_index("y"), lax.axis_index("c")
    peers = []
    for k in range(1, 8):
        px, py, pc = (1 - x if k & 4 else x), (1 - y if k & 2 else y), (1 - c if k & 1 else c)
        peers.append((k, (px, py, pc), 4 * px + 2 * py + pc))
    return 4 * x + 2 * y + c, peers


def _exchange_slices(name, v):
    def body(v_ref, land, send, recv):
        me, peers = _all_peers()
        _handshake([dev for _, dev, _ in peers])
        cps = [_remote(v_ref.at[pid], land.at[me], send.at[k], recv.at[k], dev) for k, dev, pid in peers]
        for cp in cps:
            cp.start()
        for k, dev, pid in peers:
            _remote(land.at[pid], land.at[pid], send.at[k], recv.at[k], dev).wait_recv()
        for cp in cps:
            cp.wait_send()

    return _sequencer(name, _ID_ALL, body, [v], [jax.ShapeDtypeStruct(v.shape, v.dtype)], [8, 8])[0]


def _sum_slices(name, v, landed, after):
    _, R, _ = v.shape

    def body(v_ref, land_ref, after_ref, o_ref):
        me, peers = _all_peers()
        acc = v_ref[me]
        for _, _, pid in peers:
            acc = acc + land_ref[pid]
        o_ref[...] = acc

    return pl.pallas_call(
        body, name=name, in_specs=[_vm(), _vm(), _any()], out_specs=_vm(),
        out_shape=jax.ShapeDtypeStruct((R, _LANES), F32),
        compiler_params=pltpu.CompilerParams(vmem_limit_bytes=_VMEM_LIMIT),
    )(v, landed, after)


def _broadcast_slices(name, s):
    def body(s_ref, out, send, recv):
        me, peers = _all_peers()
        _handshake([dev for _, dev, _ in peers])
        cps = [_remote(s_ref, out.at[me], send.at[k], recv.at[k], dev) for k, dev, pid in peers]
        for cp in cps:
            cp.start()
        for k, dev, pid in peers:
            _remote(out.at[pid], out.at[pid], send.at[k], recv.at[k], dev).wait_recv()
        for cp in cps:
            cp.wait_send()

    return _sequencer(name, _ID_ALL, body, [s], [jax.ShapeDtypeStruct((8,) + s.shape, s.dtype)], [8, 8])[0]


_WEIGHT_NAMES = ['norm_mix', 'norm_ffn', 'norm_ple', 'norm_final', 'gm_w_in', 'gm_ln_g', 'gm_ln_b', 'gm_w_s',
                 'gm_b_s', 'gm_w_out', 's5_w_in', 's5_a_re', 's5_a_im', 's5_log_dt', 's5_b_re', 's5_b_im',
                 's5_c_re', 's5_c_im', 's5_d', 's5_w_out', 'ffn_w1', 'ffn_w3', 'ffn_w2', 'ple_w_gate', 'ple_w_proj']
_BIG = {'gm_w_in': 'col', 'gm_w_out': 'row', 's5_w_in': 'row', 's5_w_out': 'col', 'ffn_w1': 'col',
        'ffn_w3': 'col', 'ffn_w2': 'row', 'ple_w_gate': 'row', 'ple_w_proj': 'col'}


_VIEW = {'s5_a_re': (0, 2, 1), 's5_a_im': (0, 2, 1), 's5_b_re': (0, 2, 3, 1), 's5_b_im': (0, 2, 3, 1),
         's5_c_re': (0, 2, 3, 1), 's5_c_im': (0, 2, 3, 1)}


def _to_view(name, a):
    return jnp.transpose(a, _VIEW[name]) if name in _VIEW else a


def _from_view(name, a):
    if name not in _VIEW:
        return a
    perm = _VIEW[name]
    return jnp.transpose(a, [perm.index(i) for i in range(len(perm))])


def _rc(kind, shard_shape):
    rows, cols = shard_shape[-2:]
    return (rows // 2, cols) if kind == "col" else (rows, cols // 2)


def _pack(vecs, rows_multiple):
    flat = jnp.concatenate([a.reshape(-1).astype(F32) for a in vecs])
    unit = rows_multiple * _LANES
    pad = (-flat.shape[0]) % unit
    return jnp.pad(flat, (0, pad)).reshape(-1, _LANES)


def _unpack(buf, shapes):
    flat = buf.reshape(-1)
    out, off = [], 0
    for s in shapes:
        n = math.prod(s)
        out.append(flat[off:off + n].reshape(s))
        off += n
    return out


def _ident(accs, ex):
    return accs


def _add_resid(accs, ex):
    return [accs[0] + ex[0]]


def _swiglu_epi(accs, ex):
    a, b = accs
    return [a, b, a * _sig(a) * b]


def _swiglu_bwd_epi(accs, ex):
    df = accs[0]
    a, b = ex[0].astype(F32), ex[1].astype(F32)
    sa = _sig(a)
    return [df * b * (sa * (1.0 + a * (1.0 - sa))), df * (a * sa)]


def _ple_epi(accs, ex):
    gt = _sig(accs[0])
    return [ex[0] + gt * ex[1], gt]


def _glu_epi(accs, ex):
    val, sg = accs[0], _sig(accs[1])
    return [ex[0] + val * sg, val, sg]


def kernel(x, p, norm_mix, norm_ffn, norm_ple, norm_final, gm_w_in, gm_ln_g, gm_ln_b, gm_w_s, gm_b_s, gm_w_out, s5_w_in, s5_a_re, s5_a_im, s5_log_dt, s5_b_re, s5_b_im, s5_c_re, s5_c_im, s5_d, s5_w_out, ffn_w1, ffn_w3, ffn_w2, ple_w_gate, ple_w_proj, loss_target, m_norm_mix, m_norm_ffn, m_norm_ple, m_norm_final, m_gm_w_in, m_gm_ln_g, m_gm_ln_b, m_gm_w_s, m_gm_b_s, m_gm_w_out, m_s5_w_in, m_s5_a_re, m_s5_a_im, m_s5_log_dt, m_s5_b_re, m_s5_b_im, m_s5_c_re, m_s5_c_im, m_s5_d, m_s5_w_out, m_ffn_w1, m_ffn_w3, m_ffn_w2, m_ple_w_gate, m_ple_w_proj, v_norm_mix, v_norm_ffn, v_norm_ple, v_norm_final, v_gm_w_in, v_gm_ln_g, v_gm_ln_b, v_gm_w_s, v_gm_b_s, v_gm_w_out, v_s5_w_in, v_s5_a_re, v_s5_a_im, v_s5_log_dt, v_s5_b_re, v_s5_b_im, v_s5_c_re, v_s5_c_im, v_s5_d, v_s5_w_out, v_ffn_w1, v_ffn_w3, v_ffn_w2, v_ple_w_gate, v_ple_w_proj):
    env = dict(locals())
    w = {n: env[n] for n in _WEIGHT_NAMES}
    mom = {n: env["m_" + n] for n in _WEIGHT_NAMES}
    var = {n: env["v_" + n] for n in _WEIGHT_NAMES}
    xs, tgt = x[0], loss_target[0]
    L, D = xs.shape
    depth = norm_mix.shape[0]
    qx, qy = lax.axis_index("x"), lax.axis_index("y")
    q = 2 * qx + qy

    def gather(tag, items):
        shards = []
        for name, layer in items:
            kind = _BIG[name]
            R, C = _rc(kind, w[name].shape)
            shards.append(_cast_shard(f"cast_{name}{layer}", w[name], layer, kind, R, C))
        full = _allgather(f"ag_{tag}", shards)
        return {it: _W(f, _BIG[it[0]]) for it, f in zip(items, full)}

    W = {}
    mixers = [["gm_w_in", "gm_w_out"], ["s5_w_in", "s5_w_out"]]
    for i in range(depth):
        for n in mixers[i]:
            W.update(gather(f"{n}", [(n, 0)]))
        W.update(gather(f"ffn_up{i}", [("ffn_w1", i), ("ffn_w3", i)]))
        W.update(gather(f"ffn_down{i}", [("ffn_w2", i)]))
        W.update(gather(f"ple{i}", [("ple_w_gate", i), ("ple_w_proj", i)]))

    d_slots = jnp.zeros((4, D // 4), F32)
    d_slots = lax.dynamic_update_slice(d_slots, s5_d.astype(F32), (q, 0))
    d_sum = _allreduce_small("ar_s5_d", _pack([d_slots], 64).reshape(8, -1, _LANES))
    d_full = (d_sum.reshape(-1)[:D] * 0.5).reshape(1, D)

    def ffn_fwd(i, xin):
        hf = _rms_fwd(f"rms_ffn{i}", xin, norm_ffn[i:i + 1])
        a, b, f = _mm_nn(f"ffn_up{i}", hf, [W["ffn_w1", i], W["ffn_w3", i]], 1024, 1408, ffn_w2.shape[1] * 4,
                         [], [_MXU, _MXU, _MXU], _swiglu_epi, tm=1024)
        xo = _mm_nn(f"ffn_down{i}", f, [W["ffn_w2", i]], 1408, 1024, D, [xin], [F32], _add_resid, tm=1024)[0]
        return xo, (xin, hf, a, b, f)

    def ple_fwd(i, xin):
        hp = _rms_fwd(f"rms_ple{i}", xin, norm_ple[i:i + 1])
        pi = lax.optimization_barrier((p[i, 0], hp))[0]
        pp = _mm_nn(f"ple_proj{i}", pi, [W["ple_w_proj", i]], 128, 512, D, [], [F32], _ident, tm=2048)[0]
        xo, gt = _mm_nn(f"ple_gate{i}", hp, [W["ple_w_gate", i]], 512, 1024, D, [xin, pp], [F32, _MXU], _ple_epi,
                        tm=1024)
        return xo, (xin, hp, pi, pp, gt)

    h0 = _rms_fwd("rms_mix0", xs, norm_mix[0:1])
    z = _mm_nn("gm_in", h0, [W["gm_w_in", 0]], 1024, 1024, 2 * D, [], [F32], _ident, tm=1024)[0]
    bsT = gm_b_s[0].T
    gm_m = _gmlp_fwd(z, gm_ln_g, gm_ln_b, gm_w_s[0], bsT)
    x1 = _mm_nn("gm_out", gm_m, [W["gm_w_out", 0]], 512, 1024, D, [xs], [F32], _add_resid, tm=1024)[0]
    x2, ffn0 = ffn_fwd(0, x1)
    x3, ple0 = ple_fwd(0, x2)

    T = D // _LANES
    lanes = _S5_GT * _S5_P
    sv = {n: _to_view(n, w[n])[0] for n in _VIEW}
    a_re, a_im, log_dt = sv["s5_a_re"], sv["s5_a_im"], s5_log_dt
    lbr, lbi, Bbar_re, Bbar_im = _s5_prep(a_re, a_im, log_dt, sv["s5_b_re"], sv["s5_b_im"])

    def to_bd(B):
        return jnp.transpose(B.reshape(_S5_P, _S5_C, T, _S5_GT), (2, 3, 1, 0))

    def to_cd(cw):
        return jnp.transpose(cw.reshape(_S5_C, _S5_P, T, _S5_GT), (2, 3, 1, 0))

    def to_lam(v):
        return jnp.transpose(v).reshape(T, 1, lanes)

    bd_re, bd_im = to_bd(Bbar_re), to_bd(Bbar_im)
    cd_re, cd_im = to_cd(sv["s5_c_re"]), to_cd(sv["s5_c_im"])
    lam_re, lam_im = to_lam(lbr), to_lam(lbi)

    h1 = _rms_fwd("rms_mix1", x3, norm_mix[1:2])
    u = _mm_nn("s5_in", h1, [W["s5_w_in", 0]], 512, 1024, D, [], [F32], _ident, tm=1024)[0]
    s5_g, s5_re, s5_im = _s5_fwd(u, bd_re, bd_im, cd_re, cd_im, lam_re, lam_im, d_full)
    x4, glu_val, glu_sg = _mm_nn("s5_out", s5_g, [W["s5_w_out", 0], W["s5_w_out", 0]], 1024, 1024, D, [x3],
                                 [F32, _MXU, _MXU], _glu_epi, tm=1024, cb_offsets=[0, 2])
    x5, ffn1 = ffn_fwd(1, x4)
    x6, ple1 = ple_fwd(1, x5)

    dx, d_norm_final, loss_rows = _loss_head(x6, norm_final[None], tgt)

    small = {}

    seq = _Order()
    tie, done = seq.tie, seq.done
    d_norm_ple, d_norm_ffn, d_norm_mix = [None] * depth, [None] * depth, [None] * depth
    reduced = {}

    def keep(names, layer, pairs):
        for n, pr in zip(names, pairs):
            reduced[n, layer] = pr

    ple_names, up_names, down_names = ["ple_w_gate", "ple_w_proj"], ["ffn_w1", "ffn_w3"], ["ffn_w2"]

    def ple_bwd(i, dxo, saved):
        xin, hp, pi, pp, gt = saved
        dpre, dpp = done(_ple_bwd_elem(tie(dxo), pp, gt))
        dwg = done(_mm_tn(f"ple_gate_dw{i}", tie(hp), [dpre], "row", 512, 1024, 512, 1024))[0]
        dwp = done(_mm_tn(f"ple_proj_dw{i}", tie(pi), [dpp], "col", 128, 512, 128, 512))[0]
        red = _Reduction(f"ple{i}", [dwg, dwp])
        dhp = done(_mm_nt(f"ple_gate_dx{i}", [tie(dpre)], [W["ple_w_gate", i]], 512, 1024, [], [F32], _ident,
                          tm=2048))[0]
        dxin, dxin_mxu, dg = done(_rms_bwd(f"rms_ple_bwd{i}", tie(dhp), xin, norm_ple[i:i + 1], dxo))
        return dxin, dxin_mxu, dg, red

    def ffn_bwd(i, dxo, dxo_mxu, saved, before_up):
        xin, hf, a, b, f = saved
        dw2 = done(_mm_tn(f"ffn_down_dw{i}", tie(f), [dxo_mxu], "row", 1408, 1024, 1408, 1024))[0]
        r_down = _Reduction(f"ffd{i}", [dw2])
        da, db = done(_mm_nt(f"ffn_down_dx{i}", [tie(dxo_mxu)], [W["ffn_w2", i]], 1408, 1024, [a, b], [_MXU, _MXU],
                             _swiglu_bwd_epi, tm=1024))
        for step in before_up:
            step()
        r_down.scatter(seq)
        dw1, dw3 = done(_mm_tn(f"ffn_up_dw{i}", tie(hf), [da, db], "col", 1024, 1408, 1024, 1408))
        r_up = _Reduction(f"ffu{i}", [dw1, dw3])
        dhf = done(_mm_nt(f"ffn_up_dx{i}", [tie(da), db], [W["ffn_w1", i], W["ffn_w3", i]], 1024, 1408, [], [F32],
                          _ident, tm=1024))[0]
        dxin, dxin_mxu, dg = done(_rms_bwd(f"rms_ffn_bwd{i}", tie(dhf), xin, norm_ffn[i:i + 1], dxo))
        r_up.scatter(seq)
        return dxin, dxin_mxu, dg, r_down, r_up

    dx, dx_mxu, d_norm_ple[1], r_ple1 = ple_bwd(1, dx, ple1)
    dx, _, d_norm_ffn[1], r_down1, r_up1 = ffn_bwd(1, dx, dx_mxu, ffn1, [lambda: r_ple1.scatter(seq)])

    do = done([_glu_bwd_elem(tie(dx), glu_val, glu_sg)])[0]
    dw_s5out = done(_mm_tn("s5_out_dw", tie(s5_g), [do], "col", 1024, 1024, 1024, 1024))[0]
    r_s5out = _Reduction("s5out", [dw_s5out])
    dgy = done(_mm_nt("s5_out_dx", [tie(do)], [W["s5_w_out", 0]], 1024, 1024, [], [F32], _ident, tm=1024))[0]
    keep(ple_names, 1, r_ple1.finish(seq))
    keep(down_names, 1, r_down1.finish(seq))
    du, dbd_re, dbd_im, dcd_re, dcd_im, dl_re, dl_im, dd = done(_s5_bwd(
        tie(u), dgy, s5_re, s5_im, bd_re, bd_im, cd_re, cd_im, lam_re, lam_im, d_full))
    r_s5out.scatter(seq)
    dw_s5in = done(_mm_tn("s5_in_dw", tie(h1), [du], "row", 512, 1024, 512, 1024))[0]
    r_s5in = _Reduction("s5in", [dw_s5in])
    dh1 = done(_mm_nt("s5_in_dx", [tie(du)], [W["s5_w_in", 0]], 512, 1024, [], [F32], _ident, tm=2048))[0]
    dx, _, d_norm_mix[1] = done(_rms_bwd("rms_mix1_bwd", tie(dh1), x3, norm_mix[1:2], dx))
    keep(up_names, 1, r_up1.finish(seq))
    r_s5in.scatter(seq)

    def from_bd(t):
        return jnp.transpose(t, (3, 2, 0, 1)).reshape(_S5_P, _S5_C, T * _S5_GT)

    def from_cdT(t):
        return jnp.transpose(t, (2, 3, 0, 1)).reshape(_S5_C, _S5_P, T * _S5_GT)

    def from_lam(t):
        return jnp.transpose(t.reshape(T * _S5_GT, _S5_P))

    da_re, da_im, dlog_dt, db_re, db_im = _s5_prep_bwd(
        a_re, a_im, log_dt, sv["s5_b_re"], sv["s5_b_im"], from_lam(dl_re), from_lam(dl_im),
        from_bd(dbd_re), from_bd(dbd_im))
    small["s5_a_re"], small["s5_a_im"], small["s5_log_dt"] = da_re[None], da_im[None], dlog_dt
    small["s5_b_re"], small["s5_b_im"] = db_re[None], db_im[None]
    small["s5_c_re"], small["s5_c_im"] = from_cdT(dcd_re)[None], from_cdT(dcd_im)[None]

    dx, dx_mxu, d_norm_ple[0], r_ple0 = ple_bwd(0, dx, ple0)
    keep(["s5_w_out"], 0, r_s5out.finish(seq))
    xin0, hf0, a0, b0, f0 = ffn0
    da0, db0 = done(_mm_nt("ffn_down_dx0", [tie(dx_mxu)], [W["ffn_w2", 0]], 1408, 1024, [a0, b0], [_MXU, _MXU],
                           _swiglu_bwd_epi, tm=1024))
    r_ple0.scatter(seq)
    dw1, dw3 = done(_mm_tn("ffn_up_dw0", tie(hf0), [da0, db0], "col", 1024, 1408, 1024, 1408))
    r_up0 = _Reduction("ffu0", [dw1, dw3])
    keep(["s5_w_in"], 0, r_s5in.finish(seq))
    dw2 = done(_mm_tn("ffn_down_dw0", tie(f0), [dx_mxu], "row", 1408, 1024, 1408, 1024))[0]
    r_down0 = _Reduction("ffd0", [dw2])
    r_up0.scatter(seq)
    dhf0 = done(_mm_nt("ffn_up_dx0", [tie(da0), db0], [W["ffn_w1", 0], W["ffn_w3", 0]], 1024, 1408, [], [F32], _ident,
                       tm=1024))[0]
    dx, dx_mxu, d_norm_ffn[0] = done(_rms_bwd("rms_ffn_bwd0", tie(dhf0), xin0, norm_ffn[0:1], dx))
    keep(ple_names, 0, r_ple0.finish(seq))
    r_down0.scatter(seq)

    dw_gmout = done(_mm_tn("gm_out_dw", tie(gm_m), [dx_mxu], "row", 512, 1024, 512, 1024))[0]
    r_gmout = _Reduction("gmout", [dw_gmout])
    dgm = done(_mm_nt("gm_out_dx", [tie(dx_mxu)], [W["gm_w_out", 0]], 512, 1024, [], [F32], _ident, tm=2048))[0]
    dz, dws, dbsT, dlng, dlnb = done(_gmlp_bwd(tie(z), dgm, gm_ln_g, gm_ln_b, gm_w_s[0], bsT))
    dw_gmin = done(_mm_tn("gm_in_dw", tie(h0), [dz], "col", 1024, 1024, 1024, 1024))[0]
    r_gmin = _Reduction("gmin", [dw_gmin])
    dh0 = done(_mm_nt("gm_in_dx", [tie(dz)], [W["gm_w_in", 0]], 1024, 1024, [], [F32], _ident, tm=1024))[0]
    dx, _, d_norm_mix[0] = done(_rms_bwd("rms_mix0_bwd", tie(dh0), xs, norm_mix[0:1], dx))
    grad_x = dx[None]

    small["norm_mix"], small["norm_ffn"] = jnp.concatenate(d_norm_mix), jnp.concatenate(d_norm_ffn)
    small["norm_ple"], small["norm_final"] = jnp.concatenate(d_norm_ple), d_norm_final[0]
    small["gm_ln_g"], small["gm_ln_b"], small["gm_w_s"] = dlng, dlnb, dws[None]
    small["gm_b_s"] = dbsT[:, :_GM_HEADS].T[None]
    small["s5_d"] = dd

    small_names = [n for n in _WEIGHT_NAMES if n not in _BIG]
    packed = _pack([small[n] for n in small_names] + [loss_rows[:, :1]], 64).reshape(8, -1, _LANES)
    grads, deltas, new_m, new_v = {}, {}, {}, {}
    my_c = lax.axis_index("c")

    def adamw(n, layer, prev):
        kind = _BIG[n]
        R, C = _rc(kind, w[n].shape)
        return done(_adamw_big(f"adamw_{n}{layer}", w[n], mom[n], var[n], layer, reduced[n, layer], kind, R, C,
                               seq.core, seq.tok, prev))

    def adamw_last(names):
        for n in names:
            grads[n], deltas[n], new_m[n], new_v[n] = adamw(n, 0, late.get(n))

    late = {}
    for n in down_names + up_names + ple_names:
        late[n] = adamw(n, 1, None)
    keep(up_names, 0, r_up0.finish(seq))
    r_gmout.scatter(seq)
    r_gmin.scatter(seq)
    landed = _exchange_slices("ar_small_in", packed)
    adamw_last(["s5_w_in", "s5_w_out"] + ple_names)
    keep(down_names, 0, r_down0.finish(seq))
    adamw_last(up_names)
    keep(["gm_w_out"], 0, r_gmout.finish(seq))
    mine = done([_sum_slices("ar_small_sum", packed, landed, seq.tok)])[0]
    spread = _broadcast_slices("ar_small_out", mine)
    keep(["gm_w_in"], 0, r_gmin.finish(seq))
    adamw_last(down_names + ["gm_w_out", "gm_w_in"])
    summed = lax.dynamic_update_slice(spread, mine[None], (4 * qx + 2 * qy + my_c, 0, 0))
    *red_list, loss_sum = _unpack(summed, [small[n].shape for n in small_names] + [(1, 1)])
    red_small, loss = dict(zip(small_names, red_list)), loss_sum.reshape(())
    red_small["s5_d"] = lax.dynamic_slice(red_small["s5_d"], (0, q * (D // 4)), (1, D // 4))

    def views(src):
        return [_to_view(n, src[n]) for n in small_names]

    shapes = [v_.shape for v_ in views(w)]
    dl, mo, vo = _adamw_flat(_pack(views(w), 64), _pack([red_small[n] for n in small_names], 64),
                             _pack(views(mom), 64), _pack(views(var), 64))
    for n, g_, d_, m_, v_ in zip(small_names, [red_small[n] for n in small_names], _unpack(dl, shapes),
                                 _unpack(mo, shapes), _unpack(vo, shapes)):
        grads[n], deltas[n], new_m[n], new_v[n] = [_from_view(n, t_).reshape(w[n].shape) for t_ in (g_, d_, m_, v_)]

    return (loss, grad_x, *[grads[n] for n in _WEIGHT_NAMES], *[deltas[n] for n in _WEIGHT_NAMES],
            *[new_m[n] for n in _WEIGHT_NAMES], *[new_v[n] for n in _WEIGHT_NAMES])
```

```python
import math

import jax
import jax.numpy as jnp
from jax import lax
from jax.experimental import pallas as pl
from jax.experimental.pallas import tpu as pltpu
from jax.experimental.pallas import tpu_sc as plsc

F32 = jnp.float32
_MXU = jnp.bfloat16
_WIRE = jnp.bfloat16
_EPS = 1e-6
_VMEM_LIMIT = 56 * 1024 * 1024
_LANES = 128
_MESH = pl.DeviceIdType.MESH

_LR, _B1, _B2, _AEPS, _WD, _STEP = 0.001, 0.9, 0.999, 1e-08, 0.01, 10

_GM_CHUNK = 128
_GM_HEADS = 16
_S5_GT = 8
_S5_P = 64
_S5_C = 16

_NN = (((1,), (0,)), ((), ()))
_NT = (((1,), (1,)), ((), ()))
_TN = (((0,), (0,)), ((), ()))


def _cparams(sem):
    return pltpu.CompilerParams(dimension_semantics=sem, vmem_limit_bytes=_VMEM_LIMIT)


def _sig(x):
    return 0.5 * jnp.tanh(0.5 * x) + 0.5


_GC = math.sqrt(2.0 / math.pi)


def _gelu(x):
    return 0.5 * x * (1.0 + jnp.tanh(_GC * (x + 0.044715 * (x * x * x))))


def _gelu_grad(x):
    t = jnp.tanh(_GC * (x + 0.044715 * (x * x * x)))
    return 0.5 * (1.0 + t) + 0.5 * x * (1.0 - t * t) * (_GC * (1.0 + 3.0 * 0.044715 * x * x))


def _dot(a, b, dn):
    return lax.dot_general(a.astype(_MXU), b.astype(_MXU), dn, preferred_element_type=F32)


class _W:
    def __init__(self, arr, kind):
        self.a, self.kind = arr, kind
        self.R, self.C = arr.shape[2], arr.shape[3]

    def full_shape(self):
        return (2 * self.R, 4 * self.C) if self.kind == "col" else (4 * self.R, 2 * self.C)


def _part_index(kind, R, C, tr, tc, rb, cb):
    nr, nc = R // tr, C // tc
    if kind == "col":
        return cb // nc, rb // nr, rb % nr, cb % nc
    return rb // nr, cb // nc, rb % nr, cb % nc


def _wspec(w, tr, tc, rb_fn, cb_fn):
    assert w.R % tr == 0 and w.C % tc == 0, (w.R, w.C, tr, tc)

    def imap(i, j, k):
        return _part_index(w.kind, w.R, w.C, tr, tc, rb_fn(i, j, k), cb_fn(i, j, k))

    return pl.BlockSpec((None, None, tr, tc), imap)


def _gspec(kind, R, C, tr, tc):
    assert R % tr == 0 and C % tc == 0, (R, C, tr, tc)

    def imap(i, j, k):
        part, half, rbi, cbi = _part_index(kind, R, C, tr, tc, i, j)
        return half, part, rbi, cbi

    return pl.BlockSpec((None, None, tr, tc), imap)


def _mm(name, grid, a_ops, b_ops, pairs, acc_shape, n_acc, extras, outs, epilogue):
    nk = grid[2]
    na, nb, ne, no = len(a_ops), len(b_ops), len(extras), len(outs)

    def body(*refs):
        a_refs = refs[:na]
        b_refs = refs[na:na + nb]
        e_refs = refs[na + nb:na + nb + ne]
        o_refs = refs[na + nb + ne:na + nb + ne + no]
        acc_refs = refs[na + nb + ne + no:]
        k = pl.program_id(2)

        def products():
            sums = [None] * n_acc
            for ai, bi, ci, dn in pairs:
                d = _dot(a_refs[ai][...], b_refs[bi][...], dn)
                sums[ci] = d if sums[ci] is None else sums[ci] + d
            return sums

        def finish(accs):
            res = epilogue(accs, [e[...] for e in e_refs])
            for o, r in zip(o_refs, res):
                o[...] = r.astype(o.dtype)

        if nk == 1:
            finish(products())
            return

        @pl.when(k == 0)
        def _():
            for acc, d in zip(acc_refs, products()):
                acc[...] = d

        @pl.when(jnp.logical_and(k > 0, k < nk - 1))
        def _():
            for acc, d in zip(acc_refs, products()):
                acc[...] += d

        @pl.when(k == nk - 1)
        def _():
            finish([acc[...] + d for acc, d in zip(acc_refs, products())])

    ops = list(a_ops) + list(b_ops) + list(extras)
    return pl.pallas_call(
        body, name=name, grid=grid,
        in_specs=[s for _, s in ops],
        out_specs=[s for _, s in outs],
        out_shape=[s for s, _ in outs],
        scratch_shapes=[pltpu.VMEM(acc_shape, F32) for _ in range(n_acc if nk > 1 else 0)],
        compiler_params=_cparams(("parallel", "parallel", "arbitrary")),
    )(*[a for a, _ in ops])


def _bs(shape, fn):
    return pl.BlockSpec(shape, fn)


def _tile_m(L):
    return min(L, 512)


def _mm_nn(name, x, ws, tk, tn, n_out, extras, outs_sd, epilogue, tm=None, cb_offsets=None):
    M, K = x.shape
    tm = min(M, tm or _tile_m(M))
    grid = (M // tm, n_out // tn, K // tk)
    a_ops = [(x, _bs((tm, tk), lambda i, j, k: (i, k)))]
    cb_offsets = cb_offsets or [0] * len(ws)
    b_ops = [(w.a, _wspec(w, tk, tn, lambda i, j, k: k, (lambda off: lambda i, j, k: j + off)(off)))
             for w, off in zip(ws, cb_offsets)]
    pairs = [(0, bi, bi, _NN) for bi in range(len(ws))]
    mn = _bs((tm, tn), lambda i, j, k: (i, j))
    ex = [(e, mn) for e in extras]
    outs = [(jax.ShapeDtypeStruct((M, n_out), dt), mn) for dt in outs_sd]
    return _mm(name, grid, a_ops, b_ops, pairs, (tm, tn), len(ws), ex, outs, epilogue)


def _mm_nt(name, xs, ws, tn, tk, extras, outs_sd, epilogue, tm=None):
    M, Nw = xs[0].shape
    Kw = ws[0].full_shape()[0]
    tm = min(M, tm or _tile_m(M))
    grid = (M // tm, Kw // tn, Nw // tk)
    a_ops = [(x, _bs((tm, tk), lambda i, j, k: (i, k))) for x in xs]
    b_ops = [(w.a, _wspec(w, tn, tk, lambda i, j, k: j, lambda i, j, k: k)) for w in ws]
    pairs = [(i, i, 0, _NT) for i in range(len(ws))]
    mn = _bs((tm, tn), lambda i, j, k: (i, j))
    ex = [(e, mn) for e in extras]
    outs = [(jax.ShapeDtypeStruct((M, Kw), dt), mn) for dt in outs_sd]
    return _mm(name, grid, a_ops, b_ops, pairs, (tm, tn), 1, ex, outs, epilogue)


def _mm_tn(name, x, dys, kind, R, C, tm, tn, tk=None):
    L, Kw = x.shape
    Nw = dys[0].shape[1]
    tk = tk or min(L, 1024)
    grid = (Kw // tm, Nw // tn, L // tk)
    a_ops = [(x, _bs((tk, tm), lambda i, j, k: (k, i)))]
    b_ops = [(dy, _bs((tk, tn), lambda i, j, k: (k, j))) for dy in dys]
    pairs = [(0, bi, bi, _TN) for bi in range(len(dys))]
    gs = _gspec(kind, R, C, tm, tn)
    outs = [(jax.ShapeDtypeStruct((2, 4, R, C), _WIRE), gs) for _ in dys]
    return _mm(name, grid, a_ops, b_ops, pairs, (tm, tn), len(dys), [], outs, lambda accs, ex: accs)


def _row_tile(L):
    return min(L, 256)


def _rowwise(name, body, ins, outs, L, acc_outs=()):
    tr = _row_tile(L)
    n_in, n_out = len(ins), len(outs)

    def kbody(*refs):
        i_refs, o_refs, a_refs = refs[:n_in], refs[n_in:n_in + n_out], refs[n_in + n_out:]
        res, sums = body(*[r[...] for r in i_refs])
        for o, r in zip(o_refs, res):
            o[...] = r.astype(o.dtype)
        if a_refs:
            @pl.when(pl.program_id(0) == 0)
            def _():
                for a in a_refs:
                    a[...] = jnp.zeros(a.shape, F32)
            for a, s in zip(a_refs, sums):
                a[...] += s

    in_specs = []
    for arr, kind in ins:
        if kind == "row":
            in_specs.append(pl.BlockSpec((tr, arr.shape[1]), lambda i: (i, 0)))
        else:
            in_specs.append(pl.BlockSpec(arr.shape, lambda i: (0, 0)))
    out_specs = [pl.BlockSpec((tr, c), lambda i: (i, 0)) for c, _ in outs]
    out_shape = [jax.ShapeDtypeStruct((L, c), dt) for c, dt in outs]
    out_specs += [pl.BlockSpec((1, c), lambda i: (0, 0)) for c in acc_outs]
    out_shape += [jax.ShapeDtypeStruct((1, c), F32) for c in acc_outs]
    return pl.pallas_call(
        kbody, name=name, grid=(L // tr,), in_specs=in_specs, out_specs=out_specs, out_shape=out_shape,
        compiler_params=_cparams(("arbitrary",)),
    )(*[a for a, _ in ins])


def _rms_fwd(name, x, g):
    def body(xv, gv):
        r = lax.rsqrt(jnp.mean(xv * xv, axis=-1, keepdims=True) + _EPS)
        return [xv * r * gv], []
    return _rowwise(name, body, [(x, "row"), (g, "vec")], [(x.shape[1], _MXU)], x.shape[0])[0]


def _rms_bwd(name, dh, x, g, dres):
    def body(dhv, xv, gv, dr):
        r = lax.rsqrt(jnp.mean(xv * xv, axis=-1, keepdims=True) + _EPS)
        xh = xv * r
        dxh = dhv * gv
        dx = dr + r * (dxh - xh * jnp.mean(dxh * xh, axis=-1, keepdims=True))
        return [dx, dx], [jnp.sum(dhv * xh, axis=0, keepdims=True)]
    D = x.shape[1]
    return _rowwise(name, body, [(dh, "row"), (x, "row"), (g, "vec"), (dres, "row")], [(D, F32), (D, _MXU)],
                    x.shape[0], [D])


def _loss_head(x, g, target):
    D = x.shape[1]

    def body(xv, gv, tv):
        r = lax.rsqrt(jnp.mean(xv * xv, axis=-1, keepdims=True) + _EPS)
        xh = xv * r
        e = xh * gv - tv
        dy = e * (1.0 / D)
        dxh = dy * gv
        dx = r * (dxh - xh * jnp.mean(dxh * xh, axis=-1, keepdims=True))
        row_loss = 0.5 * jnp.mean(e * e, axis=-1, keepdims=True)
        lsum = jnp.sum(row_loss, axis=0, keepdims=True) + jnp.zeros((1, _LANES), F32)
        return [dx], [jnp.sum(dy * xh, axis=0, keepdims=True), lsum]
    return _rowwise("loss_head", body, [(x, "row"), (g, "vec"), (target, "row")], [(D, F32)], x.shape[0], [D, _LANES])


def _ple_bwd_elem(dx, pp, gt):
    def body(dxv, ppv, gtv):
        gt32 = gtv.astype(F32)
        return [dxv * ppv * gt32 * (1.0 - gt32), dxv * gt32], []
    D = dx.shape[1]
    return _rowwise("ple_bwd_elem", body, [(dx, "row"), (pp, "row"), (gt, "row")], [(D, _MXU), (D, _MXU)], dx.shape[0])


def _glu_bwd_elem(dx, val, sg):
    def body(dxv, vv, sv):
        v32, s32 = vv.astype(F32), sv.astype(F32)
        return [jnp.concatenate([dxv * s32, dxv * v32 * s32 * (1.0 - s32)], axis=1)], []
    D = dx.shape[1]
    return _rowwise("glu_bwd_elem", body, [(dx, "row"), (val, "row"), (sg, "row")], [(2 * D, _MXU)], dx.shape[0])[0]


def _gm_common(z, ln_g, ln_b, wc_bf, bsT):
    W = z.shape[1] // 2
    zu, zv = z[:, :W], z[:, W:]
    u, v = _gelu(zu), _gelu(zv)
    mu = jnp.mean(v, axis=-1, keepdims=True)
    vc = v - mu
    rstd = lax.rsqrt(jnp.mean(vc * vc, axis=-1, keepdims=True) + _EPS)
    vh = vc * rstd
    vn = vh * ln_g + ln_b
    vnb = vn.astype(_MXU)
    svs = []
    for h in range(_GM_HEADS):
        sl = slice(h * _LANES, (h + 1) * _LANES)
        svs.append(_dot(wc_bf[h], vnb[:, sl], _NN) + bsT[:, h:h + 1])
    return zu, zv, u, vh, rstd, vnb, svs


def _causal(w):
    t = lax.broadcasted_iota(jnp.int32, w.shape, w.ndim - 2)
    s = lax.broadcasted_iota(jnp.int32, w.shape, w.ndim - 1)
    return jnp.where(s <= t, w, jnp.zeros_like(w))


def _gmlp_fwd(z, ln_g, ln_b, w_s, bsT):
    L, W2 = z.shape
    W = W2 // 2

    def body(z_ref, g_ref, b_ref, ws_ref, bs_ref, m_ref):
        wc = _causal(ws_ref[...]).astype(_MXU)
        _, _, u, _, _, _, svs = _gm_common(z_ref[...], g_ref[...], b_ref[...], wc, bs_ref[...])
        for h in range(_GM_HEADS):
            sl = slice(h * _LANES, (h + 1) * _LANES)
            m_ref[:, sl] = (u[:, sl] * svs[h]).astype(m_ref.dtype)

    return pl.pallas_call(
        body, name="gmlp_fwd", grid=(L // _GM_CHUNK,),
        in_specs=[pl.BlockSpec((_GM_CHUNK, W2), lambda n: (n, 0)),
                  pl.BlockSpec((1, W), lambda n: (0, 0)), pl.BlockSpec((1, W), lambda n: (0, 0)),
                  pl.BlockSpec(w_s.shape, lambda n: (0, 0, 0)), pl.BlockSpec(bsT.shape, lambda n: (0, 0))],
        out_specs=pl.BlockSpec((_GM_CHUNK, W), lambda n: (n, 0)),
        out_shape=jax.ShapeDtypeStruct((L, W), _MXU),
        compiler_params=_cparams(("arbitrary",)),
    )(z, ln_g, ln_b, w_s, bsT)


def _gmlp_bwd(z, dm, ln_g, ln_b, w_s, bsT):
    L, W2 = z.shape
    W = W2 // 2
    T = _GM_CHUNK

    def body(z_ref, dm_ref, g_ref, b_ref, ws_ref, bs_ref, dz_ref, dws_ref, dbs_ref, dg_ref, db_ref):
        @pl.when(pl.program_id(0) == 0)
        def _():
            dws_ref[...] = jnp.zeros(dws_ref.shape, F32)
            dbs_ref[...] = jnp.zeros(dbs_ref.shape, F32)
            dg_ref[...] = jnp.zeros(dg_ref.shape, F32)
            db_ref[...] = jnp.zeros(db_ref.shape, F32)

        wc = _causal(ws_ref[...]).astype(_MXU)
        ln_g_v = g_ref[...]
        zu, zv, u, vh, rstd, vnb, svs = _gm_common(z_ref[...], ln_g_v, b_ref[...], wc, bs_ref[...])
        dmv = dm_ref[...]
        lane = lax.broadcasted_iota(jnp.int32, (T, _LANES), 1)
        dbs = jnp.zeros((T, _LANES), F32)
        dvn_parts = []
        for h in range(_GM_HEADS):
            sl = slice(h * _LANES, (h + 1) * _LANES)
            dsv = dmv[:, sl] * u[:, sl]
            dz_ref[:, sl] = (dmv[:, sl] * svs[h] * _gelu_grad(zu[:, sl])).astype(dz_ref.dtype)
            dbs = dbs + jnp.where(lane == h, jnp.sum(dsv, axis=1, keepdims=True), 0.0)
            dsvb = dsv.astype(_MXU)
            dws_ref[h] += _dot(dsvb, vnb[:, sl], _NT)
            dvn_parts.append(_dot(wc[h], dsvb, _TN))
        dbs_ref[...] += dbs
        dvn = jnp.concatenate(dvn_parts, axis=1)
        dg_ref[...] += jnp.sum(dvn * vh, axis=0, keepdims=True)
        db_ref[...] += jnp.sum(dvn, axis=0, keepdims=True)
        dxh = dvn * ln_g_v
        dv = rstd * (dxh - jnp.mean(dxh, axis=-1, keepdims=True) - vh * jnp.mean(dxh * vh, axis=-1, keepdims=True))
        dz_ref[:, W:] = (dv * _gelu_grad(zv)).astype(dz_ref.dtype)

        @pl.when(pl.program_id(0) == pl.num_programs(0) - 1)
        def _():
            dws_ref[...] = _causal(dws_ref[...])

    return pl.pallas_call(
        body, name="gmlp_bwd", grid=(L // T,),
        in_specs=[pl.BlockSpec((T, W2), lambda n: (n, 0)), pl.BlockSpec((T, W), lambda n: (n, 0)),
                  pl.BlockSpec((1, W), lambda n: (0, 0)), pl.BlockSpec((1, W), lambda n: (0, 0)),
                  pl.BlockSpec(w_s.shape, lambda n: (0, 0, 0)), pl.BlockSpec(bsT.shape, lambda n: (0, 0))],
        out_specs=[pl.BlockSpec((T, W2), lambda n: (n, 0)),
                   pl.BlockSpec(w_s.shape, lambda n: (0, 0, 0)), pl.BlockSpec((T, _LANES), lambda n: (0, 0)),
                   pl.BlockSpec((1, W), lambda n: (0, 0)), pl.BlockSpec((1, W), lambda n: (0, 0))],
        out_shape=[jax.ShapeDtypeStruct((L, W2), _MXU), jax.ShapeDtypeStruct(w_s.shape, F32),
                   jax.ShapeDtypeStruct((T, _LANES), F32),
                   jax.ShapeDtypeStruct((1, W), F32), jax.ShapeDtypeStruct((1, W), F32)],
        compiler_params=_cparams(("arbitrary",)),
    )(z, dm, ln_g, ln_b, w_s, bsT)


def _s5_prep_math(a_re, a_im, log_dt):
    dt = jnp.exp(log_dt)
    xr, xi = a_re * dt, a_im * dt
    e = jnp.exp(xr)
    lbr, lbi = e * jnp.cos(xi), e * jnp.sin(xi)
    dn = a_re * a_re + a_im * a_im
    nr, ni = lbr - 1.0, lbi
    pr, pi = nr * a_re + ni * a_im, ni * a_re - nr * a_im
    return dt, lbr, lbi, dn, nr, ni, pr, pi


def _vm():
    return pl.BlockSpec(memory_space=pltpu.VMEM)


def _s5_prep(a_re, a_im, log_dt, b_re, b_im):
    def body(ar_ref, ai_ref, ld_ref, br_ref, bi_ref, lbr_ref, lbi_ref, Br_ref, Bi_ref):
        _, lbr, lbi, dn, _, _, pr, pi = _s5_prep_math(ar_ref[...], ai_ref[...], ld_ref[...])
        cr, ci = (pr / dn)[:, None, :], (pi / dn)[:, None, :]
        lbr_ref[...] = lbr
        lbi_ref[...] = lbi
        br, bi = br_ref[...], bi_ref[...]
        Br_ref[...] = cr * br - ci * bi
        Bi_ref[...] = cr * bi + ci * br

    sd = jax.ShapeDtypeStruct
    return pl.pallas_call(
        body, name="s5_prep", in_specs=[_vm()] * 5, out_specs=[_vm()] * 4,
        out_shape=[sd(a_re.shape, F32), sd(a_re.shape, F32), sd(b_re.shape, F32), sd(b_re.shape, F32)],
    )(a_re, a_im, log_dt, b_re, b_im)


def _s5_prep_bwd(a_re, a_im, log_dt, b_re, b_im, dlbr_s, dlbi_s, dBr, dBi):
    def body(ar_ref, ai_ref, ld_ref, br_ref, bi_ref, dlr_ref, dli_ref, dBr_ref, dBi_ref,
             dar_ref, dai_ref, dld_ref, dbr_ref, dbi_ref):
        a_re_v, a_im_v = ar_ref[...], ai_ref[...]
        dt, lbr, lbi, dn, nr, ni, pr, pi = _s5_prep_math(a_re_v, a_im_v, ld_ref[...])
        cr, ci = (pr / dn)[:, None, :], (pi / dn)[:, None, :]
        br, bi, dBr_v, dBi_v = br_ref[...], bi_ref[...], dBr_ref[...], dBi_ref[...]
        dbr_ref[...] = cr * dBr_v + ci * dBi_v
        dbi_ref[...] = cr * dBi_v - ci * dBr_v
        dcr = jnp.sum(br * dBr_v + bi * dBi_v, axis=1)
        dci = jnp.sum(br * dBi_v - bi * dBr_v, axis=1)
        dpr, dpi = dcr / dn, dci / dn
        ddn = -(dcr * pr + dci * pi) / (dn * dn)
        dnr = dpr * a_re_v - dpi * a_im_v
        dni = dpr * a_im_v + dpi * a_re_v
        dlbr = dlr_ref[...] + dnr
        dlbi = dli_ref[...] + dni
        dxr = dlbr * lbr + dlbi * lbi
        dxi = dlbi * lbr - dlbr * lbi
        dar_ref[...] = dpr * nr + dpi * ni + 2.0 * ddn * a_re_v + dxr * dt
        dai_ref[...] = dpr * ni - dpi * nr + 2.0 * ddn * a_im_v + dxi * dt
        dld_ref[...] = jnp.sum(dxr * a_re_v + dxi * a_im_v, axis=0, keepdims=True) * dt

    sd = jax.ShapeDtypeStruct
    return pl.pallas_call(
        body, name="s5_prep_bwd", in_specs=[_vm()] * 9, out_specs=[_vm()] * 5,
        out_shape=[sd(a_re.shape, F32), sd(a_re.shape, F32), sd(log_dt.shape, F32),
                   sd(b_re.shape, F32), sd(b_re.shape, F32)],
    )(a_re, a_im, log_dt, b_re, b_im, dlbr_s, dlbi_s, dBr, dBi)


def _shift_rows(v, down):
    n = v.shape[0]
    rolled = pltpu.roll(v, 1 if down else n - 1, 0)
    row = lax.broadcasted_iota(jnp.int32, v.shape, 0)
    return jnp.where(row == (0 if down else n - 1), 0.0, rolled)


def _cmul(ar, ai, br, bi):
    return ar * br - ai * bi, ar * bi + ai * br


_SEG = 8
_UNROLL = 8


def _seg_rows(k):
    if isinstance(k, int):
        return pl.ds(k * _SEG, _SEG)
    return pl.ds(pl.multiple_of(k * _SEG, _SEG), _SEG)


def _unrolled(n, step, init):
    main = n // _UNROLL

    def trip(kk, s):
        for uu in range(_UNROLL):
            s = step(kk * _UNROLL + uu, s)
        return s

    s = lax.fori_loop(0, main, trip, init)
    for r in range(main * _UNROLL, n):
        s = step(r, s)
    return s


def _interleave(src_ref, dst_ref, nk):
    def step(k, carry):
        dst_ref[_seg_rows(k), :] = src_ref[pl.ds(k, _SEG, stride=nk), :]
        return carry
    _unrolled(nk, step, 0)


def _deinterleave(src_ref, dst_ref, nk):
    def step(k, carry):
        dst_ref[pl.ds(k, _SEG, stride=nk), :] = src_ref[_seg_rows(k), :]
        return carry
    _unrolled(nk, step, 0)


def _segment_inits(er, ei, ar, ai, nk, down):
    pr, pi = ar, ai
    for _ in range(int(math.log2(nk))):
        pr, pi = _cmul(pr, pi, pr, pi)
    fr, fi = er, ei
    for _ in range(_SEG - 1):
        sr, si = _shift_rows(fr, down), _shift_rows(fi, down)
        mr, mi = _cmul(pr, pi, sr, si)
        fr, fi = er + mr, ei + mi
    return _shift_rows(fr, down), _shift_rows(fi, down)


def _scan_states(x_re, x_im, ar, ai, nk):
    lanes = ar.shape[1]

    def step(k, s):
        rows = _seg_rows(k)
        mr, mi = _cmul(ar, ai, s[0], s[1])
        return mr + x_re[rows, :], mi + x_im[rows, :]

    zero = jnp.zeros((_SEG, lanes), F32)
    er, ei = _unrolled(nk, step, (zero, zero))
    ir, ii = _segment_inits(er, ei, ar, ai, nk, True)

    def step2(k, s):
        rows = _seg_rows(k)
        mr, mi = _cmul(ar, ai, s[0], s[1])
        nr, ni = mr + x_re[rows, :], mi + x_im[rows, :]
        x_re[rows, :] = nr
        x_im[rows, :] = ni
        return nr, ni

    _unrolled(nk, step2, (ir, ii))


def _s5_tile_fwd(u, bd_re, bd_im, cd_re, cd_im, ar, ai, d, s_re, s_im, nk):
    s_re[...] = _dot(u, bd_re, _NN)
    s_im[...] = _dot(u, bd_im, _NN)
    _scan_states(s_re, s_im, ar, ai, nk)
    return _dot(s_re[...], cd_re, _NN) - _dot(s_im[...], cd_im, _NN) + d * u


def _s5_specs(L, T):
    lanes = _S5_GT * _S5_P
    u_spec = pl.BlockSpec((L, _LANES), lambda t: (0, t))
    bd_spec = pl.BlockSpec((None, _S5_GT, _S5_C, _S5_P), lambda t: (t, 0, 0, 0))
    cd_spec = pl.BlockSpec((None, _S5_GT, _S5_P, _S5_C), lambda t: (t, 0, 0, 0))
    lam_spec = pl.BlockSpec((None, 1, lanes), lambda t: (t, 0, 0))
    d_spec = pl.BlockSpec((1, _LANES), lambda t: (0, t))
    return lanes, u_spec, bd_spec, cd_spec, lam_spec, d_spec


def _fill_block_diag(dst_ref, blocks_ref):
    _, a, b = blocks_ref.shape
    dst_ref[...] = jnp.zeros(dst_ref.shape, F32)
    for g in range(_S5_GT):
        dst_ref[g * a:(g + 1) * a, g * b:(g + 1) * b] = blocks_ref[g]


def _take_block_diag(dst_ref, v):
    _, a, b = dst_ref.shape
    for g in range(_S5_GT):
        dst_ref[g] = v[g * a:(g + 1) * a, g * b:(g + 1) * b]


def _s5_dense(bdr, bdi, cdr, cdi, dense):
    for src, dst in zip((bdr, bdi, cdr, cdi), dense):
        _fill_block_diag(dst, src)
    return [dst[...] for dst in dense]


def _s5_dense_scratch(lanes):
    return [pltpu.VMEM((_LANES, lanes), F32), pltpu.VMEM((_LANES, lanes), F32),
            pltpu.VMEM((lanes, _LANES), F32), pltpu.VMEM((lanes, _LANES), F32)]


def _s5_fwd(u, bd_re, bd_im, cd_re, cd_im, lam_re, lam_im, d):
    L, Wd = u.shape
    T = Wd // _LANES
    nk = L // _SEG
    lanes, u_spec, bd_spec, cd_spec, lam_spec, d_spec = _s5_specs(L, T)
    s_spec = pl.BlockSpec((L, lanes), lambda t: (0, t))

    def body(u_ref, bdr, bdi, cdr, cdi, lr, li, d_ref, g_ref, s_re, s_im, up, tmp, *dense):
        ar = jnp.broadcast_to(lr[...], (_SEG, lanes))
        ai = jnp.broadcast_to(li[...], (_SEG, lanes))
        bd_re_v, bd_im_v, cd_re_v, cd_im_v = _s5_dense(bdr, bdi, cdr, cdi, dense)
        _interleave(u_ref, up, nk)
        y = _s5_tile_fwd(up[...], bd_re_v, bd_im_v, cd_re_v, cd_im_v, ar, ai, d_ref[...], s_re, s_im, nk)
        up[...] = _gelu(y)
        _deinterleave(up, tmp, nk)
        g_ref[...] = tmp[...].astype(g_ref.dtype)

    return pl.pallas_call(
        body, name="s5_fwd", grid=(T,),
        in_specs=[u_spec, bd_spec, bd_spec, cd_spec, cd_spec, lam_spec, lam_spec, d_spec],
        out_specs=[u_spec, s_spec, s_spec],
        out_shape=[jax.ShapeDtypeStruct((L, Wd), _MXU), jax.ShapeDtypeStruct((L, T * lanes), F32),
                   jax.ShapeDtypeStruct((L, T * lanes), F32)],
        scratch_shapes=[pltpu.VMEM((L, _LANES), F32) for _ in range(2)] + _s5_dense_scratch(lanes),
        compiler_params=_cparams(("arbitrary",)),
    )(u, bd_re, bd_im, cd_re, cd_im, lam_re, lam_im, d)


def _s5_bwd(u, dg, states_re, states_im, bd_re, bd_im, cd_re, cd_im, lam_re, lam_im, d):
    L, Wd = u.shape
    T = Wd // _LANES
    nk = L // _SEG
    lanes, u_spec, bd_spec, cd_spec, lam_spec, d_spec = _s5_specs(L, T)
    s_spec = pl.BlockSpec((L, lanes), lambda t: (0, t))

    def body(u_ref, dg_ref, s_re, s_im, bdr, bdi, cdr, cdi, lr, li, d_ref,
             du_ref, dbdr, dbdi, dcdr, dcdi, dlr, dli, dd_ref, g_re, g_im, up, dgp, tmp, *dense):
        ar = jnp.broadcast_to(lr[...], (_SEG, lanes))
        ai = jnp.broadcast_to(li[...], (_SEG, lanes))
        bd_re_v, bd_im_v, cd_re_v, cd_im_v = _s5_dense(bdr, bdi, cdr, cdi, dense)
        _interleave(u_ref, up, nk)
        _interleave(dg_ref, dgp, nk)
        uv, dv = up[...], d_ref[...]
        y = _dot(s_re[...], cd_re_v, _NN) - _dot(s_im[...], cd_im_v, _NN) + dv * uv
        dy = dgp[...] * _gelu_grad(y)
        dd_ref[...] = jnp.sum(dy * uv, axis=0, keepdims=True)
        dyb = dy.astype(_MXU)
        _take_block_diag(dcdr, _dot(dyb, s_re[...], _TN))
        _take_block_diag(dcdi, -_dot(dyb, s_im[...], _TN))
        g_re[...] = _dot(dyb, cd_re_v, _NT)
        g_im[...] = -_dot(dyb, cd_im_v, _NT)

        nai = -ai

        def step(j, s):
            rows = _seg_rows(nk - 1 - j)
            mr, mi = _cmul(ar, nai, s[0], s[1])
            return mr + g_re[rows, :], mi + g_im[rows, :]

        zero = jnp.zeros((_SEG, lanes), F32)
        er, ei = _unrolled(nk, step, (zero, zero))
        ir, ii = _segment_inits(er, ei, ar, nai, nk, False)

        def acc_lam(gr, gi, pr, pi, acc):
            return acc[0] + gr * pr + gi * pi, acc[1] + gi * pr - gr * pi

        def step2(j, carry):
            s, acc = carry
            k = nk - 1 - j
            rows = _seg_rows(k)
            mr, mi = _cmul(ar, nai, s[0], s[1])
            nr, ni = mr + g_re[rows, :], mi + g_im[rows, :]
            g_re[rows, :] = nr
            g_im[rows, :] = ni
            prev = _seg_rows(k - 1)
            return (nr, ni), acc_lam(nr, ni, s_re[prev, :], s_im[prev, :], acc)

        (g0r, g0i), acc = _unrolled(nk - 1, step2, ((ir, ii), (zero, zero)))
        first = _seg_rows(0)
        mr, mi = _cmul(ar, nai, g0r, g0i)
        nr, ni = mr + g_re[first, :], mi + g_im[first, :]
        g_re[first, :] = nr
        g_im[first, :] = ni
        last = _seg_rows(nk - 1)
        acc = acc_lam(nr, ni, _shift_rows(s_re[last, :], True), _shift_rows(s_im[last, :], True), acc)
        dlr[...] = jnp.sum(acc[0], axis=0, keepdims=True)
        dli[...] = jnp.sum(acc[1], axis=0, keepdims=True)

        gtr, gti = g_re[...].astype(_MXU), g_im[...].astype(_MXU)
        ub = uv.astype(_MXU)
        _take_block_diag(dbdr, _dot(ub, gtr, _TN))
        _take_block_diag(dbdi, _dot(ub, gti, _TN))
        dgp[...] = _dot(gtr, bd_re_v, _NT) + _dot(gti, bd_im_v, _NT) + dy * dv
        _deinterleave(dgp, tmp, nk)
        du_ref[...] = tmp[...].astype(du_ref.dtype)

    sd = jax.ShapeDtypeStruct
    big = sd((T, _S5_GT, _S5_C, _S5_P), F32)
    return pl.pallas_call(
        body, name="s5_bwd", grid=(T,),
        in_specs=[u_spec, u_spec, s_spec, s_spec, bd_spec, bd_spec, cd_spec, cd_spec, lam_spec, lam_spec, d_spec],
        out_specs=[u_spec, bd_spec, bd_spec, bd_spec, bd_spec, lam_spec, lam_spec, d_spec],
        out_shape=[sd((L, Wd), _MXU), big, big, big, big, sd((T, 1, lanes), F32), sd((T, 1, lanes), F32),
                   sd((1, Wd), F32)],
        scratch_shapes=[pltpu.VMEM((L, lanes), F32) for _ in range(2)]
        + [pltpu.VMEM((L, _LANES), F32) for _ in range(3)] + _s5_dense_scratch(lanes),
        compiler_params=_cparams(("arbitrary",)),
    )(u, dg, states_re, states_im, bd_re, bd_im, cd_re, cd_im, lam_re, lam_im, d)


def _half_tile(R, few_arrays=False):
    for t in ((512, 704, 128) if few_arrays else (256, 352, 128)):
        if R % t == 0:
            return t
    raise ValueError(R)


def _cast_shard(name, w, layer, kind, R, C):
    tr = _half_tile(R, True)
    nr = R // tr

    def body(w_ref, o_ref):
        o_ref[...] = w_ref[...].astype(o_ref.dtype)

    if kind == "col":
        in_map = lambda h, i: (layer, h * nr + i, 0)
    else:
        in_map = lambda h, i: (layer, i, h)
    return pl.pallas_call(
        body, name=name, grid=(2, nr), in_specs=[pl.BlockSpec((None, tr, C), in_map)],
        out_specs=pl.BlockSpec((None, tr, C), lambda h, i: (h, i, 0)),
        out_shape=jax.ShapeDtypeStruct((2, R, C), _WIRE),
        compiler_params=_cparams(("arbitrary", "arbitrary")),
    )(w)


def _adam_math(w, g, m, v):
    m2 = _B1 * m + (1.0 - _B1) * g
    v2 = _B2 * v + (1.0 - _B2) * (g * g)
    m_hat = m2 / (1.0 - _B1 ** _STEP)
    v_hat = v2 / (1.0 - _B2 ** _STEP)
    delta = -_LR * (m_hat / (jnp.sqrt(v_hat) + _AEPS) + _WD * w)
    return delta, m2, v2


def _adamw_big(name, w, m, v, layer, pair, kind, R, C, c, after, prev):
    tr = _half_tile(R, few_arrays=C <= 1024 and R % 512 == 0)
    nr = R // tr

    def body(c_ref, w_ref, m_ref, v_ref, own_ref, other_ref, *rest):
        go_ref, d_ref, mo_ref, vo_ref = rest[-4:]
        g = jnp.where(pl.program_id(0) == c_ref[0], own_ref[...], other_ref[...])
        delta, m2, v2 = _adam_math(w_ref[...], g, m_ref[...], v_ref[...])
        go_ref[...] = g
        d_ref[...] = delta
        mo_ref[...] = m2
        vo_ref[...] = v2

    if kind == "col":
        nat = pl.BlockSpec((None, tr, C), lambda h, i, c_ref: (layer, h * nr + i, 0))
    else:
        nat = pl.BlockSpec((None, tr, C), lambda h, i, c_ref: (layer, i, h))

    def gspec(own):
        return pl.BlockSpec((tr, C), lambda h, i, c_ref: (jnp.where((h == c_ref[0]) == own, i, 0), 0))

    carried = list(prev) if prev is not None else []
    gs = pltpu.PrefetchScalarGridSpec(
        num_scalar_prefetch=1, grid=(2, nr),
        in_specs=[nat, nat, nat, gspec(True), gspec(False), _any()] + [_any()] * len(carried),
        out_specs=[nat] * 4)
    sd = jax.ShapeDtypeStruct(w.shape, F32)
    return pl.pallas_call(
        body, name=name, grid_spec=gs, out_shape=[sd] * 4,
        input_output_aliases={7 + k: k for k in range(len(carried))},
        compiler_params=_cparams(("arbitrary", "arbitrary")),
    )(c, w, m, v, pair[0], pair[1], after, *carried)


def _adamw_small(ws, gs, ms, vs):
    n = len(ws)

    def body(*refs):
        for i in range(n):
            w_ref, g_ref, m_ref, v_ref, d_ref, mo_ref, vo_ref = refs[i::n]
            delta, m2, v2 = _adam_math(w_ref[...], g_ref[...], m_ref[...], v_ref[...])
            d_ref[...] = delta
            mo_ref[...] = m2
            vo_ref[...] = v2

    sds = [jax.ShapeDtypeStruct(w.shape, F32) for w in ws]
    outs = pl.pallas_call(
        body, name="adamw_small", in_specs=[_vm()] * (4 * n), out_specs=[_vm()] * (3 * n), out_shape=sds * 3,
        compiler_params=pltpu.CompilerParams(vmem_limit_bytes=_VMEM_LIMIT),
    )(*ws, *gs, *ms, *vs)
    return outs[:n], outs[n:2 * n], outs[2 * n:]


def _place():
    x, y, c = lax.axis_index("x"), lax.axis_index("y"), lax.axis_index("c")
    chips = [(1 - x, y), (x, 1 - y), (1 - x, 1 - y)]
    return x, y, c, 2 * x + y, chips


def _any():
    return pl.BlockSpec(memory_space=pl.ANY)


def _remote(src, dst, ssem, rsem, dev):
    return pltpu.make_async_remote_copy(src_ref=src, dst_ref=dst, send_sem=ssem, recv_sem=rsem,
                                        device_id=dev, device_id_type=_MESH)


def _allgather(name, shards):
    n = len(shards)

    def body(*refs):
        s_refs, g_refs = refs[:n], refs[n:2 * n]
        send0, recv0, send1, recv1, send2, recv2 = refs[2 * n:]
        x, y, c, q, _ = _place()
        sib, xn, yn = (x, y, 1 - c), (1 - x, y, c), (x, 1 - y, c)
        qx, qy, qd = 2 * (1 - x) + y, 2 * x + (1 - y), 2 * (1 - x) + (1 - y)
        _handshake([sib, xn, yn])
        own = [_remote(s_refs[a], g_refs[a].at[q], send0.at[a], recv0.at[a], sib) for a in range(n)]

        def pieces(a):
            g, half = g_refs[a], s_refs[a].shape[1] // 2
            return [g.at[qx, c], g.at[qy, c], g.at[qd, c, pl.ds(0, half)], g.at[qd, c, pl.ds(half, half)]]

        def relayed(a):
            g, half = g_refs[a], s_refs[a].shape[1] // 2
            return [(g.at[qx, c, pl.ds(0, half)], yn), (g.at[qy, c, pl.ds(half, half)], xn)]

        first = []
        for a in range(n):
            first.append(_remote(s_refs[a].at[c], g_refs[a].at[q, c], send1.at[4 * a], recv1.at[4 * a], xn))
            first.append(_remote(s_refs[a].at[c], g_refs[a].at[q, c], send1.at[4 * a + 1], recv1.at[4 * a + 1], yn))
        for cp in first + own:
            cp.start()
        later = []
        for a in range(n):
            land = pieces(a)
            for j in range(4):
                k = 4 * a + j
                _remote(land[j], land[j], send1.at[k], recv1.at[k], xn).wait_recv()
                if j < 2:
                    src, to = relayed(a)[j]
                    cp = _remote(src, src, send1.at[k + 2], recv1.at[k + 2], to)
                    cp.start()
                    later.append(cp)
                cp = _remote(land[j], land[j], send2.at[k], recv2.at[k], sib)
                cp.start()
                later.append(cp)
        for a in range(n):
            g, half = g_refs[a], s_refs[a].shape[1] // 2
            theirs = [g.at[qx, 1 - c], g.at[qy, 1 - c], g.at[qd, 1 - c, pl.ds(0, half)],
                      g.at[qd, 1 - c, pl.ds(half, half)]]
            for j in range(4):
                _remote(theirs[j], theirs[j], send2.at[4 * a + j], recv2.at[4 * a + j], sib).wait_recv()
        for cp in own:
            cp.wait()
        for cp in first + later:
            cp.wait_send()

    return _sequencer(name, _ID_GATHER, body, shards,
                      [jax.ShapeDtypeStruct((4,) + s.shape, s.dtype) for s in shards], [n, n] + [4 * n] * 4)


def _handshake(peers):
    barrier = pltpu.get_barrier_semaphore()
    for peer in peers:
        pl.semaphore_signal(barrier, inc=1, device_id=peer, device_id_type=_MESH)
    pl.semaphore_wait(barrier, len(peers))


_ID_SIBLING, _ID_CHIPS, _ID_GATHER, _ID_ALL = 1, 2, 3, 4


def _sequencer(name, collective_id, body, ins, out_types, sem_counts):
    mesh = plsc.ScalarSubcoreMesh(axis_name="seq", num_cores=1)
    moved = sum(math.prod(o.shape) * jnp.dtype(o.dtype).itemsize for o in out_types)
    return pl.kernel(
        body, name=name, out_type=out_types, mesh=mesh,
        scratch_types=[pltpu.SemaphoreType.DMA((k,)) for k in sem_counts],
        compiler_params=pltpu.CompilerParams(collective_id=collective_id),
        cost_estimate=pl.CostEstimate(flops=0, transcendentals=0, bytes_accessed=2 * moved,
                                      remote_bytes_transferred=moved),
    )(*ins)


def _swap_halves(name, grads):
    n = len(grads)

    def body(*refs):
        g_refs, t_refs = refs[:n], refs[n:2 * n]
        send, recv = refs[2 * n:]
        x, y, c, _, _ = _place()
        _handshake([(x, y, 1 - c)])
        cps = [_remote(g_refs[a].at[1 - c], t_refs[a], send.at[a], recv.at[a], (x, y, 1 - c)) for a in range(n)]
        for cp in cps:
            cp.start()
        for cp in cps:
            cp.wait()

    return _sequencer(name, _ID_SIBLING, body, grads,
                      [jax.ShapeDtypeStruct(g.shape[1:], g.dtype) for g in grads], [n, n])


def _chip_sum(name, g, t, after, core):
    _, _, R, C = g.shape
    tr = _half_tile(R, True)

    def body(c_ref, g_ref, t_ref, after_ref, o_ref):
        o_ref[...] = (g_ref[...].astype(F32) + t_ref[...].astype(F32)).astype(o_ref.dtype)

    gs = pltpu.PrefetchScalarGridSpec(
        num_scalar_prefetch=1, grid=(4, R // tr),
        in_specs=[pl.BlockSpec((None, None, tr, C), lambda r, i, c_ref: (c_ref[0], r, i, 0)),
                  pl.BlockSpec((None, tr, C), lambda r, i, c_ref: (r, i, 0)), _any()],
        out_specs=pl.BlockSpec((None, tr, C), lambda r, i, c_ref: (r, i, 0)))
    return pl.pallas_call(
        body, name=name, grid_spec=gs, out_shape=jax.ShapeDtypeStruct((4, R, C), _WIRE),
        compiler_params=_cparams(("arbitrary", "arbitrary")),
    )(core, g, t, after)


def _scatter_parts(name, parts):
    n = len(parts)

    def body(*refs):
        p_refs, t_refs = refs[:n], refs[n:2 * n]
        send, recv = refs[2 * n:]
        x, y, c, q, chips = _place()
        _handshake([(rx, ry, c) for rx, ry in chips])
        cps = []
        for a in range(n):
            for j, (rx, ry) in enumerate(chips):
                k = 3 * a + j
                cps.append(_remote(p_refs[a].at[2 * rx + ry], t_refs[a].at[q], send.at[k], recv.at[k], (rx, ry, c)))
        for cp in cps:
            cp.start()
        for a in range(n):
            for j, (rx, ry) in enumerate(chips):
                k = 3 * a + j
                land = t_refs[a].at[2 * rx + ry]
                _remote(land, land, send.at[k], recv.at[k], (rx, ry, c)).wait_recv()
        for cp in cps:
            cp.wait_send()

    return _sequencer(name, _ID_CHIPS, body, parts,
                      [jax.ShapeDtypeStruct(p.shape, p.dtype) for p in parts], [3 * n, 3 * n])


def _sum_parts(name, p, t, where, after):
    _, R, C = t.shape
    tr = _half_tile(R, True)

    def body(w_ref, p_ref, t0_ref, t1_ref, t2_ref, after_ref, o_ref):
        o_ref[...] = (p_ref[...].astype(F32) + t0_ref[...].astype(F32)
                      + t1_ref[...].astype(F32) + t2_ref[...].astype(F32))

    def part(slot):
        return pl.BlockSpec((None, tr, C), lambda i, w_ref: (w_ref[slot], i, 0))

    gs = pltpu.PrefetchScalarGridSpec(
        num_scalar_prefetch=1, grid=(R // tr,), in_specs=[part(0), part(1), part(2), part(3), _any()],
        out_specs=pl.BlockSpec((tr, C), lambda i, w_ref: (i, 0)))
    return pl.pallas_call(
        body, name=name, grid_spec=gs, out_shape=jax.ShapeDtypeStruct((R, C), F32),
        compiler_params=_cparams(("arbitrary",)),
    )(where, p, t, t, t, after)


def _send_halves(name, halves):
    n = len(halves)

    def body(*refs):
        h_refs, o_refs = refs[:n], refs[n:2 * n]
        send, recv = refs[2 * n:]
        x, y, c, _, _ = _place()
        _handshake([(x, y, 1 - c)])
        cps = [_remote(h_refs[a], o_refs[a], send.at[a], recv.at[a], (x, y, 1 - c)) for a in range(n)]
        for cp in cps:
            cp.start()
        for cp in cps:
            cp.wait()

    return _sequencer(name, _ID_SIBLING, body, halves,
                      [jax.ShapeDtypeStruct(h.shape, h.dtype) for h in halves], [n, n])


class _Order:
    def __init__(self):
        self.tok = None
        x, y, c, q, chips = _place()
        self.core = c.astype(jnp.int32).reshape(1)
        self.where = jnp.stack([q] + [2 * rx + ry for rx, ry in chips]).astype(jnp.int32)

    def tie(self, x):
        return x if self.tok is None else lax.optimization_barrier((x, self.tok))[0]

    def done(self, outs):
        self.tok = outs[0]
        return outs


class _Reduction:
    def __init__(self, tag, grads):
        self.tag, self.grads = tag, grads
        self.swapped = _swap_halves(f"rs_swap_{tag}", grads)

    def scatter(self, seq):
        self.parts = [seq.done([_chip_sum(f"rs_chipsum_{self.tag}_{a}", g, t, seq.tok, seq.core)])[0]
                      for a, (g, t) in enumerate(zip(self.grads, self.swapped))]
        self.landed = _scatter_parts(f"rs_scatter_{self.tag}", self.parts)

    def finish(self, seq):
        halves = [seq.done([_sum_parts(f"rs_sum_{self.tag}_{a}", p, t, seq.where, seq.tok)])[0]
                  for a, (p, t) in enumerate(zip(self.parts, self.landed))]
        return list(zip(halves, _send_halves(f"rs_join_{self.tag}", halves)))


def _allreduce_small(name, v):
    _, R, _ = v.shape

    def body(v_ref, o_ref, land, acc, send1, recv1, send2, recv2):
        x, y, c = lax.axis_index("x"), lax.axis_index("y"), lax.axis_index("c")
        me = 4 * x + 2 * y + c
        peers = []
        for k in range(1, 8):
            dx, dy, dc = (k >> 2) & 1, (k >> 1) & 1, k & 1
            px, py, pc = (1 - x if dx else x), (1 - y if dy else y), (1 - c if dc else c)
            peers.append((k, (px, py, pc), 4 * px + 2 * py + pc))
        land[me] = v_ref[me]
        out1 = [_remote(v_ref.at[pid], land.at[me], send1.at[k], recv1.at[k], dev) for k, dev, pid in peers]
        for cp in out1:
            cp.start()
        for k, dev, pid in peers:
            _remote(land.at[pid], land.at[pid], send1.at[k], recv1.at[k], dev).wait_recv()
        total = land[0]
        for j in range(1, 8):
            total = total + land[j]
        acc[...] = total
        o_ref[me] = total
        out2 = [_remote(acc, o_ref.at[me], send2.at[k], recv2.at[k], dev) for k, dev, pid in peers]
        for cp in out2:
            cp.start()
        for k, dev, pid in peers:
            _remote(o_ref.at[pid], o_ref.at[pid], send2.at[k], recv2.at[k], dev).wait_recv()
        for cp in out1 + out2:
            cp.wait_send()

    return pl.pallas_call(
        body, name=name, in_specs=[_vm()], out_specs=_vm(),
        out_shape=jax.ShapeDtypeStruct(v.shape, F32),
        scratch_shapes=[pltpu.VMEM(v.shape, F32), pltpu.VMEM((R, _LANES), F32)]
        + [pltpu.SemaphoreType.DMA((8,)) for _ in range(4)],
        compiler_params=pltpu.CompilerParams(vmem_limit_bytes=_VMEM_LIMIT),
    )(v)


def _all_peers():
    x, y, c = lax.axis_index("x"), lax.axis_index("y"), lax.axis_index("c")
    peers = []
    for k in range(1, 8):
        px, py, pc = (1 - x if k & 4 else x), (1 - y if k & 2 else y), (1 - c if k & 1 else c)
        peers.append((k, (px, py, pc), 4 * px + 2 * py + pc))
    return 4 * x + 2 * y + c, peers


def _exchange_slices(name, v):
    def body(v_ref, land, send, recv):
        me, peers = _all_peers()
        _handshake([dev for _, dev, _ in peers])
        cps = [_remote(v_ref.at[pid], land.at[me], send.at[k], recv.at[k], dev) for k, dev, pid in peers]
        for cp in cps:
            cp.start()
        for k, dev, pid in peers:
            _remote(land.at[pid], land.at[pid], send.at[k], recv.at[k], dev).wait_recv()
        for cp in cps:
            cp.wait_send()

    return _sequencer(name, _ID_ALL, body, [v], [jax.ShapeDtypeStruct(v.shape, v.dtype)], [8, 8])[0]


def _sum_slices(name, v, landed, after):
    _, R, _ = v.shape

    def body(v_ref, land_ref, after_ref, o_ref):
        me, peers = _all_peers()
        acc = v_ref[me]
        for _, _, pid in peers:
            acc = acc + land_ref[pid]
        o_ref[...] = acc

    return pl.pallas_call(
        body, name=name, in_specs=[_vm(), _vm(), _any()], out_specs=_vm(),
        out_shape=jax.ShapeDtypeStruct((R, _LANES), F32),
        compiler_params=pltpu.CompilerParams(vmem_limit_bytes=_VMEM_LIMIT),
    )(v, landed, after)


def _broadcast_slices(name, s):
    def body(s_ref, out, send, recv):
        me, peers = _all_peers()
        _handshake([dev for _, dev, _ in peers])
        cps = [_remote(s_ref, out.at[me], send.at[k], recv.at[k], dev) for k, dev, pid in peers]
        for cp in cps:
            cp.start()
        for k, dev, pid in peers:
            _remote(out.at[pid], out.at[pid], send.at[k], recv.at[k], dev).wait_recv()
        for cp in cps:
            cp.wait_send()

    return _sequencer(name, _ID_ALL, body, [s], [jax.ShapeDtypeStruct((8,) + s.shape, s.dtype)], [8, 8])[0]


_WEIGHT_NAMES = ['norm_mix', 'norm_ffn', 'norm_ple', 'norm_final', 'gm_w_in', 'gm_ln_g', 'gm_ln_b', 'gm_w_s',
                 'gm_b_s', 'gm_w_out', 's5_w_in', 's5_a_re', 's5_a_im', 's5_log_dt', 's5_b_re', 's5_b_im',
                 's5_c_re', 's5_c_im', 's5_d', 's5_w_out', 'ffn_w1', 'ffn_w3', 'ffn_w2', 'ple_w_gate', 'ple_w_proj']
_BIG = {'gm_w_in': 'col', 'gm_w_out': 'row', 's5_w_in': 'row', 's5_w_out': 'col', 'ffn_w1': 'col',
        'ffn_w3': 'col', 'ffn_w2': 'row', 'ple_w_gate': 'row', 'ple_w_proj': 'col'}


_VIEW = {'s5_a_re': (0, 2, 1), 's5_a_im': (0, 2, 1), 's5_b_re': (0, 2, 3, 1), 's5_b_im': (0, 2, 3, 1),
         's5_c_re': (0, 2, 3, 1), 's5_c_im': (0, 2, 3, 1)}


def _to_view(name, a):
    return jnp.transpose(a, _VIEW[name]) if name in _VIEW else a


def _from_view(name, a):
    if name not in _VIEW:
        return a
    perm = _VIEW[name]
    return jnp.transpose(a, [perm.index(i) for i in range(len(perm))])


def _rc(kind, shard_shape):
    rows, cols = shard_shape[-2:]
    return (rows // 2, cols) if kind == "col" else (rows, cols // 2)


def _pack(vecs, rows_multiple):
    flat = jnp.concatenate([a.reshape(-1).astype(F32) for a in vecs])
    unit = rows_multiple * _LANES
    pad = (-flat.shape[0]) % unit
    return jnp.pad(flat, (0, pad)).reshape(-1, _LANES)


def _unpack(buf, shapes):
    flat = buf.reshape(-1)
    out, off = [], 0
    for s in shapes:
        n = math.prod(s)
        out.append(flat[off:off + n].reshape(s))
        off += n
    return out


def _ident(accs, ex):
    return accs


def _add_resid(accs, ex):
    return [accs[0] + ex[0]]


def _swiglu_epi(accs, ex):
    a, b = accs
    return [a, b, a * _sig(a) * b]


def _swiglu_bwd_epi(accs, ex):
    df = accs[0]
    a, b = ex[0].astype(F32), ex[1].astype(F32)
    sa = _sig(a)
    return [df * b * (sa * (1.0 + a * (1.0 - sa))), df * (a * sa)]


def _ple_epi(accs, ex):
    gt = _sig(accs[0])
    return [ex[0] + gt * ex[1], gt]


def _glu_epi(accs, ex):
    val, sg = accs[0], _sig(accs[1])
    return [ex[0] + val * sg, val, sg]


def kernel(x, p, norm_mix, norm_ffn, norm_ple, norm_final, gm_w_in, gm_ln_g, gm_ln_b, gm_w_s, gm_b_s, gm_w_out, s5_w_in, s5_a_re, s5_a_im, s5_log_dt, s5_b_re, s5_b_im, s5_c_re, s5_c_im, s5_d, s5_w_out, ffn_w1, ffn_w3, ffn_w2, ple_w_gate, ple_w_proj, loss_target, m_norm_mix, m_norm_ffn, m_norm_ple, m_norm_final, m_gm_w_in, m_gm_ln_g, m_gm_ln_b, m_gm_w_s, m_gm_b_s, m_gm_w_out, m_s5_w_in, m_s5_a_re, m_s5_a_im, m_s5_log_dt, m_s5_b_re, m_s5_b_im, m_s5_c_re, m_s5_c_im, m_s5_d, m_s5_w_out, m_ffn_w1, m_ffn_w3, m_ffn_w2, m_ple_w_gate, m_ple_w_proj, v_norm_mix, v_norm_ffn, v_norm_ple, v_norm_final, v_gm_w_in, v_gm_ln_g, v_gm_ln_b, v_gm_w_s, v_gm_b_s, v_gm_w_out, v_s5_w_in, v_s5_a_re, v_s5_a_im, v_s5_log_dt, v_s5_b_re, v_s5_b_im, v_s5_c_re, v_s5_c_im, v_s5_d, v_s5_w_out, v_ffn_w1, v_ffn_w3, v_ffn_w2, v_ple_w_gate, v_ple_w_proj):
    env = dict(locals())
    w = {n: env[n] for n in _WEIGHT_NAMES}
    mom = {n: env["m_" + n] for n in _WEIGHT_NAMES}
    var = {n: env["v_" + n] for n in _WEIGHT_NAMES}
    xs, tgt = x[0], loss_target[0]
    L, D = xs.shape
    depth = norm_mix.shape[0]
    qx, qy = lax.axis_index("x"), lax.axis_index("y")
    q = 2 * qx + qy

    def gather(tag, items):
        shards = []
        for name, layer in items:
            kind = _BIG[name]
            R, C = _rc(kind, w[name].shape)
            shards.append(_cast_shard(f"cast_{name}{layer}", w[name], layer, kind, R, C))
        full = _allgather(f"ag_{tag}", shards)
        return {it: _W(f, _BIG[it[0]]) for it, f in zip(items, full)}

    W = {}
    mixers = [["gm_w_in", "gm_w_out"], ["s5_w_in", "s5_w_out"]]
    for i in range(depth):
        for n in mixers[i]:
            W.update(gather(f"{n}", [(n, 0)]))
        W.update(gather(f"ffn_up{i}", [("ffn_w1", i), ("ffn_w3", i)]))
        W.update(gather(f"ffn_down{i}", [("ffn_w2", i)]))
        W.update(gather(f"ple{i}", [("ple_w_gate", i), ("ple_w_proj", i)]))

    d_slots = jnp.zeros((4, D // 4), F32)
    d_slots = lax.dynamic_update_slice(d_slots, s5_d.astype(F32), (q, 0))
    d_sum = _allreduce_small("ar_s5_d", _pack([d_slots], 64).reshape(8, -1, _LANES))
    d_full = (d_sum.reshape(-1)[:D] * 0.5).reshape(1, D)

    def ffn_fwd(i, xin):
        hf = _rms_fwd(f"rms_ffn{i}", xin, norm_ffn[i:i + 1])
        a, b, f = _mm_nn(f"ffn_up{i}", hf, [W["ffn_w1", i], W["ffn_w3", i]], 1024, 1408, ffn_w2.shape[1] * 4,
                         [], [_MXU, _MXU, _MXU], _swiglu_epi, tm=1024)
        xo = _mm_nn(f"ffn_down{i}", f, [W["ffn_w2", i]], 1408, 1024, D, [xin], [F32], _add_resid, tm=1024)[0]
        return xo, (xin, hf, a, b, f)

    def ple_fwd(i, xin):
        hp = _rms_fwd(f"rms_ple{i}", xin, norm_ple[i:i + 1])
        pi = lax.optimization_barrier((p[i, 0], hp))[0]
        pp = _mm_nn(f"ple_proj{i}", pi, [W["ple_w_proj", i]], 128, 512, D, [], [F32], _ident, tm=2048)[0]
        xo, gt = _mm_nn(f"ple_gate{i}", hp, [W["ple_w_gate", i]], 512, 1024, D, [xin, pp], [F32, _MXU], _ple_epi,
                        tm=1024)
        return xo, (xin, hp, pi, pp, gt)

    h0 = _rms_fwd("rms_mix0", xs, norm_mix[0:1])
    z = _mm_nn("gm_in", h0, [W["gm_w_in", 0]], 1024, 1024, 2 * D, [], [F32], _ident, tm=1024)[0]
    bsT = gm_b_s[0].T
    gm_m = _gmlp_fwd(z, gm_ln_g, gm_ln_b, gm_w_s[0], bsT)
    x1 = _mm_nn("gm_out", gm_m, [W["gm_w_out", 0]], 512, 1024, D, [xs], [F32], _add_resid, tm=1024)[0]
    x2, ffn0 = ffn_fwd(0, x1)
    x3, ple0 = ple_fwd(0, x2)

    T = D // _LANES
    lanes = _S5_GT * _S5_P
    sv = {n: _to_view(n, w[n])[0] for n in _VIEW}
    a_re, a_im, log_dt = sv["s5_a_re"], sv["s5_a_im"], s5_log_dt
    lbr, lbi, Bbar_re, Bbar_im = _s5_prep(a_re, a_im, log_dt, sv["s5_b_re"], sv["s5_b_im"])

    def to_bd(B):
        return jnp.transpose(B.reshape(_S5_P, _S5_C, T, _S5_GT), (2, 3, 1, 0))

    def to_cd(cw):
        return jnp.transpose(cw.reshape(_S5_C, _S5_P, T, _S5_GT), (2, 3, 1, 0))

    def to_lam(v):
        return jnp.transpose(v).reshape(T, 1, lanes)

    bd_re, bd_im = to_bd(Bbar_re), to_bd(Bbar_im)
    cd_re, cd_im = to_cd(sv["s5_c_re"]), to_cd(sv["s5_c_im"])
    lam_re, lam_im = to_lam(lbr), to_lam(lbi)

    h1 = _rms_fwd("rms_mix1", x3, norm_mix[1:2])
    u = _mm_nn("s5_in", h1, [W["s5_w_in", 0]], 512, 1024, D, [], [F32], _ident, tm=1024)[0]
    s5_g, s5_re, s5_im = _s5_fwd(u, bd_re, bd_im, cd_re, cd_im, lam_re, lam_im, d_full)
    x4, glu_val, glu_sg = _mm_nn("s5_out", s5_g, [W["s5_w_out", 0], W["s5_w_out", 0]], 1024, 1024, D, [x3],
                                 [F32, _MXU, _MXU], _glu_epi, tm=1024, cb_offsets=[0, 2])
    x5, ffn1 = ffn_fwd(1, x4)
    x6, ple1 = ple_fwd(1, x5)

    dx, d_norm_final, loss_rows = _loss_head(x6, norm_final[None], tgt)

    small = {}

    seq = _Order()
    tie, done = seq.tie, seq.done
    d_norm_ple, d_norm_ffn, d_norm_mix = [None] * depth, [None] * depth, [None] * depth
    reduced = {}

    def keep(names, layer, pairs):
        for n, pr in zip(names, pairs):
            reduced[n, layer] = pr

    ple_names, up_names, down_names = ["ple_w_gate", "ple_w_proj"], ["ffn_w1", "ffn_w3"], ["ffn_w2"]

    def ple_bwd(i, dxo, saved):
        xin, hp, pi, pp, gt = saved
        dpre, dpp = done(_ple_bwd_elem(tie(dxo), pp, gt))
        dwg = done(_mm_tn(f"ple_gate_dw{i}", tie(hp), [dpre], "row", 512, 1024, 512, 1024))[0]
        dwp = done(_mm_tn(f"ple_proj_dw{i}", tie(pi), [dpp], "col", 128, 512, 128, 512))[0]
        red = _Reduction(f"ple{i}", [dwg, dwp])
        dhp = done(_mm_nt(f"ple_gate_dx{i}", [tie(dpre)], [W["ple_w_gate", i]], 512, 1024, [], [F32], _ident,
                          tm=2048))[0]
        dxin, dxin_mxu, dg = done(_rms_bwd(f"rms_ple_bwd{i}", tie(dhp), xin, norm_ple[i:i + 1], dxo))
        return dxin, dxin_mxu, dg, red

    def ffn_bwd(i, dxo, dxo_mxu, saved, before_up):
        xin, hf, a, b, f = saved
        dw2 = done(_mm_tn(f"ffn_down_dw{i}", tie(f), [dxo_mxu], "row", 1408, 1024, 1408, 1024))[0]
        r_down = _Reduction(f"ffd{i}", [dw2])
        da, db = done(_mm_nt(f"ffn_down_dx{i}", [tie(dxo_mxu)], [W["ffn_w2", i]], 1408, 1024, [a, b], [_MXU, _MXU],
                             _swiglu_bwd_epi, tm=1024))
        for step in before_up:
            step()
        r_down.scatter(seq)
        dw1, dw3 = done(_mm_tn(f"ffn_up_dw{i}", tie(hf), [da, db], "col", 1024, 1408, 1024, 1408))
        r_up = _Reduction(f"ffu{i}", [dw1, dw3])
        dhf = done(_mm_nt(f"ffn_up_dx{i}", [tie(da), db], [W["ffn_w1", i], W["ffn_w3", i]], 1024, 1408, [], [F32],
                          _ident, tm=1024))[0]
        dxin, dxin_mxu, dg = done(_rms_bwd(f"rms_ffn_bwd{i}", tie(dhf), xin, norm_ffn[i:i + 1], dxo))
        r_up.scatter(seq)
        return dxin, dxin_mxu, dg, r_down, r_up

    dx, dx_mxu, d_norm_ple[1], r_ple1 = ple_bwd(1, dx, ple1)
    dx, _, d_norm_ffn[1], r_down1, r_up1 = ffn_bwd(1, dx, dx_mxu, ffn1, [lambda: r_ple1.scatter(seq)])

    do = done([_glu_bwd_elem(tie(dx), glu_val, glu_sg)])[0]
    dw_s5out = done(_mm_tn("s5_out_dw", tie(s5_g), [do], "col", 1024, 1024, 1024, 1024))[0]
    r_s5out = _Reduction("s5out", [dw_s5out])
    dgy = done(_mm_nt("s5_out_dx", [tie(do)], [W["s5_w_out", 0]], 1024, 1024, [], [F32], _ident, tm=1024))[0]
    keep(ple_names, 1, r_ple1.finish(seq))
    keep(down_names, 1, r_down1.finish(seq))
    du, dbd_re, dbd_im, dcd_re, dcd_im, dl_re, dl_im, dd = done(_s5_bwd(
        tie(u), dgy, s5_re, s5_im, bd_re, bd_im, cd_re, cd_im, lam_re, lam_im, d_full))
    r_s5out.scatter(seq)
    dw_s5in = done(_mm_tn("s5_in_dw", tie(h1), [du], "row", 512, 1024, 512, 1024))[0]
    r_s5in = _Reduction("s5in", [dw_s5in])
    dh1 = done(_mm_nt("s5_in_dx", [tie(du)], [W["s5_w_in", 0]], 512, 1024, [], [F32], _ident, tm=2048))[0]
    dx, _, d_norm_mix[1] = done(_rms_bwd("rms_mix1_bwd", tie(dh1), x3, norm_mix[1:2], dx))
    keep(up_names, 1, r_up1.finish(seq))
    r_s5in.scatter(seq)

    def from_bd(t):
        return jnp.transpose(t, (3, 2, 0, 1)).reshape(_S5_P, _S5_C, T * _S5_GT)

    def from_cdT(t):
        return jnp.transpose(t, (2, 3, 0, 1)).reshape(_S5_C, _S5_P, T * _S5_GT)

    def from_lam(t):
        return jnp.transpose(t.reshape(T * _S5_GT, _S5_P))

    da_re, da_im, dlog_dt, db_re, db_im = _s5_prep_bwd(
        a_re, a_im, log_dt, sv["s5_b_re"], sv["s5_b_im"], from_lam(dl_re), from_lam(dl_im),
        from_bd(dbd_re), from_bd(dbd_im))
    small["s5_a_re"], small["s5_a_im"], small["s5_log_dt"] = da_re[None], da_im[None], dlog_dt
    small["s5_b_re"], small["s5_b_im"] = db_re[None], db_im[None]
    small["s5_c_re"], small["s5_c_im"] = from_cdT(dcd_re)[None], from_cdT(dcd_im)[None]

    dx, dx_mxu, d_norm_ple[0], r_ple0 = ple_bwd(0, dx, ple0)
    keep(["s5_w_out"], 0, r_s5out.finish(seq))
    xin0, hf0, a0, b0, f0 = ffn0
    da0, db0 = done(_mm_nt("ffn_down_dx0", [tie(dx_mxu)], [W["ffn_w2", 0]], 1408, 1024, [a0, b0], [_MXU, _MXU],
                           _swiglu_bwd_epi, tm=1024))
    r_ple0.scatter(seq)
    dw1, dw3 = done(_mm_tn("ffn_up_dw0", tie(hf0), [da0, db0], "col", 1024, 1408, 1024, 1408))
    r_up0 = _Reduction("ffu0", [dw1, dw3])
    keep(["s5_w_in"], 0, r_s5in.finish(seq))
    dw2 = done(_mm_tn("ffn_down_dw0", tie(f0), [dx_mxu], "row", 1408, 1024, 1408, 1024))[0]
    r_down0 = _Reduction("ffd0", [dw2])
    r_up0.scatter(seq)
    dhf0 = done(_mm_nt("ffn_up_dx0", [tie(da0), db0], [W["ffn_w1", 0], W["ffn_w3", 0]], 1024, 1408, [], [F32], _ident,
                       tm=1024))[0]
    dx, dx_mxu, d_norm_ffn[0] = done(_rms_bwd("rms_ffn_bwd0", tie(dhf0), xin0, norm_ffn[0:1], dx))
    keep(ple_names, 0, r_ple0.finish(seq))
    r_down0.scatter(seq)

    dw_gmout = done(_mm_tn("gm_out_dw", tie(gm_m), [dx_mxu], "row", 512, 1024, 512, 1024))[0]
    r_gmout = _Reduction("gmout", [dw_gmout])
    dgm = done(_mm_nt("gm_out_dx", [tie(dx_mxu)], [W["gm_w_out", 0]], 512, 1024, [], [F32], _ident, tm=2048))[0]
    dz, dws, dbsT, dlng, dlnb = done(_gmlp_bwd(tie(z), dgm, gm_ln_g, gm_ln_b, gm_w_s[0], bsT))
    dw_gmin = done(_mm_tn("gm_in_dw", tie(h0), [dz], "col", 1024, 1024, 1024, 1024))[0]
    r_gmin = _Reduction("gmin", [dw_gmin])
    dh0 = done(_mm_nt("gm_in_dx", [tie(dz)], [W["gm_w_in", 0]], 1024, 1024, [], [F32], _ident, tm=1024))[0]
    dx, _, d_norm_mix[0] = done(_rms_bwd("rms_mix0_bwd", tie(dh0), xs, norm_mix[0:1], dx))
    grad_x = dx[None]

    small["norm_mix"], small["norm_ffn"] = jnp.concatenate(d_norm_mix), jnp.concatenate(d_norm_ffn)
    small["norm_ple"], small["norm_final"] = jnp.concatenate(d_norm_ple), d_norm_final[0]
    small["gm_ln_g"], small["gm_ln_b"], small["gm_w_s"] = dlng, dlnb, dws[None]
    small["gm_b_s"] = dbsT[:, :_GM_HEADS].T[None]
    small["s5_d"] = dd

    small_names = [n for n in _WEIGHT_NAMES if n not in _BIG]
    packed = _pack([small[n] for n in small_names] + [loss_rows[:, :1]], 64).reshape(8, -1, _LANES)
    grads, deltas, new_m, new_v = {}, {}, {}, {}
    my_c = lax.axis_index("c")

    def adamw(n, layer, prev):
        kind = _BIG[n]
        R, C = _rc(kind, w[n].shape)
        return done(_adamw_big(f"adamw_{n}{layer}", w[n], mom[n], var[n], layer, reduced[n, layer], kind, R, C,
                               seq.core, seq.tok, prev))

    def adamw_last(names):
        for n in names:
            grads[n], deltas[n], new_m[n], new_v[n] = adamw(n, 0, late.get(n))

    late = {}
    for n in down_names + up_names + ple_names:
        late[n] = adamw(n, 1, None)
    keep(up_names, 0, r_up0.finish(seq))
    r_gmout.scatter(seq)
    r_gmin.scatter(seq)
    landed = _exchange_slices("ar_small_in", packed)
    adamw_last(["s5_w_in", "s5_w_out"] + ple_names)
    keep(down_names, 0, r_down0.finish(seq))
    adamw_last(up_names)
    keep(["gm_w_out"], 0, r_gmout.finish(seq))
    mine = done([_sum_slices("ar_small_sum", packed, landed, seq.tok)])[0]
    spread = _broadcast_slices("ar_small_out", mine)
    keep(["gm_w_in"], 0, r_gmin.finish(seq))
    adamw_last(down_names + ["gm_w_out", "gm_w_in"])
    summed = lax.dynamic_update_slice(spread, mine[None], (4 * qx + 2 * qy + my_c, 0, 0))
    *red_list, loss_sum = _unpack(summed, [small[n].shape for n in small_names] + [(1, 1)])
    red_small, loss = dict(zip(small_names, red_list)), loss_sum.reshape(())
    red_small["s5_d"] = lax.dynamic_slice(red_small["s5_d"], (0, q * (D // 4)), (1, D // 4))

    def two_d(a):
        return a.reshape((1,) * (2 - a.ndim) + a.shape)

    def views(src):
        return [two_d(_to_view(n, src[n])) for n in small_names]

    g_views = [two_d(red_small[n]) for n in small_names]
    dl, mo, vo = _adamw_small(views(w), g_views, views(mom), views(var))
    for n, g_, d_, m_, v_ in zip(small_names, g_views, dl, mo, vo):
        grads[n], deltas[n], new_m[n], new_v[n] = [_from_view(n, t_.reshape(_to_view(n, w[n]).shape)).reshape(w[n].shape)
                                                  for t_ in (g_, d_, m_, v_)]

    return (loss, grad_x, *[grads[n] for n in _WEIGHT_NAMES], *[deltas[n] for n in _WEIGHT_NAMES],
            *[new_m[n] for n in _WEIGHT_NAMES], *[new_v[n] for n in _WEIGHT_NAMES])
```

```python
import math

import jax
import jax.numpy as jnp
from jax import lax
from jax.experimental import pallas as pl
from jax.experimental.pallas import tpu as pltpu
from jax.experimental.pallas import tpu_sc as plsc

F32 = jnp.float32
_MXU = jnp.bfloat16
_WIRE = jnp.bfloat16
_EPS = 1e-6
_VMEM_LIMIT = 56 * 1024 * 1024
_LANES = 128
_MESH = pl.DeviceIdType.MESH

_LR, _B1, _B2, _AEPS, _WD, _STEP = 0.001, 0.9, 0.999, 1e-08, 0.01, 10

_GM_CHUNK = 128
_GM_HEADS = 16
_S5_GT = 8
_S5_P = 64
_S5_C = 16

_NN = (((1,), (0,)), ((), ()))
_NT = (((1,), (1,)), ((), ()))
_TN = (((0,), (0,)), ((), ()))


def _cparams(sem):
    return pltpu.CompilerParams(dimension_semantics=sem, vmem_limit_bytes=_VMEM_LIMIT)


def _sig(x):
    return 0.5 * jnp.tanh(0.5 * x) + 0.5


_GC = math.sqrt(2.0 / math.pi)


def _gelu(x):
    return 0.5 * x * (1.0 + jnp.tanh(_GC * (x + 0.044715 * (x * x * x))))


def _gelu_grad(x):
    t = jnp.tanh(_GC * (x + 0.044715 * (x * x * x)))
    return 0.5 * (1.0 + t) + 0.5 * x * (1.0 - t * t) * (_GC * (1.0 + 3.0 * 0.044715 * x * x))


def _dot(a, b, dn):
    return lax.dot_general(a.astype(_MXU), b.astype(_MXU), dn, preferred_element_type=F32)


class _W:
    def __init__(self, arr, kind):
        self.a, self.kind = arr, kind
        self.R, self.C = arr.shape[2], arr.shape[3]

    def full_shape(self):
        return (2 * self.R, 4 * self.C) if self.kind == "col" else (4 * self.R, 2 * self.C)


def _part_index(kind, R, C, tr, tc, rb, cb):
    nr, nc = R // tr, C // tc
    if kind == "col":
        return cb // nc, rb // nr, rb % nr, cb % nc
    return rb // nr, cb // nc, rb % nr, cb % nc


def _wspec(w, tr, tc, rb_fn, cb_fn):
    assert w.R % tr == 0 and w.C % tc == 0, (w.R, w.C, tr, tc)

    def imap(i, j, k):
        return _part_index(w.kind, w.R, w.C, tr, tc, rb_fn(i, j, k), cb_fn(i, j, k))

    return pl.BlockSpec((None, None, tr, tc), imap)


def _gspec(kind, R, C, tr, tc):
    assert R % tr == 0 and C % tc == 0, (R, C, tr, tc)

    def imap(i, j, k):
        part, half, rbi, cbi = _part_index(kind, R, C, tr, tc, i, j)
        return half, part, rbi, cbi

    return pl.BlockSpec((None, None, tr, tc), imap)


def _mm(name, grid, a_ops, b_ops, pairs, acc_shape, n_acc, extras, outs, epilogue):
    nk = grid[2]
    na, nb, ne, no = len(a_ops), len(b_ops), len(extras), len(outs)

    def body(*refs):
        a_refs = refs[:na]
        b_refs = refs[na:na + nb]
        e_refs = refs[na + nb:na + nb + ne]
        o_refs = refs[na + nb + ne:na + nb + ne + no]
        acc_refs = refs[na + nb + ne + no:]
        k = pl.program_id(2)

        def products():
            sums = [None] * n_acc
            for ai, bi, ci, dn in pairs:
                d = _dot(a_refs[ai][...], b_refs[bi][...], dn)
                sums[ci] = d if sums[ci] is None else sums[ci] + d
            return sums

        def finish(accs):
            res = epilogue(accs, [e[...] for e in e_refs])
            for o, r in zip(o_refs, res):
                o[...] = r.astype(o.dtype)

        if nk == 1:
            finish(products())
            return

        @pl.when(k == 0)
        def _():
            for acc, d in zip(acc_refs, products()):
                acc[...] = d

        @pl.when(jnp.logical_and(k > 0, k < nk - 1))
        def _():
            for acc, d in zip(acc_refs, products()):
                acc[...] += d

        @pl.when(k == nk - 1)
        def _():
            finish([acc[...] + d for acc, d in zip(acc_refs, products())])

    ops = list(a_ops) + list(b_ops) + list(extras)
    return pl.pallas_call(
        body, name=name, grid=grid,
        in_specs=[s for _, s in ops],
        out_specs=[s for _, s in outs],
        out_shape=[s for s, _ in outs],
        scratch_shapes=[pltpu.VMEM(acc_shape, F32) for _ in range(n_acc if nk > 1 else 0)],
        compiler_params=_cparams(("parallel", "parallel", "arbitrary")),
    )(*[a for a, _ in ops])


def _bs(shape, fn):
    return pl.BlockSpec(shape, fn)


def _tile_m(L):
    return min(L, 512)


def _mm_nn(name, x, ws, tk, tn, n_out, extras, outs_sd, epilogue, tm=None, cb_offsets=None):
    M, K = x.shape
    tm = min(M, tm or _tile_m(M))
    grid = (M // tm, n_out // tn, K // tk)
    a_ops = [(x, _bs((tm, tk), lambda i, j, k: (i, k)))]
    cb_offsets = cb_offsets or [0] * len(ws)
    b_ops = [(w.a, _wspec(w, tk, tn, lambda i, j, k: k, (lambda off: lambda i, j, k: j + off)(off)))
             for w, off in zip(ws, cb_offsets)]
    pairs = [(0, bi, bi, _NN) for bi in range(len(ws))]
    mn = _bs((tm, tn), lambda i, j, k: (i, j))
    ex = [(e, mn) for e in extras]
    outs = [(jax.ShapeDtypeStruct((M, n_out), dt), mn) for dt in outs_sd]
    return _mm(name, grid, a_ops, b_ops, pairs, (tm, tn), len(ws), ex, outs, epilogue)


def _mm_nt(name, xs, ws, tn, tk, extras, outs_sd, epilogue, tm=None):
    M, Nw = xs[0].shape
    Kw = ws[0].full_shape()[0]
    tm = min(M, tm or _tile_m(M))
    grid = (M // tm, Kw // tn, Nw // tk)
    a_ops = [(x, _bs((tm, tk), lambda i, j, k: (i, k))) for x in xs]
    b_ops = [(w.a, _wspec(w, tn, tk, lambda i, j, k: j, lambda i, j, k: k)) for w in ws]
    pairs = [(i, i, 0, _NT) for i in range(len(ws))]
    mn = _bs((tm, tn), lambda i, j, k: (i, j))
    ex = [(e, mn) for e in extras]
    outs = [(jax.ShapeDtypeStruct((M, Kw), dt), mn) for dt in outs_sd]
    return _mm(name, grid, a_ops, b_ops, pairs, (tm, tn), 1, ex, outs, epilogue)


def _mm_tn(name, x, dys, kind, R, C, tm, tn, tk=None):
    L, Kw = x.shape
    Nw = dys[0].shape[1]
    tk = tk or min(L, 1024)
    grid = (Kw // tm, Nw // tn, L // tk)
    a_ops = [(x, _bs((tk, tm), lambda i, j, k: (k, i)))]
    b_ops = [(dy, _bs((tk, tn), lambda i, j, k: (k, j))) for dy in dys]
    pairs = [(0, bi, bi, _TN) for bi in range(len(dys))]
    gs = _gspec(kind, R, C, tm, tn)
    outs = [(jax.ShapeDtypeStruct((2, 4, R, C), _WIRE), gs) for _ in dys]
    return _mm(name, grid, a_ops, b_ops, pairs, (tm, tn), len(dys), [], outs, lambda accs, ex: accs)


def _row_tile(L):
    return min(L, 256)


def _rowwise(name, body, ins, outs, L, acc_outs=()):
    tr = _row_tile(L)
    n_in, n_out = len(ins), len(outs)

    def kbody(*refs):
        i_refs, o_refs, a_refs = refs[:n_in], refs[n_in:n_in + n_out], refs[n_in + n_out:]
        res, sums = body(*[r[...] for r in i_refs])
        for o, r in zip(o_refs, res):
            o[...] = r.astype(o.dtype)
        if a_refs:
            @pl.when(pl.program_id(0) == 0)
            def _():
                for a in a_refs:
                    a[...] = jnp.zeros(a.shape, F32)
            for a, s in zip(a_refs, sums):
                a[...] += s

    in_specs = []
    for arr, kind in ins:
        if kind == "row":
            in_specs.append(pl.BlockSpec((tr, arr.shape[1]), lambda i: (i, 0)))
        else:
            in_specs.append(pl.BlockSpec(arr.shape, lambda i: (0, 0)))
    out_specs = [pl.BlockSpec((tr, c), lambda i: (i, 0)) for c, _ in outs]
    out_shape = [jax.ShapeDtypeStruct((L, c), dt) for c, dt in outs]
    out_specs += [pl.BlockSpec((1, c), lambda i: (0, 0)) for c in acc_outs]
    out_shape += [jax.ShapeDtypeStruct((1, c), F32) for c in acc_outs]
    return pl.pallas_call(
        kbody, name=name, grid=(L // tr,), in_specs=in_specs, out_specs=out_specs, out_shape=out_shape,
        compiler_params=_cparams(("arbitrary",)),
    )(*[a for a, _ in ins])


def _rms_fwd(name, x, g):
    def body(xv, gv):
        r = lax.rsqrt(jnp.mean(xv * xv, axis=-1, keepdims=True) + _EPS)
        return [xv * r * gv], []
    return _rowwise(name, body, [(x, "row"), (g, "vec")], [(x.shape[1], _MXU)], x.shape[0])[0]


def _rms_bwd(name, dh, x, g, dres):
    def body(dhv, xv, gv, dr):
        r = lax.rsqrt(jnp.mean(xv * xv, axis=-1, keepdims=True) + _EPS)
        xh = xv * r
        dxh = dhv * gv
        dx = dr + r * (dxh - xh * jnp.mean(dxh * xh, axis=-1, keepdims=True))
        return [dx, dx], [jnp.sum(dhv * xh, axis=0, keepdims=True)]
    D = x.shape[1]
    return _rowwise(name, body, [(dh, "row"), (x, "row"), (g, "vec"), (dres, "row")], [(D, F32), (D, _MXU)],
                    x.shape[0], [D])


def _loss_head(x, g, target):
    D = x.shape[1]

    def body(xv, gv, tv):
        r = lax.rsqrt(jnp.mean(xv * xv, axis=-1, keepdims=True) + _EPS)
        xh = xv * r
        e = xh * gv - tv
        dy = e * (1.0 / D)
        dxh = dy * gv
        dx = r * (dxh - xh * jnp.mean(dxh * xh, axis=-1, keepdims=True))
        row_loss = 0.5 * jnp.mean(e * e, axis=-1, keepdims=True)
        lsum = jnp.sum(row_loss, axis=0, keepdims=True) + jnp.zeros((1, _LANES), F32)
        return [dx], [jnp.sum(dy * xh, axis=0, keepdims=True), lsum]
    return _rowwise("loss_head", body, [(x, "row"), (g, "vec"), (target, "row")], [(D, F32)], x.shape[0], [D, _LANES])


def _ple_bwd_elem(dx, pp, gt):
    def body(dxv, ppv, gtv):
        gt32 = gtv.astype(F32)
        return [dxv * ppv * gt32 * (1.0 - gt32), dxv * gt32], []
    D = dx.shape[1]
    return _rowwise("ple_bwd_elem", body, [(dx, "row"), (pp, "row"), (gt, "row")], [(D, _MXU), (D, _MXU)], dx.shape[0])


def _glu_bwd_elem(dx, val, sg):
    def body(dxv, vv, sv):
        v32, s32 = vv.astype(F32), sv.astype(F32)
        return [jnp.concatenate([dxv * s32, dxv * v32 * s32 * (1.0 - s32)], axis=1)], []
    D = dx.shape[1]
    return _rowwise("glu_bwd_elem", body, [(dx, "row"), (val, "row"), (sg, "row")], [(2 * D, _MXU)], dx.shape[0])[0]


def _gm_common(z, ln_g, ln_b, wc_bf, bsT):
    W = z.shape[1] // 2
    zu, zv = z[:, :W], z[:, W:]
    u, v = _gelu(zu), _gelu(zv)
    mu = jnp.mean(v, axis=-1, keepdims=True)
    vc = v - mu
    rstd = lax.rsqrt(jnp.mean(vc * vc, axis=-1, keepdims=True) + _EPS)
    vh = vc * rstd
    vn = vh * ln_g + ln_b
    vnb = vn.astype(_MXU)
    svs = []
    for h in range(_GM_HEADS):
        sl = slice(h * _LANES, (h + 1) * _LANES)
        svs.append(_dot(wc_bf[h], vnb[:, sl], _NN) + bsT[:, h:h + 1])
    return zu, zv, u, vh, rstd, vnb, svs


def _causal(w):
    t = lax.broadcasted_iota(jnp.int32, w.shape, w.ndim - 2)
    s = lax.broadcasted_iota(jnp.int32, w.shape, w.ndim - 1)
    return jnp.where(s <= t, w, jnp.zeros_like(w))


def _gmlp_fwd(z, ln_g, ln_b, w_s, bsT):
    L, W2 = z.shape
    W = W2 // 2

    def body(z_ref, g_ref, b_ref, ws_ref, bs_ref, m_ref):
        wc = _causal(ws_ref[...]).astype(_MXU)
        _, _, u, _, _, _, svs = _gm_common(z_ref[...], g_ref[...], b_ref[...], wc, bs_ref[...])
        for h in range(_GM_HEADS):
            sl = slice(h * _LANES, (h + 1) * _LANES)
            m_ref[:, sl] = (u[:, sl] * svs[h]).astype(m_ref.dtype)

    return pl.pallas_call(
        body, name="gmlp_fwd", grid=(L // _GM_CHUNK,),
        in_specs=[pl.BlockSpec((_GM_CHUNK, W2), lambda n: (n, 0)),
                  pl.BlockSpec((1, W), lambda n: (0, 0)), pl.BlockSpec((1, W), lambda n: (0, 0)),
                  pl.BlockSpec(w_s.shape, lambda n: (0, 0, 0)), pl.BlockSpec(bsT.shape, lambda n: (0, 0))],
        out_specs=pl.BlockSpec((_GM_CHUNK, W), lambda n: (n, 0)),
        out_shape=jax.ShapeDtypeStruct((L, W), _MXU),
        compiler_params=_cparams(("arbitrary",)),
    )(z, ln_g, ln_b, w_s, bsT)


def _gmlp_bwd(z, dm, ln_g, ln_b, w_s, bsT):
    L, W2 = z.shape
    W = W2 // 2
    T = _GM_CHUNK

    def body(z_ref, dm_ref, g_ref, b_ref, ws_ref, bs_ref, dz_ref, dws_ref, dbs_ref, dg_ref, db_ref):
        @pl.when(pl.program_id(0) == 0)
        def _():
            dws_ref[...] = jnp.zeros(dws_ref.shape, F32)
            dbs_ref[...] = jnp.zeros(dbs_ref.shape, F32)
            dg_ref[...] = jnp.zeros(dg_ref.shape, F32)
            db_ref[...] = jnp.zeros(db_ref.shape, F32)

        wc = _causal(ws_ref[...]).astype(_MXU)
        ln_g_v = g_ref[...]
        zu, zv, u, vh, rstd, vnb, svs = _gm_common(z_ref[...], ln_g_v, b_ref[...], wc, bs_ref[...])
        dmv = dm_ref[...]
        lane = lax.broadcasted_iota(jnp.int32, (T, _LANES), 1)
        dbs = jnp.zeros((T, _LANES), F32)
        dvn_parts = []
        for h in range(_GM_HEADS):
            sl = slice(h * _LANES, (h + 1) * _LANES)
            dsv = dmv[:, sl] * u[:, sl]
            dz_ref[:, sl] = (dmv[:, sl] * svs[h] * _gelu_grad(zu[:, sl])).astype(dz_ref.dtype)
            dbs = dbs + jnp.where(lane == h, jnp.sum(dsv, axis=1, keepdims=True), 0.0)
            dsvb = dsv.astype(_MXU)
            dws_ref[h] += _dot(dsvb, vnb[:, sl], _NT)
            dvn_parts.append(_dot(wc[h], dsvb, _TN))
        dbs_ref[...] += dbs
        dvn = jnp.concatenate(dvn_parts, axis=1)
        dg_ref[...] += jnp.sum(dvn * vh, axis=0, keepdims=True)
        db_ref[...] += jnp.sum(dvn, axis=0, keepdims=True)
        dxh = dvn * ln_g_v
        dv = rstd * (dxh - jnp.mean(dxh, axis=-1, keepdims=True) - vh * jnp.mean(dxh * vh, axis=-1, keepdims=True))
        dz_ref[:, W:] = (dv * _gelu_grad(zv)).astype(dz_ref.dtype)

        @pl.when(pl.program_id(0) == pl.num_programs(0) - 1)
        def _():
            dws_ref[...] = _causal(dws_ref[...])

    return pl.pallas_call(
        body, name="gmlp_bwd", grid=(L // T,),
        in_specs=[pl.BlockSpec((T, W2), lambda n: (n, 0)), pl.BlockSpec((T, W), lambda n: (n, 0)),
                  pl.BlockSpec((1, W), lambda n: (0, 0)), pl.BlockSpec((1, W), lambda n: (0, 0)),
                  pl.BlockSpec(w_s.shape, lambda n: (0, 0, 0)), pl.BlockSpec(bsT.shape, lambda n: (0, 0))],
        out_specs=[pl.BlockSpec((T, W2), lambda n: (n, 0)),
                   pl.BlockSpec(w_s.shape, lambda n: (0, 0, 0)), pl.BlockSpec((T, _LANES), lambda n: (0, 0)),
                   pl.BlockSpec((1, W), lambda n: (0, 0)), pl.BlockSpec((1, W), lambda n: (0, 0))],
        out_shape=[jax.ShapeDtypeStruct((L, W2), _MXU), jax.ShapeDtypeStruct(w_s.shape, F32),
                   jax.ShapeDtypeStruct((T, _LANES), F32),
                   jax.ShapeDtypeStruct((1, W), F32), jax.ShapeDtypeStruct((1, W), F32)],
        compiler_params=_cparams(("arbitrary",)),
    )(z, dm, ln_g, ln_b, w_s, bsT)


def _s5_prep_math(a_re, a_im, log_dt):
    dt = jnp.exp(log_dt)
    xr, xi = a_re * dt, a_im * dt
    e = jnp.exp(xr)
    lbr, lbi = e * jnp.cos(xi), e * jnp.sin(xi)
    dn = a_re * a_re + a_im * a_im
    nr, ni = lbr - 1.0, lbi
    pr, pi = nr * a_re + ni * a_im, ni * a_re - nr * a_im
    return dt, lbr, lbi, dn, nr, ni, pr, pi


def _vm():
    return pl.BlockSpec(memory_space=pltpu.VMEM)


def _s5_prep(a_re, a_im, log_dt, b_re, b_im):
    def body(ar_ref, ai_ref, ld_ref, br_ref, bi_ref, lbr_ref, lbi_ref, Br_ref, Bi_ref):
        _, lbr, lbi, dn, _, _, pr, pi = _s5_prep_math(ar_ref[...], ai_ref[...], ld_ref[...])
        cr, ci = (pr / dn)[:, None, :], (pi / dn)[:, None, :]
        lbr_ref[...] = lbr
        lbi_ref[...] = lbi
        br, bi = br_ref[...], bi_ref[...]
        Br_ref[...] = cr * br - ci * bi
        Bi_ref[...] = cr * bi + ci * br

    sd = jax.ShapeDtypeStruct
    return pl.pallas_call(
        body, name="s5_prep", in_specs=[_vm()] * 5, out_specs=[_vm()] * 4,
        out_shape=[sd(a_re.shape, F32), sd(a_re.shape, F32), sd(b_re.shape, F32), sd(b_re.shape, F32)],
    )(a_re, a_im, log_dt, b_re, b_im)


def _s5_prep_bwd(a_re, a_im, log_dt, b_re, b_im, dlbr_s, dlbi_s, dBr, dBi):
    def body(ar_ref, ai_ref, ld_ref, br_ref, bi_ref, dlr_ref, dli_ref, dBr_ref, dBi_ref,
             dar_ref, dai_ref, dld_ref, dbr_ref, dbi_ref):
        a_re_v, a_im_v = ar_ref[...], ai_ref[...]
        dt, lbr, lbi, dn, nr, ni, pr, pi = _s5_prep_math(a_re_v, a_im_v, ld_ref[...])
        cr, ci = (pr / dn)[:, None, :], (pi / dn)[:, None, :]
        br, bi, dBr_v, dBi_v = br_ref[...], bi_ref[...], dBr_ref[...], dBi_ref[...]
        dbr_ref[...] = cr * dBr_v + ci * dBi_v
        dbi_ref[...] = cr * dBi_v - ci * dBr_v
        dcr = jnp.sum(br * dBr_v + bi * dBi_v, axis=1)
        dci = jnp.sum(br * dBi_v - bi * dBr_v, axis=1)
        dpr, dpi = dcr / dn, dci / dn
        ddn = -(dcr * pr + dci * pi) / (dn * dn)
        dnr = dpr * a_re_v - dpi * a_im_v
        dni = dpr * a_im_v + dpi * a_re_v
        dlbr = dlr_ref[...] + dnr
        dlbi = dli_ref[...] + dni
        dxr = dlbr * lbr + dlbi * lbi
        dxi = dlbi * lbr - dlbr * lbi
        dar_ref[...] = dpr * nr + dpi * ni + 2.0 * ddn * a_re_v + dxr * dt
        dai_ref[...] = dpr * ni - dpi * nr + 2.0 * ddn * a_im_v + dxi * dt
        dld_ref[...] = jnp.sum(dxr * a_re_v + dxi * a_im_v, axis=0, keepdims=True) * dt

    sd = jax.ShapeDtypeStruct
    return pl.pallas_call(
        body, name="s5_prep_bwd", in_specs=[_vm()] * 9, out_specs=[_vm()] * 5,
        out_shape=[sd(a_re.shape, F32), sd(a_re.shape, F32), sd(log_dt.shape, F32),
                   sd(b_re.shape, F32), sd(b_re.shape, F32)],
    )(a_re, a_im, log_dt, b_re, b_im, dlbr_s, dlbi_s, dBr, dBi)


def _shift_rows(v, down):
    n = v.shape[0]
    rolled = pltpu.roll(v, 1 if down else n - 1, 0)
    row = lax.broadcasted_iota(jnp.int32, v.shape, 0)
    return jnp.where(row == (0 if down else n - 1), 0.0, rolled)


def _cmul(ar, ai, br, bi):
    return ar * br - ai * bi, ar * bi + ai * br


_SEG = 8
_UNROLL = 8


def _seg_rows(k):
    if isinstance(k, int):
        return pl.ds(k * _SEG, _SEG)
    return pl.ds(pl.multiple_of(k * _SEG, _SEG), _SEG)


def _unrolled(n, step, init):
    main = n // _UNROLL

    def trip(kk, s):
        for uu in range(_UNROLL):
            s = step(kk * _UNROLL + uu, s)
        return s

    s = lax.fori_loop(0, main, trip, init)
    for r in range(main * _UNROLL, n):
        s = step(r, s)
    return s


def _interleave(src_ref, dst_ref, nk):
    def step(k, carry):
        dst_ref[_seg_rows(k), :] = src_ref[pl.ds(k, _SEG, stride=nk), :]
        return carry
    _unrolled(nk, step, 0)


def _deinterleave(src_ref, dst_ref, nk):
    def step(k, carry):
        dst_ref[pl.ds(k, _SEG, stride=nk), :] = src_ref[_seg_rows(k), :]
        return carry
    _unrolled(nk, step, 0)


def _segment_inits(er, ei, ar, ai, nk, down):
    pr, pi = ar, ai
    for _ in range(int(math.log2(nk))):
        pr, pi = _cmul(pr, pi, pr, pi)
    fr, fi = er, ei
    for _ in range(_SEG - 1):
        sr, si = _shift_rows(fr, down), _shift_rows(fi, down)
        mr, mi = _cmul(pr, pi, sr, si)
        fr, fi = er + mr, ei + mi
    return _shift_rows(fr, down), _shift_rows(fi, down)


def _scan_states(x_re, x_im, ar, ai, nk):
    lanes = ar.shape[1]

    def step(k, s):
        rows = _seg_rows(k)
        mr, mi = _cmul(ar, ai, s[0], s[1])
        return mr + x_re[rows, :], mi + x_im[rows, :]

    zero = jnp.zeros((_SEG, lanes), F32)
    er, ei = _unrolled(nk, step, (zero, zero))
    ir, ii = _segment_inits(er, ei, ar, ai, nk, True)

    def step2(k, s):
        rows = _seg_rows(k)
        mr, mi = _cmul(ar, ai, s[0], s[1])
        nr, ni = mr + x_re[rows, :], mi + x_im[rows, :]
        x_re[rows, :] = nr
        x_im[rows, :] = ni
        return nr, ni

    _unrolled(nk, step2, (ir, ii))


def _s5_tile_fwd(u, bd_re, bd_im, cd_re, cd_im, ar, ai, d, s_re, s_im, nk):
    s_re[...] = _dot(u, bd_re, _NN)
    s_im[...] = _dot(u, bd_im, _NN)
    _scan_states(s_re, s_im, ar, ai, nk)
    return _dot(s_re[...], cd_re, _NN) - _dot(s_im[...], cd_im, _NN) + d * u


def _s5_specs(L, T):
    lanes = _S5_GT * _S5_P
    u_spec = pl.BlockSpec((L, _LANES), lambda t: (0, t))
    bd_spec = pl.BlockSpec((None, _S5_GT, _S5_C, _S5_P), lambda t: (t, 0, 0, 0))
    cd_spec = pl.BlockSpec((None, _S5_GT, _S5_P, _S5_C), lambda t: (t, 0, 0, 0))
    lam_spec = pl.BlockSpec((None, 1, lanes), lambda t: (t, 0, 0))
    d_spec = pl.BlockSpec((1, _LANES), lambda t: (0, t))
    return lanes, u_spec, bd_spec, cd_spec, lam_spec, d_spec


def _fill_block_diag(dst_ref, blocks_ref):
    _, a, b = blocks_ref.shape
    dst_ref[...] = jnp.zeros(dst_ref.shape, F32)
    for g in range(_S5_GT):
        dst_ref[g * a:(g + 1) * a, g * b:(g + 1) * b] = blocks_ref[g]


def _take_block_diag(dst_ref, v):
    _, a, b = dst_ref.shape
    for g in range(_S5_GT):
        dst_ref[g] = v[g * a:(g + 1) * a, g * b:(g + 1) * b]


def _s5_dense(bdr, bdi, cdr, cdi, dense):
    for src, dst in zip((bdr, bdi, cdr, cdi), dense):
        _fill_block_diag(dst, src)
    return [dst[...] for dst in dense]


def _s5_dense_scratch(lanes):
    return [pltpu.VMEM((_LANES, lanes), F32), pltpu.VMEM((_LANES, lanes), F32),
            pltpu.VMEM((lanes, _LANES), F32), pltpu.VMEM((lanes, _LANES), F32)]


def _s5_fwd(u, bd_re, bd_im, cd_re, cd_im, lam_re, lam_im, d):
    L, Wd = u.shape
    T = Wd // _LANES
    nk = L // _SEG
    lanes, u_spec, bd_spec, cd_spec, lam_spec, d_spec = _s5_specs(L, T)
    s_spec = pl.BlockSpec((L, lanes), lambda t: (0, t))

    def body(u_ref, bdr, bdi, cdr, cdi, lr, li, d_ref, g_ref, s_re, s_im, up, tmp, *dense):
        ar = jnp.broadcast_to(lr[...], (_SEG, lanes))
        ai = jnp.broadcast_to(li[...], (_SEG, lanes))
        bd_re_v, bd_im_v, cd_re_v, cd_im_v = _s5_dense(bdr, bdi, cdr, cdi, dense)
        _interleave(u_ref, up, nk)
        y = _s5_tile_fwd(up[...], bd_re_v, bd_im_v, cd_re_v, cd_im_v, ar, ai, d_ref[...], s_re, s_im, nk)
        up[...] = _gelu(y)
        _deinterleave(up, tmp, nk)
        g_ref[...] = tmp[...].astype(g_ref.dtype)

    return pl.pallas_call(
        body, name="s5_fwd", grid=(T,),
        in_specs=[u_spec, bd_spec, bd_spec, cd_spec, cd_spec, lam_spec, lam_spec, d_spec],
        out_specs=[u_spec, s_spec, s_spec],
        out_shape=[jax.ShapeDtypeStruct((L, Wd), _MXU), jax.ShapeDtypeStruct((L, T * lanes), F32),
                   jax.ShapeDtypeStruct((L, T * lanes), F32)],
        scratch_shapes=[pltpu.VMEM((L, _LANES), F32) for _ in range(2)] + _s5_dense_scratch(lanes),
        compiler_params=_cparams(("arbitrary",)),
    )(u, bd_re, bd_im, cd_re, cd_im, lam_re, lam_im, d)


def _s5_bwd(u, dg, states_re, states_im, bd_re, bd_im, cd_re, cd_im, lam_re, lam_im, d):
    L, Wd = u.shape
    T = Wd // _LANES
    nk = L // _SEG
    lanes, u_spec, bd_spec, cd_spec, lam_spec, d_spec = _s5_specs(L, T)
    s_spec = pl.BlockSpec((L, lanes), lambda t: (0, t))

    def body(u_ref, dg_ref, s_re, s_im, bdr, bdi, cdr, cdi, lr, li, d_ref,
             du_ref, dbdr, dbdi, dcdr, dcdi, dlr, dli, dd_ref, g_re, g_im, up, dgp, tmp, *dense):
        ar = jnp.broadcast_to(lr[...], (_SEG, lanes))
        ai = jnp.broadcast_to(li[...], (_SEG, lanes))
        bd_re_v, bd_im_v, cd_re_v, cd_im_v = _s5_dense(bdr, bdi, cdr, cdi, dense)
        _interleave(u_ref, up, nk)
        _interleave(dg_ref, dgp, nk)
        uv, dv = up[...], d_ref[...]
        y = _dot(s_re[...], cd_re_v, _NN) - _dot(s_im[...], cd_im_v, _NN) + dv * uv
        dy = dgp[...] * _gelu_grad(y)
        dd_ref[...] = jnp.sum(dy * uv, axis=0, keepdims=True)
        dyb = dy.astype(_MXU)
        _take_block_diag(dcdr, _dot(dyb, s_re[...], _TN))
        _take_block_diag(dcdi, -_dot(dyb, s_im[...], _TN))
        g_re[...] = _dot(dyb, cd_re_v, _NT)
        g_im[...] = -_dot(dyb, cd_im_v, _NT)

        nai = -ai

        def step(j, s):
            rows = _seg_rows(nk - 1 - j)
            mr, mi = _cmul(ar, nai, s[0], s[1])
            return mr + g_re[rows, :], mi + g_im[rows, :]

        zero = jnp.zeros((_SEG, lanes), F32)
        er, ei = _unrolled(nk, step, (zero, zero))
        ir, ii = _segment_inits(er, ei, ar, nai, nk, False)

        def acc_lam(gr, gi, pr, pi, acc):
            return acc[0] + gr * pr + gi * pi, acc[1] + gi * pr - gr * pi

        def step2(j, carry):
            s, acc = carry
            k = nk - 1 - j
            rows = _seg_rows(k)
            mr, mi = _cmul(ar, nai, s[0], s[1])
            nr, ni = mr + g_re[rows, :], mi + g_im[rows, :]
            g_re[rows, :] = nr
            g_im[rows, :] = ni
            prev = _seg_rows(k - 1)
            return (nr, ni), acc_lam(nr, ni, s_re[prev, :], s_im[prev, :], acc)

        (g0r, g0i), acc = _unrolled(nk - 1, step2, ((ir, ii), (zero, zero)))
        first = _seg_rows(0)
        mr, mi = _cmul(ar, nai, g0r, g0i)
        nr, ni = mr + g_re[first, :], mi + g_im[first, :]
        g_re[first, :] = nr
        g_im[first, :] = ni
        last = _seg_rows(nk - 1)
        acc = acc_lam(nr, ni, _shift_rows(s_re[last, :], True), _shift_rows(s_im[last, :], True), acc)
        dlr[...] = jnp.sum(acc[0], axis=0, keepdims=True)
        dli[...] = jnp.sum(acc[1], axis=0, keepdims=True)

        gtr, gti = g_re[...].astype(_MXU), g_im[...].astype(_MXU)
        ub = uv.astype(_MXU)
        _take_block_diag(dbdr, _dot(ub, gtr, _TN))
        _take_block_diag(dbdi, _dot(ub, gti, _TN))
        dgp[...] = _dot(gtr, bd_re_v, _NT) + _dot(gti, bd_im_v, _NT) + dy * dv
        _deinterleave(dgp, tmp, nk)
        du_ref[...] = tmp[...].astype(du_ref.dtype)

    sd = jax.ShapeDtypeStruct
    big = sd((T, _S5_GT, _S5_C, _S5_P), F32)
    return pl.pallas_call(
        body, name="s5_bwd", grid=(T,),
        in_specs=[u_spec, u_spec, s_spec, s_spec, bd_spec, bd_spec, cd_spec, cd_spec, lam_spec, lam_spec, d_spec],
        out_specs=[u_spec, bd_spec, bd_spec, bd_spec, bd_spec, lam_spec, lam_spec, d_spec],
        out_shape=[sd((L, Wd), _MXU), big, big, big, big, sd((T, 1, lanes), F32), sd((T, 1, lanes), F32),
                   sd((1, Wd), F32)],
        scratch_shapes=[pltpu.VMEM((L, lanes), F32) for _ in range(2)]
        + [pltpu.VMEM((L, _LANES), F32) for _ in range(3)] + _s5_dense_scratch(lanes),
        compiler_params=_cparams(("arbitrary",)),
    )(u, dg, states_re, states_im, bd_re, bd_im, cd_re, cd_im, lam_re, lam_im, d)


def _half_tile(R, few_arrays=False):
    for t in ((512, 704, 128) if few_arrays else (256, 352, 128)):
        if R % t == 0:
            return t
    raise ValueError(R)


def _cast_shard(name, w, layer, kind, R, C):
    tr = _half_tile(R, True)
    nr = R // tr

    def body(w_ref, o_ref):
        o_ref[...] = w_ref[...].astype(o_ref.dtype)

    if kind == "col":
        in_map = lambda h, i: (layer, h * nr + i, 0)
    else:
        in_map = lambda h, i: (layer, i, h)
    return pl.pallas_call(
        body, name=name, grid=(2, nr), in_specs=[pl.BlockSpec((None, tr, C), in_map)],
        out_specs=pl.BlockSpec((None, tr, C), lambda h, i: (h, i, 0)),
        out_shape=jax.ShapeDtypeStruct((2, R, C), _WIRE),
        compiler_params=_cparams(("arbitrary", "arbitrary")),
    )(w)


def _adam_math(w, g, m, v):
    m2 = _B1 * m + (1.0 - _B1) * g
    v2 = _B2 * v + (1.0 - _B2) * (g * g)
    m_hat = m2 / (1.0 - _B1 ** _STEP)
    v_hat = v2 / (1.0 - _B2 ** _STEP)
    delta = -_LR * (m_hat / (jnp.sqrt(v_hat) + _AEPS) + _WD * w)
    return delta, m2, v2


def _adamw_big(name, w, m, v, layer, pair, kind, R, C, c, after, prev):
    tr = _half_tile(R, few_arrays=C <= 1024 and R % 512 == 0)
    nr = R // tr

    def body(c_ref, w_ref, m_ref, v_ref, own_ref, other_ref, *rest):
        go_ref, d_ref, mo_ref, vo_ref = rest[-4:]
        g = jnp.where(pl.program_id(0) == c_ref[0], own_ref[...], other_ref[...])
        delta, m2, v2 = _adam_math(w_ref[...], g, m_ref[...], v_ref[...])
        go_ref[...] = g
        d_ref[...] = delta
        mo_ref[...] = m2
        vo_ref[...] = v2

    if kind == "col":
        nat = pl.BlockSpec((None, tr, C), lambda h, i, c_ref: (layer, h * nr + i, 0))
    else:
        nat = pl.BlockSpec((None, tr, C), lambda h, i, c_ref: (layer, i, h))

    def gspec(own):
        return pl.BlockSpec((tr, C), lambda h, i, c_ref: (jnp.where((h == c_ref[0]) == own, i, 0), 0))

    carried = list(prev) if prev is not None else []
    gs = pltpu.PrefetchScalarGridSpec(
        num_scalar_prefetch=1, grid=(2, nr),
        in_specs=[nat, nat, nat, gspec(True), gspec(False), _any()] + [_any()] * len(carried),
        out_specs=[nat] * 4)
    sd = jax.ShapeDtypeStruct(w.shape, F32)
    return pl.pallas_call(
        body, name=name, grid_spec=gs, out_shape=[sd] * 4,
        input_output_aliases={7 + k: k for k in range(len(carried))},
        compiler_params=_cparams(("arbitrary", "arbitrary")),
    )(c, w, m, v, pair[0], pair[1], after, *carried)


def _adamw_small(ws, gs, ms, vs):
    n = len(ws)

    def body(*refs):
        for i in range(n):
            w_ref, g_ref, m_ref, v_ref, d_ref, mo_ref, vo_ref = refs[i::n]
            delta, m2, v2 = _adam_math(w_ref[...], g_ref[...], m_ref[...], v_ref[...])
            d_ref[...] = delta
            mo_ref[...] = m2
            vo_ref[...] = v2

    sds = [jax.ShapeDtypeStruct(w.shape, F32) for w in ws]
    outs = pl.pallas_call(
        body, name="adamw_small", in_specs=[_vm()] * (4 * n), out_specs=[_vm()] * (3 * n), out_shape=sds * 3,
        compiler_params=pltpu.CompilerParams(vmem_limit_bytes=_VMEM_LIMIT),
    )(*ws, *gs, *ms, *vs)
    return outs[:n], outs[n:2 * n], outs[2 * n:]


def _place():
    x, y, c = lax.axis_index("x"), lax.axis_index("y"), lax.axis_index("c")
    chips = [(1 - x, y), (x, 1 - y), (1 - x, 1 - y)]
    return x, y, c, 2 * x + y, chips


def _any():
    return pl.BlockSpec(memory_space=pl.ANY)


def _remote(src, dst, ssem, rsem, dev):
    return pltpu.make_async_remote_copy(src_ref=src, dst_ref=dst, send_sem=ssem, recv_sem=rsem,
                                        device_id=dev, device_id_type=_MESH)


def _allgather(name, shards):
    n = len(shards)

    def body(*refs):
        s_refs, g_refs = refs[:n], refs[n:2 * n]
        send0, recv0, send1, recv1, send2, recv2 = refs[2 * n:]
        x, y, c, q, _ = _place()
        sib, xn, yn = (x, y, 1 - c), (1 - x, y, c), (x, 1 - y, c)
        qx, qy, qd = 2 * (1 - x) + y, 2 * x + (1 - y), 2 * (1 - x) + (1 - y)
        _handshake([sib, xn, yn])
        own = [_remote(s_refs[a], g_refs[a].at[q], send0.at[a], recv0.at[a], sib) for a in range(n)]

        def pieces(a):
            g, half = g_refs[a], s_refs[a].shape[1] // 2
            return [g.at[qx, c], g.at[qy, c], g.at[qd, c, pl.ds(0, half)], g.at[qd, c, pl.ds(half, half)]]

        def relayed(a):
            g, half = g_refs[a], s_refs[a].shape[1] // 2
            return [(g.at[qx, c, pl.ds(0, half)], yn), (g.at[qy, c, pl.ds(half, half)], xn)]

        first = []
        for a in range(n):
            first.append(_remote(s_refs[a].at[c], g_refs[a].at[q, c], send1.at[4 * a], recv1.at[4 * a], xn))
            first.append(_remote(s_refs[a].at[c], g_refs[a].at[q, c], send1.at[4 * a + 1], recv1.at[4 * a + 1], yn))
        for cp in first + own:
            cp.start()
        later = []
        for a in range(n):
            land = pieces(a)
            for j in range(4):
                k = 4 * a + j
                _remote(land[j], land[j], send1.at[k], recv1.at[k], xn).wait_recv()
                if j < 2:
                    src, to = relayed(a)[j]
                    cp = _remote(src, src, send1.at[k + 2], recv1.at[k + 2], to)
                    cp.start()
                    later.append(cp)
                cp = _remote(land[j], land[j], send2.at[k], recv2.at[k], sib)
                cp.start()
                later.append(cp)
        for a in range(n):
            g, half = g_refs[a], s_refs[a].shape[1] // 2
            theirs = [g.at[qx, 1 - c], g.at[qy, 1 - c], g.at[qd, 1 - c, pl.ds(0, half)],
                      g.at[qd, 1 - c, pl.ds(half, half)]]
            for j in range(4):
                _remote(theirs[j], theirs[j], send2.at[4 * a + j], recv2.at[4 * a + j], sib).wait_recv()
        for cp in own:
            cp.wait()
        for cp in first + later:
            cp.wait_send()

    return _sequencer(name, _ID_GATHER, body, shards,
                      [jax.ShapeDtypeStruct((4,) + s.shape, s.dtype) for s in shards], [n, n] + [4 * n] * 4)


def _handshake(peers):
    barrier = pltpu.get_barrier_semaphore()
    for peer in peers:
        pl.semaphore_signal(barrier, inc=1, device_id=peer, device_id_type=_MESH)
    pl.semaphore_wait(barrier, len(peers))


_ID_SIBLING, _ID_CHIPS, _ID_GATHER, _ID_ALL = 1, 2, 3, 4


def _sequencer(name, collective_id, body, ins, out_types, sem_counts):
    mesh = plsc.ScalarSubcoreMesh(axis_name="seq", num_cores=1)
    moved = sum(math.prod(o.shape) * jnp.dtype(o.dtype).itemsize for o in out_types)
    return pl.kernel(
        body, name=name, out_type=out_types, mesh=mesh,
        scratch_types=[pltpu.SemaphoreType.DMA((k,)) for k in sem_counts],
        compiler_params=pltpu.CompilerParams(collective_id=collective_id),
        cost_estimate=pl.CostEstimate(flops=0, transcendentals=0, bytes_accessed=2 * moved,
                                      remote_bytes_transferred=moved),
    )(*ins)


def _swap_halves(name, grads):
    n = len(grads)

    def body(*refs):
        g_refs, t_refs = refs[:n], refs[n:2 * n]
        send, recv = refs[2 * n:]
        x, y, c, _, _ = _place()
        _handshake([(x, y, 1 - c)])
        cps = [_remote(g_refs[a].at[1 - c], t_refs[a], send.at[a], recv.at[a], (x, y, 1 - c)) for a in range(n)]
        for cp in cps:
            cp.start()
        for cp in cps:
            cp.wait()

    return _sequencer(name, _ID_SIBLING, body, grads,
                      [jax.ShapeDtypeStruct(g.shape[1:], g.dtype) for g in grads], [n, n])


def _chip_sum(name, g, t, after, core):
    _, _, R, C = g.shape
    tr = _half_tile(R, True)

    def body(c_ref, g_ref, t_ref, after_ref, o_ref):
        o_ref[...] = (g_ref[...].astype(F32) + t_ref[...].astype(F32)).astype(o_ref.dtype)

    gs = pltpu.PrefetchScalarGridSpec(
        num_scalar_prefetch=1, grid=(4, R // tr),
        in_specs=[pl.BlockSpec((None, None, tr, C), lambda r, i, c_ref: (c_ref[0], r, i, 0)),
                  pl.BlockSpec((None, tr, C), lambda r, i, c_ref: (r, i, 0)), _any()],
        out_specs=pl.BlockSpec((None, tr, C), lambda r, i, c_ref: (r, i, 0)))
    return pl.pallas_call(
        body, name=name, grid_spec=gs, out_shape=jax.ShapeDtypeStruct((4, R, C), _WIRE),
        compiler_params=_cparams(("arbitrary", "arbitrary")),
    )(core, g, t, after)


def _scatter_parts(name, parts):
    n = len(parts)

    def body(*refs):
        p_refs, t_refs = refs[:n], refs[n:2 * n]
        send, recv = refs[2 * n:]
        x, y, c, q, chips = _place()
        _handshake([(rx, ry, c) for rx, ry in chips])
        cps = []
        for a in range(n):
            for j, (rx, ry) in enumerate(chips):
                k = 3 * a + j
                cps.append(_remote(p_refs[a].at[2 * rx + ry], t_refs[a].at[q], send.at[k], recv.at[k], (rx, ry, c)))
        for cp in cps:
            cp.start()
        for a in range(n):
            for j, (rx, ry) in enumerate(chips):
                k = 3 * a + j
                land = t_refs[a].at[2 * rx + ry]
                _remote(land, land, send.at[k], recv.at[k], (rx, ry, c)).wait_recv()
        for cp in cps:
            cp.wait_send()

    return _sequencer(name, _ID_CHIPS, body, parts,
                      [jax.ShapeDtypeStruct(p.shape, p.dtype) for p in parts], [3 * n, 3 * n])


def _sum_parts(name, p, t, where, after):
    _, R, C = t.shape
    tr = _half_tile(R, True)

    def body(w_ref, p_ref, t0_ref, t1_ref, t2_ref, after_ref, o_ref):
        o_ref[...] = (p_ref[...].astype(F32) + t0_ref[...].astype(F32)
                      + t1_ref[...].astype(F32) + t2_ref[...].astype(F32))

    def part(slot):
        return pl.BlockSpec((None, tr, C), lambda i, w_ref: (w_ref[slot], i, 0))

    gs = pltpu.PrefetchScalarGridSpec(
        num_scalar_prefetch=1, grid=(R // tr,), in_specs=[part(0), part(1), part(2), part(3), _any()],
        out_specs=pl.BlockSpec((tr, C), lambda i, w_ref: (i, 0)))
    return pl.pallas_call(
        body, name=name, grid_spec=gs, out_shape=jax.ShapeDtypeStruct((R, C), F32),
        compiler_params=_cparams(("arbitrary",)),
    )(where, p, t, t, t, after)


def _send_halves(name, halves):
    n = len(halves)

    def body(*refs):
        h_refs, o_refs = refs[:n], refs[n:2 * n]
        send, recv = refs[2 * n:]
        x, y, c, _, _ = _place()
        _handshake([(x, y, 1 - c)])
        cps = [_remote(h_refs[a], o_refs[a], send.at[a], recv.at[a], (x, y, 1 - c)) for a in range(n)]
        for cp in cps:
            cp.start()
        for cp in cps:
            cp.wait()

    return _sequencer(name, _ID_SIBLING, body, halves,
                      [jax.ShapeDtypeStruct(h.shape, h.dtype) for h in halves], [n, n])


class _Order:
    def __init__(self):
        self.tok = None
        x, y, c, q, chips = _place()
        self.core = c.astype(jnp.int32).reshape(1)
        self.where = jnp.stack([q] + [2 * rx + ry for rx, ry in chips]).astype(jnp.int32)

    def tie(self, x):
        return x if self.tok is None else lax.optimization_barrier((x, self.tok))[0]

    def done(self, outs):
        self.tok = outs[0]
        return outs


class _Reduction:
    def __init__(self, tag, grads):
        self.tag, self.grads = tag, grads
        self.swapped = _swap_halves(f"rs_swap_{tag}", grads)

    def scatter(self, seq):
        self.parts = [seq.done([_chip_sum(f"rs_chipsum_{self.tag}_{a}", g, t, seq.tok, seq.core)])[0]
                      for a, (g, t) in enumerate(zip(self.grads, self.swapped))]
        self.landed = _scatter_parts(f"rs_scatter_{self.tag}", self.parts)

    def finish(self, seq):
        halves = [seq.done([_sum_parts(f"rs_sum_{self.tag}_{a}", p, t, seq.where, seq.tok)])[0]
                  for a, (p, t) in enumerate(zip(self.parts, self.landed))]
        return list(zip(halves, _send_halves(f"rs_join_{self.tag}", halves)))


def _allreduce_small(name, v):
    _, R, _ = v.shape

    def body(v_ref, o_ref, land, acc, send1, recv1, send2, recv2):
        x, y, c = lax.axis_index("x"), lax.axis_index("y"), lax.axis_index("c")
        me = 4 * x + 2 * y + c
        peers = []
        for k in range(1, 8):
            dx, dy, dc = (k >> 2) & 1, (k >> 1) & 1, k & 1
            px, py, pc = (1 - x if dx else x), (1 - y if dy else y), (1 - c if dc else c)
            peers.append((k, (px, py, pc), 4 * px + 2 * py + pc))
        land[me] = v_ref[me]
        out1 = [_remote(v_ref.at[pid], land.at[me], send1.at[k], recv1.at[k], dev) for k, dev, pid in peers]
        for cp in out1:
            cp.start()
        for k, dev, pid in peers:
            _remote(land.at[pid], land.at[pid], send1.at[k], recv1.at[k], dev).wait_recv()
        total = land[0]
        for j in range(1, 8):
            total = total + land[j]
        acc[...] = total
        o_ref[me] = total
        out2 = [_remote(acc, o_ref.at[me], send2.at[k], recv2.at[k], dev) for k, dev, pid in peers]
        for cp in out2:
            cp.start()
        for k, dev, pid in peers:
            _remote(o_ref.at[pid], o_ref.at[pid], send2.at[k], recv2.at[k], dev).wait_recv()
        for cp in out1 + out2:
            cp.wait_send()

    return pl.pallas_call(
        body, name=name, in_specs=[_vm()], out_specs=_vm(),
        out_shape=jax.ShapeDtypeStruct(v.shape, F32),
        scratch_shapes=[pltpu.VMEM(v.shape, F32), pltpu.VMEM((R, _LANES), F32)]
        + [pltpu.SemaphoreType.DMA((8,)) for _ in range(4)],
        compiler_params=pltpu.CompilerParams(vmem_limit_bytes=_VMEM_LIMIT),
    )(v)


def _all_peers():
    x, y, c = lax.axis_index("x"), lax.axis_index("y"), lax.axis_index("c")
    peers = []
    for k in range(1, 8):
        px, py, pc = (1 - x if k & 4 else x), (1 - y if k & 2 else y), (1 - c if k & 1 else c)
        peers.append((k, (px, py, pc), 4 * px + 2 * py + pc))
    return 4 * x + 2 * y + c, peers


def _exchange_slices(name, v):
    def body(v_ref, land, send, recv):
        me, peers = _all_peers()
        _handshake([dev for _, dev, _ in peers])
        cps = [_remote(v_ref.at[pid], land.at[me], send.at[k], recv.at[k], dev) for k, dev, pid in peers]
        for cp in cps:
            cp.start()
        for k, dev, pid in peers:
            _remote(land.at[pid], land.at[pid], send.at[k], recv.at[k], dev).wait_recv()
        for cp in cps:
            cp.wait_send()

    return _sequencer(name, _ID_ALL, body, [v], [jax.ShapeDtypeStruct(v.shape, v.dtype)], [8, 8])[0]


def _sum_slices(name, v, landed, after):
    _, R, _ = v.shape

    def body(v_ref, land_ref, after_ref, o_ref):
        me, peers = _all_peers()
        acc = v_ref[me]
        for _, _, pid in peers:
            acc = acc + land_ref[pid]
        o_ref[...] = acc

    return pl.pallas_call(
        body, name=name, in_specs=[_vm(), _vm(), _any()], out_specs=_vm(),
        out_shape=jax.ShapeDtypeStruct((R, _LANES), F32),
        compiler_params=pltpu.CompilerParams(vmem_limit_bytes=_VMEM_LIMIT),
    )(v, landed, after)


def _broadcast_slices(name, s):
    def body(s_ref, out, send, recv):
        me, peers = _all_peers()
        _handshake([dev for _, dev, _ in peers])
        cps = [_remote(s_ref, out.at[me], send.at[k], recv.at[k], dev) for k, dev, pid in peers]
        for cp in cps:
            cp.start()
        for k, dev, pid in peers:
            _remote(out.at[pid], out.at[pid], send.at[k], recv.at[k], dev).wait_recv()
        for cp in cps:
            cp.wait_send()

    return _sequencer(name, _ID_ALL, body, [s], [jax.ShapeDtypeStruct((8,) + s.shape, s.dtype)], [8, 8])[0]


_WEIGHT_NAMES = ['norm_mix', 'norm_ffn', 'norm_ple', 'norm_final', 'gm_w_in', 'gm_ln_g', 'gm_ln_b', 'gm_w_s',
                 'gm_b_s', 'gm_w_out', 's5_w_in', 's5_a_re', 's5_a_im', 's5_log_dt', 's5_b_re', 's5_b_im',
                 's5_c_re', 's5_c_im', 's5_d', 's5_w_out', 'ffn_w1', 'ffn_w3', 'ffn_w2', 'ple_w_gate', 'ple_w_proj']
_BIG = {'gm_w_in': 'col', 'gm_w_out': 'row', 's5_w_in': 'row', 's5_w_out': 'col', 'ffn_w1': 'col',
        'ffn_w3': 'col', 'ffn_w2': 'row', 'ple_w_gate': 'row', 'ple_w_proj': 'col'}


_VIEW = {'s5_a_re': (0, 2, 1), 's5_a_im': (0, 2, 1), 's5_b_re': (0, 2, 3, 1), 's5_b_im': (0, 2, 3, 1),
         's5_c_re': (0, 2, 3, 1), 's5_c_im': (0, 2, 3, 1)}


def _to_view(name, a):
    return jnp.transpose(a, _VIEW[name]) if name in _VIEW else a


def _from_view(name, a):
    if name not in _VIEW:
        return a
    perm = _VIEW[name]
    return jnp.transpose(a, [perm.index(i) for i in range(len(perm))])


def _rc(kind, shard_shape):
    rows, cols = shard_shape[-2:]
    return (rows // 2, cols) if kind == "col" else (rows, cols // 2)


def _pack(vecs, rows_multiple):
    flat = jnp.concatenate([a.reshape(-1).astype(F32) for a in vecs])
    unit = rows_multiple * _LANES
    pad = (-flat.shape[0]) % unit
    return jnp.pad(flat, (0, pad)).reshape(-1, _LANES)


def _unpack(buf, shapes):
    flat = buf.reshape(-1)
    out, off = [], 0
    for s in shapes:
        n = math.prod(s)
        out.append(flat[off:off + n].reshape(s))
        off += n
    return out


def _ident(accs, ex):
    return accs


def _add_resid(accs, ex):
    return [accs[0] + ex[0]]


def _swiglu_epi(accs, ex):
    a, b = accs
    return [a, b, a * _sig(a) * b]


def _swiglu_bwd_epi(accs, ex):
    df = accs[0]
    a, b = ex[0].astype(F32), ex[1].astype(F32)
    sa = _sig(a)
    return [df * b * (sa * (1.0 + a * (1.0 - sa))), df * (a * sa)]


def _ple_epi(accs, ex):
    gt = _sig(accs[0])
    return [ex[0] + gt * ex[1], gt]


def _glu_epi(accs, ex):
    val, sg = accs[0], _sig(accs[1])
    return [ex[0] + val * sg, val, sg]


def kernel(x, p, norm_mix, norm_ffn, norm_ple, norm_final, gm_w_in, gm_ln_g, gm_ln_b, gm_w_s, gm_b_s, gm_w_out, s5_w_in, s5_a_re, s5_a_im, s5_log_dt, s5_b_re, s5_b_im, s5_c_re, s5_c_im, s5_d, s5_w_out, ffn_w1, ffn_w3, ffn_w2, ple_w_gate, ple_w_proj, loss_target, m_norm_mix, m_norm_ffn, m_norm_ple, m_norm_final, m_gm_w_in, m_gm_ln_g, m_gm_ln_b, m_gm_w_s, m_gm_b_s, m_gm_w_out, m_s5_w_in, m_s5_a_re, m_s5_a_im, m_s5_log_dt, m_s5_b_re, m_s5_b_im, m_s5_c_re, m_s5_c_im, m_s5_d, m_s5_w_out, m_ffn_w1, m_ffn_w3, m_ffn_w2, m_ple_w_gate, m_ple_w_proj, v_norm_mix, v_norm_ffn, v_norm_ple, v_norm_final, v_gm_w_in, v_gm_ln_g, v_gm_ln_b, v_gm_w_s, v_gm_b_s, v_gm_w_out, v_s5_w_in, v_s5_a_re, v_s5_a_im, v_s5_log_dt, v_s5_b_re, v_s5_b_im, v_s5_c_re, v_s5_c_im, v_s5_d, v_s5_w_out, v_ffn_w1, v_ffn_w3, v_ffn_w2, v_ple_w_gate, v_ple_w_proj):
    env = dict(locals())
    w = {n: env[n] for n in _WEIGHT_NAMES}
    mom = {n: env["m_" + n] for n in _WEIGHT_NAMES}
    var = {n: env["v_" + n] for n in _WEIGHT_NAMES}
    xs, tgt = x[0], loss_target[0]
    L, D = xs.shape
    depth = norm_mix.shape[0]
    qx, qy = lax.axis_index("x"), lax.axis_index("y")
    q = 2 * qx + qy

    W, last_cast = {}, [None]

    def gather(tag, items, after=None):
        shards = []
        for name, layer in items:
            kind = _BIG[name]
            R, C = _rc(kind, w[name].shape)
            src = w[name] if after is None else lax.optimization_barrier((w[name], after))[0]
            shards.append(_cast_shard(f"cast_{name}{layer}", src, layer, kind, R, C))
        last_cast[0] = shards[-1]
        full = _allgather(f"ag_{tag}", shards)
        W.update({it: _W(f, _BIG[it[0]]) for it, f in zip(items, full)})

    gather("gm_w_in", [("gm_w_in", 0)])
    gather("gm_w_out", [("gm_w_out", 0)])
    gather("ffn_up0", [("ffn_w1", 0), ("ffn_w3", 0)])
    gather("ffn_down0", [("ffn_w2", 0)])

    def gather_rest(after):
        gather("ple0", [("ple_w_gate", 0), ("ple_w_proj", 0)], after)
        gather("s5_w_in", [("s5_w_in", 0)], after)
        gather("s5_w_out", [("s5_w_out", 0)], after)
        gather("ffn_up1", [("ffn_w1", 1), ("ffn_w3", 1)], after)
        gather("ffn_down1", [("ffn_w2", 1)], after)
        gather("ple1", [("ple_w_gate", 1), ("ple_w_proj", 1)], after)

    d_slots = jnp.zeros((4, D // 4), F32)
    d_slots = lax.dynamic_update_slice(d_slots, s5_d.astype(F32), (q, 0))
    d_sum = _allreduce_small("ar_s5_d", _pack([d_slots], 64).reshape(8, -1, _LANES))
    d_full = (d_sum.reshape(-1)[:D] * 0.5).reshape(1, D)

    def ffn_fwd(i, xin):
        hf = _rms_fwd(f"rms_ffn{i}", xin, norm_ffn[i:i + 1])
        a, b, f = _mm_nn(f"ffn_up{i}", hf, [W["ffn_w1", i], W["ffn_w3", i]], 1024, 1408, ffn_w2.shape[1] * 4,
                         [], [_MXU, _MXU, _MXU], _swiglu_epi, tm=1024)
        xo = _mm_nn(f"ffn_down{i}", f, [W["ffn_w2", i]], 1408, 1024, D, [xin], [F32], _add_resid, tm=1024)[0]
        return xo, (xin, hf, a, b, f)

    def ple_fwd(i, xin):
        hp = _rms_fwd(f"rms_ple{i}", xin, norm_ple[i:i + 1])
        pi = lax.optimization_barrier((p[i, 0], hp))[0]
        pp = _mm_nn(f"ple_proj{i}", pi, [W["ple_w_proj", i]], 128, 512, D, [], [F32], _ident, tm=2048)[0]
        xo, gt = _mm_nn(f"ple_gate{i}", hp, [W["ple_w_gate", i]], 512, 1024, D, [xin, pp], [F32, _MXU], _ple_epi,
                        tm=1024)
        return xo, (xin, hp, pi, pp, gt)

    h0 = _rms_fwd("rms_mix0", xs, norm_mix[0:1])
    z = _mm_nn("gm_in", h0, [W["gm_w_in", 0]], 1024, 1024, 2 * D, [], [F32], _ident, tm=1024)[0]
    bsT = gm_b_s[0].T
    gm_m = _gmlp_fwd(z, gm_ln_g, gm_ln_b, gm_w_s[0], bsT)
    x1 = _mm_nn("gm_out", gm_m, [W["gm_w_out", 0]], 512, 1024, D, [xs], [F32], _add_resid, tm=1024)[0]
    gather_rest(x1)
    x1 = lax.optimization_barrier((x1, last_cast[0]))[0]
    x2, ffn0 = ffn_fwd(0, x1)
    x3, ple0 = ple_fwd(0, x2)

    T = D // _LANES
    lanes = _S5_GT * _S5_P
    sv = {n: _to_view(n, w[n])[0] for n in _VIEW}
    a_re, a_im, log_dt = sv["s5_a_re"], sv["s5_a_im"], s5_log_dt
    lbr, lbi, Bbar_re, Bbar_im = _s5_prep(a_re, a_im, log_dt, sv["s5_b_re"], sv["s5_b_im"])

    def to_bd(B):
        return jnp.transpose(B.reshape(_S5_P, _S5_C, T, _S5_GT), (2, 3, 1, 0))

    def to_cd(cw):
        return jnp.transpose(cw.reshape(_S5_C, _S5_P, T, _S5_GT), (2, 3, 1, 0))

    def to_lam(v):
        return jnp.transpose(v).reshape(T, 1, lanes)

    bd_re, bd_im = to_bd(Bbar_re), to_bd(Bbar_im)
    cd_re, cd_im = to_cd(sv["s5_c_re"]), to_cd(sv["s5_c_im"])
    lam_re, lam_im = to_lam(lbr), to_lam(lbi)

    h1 = _rms_fwd("rms_mix1", x3, norm_mix[1:2])
    u = _mm_nn("s5_in", h1, [W["s5_w_in", 0]], 512, 1024, D, [], [F32], _ident, tm=1024)[0]
    s5_g, s5_re, s5_im = _s5_fwd(u, bd_re, bd_im, cd_re, cd_im, lam_re, lam_im, d_full)
    x4, glu_val, glu_sg = _mm_nn("s5_out", s5_g, [W["s5_w_out", 0], W["s5_w_out", 0]], 1024, 1024, D, [x3],
                                 [F32, _MXU, _MXU], _glu_epi, tm=1024, cb_offsets=[0, 2])
    x5, ffn1 = ffn_fwd(1, x4)
    x6, ple1 = ple_fwd(1, x5)

    dx, d_norm_final, loss_rows = _loss_head(x6, norm_final[None], tgt)

    small = {}

    seq = _Order()
    tie, done = seq.tie, seq.done
    d_norm_ple, d_norm_ffn, d_norm_mix = [None] * depth, [None] * depth, [None] * depth
    reduced = {}

    def keep(names, layer, pairs):
        for n, pr in zip(names, pairs):
            reduced[n, layer] = pr

    ple_names, up_names, down_names = ["ple_w_gate", "ple_w_proj"], ["ffn_w1", "ffn_w3"], ["ffn_w2"]

    def ple_bwd(i, dxo, saved):
        xin, hp, pi, pp, gt = saved
        dpre, dpp = done(_ple_bwd_elem(tie(dxo), pp, gt))
        dwg = done(_mm_tn(f"ple_gate_dw{i}", tie(hp), [dpre], "row", 512, 1024, 512, 1024))[0]
        dwp = done(_mm_tn(f"ple_proj_dw{i}", tie(pi), [dpp], "col", 128, 512, 128, 512))[0]
        red = _Reduction(f"ple{i}", [dwg, dwp])
        dhp = done(_mm_nt(f"ple_gate_dx{i}", [tie(dpre)], [W["ple_w_gate", i]], 512, 1024, [], [F32], _ident,
                          tm=2048))[0]
        dxin, dxin_mxu, dg = done(_rms_bwd(f"rms_ple_bwd{i}", tie(dhp), xin, norm_ple[i:i + 1], dxo))
        return dxin, dxin_mxu, dg, red

    def ffn_bwd(i, dxo, dxo_mxu, saved, before_up):
        xin, hf, a, b, f = saved
        dw2 = done(_mm_tn(f"ffn_down_dw{i}", tie(f), [dxo_mxu], "row", 1408, 1024, 1408, 1024))[0]
        r_down = _Reduction(f"ffd{i}", [dw2])
        da, db = done(_mm_nt(f"ffn_down_dx{i}", [tie(dxo_mxu)], [W["ffn_w2", i]], 1408, 1024, [a, b], [_MXU, _MXU],
                             _swiglu_bwd_epi, tm=1024))
        for step in before_up:
            step()
        r_down.scatter(seq)
        dw1, dw3 = done(_mm_tn(f"ffn_up_dw{i}", tie(hf), [da, db], "col", 1024, 1408, 1024, 1408))
        r_up = _Reduction(f"ffu{i}", [dw1, dw3])
        dhf = done(_mm_nt(f"ffn_up_dx{i}", [tie(da), db], [W["ffn_w1", i], W["ffn_w3", i]], 1024, 1408, [], [F32],
                          _ident, tm=1024))[0]
        dxin, dxin_mxu, dg = done(_rms_bwd(f"rms_ffn_bwd{i}", tie(dhf), xin, norm_ffn[i:i + 1], dxo))
        r_up.scatter(seq)
        return dxin, dxin_mxu, dg, r_down, r_up

    dx, dx_mxu, d_norm_ple[1], r_ple1 = ple_bwd(1, dx, ple1)
    dx, _, d_norm_ffn[1], r_down1, r_up1 = ffn_bwd(1, dx, dx_mxu, ffn1, [lambda: r_ple1.scatter(seq)])

    do = done([_glu_bwd_elem(tie(dx), glu_val, glu_sg)])[0]
    dw_s5out = done(_mm_tn("s5_out_dw", tie(s5_g), [do], "col", 1024, 1024, 1024, 1024))[0]
    r_s5out = _Reduction("s5out", [dw_s5out])
    dgy = done(_mm_nt("s5_out_dx", [tie(do)], [W["s5_w_out", 0]], 1024, 1024, [], [F32], _ident, tm=1024))[0]
    keep(ple_names, 1, r_ple1.finish(seq))
    keep(down_names, 1, r_down1.finish(seq))
    du, dbd_re, dbd_im, dcd_re, dcd_im, dl_re, dl_im, dd = done(_s5_bwd(
        tie(u), dgy, s5_re, s5_im, bd_re, bd_im, cd_re, cd_im, lam_re, lam_im, d_full))
    r_s5out.scatter(seq)
    dw_s5in = done(_mm_tn("s5_in_dw", tie(h1), [du], "row", 512, 1024, 512, 1024))[0]
    r_s5in = _Reduction("s5in", [dw_s5in])
    dh1 = done(_mm_nt("s5_in_dx", [tie(du)], [W["s5_w_in", 0]], 512, 1024, [], [F32], _ident, tm=2048))[0]
    dx, _, d_norm_mix[1] = done(_rms_bwd("rms_mix1_bwd", tie(dh1), x3, norm_mix[1:2], dx))
    keep(up_names, 1, r_up1.finish(seq))
    r_s5in.scatter(seq)

    def from_bd(t):
        return jnp.transpose(t, (3, 2, 0, 1)).reshape(_S5_P, _S5_C, T * _S5_GT)

    def from_cdT(t):
        return jnp.transpose(t, (2, 3, 0, 1)).reshape(_S5_C, _S5_P, T * _S5_GT)

    def from_lam(t):
        return jnp.transpose(t.reshape(T * _S5_GT, _S5_P))

    da_re, da_im, dlog_dt, db_re, db_im = _s5_prep_bwd(
        a_re, a_im, log_dt, sv["s5_b_re"], sv["s5_b_im"], from_lam(dl_re), from_lam(dl_im),
        from_bd(dbd_re), from_bd(dbd_im))
    small["s5_a_re"], small["s5_a_im"], small["s5_log_dt"] = da_re[None], da_im[None], dlog_dt
    small["s5_b_re"], small["s5_b_im"] = db_re[None], db_im[None]
    small["s5_c_re"], small["s5_c_im"] = from_cdT(dcd_re)[None], from_cdT(dcd_im)[None]

    dx, dx_mxu, d_norm_ple[0], r_ple0 = ple_bwd(0, dx, ple0)
    keep(["s5_w_out"], 0, r_s5out.finish(seq))
    xin0, hf0, a0, b0, f0 = ffn0
    da0, db0 = done(_mm_nt("ffn_down_dx0", [tie(dx_mxu)], [W["ffn_w2", 0]], 1408, 1024, [a0, b0], [_MXU, _MXU],
                           _swiglu_bwd_epi, tm=1024))
    r_ple0.scatter(seq)
    dw1, dw3 = done(_mm_tn("ffn_up_dw0", tie(hf0), [da0, db0], "col", 1024, 1408, 1024, 1408))
    r_up0 = _Reduction("ffu0", [dw1, dw3])
    keep(["s5_w_in"], 0, r_s5in.finish(seq))
    dw2 = done(_mm_tn("ffn_down_dw0", tie(f0), [dx_mxu], "row", 1408, 1024, 1408, 1024))[0]
    r_down0 = _Reduction("ffd0", [dw2])
    r_up0.scatter(seq)
    dhf0 = done(_mm_nt("ffn_up_dx0", [tie(da0), db0], [W["ffn_w1", 0], W["ffn_w3", 0]], 1024, 1408, [], [F32], _ident,
                       tm=1024))[0]
    dx, dx_mxu, d_norm_ffn[0] = done(_rms_bwd("rms_ffn_bwd0", tie(dhf0), xin0, norm_ffn[0:1], dx))
    keep(ple_names, 0, r_ple0.finish(seq))
    r_down0.scatter(seq)

    dw_gmout = done(_mm_tn("gm_out_dw", tie(gm_m), [dx_mxu], "row", 512, 1024, 512, 1024))[0]
    r_gmout = _Reduction("gmout", [dw_gmout])
    dgm = done(_mm_nt("gm_out_dx", [tie(dx_mxu)], [W["gm_w_out", 0]], 512, 1024, [], [F32], _ident, tm=2048))[0]
    dz, dws, dbsT, dlng, dlnb = done(_gmlp_bwd(tie(z), dgm, gm_ln_g, gm_ln_b, gm_w_s[0], bsT))
    dw_gmin = done(_mm_tn("gm_in_dw", tie(h0), [dz], "col", 1024, 1024, 1024, 1024))[0]
    r_gmin = _Reduction("gmin", [dw_gmin])
    dh0 = done(_mm_nt("gm_in_dx", [tie(dz)], [W["gm_w_in", 0]], 1024, 1024, [], [F32], _ident, tm=1024))[0]
    dx, _, d_norm_mix[0] = done(_rms_bwd("rms_mix0_bwd", tie(dh0), xs, norm_mix[0:1], dx))
    grad_x = dx[None]

    small["norm_mix"], small["norm_ffn"] = jnp.concatenate(d_norm_mix), jnp.concatenate(d_norm_ffn)
    small["norm_ple"], small["norm_final"] = jnp.concatenate(d_norm_ple), d_norm_final[0]
    small["gm_ln_g"], small["gm_ln_b"], small["gm_w_s"] = dlng, dlnb, dws[None]
    small["gm_b_s"] = dbsT[:, :_GM_HEADS].T[None]
    small["s5_d"] = dd

    small_names = [n for n in _WEIGHT_NAMES if n not in _BIG]
    packed = _pack([small[n] for n in small_names] + [loss_rows[:, :1]], 64).reshape(8, -1, _LANES)
    grads, deltas, new_m, new_v = {}, {}, {}, {}
    my_c = lax.axis_index("c")

    def adamw(n, layer, prev):
        kind = _BIG[n]
        R, C = _rc(kind, w[n].shape)
        return done(_adamw_big(f"adamw_{n}{layer}", w[n], mom[n], var[n], layer, reduced[n, layer], kind, R, C,
                               seq.core, seq.tok, prev))

    def adamw_last(names):
        for n in names:
            grads[n], deltas[n], new_m[n], new_v[n] = adamw(n, 0, late.get(n))

    late = {}
    for n in down_names + ple_names:
        late[n] = adamw(n, 1, None)
    keep(up_names, 0, r_up0.finish(seq))
    r_gmout.scatter(seq)
    r_gmin.scatter(seq)
    landed = _exchange_slices("ar_small_in", packed)
    for n in up_names:
        late[n] = adamw(n, 1, None)
    adamw_last(["s5_w_in", "s5_w_out"] + ple_names)
    keep(down_names, 0, r_down0.finish(seq))
    adamw_last(up_names)
    keep(["gm_w_out"], 0, r_gmout.finish(seq))
    mine = done([_sum_slices("ar_small_sum", packed, landed, seq.tok)])[0]
    spread = _broadcast_slices("ar_small_out", mine)
    adamw_last(down_names)
    keep(["gm_w_in"], 0, r_gmin.finish(seq))
    adamw_last(["gm_w_out", "gm_w_in"])
    summed = lax.dynamic_update_slice(spread, mine[None], (4 * qx + 2 * qy + my_c, 0, 0))
    *red_list, loss_sum = _unpack(summed, [small[n].shape for n in small_names] + [(1, 1)])
    red_small, loss = dict(zip(small_names, red_list)), loss_sum.reshape(())
    red_small["s5_d"] = lax.dynamic_slice(red_small["s5_d"], (0, q * (D // 4)), (1, D // 4))

    def two_d(a):
        return a.reshape((1,) * (2 - a.ndim) + a.shape)

    def views(src):
        return [two_d(_to_view(n, src[n])) for n in small_names]

    g_views = [two_d(red_small[n]) for n in small_names]
    dl, mo, vo = _adamw_small(views(w), g_views, views(mom), views(var))
    for n, g_, d_, m_, v_ in zip(small_names, g_views, dl, mo, vo):
        grads[n], deltas[n], new_m[n], new_v[n] = [_from_view(n, t_.reshape(_to_view(n, w[n]).shape)).reshape(w[n].shape)
                                                  for t_ in (g_, d_, m_, v_)]

    return (loss, grad_x, *[grads[n] for n in _WEIGHT_NAMES], *[deltas[n] for n in _WEIGHT_NAMES],
            *[new_m[n] for n in _WEIGHT_NAMES], *[new_v[n] for n in _WEIGHT_NAMES])
```

```python
import math

import jax
import jax.numpy as jnp
from jax import lax
from jax.experimental import pallas as pl
from jax.experimental.pallas import tpu as pltpu
from jax.experimental.pallas import tpu_sc as plsc

F32 = jnp.float32
_MXU = jnp.bfloat16
_WIRE = jnp.bfloat16
_EPS = 1e-6
_VMEM_LIMIT = 56 * 1024 * 1024
_LANES = 128
_MESH = pl.DeviceIdType.MESH

_LR, _B1, _B2, _AEPS, _WD, _STEP = 0.001, 0.9, 0.999, 1e-08, 0.01, 10

_GM_CHUNK = 128
_GM_HEADS = 16
_S5_GT = 8
_S5_P = 64
_S5_C = 16

_NN = (((1,), (0,)), ((), ()))
_NT = (((1,), (1,)), ((), ()))
_TN = (((0,), (0,)), ((), ()))


def _cparams(sem):
    return pltpu.CompilerParams(dimension_semantics=sem, vmem_limit_bytes=_VMEM_LIMIT)


def _sig(x):
    return 0.5 * jnp.tanh(0.5 * x) + 0.5


_GC = math.sqrt(2.0 / math.pi)


def _gelu(x):
    return 0.5 * x * (1.0 + jnp.tanh(_GC * (x + 0.044715 * (x * x * x))))


def _gelu_grad(x):
    t = jnp.tanh(_GC * (x + 0.044715 * (x * x * x)))
    return 0.5 * (1.0 + t) + 0.5 * x * (1.0 - t * t) * (_GC * (1.0 + 3.0 * 0.044715 * x * x))


def _dot(a, b, dn):
    return lax.dot_general(a.astype(_MXU), b.astype(_MXU), dn, preferred_element_type=F32)


class _W:
    def __init__(self, arr, kind):
        self.a, self.kind = arr, kind
        self.R, self.C = arr.shape[2], arr.shape[3]

    def full_shape(self):
        return (2 * self.R, 4 * self.C) if self.kind == "col" else (4 * self.R, 2 * self.C)


def _part_index(kind, R, C, tr, tc, rb, cb):
    nr, nc = R // tr, C // tc
    if kind == "col":
        return cb // nc, rb // nr, rb % nr, cb % nc
    return rb // nr, cb // nc, rb % nr, cb % nc


def _wspec(w, tr, tc, rb_fn, cb_fn):
    assert w.R % tr == 0 and w.C % tc == 0, (w.R, w.C, tr, tc)

    def imap(i, j, k):
        return _part_index(w.kind, w.R, w.C, tr, tc, rb_fn(i, j, k), cb_fn(i, j, k))

    return pl.BlockSpec((None, None, tr, tc), imap)


def _gspec(kind, R, C, tr, tc):
    assert R % tr == 0 and C % tc == 0, (R, C, tr, tc)

    def imap(i, j, k):
        part, half, rbi, cbi = _part_index(kind, R, C, tr, tc, i, j)
        return half, part, rbi, cbi

    return pl.BlockSpec((None, None, tr, tc), imap)


def _mm(name, grid, a_ops, b_ops, pairs, acc_shape, n_acc, extras, outs, epilogue):
    nk = grid[2]
    na, nb, ne, no = len(a_ops), len(b_ops), len(extras), len(outs)

    def body(*refs):
        a_refs = refs[:na]
        b_refs = refs[na:na + nb]
        e_refs = refs[na + nb:na + nb + ne]
        o_refs = refs[na + nb + ne:na + nb + ne + no]
        acc_refs = refs[na + nb + ne + no:]
        k = pl.program_id(2)

        def products():
            sums = [None] * n_acc
            for ai, bi, ci, dn in pairs:
                d = _dot(a_refs[ai][...], b_refs[bi][...], dn)
                sums[ci] = d if sums[ci] is None else sums[ci] + d
            return sums

        def finish(accs):
            res = epilogue(accs, [e[...] for e in e_refs])
            for o, r in zip(o_refs, res):
                o[...] = r.astype(o.dtype)

        if nk == 1:
            finish(products())
            return

        @pl.when(k == 0)
        def _():
            for acc, d in zip(acc_refs, products()):
                acc[...] = d

        @pl.when(jnp.logical_and(k > 0, k < nk - 1))
        def _():
            for acc, d in zip(acc_refs, products()):
                acc[...] += d

        @pl.when(k == nk - 1)
        def _():
            finish([acc[...] + d for acc, d in zip(acc_refs, products())])

    ops = list(a_ops) + list(b_ops) + list(extras)
    return pl.pallas_call(
        body, name=name, grid=grid,
        in_specs=[s for _, s in ops],
        out_specs=[s for _, s in outs],
        out_shape=[s for s, _ in outs],
        scratch_shapes=[pltpu.VMEM(acc_shape, F32) for _ in range(n_acc if nk > 1 else 0)],
        compiler_params=_cparams(("parallel", "parallel", "arbitrary")),
    )(*[a for a, _ in ops])


def _bs(shape, fn):
    return pl.BlockSpec(shape, fn)


def _tile_m(L):
    return min(L, 512)


def _mm_nn(name, x, ws, tk, tn, n_out, extras, outs_sd, epilogue, tm=None, cb_offsets=None, more=None):
    M, K = x.shape
    tm = min(M, tm or _tile_m(M))
    grid = (M // tm, n_out // tn, K // tk)
    a_ops = [(x, _bs((tm, tk), lambda i, j, k: (i, k)))]
    cb_offsets = cb_offsets or [0] * len(ws)
    b_ops = [(w.a, _wspec(w, tk, tn, lambda i, j, k: k, (lambda off: lambda i, j, k: j + off)(off)))
             for w, off in zip(ws, cb_offsets)]
    pairs = [(0, bi, bi, _NN) for bi in range(len(ws))]
    mn = _bs((tm, tn), lambda i, j, k: (i, j))
    ex = [(e, mn) for e in extras] + (more(tm) if more else [])
    outs = [(jax.ShapeDtypeStruct((M, n_out), dt), mn) for dt in outs_sd]
    return _mm(name, grid, a_ops, b_ops, pairs, (tm, tn), len(ws), ex, outs, epilogue)


def _mm_nt(name, xs, ws, tn, tk, extras, outs_sd, epilogue, tm=None):
    M, Nw = xs[0].shape
    Kw = ws[0].full_shape()[0]
    tm = min(M, tm or _tile_m(M))
    grid = (M // tm, Kw // tn, Nw // tk)
    a_ops = [(x, _bs((tm, tk), lambda i, j, k: (i, k))) for x in xs]
    b_ops = [(w.a, _wspec(w, tn, tk, lambda i, j, k: j, lambda i, j, k: k)) for w in ws]
    pairs = [(i, i, 0, _NT) for i in range(len(ws))]
    mn = _bs((tm, tn), lambda i, j, k: (i, j))
    ex = [(e, mn) for e in extras]
    outs = [(jax.ShapeDtypeStruct((M, Kw), dt), mn) for dt in outs_sd]
    return _mm(name, grid, a_ops, b_ops, pairs, (tm, tn), 1, ex, outs, epilogue)


def _mm_tn(name, x, dys, kind, R, C, tm, tn, tk=None):
    L, Kw = x.shape
    Nw = dys[0].shape[1]
    tk = tk or min(L, 1024)
    grid = (Kw // tm, Nw // tn, L // tk)
    a_ops = [(x, _bs((tk, tm), lambda i, j, k: (k, i)))]
    b_ops = [(dy, _bs((tk, tn), lambda i, j, k: (k, j))) for dy in dys]
    pairs = [(0, bi, bi, _TN) for bi in range(len(dys))]
    gs = _gspec(kind, R, C, tm, tn)
    outs = [(jax.ShapeDtypeStruct((2, 4, R, C), _WIRE), gs) for _ in dys]
    return _mm(name, grid, a_ops, b_ops, pairs, (tm, tn), len(dys), [], outs, lambda accs, ex: accs)


def _row_tile(L):
    return min(L, 256)


def _rowwise(name, body, ins, outs, L, acc_outs=()):
    tr = _row_tile(L)
    n_in, n_out = len(ins), len(outs)

    def kbody(*refs):
        i_refs, o_refs, a_refs = refs[:n_in], refs[n_in:n_in + n_out], refs[n_in + n_out:]
        res, sums = body(*[r[...] for r in i_refs])
        for o, r in zip(o_refs, res):
            o[...] = r.astype(o.dtype)
        if a_refs:
            @pl.when(pl.program_id(0) == 0)
            def _():
                for a in a_refs:
                    a[...] = jnp.zeros(a.shape, F32)
            for a, s in zip(a_refs, sums):
                a[...] += s

    in_specs = []
    for arr, kind in ins:
        if kind == "row":
            in_specs.append(pl.BlockSpec((tr, arr.shape[1]), lambda i: (i, 0)))
        else:
            in_specs.append(pl.BlockSpec(arr.shape, lambda i: (0, 0)))
    out_specs = [pl.BlockSpec((tr, c), lambda i: (i, 0)) for c, _ in outs]
    out_shape = [jax.ShapeDtypeStruct((L, c), dt) for c, dt in outs]
    out_specs += [pl.BlockSpec((1, c), lambda i: (0, 0)) for c in acc_outs]
    out_shape += [jax.ShapeDtypeStruct((1, c), F32) for c in acc_outs]
    return pl.pallas_call(
        kbody, name=name, grid=(L // tr,), in_specs=in_specs, out_specs=out_specs, out_shape=out_shape,
        compiler_params=_cparams(("arbitrary",)),
    )(*[a for a, _ in ins])


def _rms_fwd(name, x, g):
    def body(xv, gv):
        r = lax.rsqrt(jnp.mean(xv * xv, axis=-1, keepdims=True) + _EPS)
        return [xv * r * gv], []
    return _rowwise(name, body, [(x, "row"), (g, "vec")], [(x.shape[1], _MXU)], x.shape[0])[0]


def _rms_bwd(name, dh, x, g, dres):
    def body(dhv, xv, gv, dr):
        r = lax.rsqrt(jnp.mean(xv * xv, axis=-1, keepdims=True) + _EPS)
        xh = xv * r
        dxh = dhv * gv
        dx = dr + r * (dxh - xh * jnp.mean(dxh * xh, axis=-1, keepdims=True))
        return [dx, dx], [jnp.sum(dhv * xh, axis=0, keepdims=True)]
    D = x.shape[1]
    return _rowwise(name, body, [(dh, "row"), (x, "row"), (g, "vec"), (dres, "row")], [(D, F32), (D, _MXU)],
                    x.shape[0], [D])


def _loss_head(x, g, target):
    D = x.shape[1]

    def body(xv, gv, tv):
        r = lax.rsqrt(jnp.mean(xv * xv, axis=-1, keepdims=True) + _EPS)
        xh = xv * r
        e = xh * gv - tv
        dy = e * (1.0 / D)
        dxh = dy * gv
        dx = r * (dxh - xh * jnp.mean(dxh * xh, axis=-1, keepdims=True))
        row_loss = 0.5 * jnp.mean(e * e, axis=-1, keepdims=True)
        lsum = jnp.sum(row_loss, axis=0, keepdims=True) + jnp.zeros((1, _LANES), F32)
        return [dx], [jnp.sum(dy * xh, axis=0, keepdims=True), lsum]
    return _rowwise("loss_head", body, [(x, "row"), (g, "vec"), (target, "row")], [(D, F32)], x.shape[0], [D, _LANES])


def _ple_bwd_elem(dx, pp, gt):
    def body(dxv, ppv, gtv):
        gt32 = gtv.astype(F32)
        return [dxv * ppv * gt32 * (1.0 - gt32), dxv * gt32], []
    D = dx.shape[1]
    return _rowwise("ple_bwd_elem", body, [(dx, "row"), (pp, "row"), (gt, "row")], [(D, _MXU), (D, _MXU)], dx.shape[0])


def _glu_bwd_elem(dx, val, sg):
    def body(dxv, vv, sv):
        v32, s32 = vv.astype(F32), sv.astype(F32)
        return [jnp.concatenate([dxv * s32, dxv * v32 * s32 * (1.0 - s32)], axis=1)], []
    D = dx.shape[1]
    return _rowwise("glu_bwd_elem", body, [(dx, "row"), (val, "row"), (sg, "row")], [(2 * D, _MXU)], dx.shape[0])[0]


def _gm_common(z, ln_g, ln_b, wc_bf, bsT):
    W = z.shape[1] // 2
    zu, zv = z[:, :W], z[:, W:]
    u, v = _gelu(zu), _gelu(zv)
    mu = jnp.mean(v, axis=-1, keepdims=True)
    vc = v - mu
    rstd = lax.rsqrt(jnp.mean(vc * vc, axis=-1, keepdims=True) + _EPS)
    vh = vc * rstd
    vn = vh * ln_g + ln_b
    vnb = vn.astype(_MXU)
    svs = []
    for h in range(_GM_HEADS):
        sl = slice(h * _LANES, (h + 1) * _LANES)
        svs.append(_dot(wc_bf[h], vnb[:, sl], _NN) + bsT[:, h:h + 1])
    return zu, zv, u, vh, rstd, vnb, svs


def _causal(w):
    t = lax.broadcasted_iota(jnp.int32, w.shape, w.ndim - 2)
    s = lax.broadcasted_iota(jnp.int32, w.shape, w.ndim - 1)
    return jnp.where(s <= t, w, jnp.zeros_like(w))


def _gmlp_fwd(z, ln_g, ln_b, w_s, bsT):
    L, W2 = z.shape
    W = W2 // 2

    def body(z_ref, g_ref, b_ref, ws_ref, bs_ref, m_ref):
        wc = _causal(ws_ref[...]).astype(_MXU)
        _, _, u, _, _, _, svs = _gm_common(z_ref[...], g_ref[...], b_ref[...], wc, bs_ref[...])
        for h in range(_GM_HEADS):
            sl = slice(h * _LANES, (h + 1) * _LANES)
            m_ref[:, sl] = (u[:, sl] * svs[h]).astype(m_ref.dtype)

    return pl.pallas_call(
        body, name="gmlp_fwd", grid=(L // _GM_CHUNK,),
        in_specs=[pl.BlockSpec((_GM_CHUNK, W2), lambda n: (n, 0)),
                  pl.BlockSpec((1, W), lambda n: (0, 0)), pl.BlockSpec((1, W), lambda n: (0, 0)),
                  pl.BlockSpec(w_s.shape, lambda n: (0, 0, 0)), pl.BlockSpec(bsT.shape, lambda n: (0, 0))],
        out_specs=pl.BlockSpec((_GM_CHUNK, W), lambda n: (n, 0)),
        out_shape=jax.ShapeDtypeStruct((L, W), _MXU),
        compiler_params=_cparams(("arbitrary",)),
    )(z, ln_g, ln_b, w_s, bsT)


def _gmlp_bwd(z, dm, ln_g, ln_b, w_s, bsT):
    L, W2 = z.shape
    W = W2 // 2
    T = _GM_CHUNK

    def body(z_ref, dm_ref, g_ref, b_ref, ws_ref, bs_ref, dz_ref, dws_ref, dbs_ref, dg_ref, db_ref):
        @pl.when(pl.program_id(0) == 0)
        def _():
            dws_ref[...] = jnp.zeros(dws_ref.shape, F32)
            dbs_ref[...] = jnp.zeros(dbs_ref.shape, F32)
            dg_ref[...] = jnp.zeros(dg_ref.shape, F32)
            db_ref[...] = jnp.zeros(db_ref.shape, F32)

        wc = _causal(ws_ref[...]).astype(_MXU)
        ln_g_v = g_ref[...]
        zu, zv, u, vh, rstd, vnb, svs = _gm_common(z_ref[...], ln_g_v, b_ref[...], wc, bs_ref[...])
        dmv = dm_ref[...]
        lane = lax.broadcasted_iota(jnp.int32, (T, _LANES), 1)
        dbs = jnp.zeros((T, _LANES), F32)
        dvn_parts = []
        for h in range(_GM_HEADS):
            sl = slice(h * _LANES, (h + 1) * _LANES)
            dsv = dmv[:, sl] * u[:, sl]
            dz_ref[:, sl] = (dmv[:, sl] * svs[h] * _gelu_grad(zu[:, sl])).astype(dz_ref.dtype)
            dbs = dbs + jnp.where(lane == h, jnp.sum(dsv, axis=1, keepdims=True), 0.0)
            dsvb = dsv.astype(_MXU)
            dws_ref[h] += _dot(dsvb, vnb[:, sl], _NT)
            dvn_parts.append(_dot(wc[h], dsvb, _TN))
        dbs_ref[...] += dbs
        dvn = jnp.concatenate(dvn_parts, axis=1)
        dg_ref[...] += jnp.sum(dvn * vh, axis=0, keepdims=True)
        db_ref[...] += jnp.sum(dvn, axis=0, keepdims=True)
        dxh = dvn * ln_g_v
        dv = rstd * (dxh - jnp.mean(dxh, axis=-1, keepdims=True) - vh * jnp.mean(dxh * vh, axis=-1, keepdims=True))
        dz_ref[:, W:] = (dv * _gelu_grad(zv)).astype(dz_ref.dtype)

        @pl.when(pl.program_id(0) == pl.num_programs(0) - 1)
        def _():
            dws_ref[...] = _causal(dws_ref[...])

    return pl.pallas_call(
        body, name="gmlp_bwd", grid=(L // T,),
        in_specs=[pl.BlockSpec((T, W2), lambda n: (n, 0)), pl.BlockSpec((T, W), lambda n: (n, 0)),
                  pl.BlockSpec((1, W), lambda n: (0, 0)), pl.BlockSpec((1, W), lambda n: (0, 0)),
                  pl.BlockSpec(w_s.shape, lambda n: (0, 0, 0)), pl.BlockSpec(bsT.shape, lambda n: (0, 0))],
        out_specs=[pl.BlockSpec((T, W2), lambda n: (n, 0)),
                   pl.BlockSpec(w_s.shape, lambda n: (0, 0, 0)), pl.BlockSpec((T, _LANES), lambda n: (0, 0)),
                   pl.BlockSpec((1, W), lambda n: (0, 0)), pl.BlockSpec((1, W), lambda n: (0, 0))],
        out_shape=[jax.ShapeDtypeStruct((L, W2), _MXU), jax.ShapeDtypeStruct(w_s.shape, F32),
                   jax.ShapeDtypeStruct((T, _LANES), F32),
                   jax.ShapeDtypeStruct((1, W), F32), jax.ShapeDtypeStruct((1, W), F32)],
        compiler_params=_cparams(("arbitrary",)),
    )(z, dm, ln_g, ln_b, w_s, bsT)


def _s5_prep_math(a_re, a_im, log_dt):
    dt = jnp.exp(log_dt)
    xr, xi = a_re * dt, a_im * dt
    e = jnp.exp(xr)
    lbr, lbi = e * jnp.cos(xi), e * jnp.sin(xi)
    dn = a_re * a_re + a_im * a_im
    nr, ni = lbr - 1.0, lbi
    pr, pi = nr * a_re + ni * a_im, ni * a_re - nr * a_im
    return dt, lbr, lbi, dn, nr, ni, pr, pi


def _vm():
    return pl.BlockSpec(memory_space=pltpu.VMEM)


def _s5_prep(a_re, a_im, log_dt, b_re, b_im):
    def body(ar_ref, ai_ref, ld_ref, br_ref, bi_ref, lbr_ref, lbi_ref, Br_ref, Bi_ref):
        _, lbr, lbi, dn, _, _, pr, pi = _s5_prep_math(ar_ref[...], ai_ref[...], ld_ref[...])
        cr, ci = (pr / dn)[:, None, :], (pi / dn)[:, None, :]
        lbr_ref[...] = lbr
        lbi_ref[...] = lbi
        br, bi = br_ref[...], bi_ref[...]
        Br_ref[...] = cr * br - ci * bi
        Bi_ref[...] = cr * bi + ci * br

    sd = jax.ShapeDtypeStruct
    return pl.pallas_call(
        body, name="s5_prep", in_specs=[_vm()] * 5, out_specs=[_vm()] * 4,
        out_shape=[sd(a_re.shape, F32), sd(a_re.shape, F32), sd(b_re.shape, F32), sd(b_re.shape, F32)],
    )(a_re, a_im, log_dt, b_re, b_im)


def _s5_prep_bwd(a_re, a_im, log_dt, b_re, b_im, dlbr_s, dlbi_s, dBr, dBi):
    def body(ar_ref, ai_ref, ld_ref, br_ref, bi_ref, dlr_ref, dli_ref, dBr_ref, dBi_ref,
             dar_ref, dai_ref, dld_ref, dbr_ref, dbi_ref):
        a_re_v, a_im_v = ar_ref[...], ai_ref[...]
        dt, lbr, lbi, dn, nr, ni, pr, pi = _s5_prep_math(a_re_v, a_im_v, ld_ref[...])
        cr, ci = (pr / dn)[:, None, :], (pi / dn)[:, None, :]
        br, bi, dBr_v, dBi_v = br_ref[...], bi_ref[...], dBr_ref[...], dBi_ref[...]
        dbr_ref[...] = cr * dBr_v + ci * dBi_v
        dbi_ref[...] = cr * dBi_v - ci * dBr_v
        dcr = jnp.sum(br * dBr_v + bi * dBi_v, axis=1)
        dci = jnp.sum(br * dBi_v - bi * dBr_v, axis=1)
        dpr, dpi = dcr / dn, dci / dn
        ddn = -(dcr * pr + dci * pi) / (dn * dn)
        dnr = dpr * a_re_v - dpi * a_im_v
        dni = dpr * a_im_v + dpi * a_re_v
        dlbr = dlr_ref[...] + dnr
        dlbi = dli_ref[...] + dni
        dxr = dlbr * lbr + dlbi * lbi
        dxi = dlbi * lbr - dlbr * lbi
        dar_ref[...] = dpr * nr + dpi * ni + 2.0 * ddn * a_re_v + dxr * dt
        dai_ref[...] = dpr * ni - dpi * nr + 2.0 * ddn * a_im_v + dxi * dt
        dld_ref[...] = jnp.sum(dxr * a_re_v + dxi * a_im_v, axis=0, keepdims=True) * dt

    sd = jax.ShapeDtypeStruct
    return pl.pallas_call(
        body, name="s5_prep_bwd", in_specs=[_vm()] * 9, out_specs=[_vm()] * 5,
        out_shape=[sd(a_re.shape, F32), sd(a_re.shape, F32), sd(log_dt.shape, F32),
                   sd(b_re.shape, F32), sd(b_re.shape, F32)],
    )(a_re, a_im, log_dt, b_re, b_im, dlbr_s, dlbi_s, dBr, dBi)


def _shift_rows(v, down):
    n = v.shape[0]
    rolled = pltpu.roll(v, 1 if down else n - 1, 0)
    row = lax.broadcasted_iota(jnp.int32, v.shape, 0)
    return jnp.where(row == (0 if down else n - 1), 0.0, rolled)


def _cmul(ar, ai, br, bi):
    return ar * br - ai * bi, ar * bi + ai * br


_SEG = 8
_UNROLL = 8


def _seg_rows(k):
    if isinstance(k, int):
        return pl.ds(k * _SEG, _SEG)
    return pl.ds(pl.multiple_of(k * _SEG, _SEG), _SEG)


def _unrolled(n, step, init):
    main = n // _UNROLL

    def trip(kk, s):
        for uu in range(_UNROLL):
            s = step(kk * _UNROLL + uu, s)
        return s

    s = lax.fori_loop(0, main, trip, init)
    for r in range(main * _UNROLL, n):
        s = step(r, s)
    return s


def _interleave(src_ref, dst_ref, nk):
    def step(k, carry):
        dst_ref[_seg_rows(k), :] = src_ref[pl.ds(k, _SEG, stride=nk), :]
        return carry
    _unrolled(nk, step, 0)


def _deinterleave(src_ref, dst_ref, nk):
    def step(k, carry):
        dst_ref[pl.ds(k, _SEG, stride=nk), :] = src_ref[_seg_rows(k), :]
        return carry
    _unrolled(nk, step, 0)


def _segment_inits(er, ei, ar, ai, nk, down):
    pr, pi = ar, ai
    for _ in range(int(math.log2(nk))):
        pr, pi = _cmul(pr, pi, pr, pi)
    fr, fi = er, ei
    for _ in range(_SEG - 1):
        sr, si = _shift_rows(fr, down), _shift_rows(fi, down)
        mr, mi = _cmul(pr, pi, sr, si)
        fr, fi = er + mr, ei + mi
    return _shift_rows(fr, down), _shift_rows(fi, down)


def _scan_states(x_re, x_im, ar, ai, nk):
    lanes = ar.shape[1]

    def step(k, s):
        rows = _seg_rows(k)
        mr, mi = _cmul(ar, ai, s[0], s[1])
        return mr + x_re[rows, :], mi + x_im[rows, :]

    zero = jnp.zeros((_SEG, lanes), F32)
    er, ei = _unrolled(nk, step, (zero, zero))
    ir, ii = _segment_inits(er, ei, ar, ai, nk, True)

    def step2(k, s):
        rows = _seg_rows(k)
        mr, mi = _cmul(ar, ai, s[0], s[1])
        nr, ni = mr + x_re[rows, :], mi + x_im[rows, :]
        x_re[rows, :] = nr
        x_im[rows, :] = ni
        return nr, ni

    _unrolled(nk, step2, (ir, ii))


def _s5_tile_fwd(u, bd_re, bd_im, cd_re, cd_im, ar, ai, d, s_re, s_im, nk):
    s_re[...] = _dot(u, bd_re, _NN)
    s_im[...] = _dot(u, bd_im, _NN)
    _scan_states(s_re, s_im, ar, ai, nk)
    return _dot(s_re[...], cd_re, _NN) - _dot(s_im[...], cd_im, _NN) + d * u


def _s5_specs(L, T):
    lanes = _S5_GT * _S5_P
    u_spec = pl.BlockSpec((L, _LANES), lambda t: (0, t))
    bd_spec = pl.BlockSpec((None, _S5_GT, _S5_C, _S5_P), lambda t: (t, 0, 0, 0))
    cd_spec = pl.BlockSpec((None, _S5_GT, _S5_P, _S5_C), lambda t: (t, 0, 0, 0))
    lam_spec = pl.BlockSpec((None, 1, lanes), lambda t: (t, 0, 0))
    d_spec = pl.BlockSpec((1, _LANES), lambda t: (0, t))
    return lanes, u_spec, bd_spec, cd_spec, lam_spec, d_spec


def _fill_block_diag(dst_ref, blocks_ref):
    _, a, b = blocks_ref.shape
    dst_ref[...] = jnp.zeros(dst_ref.shape, F32)
    for g in range(_S5_GT):
        dst_ref[g * a:(g + 1) * a, g * b:(g + 1) * b] = blocks_ref[g]


def _take_block_diag(dst_ref, v):
    _, a, b = dst_ref.shape
    for g in range(_S5_GT):
        dst_ref[g] = v[g * a:(g + 1) * a, g * b:(g + 1) * b]


def _s5_dense(bdr, bdi, cdr, cdi, dense):
    for src, dst in zip((bdr, bdi, cdr, cdi), dense):
        _fill_block_diag(dst, src)
    return [dst[...] for dst in dense]


def _s5_dense_scratch(lanes):
    return [pltpu.VMEM((_LANES, lanes), F32), pltpu.VMEM((_LANES, lanes), F32),
            pltpu.VMEM((lanes, _LANES), F32), pltpu.VMEM((lanes, _LANES), F32)]


def _s5_fwd(u, bd_re, bd_im, cd_re, cd_im, lam_re, lam_im, d):
    L, Wd = u.shape
    T = Wd // _LANES
    nk = L // _SEG
    lanes, u_spec, bd_spec, cd_spec, lam_spec, d_spec = _s5_specs(L, T)
    s_spec = pl.BlockSpec((L, lanes), lambda t: (0, t))

    def body(u_ref, bdr, bdi, cdr, cdi, lr, li, d_ref, g_ref, s_re, s_im, up, tmp, *dense):
        ar = jnp.broadcast_to(lr[...], (_SEG, lanes))
        ai = jnp.broadcast_to(li[...], (_SEG, lanes))
        bd_re_v, bd_im_v, cd_re_v, cd_im_v = _s5_dense(bdr, bdi, cdr, cdi, dense)
        _interleave(u_ref, up, nk)
        y = _s5_tile_fwd(up[...], bd_re_v, bd_im_v, cd_re_v, cd_im_v, ar, ai, d_ref[...], s_re, s_im, nk)
        up[...] = _gelu(y)
        _deinterleave(up, tmp, nk)
        g_ref[...] = tmp[...].astype(g_ref.dtype)

    return pl.pallas_call(
        body, name="s5_fwd", grid=(T,),
        in_specs=[u_spec, bd_spec, bd_spec, cd_spec, cd_spec, lam_spec, lam_spec, d_spec],
        out_specs=[u_spec, s_spec, s_spec],
        out_shape=[jax.ShapeDtypeStruct((L, Wd), _MXU), jax.ShapeDtypeStruct((L, T * lanes), F32),
                   jax.ShapeDtypeStruct((L, T * lanes), F32)],
        scratch_shapes=[pltpu.VMEM((L, _LANES), F32) for _ in range(2)] + _s5_dense_scratch(lanes),
        compiler_params=_cparams(("arbitrary",)),
    )(u, bd_re, bd_im, cd_re, cd_im, lam_re, lam_im, d)


def _s5_bwd(u, dg, states_re, states_im, bd_re, bd_im, cd_re, cd_im, lam_re, lam_im, d):
    L, Wd = u.shape
    T = Wd // _LANES
    nk = L // _SEG
    lanes, u_spec, bd_spec, cd_spec, lam_spec, d_spec = _s5_specs(L, T)
    s_spec = pl.BlockSpec((L, lanes), lambda t: (0, t))

    def body(u_ref, dg_ref, s_re, s_im, bdr, bdi, cdr, cdi, lr, li, d_ref,
             du_ref, dbdr, dbdi, dcdr, dcdi, dlr, dli, dd_ref, g_re, g_im, up, dgp, tmp, *dense):
        ar = jnp.broadcast_to(lr[...], (_SEG, lanes))
        ai = jnp.broadcast_to(li[...], (_SEG, lanes))
        bd_re_v, bd_im_v, cd_re_v, cd_im_v = _s5_dense(bdr, bdi, cdr, cdi, dense)
        _interleave(u_ref, up, nk)
        _interleave(dg_ref, dgp, nk)
        uv, dv = up[...], d_ref[...]
        y = _dot(s_re[...], cd_re_v, _NN) - _dot(s_im[...], cd_im_v, _NN) + dv * uv
        dy = dgp[...] * _gelu_grad(y)
        dd_ref[...] = jnp.sum(dy * uv, axis=0, keepdims=True)
        dyb = dy.astype(_MXU)
        _take_block_diag(dcdr, _dot(dyb, s_re[...], _TN))
        _take_block_diag(dcdi, -_dot(dyb, s_im[...], _TN))
        g_re[...] = _dot(dyb, cd_re_v, _NT)
        g_im[...] = -_dot(dyb, cd_im_v, _NT)

        nai = -ai

        def step(j, s):
            rows = _seg_rows(nk - 1 - j)
            mr, mi = _cmul(ar, nai, s[0], s[1])
            return mr + g_re[rows, :], mi + g_im[rows, :]

        zero = jnp.zeros((_SEG, lanes), F32)
        er, ei = _unrolled(nk, step, (zero, zero))
        ir, ii = _segment_inits(er, ei, ar, nai, nk, False)

        def acc_lam(gr, gi, pr, pi, acc):
            return acc[0] + gr * pr + gi * pi, acc[1] + gi * pr - gr * pi

        def step2(j, carry):
            s, acc = carry
            k = nk - 1 - j
            rows = _seg_rows(k)
            mr, mi = _cmul(ar, nai, s[0], s[1])
            nr, ni = mr + g_re[rows, :], mi + g_im[rows, :]
            g_re[rows, :] = nr
            g_im[rows, :] = ni
            prev = _seg_rows(k - 1)
            return (nr, ni), acc_lam(nr, ni, s_re[prev, :], s_im[prev, :], acc)

        (g0r, g0i), acc = _unrolled(nk - 1, step2, ((ir, ii), (zero, zero)))
        first = _seg_rows(0)
        mr, mi = _cmul(ar, nai, g0r, g0i)
        nr, ni = mr + g_re[first, :], mi + g_im[first, :]
        g_re[first, :] = nr
        g_im[first, :] = ni
        last = _seg_rows(nk - 1)
        acc = acc_lam(nr, ni, _shift_rows(s_re[last, :], True), _shift_rows(s_im[last, :], True), acc)
        dlr[...] = jnp.sum(acc[0], axis=0, keepdims=True)
        dli[...] = jnp.sum(acc[1], axis=0, keepdims=True)

        gtr, gti = g_re[...].astype(_MXU), g_im[...].astype(_MXU)
        ub = uv.astype(_MXU)
        _take_block_diag(dbdr, _dot(ub, gtr, _TN))
        _take_block_diag(dbdi, _dot(ub, gti, _TN))
        dgp[...] = _dot(gtr, bd_re_v, _NT) + _dot(gti, bd_im_v, _NT) + dy * dv
        _deinterleave(dgp, tmp, nk)
        du_ref[...] = tmp[...].astype(du_ref.dtype)

    sd = jax.ShapeDtypeStruct
    big = sd((T, _S5_GT, _S5_C, _S5_P), F32)
    return pl.pallas_call(
        body, name="s5_bwd", grid=(T,),
        in_specs=[u_spec, u_spec, s_spec, s_spec, bd_spec, bd_spec, cd_spec, cd_spec, lam_spec, lam_spec, d_spec],
        out_specs=[u_spec, bd_spec, bd_spec, bd_spec, bd_spec, lam_spec, lam_spec, d_spec],
        out_shape=[sd((L, Wd), _MXU), big, big, big, big, sd((T, 1, lanes), F32), sd((T, 1, lanes), F32),
                   sd((1, Wd), F32)],
        scratch_shapes=[pltpu.VMEM((L, lanes), F32) for _ in range(2)]
        + [pltpu.VMEM((L, _LANES), F32) for _ in range(3)] + _s5_dense_scratch(lanes),
        compiler_params=_cparams(("arbitrary",)),
    )(u, dg, states_re, states_im, bd_re, bd_im, cd_re, cd_im, lam_re, lam_im, d)


def _half_tile(R, few_arrays=False):
    for t in ((512, 704, 128) if few_arrays else (256, 352, 128)):
        if R % t == 0:
            return t
    raise ValueError(R)


def _cast_shard(name, w, layer, kind, R, C):
    tr = _half_tile(R, True)
    nr = R // tr

    def body(w_ref, o_ref):
        o_ref[...] = w_ref[...].astype(o_ref.dtype)

    if kind == "col":
        in_map = lambda h, i: (layer, h * nr + i, 0)
    else:
        in_map = lambda h, i: (layer, i, h)
    return pl.pallas_call(
        body, name=name, grid=(2, nr), in_specs=[pl.BlockSpec((None, tr, C), in_map)],
        out_specs=pl.BlockSpec((None, tr, C), lambda h, i: (h, i, 0)),
        out_shape=jax.ShapeDtypeStruct((2, R, C), _WIRE),
        compiler_params=_cparams(("arbitrary", "arbitrary")),
    )(w)


def _adam_math(w, g, m, v):
    m2 = _B1 * m + (1.0 - _B1) * g
    v2 = _B2 * v + (1.0 - _B2) * (g * g)
    m_hat = m2 / (1.0 - _B1 ** _STEP)
    v_hat = v2 / (1.0 - _B2 ** _STEP)
    delta = -_LR * (m_hat / (jnp.sqrt(v_hat) + _AEPS) + _WD * w)
    return delta, m2, v2


def _adamw_big(name, w, m, v, layer, pair, kind, R, C, c, after, prev):
    tr = _half_tile(R, few_arrays=C <= 1024 and R % 512 == 0)
    nr = R // tr

    def body(c_ref, w_ref, m_ref, v_ref, own_ref, other_ref, *rest):
        go_ref, d_ref, mo_ref, vo_ref = rest[-4:]
        g = jnp.where(pl.program_id(0) == c_ref[0], own_ref[...], other_ref[...])
        delta, m2, v2 = _adam_math(w_ref[...], g, m_ref[...], v_ref[...])
        go_ref[...] = g
        d_ref[...] = delta
        mo_ref[...] = m2
        vo_ref[...] = v2

    if kind == "col":
        nat = pl.BlockSpec((None, tr, C), lambda h, i, c_ref: (layer, h * nr + i, 0))
    else:
        nat = pl.BlockSpec((None, tr, C), lambda h, i, c_ref: (layer, i, h))

    def gspec(own):
        return pl.BlockSpec((tr, C), lambda h, i, c_ref: (jnp.where((h == c_ref[0]) == own, i, 0), 0))

    carried = list(prev) if prev is not None else []
    gs = pltpu.PrefetchScalarGridSpec(
        num_scalar_prefetch=1, grid=(2, nr),
        in_specs=[nat, nat, nat, gspec(True), gspec(False), _any()] + [_any()] * len(carried),
        out_specs=[nat] * 4)
    sd = jax.ShapeDtypeStruct(w.shape, F32)
    return pl.pallas_call(
        body, name=name, grid_spec=gs, out_shape=[sd] * 4,
        input_output_aliases={7 + k: k for k in range(len(carried))},
        compiler_params=_cparams(("arbitrary", "arbitrary")),
    )(c, w, m, v, pair[0], pair[1], after, *carried)


def _adamw_small(ws, gs, ms, vs):
    n = len(ws)

    def body(*refs):
        for i in range(n):
            w_ref, g_ref, m_ref, v_ref, d_ref, mo_ref, vo_ref = refs[i::n]
            delta, m2, v2 = _adam_math(w_ref[...], g_ref[...], m_ref[...], v_ref[...])
            d_ref[...] = delta
            mo_ref[...] = m2
            vo_ref[...] = v2

    sds = [jax.ShapeDtypeStruct(w.shape, F32) for w in ws]
    outs = pl.pallas_call(
        body, name="adamw_small", in_specs=[_vm()] * (4 * n), out_specs=[_vm()] * (3 * n), out_shape=sds * 3,
        compiler_params=pltpu.CompilerParams(vmem_limit_bytes=_VMEM_LIMIT),
    )(*ws, *gs, *ms, *vs)
    return outs[:n], outs[n:2 * n], outs[2 * n:]


def _place():
    x, y, c = lax.axis_index("x"), lax.axis_index("y"), lax.axis_index("c")
    chips = [(1 - x, y), (x, 1 - y), (1 - x, 1 - y)]
    return x, y, c, 2 * x + y, chips


def _any():
    return pl.BlockSpec(memory_space=pl.ANY)


def _remote(src, dst, ssem, rsem, dev):
    return pltpu.make_async_remote_copy(src_ref=src, dst_ref=dst, send_sem=ssem, recv_sem=rsem,
                                        device_id=dev, device_id_type=_MESH)


def _allgather(name, shards):
    n = len(shards)

    def body(*refs):
        s_refs, g_refs = refs[:n], refs[n:2 * n]
        send0, recv0, send1, recv1, send2, recv2 = refs[2 * n:]
        x, y, c, q, _ = _place()
        sib, xn, yn = (x, y, 1 - c), (1 - x, y, c), (x, 1 - y, c)
        qx, qy, qd = 2 * (1 - x) + y, 2 * x + (1 - y), 2 * (1 - x) + (1 - y)
        _handshake([sib, xn, yn])
        own = [_remote(s_refs[a], g_refs[a].at[q], send0.at[a], recv0.at[a], sib) for a in range(n)]

        def pieces(a):
            g, half = g_refs[a], s_refs[a].shape[1] // 2
            return [g.at[qx, c], g.at[qy, c], g.at[qd, c, pl.ds(0, half)], g.at[qd, c, pl.ds(half, half)]]

        def relayed(a):
            g, half = g_refs[a], s_refs[a].shape[1] // 2
            return [(g.at[qx, c, pl.ds(0, half)], yn), (g.at[qy, c, pl.ds(half, half)], xn)]

        first = []
        for a in range(n):
            first.append(_remote(s_refs[a].at[c], g_refs[a].at[q, c], send1.at[4 * a], recv1.at[4 * a], xn))
            first.append(_remote(s_refs[a].at[c], g_refs[a].at[q, c], send1.at[4 * a + 1], recv1.at[4 * a + 1], yn))
        for cp in first + own:
            cp.start()
        later = []
        for a in range(n):
            land = pieces(a)
            for j in range(4):
                k = 4 * a + j
                _remote(land[j], land[j], send1.at[k], recv1.at[k], xn).wait_recv()
                if j < 2:
                    src, to = relayed(a)[j]
                    cp = _remote(src, src, send1.at[k + 2], recv1.at[k + 2], to)
                    cp.start()
                    later.append(cp)
                cp = _remote(land[j], land[j], send2.at[k], recv2.at[k], sib)
                cp.start()
                later.append(cp)
        for a in range(n):
            g, half = g_refs[a], s_refs[a].shape[1] // 2
            theirs = [g.at[qx, 1 - c], g.at[qy, 1 - c], g.at[qd, 1 - c, pl.ds(0, half)],
                      g.at[qd, 1 - c, pl.ds(half, half)]]
            for j in range(4):
                _remote(theirs[j], theirs[j], send2.at[4 * a + j], recv2.at[4 * a + j], sib).wait_recv()
        for cp in own:
            cp.wait()
        for cp in first + later:
            cp.wait_send()

    return _sequencer(name, _ID_GATHER, body, shards,
                      [jax.ShapeDtypeStruct((4,) + s.shape, s.dtype) for s in shards], [n, n] + [4 * n] * 4)


def _handshake(peers):
    barrier = pltpu.get_barrier_semaphore()
    for peer in peers:
        pl.semaphore_signal(barrier, inc=1, device_id=peer, device_id_type=_MESH)
    pl.semaphore_wait(barrier, len(peers))


_ID_SIBLING, _ID_CHIPS, _ID_GATHER, _ID_ALL = 1, 2, 3, 4


def _sequencer(name, collective_id, body, ins, out_types, sem_counts):
    mesh = plsc.ScalarSubcoreMesh(axis_name="seq", num_cores=1)
    moved = sum(math.prod(o.shape) * jnp.dtype(o.dtype).itemsize for o in out_types)
    return pl.kernel(
        body, name=name, out_type=out_types, mesh=mesh,
        scratch_types=[pltpu.SemaphoreType.DMA((k,)) for k in sem_counts],
        compiler_params=pltpu.CompilerParams(collective_id=collective_id),
        cost_estimate=pl.CostEstimate(flops=0, transcendentals=0, bytes_accessed=2 * moved,
                                      remote_bytes_transferred=moved),
    )(*ins)


def _swap_halves(name, grads):
    n = len(grads)

    def body(*refs):
        g_refs, t_refs = refs[:n], refs[n:2 * n]
        send, recv = refs[2 * n:]
        x, y, c, _, _ = _place()
        _handshake([(x, y, 1 - c)])
        cps = [_remote(g_refs[a].at[1 - c], t_refs[a], send.at[a], recv.at[a], (x, y, 1 - c)) for a in range(n)]
        for cp in cps:
            cp.start()
        for cp in cps:
            cp.wait()

    return _sequencer(name, _ID_SIBLING, body, grads,
                      [jax.ShapeDtypeStruct(g.shape[1:], g.dtype) for g in grads], [n, n])


def _chip_sum(name, g, t, after, core):
    _, _, R, C = g.shape
    tr = _half_tile(R, True)

    def body(c_ref, g_ref, t_ref, after_ref, o_ref):
        o_ref[...] = (g_ref[...].astype(F32) + t_ref[...].astype(F32)).astype(o_ref.dtype)

    gs = pltpu.PrefetchScalarGridSpec(
        num_scalar_prefetch=1, grid=(4, R // tr),
        in_specs=[pl.BlockSpec((None, None, tr, C), lambda r, i, c_ref: (c_ref[0], r, i, 0)),
                  pl.BlockSpec((None, tr, C), lambda r, i, c_ref: (r, i, 0)), _any()],
        out_specs=pl.BlockSpec((None, tr, C), lambda r, i, c_ref: (r, i, 0)))
    return pl.pallas_call(
        body, name=name, grid_spec=gs, out_shape=jax.ShapeDtypeStruct((4, R, C), _WIRE),
        compiler_params=_cparams(("arbitrary", "arbitrary")),
    )(core, g, t, after)


def _scatter_parts(name, parts):
    n = len(parts)

    def body(*refs):
        p_refs, t_refs = refs[:n], refs[n:2 * n]
        send, recv = refs[2 * n:]
        x, y, c, q, chips = _place()
        _handshake([(rx, ry, c) for rx, ry in chips])
        cps = []
        for a in range(n):
            for j, (rx, ry) in enumerate(chips):
                k = 3 * a + j
                cps.append(_remote(p_refs[a].at[2 * rx + ry], t_refs[a].at[q], send.at[k], recv.at[k], (rx, ry, c)))
        for cp in cps:
            cp.start()
        for a in range(n):
            for j, (rx, ry) in enumerate(chips):
                k = 3 * a + j
                land = t_refs[a].at[2 * rx + ry]
                _remote(land, land, send.at[k], recv.at[k], (rx, ry, c)).wait_recv()
        for cp in cps:
            cp.wait_send()

    return _sequencer(name, _ID_CHIPS, body, parts,
                      [jax.ShapeDtypeStruct(p.shape, p.dtype) for p in parts], [3 * n, 3 * n])


def _sum_parts(name, p, t, where, after):
    _, R, C = t.shape
    tr = _half_tile(R, True)

    def body(w_ref, p_ref, t0_ref, t1_ref, t2_ref, after_ref, o_ref):
        o_ref[...] = (p_ref[...].astype(F32) + t0_ref[...].astype(F32)
                      + t1_ref[...].astype(F32) + t2_ref[...].astype(F32))

    def part(slot):
        return pl.BlockSpec((None, tr, C), lambda i, w_ref: (w_ref[slot], i, 0))

    gs = pltpu.PrefetchScalarGridSpec(
        num_scalar_prefetch=1, grid=(R // tr,), in_specs=[part(0), part(1), part(2), part(3), _any()],
        out_specs=pl.BlockSpec((tr, C), lambda i, w_ref: (i, 0)))
    return pl.pallas_call(
        body, name=name, grid_spec=gs, out_shape=jax.ShapeDtypeStruct((R, C), F32),
        compiler_params=_cparams(("arbitrary",)),
    )(where, p, t, t, t, after)


def _send_halves(name, halves):
    n = len(halves)

    def body(*refs):
        h_refs, o_refs = refs[:n], refs[n:2 * n]
        send, recv = refs[2 * n:]
        x, y, c, _, _ = _place()
        _handshake([(x, y, 1 - c)])
        cps = [_remote(h_refs[a], o_refs[a], send.at[a], recv.at[a], (x, y, 1 - c)) for a in range(n)]
        for cp in cps:
            cp.start()
        for cp in cps:
            cp.wait()

    return _sequencer(name, _ID_SIBLING, body, halves,
                      [jax.ShapeDtypeStruct(h.shape, h.dtype) for h in halves], [n, n])


class _Order:
    def __init__(self):
        self.tok = None
        x, y, c, q, chips = _place()
        self.core = c.astype(jnp.int32).reshape(1)
        self.where = jnp.stack([q] + [2 * rx + ry for rx, ry in chips]).astype(jnp.int32)

    def tie(self, x):
        return x if self.tok is None else lax.optimization_barrier((x, self.tok))[0]

    def done(self, outs):
        self.tok = outs[0]
        return outs


class _Reduction:
    def __init__(self, tag, grads):
        self.tag, self.grads = tag, grads
        self.swapped = _swap_halves(f"rs_swap_{tag}", grads)

    def scatter(self, seq):
        self.parts = [seq.done([_chip_sum(f"rs_chipsum_{self.tag}_{a}", g, t, seq.tok, seq.core)])[0]
                      for a, (g, t) in enumerate(zip(self.grads, self.swapped))]
        self.landed = _scatter_parts(f"rs_scatter_{self.tag}", self.parts)

    def finish(self, seq):
        halves = [seq.done([_sum_parts(f"rs_sum_{self.tag}_{a}", p, t, seq.where, seq.tok)])[0]
                  for a, (p, t) in enumerate(zip(self.parts, self.landed))]
        return list(zip(halves, _send_halves(f"rs_join_{self.tag}", halves)))


def _allreduce_small(name, v):
    _, R, _ = v.shape

    def body(v_ref, o_ref, land, acc, send1, recv1, send2, recv2):
        x, y, c = lax.axis_index("x"), lax.axis_index("y"), lax.axis_index("c")
        me = 4 * x + 2 * y + c
        peers = []
        for k in range(1, 8):
            dx, dy, dc = (k >> 2) & 1, (k >> 1) & 1, k & 1
            px, py, pc = (1 - x if dx else x), (1 - y if dy else y), (1 - c if dc else c)
            peers.append((k, (px, py, pc), 4 * px + 2 * py + pc))
        land[me] = v_ref[me]
        out1 = [_remote(v_ref.at[pid], land.at[me], send1.at[k], recv1.at[k], dev) for k, dev, pid in peers]
        for cp in out1:
            cp.start()
        for k, dev, pid in peers:
            _remote(land.at[pid], land.at[pid], send1.at[k], recv1.at[k], dev).wait_recv()
        total = land[0]
        for j in range(1, 8):
            total = total + land[j]
        acc[...] = total
        o_ref[me] = total
        out2 = [_remote(acc, o_ref.at[me], send2.at[k], recv2.at[k], dev) for k, dev, pid in peers]
        for cp in out2:
            cp.start()
        for k, dev, pid in peers:
            _remote(o_ref.at[pid], o_ref.at[pid], send2.at[k], recv2.at[k], dev).wait_recv()
        for cp in out1 + out2:
            cp.wait_send()

    return pl.pallas_call(
        body, name=name, in_specs=[_vm()], out_specs=_vm(),
        out_shape=jax.ShapeDtypeStruct(v.shape, F32),
        scratch_shapes=[pltpu.VMEM(v.shape, F32), pltpu.VMEM((R, _LANES), F32)]
        + [pltpu.SemaphoreType.DMA((8,)) for _ in range(4)],
        compiler_params=pltpu.CompilerParams(vmem_limit_bytes=_VMEM_LIMIT),
    )(v)


def _all_peers():
    x, y, c = lax.axis_index("x"), lax.axis_index("y"), lax.axis_index("c")
    peers = []
    for k in range(1, 8):
        px, py, pc = (1 - x if k & 4 else x), (1 - y if k & 2 else y), (1 - c if k & 1 else c)
        peers.append((k, (px, py, pc), 4 * px + 2 * py + pc))
    return 4 * x + 2 * y + c, peers


def _exchange_slices(name, v):
    def body(v_ref, land, send, recv):
        me, peers = _all_peers()
        _handshake([dev for _, dev, _ in peers])
        cps = [_remote(v_ref.at[pid], land.at[me], send.at[k], recv.at[k], dev) for k, dev, pid in peers]
        for cp in cps:
            cp.start()
        for k, dev, pid in peers:
            _remote(land.at[pid], land.at[pid], send.at[k], recv.at[k], dev).wait_recv()
        for cp in cps:
            cp.wait_send()

    return _sequencer(name, _ID_ALL, body, [v], [jax.ShapeDtypeStruct(v.shape, v.dtype)], [8, 8])[0]


def _sum_slices(name, v, landed, after):
    _, R, _ = v.shape

    def body(v_ref, land_ref, after_ref, o_ref):
        me, peers = _all_peers()
        acc = v_ref[me]
        for _, _, pid in peers:
            acc = acc + land_ref[pid]
        o_ref[...] = acc

    return pl.pallas_call(
        body, name=name, in_specs=[_vm(), _vm(), _any()], out_specs=_vm(),
        out_shape=jax.ShapeDtypeStruct((R, _LANES), F32),
        compiler_params=pltpu.CompilerParams(vmem_limit_bytes=_VMEM_LIMIT),
    )(v, landed, after)


def _broadcast_slices(name, s):
    def body(s_ref, out, send, recv):
        me, peers = _all_peers()
        _handshake([dev for _, dev, _ in peers])
        cps = [_remote(s_ref, out.at[me], send.at[k], recv.at[k], dev) for k, dev, pid in peers]
        for cp in cps:
            cp.start()
        for k, dev, pid in peers:
            _remote(out.at[pid], out.at[pid], send.at[k], recv.at[k], dev).wait_recv()
        for cp in cps:
            cp.wait_send()

    return _sequencer(name, _ID_ALL, body, [s], [jax.ShapeDtypeStruct((8,) + s.shape, s.dtype)], [8, 8])[0]


_WEIGHT_NAMES = ['norm_mix', 'norm_ffn', 'norm_ple', 'norm_final', 'gm_w_in', 'gm_ln_g', 'gm_ln_b', 'gm_w_s',
                 'gm_b_s', 'gm_w_out', 's5_w_in', 's5_a_re', 's5_a_im', 's5_log_dt', 's5_b_re', 's5_b_im',
                 's5_c_re', 's5_c_im', 's5_d', 's5_w_out', 'ffn_w1', 'ffn_w3', 'ffn_w2', 'ple_w_gate', 'ple_w_proj']
_BIG = {'gm_w_in': 'col', 'gm_w_out': 'row', 's5_w_in': 'row', 's5_w_out': 'col', 'ffn_w1': 'col',
        'ffn_w3': 'col', 'ffn_w2': 'row', 'ple_w_gate': 'row', 'ple_w_proj': 'col'}


_VIEW = {'s5_a_re': (0, 2, 1), 's5_a_im': (0, 2, 1), 's5_b_re': (0, 2, 3, 1), 's5_b_im': (0, 2, 3, 1),
         's5_c_re': (0, 2, 3, 1), 's5_c_im': (0, 2, 3, 1)}


def _to_view(name, a):
    return jnp.transpose(a, _VIEW[name]) if name in _VIEW else a


def _from_view(name, a):
    if name not in _VIEW:
        return a
    perm = _VIEW[name]
    return jnp.transpose(a, [perm.index(i) for i in range(len(perm))])


def _rc(kind, shard_shape):
    rows, cols = shard_shape[-2:]
    return (rows // 2, cols) if kind == "col" else (rows, cols // 2)


def _pack(vecs, rows_multiple):
    flat = jnp.concatenate([a.reshape(-1).astype(F32) for a in vecs])
    unit = rows_multiple * _LANES
    pad = (-flat.shape[0]) % unit
    return jnp.pad(flat, (0, pad)).reshape(-1, _LANES)


def _unpack(buf, shapes):
    flat = buf.reshape(-1)
    out, off = [], 0
    for s in shapes:
        n = math.prod(s)
        out.append(flat[off:off + n].reshape(s))
        off += n
    return out


def _ident(accs, ex):
    return accs


def _add_resid(accs, ex):
    return [accs[0] + ex[0]]


def _swiglu_epi(accs, ex):
    a, b = accs
    return [a, b, a * _sig(a) * b]


def _swiglu_bwd_epi(accs, ex):
    df = accs[0]
    a, b = ex[0].astype(F32), ex[1].astype(F32)
    sa = _sig(a)
    return [df * b * (sa * (1.0 + a * (1.0 - sa))), df * (a * sa)]


def _ple_epi(accs, ex):
    xin, pv = ex[0], ex[1]
    kh = pv.shape[1] // 2
    pp = jnp.concatenate([_dot(pv[:, :kh], ex[2 + 2 * part], _NN) + _dot(pv[:, kh:], ex[3 + 2 * part], _NN)
                          for part in (0, 1)], axis=1)
    gt = _sig(accs[0])
    return [xin + gt * pp, gt, pp]


def _glu_epi(accs, ex):
    val, sg = accs[0], _sig(accs[1])
    return [ex[0] + val * sg, val, sg]


def kernel(x, p, norm_mix, norm_ffn, norm_ple, norm_final, gm_w_in, gm_ln_g, gm_ln_b, gm_w_s, gm_b_s, gm_w_out, s5_w_in, s5_a_re, s5_a_im, s5_log_dt, s5_b_re, s5_b_im, s5_c_re, s5_c_im, s5_d, s5_w_out, ffn_w1, ffn_w3, ffn_w2, ple_w_gate, ple_w_proj, loss_target, m_norm_mix, m_norm_ffn, m_norm_ple, m_norm_final, m_gm_w_in, m_gm_ln_g, m_gm_ln_b, m_gm_w_s, m_gm_b_s, m_gm_w_out, m_s5_w_in, m_s5_a_re, m_s5_a_im, m_s5_log_dt, m_s5_b_re, m_s5_b_im, m_s5_c_re, m_s5_c_im, m_s5_d, m_s5_w_out, m_ffn_w1, m_ffn_w3, m_ffn_w2, m_ple_w_gate, m_ple_w_proj, v_norm_mix, v_norm_ffn, v_norm_ple, v_norm_final, v_gm_w_in, v_gm_ln_g, v_gm_ln_b, v_gm_w_s, v_gm_b_s, v_gm_w_out, v_s5_w_in, v_s5_a_re, v_s5_a_im, v_s5_log_dt, v_s5_b_re, v_s5_b_im, v_s5_c_re, v_s5_c_im, v_s5_d, v_s5_w_out, v_ffn_w1, v_ffn_w3, v_ffn_w2, v_ple_w_gate, v_ple_w_proj):
    env = dict(locals())
    w = {n: env[n] for n in _WEIGHT_NAMES}
    mom = {n: env["m_" + n] for n in _WEIGHT_NAMES}
    var = {n: env["v_" + n] for n in _WEIGHT_NAMES}
    xs, tgt = x[0], loss_target[0]
    L, D = xs.shape
    depth = norm_mix.shape[0]
    qx, qy = lax.axis_index("x"), lax.axis_index("y")
    q = 2 * qx + qy

    W, last_cast = {}, [None]

    def gather(tag, items, after=None):
        shards = []
        for name, layer in items:
            kind = _BIG[name]
            R, C = _rc(kind, w[name].shape)
            src = w[name] if after is None else lax.optimization_barrier((w[name], after))[0]
            shards.append(_cast_shard(f"cast_{name}{layer}", src, layer, kind, R, C))
        last_cast[0] = shards[-1]
        full = _allgather(f"ag_{tag}", shards)
        W.update({it: _W(f, _BIG[it[0]]) for it, f in zip(items, full)})

    gather("gm_w_in", [("gm_w_in", 0)])
    gather("gm_w_out", [("gm_w_out", 0)])
    gather("ffn_up0", [("ffn_w1", 0), ("ffn_w3", 0)])
    gather("ffn_down0", [("ffn_w2", 0)])

    def gather_rest(after):
        gather("ple0", [("ple_w_gate", 0), ("ple_w_proj", 0)], after)
        gather("s5_w_in", [("s5_w_in", 0)], after)
        gather("s5_w_out", [("s5_w_out", 0)], after)
        gather("ffn_up1", [("ffn_w1", 1), ("ffn_w3", 1)], after)
        gather("ffn_down1", [("ffn_w2", 1)], after)
        gather("ple1", [("ple_w_gate", 1), ("ple_w_proj", 1)], after)

    d_slots = jnp.zeros((4, D // 4), F32)
    d_slots = lax.dynamic_update_slice(d_slots, s5_d.astype(F32), (q, 0))
    d_sum = _allreduce_small("ar_s5_d", _pack([d_slots], 64).reshape(8, -1, _LANES))
    d_full = (d_sum.reshape(-1)[:D] * 0.5).reshape(1, D)

    def ffn_fwd(i, xin):
        hf = _rms_fwd(f"rms_ffn{i}", xin, norm_ffn[i:i + 1])
        a, b, f = _mm_nn(f"ffn_up{i}", hf, [W["ffn_w1", i], W["ffn_w3", i]], 1024, 1408, ffn_w2.shape[1] * 4,
                         [], [_MXU, _MXU, _MXU], _swiglu_epi, tm=1024)
        xo = _mm_nn(f"ffn_down{i}", f, [W["ffn_w2", i]], 1408, 1024, D, [xin], [F32], _add_resid, tm=1024)[0]
        return xo, (xin, hf, a, b, f)

    def ple_fwd(i, xin):
        hp = _rms_fwd(f"rms_ple{i}", xin, norm_ple[i:i + 1])
        pi, wp = p[i, 0], W["ple_w_proj", i]

        def more(tm):
            ops = [(pi, pl.BlockSpec((tm, pi.shape[1]), lambda i_, j, k: (i_, 0)))]
            for part in (0, 1):
                for half in (0, 1):
                    ops.append((wp.a, pl.BlockSpec((None, None, wp.R, wp.C),
                                                   lambda i_, j, k, part=part, half=half: (2 * j + part, half, 0, 0))))
            return ops

        xo, gt, pp = _mm_nn(f"ple_gate{i}", hp, [W["ple_w_gate", i]], 512, 2 * wp.C, D, [xin], [F32, _MXU, F32],
                            _ple_epi, tm=1024, more=more)
        return xo, (xin, hp, pi, pp, gt)

    h0 = _rms_fwd("rms_mix0", xs, norm_mix[0:1])
    z = _mm_nn("gm_in", h0, [W["gm_w_in", 0]], 1024, 1024, 2 * D, [], [F32], _ident, tm=1024)[0]
    bsT = gm_b_s[0].T
    gm_m = _gmlp_fwd(z, gm_ln_g, gm_ln_b, gm_w_s[0], bsT)
    x1 = _mm_nn("gm_out", gm_m, [W["gm_w_out", 0]], 512, 1024, D, [xs], [F32], _add_resid, tm=1024)[0]
    gather_rest(x1)
    x1 = lax.optimization_barrier((x1, last_cast[0]))[0]
    x2, ffn0 = ffn_fwd(0, x1)
    x3, ple0 = ple_fwd(0, x2)

    T = D // _LANES
    lanes = _S5_GT * _S5_P
    sv = {n: _to_view(n, w[n])[0] for n in _VIEW}
    a_re, a_im, log_dt = sv["s5_a_re"], sv["s5_a_im"], s5_log_dt
    lbr, lbi, Bbar_re, Bbar_im = _s5_prep(a_re, a_im, log_dt, sv["s5_b_re"], sv["s5_b_im"])

    def to_bd(B):
        return jnp.transpose(B.reshape(_S5_P, _S5_C, T, _S5_GT), (2, 3, 1, 0))

    def to_cd(cw):
        return jnp.transpose(cw.reshape(_S5_C, _S5_P, T, _S5_GT), (2, 3, 1, 0))

    def to_lam(v):
        return jnp.transpose(v).reshape(T, 1, lanes)

    bd_re, bd_im = to_bd(Bbar_re), to_bd(Bbar_im)
    cd_re, cd_im = to_cd(sv["s5_c_re"]), to_cd(sv["s5_c_im"])
    lam_re, lam_im = to_lam(lbr), to_lam(lbi)

    h1 = _rms_fwd("rms_mix1", x3, norm_mix[1:2])
    u = _mm_nn("s5_in", h1, [W["s5_w_in", 0]], 512, 1024, D, [], [F32], _ident, tm=1024)[0]
    s5_g, s5_re, s5_im = _s5_fwd(u, bd_re, bd_im, cd_re, cd_im, lam_re, lam_im, d_full)
    x4, glu_val, glu_sg = _mm_nn("s5_out", s5_g, [W["s5_w_out", 0], W["s5_w_out", 0]], 1024, 1024, D, [x3],
                                 [F32, _MXU, _MXU], _glu_epi, tm=1024, cb_offsets=[0, 2])
    x5, ffn1 = ffn_fwd(1, x4)
    x6, ple1 = ple_fwd(1, x5)

    dx, d_norm_final, loss_rows = _loss_head(x6, norm_final[None], tgt)

    small = {}

    seq = _Order()
    tie, done = seq.tie, seq.done
    d_norm_ple, d_norm_ffn, d_norm_mix = [None] * depth, [None] * depth, [None] * depth
    reduced = {}

    def keep(names, layer, pairs):
        for n, pr in zip(names, pairs):
            reduced[n, layer] = pr

    ple_names, up_names, down_names = ["ple_w_gate", "ple_w_proj"], ["ffn_w1", "ffn_w3"], ["ffn_w2"]

    def ple_bwd(i, dxo, saved):
        xin, hp, pi, pp, gt = saved
        dpre, dpp = done(_ple_bwd_elem(tie(dxo), pp, gt))
        dwg = done(_mm_tn(f"ple_gate_dw{i}", tie(hp), [dpre], "row", 512, 1024, 512, 1024))[0]
        dwp = done(_mm_tn(f"ple_proj_dw{i}", tie(pi), [dpp], "col", 128, 512, 128, 512))[0]
        red = _Reduction(f"ple{i}", [dwg, dwp])
        dhp = done(_mm_nt(f"ple_gate_dx{i}", [tie(dpre)], [W["ple_w_gate", i]], 512, 1024, [], [F32], _ident,
                          tm=2048))[0]
        dxin, dxin_mxu, dg = done(_rms_bwd(f"rms_ple_bwd{i}", tie(dhp), xin, norm_ple[i:i + 1], dxo))
        return dxin, dxin_mxu, dg, red

    def ffn_bwd(i, dxo, dxo_mxu, saved, before_up):
        xin, hf, a, b, f = saved
        dw2 = done(_mm_tn(f"ffn_down_dw{i}", tie(f), [dxo_mxu], "row", 1408, 1024, 1408, 1024))[0]
        r_down = _Reduction(f"ffd{i}", [dw2])
        da, db = done(_mm_nt(f"ffn_down_dx{i}", [tie(dxo_mxu)], [W["ffn_w2", i]], 1408, 1024, [a, b], [_MXU, _MXU],
                             _swiglu_bwd_epi, tm=1024))
        for step in before_up:
            step()
        r_down.scatter(seq)
        dw1, dw3 = done(_mm_tn(f"ffn_up_dw{i}", tie(hf), [da, db], "col", 1024, 1408, 1024, 1408))
        r_up = _Reduction(f"ffu{i}", [dw1, dw3])
        dhf = done(_mm_nt(f"ffn_up_dx{i}", [tie(da), db], [W["ffn_w1", i], W["ffn_w3", i]], 1024, 1408, [], [F32],
                          _ident, tm=1024))[0]
        dxin, dxin_mxu, dg = done(_rms_bwd(f"rms_ffn_bwd{i}", tie(dhf), xin, norm_ffn[i:i + 1], dxo))
        r_up.scatter(seq)
        return dxin, dxin_mxu, dg, r_down, r_up

    dx, dx_mxu, d_norm_ple[1], r_ple1 = ple_bwd(1, dx, ple1)
    dx, _, d_norm_ffn[1], r_down1, r_up1 = ffn_bwd(1, dx, dx_mxu, ffn1, [lambda: r_ple1.scatter(seq)])

    do = done([_glu_bwd_elem(tie(dx), glu_val, glu_sg)])[0]
    dw_s5out = done(_mm_tn("s5_out_dw", tie(s5_g), [do], "col", 1024, 1024, 1024, 1024))[0]
    r_s5out = _Reduction("s5out", [dw_s5out])
    dgy = done(_mm_nt("s5_out_dx", [tie(do)], [W["s5_w_out", 0]], 1024, 1024, [], [F32], _ident, tm=1024))[0]
    keep(ple_names, 1, r_ple1.finish(seq))
    keep(down_names, 1, r_down1.finish(seq))
    du, dbd_re, dbd_im, dcd_re, dcd_im, dl_re, dl_im, dd = done(_s5_bwd(
        tie(u), dgy, s5_re, s5_im, bd_re, bd_im, cd_re, cd_im, lam_re, lam_im, d_full))
    r_s5out.scatter(seq)
    dw_s5in = done(_mm_tn("s5_in_dw", tie(h1), [du], "row", 512, 1024, 512, 1024))[0]
    r_s5in = _Reduction("s5in", [dw_s5in])
    dh1 = done(_mm_nt("s5_in_dx", [tie(du)], [W["s5_w_in", 0]], 512, 1024, [], [F32], _ident, tm=2048))[0]
    dx, _, d_norm_mix[1] = done(_rms_bwd("rms_mix1_bwd", tie(dh1), x3, norm_mix[1:2], dx))
    keep(up_names, 1, r_up1.finish(seq))
    r_s5in.scatter(seq)

    def from_bd(t):
        return jnp.transpose(t, (3, 2, 0, 1)).reshape(_S5_P, _S5_C, T * _S5_GT)

    def from_cdT(t):
        return jnp.transpose(t, (2, 3, 0, 1)).reshape(_S5_C, _S5_P, T * _S5_GT)

    def from_lam(t):
        return jnp.transpose(t.reshape(T * _S5_GT, _S5_P))

    da_re, da_im, dlog_dt, db_re, db_im = _s5_prep_bwd(
        a_re, a_im, log_dt, sv["s5_b_re"], sv["s5_b_im"], from_lam(dl_re), from_lam(dl_im),
        from_bd(dbd_re), from_bd(dbd_im))
    small["s5_a_re"], small["s5_a_im"], small["s5_log_dt"] = da_re[None], da_im[None], dlog_dt
    small["s5_b_re"], small["s5_b_im"] = db_re[None], db_im[None]
    small["s5_c_re"], small["s5_c_im"] = from_cdT(dcd_re)[None], from_cdT(dcd_im)[None]

    dx, dx_mxu, d_norm_ple[0], r_ple0 = ple_bwd(0, dx, ple0)
    keep(["s5_w_out"], 0, r_s5out.finish(seq))
    xin0, hf0, a0, b0, f0 = ffn0
    da0, db0 = done(_mm_nt("ffn_down_dx0", [tie(dx_mxu)], [W["ffn_w2", 0]], 1408, 1024, [a0, b0], [_MXU, _MXU],
                           _swiglu_bwd_epi, tm=1024))
    r_ple0.scatter(seq)
    dw1, dw3 = done(_mm_tn("ffn_up_dw0", tie(hf0), [da0, db0], "col", 1024, 1408, 1024, 1408))
    r_up0 = _Reduction("ffu0", [dw1, dw3])
    keep(["s5_w_in"], 0, r_s5in.finish(seq))
    dw2 = done(_mm_tn("ffn_down_dw0", tie(f0), [dx_mxu], "row", 1408, 1024, 1408, 1024))[0]
    r_down0 = _Reduction("ffd0", [dw2])
    r_up0.scatter(seq)
    dhf0 = done(_mm_nt("ffn_up_dx0", [tie(da0), db0], [W["ffn_w1", 0], W["ffn_w3", 0]], 1024, 1408, [], [F32], _ident,
                       tm=1024))[0]
    dx, dx_mxu, d_norm_ffn[0] = done(_rms_bwd("rms_ffn_bwd0", tie(dhf0), xin0, norm_ffn[0:1], dx))
    keep(ple_names, 0, r_ple0.finish(seq))
    r_down0.scatter(seq)

    dw_gmout = done(_mm_tn("gm_out_dw", tie(gm_m), [dx_mxu], "row", 512, 1024, 512, 1024))[0]
    r_gmout = _Reduction("gmout", [dw_gmout])
    dgm = done(_mm_nt("gm_out_dx", [tie(dx_mxu)], [W["gm_w_out", 0]], 512, 1024, [], [F32], _ident, tm=2048))[0]
    dz, dws, dbsT, dlng, dlnb = done(_gmlp_bwd(tie(z), dgm, gm_ln_g, gm_ln_b, gm_w_s[0], bsT))
    dw_gmin = done(_mm_tn("gm_in_dw", tie(h0), [dz], "col", 1024, 1024, 1024, 1024))[0]
    r_gmin = _Reduction("gmin", [dw_gmin])
    dh0 = done(_mm_nt("gm_in_dx", [tie(dz)], [W["gm_w_in", 0]], 1024, 1024, [], [F32], _ident, tm=1024))[0]
    dx, _, d_norm_mix[0] = done(_rms_bwd("rms_mix0_bwd", tie(dh0), xs, norm_mix[0:1], dx))
    grad_x = dx[None]

    small["norm_mix"], small["norm_ffn"] = jnp.concatenate(d_norm_mix), jnp.concatenate(d_norm_ffn)
    small["norm_ple"], small["norm_final"] = jnp.concatenate(d_norm_ple), d_norm_final[0]
    small["gm_ln_g"], small["gm_ln_b"], small["gm_w_s"] = dlng, dlnb, dws[None]
    small["gm_b_s"] = dbsT[:, :_GM_HEADS].T[None]
    small["s5_d"] = dd

    small_names = [n for n in _WEIGHT_NAMES if n not in _BIG]
    packed = _pack([small[n] for n in small_names] + [loss_rows[:, :1]], 64).reshape(8, -1, _LANES)
    grads, deltas, new_m, new_v = {}, {}, {}, {}
    my_c = lax.axis_index("c")

    def adamw(n, layer, prev):
        kind = _BIG[n]
        R, C = _rc(kind, w[n].shape)
        return done(_adamw_big(f"adamw_{n}{layer}", w[n], mom[n], var[n], layer, reduced[n, layer], kind, R, C,
                               seq.core, seq.tok, prev))

    def adamw_last(names):
        for n in names:
            grads[n], deltas[n], new_m[n], new_v[n] = adamw(n, 0, late.get(n))

    late = {}
    for n in down_names + ple_names:
        late[n] = adamw(n, 1, None)
    keep(up_names, 0, r_up0.finish(seq))
    r_gmout.scatter(seq)
    r_gmin.scatter(seq)
    landed = _exchange_slices("ar_small_in", packed)
    for n in up_names:
        late[n] = adamw(n, 1, None)
    adamw_last(["s5_w_in", "s5_w_out"] + ple_names)
    keep(down_names, 0, r_down0.finish(seq))
    adamw_last(up_names)
    keep(["gm_w_out"], 0, r_gmout.finish(seq))
    mine = done([_sum_slices("ar_small_sum", packed, landed, seq.tok)])[0]
    spread = _broadcast_slices("ar_small_out", mine)
    adamw_last(down_names)
    keep(["gm_w_in"], 0, r_gmin.finish(seq))
    adamw_last(["gm_w_out", "gm_w_in"])
    summed = lax.dynamic_update_slice(spread, mine[None], (4 * qx + 2 * qy + my_c, 0, 0))
    *red_list, loss_sum = _unpack(summed, [small[n].shape for n in small_names] + [(1, 1)])
    red_small, loss = dict(zip(small_names, red_list)), loss_sum.reshape(())
    red_small["s5_d"] = lax.dynamic_slice(red_small["s5_d"], (0, q * (D // 4)), (1, D // 4))

    def two_d(a):
        return a.reshape((1,) * (2 - a.ndim) + a.shape)

    def views(src):
        return [two_d(_to_view(n, src[n])) for n in small_names]

    g_views = [two_d(red_small[n]) for n in small_names]
    dl, mo, vo = _adamw_small(views(w), g_views, views(mom), views(var))
    for n, g_, d_, m_, v_ in zip(small_names, g_views, dl, mo, vo):
        grads[n], deltas[n], new_m[n], new_v[n] = [_from_view(n, t_.reshape(_to_view(n, w[n]).shape)).reshape(w[n].shape)
                                                  for t_ in (g_, d_, m_, v_)]

    return (loss, grad_x, *[grads[n] for n in _WEIGHT_NAMES], *[deltas[n] for n in _WEIGHT_NAMES],
            *[new_m[n] for n in _WEIGHT_NAMES], *[new_v[n] for n in _WEIGHT_NAMES])
```

```python
import math

import jax
import jax.numpy as jnp
from jax import lax
from jax.experimental import pallas as pl
from jax.experimental.pallas import tpu as pltpu
from jax.experimental.pallas import tpu_sc as plsc

F32 = jnp.float32
_MXU = jnp.bfloat16
_WIRE = jnp.bfloat16
_EPS = 1e-6
_VMEM_LIMIT = 56 * 1024 * 1024
_LANES = 128
_MESH = pl.DeviceIdType.MESH

_LR, _B1, _B2, _AEPS, _WD, _STEP = 0.001, 0.9, 0.999, 1e-08, 0.01, 10

_GM_CHUNK = 128
_GM_HEADS = 16
_S5_GT = 8
_S5_P = 64
_S5_C = 16

_NN = (((1,), (0,)), ((), ()))
_NT = (((1,), (1,)), ((), ()))
_TN = (((0,), (0,)), ((), ()))


def _cparams(sem):
    return pltpu.CompilerParams(dimension_semantics=sem, vmem_limit_bytes=_VMEM_LIMIT)


def _sig(x):
    return 0.5 * jnp.tanh(0.5 * x) + 0.5


_GC = math.sqrt(2.0 / math.pi)


def _gelu(x):
    return 0.5 * x * (1.0 + jnp.tanh(_GC * (x + 0.044715 * (x * x * x))))


def _gelu_grad(x):
    t = jnp.tanh(_GC * (x + 0.044715 * (x * x * x)))
    return 0.5 * (1.0 + t) + 0.5 * x * (1.0 - t * t) * (_GC * (1.0 + 3.0 * 0.044715 * x * x))


def _dot(a, b, dn):
    return lax.dot_general(a.astype(_MXU), b.astype(_MXU), dn, preferred_element_type=F32)


class _W:
    def __init__(self, arr, kind):
        self.a, self.kind = arr, kind
        self.R, self.C = arr.shape[2], arr.shape[3]

    def full_shape(self):
        return (2 * self.R, 4 * self.C) if self.kind == "col" else (4 * self.R, 2 * self.C)


def _part_index(kind, R, C, tr, tc, rb, cb):
    nr, nc = R // tr, C // tc
    if kind == "col":
        return cb // nc, rb // nr, rb % nr, cb % nc
    return rb // nr, cb // nc, rb % nr, cb % nc


def _wspec(w, tr, tc, rb_fn, cb_fn):
    assert w.R % tr == 0 and w.C % tc == 0, (w.R, w.C, tr, tc)

    def imap(i, j, k):
        return _part_index(w.kind, w.R, w.C, tr, tc, rb_fn(i, j, k), cb_fn(i, j, k))

    return pl.BlockSpec((None, None, tr, tc), imap)


def _gspec(kind, R, C, tr, tc):
    assert R % tr == 0 and C % tc == 0, (R, C, tr, tc)

    def imap(i, j, k):
        part, half, rbi, cbi = _part_index(kind, R, C, tr, tc, i, j)
        return half, part, rbi, cbi

    return pl.BlockSpec((None, None, tr, tc), imap)


def _mm(name, grid, a_ops, b_ops, pairs, acc_shape, n_acc, extras, outs, epilogue):
    nk = grid[2]
    na, nb, ne, no = len(a_ops), len(b_ops), len(extras), len(outs)

    def body(*refs):
        a_refs = refs[:na]
        b_refs = refs[na:na + nb]
        e_refs = refs[na + nb:na + nb + ne]
        o_refs = refs[na + nb + ne:na + nb + ne + no]
        acc_refs = refs[na + nb + ne + no:]
        k = pl.program_id(2)

        def products():
            sums = [None] * n_acc
            for ai, bi, ci, dn in pairs:
                d = _dot(a_refs[ai][...], b_refs[bi][...], dn)
                sums[ci] = d if sums[ci] is None else sums[ci] + d
            return sums

        def finish(accs):
            res = epilogue(accs, [e[...] for e in e_refs])
            for o, r in zip(o_refs, res):
                o[...] = r.astype(o.dtype)

        if nk == 1:
            finish(products())
            return

        @pl.when(k == 0)
        def _():
            for acc, d in zip(acc_refs, products()):
                acc[...] = d

        @pl.when(jnp.logical_and(k > 0, k < nk - 1))
        def _():
            for acc, d in zip(acc_refs, products()):
                acc[...] += d

        @pl.when(k == nk - 1)
        def _():
            finish([acc[...] + d for acc, d in zip(acc_refs, products())])

    ops = list(a_ops) + list(b_ops) + list(extras)
    return pl.pallas_call(
        body, name=name, grid=grid,
        in_specs=[s for _, s in ops],
        out_specs=[s for _, s in outs],
        out_shape=[s for s, _ in outs],
        scratch_shapes=[pltpu.VMEM(acc_shape, F32) for _ in range(n_acc if nk > 1 else 0)],
        compiler_params=_cparams(("parallel", "parallel", "arbitrary")),
    )(*[a for a, _ in ops])


def _bs(shape, fn):
    return pl.BlockSpec(shape, fn)


def _tile_m(L):
    return min(L, 512)


def _mm_nn(name, x, ws, tk, tn, n_out, extras, outs_sd, epilogue, tm=None, cb_offsets=None, more=None):
    M, K = x.shape
    tm = min(M, tm or _tile_m(M))
    grid = (M // tm, n_out // tn, K // tk)
    a_ops = [(x, _bs((tm, tk), lambda i, j, k: (i, k)))]
    cb_offsets = cb_offsets or [0] * len(ws)
    b_ops = [(w.a, _wspec(w, tk, tn, lambda i, j, k: k, (lambda off: lambda i, j, k: j + off)(off)))
             for w, off in zip(ws, cb_offsets)]
    pairs = [(0, bi, bi, _NN) for bi in range(len(ws))]
    mn = _bs((tm, tn), lambda i, j, k: (i, j))
    ex = [(e, mn) for e in extras] + (more(tm) if more else [])
    outs = [(jax.ShapeDtypeStruct((M, n_out), dt), mn) for dt in outs_sd]
    return _mm(name, grid, a_ops, b_ops, pairs, (tm, tn), len(ws), ex, outs, epilogue)


def _mm_nt(name, xs, ws, tn, tk, extras, outs_sd, epilogue, tm=None):
    M, Nw = xs[0].shape
    Kw = ws[0].full_shape()[0]
    tm = min(M, tm or _tile_m(M))
    grid = (M // tm, Kw // tn, Nw // tk)
    a_ops = [(x, _bs((tm, tk), lambda i, j, k: (i, k))) for x in xs]
    b_ops = [(w.a, _wspec(w, tn, tk, lambda i, j, k: j, lambda i, j, k: k)) for w in ws]
    pairs = [(i, i, 0, _NT) for i in range(len(ws))]
    mn = _bs((tm, tn), lambda i, j, k: (i, j))
    ex = [(e, mn) for e in extras]
    outs = [(jax.ShapeDtypeStruct((M, Kw), dt), mn) for dt in outs_sd]
    return _mm(name, grid, a_ops, b_ops, pairs, (tm, tn), 1, ex, outs, epilogue)


def _mm_tn(name, x, dys, kind, R, C, tm, tn, tk=None):
    L, Kw = x.shape
    Nw = dys[0].shape[1]
    tk = tk or min(L, 1024)
    grid = (Kw // tm, Nw // tn, L // tk)
    a_ops = [(x, _bs((tk, tm), lambda i, j, k: (k, i)))]
    b_ops = [(dy, _bs((tk, tn), lambda i, j, k: (k, j))) for dy in dys]
    pairs = [(0, bi, bi, _TN) for bi in range(len(dys))]
    gs = _gspec(kind, R, C, tm, tn)
    outs = [(jax.ShapeDtypeStruct((2, 4, R, C), _WIRE), gs) for _ in dys]
    return _mm(name, grid, a_ops, b_ops, pairs, (tm, tn), len(dys), [], outs, lambda accs, ex: accs)


def _row_tile(L):
    return min(L, 256)


def _rowwise(name, body, ins, outs, L, acc_outs=()):
    tr = _row_tile(L)
    n_in, n_out = len(ins), len(outs)

    def kbody(*refs):
        i_refs, o_refs, a_refs = refs[:n_in], refs[n_in:n_in + n_out], refs[n_in + n_out:]
        res, sums = body(*[r[...] for r in i_refs])
        for o, r in zip(o_refs, res):
            o[...] = r.astype(o.dtype)
        if a_refs:
            @pl.when(pl.program_id(0) == 0)
            def _():
                for a in a_refs:
                    a[...] = jnp.zeros(a.shape, F32)
            for a, s in zip(a_refs, sums):
                a[...] += s

    in_specs = []
    for arr, kind in ins:
        if kind == "row":
            in_specs.append(pl.BlockSpec((tr, arr.shape[1]), lambda i: (i, 0)))
        else:
            in_specs.append(pl.BlockSpec(arr.shape, lambda i: (0, 0)))
    out_specs = [pl.BlockSpec((tr, c), lambda i: (i, 0)) for c, _ in outs]
    out_shape = [jax.ShapeDtypeStruct((L, c), dt) for c, dt in outs]
    out_specs += [pl.BlockSpec((1, c), lambda i: (0, 0)) for c in acc_outs]
    out_shape += [jax.ShapeDtypeStruct((1, c), F32) for c in acc_outs]
    return pl.pallas_call(
        kbody, name=name, grid=(L // tr,), in_specs=in_specs, out_specs=out_specs, out_shape=out_shape,
        compiler_params=_cparams(("arbitrary",)),
    )(*[a for a, _ in ins])


def _rms_fwd(name, x, g):
    def body(xv, gv):
        r = lax.rsqrt(jnp.mean(xv * xv, axis=-1, keepdims=True) + _EPS)
        return [xv * r * gv], []
    return _rowwise(name, body, [(x, "row"), (g, "vec")], [(x.shape[1], _MXU)], x.shape[0])[0]


def _rms_bwd(name, dh, x, g, dres):
    def body(dhv, xv, gv, dr):
        r = lax.rsqrt(jnp.mean(xv * xv, axis=-1, keepdims=True) + _EPS)
        xh = xv * r
        dxh = dhv * gv
        dx = dr + r * (dxh - xh * jnp.mean(dxh * xh, axis=-1, keepdims=True))
        return [dx, dx], [jnp.sum(dhv * xh, axis=0, keepdims=True)]
    D = x.shape[1]
    return _rowwise(name, body, [(dh, "row"), (x, "row"), (g, "vec"), (dres, "row")], [(D, F32), (D, _MXU)],
                    x.shape[0], [D])


def _loss_head(x, g, target, pp, gt):
    D = x.shape[1]

    def body(xv, gv, tv, ppv, gtv):
        r = lax.rsqrt(jnp.mean(xv * xv, axis=-1, keepdims=True) + _EPS)
        xh = xv * r
        e = xh * gv - tv
        dy = e * (1.0 / D)
        dxh = dy * gv
        dx = r * (dxh - xh * jnp.mean(dxh * xh, axis=-1, keepdims=True))
        row_loss = 0.5 * jnp.mean(e * e, axis=-1, keepdims=True)
        lsum = jnp.sum(row_loss, axis=0, keepdims=True) + jnp.zeros((1, _LANES), F32)
        gt32 = gtv.astype(F32)
        return [dx, dx * ppv * gt32 * (1.0 - gt32), dx * gt32], [jnp.sum(dy * xh, axis=0, keepdims=True), lsum]
    return _rowwise("loss_head", body, [(x, "row"), (g, "vec"), (target, "row"), (pp, "row"), (gt, "row")],
                    [(D, F32), (D, _MXU), (D, _MXU)], x.shape[0], [D, _LANES])


def _ple_bwd_elem(dx, pp, gt):
    def body(dxv, ppv, gtv):
        gt32 = gtv.astype(F32)
        return [dxv * ppv * gt32 * (1.0 - gt32), dxv * gt32], []
    D = dx.shape[1]
    return _rowwise("ple_bwd_elem", body, [(dx, "row"), (pp, "row"), (gt, "row")], [(D, _MXU), (D, _MXU)], dx.shape[0])


def _glu_bwd_elem(dx, val, sg):
    def body(dxv, vv, sv):
        v32, s32 = vv.astype(F32), sv.astype(F32)
        return [jnp.concatenate([dxv * s32, dxv * v32 * s32 * (1.0 - s32)], axis=1)], []
    D = dx.shape[1]
    return _rowwise("glu_bwd_elem", body, [(dx, "row"), (val, "row"), (sg, "row")], [(2 * D, _MXU)], dx.shape[0])[0]


def _gm_common(z, ln_g, ln_b, wc_bf, bsT):
    W = z.shape[1] // 2
    zu, zv = z[:, :W], z[:, W:]
    u, v = _gelu(zu), _gelu(zv)
    mu = jnp.mean(v, axis=-1, keepdims=True)
    vc = v - mu
    rstd = lax.rsqrt(jnp.mean(vc * vc, axis=-1, keepdims=True) + _EPS)
    vh = vc * rstd
    vn = vh * ln_g + ln_b
    vnb = vn.astype(_MXU)
    svs = []
    for h in range(_GM_HEADS):
        sl = slice(h * _LANES, (h + 1) * _LANES)
        svs.append(_dot(wc_bf[h], vnb[:, sl], _NN) + bsT[:, h:h + 1])
    return zu, zv, u, vh, rstd, vnb, svs


def _causal(w):
    t = lax.broadcasted_iota(jnp.int32, w.shape, w.ndim - 2)
    s = lax.broadcasted_iota(jnp.int32, w.shape, w.ndim - 1)
    return jnp.where(s <= t, w, jnp.zeros_like(w))


def _gmlp_fwd(z, ln_g, ln_b, w_s, bsT):
    L, W2 = z.shape
    W = W2 // 2

    def body(z_ref, g_ref, b_ref, ws_ref, bs_ref, m_ref):
        wc = _causal(ws_ref[...]).astype(_MXU)
        _, _, u, _, _, _, svs = _gm_common(z_ref[...], g_ref[...], b_ref[...], wc, bs_ref[...])
        for h in range(_GM_HEADS):
            sl = slice(h * _LANES, (h + 1) * _LANES)
            m_ref[:, sl] = (u[:, sl] * svs[h]).astype(m_ref.dtype)

    return pl.pallas_call(
        body, name="gmlp_fwd", grid=(L // _GM_CHUNK,),
        in_specs=[pl.BlockSpec((_GM_CHUNK, W2), lambda n: (n, 0)),
                  pl.BlockSpec((1, W), lambda n: (0, 0)), pl.BlockSpec((1, W), lambda n: (0, 0)),
                  pl.BlockSpec(w_s.shape, lambda n: (0, 0, 0)), pl.BlockSpec(bsT.shape, lambda n: (0, 0))],
        out_specs=pl.BlockSpec((_GM_CHUNK, W), lambda n: (n, 0)),
        out_shape=jax.ShapeDtypeStruct((L, W), _MXU),
        compiler_params=_cparams(("arbitrary",)),
    )(z, ln_g, ln_b, w_s, bsT)


def _gmlp_bwd(z, dm, ln_g, ln_b, w_s, bsT):
    L, W2 = z.shape
    W = W2 // 2
    T = _GM_CHUNK

    def body(z_ref, dm_ref, g_ref, b_ref, ws_ref, bs_ref, dz_ref, dws_ref, dbs_ref, dg_ref, db_ref):
        @pl.when(pl.program_id(0) == 0)
        def _():
            dws_ref[...] = jnp.zeros(dws_ref.shape, F32)
            dbs_ref[...] = jnp.zeros(dbs_ref.shape, F32)
            dg_ref[...] = jnp.zeros(dg_ref.shape, F32)
            db_ref[...] = jnp.zeros(db_ref.shape, F32)

        wc = _causal(ws_ref[...]).astype(_MXU)
        ln_g_v = g_ref[...]
        zu, zv, u, vh, rstd, vnb, svs = _gm_common(z_ref[...], ln_g_v, b_ref[...], wc, bs_ref[...])
        dmv = dm_ref[...]
        lane = lax.broadcasted_iota(jnp.int32, (T, _LANES), 1)
        dbs = jnp.zeros((T, _LANES), F32)
        dvn_parts = []
        for h in range(_GM_HEADS):
            sl = slice(h * _LANES, (h + 1) * _LANES)
            dsv = dmv[:, sl] * u[:, sl]
            dz_ref[:, sl] = (dmv[:, sl] * svs[h] * _gelu_grad(zu[:, sl])).astype(dz_ref.dtype)
            dbs = dbs + jnp.where(lane == h, jnp.sum(dsv, axis=1, keepdims=True), 0.0)
            dsvb = dsv.astype(_MXU)
            dws_ref[h] += _dot(dsvb, vnb[:, sl], _NT)
            dvn_parts.append(_dot(wc[h], dsvb, _TN))
        dbs_ref[...] += dbs
        dvn = jnp.concatenate(dvn_parts, axis=1)
        dg_ref[...] += jnp.sum(dvn * vh, axis=0, keepdims=True)
        db_ref[...] += jnp.sum(dvn, axis=0, keepdims=True)
        dxh = dvn * ln_g_v
        dv = rstd * (dxh - jnp.mean(dxh, axis=-1, keepdims=True) - vh * jnp.mean(dxh * vh, axis=-1, keepdims=True))
        dz_ref[:, W:] = (dv * _gelu_grad(zv)).astype(dz_ref.dtype)

        @pl.when(pl.program_id(0) == pl.num_programs(0) - 1)
        def _():
            dws_ref[...] = _causal(dws_ref[...])

    return pl.pallas_call(
        body, name="gmlp_bwd", grid=(L // T,),
        in_specs=[pl.BlockSpec((T, W2), lambda n: (n, 0)), pl.BlockSpec((T, W), lambda n: (n, 0)),
                  pl.BlockSpec((1, W), lambda n: (0, 0)), pl.BlockSpec((1, W), lambda n: (0, 0)),
                  pl.BlockSpec(w_s.shape, lambda n: (0, 0, 0)), pl.BlockSpec(bsT.shape, lambda n: (0, 0))],
        out_specs=[pl.BlockSpec((T, W2), lambda n: (n, 0)),
                   pl.BlockSpec(w_s.shape, lambda n: (0, 0, 0)), pl.BlockSpec((T, _LANES), lambda n: (0, 0)),
                   pl.BlockSpec((1, W), lambda n: (0, 0)), pl.BlockSpec((1, W), lambda n: (0, 0))],
        out_shape=[jax.ShapeDtypeStruct((L, W2), _MXU), jax.ShapeDtypeStruct(w_s.shape, F32),
                   jax.ShapeDtypeStruct((T, _LANES), F32),
                   jax.ShapeDtypeStruct((1, W), F32), jax.ShapeDtypeStruct((1, W), F32)],
        compiler_params=_cparams(("arbitrary",)),
    )(z, dm, ln_g, ln_b, w_s, bsT)


def _s5_prep_math(a_re, a_im, log_dt):
    dt = jnp.exp(log_dt)
    xr, xi = a_re * dt, a_im * dt
    e = jnp.exp(xr)
    lbr, lbi = e * jnp.cos(xi), e * jnp.sin(xi)
    dn = a_re * a_re + a_im * a_im
    nr, ni = lbr - 1.0, lbi
    pr, pi = nr * a_re + ni * a_im, ni * a_re - nr * a_im
    return dt, lbr, lbi, dn, nr, ni, pr, pi


def _vm():
    return pl.BlockSpec(memory_space=pltpu.VMEM)


def _s5_prep(a_re, a_im, log_dt, b_re, b_im):
    def body(ar_ref, ai_ref, ld_ref, br_ref, bi_ref, lbr_ref, lbi_ref, Br_ref, Bi_ref):
        _, lbr, lbi, dn, _, _, pr, pi = _s5_prep_math(ar_ref[...], ai_ref[...], ld_ref[...])
        cr, ci = (pr / dn)[:, None, :], (pi / dn)[:, None, :]
        lbr_ref[...] = lbr
        lbi_ref[...] = lbi
        br, bi = br_ref[...], bi_ref[...]
        Br_ref[...] = cr * br - ci * bi
        Bi_ref[...] = cr * bi + ci * br

    sd = jax.ShapeDtypeStruct
    return pl.pallas_call(
        body, name="s5_prep", in_specs=[_vm()] * 5, out_specs=[_vm()] * 4,
        out_shape=[sd(a_re.shape, F32), sd(a_re.shape, F32), sd(b_re.shape, F32), sd(b_re.shape, F32)],
    )(a_re, a_im, log_dt, b_re, b_im)


def _s5_prep_bwd(a_re, a_im, log_dt, b_re, b_im, dlbr_s, dlbi_s, dBr, dBi):
    def body(ar_ref, ai_ref, ld_ref, br_ref, bi_ref, dlr_ref, dli_ref, dBr_ref, dBi_ref,
             dar_ref, dai_ref, dld_ref, dbr_ref, dbi_ref):
        a_re_v, a_im_v = ar_ref[...], ai_ref[...]
        dt, lbr, lbi, dn, nr, ni, pr, pi = _s5_prep_math(a_re_v, a_im_v, ld_ref[...])
        cr, ci = (pr / dn)[:, None, :], (pi / dn)[:, None, :]
        br, bi, dBr_v, dBi_v = br_ref[...], bi_ref[...], dBr_ref[...], dBi_ref[...]
        dbr_ref[...] = cr * dBr_v + ci * dBi_v
        dbi_ref[...] = cr * dBi_v - ci * dBr_v
        dcr = jnp.sum(br * dBr_v + bi * dBi_v, axis=1)
        dci = jnp.sum(br * dBi_v - bi * dBr_v, axis=1)
        dpr, dpi = dcr / dn, dci / dn
        ddn = -(dcr * pr + dci * pi) / (dn * dn)
        dnr = dpr * a_re_v - dpi * a_im_v
        dni = dpr * a_im_v + dpi * a_re_v
        dlbr = dlr_ref[...] + dnr
        dlbi = dli_ref[...] + dni
        dxr = dlbr * lbr + dlbi * lbi
        dxi = dlbi * lbr - dlbr * lbi
        dar_ref[...] = dpr * nr + dpi * ni + 2.0 * ddn * a_re_v + dxr * dt
        dai_ref[...] = dpr * ni - dpi * nr + 2.0 * ddn * a_im_v + dxi * dt
        dld_ref[...] = jnp.sum(dxr * a_re_v + dxi * a_im_v, axis=0, keepdims=True) * dt

    sd = jax.ShapeDtypeStruct
    return pl.pallas_call(
        body, name="s5_prep_bwd", in_specs=[_vm()] * 9, out_specs=[_vm()] * 5,
        out_shape=[sd(a_re.shape, F32), sd(a_re.shape, F32), sd(log_dt.shape, F32),
                   sd(b_re.shape, F32), sd(b_re.shape, F32)],
    )(a_re, a_im, log_dt, b_re, b_im, dlbr_s, dlbi_s, dBr, dBi)


def _shift_rows(v, down):
    n = v.shape[0]
    rolled = pltpu.roll(v, 1 if down else n - 1, 0)
    row = lax.broadcasted_iota(jnp.int32, v.shape, 0)
    return jnp.where(row == (0 if down else n - 1), 0.0, rolled)


def _cmul(ar, ai, br, bi):
    return ar * br - ai * bi, ar * bi + ai * br


_SEG = 8
_UNROLL = 8


def _seg_rows(k):
    if isinstance(k, int):
        return pl.ds(k * _SEG, _SEG)
    return pl.ds(pl.multiple_of(k * _SEG, _SEG), _SEG)


def _unrolled(n, step, init):
    main = n // _UNROLL

    def trip(kk, s):
        for uu in range(_UNROLL):
            s = step(kk * _UNROLL + uu, s)
        return s

    s = lax.fori_loop(0, main, trip, init)
    for r in range(main * _UNROLL, n):
        s = step(r, s)
    return s


def _interleave(src_ref, dst_ref, nk):
    def step(k, carry):
        dst_ref[_seg_rows(k), :] = src_ref[pl.ds(k, _SEG, stride=nk), :]
        return carry
    _unrolled(nk, step, 0)


def _deinterleave(src_ref, dst_ref, nk):
    def step(k, carry):
        dst_ref[pl.ds(k, _SEG, stride=nk), :] = src_ref[_seg_rows(k), :]
        return carry
    _unrolled(nk, step, 0)


def _segment_inits(er, ei, ar, ai, nk, down):
    pr, pi = ar, ai
    for _ in range(int(math.log2(nk))):
        pr, pi = _cmul(pr, pi, pr, pi)
    fr, fi = er, ei
    for _ in range(_SEG - 1):
        sr, si = _shift_rows(fr, down), _shift_rows(fi, down)
        mr, mi = _cmul(pr, pi, sr, si)
        fr, fi = er + mr, ei + mi
    return _shift_rows(fr, down), _shift_rows(fi, down)


def _scan_states(x_re, x_im, ar, ai, nk):
    lanes = ar.shape[1]

    def step(k, s):
        rows = _seg_rows(k)
        mr, mi = _cmul(ar, ai, s[0], s[1])
        return mr + x_re[rows, :], mi + x_im[rows, :]

    zero = jnp.zeros((_SEG, lanes), F32)
    er, ei = _unrolled(nk, step, (zero, zero))
    ir, ii = _segment_inits(er, ei, ar, ai, nk, True)

    def step2(k, s):
        rows = _seg_rows(k)
        mr, mi = _cmul(ar, ai, s[0], s[1])
        nr, ni = mr + x_re[rows, :], mi + x_im[rows, :]
        x_re[rows, :] = nr
        x_im[rows, :] = ni
        return nr, ni

    _unrolled(nk, step2, (ir, ii))


def _s5_tile_fwd(u, bd_re, bd_im, cd_re, cd_im, ar, ai, d, s_re, s_im, nk):
    s_re[...] = _dot(u, bd_re, _NN)
    s_im[...] = _dot(u, bd_im, _NN)
    _scan_states(s_re, s_im, ar, ai, nk)
    return _dot(s_re[...], cd_re, _NN) - _dot(s_im[...], cd_im, _NN) + d * u


def _s5_specs(L, T):
    lanes = _S5_GT * _S5_P
    u_spec = pl.BlockSpec((L, _LANES), lambda t: (0, t))
    bd_spec = pl.BlockSpec((None, _S5_GT, _S5_C, _S5_P), lambda t: (t, 0, 0, 0))
    cd_spec = pl.BlockSpec((None, _S5_GT, _S5_P, _S5_C), lambda t: (t, 0, 0, 0))
    lam_spec = pl.BlockSpec((None, 1, lanes), lambda t: (t, 0, 0))
    d_spec = pl.BlockSpec((1, _LANES), lambda t: (0, t))
    return lanes, u_spec, bd_spec, cd_spec, lam_spec, d_spec


def _fill_block_diag(dst_ref, blocks_ref):
    _, a, b = blocks_ref.shape
    dst_ref[...] = jnp.zeros(dst_ref.shape, F32)
    for g in range(_S5_GT):
        dst_ref[g * a:(g + 1) * a, g * b:(g + 1) * b] = blocks_ref[g]


def _take_block_diag(dst_ref, v):
    _, a, b = dst_ref.shape
    for g in range(_S5_GT):
        dst_ref[g] = v[g * a:(g + 1) * a, g * b:(g + 1) * b]


def _s5_dense(bdr, bdi, cdr, cdi, dense):
    for src, dst in zip((bdr, bdi, cdr, cdi), dense):
        _fill_block_diag(dst, src)
    return [dst[...] for dst in dense]


def _s5_dense_scratch(lanes):
    return [pltpu.VMEM((_LANES, lanes), F32), pltpu.VMEM((_LANES, lanes), F32),
            pltpu.VMEM((lanes, _LANES), F32), pltpu.VMEM((lanes, _LANES), F32)]


def _s5_fwd(u, bd_re, bd_im, cd_re, cd_im, lam_re, lam_im, d):
    L, Wd = u.shape
    T = Wd // _LANES
    nk = L // _SEG
    lanes, u_spec, bd_spec, cd_spec, lam_spec, d_spec = _s5_specs(L, T)
    s_spec = pl.BlockSpec((L, lanes), lambda t: (0, t))

    def body(u_ref, bdr, bdi, cdr, cdi, lr, li, d_ref, g_ref, s_re, s_im, up, tmp, *dense):
        ar = jnp.broadcast_to(lr[...], (_SEG, lanes))
        ai = jnp.broadcast_to(li[...], (_SEG, lanes))
        bd_re_v, bd_im_v, cd_re_v, cd_im_v = _s5_dense(bdr, bdi, cdr, cdi, dense)
        _interleave(u_ref, up, nk)
        y = _s5_tile_fwd(up[...], bd_re_v, bd_im_v, cd_re_v, cd_im_v, ar, ai, d_ref[...], s_re, s_im, nk)
        up[...] = _gelu(y)
        _deinterleave(up, tmp, nk)
        g_ref[...] = tmp[...].astype(g_ref.dtype)

    return pl.pallas_call(
        body, name="s5_fwd", grid=(T,),
        in_specs=[u_spec, bd_spec, bd_spec, cd_spec, cd_spec, lam_spec, lam_spec, d_spec],
        out_specs=[u_spec, s_spec, s_spec],
        out_shape=[jax.ShapeDtypeStruct((L, Wd), _MXU), jax.ShapeDtypeStruct((L, T * lanes), F32),
                   jax.ShapeDtypeStruct((L, T * lanes), F32)],
        scratch_shapes=[pltpu.VMEM((L, _LANES), F32) for _ in range(2)] + _s5_dense_scratch(lanes),
        compiler_params=_cparams(("arbitrary",)),
    )(u, bd_re, bd_im, cd_re, cd_im, lam_re, lam_im, d)


def _s5_bwd(u, dg, states_re, states_im, bd_re, bd_im, cd_re, cd_im, lam_re, lam_im, d):
    L, Wd = u.shape
    T = Wd // _LANES
    nk = L // _SEG
    lanes, u_spec, bd_spec, cd_spec, lam_spec, d_spec = _s5_specs(L, T)
    s_spec = pl.BlockSpec((L, lanes), lambda t: (0, t))

    def body(u_ref, dg_ref, s_re, s_im, bdr, bdi, cdr, cdi, lr, li, d_ref,
             du_ref, dbdr, dbdi, dcdr, dcdi, dlr, dli, dd_ref, g_re, g_im, up, dgp, tmp, *dense):
        ar = jnp.broadcast_to(lr[...], (_SEG, lanes))
        ai = jnp.broadcast_to(li[...], (_SEG, lanes))
        bd_re_v, bd_im_v, cd_re_v, cd_im_v = _s5_dense(bdr, bdi, cdr, cdi, dense)
        _interleave(u_ref, up, nk)
        _interleave(dg_ref, dgp, nk)
        uv, dv = up[...], d_ref[...]
        y = _dot(s_re[...], cd_re_v, _NN) - _dot(s_im[...], cd_im_v, _NN) + dv * uv
        dy = dgp[...] * _gelu_grad(y)
        dd_ref[...] = jnp.sum(dy * uv, axis=0, keepdims=True)
        dyb = dy.astype(_MXU)
        _take_block_diag(dcdr, _dot(dyb, s_re[...], _TN))
        _take_block_diag(dcdi, -_dot(dyb, s_im[...], _TN))
        g_re[...] = _dot(dyb, cd_re_v, _NT)
        g_im[...] = -_dot(dyb, cd_im_v, _NT)

        nai = -ai

        def step(j, s):
            rows = _seg_rows(nk - 1 - j)
            mr, mi = _cmul(ar, nai, s[0], s[1])
            return mr + g_re[rows, :], mi + g_im[rows, :]

        zero = jnp.zeros((_SEG, lanes), F32)
        er, ei = _unrolled(nk, step, (zero, zero))
        ir, ii = _segment_inits(er, ei, ar, nai, nk, False)

        def acc_lam(gr, gi, pr, pi, acc):
            return acc[0] + gr * pr + gi * pi, acc[1] + gi * pr - gr * pi

        def step2(j, carry):
            s, acc = carry
            k = nk - 1 - j
            rows = _seg_rows(k)
            mr, mi = _cmul(ar, nai, s[0], s[1])
            nr, ni = mr + g_re[rows, :], mi + g_im[rows, :]
            g_re[rows, :] = nr
            g_im[rows, :] = ni
            prev = _seg_rows(k - 1)
            return (nr, ni), acc_lam(nr, ni, s_re[prev, :], s_im[prev, :], acc)

        (g0r, g0i), acc = _unrolled(nk - 1, step2, ((ir, ii), (zero, zero)))
        first = _seg_rows(0)
        mr, mi = _cmul(ar, nai, g0r, g0i)
        nr, ni = mr + g_re[first, :], mi + g_im[first, :]
        g_re[first, :] = nr
        g_im[first, :] = ni
        last = _seg_rows(nk - 1)
        acc = acc_lam(nr, ni, _shift_rows(s_re[last, :], True), _shift_rows(s_im[last, :], True), acc)
        dlr[...] = jnp.sum(acc[0], axis=0, keepdims=True)
        dli[...] = jnp.sum(acc[1], axis=0, keepdims=True)

        gtr, gti = g_re[...].astype(_MXU), g_im[...].astype(_MXU)
        ub = uv.astype(_MXU)
        _take_block_diag(dbdr, _dot(ub, gtr, _TN))
        _take_block_diag(dbdi, _dot(ub, gti, _TN))
        dgp[...] = _dot(gtr, bd_re_v, _NT) + _dot(gti, bd_im_v, _NT) + dy * dv
        _deinterleave(dgp, tmp, nk)
        du_ref[...] = tmp[...].astype(du_ref.dtype)

    sd = jax.ShapeDtypeStruct
    big = sd((T, _S5_GT, _S5_C, _S5_P), F32)
    return pl.pallas_call(
        body, name="s5_bwd", grid=(T,),
        in_specs=[u_spec, u_spec, s_spec, s_spec, bd_spec, bd_spec, cd_spec, cd_spec, lam_spec, lam_spec, d_spec],
        out_specs=[u_spec, bd_spec, bd_spec, bd_spec, bd_spec, lam_spec, lam_spec, d_spec],
        out_shape=[sd((L, Wd), _MXU), big, big, big, big, sd((T, 1, lanes), F32), sd((T, 1, lanes), F32),
                   sd((1, Wd), F32)],
        scratch_shapes=[pltpu.VMEM((L, lanes), F32) for _ in range(2)]
        + [pltpu.VMEM((L, _LANES), F32) for _ in range(3)] + _s5_dense_scratch(lanes),
        compiler_params=_cparams(("arbitrary",)),
    )(u, dg, states_re, states_im, bd_re, bd_im, cd_re, cd_im, lam_re, lam_im, d)


def _half_tile(R, few_arrays=False):
    for t in ((512, 704, 128) if few_arrays else (256, 352, 128)):
        if R % t == 0:
            return t
    raise ValueError(R)


def _cast_shard(name, w, layer, kind, R, C):
    tr = _half_tile(R, True)
    nr = R // tr

    def body(w_ref, o_ref):
        o_ref[...] = w_ref[...].astype(o_ref.dtype)

    if kind == "col":
        in_map = lambda h, i: (layer, h * nr + i, 0)
    else:
        in_map = lambda h, i: (layer, i, h)
    return pl.pallas_call(
        body, name=name, grid=(2, nr), in_specs=[pl.BlockSpec((None, tr, C), in_map)],
        out_specs=pl.BlockSpec((None, tr, C), lambda h, i: (h, i, 0)),
        out_shape=jax.ShapeDtypeStruct((2, R, C), _WIRE),
        compiler_params=_cparams(("arbitrary", "arbitrary")),
    )(w)


def _adam_math(w, g, m, v):
    m2 = _B1 * m + (1.0 - _B1) * g
    v2 = _B2 * v + (1.0 - _B2) * (g * g)
    m_hat = m2 / (1.0 - _B1 ** _STEP)
    v_hat = v2 / (1.0 - _B2 ** _STEP)
    delta = -_LR * (m_hat / (jnp.sqrt(v_hat) + _AEPS) + _WD * w)
    return delta, m2, v2


def _adamw_big(name, w, m, v, layer, pair, kind, R, C, c, after, prev):
    tr = _half_tile(R, few_arrays=C <= 1024 and R % 512 == 0)
    nr = R // tr

    def body(c_ref, w_ref, m_ref, v_ref, own_ref, other_ref, *rest):
        go_ref, d_ref, mo_ref, vo_ref = rest[-4:]
        g = jnp.where(pl.program_id(0) == c_ref[0], own_ref[...], other_ref[...])
        delta, m2, v2 = _adam_math(w_ref[...], g, m_ref[...], v_ref[...])
        go_ref[...] = g
        d_ref[...] = delta
        mo_ref[...] = m2
        vo_ref[...] = v2

    if kind == "col":
        nat = pl.BlockSpec((None, tr, C), lambda h, i, c_ref: (layer, h * nr + i, 0))
    else:
        nat = pl.BlockSpec((None, tr, C), lambda h, i, c_ref: (layer, i, h))

    def gspec(own):
        return pl.BlockSpec((tr, C), lambda h, i, c_ref: (jnp.where((h == c_ref[0]) == own, i, 0), 0))

    carried = list(prev) if prev is not None else []
    gs = pltpu.PrefetchScalarGridSpec(
        num_scalar_prefetch=1, grid=(2, nr),
        in_specs=[nat, nat, nat, gspec(True), gspec(False), _any()] + [_any()] * len(carried),
        out_specs=[nat] * 4)
    sd = jax.ShapeDtypeStruct(w.shape, F32)
    return pl.pallas_call(
        body, name=name, grid_spec=gs, out_shape=[sd] * 4,
        input_output_aliases={7 + k: k for k in range(len(carried))},
        compiler_params=_cparams(("arbitrary", "arbitrary")),
    )(c, w, m, v, pair[0], pair[1], after, *carried)


def _adamw_small(ws, gs, ms, vs):
    n = len(ws)

    def body(*refs):
        for i in range(n):
            w_ref, g_ref, m_ref, v_ref, d_ref, mo_ref, vo_ref = refs[i::n]
            delta, m2, v2 = _adam_math(w_ref[...], g_ref[...], m_ref[...], v_ref[...])
            d_ref[...] = delta
            mo_ref[...] = m2
            vo_ref[...] = v2

    sds = [jax.ShapeDtypeStruct(w.shape, F32) for w in ws]
    outs = pl.pallas_call(
        body, name="adamw_small", in_specs=[_vm()] * (4 * n), out_specs=[_vm()] * (3 * n), out_shape=sds * 3,
        compiler_params=pltpu.CompilerParams(vmem_limit_bytes=_VMEM_LIMIT),
    )(*ws, *gs, *ms, *vs)
    return outs[:n], outs[n:2 * n], outs[2 * n:]


def _place():
    x, y, c = lax.axis_index("x"), lax.axis_index("y"), lax.axis_index("c")
    chips = [(1 - x, y), (x, 1 - y), (1 - x, 1 - y)]
    return x, y, c, 2 * x + y, chips


def _any():
    return pl.BlockSpec(memory_space=pl.ANY)


def _remote(src, dst, ssem, rsem, dev):
    return pltpu.make_async_remote_copy(src_ref=src, dst_ref=dst, send_sem=ssem, recv_sem=rsem,
                                        device_id=dev, device_id_type=_MESH)


def _allgather(name, shards):
    n = len(shards)

    def body(*refs):
        s_refs, g_refs = refs[:n], refs[n:2 * n]
        send0, recv0, send1, recv1, send2, recv2 = refs[2 * n:]
        x, y, c, q, _ = _place()
        sib, xn, yn = (x, y, 1 - c), (1 - x, y, c), (x, 1 - y, c)
        qx, qy, qd = 2 * (1 - x) + y, 2 * x + (1 - y), 2 * (1 - x) + (1 - y)
        _handshake([sib, xn, yn])
        own = [_remote(s_refs[a], g_refs[a].at[q], send0.at[a], recv0.at[a], sib) for a in range(n)]

        def pieces(a):
            g, half = g_refs[a], s_refs[a].shape[1] // 2
            return [g.at[qx, c], g.at[qy, c], g.at[qd, c, pl.ds(0, half)], g.at[qd, c, pl.ds(half, half)]]

        def relayed(a):
            g, half = g_refs[a], s_refs[a].shape[1] // 2
            return [(g.at[qx, c, pl.ds(0, half)], yn), (g.at[qy, c, pl.ds(half, half)], xn)]

        first = []
        for a in range(n):
            first.append(_remote(s_refs[a].at[c], g_refs[a].at[q, c], send1.at[4 * a], recv1.at[4 * a], xn))
            first.append(_remote(s_refs[a].at[c], g_refs[a].at[q, c], send1.at[4 * a + 1], recv1.at[4 * a + 1], yn))
        for cp in first + own:
            cp.start()
        later = []
        for a in range(n):
            land = pieces(a)
            for j in range(4):
                k = 4 * a + j
                _remote(land[j], land[j], send1.at[k], recv1.at[k], xn).wait_recv()
                if j < 2:
                    src, to = relayed(a)[j]
                    cp = _remote(src, src, send1.at[k + 2], recv1.at[k + 2], to)
                    cp.start()
                    later.append(cp)
                cp = _remote(land[j], land[j], send2.at[k], recv2.at[k], sib)
                cp.start()
                later.append(cp)
        for a in range(n):
            g, half = g_refs[a], s_refs[a].shape[1] // 2
            theirs = [g.at[qx, 1 - c], g.at[qy, 1 - c], g.at[qd, 1 - c, pl.ds(0, half)],
                      g.at[qd, 1 - c, pl.ds(half, half)]]
            for j in range(4):
                _remote(theirs[j], theirs[j], send2.at[4 * a + j], recv2.at[4 * a + j], sib).wait_recv()
        for cp in own:
            cp.wait()
        for cp in first + later:
            cp.wait_send()

    return _sequencer(name, _ID_GATHER, body, shards,
                      [jax.ShapeDtypeStruct((4,) + s.shape, s.dtype) for s in shards], [n, n] + [4 * n] * 4)


def _handshake(peers):
    barrier = pltpu.get_barrier_semaphore()
    for peer in peers:
        pl.semaphore_signal(barrier, inc=1, device_id=peer, device_id_type=_MESH)
    pl.semaphore_wait(barrier, len(peers))


_ID_SIBLING, _ID_CHIPS, _ID_GATHER, _ID_ALL = 1, 2, 3, 4


def _sequencer(name, collective_id, body, ins, out_types, sem_counts):
    mesh = plsc.ScalarSubcoreMesh(axis_name="seq", num_cores=1)
    moved = sum(math.prod(o.shape) * jnp.dtype(o.dtype).itemsize for o in out_types)
    return pl.kernel(
        body, name=name, out_type=out_types, mesh=mesh,
        scratch_types=[pltpu.SemaphoreType.DMA((k,)) for k in sem_counts],
        compiler_params=pltpu.CompilerParams(collective_id=collective_id),
        cost_estimate=pl.CostEstimate(flops=0, transcendentals=0, bytes_accessed=2 * moved,
                                      remote_bytes_transferred=moved),
    )(*ins)


def _swap_halves(name, grads):
    n = len(grads)

    def body(*refs):
        g_refs, t_refs = refs[:n], refs[n:2 * n]
        send, recv = refs[2 * n:]
        x, y, c, _, _ = _place()
        _handshake([(x, y, 1 - c)])
        cps = [_remote(g_refs[a].at[1 - c], t_refs[a], send.at[a], recv.at[a], (x, y, 1 - c)) for a in range(n)]
        for cp in cps:
            cp.start()
        for cp in cps:
            cp.wait()

    return _sequencer(name, _ID_SIBLING, body, grads,
                      [jax.ShapeDtypeStruct(g.shape[1:], g.dtype) for g in grads], [n, n])


def _chip_sum(name, g, t, after, core):
    _, _, R, C = g.shape
    tr = _half_tile(R, True)

    def body(c_ref, g_ref, t_ref, after_ref, o_ref):
        o_ref[...] = (g_ref[...].astype(F32) + t_ref[...].astype(F32)).astype(o_ref.dtype)

    gs = pltpu.PrefetchScalarGridSpec(
        num_scalar_prefetch=1, grid=(4, R // tr),
        in_specs=[pl.BlockSpec((None, None, tr, C), lambda r, i, c_ref: (c_ref[0], r, i, 0)),
                  pl.BlockSpec((None, tr, C), lambda r, i, c_ref: (r, i, 0)), _any()],
        out_specs=pl.BlockSpec((None, tr, C), lambda r, i, c_ref: (r, i, 0)))
    return pl.pallas_call(
        body, name=name, grid_spec=gs, out_shape=jax.ShapeDtypeStruct((4, R, C), _WIRE),
        compiler_params=_cparams(("arbitrary", "arbitrary")),
    )(core, g, t, after)


def _scatter_parts(name, parts):
    n = len(parts)

    def body(*refs):
        p_refs, t_refs = refs[:n], refs[n:2 * n]
        send, recv = refs[2 * n:]
        x, y, c, q, chips = _place()
        _handshake([(rx, ry, c) for rx, ry in chips])
        cps = []
        for a in range(n):
            for j, (rx, ry) in enumerate(chips):
                k = 3 * a + j
                cps.append(_remote(p_refs[a].at[2 * rx + ry], t_refs[a].at[q], send.at[k], recv.at[k], (rx, ry, c)))
        for cp in cps:
            cp.start()
        for a in range(n):
            for j, (rx, ry) in enumerate(chips):
                k = 3 * a + j
                land = t_refs[a].at[2 * rx + ry]
                _remote(land, land, send.at[k], recv.at[k], (rx, ry, c)).wait_recv()
        for cp in cps:
            cp.wait_send()

    return _sequencer(name, _ID_CHIPS, body, parts,
                      [jax.ShapeDtypeStruct(p.shape, p.dtype) for p in parts], [3 * n, 3 * n])


def _sum_parts(name, p, t, where, after):
    _, R, C = t.shape
    tr = _half_tile(R, True)

    def body(w_ref, p_ref, t0_ref, t1_ref, t2_ref, after_ref, o_ref):
        o_ref[...] = (p_ref[...].astype(F32) + t0_ref[...].astype(F32)
                      + t1_ref[...].astype(F32) + t2_ref[...].astype(F32))

    def part(slot):
        return pl.BlockSpec((None, tr, C), lambda i, w_ref: (w_ref[slot], i, 0))

    gs = pltpu.PrefetchScalarGridSpec(
        num_scalar_prefetch=1, grid=(R // tr,), in_specs=[part(0), part(1), part(2), part(3), _any()],
        out_specs=pl.BlockSpec((tr, C), lambda i, w_ref: (i, 0)))
    return pl.pallas_call(
        body, name=name, grid_spec=gs, out_shape=jax.ShapeDtypeStruct((R, C), F32),
        compiler_params=_cparams(("arbitrary",)),
    )(where, p, t, t, t, after)


def _send_halves(name, halves):
    n = len(halves)

    def body(*refs):
        h_refs, o_refs = refs[:n], refs[n:2 * n]
        send, recv = refs[2 * n:]
        x, y, c, _, _ = _place()
        _handshake([(x, y, 1 - c)])
        cps = [_remote(h_refs[a], o_refs[a], send.at[a], recv.at[a], (x, y, 1 - c)) for a in range(n)]
        for cp in cps:
            cp.start()
        for cp in cps:
            cp.wait()

    return _sequencer(name, _ID_SIBLING, body, halves,
                      [jax.ShapeDtypeStruct(h.shape, h.dtype) for h in halves], [n, n])


class _Order:
    def __init__(self):
        self.tok = None
        x, y, c, q, chips = _place()
        self.core = c.astype(jnp.int32).reshape(1)
        self.where = jnp.stack([q] + [2 * rx + ry for rx, ry in chips]).astype(jnp.int32)

    def tie(self, x):
        return x if self.tok is None else lax.optimization_barrier((x, self.tok))[0]

    def done(self, outs):
        self.tok = outs[0]
        return outs


class _Reduction:
    def __init__(self, tag, grads):
        self.tag, self.grads = tag, grads
        self.swapped = _swap_halves(f"rs_swap_{tag}", grads)

    def scatter(self, seq):
        self.parts = [seq.done([_chip_sum(f"rs_chipsum_{self.tag}_{a}", g, t, seq.tok, seq.core)])[0]
                      for a, (g, t) in enumerate(zip(self.grads, self.swapped))]
        self.landed = _scatter_parts(f"rs_scatter_{self.tag}", self.parts)

    def finish(self, seq):
        halves = [seq.done([_sum_parts(f"rs_sum_{self.tag}_{a}", p, t, seq.where, seq.tok)])[0]
                  for a, (p, t) in enumerate(zip(self.parts, self.landed))]
        return list(zip(halves, _send_halves(f"rs_join_{self.tag}", halves)))


def _allreduce_small(name, v):
    _, R, _ = v.shape

    def body(v_ref, o_ref, land, acc, send1, recv1, send2, recv2):
        x, y, c = lax.axis_index("x"), lax.axis_index("y"), lax.axis_index("c")
        me = 4 * x + 2 * y + c
        peers = []
        for k in range(1, 8):
            dx, dy, dc = (k >> 2) & 1, (k >> 1) & 1, k & 1
            px, py, pc = (1 - x if dx else x), (1 - y if dy else y), (1 - c if dc else c)
            peers.append((k, (px, py, pc), 4 * px + 2 * py + pc))
        land[me] = v_ref[me]
        out1 = [_remote(v_ref.at[pid], land.at[me], send1.at[k], recv1.at[k], dev) for k, dev, pid in peers]
        for cp in out1:
            cp.start()
        for k, dev, pid in peers:
            _remote(land.at[pid], land.at[pid], send1.at[k], recv1.at[k], dev).wait_recv()
        total = land[0]
        for j in range(1, 8):
            total = total + land[j]
        acc[...] = total
        o_ref[me] = total
        out2 = [_remote(acc, o_ref.at[me], send2.at[k], recv2.at[k], dev) for k, dev, pid in peers]
        for cp in out2:
            cp.start()
        for k, dev, pid in peers:
            _remote(o_ref.at[pid], o_ref.at[pid], send2.at[k], recv2.at[k], dev).wait_recv()
        for cp in out1 + out2:
            cp.wait_send()

    return pl.pallas_call(
        body, name=name, in_specs=[_vm()], out_specs=_vm(),
        out_shape=jax.ShapeDtypeStruct(v.shape, F32),
        scratch_shapes=[pltpu.VMEM(v.shape, F32), pltpu.VMEM((R, _LANES), F32)]
        + [pltpu.SemaphoreType.DMA((8,)) for _ in range(4)],
        compiler_params=pltpu.CompilerParams(vmem_limit_bytes=_VMEM_LIMIT),
    )(v)


def _all_peers():
    x, y, c = lax.axis_index("x"), lax.axis_index("y"), lax.axis_index("c")
    peers = []
    for k in range(1, 8):
        px, py, pc = (1 - x if k & 4 else x), (1 - y if k & 2 else y), (1 - c if k & 1 else c)
        peers.append((k, (px, py, pc), 4 * px + 2 * py + pc))
    return 4 * x + 2 * y + c, peers


def _exchange_slices(name, v):
    def body(v_ref, land, send, recv):
        me, peers = _all_peers()
        _handshake([dev for _, dev, _ in peers])
        cps = [_remote(v_ref.at[pid], land.at[me], send.at[k], recv.at[k], dev) for k, dev, pid in peers]
        for cp in cps:
            cp.start()
        for k, dev, pid in peers:
            _remote(land.at[pid], land.at[pid], send.at[k], recv.at[k], dev).wait_recv()
        for cp in cps:
            cp.wait_send()

    return _sequencer(name, _ID_ALL, body, [v], [jax.ShapeDtypeStruct(v.shape, v.dtype)], [8, 8])[0]


def _sum_slices(name, v, landed, after):
    _, R, _ = v.shape

    def body(v_ref, land_ref, after_ref, o_ref):
        me, peers = _all_peers()
        acc = v_ref[me]
        for _, _, pid in peers:
            acc = acc + land_ref[pid]
        o_ref[...] = acc

    return pl.pallas_call(
        body, name=name, in_specs=[_vm(), _vm(), _any()], out_specs=_vm(),
        out_shape=jax.ShapeDtypeStruct((R, _LANES), F32),
        compiler_params=pltpu.CompilerParams(vmem_limit_bytes=_VMEM_LIMIT),
    )(v, landed, after)


def _broadcast_slices(name, s):
    def body(s_ref, out, send, recv):
        me, peers = _all_peers()
        _handshake([dev for _, dev, _ in peers])
        cps = [_remote(s_ref, out.at[me], send.at[k], recv.at[k], dev) for k, dev, pid in peers]
        for cp in cps:
            cp.start()
        for k, dev, pid in peers:
            _remote(out.at[pid], out.at[pid], send.at[k], recv.at[k], dev).wait_recv()
        for cp in cps:
            cp.wait_send()

    return _sequencer(name, _ID_ALL, body, [s], [jax.ShapeDtypeStruct((8,) + s.shape, s.dtype)], [8, 8])[0]


_WEIGHT_NAMES = ['norm_mix', 'norm_ffn', 'norm_ple', 'norm_final', 'gm_w_in', 'gm_ln_g', 'gm_ln_b', 'gm_w_s',
                 'gm_b_s', 'gm_w_out', 's5_w_in', 's5_a_re', 's5_a_im', 's5_log_dt', 's5_b_re', 's5_b_im',
                 's5_c_re', 's5_c_im', 's5_d', 's5_w_out', 'ffn_w1', 'ffn_w3', 'ffn_w2', 'ple_w_gate', 'ple_w_proj']
_BIG = {'gm_w_in': 'col', 'gm_w_out': 'row', 's5_w_in': 'row', 's5_w_out': 'col', 'ffn_w1': 'col',
        'ffn_w3': 'col', 'ffn_w2': 'row', 'ple_w_gate': 'row', 'ple_w_proj': 'col'}


_VIEW = {'s5_a_re': (0, 2, 1), 's5_a_im': (0, 2, 1), 's5_b_re': (0, 2, 3, 1), 's5_b_im': (0, 2, 3, 1),
         's5_c_re': (0, 2, 3, 1), 's5_c_im': (0, 2, 3, 1)}


def _to_view(name, a):
    return jnp.transpose(a, _VIEW[name]) if name in _VIEW else a


def _from_view(name, a):
    if name not in _VIEW:
        return a
    perm = _VIEW[name]
    return jnp.transpose(a, [perm.index(i) for i in range(len(perm))])


def _rc(kind, shard_shape):
    rows, cols = shard_shape[-2:]
    return (rows // 2, cols) if kind == "col" else (rows, cols // 2)


def _pack(vecs, rows_multiple):
    flat = jnp.concatenate([a.reshape(-1).astype(F32) for a in vecs])
    unit = rows_multiple * _LANES
    pad = (-flat.shape[0]) % unit
    return jnp.pad(flat, (0, pad)).reshape(-1, _LANES)


def _unpack(buf, shapes):
    flat = buf.reshape(-1)
    out, off = [], 0
    for s in shapes:
        n = math.prod(s)
        out.append(flat[off:off + n].reshape(s))
        off += n
    return out


def _ident(accs, ex):
    return accs


def _add_resid(accs, ex):
    return [accs[0] + ex[0]]


def _swiglu_epi(accs, ex):
    a, b = accs
    return [a, b, a * _sig(a) * b]


def _swiglu_bwd_epi(accs, ex):
    df = accs[0]
    a, b = ex[0].astype(F32), ex[1].astype(F32)
    sa = _sig(a)
    return [df * b * (sa * (1.0 + a * (1.0 - sa))), df * (a * sa)]


def _ple_epi(accs, ex):
    xin, pv = ex[0], ex[1]
    kh = pv.shape[1] // 2
    pp = jnp.concatenate([_dot(pv[:, :kh], ex[2 + 2 * part], _NN) + _dot(pv[:, kh:], ex[3 + 2 * part], _NN)
                          for part in (0, 1)], axis=1)
    gt = _sig(accs[0])
    return [xin + gt * pp, gt, pp]


def _glu_epi(accs, ex):
    val, sg = accs[0], _sig(accs[1])
    return [ex[0] + val * sg, val, sg]


def kernel(x, p, norm_mix, norm_ffn, norm_ple, norm_final, gm_w_in, gm_ln_g, gm_ln_b, gm_w_s, gm_b_s, gm_w_out, s5_w_in, s5_a_re, s5_a_im, s5_log_dt, s5_b_re, s5_b_im, s5_c_re, s5_c_im, s5_d, s5_w_out, ffn_w1, ffn_w3, ffn_w2, ple_w_gate, ple_w_proj, loss_target, m_norm_mix, m_norm_ffn, m_norm_ple, m_norm_final, m_gm_w_in, m_gm_ln_g, m_gm_ln_b, m_gm_w_s, m_gm_b_s, m_gm_w_out, m_s5_w_in, m_s5_a_re, m_s5_a_im, m_s5_log_dt, m_s5_b_re, m_s5_b_im, m_s5_c_re, m_s5_c_im, m_s5_d, m_s5_w_out, m_ffn_w1, m_ffn_w3, m_ffn_w2, m_ple_w_gate, m_ple_w_proj, v_norm_mix, v_norm_ffn, v_norm_ple, v_norm_final, v_gm_w_in, v_gm_ln_g, v_gm_ln_b, v_gm_w_s, v_gm_b_s, v_gm_w_out, v_s5_w_in, v_s5_a_re, v_s5_a_im, v_s5_log_dt, v_s5_b_re, v_s5_b_im, v_s5_c_re, v_s5_c_im, v_s5_d, v_s5_w_out, v_ffn_w1, v_ffn_w3, v_ffn_w2, v_ple_w_gate, v_ple_w_proj):
    env = dict(locals())
    w = {n: env[n] for n in _WEIGHT_NAMES}
    mom = {n: env["m_" + n] for n in _WEIGHT_NAMES}
    var = {n: env["v_" + n] for n in _WEIGHT_NAMES}
    xs, tgt = x[0], loss_target[0]
    L, D = xs.shape
    depth = norm_mix.shape[0]
    qx, qy = lax.axis_index("x"), lax.axis_index("y")
    q = 2 * qx + qy

    W, last_cast = {}, [None]

    def gather(tag, items, after=None):
        shards = []
        for name, layer in items:
            kind = _BIG[name]
            R, C = _rc(kind, w[name].shape)
            src = w[name] if after is None else lax.optimization_barrier((w[name], after))[0]
            shards.append(_cast_shard(f"cast_{name}{layer}", src, layer, kind, R, C))
        last_cast[0] = shards[-1]
        full = _allgather(f"ag_{tag}", shards)
        W.update({it: _W(f, _BIG[it[0]]) for it, f in zip(items, full)})

    gather("gm_w_in", [("gm_w_in", 0)])
    gather("gm_w_out", [("gm_w_out", 0)])
    gather("ffn_up0", [("ffn_w1", 0), ("ffn_w3", 0)])
    gather("ffn_down0", [("ffn_w2", 0)])

    def gather_rest(after):
        gather("ple0", [("ple_w_gate", 0), ("ple_w_proj", 0)], after)
        gather("s5_w_in", [("s5_w_in", 0)], after)
        gather("s5_w_out", [("s5_w_out", 0)], after)
        gather("ffn_up1", [("ffn_w1", 1), ("ffn_w3", 1)], after)
        gather("ffn_down1", [("ffn_w2", 1)], after)
        gather("ple1", [("ple_w_gate", 1), ("ple_w_proj", 1)], after)

    d_slots = jnp.zeros((4, D // 4), F32)
    d_slots = lax.dynamic_update_slice(d_slots, s5_d.astype(F32), (q, 0))
    d_sum = _allreduce_small("ar_s5_d", _pack([d_slots], 64).reshape(8, -1, _LANES))
    d_full = (d_sum.reshape(-1)[:D] * 0.5).reshape(1, D)

    def ffn_fwd(i, xin):
        hf = _rms_fwd(f"rms_ffn{i}", xin, norm_ffn[i:i + 1])
        a, b, f = _mm_nn(f"ffn_up{i}", hf, [W["ffn_w1", i], W["ffn_w3", i]], 1024, 1408, ffn_w2.shape[1] * 4,
                         [], [_MXU, _MXU, _MXU], _swiglu_epi, tm=1024)
        xo = _mm_nn(f"ffn_down{i}", f, [W["ffn_w2", i]], 1408, 1024, D, [xin], [F32], _add_resid, tm=1024)[0]
        return xo, (xin, hf, a, b, f)

    def ple_fwd(i, xin):
        hp = _rms_fwd(f"rms_ple{i}", xin, norm_ple[i:i + 1])
        pi, wp = p[i, 0], W["ple_w_proj", i]

        def more(tm):
            ops = [(pi, pl.BlockSpec((tm, pi.shape[1]), lambda i_, j, k: (i_, 0)))]
            for part in (0, 1):
                for half in (0, 1):
                    ops.append((wp.a, pl.BlockSpec((None, None, wp.R, wp.C),
                                                   lambda i_, j, k, part=part, half=half: (2 * j + part, half, 0, 0))))
            return ops

        xo, gt, pp = _mm_nn(f"ple_gate{i}", hp, [W["ple_w_gate", i]], 512, 2 * wp.C, D, [xin], [F32, _MXU, F32],
                            _ple_epi, tm=1024, more=more)
        return xo, (xin, hp, pi, pp, gt)

    h0 = _rms_fwd("rms_mix0", xs, norm_mix[0:1])
    z = _mm_nn("gm_in", h0, [W["gm_w_in", 0]], 1024, 1024, 2 * D, [], [F32], _ident, tm=1024)[0]
    bsT = gm_b_s[0].T
    gm_m = _gmlp_fwd(z, gm_ln_g, gm_ln_b, gm_w_s[0], bsT)
    x1 = _mm_nn("gm_out", gm_m, [W["gm_w_out", 0]], 512, 1024, D, [xs], [F32], _add_resid, tm=2048)[0]
    gather_rest(x1)
    x1 = lax.optimization_barrier((x1, last_cast[0]))[0]
    x2, ffn0 = ffn_fwd(0, x1)
    x3, ple0 = ple_fwd(0, x2)

    T = D // _LANES
    lanes = _S5_GT * _S5_P
    sv = {n: _to_view(n, w[n])[0] for n in _VIEW}
    a_re, a_im, log_dt = sv["s5_a_re"], sv["s5_a_im"], s5_log_dt
    lbr, lbi, Bbar_re, Bbar_im = _s5_prep(a_re, a_im, log_dt, sv["s5_b_re"], sv["s5_b_im"])

    def to_bd(B):
        return jnp.transpose(B.reshape(_S5_P, _S5_C, T, _S5_GT), (2, 3, 1, 0))

    def to_cd(cw):
        return jnp.transpose(cw.reshape(_S5_C, _S5_P, T, _S5_GT), (2, 3, 1, 0))

    def to_lam(v):
        return jnp.transpose(v).reshape(T, 1, lanes)

    bd_re, bd_im = to_bd(Bbar_re), to_bd(Bbar_im)
    cd_re, cd_im = to_cd(sv["s5_c_re"]), to_cd(sv["s5_c_im"])
    lam_re, lam_im = to_lam(lbr), to_lam(lbi)

    h1 = _rms_fwd("rms_mix1", x3, norm_mix[1:2])
    u = _mm_nn("s5_in", h1, [W["s5_w_in", 0]], 512, 1024, D, [], [F32], _ident, tm=2048)[0]
    s5_g, s5_re, s5_im = _s5_fwd(u, bd_re, bd_im, cd_re, cd_im, lam_re, lam_im, d_full)
    x4, glu_val, glu_sg = _mm_nn("s5_out", s5_g, [W["s5_w_out", 0], W["s5_w_out", 0]], 1024, 1024, D, [x3],
                                 [F32, _MXU, _MXU], _glu_epi, tm=1024, cb_offsets=[0, 2])
    x5, ffn1 = ffn_fwd(1, x4)
    x6, ple1 = ple_fwd(1, x5)

    dx, dpre1, dpp1, d_norm_final, loss_rows = _loss_head(x6, norm_final[None], tgt, ple1[3], ple1[4])

    small = {}

    seq = _Order()
    tie, done = seq.tie, seq.done
    d_norm_ple, d_norm_ffn, d_norm_mix = [None] * depth, [None] * depth, [None] * depth
    reduced = {}

    def keep(names, layer, pairs):
        for n, pr in zip(names, pairs):
            reduced[n, layer] = pr

    ple_names, up_names, down_names = ["ple_w_gate", "ple_w_proj"], ["ffn_w1", "ffn_w3"], ["ffn_w2"]

    def ple_bwd(i, dxo, saved, elem=None):
        xin, hp, pi, pp, gt = saved
        dpre, dpp = elem or done(_ple_bwd_elem(tie(dxo), pp, gt))
        dwg = done(_mm_tn(f"ple_gate_dw{i}", tie(hp), [dpre], "row", 512, 1024, 512, 1024))[0]
        dwp = done(_mm_tn(f"ple_proj_dw{i}", tie(pi), [dpp], "col", 128, 512, 128, 512))[0]
        red = _Reduction(f"ple{i}", [dwg, dwp])
        dhp = done(_mm_nt(f"ple_gate_dx{i}", [tie(dpre)], [W["ple_w_gate", i]], 512, 1024, [], [F32], _ident,
                          tm=2048))[0]
        dxin, dxin_mxu, dg = done(_rms_bwd(f"rms_ple_bwd{i}", tie(dhp), xin, norm_ple[i:i + 1], dxo))
        return dxin, dxin_mxu, dg, red

    def ffn_bwd(i, dxo, dxo_mxu, saved, before_up):
        xin, hf, a, b, f = saved
        dw2 = done(_mm_tn(f"ffn_down_dw{i}", tie(f), [dxo_mxu], "row", 1408, 1024, 1408, 1024))[0]
        r_down = _Reduction(f"ffd{i}", [dw2])
        da, db = done(_mm_nt(f"ffn_down_dx{i}", [tie(dxo_mxu)], [W["ffn_w2", i]], 1408, 1024, [a, b], [_MXU, _MXU],
                             _swiglu_bwd_epi, tm=1024))
        for step in before_up:
            step()
        r_down.scatter(seq)
        dw1, dw3 = done(_mm_tn(f"ffn_up_dw{i}", tie(hf), [da, db], "col", 1024, 1408, 1024, 1408))
        r_up = _Reduction(f"ffu{i}", [dw1, dw3])
        dhf = done(_mm_nt(f"ffn_up_dx{i}", [tie(da), db], [W["ffn_w1", i], W["ffn_w3", i]], 1024, 1408, [], [F32],
                          _ident, tm=1024))[0]
        dxin, dxin_mxu, dg = done(_rms_bwd(f"rms_ffn_bwd{i}", tie(dhf), xin, norm_ffn[i:i + 1], dxo))
        r_up.scatter(seq)
        return dxin, dxin_mxu, dg, r_down, r_up

    dx, dx_mxu, d_norm_ple[1], r_ple1 = ple_bwd(1, dx, ple1, (dpre1, dpp1))
    dx, _, d_norm_ffn[1], r_down1, r_up1 = ffn_bwd(1, dx, dx_mxu, ffn1, [lambda: r_ple1.scatter(seq)])

    do = done([_glu_bwd_elem(tie(dx), glu_val, glu_sg)])[0]
    dw_s5out = done(_mm_tn("s5_out_dw", tie(s5_g), [do], "col", 1024, 1024, 1024, 1024))[0]
    r_s5out = _Reduction("s5out", [dw_s5out])
    dgy = done(_mm_nt("s5_out_dx", [tie(do)], [W["s5_w_out", 0]], 1024, 1024, [], [F32], _ident, tm=1024))[0]
    keep(ple_names, 1, r_ple1.finish(seq))
    keep(down_names, 1, r_down1.finish(seq))
    du, dbd_re, dbd_im, dcd_re, dcd_im, dl_re, dl_im, dd = done(_s5_bwd(
        tie(u), dgy, s5_re, s5_im, bd_re, bd_im, cd_re, cd_im, lam_re, lam_im, d_full))
    r_s5out.scatter(seq)
    dw_s5in = done(_mm_tn("s5_in_dw", tie(h1), [du], "row", 512, 1024, 512, 1024))[0]
    r_s5in = _Reduction("s5in", [dw_s5in])
    dh1 = done(_mm_nt("s5_in_dx", [tie(du)], [W["s5_w_in", 0]], 512, 1024, [], [F32], _ident, tm=2048))[0]
    dx, _, d_norm_mix[1] = done(_rms_bwd("rms_mix1_bwd", tie(dh1), x3, norm_mix[1:2], dx))
    keep(up_names, 1, r_up1.finish(seq))
    r_s5in.scatter(seq)

    def from_bd(t):
        return jnp.transpose(t, (3, 2, 0, 1)).reshape(_S5_P, _S5_C, T * _S5_GT)

    def from_cdT(t):
        return jnp.transpose(t, (2, 3, 0, 1)).reshape(_S5_C, _S5_P, T * _S5_GT)

    def from_lam(t):
        return jnp.transpose(t.reshape(T * _S5_GT, _S5_P))

    da_re, da_im, dlog_dt, db_re, db_im = _s5_prep_bwd(
        a_re, a_im, log_dt, sv["s5_b_re"], sv["s5_b_im"], from_lam(dl_re), from_lam(dl_im),
        from_bd(dbd_re), from_bd(dbd_im))
    small["s5_a_re"], small["s5_a_im"], small["s5_log_dt"] = da_re[None], da_im[None], dlog_dt
    small["s5_b_re"], small["s5_b_im"] = db_re[None], db_im[None]
    small["s5_c_re"], small["s5_c_im"] = from_cdT(dcd_re)[None], from_cdT(dcd_im)[None]

    dx, dx_mxu, d_norm_ple[0], r_ple0 = ple_bwd(0, dx, ple0)
    keep(["s5_w_out"], 0, r_s5out.finish(seq))
    xin0, hf0, a0, b0, f0 = ffn0
    da0, db0 = done(_mm_nt("ffn_down_dx0", [tie(dx_mxu)], [W["ffn_w2", 0]], 1408, 1024, [a0, b0], [_MXU, _MXU],
                           _swiglu_bwd_epi, tm=1024))
    r_ple0.scatter(seq)
    dw1, dw3 = done(_mm_tn("ffn_up_dw0", tie(hf0), [da0, db0], "col", 1024, 1408, 1024, 1408))
    r_up0 = _Reduction("ffu0", [dw1, dw3])
    keep(["s5_w_in"], 0, r_s5in.finish(seq))
    dw2 = done(_mm_tn("ffn_down_dw0", tie(f0), [dx_mxu], "row", 1408, 1024, 1408, 1024))[0]
    r_down0 = _Reduction("ffd0", [dw2])
    r_up0.scatter(seq)
    dhf0 = done(_mm_nt("ffn_up_dx0", [tie(da0), db0], [W["ffn_w1", 0], W["ffn_w3", 0]], 1024, 1408, [], [F32], _ident,
                       tm=1024))[0]
    dx, dx_mxu, d_norm_ffn[0] = done(_rms_bwd("rms_ffn_bwd0", tie(dhf0), xin0, norm_ffn[0:1], dx))
    keep(ple_names, 0, r_ple0.finish(seq))
    r_down0.scatter(seq)

    dw_gmout = done(_mm_tn("gm_out_dw", tie(gm_m), [dx_mxu], "row", 512, 1024, 512, 1024))[0]
    r_gmout = _Reduction("gmout", [dw_gmout])
    dgm = done(_mm_nt("gm_out_dx", [tie(dx_mxu)], [W["gm_w_out", 0]], 512, 1024, [], [F32], _ident, tm=2048))[0]
    dz, dws, dbsT, dlng, dlnb = done(_gmlp_bwd(tie(z), dgm, gm_ln_g, gm_ln_b, gm_w_s[0], bsT))
    dw_gmin = done(_mm_tn("gm_in_dw", tie(h0), [dz], "col", 1024, 1024, 1024, 1024))[0]
    r_gmin = _Reduction("gmin", [dw_gmin])
    dh0 = done(_mm_nt("gm_in_dx", [tie(dz)], [W["gm_w_in", 0]], 1024, 1024, [], [F32], _ident, tm=1024))[0]
    dx, _, d_norm_mix[0] = done(_rms_bwd("rms_mix0_bwd", tie(dh0), xs, norm_mix[0:1], dx))
    grad_x = dx[None]

    small["norm_mix"], small["norm_ffn"] = jnp.concatenate(d_norm_mix), jnp.concatenate(d_norm_ffn)
    small["norm_ple"], small["norm_final"] = jnp.concatenate(d_norm_ple), d_norm_final[0]
    small["gm_ln_g"], small["gm_ln_b"], small["gm_w_s"] = dlng, dlnb, dws[None]
    small["gm_b_s"] = dbsT[:, :_GM_HEADS].T[None]
    small["s5_d"] = dd

    small_names = [n for n in _WEIGHT_NAMES if n not in _BIG]
    packed = _pack([small[n] for n in small_names] + [loss_rows[:, :1]], 64).reshape(8, -1, _LANES)
    grads, deltas, new_m, new_v = {}, {}, {}, {}
    my_c = lax.axis_index("c")

    def adamw(n, layer, prev):
        kind = _BIG[n]
        R, C = _rc(kind, w[n].shape)
        return done(_adamw_big(f"adamw_{n}{layer}", w[n], mom[n], var[n], layer, reduced[n, layer], kind, R, C,
                               seq.core, seq.tok, prev))

    def adamw_last(names):
        for n in names:
            grads[n], deltas[n], new_m[n], new_v[n] = adamw(n, 0, late.get(n))

    late = {}
    for n in down_names + ple_names:
        late[n] = adamw(n, 1, None)
    keep(up_names, 0, r_up0.finish(seq))
    r_gmout.scatter(seq)
    r_gmin.scatter(seq)
    landed = _exchange_slices("ar_small_in", packed)
    for n in up_names:
        late[n] = adamw(n, 1, None)
    adamw_last(["s5_w_in", "s5_w_out"] + ple_names)
    keep(down_names, 0, r_down0.finish(seq))
    adamw_last(up_names)
    keep(["gm_w_out"], 0, r_gmout.finish(seq))
    mine = done([_sum_slices("ar_small_sum", packed, landed, seq.tok)])[0]
    spread = _broadcast_slices("ar_small_out", mine)
    adamw_last(down_names)
    keep(["gm_w_in"], 0, r_gmin.finish(seq))
    adamw_last(["gm_w_out", "gm_w_in"])
    summed = lax.dynamic_update_slice(spread, mine[None], (4 * qx + 2 * qy + my_c, 0, 0))
    *red_list, loss_sum = _unpack(summed, [small[n].shape for n in small_names] + [(1, 1)])
    red_small, loss = dict(zip(small_names, red_list)), loss_sum.reshape(())
    red_small["s5_d"] = lax.dynamic_slice(red_small["s5_d"], (0, q * (D // 4)), (1, D // 4))

    def two_d(a):
        return a.reshape((1,) * (2 - a.ndim) + a.shape)

    def views(src):
        return [two_d(_to_view(n, src[n])) for n in small_names]

    g_views = [two_d(red_small[n]) for n in small_names]
    dl, mo, vo = _adamw_small(views(w), g_views, views(mom), views(var))
    for n, g_, d_, m_, v_ in zip(small_names, g_views, dl, mo, vo):
        grads[n], deltas[n], new_m[n], new_v[n] = [_from_view(n, t_.reshape(_to_view(n, w[n]).shape)).reshape(w[n].shape)
                                                  for t_ in (g_, d_, m_, v_)]

    return (loss, grad_x, *[grads[n] for n in _WEIGHT_NAMES], *[deltas[n] for n in _WEIGHT_NAMES],
            *[new_m[n] for n in _WEIGHT_NAMES], *[new_v[n] for n in _WEIGHT_NAMES])
```

```python
import math

import jax
import jax.numpy as jnp
from jax import lax
from jax.experimental import pallas as pl
from jax.experimental.pallas import tpu as pltpu
from jax.experimental.pallas import tpu_sc as plsc

F32 = jnp.float32
_MXU = jnp.bfloat16
_WIRE = jnp.bfloat16
_EPS = 1e-6
_VMEM_LIMIT = 56 * 1024 * 1024
_LANES = 128
_MESH = pl.DeviceIdType.MESH

_LR, _B1, _B2, _AEPS, _WD, _STEP = 0.001, 0.9, 0.999, 1e-08, 0.01, 10

_GM_CHUNK = 128
_GM_HEADS = 16
_S5_GT = 8
_S5_P = 64
_S5_C = 16

_NN = (((1,), (0,)), ((), ()))
_NT = (((1,), (1,)), ((), ()))
_TN = (((0,), (0,)), ((), ()))


def _cparams(sem):
    return pltpu.CompilerParams(dimension_semantics=sem, vmem_limit_bytes=_VMEM_LIMIT)


def _sig(x):
    return 0.5 * jnp.tanh(0.5 * x) + 0.5


_GC = math.sqrt(2.0 / math.pi)


def _gelu(x):
    return 0.5 * x * (1.0 + jnp.tanh(_GC * (x + 0.044715 * (x * x * x))))


def _gelu_grad(x):
    t = jnp.tanh(_GC * (x + 0.044715 * (x * x * x)))
    return 0.5 * (1.0 + t) + 0.5 * x * (1.0 - t * t) * (_GC * (1.0 + 3.0 * 0.044715 * x * x))


def _dot(a, b, dn):
    return lax.dot_general(a.astype(_MXU), b.astype(_MXU), dn, preferred_element_type=F32)


class _W:
    def __init__(self, arr, kind):
        self.a, self.kind = arr, kind
        self.R, self.C = arr.shape[2], arr.shape[3]

    def full_shape(self):
        return (2 * self.R, 4 * self.C) if self.kind == "col" else (4 * self.R, 2 * self.C)


def _part_index(kind, R, C, tr, tc, rb, cb):
    nr, nc = R // tr, C // tc
    if kind == "col":
        return cb // nc, rb // nr, rb % nr, cb % nc
    return rb // nr, cb // nc, rb % nr, cb % nc


def _wspec(w, tr, tc, rb_fn, cb_fn):
    assert w.R % tr == 0 and w.C % tc == 0, (w.R, w.C, tr, tc)

    def imap(i, j, k):
        return _part_index(w.kind, w.R, w.C, tr, tc, rb_fn(i, j, k), cb_fn(i, j, k))

    return pl.BlockSpec((None, None, tr, tc), imap)


def _gspec(kind, R, C, tr, tc):
    assert R % tr == 0 and C % tc == 0, (R, C, tr, tc)

    def imap(i, j, k):
        part, half, rbi, cbi = _part_index(kind, R, C, tr, tc, i, j)
        return half, part, rbi, cbi

    return pl.BlockSpec((None, None, tr, tc), imap)


def _mm(name, grid, a_ops, b_ops, pairs, acc_shape, n_acc, extras, outs, epilogue):
    nk = grid[2]
    na, nb, ne, no = len(a_ops), len(b_ops), len(extras), len(outs)

    def body(*refs):
        a_refs = refs[:na]
        b_refs = refs[na:na + nb]
        e_refs = refs[na + nb:na + nb + ne]
        o_refs = refs[na + nb + ne:na + nb + ne + no]
        acc_refs = refs[na + nb + ne + no:]
        k = pl.program_id(2)

        def products():
            sums = [None] * n_acc
            for ai, bi, ci, dn in pairs:
                d = _dot(a_refs[ai][...], b_refs[bi][...], dn)
                sums[ci] = d if sums[ci] is None else sums[ci] + d
            return sums

        def finish(accs):
            res = epilogue(accs, [e[...] for e in e_refs])
            for o, r in zip(o_refs, res):
                o[...] = r.astype(o.dtype)

        if nk == 1:
            finish(products())
            return

        @pl.when(k == 0)
        def _():
            for acc, d in zip(acc_refs, products()):
                acc[...] = d

        @pl.when(jnp.logical_and(k > 0, k < nk - 1))
        def _():
            for acc, d in zip(acc_refs, products()):
                acc[...] += d

        @pl.when(k == nk - 1)
        def _():
            finish([acc[...] + d for acc, d in zip(acc_refs, products())])

    ops = list(a_ops) + list(b_ops) + list(extras)
    return pl.pallas_call(
        body, name=name, grid=grid,
        in_specs=[s for _, s in ops],
        out_specs=[s for _, s in outs],
        out_shape=[s for s, _ in outs],
        scratch_shapes=[pltpu.VMEM(acc_shape, F32) for _ in range(n_acc if nk > 1 else 0)],
        compiler_params=_cparams(("parallel", "parallel", "arbitrary")),
    )(*[a for a, _ in ops])


def _bs(shape, fn):
    return pl.BlockSpec(shape, fn)


def _tile_m(L):
    return min(L, 512)


def _mm_nn(name, x, ws, tk, tn, n_out, extras, outs_sd, epilogue, tm=None, cb_offsets=None, more=None):
    M, K = x.shape
    tm = min(M, tm or _tile_m(M))
    grid = (M // tm, n_out // tn, K // tk)
    a_ops = [(x, _bs((tm, tk), lambda i, j, k: (i, k)))]
    cb_offsets = cb_offsets or [0] * len(ws)
    b_ops = [(w.a, _wspec(w, tk, tn, lambda i, j, k: k, (lambda off: lambda i, j, k: j + off)(off)))
             for w, off in zip(ws, cb_offsets)]
    pairs = [(0, bi, bi, _NN) for bi in range(len(ws))]
    mn = _bs((tm, tn), lambda i, j, k: (i, j))
    ex = [(e, mn) for e in extras] + (more(tm) if more else [])
    outs = [(jax.ShapeDtypeStruct((M, n_out), dt), mn) for dt in outs_sd]
    return _mm(name, grid, a_ops, b_ops, pairs, (tm, tn), len(ws), ex, outs, epilogue)


def _mm_nt(name, xs, ws, tn, tk, extras, outs_sd, epilogue, tm=None):
    M, Nw = xs[0].shape
    Kw = ws[0].full_shape()[0]
    tm = min(M, tm or _tile_m(M))
    grid = (M // tm, Kw // tn, Nw // tk)
    a_ops = [(x, _bs((tm, tk), lambda i, j, k: (i, k))) for x in xs]
    b_ops = [(w.a, _wspec(w, tn, tk, lambda i, j, k: j, lambda i, j, k: k)) for w in ws]
    pairs = [(i, i, 0, _NT) for i in range(len(ws))]
    mn = _bs((tm, tn), lambda i, j, k: (i, j))
    ex = [(e, mn) for e in extras]
    outs = [(jax.ShapeDtypeStruct((M, Kw), dt), mn) for dt in outs_sd]
    return _mm(name, grid, a_ops, b_ops, pairs, (tm, tn), 1, ex, outs, epilogue)


def _mm_tn(name, x, dys, kind, R, C, tm, tn, tk=None):
    L, Kw = x.shape
    Nw = dys[0].shape[1]
    tk = tk or min(L, 1024)
    grid = (Kw // tm, Nw // tn, L // tk)
    a_ops = [(x, _bs((tk, tm), lambda i, j, k: (k, i)))]
    b_ops = [(dy, _bs((tk, tn), lambda i, j, k: (k, j))) for dy in dys]
    pairs = [(0, bi, bi, _TN) for bi in range(len(dys))]
    gs = _gspec(kind, R, C, tm, tn)
    outs = [(jax.ShapeDtypeStruct((2, 4, R, C), _WIRE), gs) for _ in dys]
    return _mm(name, grid, a_ops, b_ops, pairs, (tm, tn), len(dys), [], outs, lambda accs, ex: accs)


def _row_tile(L):
    return min(L, 256)


def _rowwise(name, body, ins, outs, L, acc_outs=()):
    tr = _row_tile(L)
    n_in, n_out = len(ins), len(outs)

    def kbody(*refs):
        i_refs, o_refs, a_refs = refs[:n_in], refs[n_in:n_in + n_out], refs[n_in + n_out:]
        res, sums = body(*[r[...] for r in i_refs])
        for o, r in zip(o_refs, res):
            o[...] = r.astype(o.dtype)
        if a_refs:
            @pl.when(pl.program_id(0) == 0)
            def _():
                for a in a_refs:
                    a[...] = jnp.zeros(a.shape, F32)
            for a, s in zip(a_refs, sums):
                a[...] += s

    in_specs = []
    for arr, kind in ins:
        if kind == "row":
            in_specs.append(pl.BlockSpec((tr, arr.shape[1]), lambda i: (i, 0)))
        else:
            in_specs.append(pl.BlockSpec(arr.shape, lambda i: (0, 0)))
    out_specs = [pl.BlockSpec((tr, c), lambda i: (i, 0)) for c, _ in outs]
    out_shape = [jax.ShapeDtypeStruct((L, c), dt) for c, dt in outs]
    out_specs += [pl.BlockSpec((1, c), lambda i: (0, 0)) for c in acc_outs]
    out_shape += [jax.ShapeDtypeStruct((1, c), F32) for c in acc_outs]
    return pl.pallas_call(
        kbody, name=name, grid=(L // tr,), in_specs=in_specs, out_specs=out_specs, out_shape=out_shape,
        compiler_params=_cparams(("arbitrary",)),
    )(*[a for a, _ in ins])


def _rms_fwd(name, x, g):
    def body(xv, gv):
        r = lax.rsqrt(jnp.mean(xv * xv, axis=-1, keepdims=True) + _EPS)
        return [xv * r * gv], []
    return _rowwise(name, body, [(x, "row"), (g, "vec")], [(x.shape[1], _MXU)], x.shape[0])[0]


def _rms_bwd(name, dh, x, g, dres):
    def body(dhv, xv, gv, dr):
        r = lax.rsqrt(jnp.mean(xv * xv, axis=-1, keepdims=True) + _EPS)
        xh = xv * r
        dxh = dhv * gv
        dx = dr + r * (dxh - xh * jnp.mean(dxh * xh, axis=-1, keepdims=True))
        return [dx, dx], [jnp.sum(dhv * xh, axis=0, keepdims=True)]
    D = x.shape[1]
    return _rowwise(name, body, [(dh, "row"), (x, "row"), (g, "vec"), (dres, "row")], [(D, F32), (D, _MXU)],
                    x.shape[0], [D])


def _loss_head(x, g, target, pp, gt):
    D = x.shape[1]

    def body(xv, gv, tv, ppv, gtv):
        r = lax.rsqrt(jnp.mean(xv * xv, axis=-1, keepdims=True) + _EPS)
        xh = xv * r
        e = xh * gv - tv
        dy = e * (1.0 / D)
        dxh = dy * gv
        dx = r * (dxh - xh * jnp.mean(dxh * xh, axis=-1, keepdims=True))
        row_loss = 0.5 * jnp.mean(e * e, axis=-1, keepdims=True)
        lsum = jnp.sum(row_loss, axis=0, keepdims=True) + jnp.zeros((1, _LANES), F32)
        gt32 = gtv.astype(F32)
        return [dx, dx * ppv * gt32 * (1.0 - gt32), dx * gt32], [jnp.sum(dy * xh, axis=0, keepdims=True), lsum]
    return _rowwise("loss_head", body, [(x, "row"), (g, "vec"), (target, "row"), (pp, "row"), (gt, "row")],
                    [(D, F32), (D, _MXU), (D, _MXU)], x.shape[0], [D, _LANES])


def _ple_bwd_elem(dx, pp, gt):
    def body(dxv, ppv, gtv):
        gt32 = gtv.astype(F32)
        return [dxv * ppv * gt32 * (1.0 - gt32), dxv * gt32], []
    D = dx.shape[1]
    return _rowwise("ple_bwd_elem", body, [(dx, "row"), (pp, "row"), (gt, "row")], [(D, _MXU), (D, _MXU)], dx.shape[0])


def _glu_bwd_elem(dx, val, sg):
    def body(dxv, vv, sv):
        v32, s32 = vv.astype(F32), sv.astype(F32)
        return [jnp.concatenate([dxv * s32, dxv * v32 * s32 * (1.0 - s32)], axis=1)], []
    D = dx.shape[1]
    return _rowwise("glu_bwd_elem", body, [(dx, "row"), (val, "row"), (sg, "row")], [(2 * D, _MXU)], dx.shape[0])[0]


def _gm_common(z, ln_g, ln_b, wc_bf, bsT):
    W = z.shape[1] // 2
    zu, zv = z[:, :W], z[:, W:]
    u, v = _gelu(zu), _gelu(zv)
    mu = jnp.mean(v, axis=-1, keepdims=True)
    vc = v - mu
    rstd = lax.rsqrt(jnp.mean(vc * vc, axis=-1, keepdims=True) + _EPS)
    vh = vc * rstd
    vn = vh * ln_g + ln_b
    vnb = vn.astype(_MXU)
    svs = []
    for h in range(_GM_HEADS):
        sl = slice(h * _LANES, (h + 1) * _LANES)
        svs.append(_dot(wc_bf[h], vnb[:, sl], _NN) + bsT[:, h:h + 1])
    return zu, zv, u, vh, rstd, vnb, svs


def _causal(w):
    t = lax.broadcasted_iota(jnp.int32, w.shape, w.ndim - 2)
    s = lax.broadcasted_iota(jnp.int32, w.shape, w.ndim - 1)
    return jnp.where(s <= t, w, jnp.zeros_like(w))


def _gmlp_fwd(z, ln_g, ln_b, w_s, bsT):
    L, W2 = z.shape
    W = W2 // 2

    def body(z_ref, g_ref, b_ref, ws_ref, bs_ref, m_ref):
        wc = _causal(ws_ref[...]).astype(_MXU)
        _, _, u, _, _, _, svs = _gm_common(z_ref[...], g_ref[...], b_ref[...], wc, bs_ref[...])
        for h in range(_GM_HEADS):
            sl = slice(h * _LANES, (h + 1) * _LANES)
            m_ref[:, sl] = (u[:, sl] * svs[h]).astype(m_ref.dtype)

    return pl.pallas_call(
        body, name="gmlp_fwd", grid=(L // _GM_CHUNK,),
        in_specs=[pl.BlockSpec((_GM_CHUNK, W2), lambda n: (n, 0)),
                  pl.BlockSpec((1, W), lambda n: (0, 0)), pl.BlockSpec((1, W), lambda n: (0, 0)),
                  pl.BlockSpec(w_s.shape, lambda n: (0, 0, 0)), pl.BlockSpec(bsT.shape, lambda n: (0, 0))],
        out_specs=pl.BlockSpec((_GM_CHUNK, W), lambda n: (n, 0)),
        out_shape=jax.ShapeDtypeStruct((L, W), _MXU),
        compiler_params=_cparams(("arbitrary",)),
    )(z, ln_g, ln_b, w_s, bsT)


def _gmlp_bwd(z, dm, ln_g, ln_b, w_s, bsT):
    L, W2 = z.shape
    W = W2 // 2
    T = _GM_CHUNK

    def body(z_ref, dm_ref, g_ref, b_ref, ws_ref, bs_ref, dz_ref, dws_ref, dbs_ref, dg_ref, db_ref):
        @pl.when(pl.program_id(0) == 0)
        def _():
            dws_ref[...] = jnp.zeros(dws_ref.shape, F32)
            dbs_ref[...] = jnp.zeros(dbs_ref.shape, F32)
            dg_ref[...] = jnp.zeros(dg_ref.shape, F32)
            db_ref[...] = jnp.zeros(db_ref.shape, F32)

        wc = _causal(ws_ref[...]).astype(_MXU)
        ln_g_v = g_ref[...]
        zu, zv, u, vh, rstd, vnb, svs = _gm_common(z_ref[...], ln_g_v, b_ref[...], wc, bs_ref[...])
        dmv = dm_ref[...]
        lane = lax.broadcasted_iota(jnp.int32, (T, _LANES), 1)
        dbs = jnp.zeros((T, _LANES), F32)
        dvn_parts = []
        for h in range(_GM_HEADS):
            sl = slice(h * _LANES, (h + 1) * _LANES)
            dsv = dmv[:, sl] * u[:, sl]
            dz_ref[:, sl] = (dmv[:, sl] * svs[h] * _gelu_grad(zu[:, sl])).astype(dz_ref.dtype)
            dbs = dbs + jnp.where(lane == h, jnp.sum(dsv, axis=1, keepdims=True), 0.0)
            dsvb = dsv.astype(_MXU)
            dws_ref[h] += _dot(dsvb, vnb[:, sl], _NT)
            dvn_parts.append(_dot(wc[h], dsvb, _TN))
        dbs_ref[...] += dbs
        dvn = jnp.concatenate(dvn_parts, axis=1)
        dg_ref[...] += jnp.sum(dvn * vh, axis=0, keepdims=True)
        db_ref[...] += jnp.sum(dvn, axis=0, keepdims=True)
        dxh = dvn * ln_g_v
        dv = rstd * (dxh - jnp.mean(dxh, axis=-1, keepdims=True) - vh * jnp.mean(dxh * vh, axis=-1, keepdims=True))
        dz_ref[:, W:] = (dv * _gelu_grad(zv)).astype(dz_ref.dtype)

        @pl.when(pl.program_id(0) == pl.num_programs(0) - 1)
        def _():
            dws_ref[...] = _causal(dws_ref[...])

    return pl.pallas_call(
        body, name="gmlp_bwd", grid=(L // T,),
        in_specs=[pl.BlockSpec((T, W2), lambda n: (n, 0)), pl.BlockSpec((T, W), lambda n: (n, 0)),
                  pl.BlockSpec((1, W), lambda n: (0, 0)), pl.BlockSpec((1, W), lambda n: (0, 0)),
                  pl.BlockSpec(w_s.shape, lambda n: (0, 0, 0)), pl.BlockSpec(bsT.shape, lambda n: (0, 0))],
        out_specs=[pl.BlockSpec((T, W2), lambda n: (n, 0)),
                   pl.BlockSpec(w_s.shape, lambda n: (0, 0, 0)), pl.BlockSpec((T, _LANES), lambda n: (0, 0)),
                   pl.BlockSpec((1, W), lambda n: (0, 0)), pl.BlockSpec((1, W), lambda n: (0, 0))],
        out_shape=[jax.ShapeDtypeStruct((L, W2), _MXU), jax.ShapeDtypeStruct(w_s.shape, F32),
                   jax.ShapeDtypeStruct((T, _LANES), F32),
                   jax.ShapeDtypeStruct((1, W), F32), jax.ShapeDtypeStruct((1, W), F32)],
        compiler_params=_cparams(("arbitrary",)),
    )(z, dm, ln_g, ln_b, w_s, bsT)


def _s5_prep_math(a_re, a_im, log_dt):
    dt = jnp.exp(log_dt)
    xr, xi = a_re * dt, a_im * dt
    e = jnp.exp(xr)
    lbr, lbi = e * jnp.cos(xi), e * jnp.sin(xi)
    dn = a_re * a_re + a_im * a_im
    nr, ni = lbr - 1.0, lbi
    pr, pi = nr * a_re + ni * a_im, ni * a_re - nr * a_im
    return dt, lbr, lbi, dn, nr, ni, pr, pi


def _vm():
    return pl.BlockSpec(memory_space=pltpu.VMEM)


def _s5_prep(a_re, a_im, log_dt, b_re, b_im):
    def body(ar_ref, ai_ref, ld_ref, br_ref, bi_ref, lbr_ref, lbi_ref, Br_ref, Bi_ref):
        _, lbr, lbi, dn, _, _, pr, pi = _s5_prep_math(ar_ref[...], ai_ref[...], ld_ref[...])
        cr, ci = (pr / dn)[:, None, :], (pi / dn)[:, None, :]
        lbr_ref[...] = lbr
        lbi_ref[...] = lbi
        br, bi = br_ref[...], bi_ref[...]
        Br_ref[...] = cr * br - ci * bi
        Bi_ref[...] = cr * bi + ci * br

    sd = jax.ShapeDtypeStruct
    return pl.pallas_call(
        body, name="s5_prep", in_specs=[_vm()] * 5, out_specs=[_vm()] * 4,
        out_shape=[sd(a_re.shape, F32), sd(a_re.shape, F32), sd(b_re.shape, F32), sd(b_re.shape, F32)],
    )(a_re, a_im, log_dt, b_re, b_im)


def _s5_prep_bwd(a_re, a_im, log_dt, b_re, b_im, dlbr_s, dlbi_s, dBr, dBi):
    def body(ar_ref, ai_ref, ld_ref, br_ref, bi_ref, dlr_ref, dli_ref, dBr_ref, dBi_ref,
             dar_ref, dai_ref, dld_ref, dbr_ref, dbi_ref):
        a_re_v, a_im_v = ar_ref[...], ai_ref[...]
        dt, lbr, lbi, dn, nr, ni, pr, pi = _s5_prep_math(a_re_v, a_im_v, ld_ref[...])
        cr, ci = (pr / dn)[:, None, :], (pi / dn)[:, None, :]
        br, bi, dBr_v, dBi_v = br_ref[...], bi_ref[...], dBr_ref[...], dBi_ref[...]
        dbr_ref[...] = cr * dBr_v + ci * dBi_v
        dbi_ref[...] = cr * dBi_v - ci * dBr_v
        dcr = jnp.sum(br * dBr_v + bi * dBi_v, axis=1)
        dci = jnp.sum(br * dBi_v - bi * dBr_v, axis=1)
        dpr, dpi = dcr / dn, dci / dn
        ddn = -(dcr * pr + dci * pi) / (dn * dn)
        dnr = dpr * a_re_v - dpi * a_im_v
        dni = dpr * a_im_v + dpi * a_re_v
        dlbr = dlr_ref[...] + dnr
        dlbi = dli_ref[...] + dni
        dxr = dlbr * lbr + dlbi * lbi
        dxi = dlbi * lbr - dlbr * lbi
        dar_ref[...] = dpr * nr + dpi * ni + 2.0 * ddn * a_re_v + dxr * dt
        dai_ref[...] = dpr * ni - dpi * nr + 2.0 * ddn * a_im_v + dxi * dt
        dld_ref[...] = jnp.sum(dxr * a_re_v + dxi * a_im_v, axis=0, keepdims=True) * dt

    sd = jax.ShapeDtypeStruct
    return pl.pallas_call(
        body, name="s5_prep_bwd", in_specs=[_vm()] * 9, out_specs=[_vm()] * 5,
        out_shape=[sd(a_re.shape, F32), sd(a_re.shape, F32), sd(log_dt.shape, F32),
                   sd(b_re.shape, F32), sd(b_re.shape, F32)],
    )(a_re, a_im, log_dt, b_re, b_im, dlbr_s, dlbi_s, dBr, dBi)


def _shift_rows(v, down):
    n = v.shape[0]
    rolled = pltpu.roll(v, 1 if down else n - 1, 0)
    row = lax.broadcasted_iota(jnp.int32, v.shape, 0)
    return jnp.where(row == (0 if down else n - 1), 0.0, rolled)


def _cmul(ar, ai, br, bi):
    return ar * br - ai * bi, ar * bi + ai * br


_SEG = 8
_UNROLL = 8


def _seg_rows(k):
    if isinstance(k, int):
        return pl.ds(k * _SEG, _SEG)
    return pl.ds(pl.multiple_of(k * _SEG, _SEG), _SEG)


def _unrolled(n, step, init):
    main = n // _UNROLL

    def trip(kk, s):
        for uu in range(_UNROLL):
            s = step(kk * _UNROLL + uu, s)
        return s

    s = lax.fori_loop(0, main, trip, init)
    for r in range(main * _UNROLL, n):
        s = step(r, s)
    return s


def _interleave(src_ref, dst_ref, nk):
    def step(k, carry):
        dst_ref[_seg_rows(k), :] = src_ref[pl.ds(k, _SEG, stride=nk), :]
        return carry
    _unrolled(nk, step, 0)


def _deinterleave(src_ref, dst_ref, nk):
    def step(k, carry):
        dst_ref[pl.ds(k, _SEG, stride=nk), :] = src_ref[_seg_rows(k), :]
        return carry
    _unrolled(nk, step, 0)


def _segment_inits(er, ei, ar, ai, nk, down):
    pr, pi = ar, ai
    for _ in range(int(math.log2(nk))):
        pr, pi = _cmul(pr, pi, pr, pi)
    fr, fi = er, ei
    for _ in range(_SEG - 1):
        sr, si = _shift_rows(fr, down), _shift_rows(fi, down)
        mr, mi = _cmul(pr, pi, sr, si)
        fr, fi = er + mr, ei + mi
    return _shift_rows(fr, down), _shift_rows(fi, down)


def _scan_states(x_re, x_im, ar, ai, nk):
    lanes = ar.shape[1]

    def step(k, s):
        rows = _seg_rows(k)
        mr, mi = _cmul(ar, ai, s[0], s[1])
        return mr + x_re[rows, :], mi + x_im[rows, :]

    zero = jnp.zeros((_SEG, lanes), F32)
    er, ei = _unrolled(nk, step, (zero, zero))
    ir, ii = _segment_inits(er, ei, ar, ai, nk, True)

    def step2(k, s):
        rows = _seg_rows(k)
        mr, mi = _cmul(ar, ai, s[0], s[1])
        nr, ni = mr + x_re[rows, :], mi + x_im[rows, :]
        x_re[rows, :] = nr
        x_im[rows, :] = ni
        return nr, ni

    _unrolled(nk, step2, (ir, ii))


def _s5_tile_fwd(u, bd_re, bd_im, cd_re, cd_im, ar, ai, d, s_re, s_im, nk):
    s_re[...] = _dot(u, bd_re, _NN)
    s_im[...] = _dot(u, bd_im, _NN)
    _scan_states(s_re, s_im, ar, ai, nk)
    return _dot(s_re[...], cd_re, _NN) - _dot(s_im[...], cd_im, _NN) + d * u


def _s5_specs(L, T):
    lanes = _S5_GT * _S5_P
    u_spec = pl.BlockSpec((L, _LANES), lambda t: (0, t))
    bd_spec = pl.BlockSpec((None, _S5_GT, _S5_C, _S5_P), lambda t: (t, 0, 0, 0))
    cd_spec = pl.BlockSpec((None, _S5_GT, _S5_P, _S5_C), lambda t: (t, 0, 0, 0))
    lam_spec = pl.BlockSpec((None, 1, lanes), lambda t: (t, 0, 0))
    d_spec = pl.BlockSpec((1, _LANES), lambda t: (0, t))
    return lanes, u_spec, bd_spec, cd_spec, lam_spec, d_spec


def _fill_block_diag(dst_ref, blocks_ref):
    _, a, b = blocks_ref.shape
    dst_ref[...] = jnp.zeros(dst_ref.shape, F32)
    for g in range(_S5_GT):
        dst_ref[g * a:(g + 1) * a, g * b:(g + 1) * b] = blocks_ref[g]


def _take_block_diag(dst_ref, v):
    _, a, b = dst_ref.shape
    for g in range(_S5_GT):
        dst_ref[g] = v[g * a:(g + 1) * a, g * b:(g + 1) * b]


def _s5_dense(bdr, bdi, cdr, cdi, dense):
    for src, dst in zip((bdr, bdi, cdr, cdi), dense):
        _fill_block_diag(dst, src)
    return [dst[...] for dst in dense]


def _s5_dense_scratch(lanes):
    return [pltpu.VMEM((_LANES, lanes), F32), pltpu.VMEM((_LANES, lanes), F32),
            pltpu.VMEM((lanes, _LANES), F32), pltpu.VMEM((lanes, _LANES), F32)]


def _s5_fwd(u, bd_re, bd_im, cd_re, cd_im, lam_re, lam_im, d):
    L, Wd = u.shape
    T = Wd // _LANES
    nk = L // _SEG
    lanes, u_spec, bd_spec, cd_spec, lam_spec, d_spec = _s5_specs(L, T)
    s_spec = pl.BlockSpec((L, lanes), lambda t: (0, t))

    def body(u_ref, bdr, bdi, cdr, cdi, lr, li, d_ref, g_ref, s_re, s_im, up, tmp, *dense):
        ar = jnp.broadcast_to(lr[...], (_SEG, lanes))
        ai = jnp.broadcast_to(li[...], (_SEG, lanes))
        bd_re_v, bd_im_v, cd_re_v, cd_im_v = _s5_dense(bdr, bdi, cdr, cdi, dense)
        _interleave(u_ref, up, nk)
        y = _s5_tile_fwd(up[...], bd_re_v, bd_im_v, cd_re_v, cd_im_v, ar, ai, d_ref[...], s_re, s_im, nk)
        up[...] = _gelu(y)
        _deinterleave(up, tmp, nk)
        g_ref[...] = tmp[...].astype(g_ref.dtype)

    return pl.pallas_call(
        body, name="s5_fwd", grid=(T,),
        in_specs=[u_spec, bd_spec, bd_spec, cd_spec, cd_spec, lam_spec, lam_spec, d_spec],
        out_specs=[u_spec, s_spec, s_spec],
        out_shape=[jax.ShapeDtypeStruct((L, Wd), _MXU), jax.ShapeDtypeStruct((L, T * lanes), F32),
                   jax.ShapeDtypeStruct((L, T * lanes), F32)],
        scratch_shapes=[pltpu.VMEM((L, _LANES), F32) for _ in range(2)] + _s5_dense_scratch(lanes),
        compiler_params=_cparams(("arbitrary",)),
    )(u, bd_re, bd_im, cd_re, cd_im, lam_re, lam_im, d)


def _s5_bwd(u, dg, states_re, states_im, bd_re, bd_im, cd_re, cd_im, lam_re, lam_im, d):
    L, Wd = u.shape
    T = Wd // _LANES
    nk = L // _SEG
    lanes, u_spec, bd_spec, cd_spec, lam_spec, d_spec = _s5_specs(L, T)
    s_spec = pl.BlockSpec((L, lanes), lambda t: (0, t))

    def body(u_ref, dg_ref, s_re, s_im, bdr, bdi, cdr, cdi, lr, li, d_ref,
             du_ref, dbdr, dbdi, dcdr, dcdi, dlr, dli, dd_ref, g_re, g_im, up, dgp, tmp, *dense):
        ar = jnp.broadcast_to(lr[...], (_SEG, lanes))
        ai = jnp.broadcast_to(li[...], (_SEG, lanes))
        bd_re_v, bd_im_v, cd_re_v, cd_im_v = _s5_dense(bdr, bdi, cdr, cdi, dense)
        _interleave(u_ref, up, nk)
        _interleave(dg_ref, dgp, nk)
        uv, dv = up[...], d_ref[...]
        y = _dot(s_re[...], cd_re_v, _NN) - _dot(s_im[...], cd_im_v, _NN) + dv * uv
        dy = dgp[...] * _gelu_grad(y)
        dd_ref[...] = jnp.sum(dy * uv, axis=0, keepdims=True)
        dyb = dy.astype(_MXU)
        _take_block_diag(dcdr, _dot(dyb, s_re[...], _TN))
        _take_block_diag(dcdi, -_dot(dyb, s_im[...], _TN))
        g_re[...] = _dot(dyb, cd_re_v, _NT)
        g_im[...] = -_dot(dyb, cd_im_v, _NT)

        nai = -ai

        def step(j, s):
            rows = _seg_rows(nk - 1 - j)
            mr, mi = _cmul(ar, nai, s[0], s[1])
            return mr + g_re[rows, :], mi + g_im[rows, :]

        zero = jnp.zeros((_SEG, lanes), F32)
        er, ei = _unrolled(nk, step, (zero, zero))
        ir, ii = _segment_inits(er, ei, ar, nai, nk, False)

        def acc_lam(gr, gi, pr, pi, acc):
            return acc[0] + gr * pr + gi * pi, acc[1] + gi * pr - gr * pi

        def step2(j, carry):
            s, acc = carry
            k = nk - 1 - j
            rows = _seg_rows(k)
            mr, mi = _cmul(ar, nai, s[0], s[1])
            nr, ni = mr + g_re[rows, :], mi + g_im[rows, :]
            g_re[rows, :] = nr
            g_im[rows, :] = ni
            prev = _seg_rows(k - 1)
            return (nr, ni), acc_lam(nr, ni, s_re[prev, :], s_im[prev, :], acc)

        (g0r, g0i), acc = _unrolled(nk - 1, step2, ((ir, ii), (zero, zero)))
        first = _seg_rows(0)
        mr, mi = _cmul(ar, nai, g0r, g0i)
        nr, ni = mr + g_re[first, :], mi + g_im[first, :]
        g_re[first, :] = nr
        g_im[first, :] = ni
        last = _seg_rows(nk - 1)
        acc = acc_lam(nr, ni, _shift_rows(s_re[last, :], True), _shift_rows(s_im[last, :], True), acc)
        dlr[...] = jnp.sum(acc[0], axis=0, keepdims=True)
        dli[...] = jnp.sum(acc[1], axis=0, keepdims=True)

        gtr, gti = g_re[...].astype(_MXU), g_im[...].astype(_MXU)
        ub = uv.astype(_MXU)
        _take_block_diag(dbdr, _dot(ub, gtr, _TN))
        _take_block_diag(dbdi, _dot(ub, gti, _TN))
        dgp[...] = _dot(gtr, bd_re_v, _NT) + _dot(gti, bd_im_v, _NT) + dy * dv
        _deinterleave(dgp, tmp, nk)
        du_ref[...] = tmp[...].astype(du_ref.dtype)

    sd = jax.ShapeDtypeStruct
    big = sd((T, _S5_GT, _S5_C, _S5_P), F32)
    return pl.pallas_call(
        body, name="s5_bwd", grid=(T,),
        in_specs=[u_spec, u_spec, s_spec, s_spec, bd_spec, bd_spec, cd_spec, cd_spec, lam_spec, lam_spec, d_spec],
        out_specs=[u_spec, bd_spec, bd_spec, bd_spec, bd_spec, lam_spec, lam_spec, d_spec],
        out_shape=[sd((L, Wd), _MXU), big, big, big, big, sd((T, 1, lanes), F32), sd((T, 1, lanes), F32),
                   sd((1, Wd), F32)],
        scratch_shapes=[pltpu.VMEM((L, lanes), F32) for _ in range(2)]
        + [pltpu.VMEM((L, _LANES), F32) for _ in range(3)] + _s5_dense_scratch(lanes),
        compiler_params=_cparams(("arbitrary",)),
    )(u, dg, states_re, states_im, bd_re, bd_im, cd_re, cd_im, lam_re, lam_im, d)


def _half_tile(R, few_arrays=False):
    for t in ((512, 704, 128) if few_arrays else (256, 352, 128)):
        if R % t == 0:
            return t
    raise ValueError(R)


def _cast_shard(name, w, layer, kind, R, C):
    tr = _half_tile(R, True)
    nr = R // tr

    def body(w_ref, o_ref):
        o_ref[...] = w_ref[...].astype(o_ref.dtype)

    if kind == "col":
        in_map = lambda h, i: (layer, h * nr + i, 0)
    else:
        in_map = lambda h, i: (layer, i, h)
    return pl.pallas_call(
        body, name=name, grid=(2, nr), in_specs=[pl.BlockSpec((None, tr, C), in_map)],
        out_specs=pl.BlockSpec((None, tr, C), lambda h, i: (h, i, 0)),
        out_shape=jax.ShapeDtypeStruct((2, R, C), _WIRE),
        compiler_params=_cparams(("arbitrary", "arbitrary")),
    )(w)


def _adam_math(w, g, m, v):
    m2 = _B1 * m + (1.0 - _B1) * g
    v2 = _B2 * v + (1.0 - _B2) * (g * g)
    m_hat = m2 / (1.0 - _B1 ** _STEP)
    v_hat = v2 / (1.0 - _B2 ** _STEP)
    delta = -_LR * (m_hat / (jnp.sqrt(v_hat) + _AEPS) + _WD * w)
    return delta, m2, v2


def _adamw_big(name, w, m, v, layer, pair, kind, R, C, c, after, prev):
    tr = _half_tile(R, few_arrays=C <= 1024 and R % 512 == 0)
    nr = R // tr

    def body(c_ref, w_ref, m_ref, v_ref, own_ref, other_ref, *rest):
        go_ref, d_ref, mo_ref, vo_ref = rest[-4:]
        g = jnp.where(pl.program_id(0) == c_ref[0], own_ref[...], other_ref[...])
        delta, m2, v2 = _adam_math(w_ref[...], g, m_ref[...], v_ref[...])
        go_ref[...] = g
        d_ref[...] = delta
        mo_ref[...] = m2
        vo_ref[...] = v2

    if kind == "col":
        nat = pl.BlockSpec((None, tr, C), lambda h, i, c_ref: (layer, h * nr + i, 0))
    else:
        nat = pl.BlockSpec((None, tr, C), lambda h, i, c_ref: (layer, i, h))

    def gspec(own):
        return pl.BlockSpec((tr, C), lambda h, i, c_ref: (jnp.where((h == c_ref[0]) == own, i, 0), 0))

    carried = list(prev) if prev is not None else []
    gs = pltpu.PrefetchScalarGridSpec(
        num_scalar_prefetch=1, grid=(2, nr),
        in_specs=[nat, nat, nat, gspec(True), gspec(False), _any()] + [_any()] * len(carried),
        out_specs=[nat] * 4)
    sd = jax.ShapeDtypeStruct(w.shape, F32)
    return pl.pallas_call(
        body, name=name, grid_spec=gs, out_shape=[sd] * 4,
        input_output_aliases={7 + k: k for k in range(len(carried))},
        compiler_params=_cparams(("arbitrary", "arbitrary")),
    )(c, w, m, v, pair[0], pair[1], after, *carried)


def _adamw_small(ws, gs, ms, vs):
    n = len(ws)

    def body(*refs):
        for i in range(n):
            w_ref, g_ref, m_ref, v_ref, d_ref, mo_ref, vo_ref = refs[i::n]
            delta, m2, v2 = _adam_math(w_ref[...], g_ref[...], m_ref[...], v_ref[...])
            d_ref[...] = delta
            mo_ref[...] = m2
            vo_ref[...] = v2

    sds = [jax.ShapeDtypeStruct(w.shape, F32) for w in ws]
    outs = pl.pallas_call(
        body, name="adamw_small", in_specs=[_vm()] * (4 * n), out_specs=[_vm()] * (3 * n), out_shape=sds * 3,
        compiler_params=pltpu.CompilerParams(vmem_limit_bytes=_VMEM_LIMIT),
    )(*ws, *gs, *ms, *vs)
    return outs[:n], outs[n:2 * n], outs[2 * n:]


def _place():
    x, y, c = lax.axis_index("x"), lax.axis_index("y"), lax.axis_index("c")
    chips = [(1 - x, y), (x, 1 - y), (1 - x, 1 - y)]
    return x, y, c, 2 * x + y, chips


def _any():
    return pl.BlockSpec(memory_space=pl.ANY)


def _remote(src, dst, ssem, rsem, dev):
    return pltpu.make_async_remote_copy(src_ref=src, dst_ref=dst, send_sem=ssem, recv_sem=rsem,
                                        device_id=dev, device_id_type=_MESH)


def _allgather(name, shards):
    n = len(shards)

    def body(*refs):
        s_refs, g_refs = refs[:n], refs[n:2 * n]
        send0, recv0, send1, recv1, send2, recv2 = refs[2 * n:]
        x, y, c, q, _ = _place()
        sib, xn, yn = (x, y, 1 - c), (1 - x, y, c), (x, 1 - y, c)
        qx, qy, qd = 2 * (1 - x) + y, 2 * x + (1 - y), 2 * (1 - x) + (1 - y)
        _handshake([sib, xn, yn])
        own = [_remote(s_refs[a], g_refs[a].at[q], send0.at[a], recv0.at[a], sib) for a in range(n)]

        def pieces(a):
            g, half = g_refs[a], s_refs[a].shape[1] // 2
            return [g.at[qx, c], g.at[qy, c], g.at[qd, c, pl.ds(0, half)], g.at[qd, c, pl.ds(half, half)]]

        def relayed(a):
            g, half = g_refs[a], s_refs[a].shape[1] // 2
            return [(g.at[qx, c, pl.ds(0, half)], yn), (g.at[qy, c, pl.ds(half, half)], xn)]

        first = []
        for a in range(n):
            first.append(_remote(s_refs[a].at[c], g_refs[a].at[q, c], send1.at[4 * a], recv1.at[4 * a], xn))
            first.append(_remote(s_refs[a].at[c], g_refs[a].at[q, c], send1.at[4 * a + 1], recv1.at[4 * a + 1], yn))
        for cp in first + own:
            cp.start()
        later = []
        for a in range(n):
            land = pieces(a)
            for j in range(4):
                k = 4 * a + j
                _remote(land[j], land[j], send1.at[k], recv1.at[k], xn).wait_recv()
                if j < 2:
                    src, to = relayed(a)[j]
                    cp = _remote(src, src, send1.at[k + 2], recv1.at[k + 2], to)
                    cp.start()
                    later.append(cp)
                cp = _remote(land[j], land[j], send2.at[k], recv2.at[k], sib)
                cp.start()
                later.append(cp)
        for a in range(n):
            g, half = g_refs[a], s_refs[a].shape[1] // 2
            theirs = [g.at[qx, 1 - c], g.at[qy, 1 - c], g.at[qd, 1 - c, pl.ds(0, half)],
                      g.at[qd, 1 - c, pl.ds(half, half)]]
            for j in range(4):
                _remote(theirs[j], theirs[j], send2.at[4 * a + j], recv2.at[4 * a + j], sib).wait_recv()
        for cp in own:
            cp.wait()
        for cp in first + later:
            cp.wait_send()

    return _sequencer(name, _ID_GATHER, body, shards,
                      [jax.ShapeDtypeStruct((4,) + s.shape, s.dtype) for s in shards], [n, n] + [4 * n] * 4)


def _handshake(peers):
    barrier = pltpu.get_barrier_semaphore()
    for peer in peers:
        pl.semaphore_signal(barrier, inc=1, device_id=peer, device_id_type=_MESH)
    pl.semaphore_wait(barrier, len(peers))


_ID_SIBLING, _ID_CHIPS, _ID_GATHER, _ID_ALL = 1, 2, 3, 4


def _sequencer(name, collective_id, body, ins, out_types, sem_counts):
    mesh = plsc.ScalarSubcoreMesh(axis_name="seq", num_cores=1)
    moved = sum(math.prod(o.shape) * jnp.dtype(o.dtype).itemsize for o in out_types)
    return pl.kernel(
        body, name=name, out_type=out_types, mesh=mesh,
        scratch_types=[pltpu.SemaphoreType.DMA((k,)) for k in sem_counts],
        compiler_params=pltpu.CompilerParams(collective_id=collective_id),
        cost_estimate=pl.CostEstimate(flops=0, transcendentals=0, bytes_accessed=2 * moved,
                                      remote_bytes_transferred=moved),
    )(*ins)


def _swap_halves(name, grads):
    n = len(grads)

    def body(*refs):
        g_refs, t_refs = refs[:n], refs[n:2 * n]
        send, recv = refs[2 * n:]
        x, y, c, _, _ = _place()
        _handshake([(x, y, 1 - c)])
        cps = [_remote(g_refs[a].at[1 - c], t_refs[a], send.at[a], recv.at[a], (x, y, 1 - c)) for a in range(n)]
        for cp in cps:
            cp.start()
        for cp in cps:
            cp.wait()

    return _sequencer(name, _ID_SIBLING, body, grads,
                      [jax.ShapeDtypeStruct(g.shape[1:], g.dtype) for g in grads], [n, n])


def _chip_sum(name, g, t, after, core):
    _, _, R, C = g.shape
    tr = _half_tile(R, True)

    def body(c_ref, g_ref, t_ref, after_ref, o_ref):
        o_ref[...] = (g_ref[...].astype(F32) + t_ref[...].astype(F32)).astype(o_ref.dtype)

    gs = pltpu.PrefetchScalarGridSpec(
        num_scalar_prefetch=1, grid=(4, R // tr),
        in_specs=[pl.BlockSpec((None, None, tr, C), lambda r, i, c_ref: (c_ref[0], r, i, 0)),
                  pl.BlockSpec((None, tr, C), lambda r, i, c_ref: (r, i, 0)), _any()],
        out_specs=pl.BlockSpec((None, tr, C), lambda r, i, c_ref: (r, i, 0)))
    return pl.pallas_call(
        body, name=name, grid_spec=gs, out_shape=jax.ShapeDtypeStruct((4, R, C), _WIRE),
        compiler_params=_cparams(("arbitrary", "arbitrary")),
    )(core, g, t, after)


def _scatter_parts(name, parts):
    n = len(parts)

    def body(*refs):
        p_refs, t_refs = refs[:n], refs[n:2 * n]
        send, recv = refs[2 * n:]
        x, y, c, q, chips = _place()
        _handshake([(rx, ry, c) for rx, ry in chips])
        cps = []
        for a in range(n):
            for j, (rx, ry) in enumerate(chips):
                k = 3 * a + j
                cps.append(_remote(p_refs[a].at[2 * rx + ry], t_refs[a].at[q], send.at[k], recv.at[k], (rx, ry, c)))
        for cp in cps:
            cp.start()
        for a in range(n):
            for j, (rx, ry) in enumerate(chips):
                k = 3 * a + j
                land = t_refs[a].at[2 * rx + ry]
                _remote(land, land, send.at[k], recv.at[k], (rx, ry, c)).wait_recv()
        for cp in cps:
            cp.wait_send()

    return _sequencer(name, _ID_CHIPS, body, parts,
                      [jax.ShapeDtypeStruct(p.shape, p.dtype) for p in parts], [3 * n, 3 * n])


def _sum_parts(name, p, t, where, after):
    _, R, C = t.shape
    tr = _half_tile(R, True)

    def body(w_ref, p_ref, t0_ref, t1_ref, t2_ref, after_ref, o_ref):
        o_ref[...] = (p_ref[...].astype(F32) + t0_ref[...].astype(F32)
                      + t1_ref[...].astype(F32) + t2_ref[...].astype(F32))

    def part(slot):
        return pl.BlockSpec((None, tr, C), lambda i, w_ref: (w_ref[slot], i, 0))

    gs = pltpu.PrefetchScalarGridSpec(
        num_scalar_prefetch=1, grid=(R // tr,), in_specs=[part(0), part(1), part(2), part(3), _any()],
        out_specs=pl.BlockSpec((tr, C), lambda i, w_ref: (i, 0)))
    return pl.pallas_call(
        body, name=name, grid_spec=gs, out_shape=jax.ShapeDtypeStruct((R, C), F32),
        compiler_params=_cparams(("arbitrary",)),
    )(where, p, t, t, t, after)


def _send_halves(name, halves):
    n = len(halves)

    def body(*refs):
        h_refs, o_refs = refs[:n], refs[n:2 * n]
        send, recv = refs[2 * n:]
        x, y, c, _, _ = _place()
        _handshake([(x, y, 1 - c)])
        cps = [_remote(h_refs[a], o_refs[a], send.at[a], recv.at[a], (x, y, 1 - c)) for a in range(n)]
        for cp in cps:
            cp.start()
        for cp in cps:
            cp.wait()

    return _sequencer(name, _ID_SIBLING, body, halves,
                      [jax.ShapeDtypeStruct(h.shape, h.dtype) for h in halves], [n, n])


class _Order:
    def __init__(self):
        self.tok = None
        x, y, c, q, chips = _place()
        self.core = c.astype(jnp.int32).reshape(1)
        self.where = jnp.stack([q] + [2 * rx + ry for rx, ry in chips]).astype(jnp.int32)

    def tie(self, x):
        return x if self.tok is None else lax.optimization_barrier((x, self.tok))[0]

    def done(self, outs):
        self.tok = outs[0]
        return outs


class _Reduction:
    def __init__(self, tag, grads):
        self.tag, self.grads = tag, grads
        self.swapped = _swap_halves(f"rs_swap_{tag}", grads)

    def scatter(self, seq):
        self.parts = [seq.done([_chip_sum(f"rs_chipsum_{self.tag}_{a}", g, t, seq.tok, seq.core)])[0]
                      for a, (g, t) in enumerate(zip(self.grads, self.swapped))]
        self.landed = _scatter_parts(f"rs_scatter_{self.tag}", self.parts)

    def finish(self, seq):
        halves = [seq.done([_sum_parts(f"rs_sum_{self.tag}_{a}", p, t, seq.where, seq.tok)])[0]
                  for a, (p, t) in enumerate(zip(self.parts, self.landed))]
        return list(zip(halves, _send_halves(f"rs_join_{self.tag}", halves)))


def _allreduce_small(name, v):
    _, R, _ = v.shape

    def body(v_ref, o_ref, land, acc, send1, recv1, send2, recv2):
        x, y, c = lax.axis_index("x"), lax.axis_index("y"), lax.axis_index("c")
        me = 4 * x + 2 * y + c
        peers = []
        for k in range(1, 8):
            dx, dy, dc = (k >> 2) & 1, (k >> 1) & 1, k & 1
            px, py, pc = (1 - x if dx else x), (1 - y if dy else y), (1 - c if dc else c)
            peers.append((k, (px, py, pc), 4 * px + 2 * py + pc))
        land[me] = v_ref[me]
        out1 = [_remote(v_ref.at[pid], land.at[me], send1.at[k], recv1.at[k], dev) for k, dev, pid in peers]
        for cp in out1:
            cp.start()
        for k, dev, pid in peers:
            _remote(land.at[pid], land.at[pid], send1.at[k], recv1.at[k], dev).wait_recv()
        total = land[0]
        for j in range(1, 8):
            total = total + land[j]
        acc[...] = total
        o_ref[me] = total
        out2 = [_remote(acc, o_ref.at[me], send2.at[k], recv2.at[k], dev) for k, dev, pid in peers]
        for cp in out2:
            cp.start()
        for k, dev, pid in peers:
            _remote(o_ref.at[pid], o_ref.at[pid], send2.at[k], recv2.at[k], dev).wait_recv()
        for cp in out1 + out2:
            cp.wait_send()

    return pl.pallas_call(
        body, name=name, in_specs=[_vm()], out_specs=_vm(),
        out_shape=jax.ShapeDtypeStruct(v.shape, F32),
        scratch_shapes=[pltpu.VMEM(v.shape, F32), pltpu.VMEM((R, _LANES), F32)]
        + [pltpu.SemaphoreType.DMA((8,)) for _ in range(4)],
        compiler_params=pltpu.CompilerParams(vmem_limit_bytes=_VMEM_LIMIT),
    )(v)


def _all_peers():
    x, y, c = lax.axis_index("x"), lax.axis_index("y"), lax.axis_index("c")
    peers = []
    for k in range(1, 8):
        px, py, pc = (1 - x if k & 4 else x), (1 - y if k & 2 else y), (1 - c if k & 1 else c)
        peers.append((k, (px, py, pc), 4 * px + 2 * py + pc))
    return 4 * x + 2 * y + c, peers


def _exchange_slices(name, v):
    def body(v_ref, land, send, recv):
        me, peers = _all_peers()
        _handshake([dev for _, dev, _ in peers])
        cps = [_remote(v_ref.at[pid], land.at[me], send.at[k], recv.at[k], dev) for k, dev, pid in peers]
        for cp in cps:
            cp.start()
        for k, dev, pid in peers:
            _remote(land.at[pid], land.at[pid], send.at[k], recv.at[k], dev).wait_recv()
        for cp in cps:
            cp.wait_send()

    return _sequencer(name, _ID_ALL, body, [v], [jax.ShapeDtypeStruct(v.shape, v.dtype)], [8, 8])[0]


def _sum_slices(name, v, landed, after):
    _, R, _ = v.shape

    def body(v_ref, land_ref, after_ref, o_ref):
        me, peers = _all_peers()
        acc = v_ref[me]
        for _, _, pid in peers:
            acc = acc + land_ref[pid]
        o_ref[...] = acc

    return pl.pallas_call(
        body, name=name, in_specs=[_vm(), _vm(), _any()], out_specs=_vm(),
        out_shape=jax.ShapeDtypeStruct((R, _LANES), F32),
        compiler_params=pltpu.CompilerParams(vmem_limit_bytes=_VMEM_LIMIT),
    )(v, landed, after)


def _broadcast_slices(name, s):
    def body(s_ref, out, send, recv):
        me, peers = _all_peers()
        _handshake([dev for _, dev, _ in peers])
        cps = [_remote(s_ref, out.at[me], send.at[k], recv.at[k], dev) for k, dev, pid in peers]
        for cp in cps:
            cp.start()
        for k, dev, pid in peers:
            _remote(out.at[pid], out.at[pid], send.at[k], recv.at[k], dev).wait_recv()
        for cp in cps:
            cp.wait_send()

    return _sequencer(name, _ID_ALL, body, [s], [jax.ShapeDtypeStruct((8,) + s.shape, s.dtype)], [8, 8])[0]


_WEIGHT_NAMES = ['norm_mix', 'norm_ffn', 'norm_ple', 'norm_final', 'gm_w_in', 'gm_ln_g', 'gm_ln_b', 'gm_w_s',
                 'gm_b_s', 'gm_w_out', 's5_w_in', 's5_a_re', 's5_a_im', 's5_log_dt', 's5_b_re', 's5_b_im',
                 's5_c_re', 's5_c_im', 's5_d', 's5_w_out', 'ffn_w1', 'ffn_w3', 'ffn_w2', 'ple_w_gate', 'ple_w_proj']
_BIG = {'gm_w_in': 'col', 'gm_w_out': 'row', 's5_w_in': 'row', 's5_w_out': 'col', 'ffn_w1': 'col',
        'ffn_w3': 'col', 'ffn_w2': 'row', 'ple_w_gate': 'row', 'ple_w_proj': 'col'}


_VIEW = {'s5_a_re': (0, 2, 1), 's5_a_im': (0, 2, 1), 's5_b_re': (0, 2, 3, 1), 's5_b_im': (0, 2, 3, 1),
         's5_c_re': (0, 2, 3, 1), 's5_c_im': (0, 2, 3, 1)}


def _to_view(name, a):
    return jnp.transpose(a, _VIEW[name]) if name in _VIEW else a


def _from_view(name, a):
    if name not in _VIEW:
        return a
    perm = _VIEW[name]
    return jnp.transpose(a, [perm.index(i) for i in range(len(perm))])


def _rc(kind, shard_shape):
    rows, cols = shard_shape[-2:]
    return (rows // 2, cols) if kind == "col" else (rows, cols // 2)


def _pack(vecs, rows_multiple):
    flat = jnp.concatenate([a.reshape(-1).astype(F32) for a in vecs])
    unit = rows_multiple * _LANES
    pad = (-flat.shape[0]) % unit
    return jnp.pad(flat, (0, pad)).reshape(-1, _LANES)


def _unpack(buf, shapes):
    flat = buf.reshape(-1)
    out, off = [], 0
    for s in shapes:
        n = math.prod(s)
        out.append(flat[off:off + n].reshape(s))
        off += n
    return out


def _ident(accs, ex):
    return accs


def _add_resid(accs, ex):
    return [accs[0] + ex[0]]


def _swiglu_epi(accs, ex):
    a, b = accs
    return [a, b, a * _sig(a) * b]


def _swiglu_bwd_epi(accs, ex):
    df = accs[0]
    a, b = ex[0].astype(F32), ex[1].astype(F32)
    sa = _sig(a)
    return [df * b * (sa * (1.0 + a * (1.0 - sa))), df * (a * sa)]


def _ple_epi(accs, ex):
    xin, pv = ex[0], ex[1]
    kh = pv.shape[1] // 2
    pp = jnp.concatenate([_dot(pv[:, :kh], ex[2 + 2 * part], _NN) + _dot(pv[:, kh:], ex[3 + 2 * part], _NN)
                          for part in (0, 1)], axis=1)
    gt = _sig(accs[0])
    return [xin + gt * pp, gt, pp]


def _glu_epi(accs, ex):
    val, sg = accs[0], _sig(accs[1])
    return [ex[0] + val * sg, val, sg]


def kernel(x, p, norm_mix, norm_ffn, norm_ple, norm_final, gm_w_in, gm_ln_g, gm_ln_b, gm_w_s, gm_b_s, gm_w_out, s5_w_in, s5_a_re, s5_a_im, s5_log_dt, s5_b_re, s5_b_im, s5_c_re, s5_c_im, s5_d, s5_w_out, ffn_w1, ffn_w3, ffn_w2, ple_w_gate, ple_w_proj, loss_target, m_norm_mix, m_norm_ffn, m_norm_ple, m_norm_final, m_gm_w_in, m_gm_ln_g, m_gm_ln_b, m_gm_w_s, m_gm_b_s, m_gm_w_out, m_s5_w_in, m_s5_a_re, m_s5_a_im, m_s5_log_dt, m_s5_b_re, m_s5_b_im, m_s5_c_re, m_s5_c_im, m_s5_d, m_s5_w_out, m_ffn_w1, m_ffn_w3, m_ffn_w2, m_ple_w_gate, m_ple_w_proj, v_norm_mix, v_norm_ffn, v_norm_ple, v_norm_final, v_gm_w_in, v_gm_ln_g, v_gm_ln_b, v_gm_w_s, v_gm_b_s, v_gm_w_out, v_s5_w_in, v_s5_a_re, v_s5_a_im, v_s5_log_dt, v_s5_b_re, v_s5_b_im, v_s5_c_re, v_s5_c_im, v_s5_d, v_s5_w_out, v_ffn_w1, v_ffn_w3, v_ffn_w2, v_ple_w_gate, v_ple_w_proj):
    env = dict(locals())
    w = {n: env[n] for n in _WEIGHT_NAMES}
    mom = {n: env["m_" + n] for n in _WEIGHT_NAMES}
    var = {n: env["v_" + n] for n in _WEIGHT_NAMES}
    xs, tgt = x[0], loss_target[0]
    L, D = xs.shape
    depth = norm_mix.shape[0]
    qx, qy = lax.axis_index("x"), lax.axis_index("y")
    q = 2 * qx + qy

    W, last_cast = {}, [None]

    def gather(tag, items, after=None):
        shards = []
        for name, layer in items:
            kind = _BIG[name]
            R, C = _rc(kind, w[name].shape)
            src = w[name] if after is None else lax.optimization_barrier((w[name], after))[0]
            shards.append(_cast_shard(f"cast_{name}{layer}", src, layer, kind, R, C))
        last_cast[0] = shards[-1]
        full = _allgather(f"ag_{tag}", shards)
        W.update({it: _W(f, _BIG[it[0]]) for it, f in zip(items, full)})

    gather("gm_w_in", [("gm_w_in", 0)])
    gather("gm_w_out", [("gm_w_out", 0)])
    gather("ffn_up0", [("ffn_w1", 0), ("ffn_w3", 0)])
    gather("ffn_down0", [("ffn_w2", 0)])

    def gather_rest(after):
        gather("ple0", [("ple_w_gate", 0), ("ple_w_proj", 0)], after)
        gather("s5_w_in", [("s5_w_in", 0)], after)
        gather("s5_w_out", [("s5_w_out", 0)], after)
        gather("ffn_up1", [("ffn_w1", 1), ("ffn_w3", 1)], after)
        gather("ffn_down1", [("ffn_w2", 1)], after)
        gather("ple1", [("ple_w_gate", 1), ("ple_w_proj", 1)], after)

    d_slots = jnp.zeros((4, D // 4), F32)
    d_slots = lax.dynamic_update_slice(d_slots, s5_d.astype(F32), (q, 0))
    d_sum = _allreduce_small("ar_s5_d", _pack([d_slots], 64).reshape(8, -1, _LANES))
    d_full = (d_sum.reshape(-1)[:D] * 0.5).reshape(1, D)

    def ffn_fwd(i, xin):
        hf = _rms_fwd(f"rms_ffn{i}", xin, norm_ffn[i:i + 1])
        a, b, f = _mm_nn(f"ffn_up{i}", hf, [W["ffn_w1", i], W["ffn_w3", i]], 1024, 1408, ffn_w2.shape[1] * 4,
                         [], [_MXU, _MXU, _MXU], _swiglu_epi, tm=1024)
        xo = _mm_nn(f"ffn_down{i}", f, [W["ffn_w2", i]], 1408, 1024, D, [xin], [F32], _add_resid, tm=1024)[0]
        return xo, (xin, hf, a, b, f)

    def ple_fwd(i, xin):
        hp = _rms_fwd(f"rms_ple{i}", xin, norm_ple[i:i + 1])
        pi, wp = p[i, 0], W["ple_w_proj", i]

        def more(tm):
            ops = [(pi, pl.BlockSpec((tm, pi.shape[1]), lambda i_, j, k: (i_, 0)))]
            for part in (0, 1):
                for half in (0, 1):
                    ops.append((wp.a, pl.BlockSpec((None, None, wp.R, wp.C),
                                                   lambda i_, j, k, part=part, half=half: (2 * j + part, half, 0, 0))))
            return ops

        xo, gt, pp = _mm_nn(f"ple_gate{i}", hp, [W["ple_w_gate", i]], 512, 2 * wp.C, D, [xin], [F32, _MXU, F32],
                            _ple_epi, tm=1024, more=more)
        return xo, (xin, hp, pi, pp, gt)

    h0 = _rms_fwd("rms_mix0", xs, norm_mix[0:1])
    z = _mm_nn("gm_in", h0, [W["gm_w_in", 0]], 1024, 1024, 2 * D, [], [F32], _ident, tm=1024)[0]
    bsT = gm_b_s[0].T
    gm_m = _gmlp_fwd(z, gm_ln_g, gm_ln_b, gm_w_s[0], bsT)
    x1 = _mm_nn("gm_out", gm_m, [W["gm_w_out", 0]], 512, 1024, D, [xs], [F32], _add_resid, tm=2048)[0]
    gather_rest(x1)
    x1 = lax.optimization_barrier((x1, last_cast[0]))[0]
    x2, ffn0 = ffn_fwd(0, x1)
    x3, ple0 = ple_fwd(0, x2)

    T = D // _LANES
    lanes = _S5_GT * _S5_P
    sv = {n: _to_view(n, w[n])[0] for n in _VIEW}
    a_re, a_im, log_dt = sv["s5_a_re"], sv["s5_a_im"], s5_log_dt
    lbr, lbi, Bbar_re, Bbar_im = _s5_prep(a_re, a_im, log_dt, sv["s5_b_re"], sv["s5_b_im"])

    def to_bd(B):
        return jnp.transpose(B.reshape(_S5_P, _S5_C, T, _S5_GT), (2, 3, 1, 0))

    def to_cd(cw):
        return jnp.transpose(cw.reshape(_S5_C, _S5_P, T, _S5_GT), (2, 3, 1, 0))

    def to_lam(v):
        return jnp.transpose(v).reshape(T, 1, lanes)

    bd_re, bd_im = to_bd(Bbar_re), to_bd(Bbar_im)
    cd_re, cd_im = to_cd(sv["s5_c_re"]), to_cd(sv["s5_c_im"])
    lam_re, lam_im = to_lam(lbr), to_lam(lbi)

    h1 = _rms_fwd("rms_mix1", x3, norm_mix[1:2])
    u = _mm_nn("s5_in", h1, [W["s5_w_in", 0]], 512, 1024, D, [], [F32], _ident, tm=2048)[0]
    s5_g, s5_re, s5_im = _s5_fwd(u, bd_re, bd_im, cd_re, cd_im, lam_re, lam_im, d_full)
    x4, glu_val, glu_sg = _mm_nn("s5_out", s5_g, [W["s5_w_out", 0], W["s5_w_out", 0]], 1024, 1024, D, [x3],
                                 [F32, _MXU, _MXU], _glu_epi, tm=1024, cb_offsets=[0, 2])
    x5, ffn1 = ffn_fwd(1, x4)
    x6, ple1 = ple_fwd(1, x5)

    dx, dpre1, dpp1, d_norm_final, loss_rows = _loss_head(x6, norm_final[None], tgt, ple1[3], ple1[4])

    small = {}

    seq = _Order()
    tie, done = seq.tie, seq.done
    d_norm_ple, d_norm_ffn, d_norm_mix = [None] * depth, [None] * depth, [None] * depth
    reduced = {}

    def keep(names, layer, pairs):
        for n, pr in zip(names, pairs):
            reduced[n, layer] = pr

    ple_names, up_names, down_names = ["ple_w_gate", "ple_w_proj"], ["ffn_w1", "ffn_w3"], ["ffn_w2"]

    def ple_bwd(i, dxo, saved, elem=None):
        xin, hp, pi, pp, gt = saved
        dpre, dpp = elem or done(_ple_bwd_elem(tie(dxo), pp, gt))
        dwg = done(_mm_tn(f"ple_gate_dw{i}", tie(hp), [dpre], "row", 512, 1024, 512, 1024))[0]
        dwp = done(_mm_tn(f"ple_proj_dw{i}", tie(pi), [dpp], "col", 128, 512, 128, 512))[0]
        red = _Reduction(f"ple{i}", [dwg, dwp])
        dhp = done(_mm_nt(f"ple_gate_dx{i}", [tie(dpre)], [W["ple_w_gate", i]], 512, 1024, [], [F32], _ident,
                          tm=2048))[0]
        dxin, dxin_mxu, dg = done(_rms_bwd(f"rms_ple_bwd{i}", tie(dhp), xin, norm_ple[i:i + 1], dxo))
        return dxin, dxin_mxu, dg, red

    def ffn_bwd(i, dxo, dxo_mxu, saved, before_up):
        xin, hf, a, b, f = saved
        dw2 = done(_mm_tn(f"ffn_down_dw{i}", tie(f), [dxo_mxu], "row", 1408, 1024, 1408, 1024))[0]
        r_down = _Reduction(f"ffd{i}", [dw2])
        da, db = done(_mm_nt(f"ffn_down_dx{i}", [tie(dxo_mxu)], [W["ffn_w2", i]], 1408, 1024, [a, b], [_MXU, _MXU],
                             _swiglu_bwd_epi, tm=1024))
        for step in before_up:
            step()
        r_down.scatter(seq)
        dw1, dw3 = done(_mm_tn(f"ffn_up_dw{i}", tie(hf), [da, db], "col", 1024, 1408, 1024, 1408))
        r_up = _Reduction(f"ffu{i}", [dw1, dw3])
        dhf = done(_mm_nt(f"ffn_up_dx{i}", [tie(da), db], [W["ffn_w1", i], W["ffn_w3", i]], 1024, 1408, [], [F32],
                          _ident, tm=1024))[0]
        dxin, dxin_mxu, dg = done(_rms_bwd(f"rms_ffn_bwd{i}", tie(dhf), xin, norm_ffn[i:i + 1], dxo))
        r_up.scatter(seq)
        return dxin, dxin_mxu, dg, r_down, r_up

    dx, dx_mxu, d_norm_ple[1], r_ple1 = ple_bwd(1, dx, ple1, (dpre1, dpp1))
    dx, _, d_norm_ffn[1], r_down1, r_up1 = ffn_bwd(1, dx, dx_mxu, ffn1, [lambda: r_ple1.scatter(seq)])

    do = done([_glu_bwd_elem(tie(dx), glu_val, glu_sg)])[0]
    dw_s5out = done(_mm_tn("s5_out_dw", tie(s5_g), [do], "col", 1024, 1024, 1024, 1024))[0]
    r_s5out = _Reduction("s5out", [dw_s5out])
    dgy = done(_mm_nt("s5_out_dx", [tie(do)], [W["s5_w_out", 0]], 1024, 1024, [], [F32], _ident, tm=1024))[0]
    keep(ple_names, 1, r_ple1.finish(seq))
    keep(down_names, 1, r_down1.finish(seq))
    du, dbd_re, dbd_im, dcd_re, dcd_im, dl_re, dl_im, dd = done(_s5_bwd(
        tie(u), dgy, s5_re, s5_im, bd_re, bd_im, cd_re, cd_im, lam_re, lam_im, d_full))
    r_s5out.scatter(seq)
    dw_s5in = done(_mm_tn("s5_in_dw", tie(h1), [du], "row", 512, 1024, 512, 1024))[0]
    r_s5in = _Reduction("s5in", [dw_s5in])
    dh1 = done(_mm_nt("s5_in_dx", [tie(du)], [W["s5_w_in", 0]], 512, 1024, [], [F32], _ident, tm=2048))[0]
    dx, _, d_norm_mix[1] = done(_rms_bwd("rms_mix1_bwd", tie(dh1), x3, norm_mix[1:2], dx))
    keep(up_names, 1, r_up1.finish(seq))
    r_s5in.scatter(seq)

    def from_bd(t):
        return jnp.transpose(t, (3, 2, 0, 1)).reshape(_S5_P, _S5_C, T * _S5_GT)

    def from_cdT(t):
        return jnp.transpose(t, (2, 3, 0, 1)).reshape(_S5_C, _S5_P, T * _S5_GT)

    def from_lam(t):
        return jnp.transpose(t.reshape(T * _S5_GT, _S5_P))

    da_re, da_im, dlog_dt, db_re, db_im = _s5_prep_bwd(
        a_re, a_im, log_dt, sv["s5_b_re"], sv["s5_b_im"], from_lam(dl_re), from_lam(dl_im),
        from_bd(dbd_re), from_bd(dbd_im))
    small["s5_a_re"], small["s5_a_im"], small["s5_log_dt"] = da_re[None], da_im[None], dlog_dt
    small["s5_b_re"], small["s5_b_im"] = db_re[None], db_im[None]
    small["s5_c_re"], small["s5_c_im"] = from_cdT(dcd_re)[None], from_cdT(dcd_im)[None]

    dx, dx_mxu, d_norm_ple[0], r_ple0 = ple_bwd(0, dx, ple0)
    keep(["s5_w_out"], 0, r_s5out.finish(seq))
    xin0, hf0, a0, b0, f0 = ffn0
    da0, db0 = done(_mm_nt("ffn_down_dx0", [tie(dx_mxu)], [W["ffn_w2", 0]], 1408, 1024, [a0, b0], [_MXU, _MXU],
                           _swiglu_bwd_epi, tm=1024))
    r_ple0.scatter(seq)
    dw1, dw3 = done(_mm_tn("ffn_up_dw0", tie(hf0), [da0, db0], "col", 1024, 1408, 1024, 1408))
    r_up0 = _Reduction("ffu0", [dw1, dw3])
    keep(["s5_w_in"], 0, r_s5in.finish(seq))
    dw2 = done(_mm_tn("ffn_down_dw0", tie(f0), [dx_mxu], "row", 1408, 1024, 1408, 1024))[0]
    r_down0 = _Reduction("ffd0", [dw2])
    r_up0.scatter(seq)
    dhf0 = done(_mm_nt("ffn_up_dx0", [tie(da0), db0], [W["ffn_w1", 0], W["ffn_w3", 0]], 1024, 1408, [], [F32], _ident,
                       tm=1024))[0]
    dx, dx_mxu, d_norm_ffn[0] = done(_rms_bwd("rms_ffn_bwd0", tie(dhf0), xin0, norm_ffn[0:1], dx))
    keep(ple_names, 0, r_ple0.finish(seq))
    r_down0.scatter(seq)

    dw_gmout = done(_mm_tn("gm_out_dw", tie(gm_m), [dx_mxu], "row", 512, 1024, 512, 1024))[0]
    r_gmout = _Reduction("gmout", [dw_gmout])
    dgm = done(_mm_nt("gm_out_dx", [tie(dx_mxu)], [W["gm_w_out", 0]], 512, 1024, [], [F32], _ident, tm=2048))[0]
    dz, dws, dbsT, dlng, dlnb = done(_gmlp_bwd(tie(z), dgm, gm_ln_g, gm_ln_b, gm_w_s[0], bsT))
    dw_gmin = done(_mm_tn("gm_in_dw", tie(h0), [dz], "col", 1024, 1024, 1024, 1024))[0]
    r_gmin = _Reduction("gmin", [dw_gmin])
    dh0 = done(_mm_nt("gm_in_dx", [tie(dz)], [W["gm_w_in", 0]], 1024, 1024, [], [F32], _ident, tm=1024))[0]
    dx, _, d_norm_mix[0] = done(_rms_bwd("rms_mix0_bwd", tie(dh0), xs, norm_mix[0:1], dx))
    grad_x = dx[None]

    small["norm_mix"], small["norm_ffn"] = jnp.concatenate(d_norm_mix), jnp.concatenate(d_norm_ffn)
    small["norm_ple"], small["norm_final"] = jnp.concatenate(d_norm_ple), d_norm_final[0]
    small["gm_ln_g"], small["gm_ln_b"], small["gm_w_s"] = dlng, dlnb, dws[None]
    small["gm_b_s"] = dbsT[:, :_GM_HEADS].T[None]
    small["s5_d"] = dd

    small_names = [n for n in _WEIGHT_NAMES if n not in _BIG]
    packed = _pack([small[n] for n in small_names] + [loss_rows[:, :1]], 64).reshape(8, -1, _LANES)
    grads, deltas, new_m, new_v = {}, {}, {}, {}
    my_c = lax.axis_index("c")

    def adamw(n, layer, prev):
        kind = _BIG[n]
        R, C = _rc(kind, w[n].shape)
        return done(_adamw_big(f"adamw_{n}{layer}", w[n], mom[n], var[n], layer, reduced[n, layer], kind, R, C,
                               seq.core, seq.tok, prev))

    def adamw_last(names):
        for n in names:
            grads[n], deltas[n], new_m[n], new_v[n] = adamw(n, 0, late.get(n))

    late = {}
    for n in down_names + ple_names:
        late[n] = adamw(n, 1, None)
    keep(up_names, 0, r_up0.finish(seq))
    r_gmout.scatter(seq)
    r_gmin.scatter(seq)
    landed = _exchange_slices("ar_small_in", packed)
    for n in up_names:
        late[n] = adamw(n, 1, None)
    adamw_last(["s5_w_in", "s5_w_out"] + ple_names)
    keep(down_names, 0, r_down0.finish(seq))
    mine = done([_sum_slices("ar_small_sum", packed, landed, seq.tok)])[0]
    spread = _broadcast_slices("ar_small_out", mine)
    adamw_last(up_names)
    keep(["gm_w_out"], 0, r_gmout.finish(seq))
    adamw_last(down_names)

    spread = lax.optimization_barrier((spread, seq.tok))[0]
    summed = lax.dynamic_update_slice(spread, mine[None], (4 * qx + 2 * qy + my_c, 0, 0))
    *red_list, loss_sum = _unpack(summed, [small[n].shape for n in small_names] + [(1, 1)])
    red_small, loss = dict(zip(small_names, red_list)), loss_sum.reshape(())
    red_small["s5_d"] = lax.dynamic_slice(red_small["s5_d"], (0, q * (D // 4)), (1, D // 4))

    def two_d(a):
        return a.reshape((1,) * (2 - a.ndim) + a.shape)

    def views(src):
        return [two_d(_to_view(n, src[n])) for n in small_names]

    g_views = [two_d(red_small[n]) for n in small_names]
    w_views = views(w)
    dl, mo, vo = _adamw_small([tie(w_views[0])] + w_views[1:], g_views, views(mom), views(var))
    done(dl)
    for n, g_, d_, m_, v_ in zip(small_names, g_views, dl, mo, vo):
        grads[n], deltas[n], new_m[n], new_v[n] = [_from_view(n, t_.reshape(_to_view(n, w[n]).shape)).reshape(w[n].shape)
                                                  for t_ in (g_, d_, m_, v_)]

    keep(["gm_w_in"], 0, r_gmin.finish(seq))
    adamw_last(["gm_w_out", "gm_w_in"])

    return (loss, grad_x, *[grads[n] for n in _WEIGHT_NAMES], *[deltas[n] for n in _WEIGHT_NAMES],
            *[new_m[n] for n in _WEIGHT_NAMES], *[new_v[n] for n in _WEIGHT_NAMES])
```

```python
import math

import jax
import jax.numpy as jnp
from jax import lax
from jax.experimental import pallas as pl
from jax.experimental.pallas import tpu as pltpu
from jax.experimental.pallas import tpu_sc as plsc

F32 = jnp.float32
_MXU = jnp.bfloat16
_WIRE = jnp.bfloat16
_EPS = 1e-6
_VMEM_LIMIT = 56 * 1024 * 1024
_LANES = 128
_MESH = pl.DeviceIdType.MESH

_LR, _B1, _B2, _AEPS, _WD, _STEP = 0.001, 0.9, 0.999, 1e-08, 0.01, 10

_GM_CHUNK = 128
_GM_HEADS = 16
_S5_GT = 8
_S5_P = 64
_S5_C = 16

_NN = (((1,), (0,)), ((), ()))
_NT = (((1,), (1,)), ((), ()))
_TN = (((0,), (0,)), ((), ()))


def _cparams(sem):
    return pltpu.CompilerParams(dimension_semantics=sem, vmem_limit_bytes=_VMEM_LIMIT)


def _sig(x):
    return 0.5 * jnp.tanh(0.5 * x) + 0.5


_GC = math.sqrt(2.0 / math.pi)


def _gelu(x):
    return 0.5 * x * (1.0 + jnp.tanh(_GC * (x + 0.044715 * (x * x * x))))


def _gelu_grad(x):
    t = jnp.tanh(_GC * (x + 0.044715 * (x * x * x)))
    return 0.5 * (1.0 + t) + 0.5 * x * (1.0 - t * t) * (_GC * (1.0 + 3.0 * 0.044715 * x * x))


def _dot(a, b, dn):
    return lax.dot_general(a.astype(_MXU), b.astype(_MXU), dn, preferred_element_type=F32)


class _W:
    def __init__(self, arr, kind):
        self.a, self.kind = arr, kind
        self.R, self.C = arr.shape[2], arr.shape[3]

    def full_shape(self):
        return (2 * self.R, 4 * self.C) if self.kind == "col" else (4 * self.R, 2 * self.C)


def _part_index(kind, R, C, tr, tc, rb, cb):
    nr, nc = R // tr, C // tc
    if kind == "col":
        return cb // nc, rb // nr, rb % nr, cb % nc
    return rb // nr, cb // nc, rb % nr, cb % nc


def _wspec(w, tr, tc, rb_fn, cb_fn):
    assert w.R % tr == 0 and w.C % tc == 0, (w.R, w.C, tr, tc)

    def imap(i, j, k):
        return _part_index(w.kind, w.R, w.C, tr, tc, rb_fn(i, j, k), cb_fn(i, j, k))

    return pl.BlockSpec((None, None, tr, tc), imap)


def _gspec(kind, R, C, tr, tc):
    assert R % tr == 0 and C % tc == 0, (R, C, tr, tc)

    def imap(i, j, k):
        part, half, rbi, cbi = _part_index(kind, R, C, tr, tc, i, j)
        return half, part, rbi, cbi

    return pl.BlockSpec((None, None, tr, tc), imap)


def _mm(name, grid, a_ops, b_ops, pairs, acc_shape, n_acc, extras, outs, epilogue):
    nk = grid[2]
    na, nb, ne, no = len(a_ops), len(b_ops), len(extras), len(outs)

    def body(*refs):
        a_refs = refs[:na]
        b_refs = refs[na:na + nb]
        e_refs = refs[na + nb:na + nb + ne]
        o_refs = refs[na + nb + ne:na + nb + ne + no]
        acc_refs = refs[na + nb + ne + no:]
        k = pl.program_id(2)

        def products():
            sums = [None] * n_acc
            for ai, bi, ci, dn in pairs:
                d = _dot(a_refs[ai][...], b_refs[bi][...], dn)
                sums[ci] = d if sums[ci] is None else sums[ci] + d
            return sums

        def finish(accs):
            res = epilogue(accs, [e[...] for e in e_refs])
            for o, r in zip(o_refs, res):
                o[...] = r.astype(o.dtype)

        if nk == 1:
            finish(products())
            return

        @pl.when(k == 0)
        def _():
            for acc, d in zip(acc_refs, products()):
                acc[...] = d

        @pl.when(jnp.logical_and(k > 0, k < nk - 1))
        def _():
            for acc, d in zip(acc_refs, products()):
                acc[...] += d

        @pl.when(k == nk - 1)
        def _():
            finish([acc[...] + d for acc, d in zip(acc_refs, products())])

    ops = list(a_ops) + list(b_ops) + list(extras)
    return pl.pallas_call(
        body, name=name, grid=grid,
        in_specs=[s for _, s in ops],
        out_specs=[s for _, s in outs],
        out_shape=[s for s, _ in outs],
        scratch_shapes=[pltpu.VMEM(acc_shape, F32) for _ in range(n_acc if nk > 1 else 0)],
        compiler_params=_cparams(("parallel", "parallel", "arbitrary")),
    )(*[a for a, _ in ops])


def _bs(shape, fn):
    return pl.BlockSpec(shape, fn)


def _tile_m(L):
    return min(L, 512)


def _mm_nn(name, x, ws, tk, tn, n_out, extras, outs_sd, epilogue, tm=None, cb_offsets=None, more=None):
    M, K = x.shape
    tm = min(M, tm or _tile_m(M))
    grid = (M // tm, n_out // tn, K // tk)
    a_ops = [(x, _bs((tm, tk), lambda i, j, k: (i, k)))]
    cb_offsets = cb_offsets or [0] * len(ws)
    b_ops = [(w.a, _wspec(w, tk, tn, lambda i, j, k: k, (lambda off: lambda i, j, k: j + off)(off)))
             for w, off in zip(ws, cb_offsets)]
    pairs = [(0, bi, bi, _NN) for bi in range(len(ws))]
    mn = _bs((tm, tn), lambda i, j, k: (i, j))
    ex = [(e, mn) for e in extras] + (more(tm) if more else [])
    outs = [(jax.ShapeDtypeStruct((M, n_out), dt), mn) for dt in outs_sd]
    return _mm(name, grid, a_ops, b_ops, pairs, (tm, tn), len(ws), ex, outs, epilogue)


def _mm_nt(name, xs, ws, tn, tk, extras, outs_sd, epilogue, tm=None):
    M, Nw = xs[0].shape
    Kw = ws[0].full_shape()[0]
    tm = min(M, tm or _tile_m(M))
    grid = (M // tm, Kw // tn, Nw // tk)
    a_ops = [(x, _bs((tm, tk), lambda i, j, k: (i, k))) for x in xs]
    b_ops = [(w.a, _wspec(w, tn, tk, lambda i, j, k: j, lambda i, j, k: k)) for w in ws]
    pairs = [(i, i, 0, _NT) for i in range(len(ws))]
    mn = _bs((tm, tn), lambda i, j, k: (i, j))
    ex = [(e, mn) for e in extras]
    outs = [(jax.ShapeDtypeStruct((M, Kw), dt), mn) for dt in outs_sd]
    return _mm(name, grid, a_ops, b_ops, pairs, (tm, tn), 1, ex, outs, epilogue)


def _mm_tn(name, x, dys, kind, R, C, tm, tn, tk=None):
    L, Kw = x.shape
    Nw = dys[0].shape[1]
    tk = tk or min(L, 1024)
    grid = (Kw // tm, Nw // tn, L // tk)
    a_ops = [(x, _bs((tk, tm), lambda i, j, k: (k, i)))]
    b_ops = [(dy, _bs((tk, tn), lambda i, j, k: (k, j))) for dy in dys]
    pairs = [(0, bi, bi, _TN) for bi in range(len(dys))]
    gs = _gspec(kind, R, C, tm, tn)
    outs = [(jax.ShapeDtypeStruct((2, 4, R, C), _WIRE), gs) for _ in dys]
    return _mm(name, grid, a_ops, b_ops, pairs, (tm, tn), len(dys), [], outs, lambda accs, ex: accs)


def _row_tile(L):
    return min(L, 256)


def _rowwise(name, body, ins, outs, L, acc_outs=()):
    tr = _row_tile(L)
    n_in, n_out = len(ins), len(outs)

    def kbody(*refs):
        i_refs, o_refs, a_refs = refs[:n_in], refs[n_in:n_in + n_out], refs[n_in + n_out:]
        res, sums = body(*[r[...] for r in i_refs])
        for o, r in zip(o_refs, res):
            o[...] = r.astype(o.dtype)
        if a_refs:
            @pl.when(pl.program_id(0) == 0)
            def _():
                for a in a_refs:
                    a[...] = jnp.zeros(a.shape, F32)
            for a, s in zip(a_refs, sums):
                a[...] += s

    in_specs = []
    for arr, kind in ins:
        if kind == "row":
            in_specs.append(pl.BlockSpec((tr, arr.shape[1]), lambda i: (i, 0)))
        else:
            in_specs.append(pl.BlockSpec(arr.shape, lambda i: (0, 0)))
    out_specs = [pl.BlockSpec((tr, c), lambda i: (i, 0)) for c, _ in outs]
    out_shape = [jax.ShapeDtypeStruct((L, c), dt) for c, dt in outs]
    out_specs += [pl.BlockSpec((1, c), lambda i: (0, 0)) for c in acc_outs]
    out_shape += [jax.ShapeDtypeStruct((1, c), F32) for c in acc_outs]
    return pl.pallas_call(
        kbody, name=name, grid=(L // tr,), in_specs=in_specs, out_specs=out_specs, out_shape=out_shape,
        compiler_params=_cparams(("arbitrary",)),
    )(*[a for a, _ in ins])


def _rms_fwd(name, x, g):
    def body(xv, gv):
        r = lax.rsqrt(jnp.mean(xv * xv, axis=-1, keepdims=True) + _EPS)
        return [xv * r * gv], []
    return _rowwise(name, body, [(x, "row"), (g, "vec")], [(x.shape[1], _MXU)], x.shape[0])[0]


def _rms_bwd(name, dh, x, g, dres):
    def body(dhv, xv, gv, dr):
        r = lax.rsqrt(jnp.mean(xv * xv, axis=-1, keepdims=True) + _EPS)
        xh = xv * r
        dxh = dhv * gv
        dx = dr + r * (dxh - xh * jnp.mean(dxh * xh, axis=-1, keepdims=True))
        return [dx, dx], [jnp.sum(dhv * xh, axis=0, keepdims=True)]
    D = x.shape[1]
    return _rowwise(name, body, [(dh, "row"), (x, "row"), (g, "vec"), (dres, "row")], [(D, F32), (D, _MXU)],
                    x.shape[0], [D])


def _loss_head(x, g, target, pp, gt):
    D = x.shape[1]

    def body(xv, gv, tv, ppv, gtv):
        r = lax.rsqrt(jnp.mean(xv * xv, axis=-1, keepdims=True) + _EPS)
        xh = xv * r
        e = xh * gv - tv
        dy = e * (1.0 / D)
        dxh = dy * gv
        dx = r * (dxh - xh * jnp.mean(dxh * xh, axis=-1, keepdims=True))
        row_loss = 0.5 * jnp.mean(e * e, axis=-1, keepdims=True)
        lsum = jnp.sum(row_loss, axis=0, keepdims=True) + jnp.zeros((1, _LANES), F32)
        gt32 = gtv.astype(F32)
        return [dx, dx * ppv * gt32 * (1.0 - gt32), dx * gt32], [jnp.sum(dy * xh, axis=0, keepdims=True), lsum]
    return _rowwise("loss_head", body, [(x, "row"), (g, "vec"), (target, "row"), (pp, "row"), (gt, "row")],
                    [(D, F32), (D, _MXU), (D, _MXU)], x.shape[0], [D, _LANES])


def _ple_bwd_elem(dx, pp, gt):
    def body(dxv, ppv, gtv):
        gt32 = gtv.astype(F32)
        return [dxv * ppv * gt32 * (1.0 - gt32), dxv * gt32], []
    D = dx.shape[1]
    return _rowwise("ple_bwd_elem", body, [(dx, "row"), (pp, "row"), (gt, "row")], [(D, _MXU), (D, _MXU)], dx.shape[0])


def _glu_bwd_elem(dx, val, sg):
    def body(dxv, vv, sv):
        v32, s32 = vv.astype(F32), sv.astype(F32)
        return [jnp.concatenate([dxv * s32, dxv * v32 * s32 * (1.0 - s32)], axis=1)], []
    D = dx.shape[1]
    return _rowwise("glu_bwd_elem", body, [(dx, "row"), (val, "row"), (sg, "row")], [(2 * D, _MXU)], dx.shape[0])[0]


def _gm_common(z, ln_g, ln_b, wc_bf, bsT):
    W = z.shape[1] // 2
    zu, zv = z[:, :W], z[:, W:]
    u, v = _gelu(zu), _gelu(zv)
    mu = jnp.mean(v, axis=-1, keepdims=True)
    vc = v - mu
    rstd = lax.rsqrt(jnp.mean(vc * vc, axis=-1, keepdims=True) + _EPS)
    vh = vc * rstd
    vn = vh * ln_g + ln_b
    vnb = vn.astype(_MXU)
    svs = []
    for h in range(_GM_HEADS):
        sl = slice(h * _LANES, (h + 1) * _LANES)
        svs.append(_dot(wc_bf[h], vnb[:, sl], _NN) + bsT[:, h:h + 1])
    return zu, zv, u, vh, rstd, vnb, svs


def _causal(w):
    t = lax.broadcasted_iota(jnp.int32, w.shape, w.ndim - 2)
    s = lax.broadcasted_iota(jnp.int32, w.shape, w.ndim - 1)
    return jnp.where(s <= t, w, jnp.zeros_like(w))


def _gmlp_fwd(z, ln_g, ln_b, w_s, bsT):
    L, W2 = z.shape
    W = W2 // 2

    def body(z_ref, g_ref, b_ref, ws_ref, bs_ref, m_ref):
        wc = _causal(ws_ref[...]).astype(_MXU)
        _, _, u, _, _, _, svs = _gm_common(z_ref[...], g_ref[...], b_ref[...], wc, bs_ref[...])
        for h in range(_GM_HEADS):
            sl = slice(h * _LANES, (h + 1) * _LANES)
            m_ref[:, sl] = (u[:, sl] * svs[h]).astype(m_ref.dtype)

    return pl.pallas_call(
        body, name="gmlp_fwd", grid=(L // _GM_CHUNK,),
        in_specs=[pl.BlockSpec((_GM_CHUNK, W2), lambda n: (n, 0)),
                  pl.BlockSpec((1, W), lambda n: (0, 0)), pl.BlockSpec((1, W), lambda n: (0, 0)),
                  pl.BlockSpec(w_s.shape, lambda n: (0, 0, 0)), pl.BlockSpec(bsT.shape, lambda n: (0, 0))],
        out_specs=pl.BlockSpec((_GM_CHUNK, W), lambda n: (n, 0)),
        out_shape=jax.ShapeDtypeStruct((L, W), _MXU),
        compiler_params=_cparams(("arbitrary",)),
    )(z, ln_g, ln_b, w_s, bsT)


def _gmlp_bwd(z, dm, ln_g, ln_b, w_s, bsT):
    L, W2 = z.shape
    W = W2 // 2
    T = _GM_CHUNK

    def body(z_ref, dm_ref, g_ref, b_ref, ws_ref, bs_ref, dz_ref, dws_ref, dbs_ref, dg_ref, db_ref):
        @pl.when(pl.program_id(0) == 0)
        def _():
            dws_ref[...] = jnp.zeros(dws_ref.shape, F32)
            dbs_ref[...] = jnp.zeros(dbs_ref.shape, F32)
            dg_ref[...] = jnp.zeros(dg_ref.shape, F32)
            db_ref[...] = jnp.zeros(db_ref.shape, F32)

        wc = _causal(ws_ref[...]).astype(_MXU)
        ln_g_v = g_ref[...]
        zu, zv, u, vh, rstd, vnb, svs = _gm_common(z_ref[...], ln_g_v, b_ref[...], wc, bs_ref[...])
        dmv = dm_ref[...]
        lane = lax.broadcasted_iota(jnp.int32, (T, _LANES), 1)
        dbs = jnp.zeros((T, _LANES), F32)
        dvn_parts = []
        for h in range(_GM_HEADS):
            sl = slice(h * _LANES, (h + 1) * _LANES)
            dsv = dmv[:, sl] * u[:, sl]
            dz_ref[:, sl] = (dmv[:, sl] * svs[h] * _gelu_grad(zu[:, sl])).astype(dz_ref.dtype)
            dbs = dbs + jnp.where(lane == h, jnp.sum(dsv, axis=1, keepdims=True), 0.0)
            dsvb = dsv.astype(_MXU)
            dws_ref[h] += _dot(dsvb, vnb[:, sl], _NT)
            dvn_parts.append(_dot(wc[h], dsvb, _TN))
        dbs_ref[...] += dbs
        dvn = jnp.concatenate(dvn_parts, axis=1)
        dg_ref[...] += jnp.sum(dvn * vh, axis=0, keepdims=True)
        db_ref[...] += jnp.sum(dvn, axis=0, keepdims=True)
        dxh = dvn * ln_g_v
        dv = rstd * (dxh - jnp.mean(dxh, axis=-1, keepdims=True) - vh * jnp.mean(dxh * vh, axis=-1, keepdims=True))
        dz_ref[:, W:] = (dv * _gelu_grad(zv)).astype(dz_ref.dtype)

        @pl.when(pl.program_id(0) == pl.num_programs(0) - 1)
        def _():
            dws_ref[...] = _causal(dws_ref[...])

    return pl.pallas_call(
        body, name="gmlp_bwd", grid=(L // T,),
        in_specs=[pl.BlockSpec((T, W2), lambda n: (n, 0)), pl.BlockSpec((T, W), lambda n: (n, 0)),
                  pl.BlockSpec((1, W), lambda n: (0, 0)), pl.BlockSpec((1, W), lambda n: (0, 0)),
                  pl.BlockSpec(w_s.shape, lambda n: (0, 0, 0)), pl.BlockSpec(bsT.shape, lambda n: (0, 0))],
        out_specs=[pl.BlockSpec((T, W2), lambda n: (n, 0)),
                   pl.BlockSpec(w_s.shape, lambda n: (0, 0, 0)), pl.BlockSpec((T, _LANES), lambda n: (0, 0)),
                   pl.BlockSpec((1, W), lambda n: (0, 0)), pl.BlockSpec((1, W), lambda n: (0, 0))],
        out_shape=[jax.ShapeDtypeStruct((L, W2), _MXU), jax.ShapeDtypeStruct(w_s.shape, F32),
                   jax.ShapeDtypeStruct((T, _LANES), F32),
                   jax.ShapeDtypeStruct((1, W), F32), jax.ShapeDtypeStruct((1, W), F32)],
        compiler_params=_cparams(("arbitrary",)),
    )(z, dm, ln_g, ln_b, w_s, bsT)


def _s5_prep_math(a_re, a_im, log_dt):
    dt = jnp.exp(log_dt)
    xr, xi = a_re * dt, a_im * dt
    e = jnp.exp(xr)
    lbr, lbi = e * jnp.cos(xi), e * jnp.sin(xi)
    dn = a_re * a_re + a_im * a_im
    nr, ni = lbr - 1.0, lbi
    pr, pi = nr * a_re + ni * a_im, ni * a_re - nr * a_im
    return dt, lbr, lbi, dn, nr, ni, pr, pi


def _vm():
    return pl.BlockSpec(memory_space=pltpu.VMEM)


def _s5_prep(a_re, a_im, log_dt, b_re, b_im):
    def body(ar_ref, ai_ref, ld_ref, br_ref, bi_ref, lbr_ref, lbi_ref, Br_ref, Bi_ref):
        _, lbr, lbi, dn, _, _, pr, pi = _s5_prep_math(ar_ref[...], ai_ref[...], ld_ref[...])
        cr, ci = (pr / dn)[:, None, :], (pi / dn)[:, None, :]
        lbr_ref[...] = lbr
        lbi_ref[...] = lbi
        br, bi = br_ref[...], bi_ref[...]
        Br_ref[...] = cr * br - ci * bi
        Bi_ref[...] = cr * bi + ci * br

    sd = jax.ShapeDtypeStruct
    return pl.pallas_call(
        body, name="s5_prep", in_specs=[_vm()] * 5, out_specs=[_vm()] * 4,
        out_shape=[sd(a_re.shape, F32), sd(a_re.shape, F32), sd(b_re.shape, F32), sd(b_re.shape, F32)],
    )(a_re, a_im, log_dt, b_re, b_im)


def _s5_prep_bwd(a_re, a_im, log_dt, b_re, b_im, dlbr_s, dlbi_s, dBr, dBi):
    def body(ar_ref, ai_ref, ld_ref, br_ref, bi_ref, dlr_ref, dli_ref, dBr_ref, dBi_ref,
             dar_ref, dai_ref, dld_ref, dbr_ref, dbi_ref):
        a_re_v, a_im_v = ar_ref[...], ai_ref[...]
        dt, lbr, lbi, dn, nr, ni, pr, pi = _s5_prep_math(a_re_v, a_im_v, ld_ref[...])
        cr, ci = (pr / dn)[:, None, :], (pi / dn)[:, None, :]
        br, bi, dBr_v, dBi_v = br_ref[...], bi_ref[...], dBr_ref[...], dBi_ref[...]
        dbr_ref[...] = cr * dBr_v + ci * dBi_v
        dbi_ref[...] = cr * dBi_v - ci * dBr_v
        dcr = jnp.sum(br * dBr_v + bi * dBi_v, axis=1)
        dci = jnp.sum(br * dBi_v - bi * dBr_v, axis=1)
        dpr, dpi = dcr / dn, dci / dn
        ddn = -(dcr * pr + dci * pi) / (dn * dn)
        dnr = dpr * a_re_v - dpi * a_im_v
        dni = dpr * a_im_v + dpi * a_re_v
        dlbr = dlr_ref[...] + dnr
        dlbi = dli_ref[...] + dni
        dxr = dlbr * lbr + dlbi * lbi
        dxi = dlbi * lbr - dlbr * lbi
        dar_ref[...] = dpr * nr + dpi * ni + 2.0 * ddn * a_re_v + dxr * dt
        dai_ref[...] = dpr * ni - dpi * nr + 2.0 * ddn * a_im_v + dxi * dt
        dld_ref[...] = jnp.sum(dxr * a_re_v + dxi * a_im_v, axis=0, keepdims=True) * dt

    sd = jax.ShapeDtypeStruct
    return pl.pallas_call(
        body, name="s5_prep_bwd", in_specs=[_vm()] * 9, out_specs=[_vm()] * 5,
        out_shape=[sd(a_re.shape, F32), sd(a_re.shape, F32), sd(log_dt.shape, F32),
                   sd(b_re.shape, F32), sd(b_re.shape, F32)],
    )(a_re, a_im, log_dt, b_re, b_im, dlbr_s, dlbi_s, dBr, dBi)


def _shift_rows(v, down):
    n = v.shape[0]
    rolled = pltpu.roll(v, 1 if down else n - 1, 0)
    row = lax.broadcasted_iota(jnp.int32, v.shape, 0)
    return jnp.where(row == (0 if down else n - 1), 0.0, rolled)


def _cmul(ar, ai, br, bi):
    return ar * br - ai * bi, ar * bi + ai * br


_SEG = 8
_UNROLL = 8


def _seg_rows(k):
    if isinstance(k, int):
        return pl.ds(k * _SEG, _SEG)
    return pl.ds(pl.multiple_of(k * _SEG, _SEG), _SEG)


def _unrolled(n, step, init):
    main = n // _UNROLL

    def trip(kk, s):
        for uu in range(_UNROLL):
            s = step(kk * _UNROLL + uu, s)
        return s

    s = lax.fori_loop(0, main, trip, init)
    for r in range(main * _UNROLL, n):
        s = step(r, s)
    return s


def _interleave(src_ref, dst_ref, nk):
    def step(k, carry):
        dst_ref[_seg_rows(k), :] = src_ref[pl.ds(k, _SEG, stride=nk), :]
        return carry
    _unrolled(nk, step, 0)


def _deinterleave(src_ref, dst_ref, nk):
    def step(k, carry):
        dst_ref[pl.ds(k, _SEG, stride=nk), :] = src_ref[_seg_rows(k), :]
        return carry
    _unrolled(nk, step, 0)


def _segment_inits(er, ei, ar, ai, nk, down):
    pr, pi = ar, ai
    for _ in range(int(math.log2(nk))):
        pr, pi = _cmul(pr, pi, pr, pi)
    fr, fi = er, ei
    for _ in range(_SEG - 1):
        sr, si = _shift_rows(fr, down), _shift_rows(fi, down)
        mr, mi = _cmul(pr, pi, sr, si)
        fr, fi = er + mr, ei + mi
    return _shift_rows(fr, down), _shift_rows(fi, down)


def _scan_states(x_re, x_im, ar, ai, nk):
    lanes = ar.shape[1]

    def step(k, s):
        rows = _seg_rows(k)
        mr, mi = _cmul(ar, ai, s[0], s[1])
        return mr + x_re[rows, :], mi + x_im[rows, :]

    zero = jnp.zeros((_SEG, lanes), F32)
    er, ei = _unrolled(nk, step, (zero, zero))
    ir, ii = _segment_inits(er, ei, ar, ai, nk, True)

    def step2(k, s):
        rows = _seg_rows(k)
        mr, mi = _cmul(ar, ai, s[0], s[1])
        nr, ni = mr + x_re[rows, :], mi + x_im[rows, :]
        x_re[rows, :] = nr
        x_im[rows, :] = ni
        return nr, ni

    _unrolled(nk, step2, (ir, ii))


def _s5_tile_fwd(u, bd_re, bd_im, cd_re, cd_im, ar, ai, d, s_re, s_im, nk):
    s_re[...] = _dot(u, bd_re, _NN)
    s_im[...] = _dot(u, bd_im, _NN)
    _scan_states(s_re, s_im, ar, ai, nk)
    return _dot(s_re[...], cd_re, _NN) - _dot(s_im[...], cd_im, _NN) + d * u


def _s5_specs(L, T):
    lanes = _S5_GT * _S5_P
    u_spec = pl.BlockSpec((L, _LANES), lambda t: (0, t))
    bd_spec = pl.BlockSpec((None, _S5_GT, _S5_C, _S5_P), lambda t: (t, 0, 0, 0))
    cd_spec = pl.BlockSpec((None, _S5_GT, _S5_P, _S5_C), lambda t: (t, 0, 0, 0))
    lam_spec = pl.BlockSpec((None, 1, lanes), lambda t: (t, 0, 0))
    d_spec = pl.BlockSpec((1, _LANES), lambda t: (0, t))
    return lanes, u_spec, bd_spec, cd_spec, lam_spec, d_spec


def _fill_block_diag(dst_ref, blocks_ref):
    _, a, b = blocks_ref.shape
    dst_ref[...] = jnp.zeros(dst_ref.shape, F32)
    for g in range(_S5_GT):
        dst_ref[g * a:(g + 1) * a, g * b:(g + 1) * b] = blocks_ref[g]


def _take_block_diag(dst_ref, v):
    _, a, b = dst_ref.shape
    for g in range(_S5_GT):
        dst_ref[g] = v[g * a:(g + 1) * a, g * b:(g + 1) * b]


def _s5_dense(bdr, bdi, cdr, cdi, dense):
    for src, dst in zip((bdr, bdi, cdr, cdi), dense):
        _fill_block_diag(dst, src)
    return [dst[...] for dst in dense]


def _s5_dense_scratch(lanes):
    return [pltpu.VMEM((_LANES, lanes), F32), pltpu.VMEM((_LANES, lanes), F32),
            pltpu.VMEM((lanes, _LANES), F32), pltpu.VMEM((lanes, _LANES), F32)]


def _s5_fwd(u, bd_re, bd_im, cd_re, cd_im, lam_re, lam_im, d):
    L, Wd = u.shape
    T = Wd // _LANES
    nk = L // _SEG
    lanes, u_spec, bd_spec, cd_spec, lam_spec, d_spec = _s5_specs(L, T)
    s_spec = pl.BlockSpec((L, lanes), lambda t: (0, t))

    def body(u_ref, bdr, bdi, cdr, cdi, lr, li, d_ref, g_ref, s_re, s_im, up, tmp, *dense):
        ar = jnp.broadcast_to(lr[...], (_SEG, lanes))
        ai = jnp.broadcast_to(li[...], (_SEG, lanes))
        bd_re_v, bd_im_v, cd_re_v, cd_im_v = _s5_dense(bdr, bdi, cdr, cdi, dense)
        _interleave(u_ref, up, nk)
        y = _s5_tile_fwd(up[...], bd_re_v, bd_im_v, cd_re_v, cd_im_v, ar, ai, d_ref[...], s_re, s_im, nk)
        up[...] = _gelu(y)
        _deinterleave(up, tmp, nk)
        g_ref[...] = tmp[...].astype(g_ref.dtype)

    return pl.pallas_call(
        body, name="s5_fwd", grid=(T,),
        in_specs=[u_spec, bd_spec, bd_spec, cd_spec, cd_spec, lam_spec, lam_spec, d_spec],
        out_specs=[u_spec, s_spec, s_spec],
        out_shape=[jax.ShapeDtypeStruct((L, Wd), _MXU), jax.ShapeDtypeStruct((L, T * lanes), F32),
                   jax.ShapeDtypeStruct((L, T * lanes), F32)],
        scratch_shapes=[pltpu.VMEM((L, _LANES), F32) for _ in range(2)] + _s5_dense_scratch(lanes),
        compiler_params=_cparams(("arbitrary",)),
    )(u, bd_re, bd_im, cd_re, cd_im, lam_re, lam_im, d)


def _s5_bwd(u, dg, states_re, states_im, bd_re, bd_im, cd_re, cd_im, lam_re, lam_im, d):
    L, Wd = u.shape
    T = Wd // _LANES
    nk = L // _SEG
    lanes, u_spec, bd_spec, cd_spec, lam_spec, d_spec = _s5_specs(L, T)
    s_spec = pl.BlockSpec((L, lanes), lambda t: (0, t))

    def body(u_ref, dg_ref, s_re, s_im, bdr, bdi, cdr, cdi, lr, li, d_ref,
             du_ref, dbdr, dbdi, dcdr, dcdi, dlr, dli, dd_ref, g_re, g_im, up, dgp, tmp, *dense):
        ar = jnp.broadcast_to(lr[...], (_SEG, lanes))
        ai = jnp.broadcast_to(li[...], (_SEG, lanes))
        bd_re_v, bd_im_v, cd_re_v, cd_im_v = _s5_dense(bdr, bdi, cdr, cdi, dense)
        _interleave(u_ref, up, nk)
        _interleave(dg_ref, dgp, nk)
        uv, dv = up[...], d_ref[...]
        y = _dot(s_re[...], cd_re_v, _NN) - _dot(s_im[...], cd_im_v, _NN) + dv * uv
        dy = dgp[...] * _gelu_grad(y)
        dd_ref[...] = jnp.sum(dy * uv, axis=0, keepdims=True)
        dyb = dy.astype(_MXU)
        _take_block_diag(dcdr, _dot(dyb, s_re[...], _TN))
        _take_block_diag(dcdi, -_dot(dyb, s_im[...], _TN))
        g_re[...] = _dot(dyb, cd_re_v, _NT)
        g_im[...] = -_dot(dyb, cd_im_v, _NT)

        nai = -ai

        def step(j, s):
            rows = _seg_rows(nk - 1 - j)
            mr, mi = _cmul(ar, nai, s[0], s[1])
            return mr + g_re[rows, :], mi + g_im[rows, :]

        zero = jnp.zeros((_SEG, lanes), F32)
        er, ei = _unrolled(nk, step, (zero, zero))
        ir, ii = _segment_inits(er, ei, ar, nai, nk, False)

        def acc_lam(gr, gi, pr, pi, acc):
            return acc[0] + gr * pr + gi * pi, acc[1] + gi * pr - gr * pi

        def step2(j, carry):
            s, acc = carry
            k = nk - 1 - j
            rows = _seg_rows(k)
            mr, mi = _cmul(ar, nai, s[0], s[1])
            nr, ni = mr + g_re[rows, :], mi + g_im[rows, :]
            g_re[rows, :] = nr
            g_im[rows, :] = ni
            prev = _seg_rows(k - 1)
            return (nr, ni), acc_lam(nr, ni, s_re[prev, :], s_im[prev, :], acc)

        (g0r, g0i), acc = _unrolled(nk - 1, step2, ((ir, ii), (zero, zero)))
        first = _seg_rows(0)
        mr, mi = _cmul(ar, nai, g0r, g0i)
        nr, ni = mr + g_re[first, :], mi + g_im[first, :]
        g_re[first, :] = nr
        g_im[first, :] = ni
        last = _seg_rows(nk - 1)
        acc = acc_lam(nr, ni, _shift_rows(s_re[last, :], True), _shift_rows(s_im[last, :], True), acc)
        dlr[...] = jnp.sum(acc[0], axis=0, keepdims=True)
        dli[...] = jnp.sum(acc[1], axis=0, keepdims=True)

        gtr, gti = g_re[...].astype(_MXU), g_im[...].astype(_MXU)
        ub = uv.astype(_MXU)
        _take_block_diag(dbdr, _dot(ub, gtr, _TN))
        _take_block_diag(dbdi, _dot(ub, gti, _TN))
        dgp[...] = _dot(gtr, bd_re_v, _NT) + _dot(gti, bd_im_v, _NT) + dy * dv
        _deinterleave(dgp, tmp, nk)
        du_ref[...] = tmp[...].astype(du_ref.dtype)

    sd = jax.ShapeDtypeStruct
    big = sd((T, _S5_GT, _S5_C, _S5_P), F32)
    return pl.pallas_call(
        body, name="s5_bwd", grid=(T,),
        in_specs=[u_spec, u_spec, s_spec, s_spec, bd_spec, bd_spec, cd_spec, cd_spec, lam_spec, lam_spec, d_spec],
        out_specs=[u_spec, bd_spec, bd_spec, bd_spec, bd_spec, lam_spec, lam_spec, d_spec],
        out_shape=[sd((L, Wd), _MXU), big, big, big, big, sd((T, 1, lanes), F32), sd((T, 1, lanes), F32),
                   sd((1, Wd), F32)],
        scratch_shapes=[pltpu.VMEM((L, lanes), F32) for _ in range(2)]
        + [pltpu.VMEM((L, _LANES), F32) for _ in range(3)] + _s5_dense_scratch(lanes),
        compiler_params=_cparams(("arbitrary",)),
    )(u, dg, states_re, states_im, bd_re, bd_im, cd_re, cd_im, lam_re, lam_im, d)


def _half_tile(R, few_arrays=False):
    for t in ((512, 704, 128) if few_arrays else (256, 352, 128)):
        if R % t == 0:
            return t
    raise ValueError(R)


def _cast_shard(name, w, layer, kind, R, C):
    tr = _half_tile(R, True)
    nr = R // tr

    def body(w_ref, o_ref):
        o_ref[...] = w_ref[...].astype(o_ref.dtype)

    if kind == "col":
        in_map = lambda h, i: (layer, h * nr + i, 0)
    else:
        in_map = lambda h, i: (layer, i, h)
    return pl.pallas_call(
        body, name=name, grid=(2, nr), in_specs=[pl.BlockSpec((None, tr, C), in_map)],
        out_specs=pl.BlockSpec((None, tr, C), lambda h, i: (h, i, 0)),
        out_shape=jax.ShapeDtypeStruct((2, R, C), _WIRE),
        compiler_params=_cparams(("arbitrary", "arbitrary")),
    )(w)


def _adam_math(w, g, m, v):
    m2 = _B1 * m + (1.0 - _B1) * g
    v2 = _B2 * v + (1.0 - _B2) * (g * g)
    m_hat = m2 / (1.0 - _B1 ** _STEP)
    v_hat = v2 / (1.0 - _B2 ** _STEP)
    delta = -_LR * (m_hat / (jnp.sqrt(v_hat) + _AEPS) + _WD * w)
    return delta, m2, v2


def _adamw_big(name, w, m, v, layer, pair, kind, R, C, c, after, prev):
    tr = _half_tile(R, few_arrays=C <= 1024 and R % 512 == 0)
    nr = R // tr

    def body(c_ref, w_ref, m_ref, v_ref, own_ref, other_ref, *rest):
        go_ref, d_ref, mo_ref, vo_ref = rest[-4:]
        g = jnp.where(pl.program_id(0) == c_ref[0], own_ref[...], other_ref[...])
        delta, m2, v2 = _adam_math(w_ref[...], g, m_ref[...], v_ref[...])
        go_ref[...] = g
        d_ref[...] = delta
        mo_ref[...] = m2
        vo_ref[...] = v2

    if kind == "col":
        nat = pl.BlockSpec((None, tr, C), lambda h, i, c_ref: (layer, h * nr + i, 0))
    else:
        nat = pl.BlockSpec((None, tr, C), lambda h, i, c_ref: (layer, i, h))

    def gspec(own):
        return pl.BlockSpec((tr, C), lambda h, i, c_ref: (jnp.where((h == c_ref[0]) == own, i, 0), 0))

    carried = list(prev) if prev is not None else []
    gs = pltpu.PrefetchScalarGridSpec(
        num_scalar_prefetch=1, grid=(2, nr),
        in_specs=[nat, nat, nat, gspec(True), gspec(False), _any()] + [_any()] * len(carried),
        out_specs=[nat] * 4)
    sd = jax.ShapeDtypeStruct(w.shape, F32)
    return pl.pallas_call(
        body, name=name, grid_spec=gs, out_shape=[sd] * 4,
        input_output_aliases={7 + k: k for k in range(len(carried))},
        compiler_params=_cparams(("arbitrary", "arbitrary")),
    )(c, w, m, v, pair[0], pair[1], after, *carried)


def _adamw_small(ws, gs, ms, vs):
    n = len(ws)

    def body(*refs):
        for i in range(n):
            w_ref, g_ref, m_ref, v_ref, d_ref, mo_ref, vo_ref = refs[i::n]
            delta, m2, v2 = _adam_math(w_ref[...], g_ref[...], m_ref[...], v_ref[...])
            d_ref[...] = delta
            mo_ref[...] = m2
            vo_ref[...] = v2

    sds = [jax.ShapeDtypeStruct(w.shape, F32) for w in ws]
    outs = pl.pallas_call(
        body, name="adamw_small", in_specs=[_vm()] * (4 * n), out_specs=[_vm()] * (3 * n), out_shape=sds * 3,
        compiler_params=pltpu.CompilerParams(vmem_limit_bytes=_VMEM_LIMIT),
    )(*ws, *gs, *ms, *vs)
    return outs[:n], outs[n:2 * n], outs[2 * n:]


def _place():
    x, y, c = lax.axis_index("x"), lax.axis_index("y"), lax.axis_index("c")
    chips = [(1 - x, y), (x, 1 - y), (1 - x, 1 - y)]
    return x, y, c, 2 * x + y, chips


def _any():
    return pl.BlockSpec(memory_space=pl.ANY)


def _remote(src, dst, ssem, rsem, dev):
    return pltpu.make_async_remote_copy(src_ref=src, dst_ref=dst, send_sem=ssem, recv_sem=rsem,
                                        device_id=dev, device_id_type=_MESH)


def _allgather(name, shards):
    n = len(shards)

    def body(*refs):
        s_refs, g_refs = refs[:n], refs[n:2 * n]
        send0, recv0, send1, recv1, send2, recv2 = refs[2 * n:]
        x, y, c, q, _ = _place()
        sib, xn, yn = (x, y, 1 - c), (1 - x, y, c), (x, 1 - y, c)
        qx, qy, qd = 2 * (1 - x) + y, 2 * x + (1 - y), 2 * (1 - x) + (1 - y)
        _handshake([sib, xn, yn])
        own = [_remote(s_refs[a], g_refs[a].at[q], send0.at[a], recv0.at[a], sib) for a in range(n)]

        def pieces(a):
            g, half = g_refs[a], s_refs[a].shape[1] // 2
            return [g.at[qx, c], g.at[qy, c], g.at[qd, c, pl.ds(0, half)], g.at[qd, c, pl.ds(half, half)]]

        def relayed(a):
            g, half = g_refs[a], s_refs[a].shape[1] // 2
            return [(g.at[qx, c, pl.ds(0, half)], yn), (g.at[qy, c, pl.ds(half, half)], xn)]

        first = []
        for a in range(n):
            first.append(_remote(s_refs[a].at[c], g_refs[a].at[q, c], send1.at[4 * a], recv1.at[4 * a], xn))
            first.append(_remote(s_refs[a].at[c], g_refs[a].at[q, c], send1.at[4 * a + 1], recv1.at[4 * a + 1], yn))
        for cp in first + own:
            cp.start()
        later = []
        for a in range(n):
            land = pieces(a)
            for j in range(4):
                k = 4 * a + j
                _remote(land[j], land[j], send1.at[k], recv1.at[k], xn).wait_recv()
                if j < 2:
                    src, to = relayed(a)[j]
                    cp = _remote(src, src, send1.at[k + 2], recv1.at[k + 2], to)
                    cp.start()
                    later.append(cp)
                cp = _remote(land[j], land[j], send2.at[k], recv2.at[k], sib)
                cp.start()
                later.append(cp)
        for a in range(n):
            g, half = g_refs[a], s_refs[a].shape[1] // 2
            theirs = [g.at[qx, 1 - c], g.at[qy, 1 - c], g.at[qd, 1 - c, pl.ds(0, half)],
                      g.at[qd, 1 - c, pl.ds(half, half)]]
            for j in range(4):
                _remote(theirs[j], theirs[j], send2.at[4 * a + j], recv2.at[4 * a + j], sib).wait_recv()
        for cp in own:
            cp.wait()
        for cp in first + later:
            cp.wait_send()

    return _sequencer(name, _ID_GATHER, body, shards,
                      [jax.ShapeDtypeStruct((4,) + s.shape, s.dtype) for s in shards], [n, n] + [4 * n] * 4)


def _handshake(peers):
    barrier = pltpu.get_barrier_semaphore()
    for peer in peers:
        pl.semaphore_signal(barrier, inc=1, device_id=peer, device_id_type=_MESH)
    pl.semaphore_wait(barrier, len(peers))


_ID_SIBLING, _ID_CHIPS, _ID_GATHER, _ID_ALL = 1, 2, 3, 4


def _sequencer(name, collective_id, body, ins, out_types, sem_counts):
    mesh = plsc.ScalarSubcoreMesh(axis_name="seq", num_cores=1)
    moved = sum(math.prod(o.shape) * jnp.dtype(o.dtype).itemsize for o in out_types)
    return pl.kernel(
        body, name=name, out_type=out_types, mesh=mesh,
        scratch_types=[pltpu.SemaphoreType.DMA((k,)) for k in sem_counts],
        compiler_params=pltpu.CompilerParams(collective_id=collective_id),
        cost_estimate=pl.CostEstimate(flops=0, transcendentals=0, bytes_accessed=2 * moved,
                                      remote_bytes_transferred=moved),
    )(*ins)


def _swap_halves(name, grads):
    n = len(grads)

    def body(*refs):
        g_refs, t_refs = refs[:n], refs[n:2 * n]
        send, recv = refs[2 * n:]
        x, y, c, _, _ = _place()
        _handshake([(x, y, 1 - c)])
        cps = [_remote(g_refs[a].at[1 - c], t_refs[a], send.at[a], recv.at[a], (x, y, 1 - c)) for a in range(n)]
        for cp in cps:
            cp.start()
        for cp in cps:
            cp.wait()

    return _sequencer(name, _ID_SIBLING, body, grads,
                      [jax.ShapeDtypeStruct(g.shape[1:], g.dtype) for g in grads], [n, n])


def _chip_sum(name, grads, swapped, after, core):
    n = len(grads)
    _, _, R, C = grads[0].shape
    tr = _half_tile(R, True)

    def body(c_ref, *refs):
        for g_ref, t_ref, o_ref in zip(refs[:n], refs[n:2 * n], refs[2 * n + 1:]):
            o_ref[...] = (g_ref[...].astype(F32) + t_ref[...].astype(F32)).astype(o_ref.dtype)

    own = pl.BlockSpec((None, None, tr, C), lambda r, i, c_ref: (c_ref[0], r, i, 0))
    part = pl.BlockSpec((None, tr, C), lambda r, i, c_ref: (r, i, 0))
    gs = pltpu.PrefetchScalarGridSpec(
        num_scalar_prefetch=1, grid=(4, R // tr), in_specs=[own] * n + [part] * n + [_any()], out_specs=[part] * n)
    return pl.pallas_call(
        body, name=name, grid_spec=gs, out_shape=[jax.ShapeDtypeStruct((4, R, C), _WIRE)] * n,
        compiler_params=_cparams(("arbitrary", "arbitrary")),
    )(core, *grads, *swapped, after)


def _scatter_parts(name, parts):
    n = len(parts)

    def body(*refs):
        p_refs, t_refs = refs[:n], refs[n:2 * n]
        send, recv = refs[2 * n:]
        x, y, c, q, chips = _place()
        _handshake([(rx, ry, c) for rx, ry in chips])
        cps = []
        for a in range(n):
            for j, (rx, ry) in enumerate(chips):
                k = 3 * a + j
                cps.append(_remote(p_refs[a].at[2 * rx + ry], t_refs[a].at[q], send.at[k], recv.at[k], (rx, ry, c)))
        for cp in cps:
            cp.start()
        for a in range(n):
            for j, (rx, ry) in enumerate(chips):
                k = 3 * a + j
                land = t_refs[a].at[2 * rx + ry]
                _remote(land, land, send.at[k], recv.at[k], (rx, ry, c)).wait_recv()
        for cp in cps:
            cp.wait_send()

    return _sequencer(name, _ID_CHIPS, body, parts,
                      [jax.ShapeDtypeStruct(p.shape, p.dtype) for p in parts], [3 * n, 3 * n])


def _sum_parts(name, parts, landed, where, after):
    n = len(parts)
    _, R, C = parts[0].shape
    tr = _half_tile(R, True)

    def body(w_ref, *refs):
        for a in range(n):
            p_ref, t0_ref, t1_ref, t2_ref = refs[4 * a:4 * a + 4]
            refs[4 * n + 1 + a][...] = (p_ref[...].astype(F32) + t0_ref[...].astype(F32)
                                        + t1_ref[...].astype(F32) + t2_ref[...].astype(F32))

    def part(slot):
        return pl.BlockSpec((None, tr, C), lambda i, w_ref: (w_ref[slot], i, 0))

    gs = pltpu.PrefetchScalarGridSpec(
        num_scalar_prefetch=1, grid=(R // tr,), in_specs=[part(0), part(1), part(2), part(3)] * n + [_any()],
        out_specs=[pl.BlockSpec((tr, C), lambda i, w_ref: (i, 0))] * n)
    ops = [x for p, t in zip(parts, landed) for x in (p, t, t, t)]
    return pl.pallas_call(
        body, name=name, grid_spec=gs, out_shape=[jax.ShapeDtypeStruct((R, C), F32)] * n,
        compiler_params=_cparams(("arbitrary",)),
    )(where, *ops, after)


def _send_halves(name, halves):
    n = len(halves)

    def body(*refs):
        h_refs, o_refs = refs[:n], refs[n:2 * n]
        send, recv = refs[2 * n:]
        x, y, c, _, _ = _place()
        _handshake([(x, y, 1 - c)])
        cps = [_remote(h_refs[a], o_refs[a], send.at[a], recv.at[a], (x, y, 1 - c)) for a in range(n)]
        for cp in cps:
            cp.start()
        for cp in cps:
            cp.wait()

    return _sequencer(name, _ID_SIBLING, body, halves,
                      [jax.ShapeDtypeStruct(h.shape, h.dtype) for h in halves], [n, n])


class _Order:
    def __init__(self):
        self.tok = None
        x, y, c, q, chips = _place()
        self.core = c.astype(jnp.int32).reshape(1)
        self.where = jnp.stack([q] + [2 * rx + ry for rx, ry in chips]).astype(jnp.int32)

    def tie(self, x):
        return x if self.tok is None else lax.optimization_barrier((x, self.tok))[0]

    def done(self, outs):
        self.tok = outs[0]
        return outs


class _Reduction:
    def __init__(self, tag, grads):
        self.tag, self.grads = tag, grads
        self.swapped = _swap_halves(f"rs_swap_{tag}", grads)

    def _by_shape(self):
        groups = {}
        for a, g in enumerate(self.grads):
            groups.setdefault(g.shape, []).append(a)
        return list(groups.values())

    def scatter(self, seq):
        self.parts = [None] * len(self.grads)
        for idx in self._by_shape():
            outs = seq.done(_chip_sum(f"rs_chipsum_{self.tag}_{idx[0]}", [self.grads[a] for a in idx],
                                      [self.swapped[a] for a in idx], seq.tok, seq.core))
            for a, o in zip(idx, outs):
                self.parts[a] = o
        self.landed = _scatter_parts(f"rs_scatter_{self.tag}", self.parts)

    def finish(self, seq):
        halves = [None] * len(self.grads)
        for idx in self._by_shape():
            outs = seq.done(_sum_parts(f"rs_sum_{self.tag}_{idx[0]}", [self.parts[a] for a in idx],
                                       [self.landed[a] for a in idx], seq.where, seq.tok))
            for a, o in zip(idx, outs):
                halves[a] = o
        return list(zip(halves, _send_halves(f"rs_join_{self.tag}", halves)))


def _allreduce_small(name, v):
    _, R, _ = v.shape

    def body(v_ref, o_ref, land, acc, send1, recv1, send2, recv2):
        x, y, c = lax.axis_index("x"), lax.axis_index("y"), lax.axis_index("c")
        me = 4 * x + 2 * y + c
        peers = []
        for k in range(1, 8):
            dx, dy, dc = (k >> 2) & 1, (k >> 1) & 1, k & 1
            px, py, pc = (1 - x if dx else x), (1 - y if dy else y), (1 - c if dc else c)
            peers.append((k, (px, py, pc), 4 * px + 2 * py + pc))
        land[me] = v_ref[me]
        out1 = [_remote(v_ref.at[pid], land.at[me], send1.at[k], recv1.at[k], dev) for k, dev, pid in peers]
        for cp in out1:
            cp.start()
        for k, dev, pid in peers:
            _remote(land.at[pid], land.at[pid], send1.at[k], recv1.at[k], dev).wait_recv()
        total = land[0]
        for j in range(1, 8):
            total = total + land[j]
        acc[...] = total
        o_ref[me] = total
        out2 = [_remote(acc, o_ref.at[me], send2.at[k], recv2.at[k], dev) for k, dev, pid in peers]
        for cp in out2:
            cp.start()
        for k, dev, pid in peers:
            _remote(o_ref.at[pid], o_ref.at[pid], send2.at[k], recv2.at[k], dev).wait_recv()
        for cp in out1 + out2:
            cp.wait_send()

    return pl.pallas_call(
        body, name=name, in_specs=[_vm()], out_specs=_vm(),
        out_shape=jax.ShapeDtypeStruct(v.shape, F32),
        scratch_shapes=[pltpu.VMEM(v.shape, F32), pltpu.VMEM((R, _LANES), F32)]
        + [pltpu.SemaphoreType.DMA((8,)) for _ in range(4)],
        compiler_params=pltpu.CompilerParams(vmem_limit_bytes=_VMEM_LIMIT),
    )(v)


def _all_peers():
    x, y, c = lax.axis_index("x"), lax.axis_index("y"), lax.axis_index("c")
    peers = []
    for k in range(1, 8):
        px, py, pc = (1 - x if k & 4 else x), (1 - y if k & 2 else y), (1 - c if k & 1 else c)
        peers.append((k, (px, py, pc), 4 * px + 2 * py + pc))
    return 4 * x + 2 * y + c, peers


def _exchange_slices(name, v):
    def body(v_ref, land, send, recv):
        me, peers = _all_peers()
        _handshake([dev for _, dev, _ in peers])
        cps = [_remote(v_ref.at[pid], land.at[me], send.at[k], recv.at[k], dev) for k, dev, pid in peers]
        for cp in cps:
            cp.start()
        for k, dev, pid in peers:
            _remote(land.at[pid], land.at[pid], send.at[k], recv.at[k], dev).wait_recv()
        for cp in cps:
            cp.wait_send()

    return _sequencer(name, _ID_ALL, body, [v], [jax.ShapeDtypeStruct(v.shape, v.dtype)], [8, 8])[0]


def _sum_slices(name, v, landed, after):
    _, R, _ = v.shape

    def body(v_ref, land_ref, after_ref, o_ref):
        me, peers = _all_peers()
        acc = v_ref[me]
        for _, _, pid in peers:
            acc = acc + land_ref[pid]
        o_ref[...] = acc

    return pl.pallas_call(
        body, name=name, in_specs=[_vm(), _vm(), _any()], out_specs=_vm(),
        out_shape=jax.ShapeDtypeStruct((R, _LANES), F32),
        compiler_params=pltpu.CompilerParams(vmem_limit_bytes=_VMEM_LIMIT),
    )(v, landed, after)


def _broadcast_slices(name, s):
    def body(s_ref, out, send, recv):
        me, peers = _all_peers()
        _handshake([dev for _, dev, _ in peers])
        cps = [_remote(s_ref, out.at[me], send.at[k], recv.at[k], dev) for k, dev, pid in peers]
        for cp in cps:
            cp.start()
        for k, dev, pid in peers:
            _remote(out.at[pid], out.at[pid], send.at[k], recv.at[k], dev).wait_recv()
        for cp in cps:
            cp.wait_send()

    return _sequencer(name, _ID_ALL, body, [s], [jax.ShapeDtypeStruct((8,) + s.shape, s.dtype)], [8, 8])[0]


_WEIGHT_NAMES = ['norm_mix', 'norm_ffn', 'norm_ple', 'norm_final', 'gm_w_in', 'gm_ln_g', 'gm_ln_b', 'gm_w_s',
                 'gm_b_s', 'gm_w_out', 's5_w_in', 's5_a_re', 's5_a_im', 's5_log_dt', 's5_b_re', 's5_b_im',
                 's5_c_re', 's5_c_im', 's5_d', 's5_w_out', 'ffn_w1', 'ffn_w3', 'ffn_w2', 'ple_w_gate', 'ple_w_proj']
_BIG = {'gm_w_in': 'col', 'gm_w_out': 'row', 's5_w_in': 'row', 's5_w_out': 'col', 'ffn_w1': 'col',
        'ffn_w3': 'col', 'ffn_w2': 'row', 'ple_w_gate': 'row', 'ple_w_proj': 'col'}


_VIEW = {'s5_a_re': (0, 2, 1), 's5_a_im': (0, 2, 1), 's5_b_re': (0, 2, 3, 1), 's5_b_im': (0, 2, 3, 1),
         's5_c_re': (0, 2, 3, 1), 's5_c_im': (0, 2, 3, 1)}


def _to_view(name, a):
    return jnp.transpose(a, _VIEW[name]) if name in _VIEW else a


def _from_view(name, a):
    if name not in _VIEW:
        return a
    perm = _VIEW[name]
    return jnp.transpose(a, [perm.index(i) for i in range(len(perm))])


def _rc(kind, shard_shape):
    rows, cols = shard_shape[-2:]
    return (rows // 2, cols) if kind == "col" else (rows, cols // 2)


def _pack(vecs, rows_multiple):
    flat = jnp.concatenate([a.reshape(-1).astype(F32) for a in vecs])
    unit = rows_multiple * _LANES
    pad = (-flat.shape[0]) % unit
    return jnp.pad(flat, (0, pad)).reshape(-1, _LANES)


def _unpack(buf, shapes):
    flat = buf.reshape(-1)
    out, off = [], 0
    for s in shapes:
        n = math.prod(s)
        out.append(flat[off:off + n].reshape(s))
        off += n
    return out


def _ident(accs, ex):
    return accs


def _add_resid(accs, ex):
    return [accs[0] + ex[0]]


def _swiglu_epi(accs, ex):
    a, b = accs
    return [a, b, a * _sig(a) * b]


def _swiglu_bwd_epi(accs, ex):
    df = accs[0]
    a, b = ex[0].astype(F32), ex[1].astype(F32)
    sa = _sig(a)
    return [df * b * (sa * (1.0 + a * (1.0 - sa))), df * (a * sa)]


def _ple_epi(accs, ex):
    xin, pv = ex[0], ex[1]
    kh = pv.shape[1] // 2
    pp = jnp.concatenate([_dot(pv[:, :kh], ex[2 + 2 * part], _NN) + _dot(pv[:, kh:], ex[3 + 2 * part], _NN)
                          for part in (0, 1)], axis=1)
    gt = _sig(accs[0])
    return [xin + gt * pp, gt, pp]


def _glu_epi(accs, ex):
    val, sg = accs[0], _sig(accs[1])
    return [ex[0] + val * sg, val, sg]


def kernel(x, p, norm_mix, norm_ffn, norm_ple, norm_final, gm_w_in, gm_ln_g, gm_ln_b, gm_w_s, gm_b_s, gm_w_out, s5_w_in, s5_a_re, s5_a_im, s5_log_dt, s5_b_re, s5_b_im, s5_c_re, s5_c_im, s5_d, s5_w_out, ffn_w1, ffn_w3, ffn_w2, ple_w_gate, ple_w_proj, loss_target, m_norm_mix, m_norm_ffn, m_norm_ple, m_norm_final, m_gm_w_in, m_gm_ln_g, m_gm_ln_b, m_gm_w_s, m_gm_b_s, m_gm_w_out, m_s5_w_in, m_s5_a_re, m_s5_a_im, m_s5_log_dt, m_s5_b_re, m_s5_b_im, m_s5_c_re, m_s5_c_im, m_s5_d, m_s5_w_out, m_ffn_w1, m_ffn_w3, m_ffn_w2, m_ple_w_gate, m_ple_w_proj, v_norm_mix, v_norm_ffn, v_norm_ple, v_norm_final, v_gm_w_in, v_gm_ln_g, v_gm_ln_b, v_gm_w_s, v_gm_b_s, v_gm_w_out, v_s5_w_in, v_s5_a_re, v_s5_a_im, v_s5_log_dt, v_s5_b_re, v_s5_b_im, v_s5_c_re, v_s5_c_im, v_s5_d, v_s5_w_out, v_ffn_w1, v_ffn_w3, v_ffn_w2, v_ple_w_gate, v_ple_w_proj):
    env = dict(locals())
    w = {n: env[n] for n in _WEIGHT_NAMES}
    mom = {n: env["m_" + n] for n in _WEIGHT_NAMES}
    var = {n: env["v_" + n] for n in _WEIGHT_NAMES}
    xs, tgt = x[0], loss_target[0]
    L, D = xs.shape
    depth = norm_mix.shape[0]
    qx, qy = lax.axis_index("x"), lax.axis_index("y")
    q = 2 * qx + qy

    W, last_cast = {}, [None]

    def gather(tag, items, after=None):
        shards = []
        for name, layer in items:
            kind = _BIG[name]
            R, C = _rc(kind, w[name].shape)
            src = w[name] if after is None else lax.optimization_barrier((w[name], after))[0]
            shards.append(_cast_shard(f"cast_{name}{layer}", src, layer, kind, R, C))
        last_cast[0] = shards[-1]
        full = _allgather(f"ag_{tag}", shards)
        W.update({it: _W(f, _BIG[it[0]]) for it, f in zip(items, full)})

    gather("gm_w_in", [("gm_w_in", 0)])
    gather("gm_w_out", [("gm_w_out", 0)])
    gather("ffn_up0", [("ffn_w1", 0), ("ffn_w3", 0)])
    gather("ffn_down0", [("ffn_w2", 0)])

    def gather_rest(after):
        gather("ple0", [("ple_w_gate", 0), ("ple_w_proj", 0)], after)
        gather("s5_w_in", [("s5_w_in", 0)], after)
        gather("s5_w_out", [("s5_w_out", 0)], after)
        gather("ffn_up1", [("ffn_w1", 1), ("ffn_w3", 1)], after)
        gather("ffn_down1", [("ffn_w2", 1)], after)
        gather("ple1", [("ple_w_gate", 1), ("ple_w_proj", 1)], after)

    d_slots = jnp.zeros((4, D // 4), F32)
    d_slots = lax.dynamic_update_slice(d_slots, s5_d.astype(F32), (q, 0))
    d_sum = _allreduce_small("ar_s5_d", _pack([d_slots], 64).reshape(8, -1, _LANES))
    d_full = (d_sum.reshape(-1)[:D] * 0.5).reshape(1, D)

    def ffn_fwd(i, xin):
        hf = _rms_fwd(f"rms_ffn{i}", xin, norm_ffn[i:i + 1])
        a, b, f = _mm_nn(f"ffn_up{i}", hf, [W["ffn_w1", i], W["ffn_w3", i]], 1024, 1408, ffn_w2.shape[1] * 4,
                         [], [_MXU, _MXU, _MXU], _swiglu_epi, tm=1024)
        xo = _mm_nn(f"ffn_down{i}", f, [W["ffn_w2", i]], 1408, 1024, D, [xin], [F32], _add_resid, tm=1024)[0]
        return xo, (xin, hf, a, b, f)

    def ple_fwd(i, xin):
        hp = _rms_fwd(f"rms_ple{i}", xin, norm_ple[i:i + 1])
        pi, wp = p[i, 0], W["ple_w_proj", i]

        def more(tm):
            ops = [(pi, pl.BlockSpec((tm, pi.shape[1]), lambda i_, j, k: (i_, 0)))]
            for part in (0, 1):
                for half in (0, 1):
                    ops.append((wp.a, pl.BlockSpec((None, None, wp.R, wp.C),
                                                   lambda i_, j, k, part=part, half=half: (2 * j + part, half, 0, 0))))
            return ops

        xo, gt, pp = _mm_nn(f"ple_gate{i}", hp, [W["ple_w_gate", i]], 512, 2 * wp.C, D, [xin], [F32, _MXU, F32],
                            _ple_epi, tm=1024, more=more)
        return xo, (xin, hp, pi, pp, gt)

    h0 = _rms_fwd("rms_mix0", xs, norm_mix[0:1])
    z = _mm_nn("gm_in", h0, [W["gm_w_in", 0]], 1024, 1024, 2 * D, [], [F32], _ident, tm=1024)[0]
    bsT = gm_b_s[0].T
    gm_m = _gmlp_fwd(z, gm_ln_g, gm_ln_b, gm_w_s[0], bsT)
    x1 = _mm_nn("gm_out", gm_m, [W["gm_w_out", 0]], 512, 1024, D, [xs], [F32], _add_resid, tm=2048)[0]
    gather_rest(x1)
    x1 = lax.optimization_barrier((x1, last_cast[0]))[0]
    x2, ffn0 = ffn_fwd(0, x1)
    x3, ple0 = ple_fwd(0, x2)

    T = D // _LANES
    lanes = _S5_GT * _S5_P
    sv = {n: _to_view(n, w[n])[0] for n in _VIEW}
    a_re, a_im, log_dt = sv["s5_a_re"], sv["s5_a_im"], s5_log_dt
    lbr, lbi, Bbar_re, Bbar_im = _s5_prep(a_re, a_im, log_dt, sv["s5_b_re"], sv["s5_b_im"])

    def to_bd(B):
        return jnp.transpose(B.reshape(_S5_P, _S5_C, T, _S5_GT), (2, 3, 1, 0))

    def to_cd(cw):
        return jnp.transpose(cw.reshape(_S5_C, _S5_P, T, _S5_GT), (2, 3, 1, 0))

    def to_lam(v):
        return jnp.transpose(v).reshape(T, 1, lanes)

    bd_re, bd_im = to_bd(Bbar_re), to_bd(Bbar_im)
    cd_re, cd_im = to_cd(sv["s5_c_re"]), to_cd(sv["s5_c_im"])
    lam_re, lam_im = to_lam(lbr), to_lam(lbi)

    h1 = _rms_fwd("rms_mix1", x3, norm_mix[1:2])
    u = _mm_nn("s5_in", h1, [W["s5_w_in", 0]], 512, 1024, D, [], [F32], _ident, tm=2048)[0]
    s5_g, s5_re, s5_im = _s5_fwd(u, bd_re, bd_im, cd_re, cd_im, lam_re, lam_im, d_full)
    x4, glu_val, glu_sg = _mm_nn("s5_out", s5_g, [W["s5_w_out", 0], W["s5_w_out", 0]], 1024, 1024, D, [x3],
                                 [F32, _MXU, _MXU], _glu_epi, tm=1024, cb_offsets=[0, 2])
    x5, ffn1 = ffn_fwd(1, x4)
    x6, ple1 = ple_fwd(1, x5)

    dx, dpre1, dpp1, d_norm_final, loss_rows = _loss_head(x6, norm_final[None], tgt, ple1[3], ple1[4])

    small = {}

    seq = _Order()
    tie, done = seq.tie, seq.done
    d_norm_ple, d_norm_ffn, d_norm_mix = [None] * depth, [None] * depth, [None] * depth
    reduced = {}

    def keep(names, layer, pairs):
        for n, pr in zip(names, pairs):
            reduced[n, layer] = pr

    ple_names, up_names, down_names = ["ple_w_gate", "ple_w_proj"], ["ffn_w1", "ffn_w3"], ["ffn_w2"]

    def ple_bwd(i, dxo, saved, elem=None):
        xin, hp, pi, pp, gt = saved
        dpre, dpp = elem or done(_ple_bwd_elem(tie(dxo), pp, gt))
        dwg = done(_mm_tn(f"ple_gate_dw{i}", tie(hp), [dpre], "row", 512, 1024, 512, 1024))[0]
        dwp = done(_mm_tn(f"ple_proj_dw{i}", tie(pi), [dpp], "col", 128, 512, 128, 512))[0]
        red = _Reduction(f"ple{i}", [dwg, dwp])
        dhp = done(_mm_nt(f"ple_gate_dx{i}", [tie(dpre)], [W["ple_w_gate", i]], 512, 1024, [], [F32], _ident,
                          tm=2048))[0]
        dxin, dxin_mxu, dg = done(_rms_bwd(f"rms_ple_bwd{i}", tie(dhp), xin, norm_ple[i:i + 1], dxo))
        return dxin, dxin_mxu, dg, red

    def ffn_bwd(i, dxo, dxo_mxu, saved, before_up):
        xin, hf, a, b, f = saved
        dw2 = done(_mm_tn(f"ffn_down_dw{i}", tie(f), [dxo_mxu], "row", 1408, 1024, 1408, 1024))[0]
        r_down = _Reduction(f"ffd{i}", [dw2])
        da, db = done(_mm_nt(f"ffn_down_dx{i}", [tie(dxo_mxu)], [W["ffn_w2", i]], 1408, 1024, [a, b], [_MXU, _MXU],
                             _swiglu_bwd_epi, tm=1024))
        for step in before_up:
            step()
        r_down.scatter(seq)
        dw1, dw3 = done(_mm_tn(f"ffn_up_dw{i}", tie(hf), [da, db], "col", 1024, 1408, 1024, 1408))
        r_up = _Reduction(f"ffu{i}", [dw1, dw3])
        dhf = done(_mm_nt(f"ffn_up_dx{i}", [tie(da), db], [W["ffn_w1", i], W["ffn_w3", i]], 1024, 1408, [], [F32],
                          _ident, tm=1024))[0]
        dxin, dxin_mxu, dg = done(_rms_bwd(f"rms_ffn_bwd{i}", tie(dhf), xin, norm_ffn[i:i + 1], dxo))
        r_up.scatter(seq)
        return dxin, dxin_mxu, dg, r_down, r_up

    dx, dx_mxu, d_norm_ple[1], r_ple1 = ple_bwd(1, dx, ple1, (dpre1, dpp1))
    dx, _, d_norm_ffn[1], r_down1, r_up1 = ffn_bwd(1, dx, dx_mxu, ffn1, [lambda: r_ple1.scatter(seq)])

    do = done([_glu_bwd_elem(tie(dx), glu_val, glu_sg)])[0]
    dw_s5out = done(_mm_tn("s5_out_dw", tie(s5_g), [do], "col", 1024, 1024, 1024, 1024))[0]
    r_s5out = _Reduction("s5out", [dw_s5out])
    dgy = done(_mm_nt("s5_out_dx", [tie(do)], [W["s5_w_out", 0]], 1024, 1024, [], [F32], _ident, tm=1024))[0]
    keep(ple_names, 1, r_ple1.finish(seq))
    keep(down_names, 1, r_down1.finish(seq))
    du, dbd_re, dbd_im, dcd_re, dcd_im, dl_re, dl_im, dd = done(_s5_bwd(
        tie(u), dgy, s5_re, s5_im, bd_re, bd_im, cd_re, cd_im, lam_re, lam_im, d_full))
    r_s5out.scatter(seq)
    dw_s5in = done(_mm_tn("s5_in_dw", tie(h1), [du], "row", 512, 1024, 512, 1024))[0]
    r_s5in = _Reduction("s5in", [dw_s5in])
    dh1 = done(_mm_nt("s5_in_dx", [tie(du)], [W["s5_w_in", 0]], 512, 1024, [], [F32], _ident, tm=2048))[0]
    dx, _, d_norm_mix[1] = done(_rms_bwd("rms_mix1_bwd", tie(dh1), x3, norm_mix[1:2], dx))
    keep(up_names, 1, r_up1.finish(seq))
    r_s5in.scatter(seq)

    def from_bd(t):
        return jnp.transpose(t, (3, 2, 0, 1)).reshape(_S5_P, _S5_C, T * _S5_GT)

    def from_cdT(t):
        return jnp.transpose(t, (2, 3, 0, 1)).reshape(_S5_C, _S5_P, T * _S5_GT)

    def from_lam(t):
        return jnp.transpose(t.reshape(T * _S5_GT, _S5_P))

    da_re, da_im, dlog_dt, db_re, db_im = _s5_prep_bwd(
        a_re, a_im, log_dt, sv["s5_b_re"], sv["s5_b_im"], from_lam(dl_re), from_lam(dl_im),
        from_bd(dbd_re), from_bd(dbd_im))
    small["s5_a_re"], small["s5_a_im"], small["s5_log_dt"] = da_re[None], da_im[None], dlog_dt
    small["s5_b_re"], small["s5_b_im"] = db_re[None], db_im[None]
    small["s5_c_re"], small["s5_c_im"] = from_cdT(dcd_re)[None], from_cdT(dcd_im)[None]

    dx, dx_mxu, d_norm_ple[0], r_ple0 = ple_bwd(0, dx, ple0)
    keep(["s5_w_out"], 0, r_s5out.finish(seq))
    xin0, hf0, a0, b0, f0 = ffn0
    da0, db0 = done(_mm_nt("ffn_down_dx0", [tie(dx_mxu)], [W["ffn_w2", 0]], 1408, 1024, [a0, b0], [_MXU, _MXU],
                           _swiglu_bwd_epi, tm=1024))
    r_ple0.scatter(seq)
    dw1, dw3 = done(_mm_tn("ffn_up_dw0", tie(hf0), [da0, db0], "col", 1024, 1408, 1024, 1408))
    r_up0 = _Reduction("ffu0", [dw1, dw3])
    keep(["s5_w_in"], 0, r_s5in.finish(seq))
    dw2 = done(_mm_tn("ffn_down_dw0", tie(f0), [dx_mxu], "row", 1408, 1024, 1408, 1024))[0]
    r_down0 = _Reduction("ffd0", [dw2])
    r_up0.scatter(seq)
    dhf0 = done(_mm_nt("ffn_up_dx0", [tie(da0), db0], [W["ffn_w1", 0], W["ffn_w3", 0]], 1024, 1408, [], [F32], _ident,
                       tm=1024))[0]
    dx, dx_mxu, d_norm_ffn[0] = done(_rms_bwd("rms_ffn_bwd0", tie(dhf0), xin0, norm_ffn[0:1], dx))
    keep(ple_names, 0, r_ple0.finish(seq))
    r_down0.scatter(seq)

    dw_gmout = done(_mm_tn("gm_out_dw", tie(gm_m), [dx_mxu], "row", 512, 1024, 512, 1024))[0]
    r_gmout = _Reduction("gmout", [dw_gmout])
    dgm = done(_mm_nt("gm_out_dx", [tie(dx_mxu)], [W["gm_w_out", 0]], 512, 1024, [], [F32], _ident, tm=2048))[0]
    dz, dws, dbsT, dlng, dlnb = done(_gmlp_bwd(tie(z), dgm, gm_ln_g, gm_ln_b, gm_w_s[0], bsT))
    dw_gmin = done(_mm_tn("gm_in_dw", tie(h0), [dz], "col", 1024, 1024, 1024, 1024))[0]
    r_gmin = _Reduction("gmin", [dw_gmin])
    dh0 = done(_mm_nt("gm_in_dx", [tie(dz)], [W["gm_w_in", 0]], 1024, 1024, [], [F32], _ident, tm=1024))[0]
    dx, _, d_norm_mix[0] = done(_rms_bwd("rms_mix0_bwd", tie(dh0), xs, norm_mix[0:1], dx))
    grad_x = dx[None]

    small["norm_mix"], small["norm_ffn"] = jnp.concatenate(d_norm_mix), jnp.concatenate(d_norm_ffn)
    small["norm_ple"], small["norm_final"] = jnp.concatenate(d_norm_ple), d_norm_final[0]
    small["gm_ln_g"], small["gm_ln_b"], small["gm_w_s"] = dlng, dlnb, dws[None]
    small["gm_b_s"] = dbsT[:, :_GM_HEADS].T[None]
    small["s5_d"] = dd

    small_names = [n for n in _WEIGHT_NAMES if n not in _BIG]
    packed = _pack([small[n] for n in small_names] + [loss_rows[:, :1]], 64).reshape(8, -1, _LANES)
    grads, deltas, new_m, new_v = {}, {}, {}, {}
    my_c = lax.axis_index("c")

    def adamw(n, layer, prev):
        kind = _BIG[n]
        R, C = _rc(kind, w[n].shape)
        return done(_adamw_big(f"adamw_{n}{layer}", w[n], mom[n], var[n], layer, reduced[n, layer], kind, R, C,
                               seq.core, seq.tok, prev))

    def adamw_last(names):
        for n in names:
            grads[n], deltas[n], new_m[n], new_v[n] = adamw(n, 0, late.get(n))

    late = {}
    for n in down_names + ple_names:
        late[n] = adamw(n, 1, None)
    keep(up_names, 0, r_up0.finish(seq))
    r_gmout.scatter(seq)
    r_gmin.scatter(seq)
    landed = _exchange_slices("ar_small_in", packed)
    for n in up_names:
        late[n] = adamw(n, 1, None)
    adamw_last(["s5_w_in", "s5_w_out"] + ple_names)
    keep(down_names, 0, r_down0.finish(seq))
    mine = done([_sum_slices("ar_small_sum", packed, landed, seq.tok)])[0]
    spread = _broadcast_slices("ar_small_out", mine)
    adamw_last(up_names)
    keep(["gm_w_out"], 0, r_gmout.finish(seq))
    adamw_last(down_names)

    spread = lax.optimization_barrier((spread, seq.tok))[0]
    summed = lax.dynamic_update_slice(spread, mine[None], (4 * qx + 2 * qy + my_c, 0, 0))
    *red_list, loss_sum = _unpack(summed, [small[n].shape for n in small_names] + [(1, 1)])
    red_small, loss = dict(zip(small_names, red_list)), loss_sum.reshape(())
    red_small["s5_d"] = lax.dynamic_slice(red_small["s5_d"], (0, q * (D // 4)), (1, D // 4))

    def two_d(a):
        return a.reshape((1,) * (2 - a.ndim) + a.shape)

    def views(src):
        return [two_d(_to_view(n, src[n])) for n in small_names]

    g_views = [two_d(red_small[n]) for n in small_names]
    w_views = views(w)
    dl, mo, vo = _adamw_small([tie(w_views[0])] + w_views[1:], g_views, views(mom), views(var))
    done(dl)
    for n, g_, d_, m_, v_ in zip(small_names, g_views, dl, mo, vo):
        grads[n], deltas[n], new_m[n], new_v[n] = [_from_view(n, t_.reshape(_to_view(n, w[n]).shape)).reshape(w[n].shape)
                                                  for t_ in (g_, d_, m_, v_)]

    keep(["gm_w_in"], 0, r_gmin.finish(seq))
    adamw_last(["gm_w_out", "gm_w_in"])

    return (loss, grad_x, *[grads[n] for n in _WEIGHT_NAMES], *[deltas[n] for n in _WEIGHT_NAMES],
            *[new_m[n] for n in _WEIGHT_NAMES], *[new_v[n] for n in _WEIGHT_NAMES])
```

```python
import math

import jax
import jax.numpy as jnp
from jax import lax
from jax.experimental import pallas as pl
from jax.experimental.pallas import tpu as pltpu
from jax.experimental.pallas import tpu_sc as plsc

F32 = jnp.float32
_MXU = jnp.bfloat16
_WIRE = jnp.bfloat16
_EPS = 1e-6
_VMEM_LIMIT = 56 * 1024 * 1024
_LANES = 128
_MESH = pl.DeviceIdType.MESH

_LR, _B1, _B2, _AEPS, _WD, _STEP = 0.001, 0.9, 0.999, 1e-08, 0.01, 10

_GM_CHUNK = 128
_GM_HEADS = 16
_S5_GT = 8
_S5_P = 64
_S5_C = 16

_NN = (((1,), (0,)), ((), ()))
_NT = (((1,), (1,)), ((), ()))
_TN = (((0,), (0,)), ((), ()))


def _cparams(sem):
    return pltpu.CompilerParams(dimension_semantics=sem, vmem_limit_bytes=_VMEM_LIMIT)


def _sig(x):
    return 0.5 * jnp.tanh(0.5 * x) + 0.5


_GC = math.sqrt(2.0 / math.pi)


def _gelu(x):
    return 0.5 * x * (1.0 + jnp.tanh(_GC * (x + 0.044715 * (x * x * x))))


def _gelu_grad(x):
    t = jnp.tanh(_GC * (x + 0.044715 * (x * x * x)))
    return 0.5 * (1.0 + t) + 0.5 * x * (1.0 - t * t) * (_GC * (1.0 + 3.0 * 0.044715 * x * x))


def _dot(a, b, dn):
    return lax.dot_general(a.astype(_MXU), b.astype(_MXU), dn, preferred_element_type=F32)


class _W:
    def __init__(self, arr, kind):
        self.a, self.kind = arr, kind
        self.R, self.C = arr.shape[2], arr.shape[3]

    def full_shape(self):
        return (2 * self.R, 4 * self.C) if self.kind == "col" else (4 * self.R, 2 * self.C)


def _part_index(kind, R, C, tr, tc, rb, cb):
    nr, nc = R // tr, C // tc
    if kind == "col":
        return cb // nc, rb // nr, rb % nr, cb % nc
    return rb // nr, cb // nc, rb % nr, cb % nc


def _wspec(w, tr, tc, rb_fn, cb_fn):
    assert w.R % tr == 0 and w.C % tc == 0, (w.R, w.C, tr, tc)

    def imap(i, j, k):
        return _part_index(w.kind, w.R, w.C, tr, tc, rb_fn(i, j, k), cb_fn(i, j, k))

    return pl.BlockSpec((None, None, tr, tc), imap)


def _gspec(kind, R, C, tr, tc):
    assert R % tr == 0 and C % tc == 0, (R, C, tr, tc)

    def imap(i, j, k):
        part, half, rbi, cbi = _part_index(kind, R, C, tr, tc, i, j)
        return half, part, rbi, cbi

    return pl.BlockSpec((None, None, tr, tc), imap)


def _mm(name, grid, a_ops, b_ops, pairs, acc_shape, n_acc, extras, outs, epilogue):
    nk = grid[2]
    na, nb, ne, no = len(a_ops), len(b_ops), len(extras), len(outs)

    def body(*refs):
        a_refs = refs[:na]
        b_refs = refs[na:na + nb]
        e_refs = refs[na + nb:na + nb + ne]
        o_refs = refs[na + nb + ne:na + nb + ne + no]
        acc_refs = refs[na + nb + ne + no:]
        k = pl.program_id(2)

        def products():
            sums = [None] * n_acc
            for ai, bi, ci, dn in pairs:
                d = _dot(a_refs[ai][...], b_refs[bi][...], dn)
                sums[ci] = d if sums[ci] is None else sums[ci] + d
            return sums

        def finish(accs):
            res = epilogue(accs, [e[...] for e in e_refs])
            for o, r in zip(o_refs, res):
                o[...] = r.astype(o.dtype)

        if nk == 1:
            finish(products())
            return

        @pl.when(k == 0)
        def _():
            for acc, d in zip(acc_refs, products()):
                acc[...] = d

        @pl.when(jnp.logical_and(k > 0, k < nk - 1))
        def _():
            for acc, d in zip(acc_refs, products()):
                acc[...] += d

        @pl.when(k == nk - 1)
        def _():
            finish([acc[...] + d for acc, d in zip(acc_refs, products())])

    ops = list(a_ops) + list(b_ops) + list(extras)
    return pl.pallas_call(
        body, name=name, grid=grid,
        in_specs=[s for _, s in ops],
        out_specs=[s for _, s in outs],
        out_shape=[s for s, _ in outs],
        scratch_shapes=[pltpu.VMEM(acc_shape, F32) for _ in range(n_acc if nk > 1 else 0)],
        compiler_params=_cparams(("parallel", "parallel", "arbitrary")),
    )(*[a for a, _ in ops])


def _bs(shape, fn):
    return pl.BlockSpec(shape, fn)


def _tile_m(L):
    return min(L, 512)


def _mm_nn(name, x, ws, tk, tn, n_out, extras, outs_sd, epilogue, tm=None, cb_offsets=None, more=None):
    M, K = x.shape
    tm = min(M, tm or _tile_m(M))
    grid = (M // tm, n_out // tn, K // tk)
    a_ops = [(x, _bs((tm, tk), lambda i, j, k: (i, k)))]
    cb_offsets = cb_offsets or [0] * len(ws)
    b_ops = [(w.a, _wspec(w, tk, tn, lambda i, j, k: k, (lambda off: lambda i, j, k: j + off)(off)))
             for w, off in zip(ws, cb_offsets)]
    pairs = [(0, bi, bi, _NN) for bi in range(len(ws))]
    mn = _bs((tm, tn), lambda i, j, k: (i, j))
    ex = [(e, mn) for e in extras] + (more(tm) if more else [])
    outs = [(jax.ShapeDtypeStruct((M, n_out), dt), mn) for dt in outs_sd]
    return _mm(name, grid, a_ops, b_ops, pairs, (tm, tn), len(ws), ex, outs, epilogue)


def _mm_nt(name, xs, ws, tn, tk, extras, outs_sd, epilogue, tm=None):
    M, Nw = xs[0].shape
    Kw = ws[0].full_shape()[0]
    tm = min(M, tm or _tile_m(M))
    grid = (M // tm, Kw // tn, Nw // tk)
    a_ops = [(x, _bs((tm, tk), lambda i, j, k: (i, k))) for x in xs]
    b_ops = [(w.a, _wspec(w, tn, tk, lambda i, j, k: j, lambda i, j, k: k)) for w in ws]
    pairs = [(i, i, 0, _NT) for i in range(len(ws))]
    mn = _bs((tm, tn), lambda i, j, k: (i, j))
    ex = [(e, mn) for e in extras]
    outs = [(jax.ShapeDtypeStruct((M, Kw), dt), mn) for dt in outs_sd]
    return _mm(name, grid, a_ops, b_ops, pairs, (tm, tn), 1, ex, outs, epilogue)


def _mm_tn(name, x, dys, kind, R, C, tm, tn, tk=None):
    L, Kw = x.shape
    Nw = dys[0].shape[1]
    tk = tk or min(L, 1024)
    grid = (Kw // tm, Nw // tn, L // tk)
    a_ops = [(x, _bs((tk, tm), lambda i, j, k: (k, i)))]
    b_ops = [(dy, _bs((tk, tn), lambda i, j, k: (k, j))) for dy in dys]
    pairs = [(0, bi, bi, _TN) for bi in range(len(dys))]
    gs = _gspec(kind, R, C, tm, tn)
    outs = [(jax.ShapeDtypeStruct((2, 4, R, C), _WIRE), gs) for _ in dys]
    return _mm(name, grid, a_ops, b_ops, pairs, (tm, tn), len(dys), [], outs, lambda accs, ex: accs)


def _row_tile(L):
    return min(L, 256)


def _rowwise(name, body, ins, outs, L, acc_outs=()):
    tr = _row_tile(L)
    n_in, n_out = len(ins), len(outs)

    def kbody(*refs):
        i_refs, o_refs, a_refs = refs[:n_in], refs[n_in:n_in + n_out], refs[n_in + n_out:]
        res, sums = body(*[r[...] for r in i_refs])
        for o, r in zip(o_refs, res):
            o[...] = r.astype(o.dtype)
        if a_refs:
            @pl.when(pl.program_id(0) == 0)
            def _():
                for a in a_refs:
                    a[...] = jnp.zeros(a.shape, F32)
            for a, s in zip(a_refs, sums):
                a[...] += s

    in_specs = []
    for arr, kind in ins:
        if kind == "row":
            in_specs.append(pl.BlockSpec((tr, arr.shape[1]), lambda i: (i, 0)))
        else:
            in_specs.append(pl.BlockSpec(arr.shape, lambda i: (0, 0)))
    out_specs = [pl.BlockSpec((tr, c), lambda i: (i, 0)) for c, _ in outs]
    out_shape = [jax.ShapeDtypeStruct((L, c), dt) for c, dt in outs]
    out_specs += [pl.BlockSpec((1, c), lambda i: (0, 0)) for c in acc_outs]
    out_shape += [jax.ShapeDtypeStruct((1, c), F32) for c in acc_outs]
    return pl.pallas_call(
        kbody, name=name, grid=(L // tr,), in_specs=in_specs, out_specs=out_specs, out_shape=out_shape,
        compiler_params=_cparams(("arbitrary",)),
    )(*[a for a, _ in ins])


def _rms_fwd(name, x, g):
    def body(xv, gv):
        r = lax.rsqrt(jnp.mean(xv * xv, axis=-1, keepdims=True) + _EPS)
        return [xv * r * gv], []
    return _rowwise(name, body, [(x, "row"), (g, "vec")], [(x.shape[1], _MXU)], x.shape[0])[0]


def _rms_bwd(name, dh, x, g, dres):
    def body(dhv, xv, gv, dr):
        r = lax.rsqrt(jnp.mean(xv * xv, axis=-1, keepdims=True) + _EPS)
        xh = xv * r
        dxh = dhv * gv
        dx = dr + r * (dxh - xh * jnp.mean(dxh * xh, axis=-1, keepdims=True))
        return [dx, dx], [jnp.sum(dhv * xh, axis=0, keepdims=True)]
    D = x.shape[1]
    return _rowwise(name, body, [(dh, "row"), (x, "row"), (g, "vec"), (dres, "row")], [(D, F32), (D, _MXU)],
                    x.shape[0], [D])


def _loss_head(x, g, target, pp, gt):
    D = x.shape[1]

    def body(xv, gv, tv, ppv, gtv):
        r = lax.rsqrt(jnp.mean(xv * xv, axis=-1, keepdims=True) + _EPS)
        xh = xv * r
        e = xh * gv - tv
        dy = e * (1.0 / D)
        dxh = dy * gv
        dx = r * (dxh - xh * jnp.mean(dxh * xh, axis=-1, keepdims=True))
        row_loss = 0.5 * jnp.mean(e * e, axis=-1, keepdims=True)
        lsum = jnp.sum(row_loss, axis=0, keepdims=True) + jnp.zeros((1, _LANES), F32)
        gt32 = gtv.astype(F32)
        return [dx, dx * ppv * gt32 * (1.0 - gt32), dx * gt32], [jnp.sum(dy * xh, axis=0, keepdims=True), lsum]
    return _rowwise("loss_head", body, [(x, "row"), (g, "vec"), (target, "row"), (pp, "row"), (gt, "row")],
                    [(D, F32), (D, _MXU), (D, _MXU)], x.shape[0], [D, _LANES])


def _ple_bwd_elem(dx, pp, gt):
    def body(dxv, ppv, gtv):
        gt32 = gtv.astype(F32)
        return [dxv * ppv * gt32 * (1.0 - gt32), dxv * gt32], []
    D = dx.shape[1]
    return _rowwise("ple_bwd_elem", body, [(dx, "row"), (pp, "row"), (gt, "row")], [(D, _MXU), (D, _MXU)], dx.shape[0])


def _glu_bwd_elem(dx, val, sg):
    def body(dxv, vv, sv):
        v32, s32 = vv.astype(F32), sv.astype(F32)
        return [jnp.concatenate([dxv * s32, dxv * v32 * s32 * (1.0 - s32)], axis=1)], []
    D = dx.shape[1]
    return _rowwise("glu_bwd_elem", body, [(dx, "row"), (val, "row"), (sg, "row")], [(2 * D, _MXU)], dx.shape[0])[0]


def _gm_common(z, ln_g, ln_b, wc_bf, bsT):
    W = z.shape[1] // 2
    zu, zv = z[:, :W], z[:, W:]
    u, v = _gelu(zu), _gelu(zv)
    mu = jnp.mean(v, axis=-1, keepdims=True)
    vc = v - mu
    rstd = lax.rsqrt(jnp.mean(vc * vc, axis=-1, keepdims=True) + _EPS)
    vh = vc * rstd
    vn = vh * ln_g + ln_b
    vnb = vn.astype(_MXU)
    svs = []
    for h in range(_GM_HEADS):
        sl = slice(h * _LANES, (h + 1) * _LANES)
        svs.append(_dot(wc_bf[h], vnb[:, sl], _NN) + bsT[:, h:h + 1])
    return zu, zv, u, vh, rstd, vnb, svs


def _causal(w):
    t = lax.broadcasted_iota(jnp.int32, w.shape, w.ndim - 2)
    s = lax.broadcasted_iota(jnp.int32, w.shape, w.ndim - 1)
    return jnp.where(s <= t, w, jnp.zeros_like(w))


def _gmlp_fwd(z, ln_g, ln_b, w_s, bsT):
    L, W2 = z.shape
    W = W2 // 2

    def body(z_ref, g_ref, b_ref, ws_ref, bs_ref, m_ref):
        wc = _causal(ws_ref[...]).astype(_MXU)
        _, _, u, _, _, _, svs = _gm_common(z_ref[...], g_ref[...], b_ref[...], wc, bs_ref[...])
        for h in range(_GM_HEADS):
            sl = slice(h * _LANES, (h + 1) * _LANES)
            m_ref[:, sl] = (u[:, sl] * svs[h]).astype(m_ref.dtype)

    return pl.pallas_call(
        body, name="gmlp_fwd", grid=(L // _GM_CHUNK,),
        in_specs=[pl.BlockSpec((_GM_CHUNK, W2), lambda n: (n, 0)),
                  pl.BlockSpec((1, W), lambda n: (0, 0)), pl.BlockSpec((1, W), lambda n: (0, 0)),
                  pl.BlockSpec(w_s.shape, lambda n: (0, 0, 0)), pl.BlockSpec(bsT.shape, lambda n: (0, 0))],
        out_specs=pl.BlockSpec((_GM_CHUNK, W), lambda n: (n, 0)),
        out_shape=jax.ShapeDtypeStruct((L, W), _MXU),
        compiler_params=_cparams(("arbitrary",)),
    )(z, ln_g, ln_b, w_s, bsT)


def _gmlp_bwd(z, dm, ln_g, ln_b, w_s, bsT):
    L, W2 = z.shape
    W = W2 // 2
    T = _GM_CHUNK

    def body(z_ref, dm_ref, g_ref, b_ref, ws_ref, bs_ref, dz_ref, dws_ref, dbs_ref, dg_ref, db_ref):
        @pl.when(pl.program_id(0) == 0)
        def _():
            dws_ref[...] = jnp.zeros(dws_ref.shape, F32)
            dbs_ref[...] = jnp.zeros(dbs_ref.shape, F32)
            dg_ref[...] = jnp.zeros(dg_ref.shape, F32)
            db_ref[...] = jnp.zeros(db_ref.shape, F32)

        wc = _causal(ws_ref[...]).astype(_MXU)
        ln_g_v = g_ref[...]
        zu, zv, u, vh, rstd, vnb, svs = _gm_common(z_ref[...], ln_g_v, b_ref[...], wc, bs_ref[...])
        dmv = dm_ref[...]
        lane = lax.broadcasted_iota(jnp.int32, (T, _LANES), 1)
        dbs = jnp.zeros((T, _LANES), F32)
        dvn_parts = []
        for h in range(_GM_HEADS):
            sl = slice(h * _LANES, (h + 1) * _LANES)
            dsv = dmv[:, sl] * u[:, sl]
            dz_ref[:, sl] = (dmv[:, sl] * svs[h] * _gelu_grad(zu[:, sl])).astype(dz_ref.dtype)
            dbs = dbs + jnp.where(lane == h, jnp.sum(dsv, axis=1, keepdims=True), 0.0)
            dsvb = dsv.astype(_MXU)
            dws_ref[h] += _dot(dsvb, vnb[:, sl], _NT)
            dvn_parts.append(_dot(wc[h], dsvb, _TN))
        dbs_ref[...] += dbs
        dvn = jnp.concatenate(dvn_parts, axis=1)
        dg_ref[...] += jnp.sum(dvn * vh, axis=0, keepdims=True)
        db_ref[...] += jnp.sum(dvn, axis=0, keepdims=True)
        dxh = dvn * ln_g_v
        dv = rstd * (dxh - jnp.mean(dxh, axis=-1, keepdims=True) - vh * jnp.mean(dxh * vh, axis=-1, keepdims=True))
        dz_ref[:, W:] = (dv * _gelu_grad(zv)).astype(dz_ref.dtype)

        @pl.when(pl.program_id(0) == pl.num_programs(0) - 1)
        def _():
            dws_ref[...] = _causal(dws_ref[...])

    return pl.pallas_call(
        body, name="gmlp_bwd", grid=(L // T,),
        in_specs=[pl.BlockSpec((T, W2), lambda n: (n, 0)), pl.BlockSpec((T, W), lambda n: (n, 0)),
                  pl.BlockSpec((1, W), lambda n: (0, 0)), pl.BlockSpec((1, W), lambda n: (0, 0)),
                  pl.BlockSpec(w_s.shape, lambda n: (0, 0, 0)), pl.BlockSpec(bsT.shape, lambda n: (0, 0))],
        out_specs=[pl.BlockSpec((T, W2), lambda n: (n, 0)),
                   pl.BlockSpec(w_s.shape, lambda n: (0, 0, 0)), pl.BlockSpec((T, _LANES), lambda n: (0, 0)),
                   pl.BlockSpec((1, W), lambda n: (0, 0)), pl.BlockSpec((1, W), lambda n: (0, 0))],
        out_shape=[jax.ShapeDtypeStruct((L, W2), _MXU), jax.ShapeDtypeStruct(w_s.shape, F32),
                   jax.ShapeDtypeStruct((T, _LANES), F32),
                   jax.ShapeDtypeStruct((1, W), F32), jax.ShapeDtypeStruct((1, W), F32)],
        compiler_params=_cparams(("arbitrary",)),
    )(z, dm, ln_g, ln_b, w_s, bsT)


def _s5_prep_math(a_re, a_im, log_dt):
    dt = jnp.exp(log_dt)
    xr, xi = a_re * dt, a_im * dt
    e = jnp.exp(xr)
    lbr, lbi = e * jnp.cos(xi), e * jnp.sin(xi)
    dn = a_re * a_re + a_im * a_im
    nr, ni = lbr - 1.0, lbi
    pr, pi = nr * a_re + ni * a_im, ni * a_re - nr * a_im
    return dt, lbr, lbi, dn, nr, ni, pr, pi


def _vm():
    return pl.BlockSpec(memory_space=pltpu.VMEM)


def _s5_prep(a_re, a_im, log_dt, b_re, b_im):
    def body(ar_ref, ai_ref, ld_ref, br_ref, bi_ref, lbr_ref, lbi_ref, Br_ref, Bi_ref):
        _, lbr, lbi, dn, _, _, pr, pi = _s5_prep_math(ar_ref[...], ai_ref[...], ld_ref[...])
        cr, ci = (pr / dn)[:, None, :], (pi / dn)[:, None, :]
        lbr_ref[...] = lbr
        lbi_ref[...] = lbi
        br, bi = br_ref[...], bi_ref[...]
        Br_ref[...] = cr * br - ci * bi
        Bi_ref[...] = cr * bi + ci * br

    sd = jax.ShapeDtypeStruct
    return pl.pallas_call(
        body, name="s5_prep", in_specs=[_vm()] * 5, out_specs=[_vm()] * 4,
        out_shape=[sd(a_re.shape, F32), sd(a_re.shape, F32), sd(b_re.shape, F32), sd(b_re.shape, F32)],
    )(a_re, a_im, log_dt, b_re, b_im)


def _s5_prep_bwd(a_re, a_im, log_dt, b_re, b_im, dlbr_s, dlbi_s, dBr, dBi):
    def body(ar_ref, ai_ref, ld_ref, br_ref, bi_ref, dlr_ref, dli_ref, dBr_ref, dBi_ref,
             dar_ref, dai_ref, dld_ref, dbr_ref, dbi_ref):
        a_re_v, a_im_v = ar_ref[...], ai_ref[...]
        dt, lbr, lbi, dn, nr, ni, pr, pi = _s5_prep_math(a_re_v, a_im_v, ld_ref[...])
        cr, ci = (pr / dn)[:, None, :], (pi / dn)[:, None, :]
        br, bi, dBr_v, dBi_v = br_ref[...], bi_ref[...], dBr_ref[...], dBi_ref[...]
        dbr_ref[...] = cr * dBr_v + ci * dBi_v
        dbi_ref[...] = cr * dBi_v - ci * dBr_v
        dcr = jnp.sum(br * dBr_v + bi * dBi_v, axis=1)
        dci = jnp.sum(br * dBi_v - bi * dBr_v, axis=1)
        dpr, dpi = dcr / dn, dci / dn
        ddn = -(dcr * pr + dci * pi) / (dn * dn)
        dnr = dpr * a_re_v - dpi * a_im_v
        dni = dpr * a_im_v + dpi * a_re_v
        dlbr = dlr_ref[...] + dnr
        dlbi = dli_ref[...] + dni
        dxr = dlbr * lbr + dlbi * lbi
        dxi = dlbi * lbr - dlbr * lbi
        dar_ref[...] = dpr * nr + dpi * ni + 2.0 * ddn * a_re_v + dxr * dt
        dai_ref[...] = dpr * ni - dpi * nr + 2.0 * ddn * a_im_v + dxi * dt
        dld_ref[...] = jnp.sum(dxr * a_re_v + dxi * a_im_v, axis=0, keepdims=True) * dt

    sd = jax.ShapeDtypeStruct
    return pl.pallas_call(
        body, name="s5_prep_bwd", in_specs=[_vm()] * 9, out_specs=[_vm()] * 5,
        out_shape=[sd(a_re.shape, F32), sd(a_re.shape, F32), sd(log_dt.shape, F32),
                   sd(b_re.shape, F32), sd(b_re.shape, F32)],
    )(a_re, a_im, log_dt, b_re, b_im, dlbr_s, dlbi_s, dBr, dBi)


def _shift_rows(v, down):
    n = v.shape[0]
    rolled = pltpu.roll(v, 1 if down else n - 1, 0)
    row = lax.broadcasted_iota(jnp.int32, v.shape, 0)
    return jnp.where(row == (0 if down else n - 1), 0.0, rolled)


def _cmul(ar, ai, br, bi):
    return ar * br - ai * bi, ar * bi + ai * br


_SEG = 8
_UNROLL = 8


def _seg_rows(k):
    if isinstance(k, int):
        return pl.ds(k * _SEG, _SEG)
    return pl.ds(pl.multiple_of(k * _SEG, _SEG), _SEG)


def _unrolled(n, step, init):
    main = n // _UNROLL

    def trip(kk, s):
        for uu in range(_UNROLL):
            s = step(kk * _UNROLL + uu, s)
        return s

    s = lax.fori_loop(0, main, trip, init)
    for r in range(main * _UNROLL, n):
        s = step(r, s)
    return s


def _interleave(src_ref, dst_ref, nk):
    def step(k, carry):
        dst_ref[_seg_rows(k), :] = src_ref[pl.ds(k, _SEG, stride=nk), :]
        return carry
    _unrolled(nk, step, 0)


def _deinterleave(src_ref, dst_ref, nk):
    def step(k, carry):
        dst_ref[pl.ds(k, _SEG, stride=nk), :] = src_ref[_seg_rows(k), :]
        return carry
    _unrolled(nk, step, 0)


def _segment_inits(er, ei, ar, ai, nk, down):
    pr, pi = ar, ai
    for _ in range(int(math.log2(nk))):
        pr, pi = _cmul(pr, pi, pr, pi)
    fr, fi = er, ei
    for _ in range(_SEG - 1):
        sr, si = _shift_rows(fr, down), _shift_rows(fi, down)
        mr, mi = _cmul(pr, pi, sr, si)
        fr, fi = er + mr, ei + mi
    return _shift_rows(fr, down), _shift_rows(fi, down)


def _scan_states(x_re, x_im, ar, ai, nk):
    lanes = ar.shape[1]

    def step(k, s):
        rows = _seg_rows(k)
        mr, mi = _cmul(ar, ai, s[0], s[1])
        return mr + x_re[rows, :], mi + x_im[rows, :]

    zero = jnp.zeros((_SEG, lanes), F32)
    er, ei = _unrolled(nk, step, (zero, zero))
    ir, ii = _segment_inits(er, ei, ar, ai, nk, True)

    def step2(k, s):
        rows = _seg_rows(k)
        mr, mi = _cmul(ar, ai, s[0], s[1])
        nr, ni = mr + x_re[rows, :], mi + x_im[rows, :]
        x_re[rows, :] = nr
        x_im[rows, :] = ni
        return nr, ni

    _unrolled(nk, step2, (ir, ii))


def _s5_tile_fwd(u, bd_re, bd_im, cd_re, cd_im, ar, ai, d, s_re, s_im, nk):
    s_re[...] = _dot(u, bd_re, _NN)
    s_im[...] = _dot(u, bd_im, _NN)
    _scan_states(s_re, s_im, ar, ai, nk)
    return _dot(s_re[...], cd_re, _NN) - _dot(s_im[...], cd_im, _NN) + d * u


def _s5_specs(L, T):
    lanes = _S5_GT * _S5_P
    u_spec = pl.BlockSpec((L, _LANES), lambda t: (0, t))
    bd_spec = pl.BlockSpec((None, _S5_GT, _S5_C, _S5_P), lambda t: (t, 0, 0, 0))
    cd_spec = pl.BlockSpec((None, _S5_GT, _S5_P, _S5_C), lambda t: (t, 0, 0, 0))
    lam_spec = pl.BlockSpec((None, 1, lanes), lambda t: (t, 0, 0))
    d_spec = pl.BlockSpec((1, _LANES), lambda t: (0, t))
    return lanes, u_spec, bd_spec, cd_spec, lam_spec, d_spec


def _fill_block_diag(dst_ref, blocks_ref):
    _, a, b = blocks_ref.shape
    dst_ref[...] = jnp.zeros(dst_ref.shape, F32)
    for g in range(_S5_GT):
        dst_ref[g * a:(g + 1) * a, g * b:(g + 1) * b] = blocks_ref[g]


def _take_block_diag(dst_ref, v):
    _, a, b = dst_ref.shape
    for g in range(_S5_GT):
        dst_ref[g] = v[g * a:(g + 1) * a, g * b:(g + 1) * b]


def _s5_dense(bdr, bdi, cdr, cdi, dense):
    for src, dst in zip((bdr, bdi, cdr, cdi), dense):
        _fill_block_diag(dst, src)
    return [dst[...] for dst in dense]


def _s5_dense_scratch(lanes):
    return [pltpu.VMEM((_LANES, lanes), F32), pltpu.VMEM((_LANES, lanes), F32),
            pltpu.VMEM((lanes, _LANES), F32), pltpu.VMEM((lanes, _LANES), F32)]


def _s5_fwd(u, bd_re, bd_im, cd_re, cd_im, lam_re, lam_im, d):
    L, Wd = u.shape
    T = Wd // _LANES
    nk = L // _SEG
    lanes, u_spec, bd_spec, cd_spec, lam_spec, d_spec = _s5_specs(L, T)
    s_spec = pl.BlockSpec((L, lanes), lambda t: (0, t))

    def body(u_ref, bdr, bdi, cdr, cdi, lr, li, d_ref, g_ref, s_re, s_im, up, tmp, *dense):
        ar = jnp.broadcast_to(lr[...], (_SEG, lanes))
        ai = jnp.broadcast_to(li[...], (_SEG, lanes))
        bd_re_v, bd_im_v, cd_re_v, cd_im_v = _s5_dense(bdr, bdi, cdr, cdi, dense)
        _interleave(u_ref, up, nk)
        y = _s5_tile_fwd(up[...], bd_re_v, bd_im_v, cd_re_v, cd_im_v, ar, ai, d_ref[...], s_re, s_im, nk)
        up[...] = _gelu(y)
        _deinterleave(up, tmp, nk)
        g_ref[...] = tmp[...].astype(g_ref.dtype)

    return pl.pallas_call(
        body, name="s5_fwd", grid=(T,),
        in_specs=[u_spec, bd_spec, bd_spec, cd_spec, cd_spec, lam_spec, lam_spec, d_spec],
        out_specs=[u_spec, s_spec, s_spec],
        out_shape=[jax.ShapeDtypeStruct((L, Wd), _MXU), jax.ShapeDtypeStruct((L, T * lanes), F32),
                   jax.ShapeDtypeStruct((L, T * lanes), F32)],
        scratch_shapes=[pltpu.VMEM((L, _LANES), F32) for _ in range(2)] + _s5_dense_scratch(lanes),
        compiler_params=_cparams(("arbitrary",)),
    )(u, bd_re, bd_im, cd_re, cd_im, lam_re, lam_im, d)


def _s5_bwd(u, dg, states_re, states_im, bd_re, bd_im, cd_re, cd_im, lam_re, lam_im, d):
    L, Wd = u.shape
    T = Wd // _LANES
    nk = L // _SEG
    lanes, u_spec, bd_spec, cd_spec, lam_spec, d_spec = _s5_specs(L, T)
    s_spec = pl.BlockSpec((L, lanes), lambda t: (0, t))

    def body(u_ref, dg_ref, s_re, s_im, bdr, bdi, cdr, cdi, lr, li, d_ref,
             du_ref, dbdr, dbdi, dcdr, dcdi, dlr, dli, dd_ref, g_re, g_im, up, dgp, tmp, *dense):
        ar = jnp.broadcast_to(lr[...], (_SEG, lanes))
        ai = jnp.broadcast_to(li[...], (_SEG, lanes))
        bd_re_v, bd_im_v, cd_re_v, cd_im_v = _s5_dense(bdr, bdi, cdr, cdi, dense)
        _interleave(u_ref, up, nk)
        _interleave(dg_ref, dgp, nk)
        uv, dv = up[...], d_ref[...]
        y = _dot(s_re[...], cd_re_v, _NN) - _dot(s_im[...], cd_im_v, _NN) + dv * uv
        dy = dgp[...] * _gelu_grad(y)
        dd_ref[...] = jnp.sum(dy * uv, axis=0, keepdims=True)
        dyb = dy.astype(_MXU)
        _take_block_diag(dcdr, _dot(dyb, s_re[...], _TN))
        _take_block_diag(dcdi, -_dot(dyb, s_im[...], _TN))
        g_re[...] = _dot(dyb, cd_re_v, _NT)
        g_im[...] = -_dot(dyb, cd_im_v, _NT)

        nai = -ai

        def step(j, s):
            rows = _seg_rows(nk - 1 - j)
            mr, mi = _cmul(ar, nai, s[0], s[1])
            return mr + g_re[rows, :], mi + g_im[rows, :]

        zero = jnp.zeros((_SEG, lanes), F32)
        er, ei = _unrolled(nk, step, (zero, zero))
        ir, ii = _segment_inits(er, ei, ar, nai, nk, False)

        def acc_lam(gr, gi, pr, pi, acc):
            return acc[0] + gr * pr + gi * pi, acc[1] + gi * pr - gr * pi

        def step2(j, carry):
            s, acc = carry
            k = nk - 1 - j
            rows = _seg_rows(k)
            mr, mi = _cmul(ar, nai, s[0], s[1])
            nr, ni = mr + g_re[rows, :], mi + g_im[rows, :]
            g_re[rows, :] = nr
            g_im[rows, :] = ni
            prev = _seg_rows(k - 1)
            return (nr, ni), acc_lam(nr, ni, s_re[prev, :], s_im[prev, :], acc)

        (g0r, g0i), acc = _unrolled(nk - 1, step2, ((ir, ii), (zero, zero)))
        first = _seg_rows(0)
        mr, mi = _cmul(ar, nai, g0r, g0i)
        nr, ni = mr + g_re[first, :], mi + g_im[first, :]
        g_re[first, :] = nr
        g_im[first, :] = ni
        last = _seg_rows(nk - 1)
        acc = acc_lam(nr, ni, _shift_rows(s_re[last, :], True), _shift_rows(s_im[last, :], True), acc)
        dlr[...] = jnp.sum(acc[0], axis=0, keepdims=True)
        dli[...] = jnp.sum(acc[1], axis=0, keepdims=True)

        gtr, gti = g_re[...].astype(_MXU), g_im[...].astype(_MXU)
        ub = uv.astype(_MXU)
        _take_block_diag(dbdr, _dot(ub, gtr, _TN))
        _take_block_diag(dbdi, _dot(ub, gti, _TN))
        dgp[...] = _dot(gtr, bd_re_v, _NT) + _dot(gti, bd_im_v, _NT) + dy * dv
        _deinterleave(dgp, tmp, nk)
        du_ref[...] = tmp[...].astype(du_ref.dtype)

    sd = jax.ShapeDtypeStruct
    big = sd((T, _S5_GT, _S5_C, _S5_P), F32)
    return pl.pallas_call(
        body, name="s5_bwd", grid=(T,),
        in_specs=[u_spec, u_spec, s_spec, s_spec, bd_spec, bd_spec, cd_spec, cd_spec, lam_spec, lam_spec, d_spec],
        out_specs=[u_spec, bd_spec, bd_spec, bd_spec, bd_spec, lam_spec, lam_spec, d_spec],
        out_shape=[sd((L, Wd), _MXU), big, big, big, big, sd((T, 1, lanes), F32), sd((T, 1, lanes), F32),
                   sd((1, Wd), F32)],
        scratch_shapes=[pltpu.VMEM((L, lanes), F32) for _ in range(2)]
        + [pltpu.VMEM((L, _LANES), F32) for _ in range(3)] + _s5_dense_scratch(lanes),
        compiler_params=_cparams(("arbitrary",)),
    )(u, dg, states_re, states_im, bd_re, bd_im, cd_re, cd_im, lam_re, lam_im, d)


def _half_tile(R, few_arrays=False):
    for t in ((512, 704, 128) if few_arrays else (256, 352, 128)):
        if R % t == 0:
            return t
    raise ValueError(R)


def _cast_shards(name, ws, layer, kind, R, C):
    n = len(ws)
    tr = _half_tile(R, True)
    nr = R // tr

    def body(*refs):
        for w_ref, o_ref in zip(refs[:n], refs[n:]):
            o_ref[...] = w_ref[...].astype(o_ref.dtype)

    if kind == "col":
        in_map = lambda h, i: (layer, h * nr + i, 0)
    else:
        in_map = lambda h, i: (layer, i, h)
    return pl.pallas_call(
        body, name=name, grid=(2, nr), in_specs=[pl.BlockSpec((None, tr, C), in_map)] * n,
        out_specs=[pl.BlockSpec((None, tr, C), lambda h, i: (h, i, 0))] * n,
        out_shape=[jax.ShapeDtypeStruct((2, R, C), _WIRE)] * n,
        compiler_params=_cparams(("arbitrary", "arbitrary")),
    )(*ws)


def _adam_math(w, g, m, v):
    m2 = _B1 * m + (1.0 - _B1) * g
    v2 = _B2 * v + (1.0 - _B2) * (g * g)
    m_hat = m2 / (1.0 - _B1 ** _STEP)
    v_hat = v2 / (1.0 - _B2 ** _STEP)
    delta = -_LR * (m_hat / (jnp.sqrt(v_hat) + _AEPS) + _WD * w)
    return delta, m2, v2


def _adamw_big(name, w, m, v, layer, pair, kind, R, C, c, after, prev):
    tr = _half_tile(R, few_arrays=C <= 1024 and R % 512 == 0)
    nr = R // tr

    def body(c_ref, w_ref, m_ref, v_ref, own_ref, other_ref, *rest):
        go_ref, d_ref, mo_ref, vo_ref = rest[-4:]
        g = jnp.where(pl.program_id(0) == c_ref[0], own_ref[...], other_ref[...])
        delta, m2, v2 = _adam_math(w_ref[...], g, m_ref[...], v_ref[...])
        go_ref[...] = g
        d_ref[...] = delta
        mo_ref[...] = m2
        vo_ref[...] = v2

    if kind == "col":
        nat = pl.BlockSpec((None, tr, C), lambda h, i, c_ref: (layer, h * nr + i, 0))
    else:
        nat = pl.BlockSpec((None, tr, C), lambda h, i, c_ref: (layer, i, h))

    def gspec(own):
        return pl.BlockSpec((tr, C), lambda h, i, c_ref: (jnp.where((h == c_ref[0]) == own, i, 0), 0))

    carried = list(prev) if prev is not None else []
    gs = pltpu.PrefetchScalarGridSpec(
        num_scalar_prefetch=1, grid=(2, nr),
        in_specs=[nat, nat, nat, gspec(True), gspec(False), _any()] + [_any()] * len(carried),
        out_specs=[nat] * 4)
    sd = jax.ShapeDtypeStruct(w.shape, F32)
    return pl.pallas_call(
        body, name=name, grid_spec=gs, out_shape=[sd] * 4,
        input_output_aliases={7 + k: k for k in range(len(carried))},
        compiler_params=_cparams(("arbitrary", "arbitrary")),
    )(c, w, m, v, pair[0], pair[1], after, *carried)


def _adamw_small(ws, gs, ms, vs):
    n = len(ws)

    def body(*refs):
        for i in range(n):
            w_ref, g_ref, m_ref, v_ref, d_ref, mo_ref, vo_ref = refs[i::n]
            delta, m2, v2 = _adam_math(w_ref[...], g_ref[...], m_ref[...], v_ref[...])
            d_ref[...] = delta
            mo_ref[...] = m2
            vo_ref[...] = v2

    sds = [jax.ShapeDtypeStruct(w.shape, F32) for w in ws]
    outs = pl.pallas_call(
        body, name="adamw_small", in_specs=[_vm()] * (4 * n), out_specs=[_vm()] * (3 * n), out_shape=sds * 3,
        compiler_params=pltpu.CompilerParams(vmem_limit_bytes=_VMEM_LIMIT),
    )(*ws, *gs, *ms, *vs)
    return outs[:n], outs[n:2 * n], outs[2 * n:]


def _place():
    x, y, c = lax.axis_index("x"), lax.axis_index("y"), lax.axis_index("c")
    chips = [(1 - x, y), (x, 1 - y), (1 - x, 1 - y)]
    return x, y, c, 2 * x + y, chips


def _any():
    return pl.BlockSpec(memory_space=pl.ANY)


def _remote(src, dst, ssem, rsem, dev):
    return pltpu.make_async_remote_copy(src_ref=src, dst_ref=dst, send_sem=ssem, recv_sem=rsem,
                                        device_id=dev, device_id_type=_MESH)


def _allgather(name, shards):
    n = len(shards)

    def body(*refs):
        s_refs, g_refs = refs[:n], refs[n:2 * n]
        send0, recv0, send1, recv1, send2, recv2 = refs[2 * n:]
        x, y, c, q, _ = _place()
        sib, xn, yn = (x, y, 1 - c), (1 - x, y, c), (x, 1 - y, c)
        qx, qy, qd = 2 * (1 - x) + y, 2 * x + (1 - y), 2 * (1 - x) + (1 - y)
        _handshake([sib, xn, yn])
        own = [_remote(s_refs[a], g_refs[a].at[q], send0.at[a], recv0.at[a], sib) for a in range(n)]

        def pieces(a):
            g, half = g_refs[a], s_refs[a].shape[1] // 2
            return [g.at[qx, c], g.at[qy, c], g.at[qd, c, pl.ds(0, half)], g.at[qd, c, pl.ds(half, half)]]

        def relayed(a):
            g, half = g_refs[a], s_refs[a].shape[1] // 2
            return [(g.at[qx, c, pl.ds(0, half)], yn), (g.at[qy, c, pl.ds(half, half)], xn)]

        first = []
        for a in range(n):
            first.append(_remote(s_refs[a].at[c], g_refs[a].at[q, c], send1.at[4 * a], recv1.at[4 * a], xn))
            first.append(_remote(s_refs[a].at[c], g_refs[a].at[q, c], send1.at[4 * a + 1], recv1.at[4 * a + 1], yn))
        for cp in first + own:
            cp.start()
        later = []
        for a in range(n):
            land = pieces(a)
            for j in range(4):
                k = 4 * a + j
                _remote(land[j], land[j], send1.at[k], recv1.at[k], xn).wait_recv()
                if j < 2:
                    src, to = relayed(a)[j]
                    cp = _remote(src, src, send1.at[k + 2], recv1.at[k + 2], to)
                    cp.start()
                    later.append(cp)
                cp = _remote(land[j], land[j], send2.at[k], recv2.at[k], sib)
                cp.start()
                later.append(cp)
        for a in range(n):
            g, half = g_refs[a], s_refs[a].shape[1] // 2
            theirs = [g.at[qx, 1 - c], g.at[qy, 1 - c], g.at[qd, 1 - c, pl.ds(0, half)],
                      g.at[qd, 1 - c, pl.ds(half, half)]]
            for j in range(4):
                _remote(theirs[j], theirs[j], send2.at[4 * a + j], recv2.at[4 * a + j], sib).wait_recv()
        for cp in own:
            cp.wait()
        for cp in first + later:
            cp.wait_send()

    return _sequencer(name, _ID_GATHER, body, shards,
                      [jax.ShapeDtypeStruct((4,) + s.shape, s.dtype) for s in shards], [n, n] + [4 * n] * 4)


def _handshake(peers):
    barrier = pltpu.get_barrier_semaphore()
    for peer in peers:
        pl.semaphore_signal(barrier, inc=1, device_id=peer, device_id_type=_MESH)
    pl.semaphore_wait(barrier, len(peers))


_ID_SIBLING, _ID_CHIPS, _ID_GATHER, _ID_ALL = 1, 2, 3, 4


def _sequencer(name, collective_id, body, ins, out_types, sem_counts):
    mesh = plsc.ScalarSubcoreMesh(axis_name="seq", num_cores=1)
    moved = sum(math.prod(o.shape) * jnp.dtype(o.dtype).itemsize for o in out_types)
    return pl.kernel(
        body, name=name, out_type=out_types, mesh=mesh,
        scratch_types=[pltpu.SemaphoreType.DMA((k,)) for k in sem_counts],
        compiler_params=pltpu.CompilerParams(collective_id=collective_id),
        cost_estimate=pl.CostEstimate(flops=0, transcendentals=0, bytes_accessed=2 * moved,
                                      remote_bytes_transferred=moved),
    )(*ins)


def _swap_halves(name, grads):
    n = len(grads)

    def body(*refs):
        g_refs, t_refs = refs[:n], refs[n:2 * n]
        send, recv = refs[2 * n:]
        x, y, c, _, _ = _place()
        _handshake([(x, y, 1 - c)])
        cps = [_remote(g_refs[a].at[1 - c], t_refs[a], send.at[a], recv.at[a], (x, y, 1 - c)) for a in range(n)]
        for cp in cps:
            cp.start()
        for cp in cps:
            cp.wait()

    return _sequencer(name, _ID_SIBLING, body, grads,
                      [jax.ShapeDtypeStruct(g.shape[1:], g.dtype) for g in grads], [n, n])


def _chip_sum(name, grads, swapped, after, core):
    n = len(grads)
    _, _, R, C = grads[0].shape
    tr = _half_tile(R, True)

    def body(c_ref, *refs):
        for g_ref, t_ref, o_ref in zip(refs[:n], refs[n:2 * n], refs[2 * n + 1:]):
            o_ref[...] = (g_ref[...].astype(F32) + t_ref[...].astype(F32)).astype(o_ref.dtype)

    own = pl.BlockSpec((None, None, tr, C), lambda r, i, c_ref: (c_ref[0], r, i, 0))
    part = pl.BlockSpec((None, tr, C), lambda r, i, c_ref: (r, i, 0))
    gs = pltpu.PrefetchScalarGridSpec(
        num_scalar_prefetch=1, grid=(4, R // tr), in_specs=[own] * n + [part] * n + [_any()], out_specs=[part] * n)
    return pl.pallas_call(
        body, name=name, grid_spec=gs, out_shape=[jax.ShapeDtypeStruct((4, R, C), _WIRE)] * n,
        compiler_params=_cparams(("arbitrary", "arbitrary")),
    )(core, *grads, *swapped, after)


def _scatter_parts(name, parts):
    n = len(parts)

    def body(*refs):
        p_refs, t_refs = refs[:n], refs[n:2 * n]
        send, recv = refs[2 * n:]
        x, y, c, q, chips = _place()
        _handshake([(rx, ry, c) for rx, ry in chips])
        cps = []
        for a in range(n):
            for j, (rx, ry) in enumerate(chips):
                k = 3 * a + j
                cps.append(_remote(p_refs[a].at[2 * rx + ry], t_refs[a].at[q], send.at[k], recv.at[k], (rx, ry, c)))
        for cp in cps:
            cp.start()
        for a in range(n):
            for j, (rx, ry) in enumerate(chips):
                k = 3 * a + j
                land = t_refs[a].at[2 * rx + ry]
                _remote(land, land, send.at[k], recv.at[k], (rx, ry, c)).wait_recv()
        for cp in cps:
            cp.wait_send()

    return _sequencer(name, _ID_CHIPS, body, parts,
                      [jax.ShapeDtypeStruct(p.shape, p.dtype) for p in parts], [3 * n, 3 * n])


def _sum_parts(name, parts, landed, where, after):
    n = len(parts)
    _, R, C = parts[0].shape
    tr = _half_tile(R, True)

    def body(w_ref, *refs):
        for a in range(n):
            p_ref, t0_ref, t1_ref, t2_ref = refs[4 * a:4 * a + 4]
            refs[4 * n + 1 + a][...] = (p_ref[...].astype(F32) + t0_ref[...].astype(F32)
                                        + t1_ref[...].astype(F32) + t2_ref[...].astype(F32))

    def part(slot):
        return pl.BlockSpec((None, tr, C), lambda i, w_ref: (w_ref[slot], i, 0))

    gs = pltpu.PrefetchScalarGridSpec(
        num_scalar_prefetch=1, grid=(R // tr,), in_specs=[part(0), part(1), part(2), part(3)] * n + [_any()],
        out_specs=[pl.BlockSpec((tr, C), lambda i, w_ref: (i, 0))] * n)
    ops = [x for p, t in zip(parts, landed) for x in (p, t, t, t)]
    return pl.pallas_call(
        body, name=name, grid_spec=gs, out_shape=[jax.ShapeDtypeStruct((R, C), F32)] * n,
        compiler_params=_cparams(("arbitrary",)),
    )(where, *ops, after)


def _send_halves(name, halves):
    n = len(halves)

    def body(*refs):
        h_refs, o_refs = refs[:n], refs[n:2 * n]
        send, recv = refs[2 * n:]
        x, y, c, _, _ = _place()
        _handshake([(x, y, 1 - c)])
        cps = [_remote(h_refs[a], o_refs[a], send.at[a], recv.at[a], (x, y, 1 - c)) for a in range(n)]
        for cp in cps:
            cp.start()
        for cp in cps:
            cp.wait()

    return _sequencer(name, _ID_SIBLING, body, halves,
                      [jax.ShapeDtypeStruct(h.shape, h.dtype) for h in halves], [n, n])


class _Order:
    def __init__(self):
        self.tok = None
        x, y, c, q, chips = _place()
        self.core = c.astype(jnp.int32).reshape(1)
        self.where = jnp.stack([q] + [2 * rx + ry for rx, ry in chips]).astype(jnp.int32)

    def tie(self, x):
        return x if self.tok is None else lax.optimization_barrier((x, self.tok))[0]

    def done(self, outs):
        self.tok = outs[0]
        return outs


class _Reduction:
    def __init__(self, tag, grads):
        self.tag, self.grads = tag, grads
        self.swapped = _swap_halves(f"rs_swap_{tag}", grads)

    def _by_shape(self):
        groups = {}
        for a, g in enumerate(self.grads):
            groups.setdefault(g.shape, []).append(a)
        return list(groups.values())

    def scatter(self, seq):
        self.parts = [None] * len(self.grads)
        for idx in self._by_shape():
            outs = seq.done(_chip_sum(f"rs_chipsum_{self.tag}_{idx[0]}", [self.grads[a] for a in idx],
                                      [self.swapped[a] for a in idx], seq.tok, seq.core))
            for a, o in zip(idx, outs):
                self.parts[a] = o
        self.landed = _scatter_parts(f"rs_scatter_{self.tag}", self.parts)

    def finish(self, seq):
        halves = [None] * len(self.grads)
        for idx in self._by_shape():
            outs = seq.done(_sum_parts(f"rs_sum_{self.tag}_{idx[0]}", [self.parts[a] for a in idx],
                                       [self.landed[a] for a in idx], seq.where, seq.tok))
            for a, o in zip(idx, outs):
                halves[a] = o
        return list(zip(halves, _send_halves(f"rs_join_{self.tag}", halves)))


def _allreduce_small(name, v):
    _, R, _ = v.shape

    def body(v_ref, o_ref, land, acc, send1, recv1, send2, recv2):
        x, y, c = lax.axis_index("x"), lax.axis_index("y"), lax.axis_index("c")
        me = 4 * x + 2 * y + c
        peers = []
        for k in range(1, 8):
            dx, dy, dc = (k >> 2) & 1, (k >> 1) & 1, k & 1
            px, py, pc = (1 - x if dx else x), (1 - y if dy else y), (1 - c if dc else c)
            peers.append((k, (px, py, pc), 4 * px + 2 * py + pc))
        land[me] = v_ref[me]
        out1 = [_remote(v_ref.at[pid], land.at[me], send1.at[k], recv1.at[k], dev) for k, dev, pid in peers]
        for cp in out1:
            cp.start()
        for k, dev, pid in peers:
            _remote(land.at[pid], land.at[pid], send1.at[k], recv1.at[k], dev).wait_recv()
        total = land[0]
        for j in range(1, 8):
            total = total + land[j]
        acc[...] = total
        o_ref[me] = total
        out2 = [_remote(acc, o_ref.at[me], send2.at[k], recv2.at[k], dev) for k, dev, pid in peers]
        for cp in out2:
            cp.start()
        for k, dev, pid in peers:
            _remote(o_ref.at[pid], o_ref.at[pid], send2.at[k], recv2.at[k], dev).wait_recv()
        for cp in out1 + out2:
            cp.wait_send()

    return pl.pallas_call(
        body, name=name, in_specs=[_vm()], out_specs=_vm(),
        out_shape=jax.ShapeDtypeStruct(v.shape, F32),
        scratch_shapes=[pltpu.VMEM(v.shape, F32), pltpu.VMEM((R, _LANES), F32)]
        + [pltpu.SemaphoreType.DMA((8,)) for _ in range(4)],
        compiler_params=pltpu.CompilerParams(vmem_limit_bytes=_VMEM_LIMIT),
    )(v)


def _all_peers():
    x, y, c = lax.axis_index("x"), lax.axis_index("y"), lax.axis_index("c")
    peers = []
    for k in range(1, 8):
        px, py, pc = (1 - x if k & 4 else x), (1 - y if k & 2 else y), (1 - c if k & 1 else c)
        peers.append((k, (px, py, pc), 4 * px + 2 * py + pc))
    return 4 * x + 2 * y + c, peers


def _exchange_slices(name, v):
    def body(v_ref, land, send, recv):
        me, peers = _all_peers()
        _handshake([dev for _, dev, _ in peers])
        cps = [_remote(v_ref.at[pid], land.at[me], send.at[k], recv.at[k], dev) for k, dev, pid in peers]
        for cp in cps:
            cp.start()
        for k, dev, pid in peers:
            _remote(land.at[pid], land.at[pid], send.at[k], recv.at[k], dev).wait_recv()
        for cp in cps:
            cp.wait_send()

    return _sequencer(name, _ID_ALL, body, [v], [jax.ShapeDtypeStruct(v.shape, v.dtype)], [8, 8])[0]


def _sum_slices(name, v, landed, after):
    _, R, _ = v.shape

    def body(v_ref, land_ref, after_ref, o_ref):
        me, peers = _all_peers()
        acc = v_ref[me]
        for _, _, pid in peers:
            acc = acc + land_ref[pid]
        o_ref[...] = acc

    return pl.pallas_call(
        body, name=name, in_specs=[_vm(), _vm(), _any()], out_specs=_vm(),
        out_shape=jax.ShapeDtypeStruct((R, _LANES), F32),
        compiler_params=pltpu.CompilerParams(vmem_limit_bytes=_VMEM_LIMIT),
    )(v, landed, after)


def _broadcast_slices(name, s):
    def body(s_ref, out, send, recv):
        me, peers = _all_peers()
        _handshake([dev for _, dev, _ in peers])
        cps = [_remote(s_ref, out.at[me], send.at[k], recv.at[k], dev) for k, dev, pid in peers]
        for cp in cps:
            cp.start()
        for k, dev, pid in peers:
            _remote(out.at[pid], out.at[pid], send.at[k], recv.at[k], dev).wait_recv()
        for cp in cps:
            cp.wait_send()

    return _sequencer(name, _ID_ALL, body, [s], [jax.ShapeDtypeStruct((8,) + s.shape, s.dtype)], [8, 8])[0]


_WEIGHT_NAMES = ['norm_mix', 'norm_ffn', 'norm_ple', 'norm_final', 'gm_w_in', 'gm_ln_g', 'gm_ln_b', 'gm_w_s',
                 'gm_b_s', 'gm_w_out', 's5_w_in', 's5_a_re', 's5_a_im', 's5_log_dt', 's5_b_re', 's5_b_im',
                 's5_c_re', 's5_c_im', 's5_d', 's5_w_out', 'ffn_w1', 'ffn_w3', 'ffn_w2', 'ple_w_gate', 'ple_w_proj']
_BIG = {'gm_w_in': 'col', 'gm_w_out': 'row', 's5_w_in': 'row', 's5_w_out': 'col', 'ffn_w1': 'col',
        'ffn_w3': 'col', 'ffn_w2': 'row', 'ple_w_gate': 'row', 'ple_w_proj': 'col'}


_VIEW = {'s5_a_re': (0, 2, 1), 's5_a_im': (0, 2, 1), 's5_b_re': (0, 2, 3, 1), 's5_b_im': (0, 2, 3, 1),
         's5_c_re': (0, 2, 3, 1), 's5_c_im': (0, 2, 3, 1)}


def _to_view(name, a):
    return jnp.transpose(a, _VIEW[name]) if name in _VIEW else a


def _from_view(name, a):
    if name not in _VIEW:
        return a
    perm = _VIEW[name]
    return jnp.transpose(a, [perm.index(i) for i in range(len(perm))])


def _rc(kind, shard_shape):
    rows, cols = shard_shape[-2:]
    return (rows // 2, cols) if kind == "col" else (rows, cols // 2)


def _pack(vecs, rows_multiple):
    flat = jnp.concatenate([a.reshape(-1).astype(F32) for a in vecs])
    unit = rows_multiple * _LANES
    pad = (-flat.shape[0]) % unit
    return jnp.pad(flat, (0, pad)).reshape(-1, _LANES)


def _unpack(buf, shapes):
    flat = buf.reshape(-1)
    out, off = [], 0
    for s in shapes:
        n = math.prod(s)
        out.append(flat[off:off + n].reshape(s))
        off += n
    return out


def _ident(accs, ex):
    return accs


def _add_resid(accs, ex):
    return [accs[0] + ex[0]]


def _swiglu_epi(accs, ex):
    a, b = accs
    return [a, b, a * _sig(a) * b]


def _swiglu_bwd_epi(accs, ex):
    df = accs[0]
    a, b = ex[0].astype(F32), ex[1].astype(F32)
    sa = _sig(a)
    return [df * b * (sa * (1.0 + a * (1.0 - sa))), df * (a * sa)]


def _ple_epi(accs, ex):
    xin, pv = ex[0], ex[1]
    kh = pv.shape[1] // 2
    pp = jnp.concatenate([_dot(pv[:, :kh], ex[2 + 2 * part], _NN) + _dot(pv[:, kh:], ex[3 + 2 * part], _NN)
                          for part in (0, 1)], axis=1)
    gt = _sig(accs[0])
    return [xin + gt * pp, gt, pp]


def _glu_epi(accs, ex):
    val, sg = accs[0], _sig(accs[1])
    return [ex[0] + val * sg, val, sg]


def kernel(x, p, norm_mix, norm_ffn, norm_ple, norm_final, gm_w_in, gm_ln_g, gm_ln_b, gm_w_s, gm_b_s, gm_w_out, s5_w_in, s5_a_re, s5_a_im, s5_log_dt, s5_b_re, s5_b_im, s5_c_re, s5_c_im, s5_d, s5_w_out, ffn_w1, ffn_w3, ffn_w2, ple_w_gate, ple_w_proj, loss_target, m_norm_mix, m_norm_ffn, m_norm_ple, m_norm_final, m_gm_w_in, m_gm_ln_g, m_gm_ln_b, m_gm_w_s, m_gm_b_s, m_gm_w_out, m_s5_w_in, m_s5_a_re, m_s5_a_im, m_s5_log_dt, m_s5_b_re, m_s5_b_im, m_s5_c_re, m_s5_c_im, m_s5_d, m_s5_w_out, m_ffn_w1, m_ffn_w3, m_ffn_w2, m_ple_w_gate, m_ple_w_proj, v_norm_mix, v_norm_ffn, v_norm_ple, v_norm_final, v_gm_w_in, v_gm_ln_g, v_gm_ln_b, v_gm_w_s, v_gm_b_s, v_gm_w_out, v_s5_w_in, v_s5_a_re, v_s5_a_im, v_s5_log_dt, v_s5_b_re, v_s5_b_im, v_s5_c_re, v_s5_c_im, v_s5_d, v_s5_w_out, v_ffn_w1, v_ffn_w3, v_ffn_w2, v_ple_w_gate, v_ple_w_proj):
    env = dict(locals())
    w = {n: env[n] for n in _WEIGHT_NAMES}
    mom = {n: env["m_" + n] for n in _WEIGHT_NAMES}
    var = {n: env["v_" + n] for n in _WEIGHT_NAMES}
    xs, tgt = x[0], loss_target[0]
    L, D = xs.shape
    depth = norm_mix.shape[0]
    qx, qy = lax.axis_index("x"), lax.axis_index("y")
    q = 2 * qx + qy

    W, last_cast = {}, [None]

    def gather(tag, items, after=None):
        def src(name):
            return w[name] if after is None else lax.optimization_barrier((w[name], after))[0]

        layer = items[0][1]
        if len({w[name].shape for name, _ in items}) == 1:
            kind = _BIG[items[0][0]]
            R, C = _rc(kind, w[items[0][0]].shape)
            shards = _cast_shards(f"cast_{tag}", [src(name) for name, _ in items], layer, kind, R, C)
        else:
            shards = []
            for name, _ in items:
                kind = _BIG[name]
                R, C = _rc(kind, w[name].shape)
                shards += _cast_shards(f"cast_{name}{layer}", [src(name)], layer, kind, R, C)
        last_cast[0] = shards[-1]
        full = _allgather(f"ag_{tag}", shards)
        W.update({it: _W(f, _BIG[it[0]]) for it, f in zip(items, full)})

    gather("gm_w_in", [("gm_w_in", 0)])
    gather("gm_w_out", [("gm_w_out", 0)])
    gather("ffn_up0", [("ffn_w1", 0), ("ffn_w3", 0)])
    gather("ffn_down0", [("ffn_w2", 0)])

    def gather_rest(after):
        gather("ple0", [("ple_w_gate", 0), ("ple_w_proj", 0)], after)
        gather("s5_w_in", [("s5_w_in", 0)], after)
        gather("s5_w_out", [("s5_w_out", 0)], after)
        gather("ffn_up1", [("ffn_w1", 1), ("ffn_w3", 1)], after)
        gather("ffn_down1", [("ffn_w2", 1)], after)
        gather("ple1", [("ple_w_gate", 1), ("ple_w_proj", 1)], after)

    d_slots = jnp.zeros((4, D // 4), F32)
    d_slots = lax.dynamic_update_slice(d_slots, s5_d.astype(F32), (q, 0))
    d_sum = _allreduce_small("ar_s5_d", _pack([d_slots], 64).reshape(8, -1, _LANES))
    d_full = (d_sum.reshape(-1)[:D] * 0.5).reshape(1, D)

    def ffn_fwd(i, xin):
        hf = _rms_fwd(f"rms_ffn{i}", xin, norm_ffn[i:i + 1])
        a, b, f = _mm_nn(f"ffn_up{i}", hf, [W["ffn_w1", i], W["ffn_w3", i]], 1024, 1408, ffn_w2.shape[1] * 4,
                         [], [_MXU, _MXU, _MXU], _swiglu_epi, tm=1024)
        xo = _mm_nn(f"ffn_down{i}", f, [W["ffn_w2", i]], 1408, 1024, D, [xin], [F32], _add_resid, tm=1024)[0]
        return xo, (xin, hf, a, b, f)

    def ple_fwd(i, xin):
        hp = _rms_fwd(f"rms_ple{i}", xin, norm_ple[i:i + 1])
        pi, wp = p[i, 0], W["ple_w_proj", i]

        def more(tm):
            ops = [(pi, pl.BlockSpec((tm, pi.shape[1]), lambda i_, j, k: (i_, 0)))]
            for part in (0, 1):
                for half in (0, 1):
                    ops.append((wp.a, pl.BlockSpec((None, None, wp.R, wp.C),
                                                   lambda i_, j, k, part=part, half=half: (2 * j + part, half, 0, 0))))
            return ops

        xo, gt, pp = _mm_nn(f"ple_gate{i}", hp, [W["ple_w_gate", i]], 512, 2 * wp.C, D, [xin], [F32, _MXU, F32],
                            _ple_epi, tm=1024, more=more)
        return xo, (xin, hp, pi, pp, gt)

    h0 = _rms_fwd("rms_mix0", xs, norm_mix[0:1])
    z = _mm_nn("gm_in", h0, [W["gm_w_in", 0]], 1024, 1024, 2 * D, [], [F32], _ident, tm=1024)[0]
    bsT = gm_b_s[0].T
    gm_m = _gmlp_fwd(z, gm_ln_g, gm_ln_b, gm_w_s[0], bsT)
    x1 = _mm_nn("gm_out", gm_m, [W["gm_w_out", 0]], 512, 1024, D, [xs], [F32], _add_resid, tm=2048)[0]
    gather_rest(x1)
    x1 = lax.optimization_barrier((x1, last_cast[0]))[0]
    x2, ffn0 = ffn_fwd(0, x1)
    x3, ple0 = ple_fwd(0, x2)

    T = D // _LANES
    lanes = _S5_GT * _S5_P
    sv = {n: _to_view(n, w[n])[0] for n in _VIEW}
    a_re, a_im, log_dt = sv["s5_a_re"], sv["s5_a_im"], s5_log_dt
    lbr, lbi, Bbar_re, Bbar_im = _s5_prep(a_re, a_im, log_dt, sv["s5_b_re"], sv["s5_b_im"])

    def to_bd(B):
        return jnp.transpose(B.reshape(_S5_P, _S5_C, T, _S5_GT), (2, 3, 1, 0))

    def to_cd(cw):
        return jnp.transpose(cw.reshape(_S5_C, _S5_P, T, _S5_GT), (2, 3, 1, 0))

    def to_lam(v):
        return jnp.transpose(v).reshape(T, 1, lanes)

    bd_re, bd_im = to_bd(Bbar_re), to_bd(Bbar_im)
    cd_re, cd_im = to_cd(sv["s5_c_re"]), to_cd(sv["s5_c_im"])
    lam_re, lam_im = to_lam(lbr), to_lam(lbi)

    h1 = _rms_fwd("rms_mix1", x3, norm_mix[1:2])
    u = _mm_nn("s5_in", h1, [W["s5_w_in", 0]], 512, 1024, D, [], [F32], _ident, tm=2048)[0]
    s5_g, s5_re, s5_im = _s5_fwd(u, bd_re, bd_im, cd_re, cd_im, lam_re, lam_im, d_full)
    x4, glu_val, glu_sg = _mm_nn("s5_out", s5_g, [W["s5_w_out", 0], W["s5_w_out", 0]], 1024, 1024, D, [x3],
                                 [F32, _MXU, _MXU], _glu_epi, tm=1024, cb_offsets=[0, 2])
    x5, ffn1 = ffn_fwd(1, x4)
    x6, ple1 = ple_fwd(1, x5)

    dx, dpre1, dpp1, d_norm_final, loss_rows = _loss_head(x6, norm_final[None], tgt, ple1[3], ple1[4])

    small = {}

    seq = _Order()
    tie, done = seq.tie, seq.done
    d_norm_ple, d_norm_ffn, d_norm_mix = [None] * depth, [None] * depth, [None] * depth
    reduced = {}

    def keep(names, layer, pairs):
        for n, pr in zip(names, pairs):
            reduced[n, layer] = pr

    ple_names, up_names, down_names = ["ple_w_gate", "ple_w_proj"], ["ffn_w1", "ffn_w3"], ["ffn_w2"]

    def ple_bwd(i, dxo, saved, elem=None):
        xin, hp, pi, pp, gt = saved
        dpre, dpp = elem or done(_ple_bwd_elem(tie(dxo), pp, gt))
        dwg = done(_mm_tn(f"ple_gate_dw{i}", tie(hp), [dpre], "row", 512, 1024, 512, 1024))[0]
        dwp = done(_mm_tn(f"ple_proj_dw{i}", tie(pi), [dpp], "col", 128, 512, 128, 512))[0]
        red = _Reduction(f"ple{i}", [dwg, dwp])
        dhp = done(_mm_nt(f"ple_gate_dx{i}", [tie(dpre)], [W["ple_w_gate", i]], 512, 1024, [], [F32], _ident,
                          tm=2048))[0]
        dxin, dxin_mxu, dg = done(_rms_bwd(f"rms_ple_bwd{i}", tie(dhp), xin, norm_ple[i:i + 1], dxo))
        return dxin, dxin_mxu, dg, red

    def ffn_bwd(i, dxo, dxo_mxu, saved, before_up):
        xin, hf, a, b, f = saved
        dw2 = done(_mm_tn(f"ffn_down_dw{i}", tie(f), [dxo_mxu], "row", 1408, 1024, 1408, 1024))[0]
        r_down = _Reduction(f"ffd{i}", [dw2])
        da, db = done(_mm_nt(f"ffn_down_dx{i}", [tie(dxo_mxu)], [W["ffn_w2", i]], 1408, 1024, [a, b], [_MXU, _MXU],
                             _swiglu_bwd_epi, tm=1024))
        for step in before_up:
            step()
        r_down.scatter(seq)
        dw1, dw3 = done(_mm_tn(f"ffn_up_dw{i}", tie(hf), [da, db], "col", 1024, 1408, 1024, 1408))
        r_up = _Reduction(f"ffu{i}", [dw1, dw3])
        dhf = done(_mm_nt(f"ffn_up_dx{i}", [tie(da), db], [W["ffn_w1", i], W["ffn_w3", i]], 1024, 1408, [], [F32],
                          _ident, tm=1024))[0]
        dxin, dxin_mxu, dg = done(_rms_bwd(f"rms_ffn_bwd{i}", tie(dhf), xin, norm_ffn[i:i + 1], dxo))
        r_up.scatter(seq)
        return dxin, dxin_mxu, dg, r_down, r_up

    dx, dx_mxu, d_norm_ple[1], r_ple1 = ple_bwd(1, dx, ple1, (dpre1, dpp1))
    dx, _, d_norm_ffn[1], r_down1, r_up1 = ffn_bwd(1, dx, dx_mxu, ffn1, [lambda: r_ple1.scatter(seq)])

    do = done([_glu_bwd_elem(tie(dx), glu_val, glu_sg)])[0]
    dw_s5out = done(_mm_tn("s5_out_dw", tie(s5_g), [do], "col", 1024, 1024, 1024, 1024))[0]
    r_s5out = _Reduction("s5out", [dw_s5out])
    dgy = done(_mm_nt("s5_out_dx", [tie(do)], [W["s5_w_out", 0]], 1024, 1024, [], [F32], _ident, tm=1024))[0]
    keep(ple_names, 1, r_ple1.finish(seq))
    keep(down_names, 1, r_down1.finish(seq))
    du, dbd_re, dbd_im, dcd_re, dcd_im, dl_re, dl_im, dd = done(_s5_bwd(
        tie(u), dgy, s5_re, s5_im, bd_re, bd_im, cd_re, cd_im, lam_re, lam_im, d_full))
    r_s5out.scatter(seq)
    dw_s5in = done(_mm_tn("s5_in_dw", tie(h1), [du], "row", 512, 1024, 512, 1024))[0]
    r_s5in = _Reduction("s5in", [dw_s5in])
    dh1 = done(_mm_nt("s5_in_dx", [tie(du)], [W["s5_w_in", 0]], 512, 1024, [], [F32], _ident, tm=2048))[0]
    dx, _, d_norm_mix[1] = done(_rms_bwd("rms_mix1_bwd", tie(dh1), x3, norm_mix[1:2], dx))
    keep(up_names, 1, r_up1.finish(seq))
    r_s5in.scatter(seq)

    def from_bd(t):
        return jnp.transpose(t, (3, 2, 0, 1)).reshape(_S5_P, _S5_C, T * _S5_GT)

    def from_cdT(t):
        return jnp.transpose(t, (2, 3, 0, 1)).reshape(_S5_C, _S5_P, T * _S5_GT)

    def from_lam(t):
        return jnp.transpose(t.reshape(T * _S5_GT, _S5_P))

    da_re, da_im, dlog_dt, db_re, db_im = _s5_prep_bwd(
        a_re, a_im, log_dt, sv["s5_b_re"], sv["s5_b_im"], from_lam(dl_re), from_lam(dl_im),
        from_bd(dbd_re), from_bd(dbd_im))
    small["s5_a_re"], small["s5_a_im"], small["s5_log_dt"] = da_re[None], da_im[None], dlog_dt
    small["s5_b_re"], small["s5_b_im"] = db_re[None], db_im[None]
    small["s5_c_re"], small["s5_c_im"] = from_cdT(dcd_re)[None], from_cdT(dcd_im)[None]

    dx, dx_mxu, d_norm_ple[0], r_ple0 = ple_bwd(0, dx, ple0)
    keep(["s5_w_out"], 0, r_s5out.finish(seq))
    xin0, hf0, a0, b0, f0 = ffn0
    da0, db0 = done(_mm_nt("ffn_down_dx0", [tie(dx_mxu)], [W["ffn_w2", 0]], 1408, 1024, [a0, b0], [_MXU, _MXU],
                           _swiglu_bwd_epi, tm=1024))
    r_ple0.scatter(seq)
    dw1, dw3 = done(_mm_tn("ffn_up_dw0", tie(hf0), [da0, db0], "col", 1024, 1408, 1024, 1408))
    r_up0 = _Reduction("ffu0", [dw1, dw3])
    keep(["s5_w_in"], 0, r_s5in.finish(seq))
    dw2 = done(_mm_tn("ffn_down_dw0", tie(f0), [dx_mxu], "row", 1408, 1024, 1408, 1024))[0]
    r_down0 = _Reduction("ffd0", [dw2])
    r_up0.scatter(seq)
    dhf0 = done(_mm_nt("ffn_up_dx0", [tie(da0), db0], [W["ffn_w1", 0], W["ffn_w3", 0]], 1024, 1408, [], [F32], _ident,
                       tm=1024))[0]
    dx, dx_mxu, d_norm_ffn[0] = done(_rms_bwd("rms_ffn_bwd0", tie(dhf0), xin0, norm_ffn[0:1], dx))
    keep(ple_names, 0, r_ple0.finish(seq))
    r_down0.scatter(seq)

    dw_gmout = done(_mm_tn("gm_out_dw", tie(gm_m), [dx_mxu], "row", 512, 1024, 512, 1024))[0]
    r_gmout = _Reduction("gmout", [dw_gmout])
    dgm = done(_mm_nt("gm_out_dx", [tie(dx_mxu)], [W["gm_w_out", 0]], 512, 1024, [], [F32], _ident, tm=2048))[0]
    dz, dws, dbsT, dlng, dlnb = done(_gmlp_bwd(tie(z), dgm, gm_ln_g, gm_ln_b, gm_w_s[0], bsT))
    dw_gmin = done(_mm_tn("gm_in_dw", tie(h0), [dz], "col", 1024, 1024, 1024, 1024))[0]
    r_gmin = _Reduction("gmin", [dw_gmin])
    dh0 = done(_mm_nt("gm_in_dx", [tie(dz)], [W["gm_w_in", 0]], 1024, 1024, [], [F32], _ident, tm=1024))[0]
    dx, _, d_norm_mix[0] = done(_rms_bwd("rms_mix0_bwd", tie(dh0), xs, norm_mix[0:1], dx))
    grad_x = dx[None]

    small["norm_mix"], small["norm_ffn"] = jnp.concatenate(d_norm_mix), jnp.concatenate(d_norm_ffn)
    small["norm_ple"], small["norm_final"] = jnp.concatenate(d_norm_ple), d_norm_final[0]
    small["gm_ln_g"], small["gm_ln_b"], small["gm_w_s"] = dlng, dlnb, dws[None]
    small["gm_b_s"] = dbsT[:, :_GM_HEADS].T[None]
    small["s5_d"] = dd

    small_names = [n for n in _WEIGHT_NAMES if n not in _BIG]
    packed = _pack([small[n] for n in small_names] + [loss_rows[:, :1]], 64).reshape(8, -1, _LANES)
    grads, deltas, new_m, new_v = {}, {}, {}, {}
    my_c = lax.axis_index("c")

    def adamw(n, layer, prev):
        kind = _BIG[n]
        R, C = _rc(kind, w[n].shape)
        return done(_adamw_big(f"adamw_{n}{layer}", w[n], mom[n], var[n], layer, reduced[n, layer], kind, R, C,
                               seq.core, seq.tok, prev))

    def adamw_last(names):
        for n in names:
            grads[n], deltas[n], new_m[n], new_v[n] = adamw(n, 0, late.get(n))

    late = {}
    for n in down_names + ple_names:
        late[n] = adamw(n, 1, None)
    keep(up_names, 0, r_up0.finish(seq))
    r_gmout.scatter(seq)
    r_gmin.scatter(seq)
    landed = _exchange_slices("ar_small_in", packed)
    for n in up_names:
        late[n] = adamw(n, 1, None)
    adamw_last(["s5_w_in", "s5_w_out"] + ple_names)
    keep(down_names, 0, r_down0.finish(seq))
    mine = done([_sum_slices("ar_small_sum", packed, landed, seq.tok)])[0]
    spread = _broadcast_slices("ar_small_out", mine)
    adamw_last(up_names)
    keep(["gm_w_out"], 0, r_gmout.finish(seq))
    adamw_last(down_names)

    spread = lax.optimization_barrier((spread, seq.tok))[0]
    summed = lax.dynamic_update_slice(spread, mine[None], (4 * qx + 2 * qy + my_c, 0, 0))
    *red_list, loss_sum = _unpack(summed, [small[n].shape for n in small_names] + [(1, 1)])
    red_small, loss = dict(zip(small_names, red_list)), loss_sum.reshape(())
    red_small["s5_d"] = lax.dynamic_slice(red_small["s5_d"], (0, q * (D // 4)), (1, D // 4))

    def two_d(a):
        return a.reshape((1,) * (2 - a.ndim) + a.shape)

    def views(src):
        return [two_d(_to_view(n, src[n])) for n in small_names]

    g_views = [two_d(red_small[n]) for n in small_names]
    w_views = views(w)
    dl, mo, vo = _adamw_small([tie(w_views[0])] + w_views[1:], g_views, views(mom), views(var))
    done(dl)
    for n, g_, d_, m_, v_ in zip(small_names, g_views, dl, mo, vo):
        grads[n], deltas[n], new_m[n], new_v[n] = [_from_view(n, t_.reshape(_to_view(n, w[n]).shape)).reshape(w[n].shape)
                                                  for t_ in (g_, d_, m_, v_)]

    keep(["gm_w_in"], 0, r_gmin.finish(seq))
    adamw_last(["gm_w_out", "gm_w_in"])

    return (loss, grad_x, *[grads[n] for n in _WEIGHT_NAMES], *[deltas[n] for n in _WEIGHT_NAMES],
            *[new_m[n] for n in _WEIGHT_NAMES], *[new_v[n] for n in _WEIGHT_NAMES])
```

```python
import math

import jax
import jax.numpy as jnp
from jax import lax
from jax.experimental import pallas as pl
from jax.experimental.pallas import tpu as pltpu
from jax.experimental.pallas import tpu_sc as plsc

F32 = jnp.float32
_MXU = jnp.bfloat16
_WIRE = jnp.bfloat16
_EPS = 1e-6
_VMEM_LIMIT = 56 * 1024 * 1024
_LANES = 128
_MESH = pl.DeviceIdType.MESH

_LR, _B1, _B2, _AEPS, _WD, _STEP = 0.001, 0.9, 0.999, 1e-08, 0.01, 10

_GM_CHUNK = 128
_GM_HEADS = 16
_S5_GT = 8
_S5_P = 64
_S5_C = 16

_NN = (((1,), (0,)), ((), ()))
_NT = (((1,), (1,)), ((), ()))
_TN = (((0,), (0,)), ((), ()))


def _cparams(sem):
    return pltpu.CompilerParams(dimension_semantics=sem, vmem_limit_bytes=_VMEM_LIMIT)


def _sig(x):
    return 0.5 * jnp.tanh(0.5 * x) + 0.5


_GC = math.sqrt(2.0 / math.pi)


def _gelu(x):
    return 0.5 * x * (1.0 + jnp.tanh(_GC * (x + 0.044715 * (x * x * x))))


def _gelu_grad(x):
    t = jnp.tanh(_GC * (x + 0.044715 * (x * x * x)))
    return 0.5 * (1.0 + t) + 0.5 * x * (1.0 - t * t) * (_GC * (1.0 + 3.0 * 0.044715 * x * x))


def _dot(a, b, dn):
    return lax.dot_general(a.astype(_MXU), b.astype(_MXU), dn, preferred_element_type=F32)


class _W:
    def __init__(self, arr, kind):
        self.a, self.kind = arr, kind
        self.R, self.C = arr.shape[2], arr.shape[3]

    def full_shape(self):
        return (2 * self.R, 4 * self.C) if self.kind == "col" else (4 * self.R, 2 * self.C)


def _part_index(kind, R, C, tr, tc, rb, cb):
    nr, nc = R // tr, C // tc
    if kind == "col":
        return cb // nc, rb // nr, rb % nr, cb % nc
    return rb // nr, cb // nc, rb % nr, cb % nc


def _wspec(w, tr, tc, rb_fn, cb_fn):
    assert w.R % tr == 0 and w.C % tc == 0, (w.R, w.C, tr, tc)

    def imap(i, j, k):
        return _part_index(w.kind, w.R, w.C, tr, tc, rb_fn(i, j, k), cb_fn(i, j, k))

    return pl.BlockSpec((None, None, tr, tc), imap)


def _gspec(kind, R, C, tr, tc):
    assert R % tr == 0 and C % tc == 0, (R, C, tr, tc)

    def imap(i, j, k):
        part, half, rbi, cbi = _part_index(kind, R, C, tr, tc, i, j)
        return half, part, rbi, cbi

    return pl.BlockSpec((None, None, tr, tc), imap)


def _mm(name, grid, a_ops, b_ops, pairs, acc_shape, n_acc, extras, outs, epilogue):
    nk = grid[2]
    na, nb, ne, no = len(a_ops), len(b_ops), len(extras), len(outs)

    def body(*refs):
        a_refs = refs[:na]
        b_refs = refs[na:na + nb]
        e_refs = refs[na + nb:na + nb + ne]
        o_refs = refs[na + nb + ne:na + nb + ne + no]
        acc_refs = refs[na + nb + ne + no:]
        k = pl.program_id(2)

        def products():
            sums = [None] * n_acc
            for ai, bi, ci, dn in pairs:
                d = _dot(a_refs[ai][...], b_refs[bi][...], dn)
                sums[ci] = d if sums[ci] is None else sums[ci] + d
            return sums

        def finish(accs):
            res = epilogue(accs, [e[...] for e in e_refs])
            for o, r in zip(o_refs, res):
                o[...] = r.astype(o.dtype)

        if nk == 1:
            finish(products())
            return

        @pl.when(k == 0)
        def _():
            for acc, d in zip(acc_refs, products()):
                acc[...] = d

        @pl.when(jnp.logical_and(k > 0, k < nk - 1))
        def _():
            for acc, d in zip(acc_refs, products()):
                acc[...] += d

        @pl.when(k == nk - 1)
        def _():
            finish([acc[...] + d for acc, d in zip(acc_refs, products())])

    ops = list(a_ops) + list(b_ops) + list(extras)
    return pl.pallas_call(
        body, name=name, grid=grid,
        in_specs=[s for _, s in ops],
        out_specs=[s for _, s in outs],
        out_shape=[s for s, _ in outs],
        scratch_shapes=[pltpu.VMEM(acc_shape, F32) for _ in range(n_acc if nk > 1 else 0)],
        compiler_params=_cparams(("parallel", "parallel", "arbitrary")),
    )(*[a for a, _ in ops])


def _bs(shape, fn):
    return pl.BlockSpec(shape, fn)


def _tile_m(L):
    return min(L, 512)


def _mm_nn(name, x, ws, tk, tn, n_out, extras, outs_sd, epilogue, tm=None, cb_offsets=None, more=None):
    M, K = x.shape
    tm = min(M, tm or _tile_m(M))
    grid = (M // tm, n_out // tn, K // tk)
    a_ops = [(x, _bs((tm, tk), lambda i, j, k: (i, k)))]
    cb_offsets = cb_offsets or [0] * len(ws)
    b_ops = [(w.a, _wspec(w, tk, tn, lambda i, j, k: k, (lambda off: lambda i, j, k: j + off)(off)))
             for w, off in zip(ws, cb_offsets)]
    pairs = [(0, bi, bi, _NN) for bi in range(len(ws))]
    mn = _bs((tm, tn), lambda i, j, k: (i, j))
    ex = [(e, mn) for e in extras] + (more(tm) if more else [])
    outs = [(jax.ShapeDtypeStruct((M, n_out), dt), mn) for dt in outs_sd]
    return _mm(name, grid, a_ops, b_ops, pairs, (tm, tn), len(ws), ex, outs, epilogue)


def _mm_nt(name, xs, ws, tn, tk, extras, outs_sd, epilogue, tm=None):
    M, Nw = xs[0].shape
    Kw = ws[0].full_shape()[0]
    tm = min(M, tm or _tile_m(M))
    grid = (M // tm, Kw // tn, Nw // tk)
    a_ops = [(x, _bs((tm, tk), lambda i, j, k: (i, k))) for x in xs]
    b_ops = [(w.a, _wspec(w, tn, tk, lambda i, j, k: j, lambda i, j, k: k)) for w in ws]
    pairs = [(i, i, 0, _NT) for i in range(len(ws))]
    mn = _bs((tm, tn), lambda i, j, k: (i, j))
    ex = [(e, mn) for e in extras]
    outs = [(jax.ShapeDtypeStruct((M, Kw), dt), mn) for dt in outs_sd]
    return _mm(name, grid, a_ops, b_ops, pairs, (tm, tn), 1, ex, outs, epilogue)


def _mm_tn(name, x, dys, kind, R, C, tm, tn, tk=None):
    L, Kw = x.shape
    Nw = dys[0].shape[1]
    tk = tk or min(L, 1024)
    grid = (Kw // tm, Nw // tn, L // tk)
    a_ops = [(x, _bs((tk, tm), lambda i, j, k: (k, i)))]
    b_ops = [(dy, _bs((tk, tn), lambda i, j, k: (k, j))) for dy in dys]
    pairs = [(0, bi, bi, _TN) for bi in range(len(dys))]
    gs = _gspec(kind, R, C, tm, tn)
    outs = [(jax.ShapeDtypeStruct((2, 4, R, C), _WIRE), gs) for _ in dys]
    return _mm(name, grid, a_ops, b_ops, pairs, (tm, tn), len(dys), [], outs, lambda accs, ex: accs)


def _row_tile(L, n_streams):
    return min(L, 512 if n_streams <= 5 else 256)


def _rowwise(name, body, ins, outs, L, acc_outs=()):
    tr = _row_tile(L, sum(kind == "row" for _, kind in ins) + len(outs))
    n_in, n_out = len(ins), len(outs)

    def kbody(*refs):
        i_refs, o_refs, a_refs = refs[:n_in], refs[n_in:n_in + n_out], refs[n_in + n_out:]
        res, sums = body(*[r[...] for r in i_refs])
        for o, r in zip(o_refs, res):
            o[...] = r.astype(o.dtype)
        if a_refs:
            @pl.when(pl.program_id(0) == 0)
            def _():
                for a in a_refs:
                    a[...] = jnp.zeros(a.shape, F32)
            for a, s in zip(a_refs, sums):
                a[...] += s

    in_specs = []
    for arr, kind in ins:
        if kind == "row":
            in_specs.append(pl.BlockSpec((tr, arr.shape[1]), lambda i: (i, 0)))
        else:
            in_specs.append(pl.BlockSpec(arr.shape, lambda i: (0, 0)))
    out_specs = [pl.BlockSpec((tr, c), lambda i: (i, 0)) for c, _ in outs]
    out_shape = [jax.ShapeDtypeStruct((L, c), dt) for c, dt in outs]
    out_specs += [pl.BlockSpec((1, c), lambda i: (0, 0)) for c in acc_outs]
    out_shape += [jax.ShapeDtypeStruct((1, c), F32) for c in acc_outs]
    return pl.pallas_call(
        kbody, name=name, grid=(L // tr,), in_specs=in_specs, out_specs=out_specs, out_shape=out_shape,
        compiler_params=_cparams(("arbitrary",)),
    )(*[a for a, _ in ins])


def _rms_fwd(name, x, g):
    def body(xv, gv):
        r = lax.rsqrt(jnp.mean(xv * xv, axis=-1, keepdims=True) + _EPS)
        return [xv * r * gv], []
    return _rowwise(name, body, [(x, "row"), (g, "vec")], [(x.shape[1], _MXU)], x.shape[0])[0]


def _rms_bwd(name, dh, x, g, dres):
    def body(dhv, xv, gv, dr):
        r = lax.rsqrt(jnp.mean(xv * xv, axis=-1, keepdims=True) + _EPS)
        xh = xv * r
        dxh = dhv * gv
        dx = dr + r * (dxh - xh * jnp.mean(dxh * xh, axis=-1, keepdims=True))
        return [dx, dx], [jnp.sum(dhv * xh, axis=0, keepdims=True)]
    D = x.shape[1]
    return _rowwise(name, body, [(dh, "row"), (x, "row"), (g, "vec"), (dres, "row")], [(D, F32), (D, _MXU)],
                    x.shape[0], [D])


def _loss_head(x, g, target, pp, gt):
    D = x.shape[1]

    def body(xv, gv, tv, ppv, gtv):
        r = lax.rsqrt(jnp.mean(xv * xv, axis=-1, keepdims=True) + _EPS)
        xh = xv * r
        e = xh * gv - tv
        dy = e * (1.0 / D)
        dxh = dy * gv
        dx = r * (dxh - xh * jnp.mean(dxh * xh, axis=-1, keepdims=True))
        row_loss = 0.5 * jnp.mean(e * e, axis=-1, keepdims=True)
        lsum = jnp.sum(row_loss, axis=0, keepdims=True) + jnp.zeros((1, _LANES), F32)
        gt32 = gtv.astype(F32)
        return [dx, dx * ppv * gt32 * (1.0 - gt32), dx * gt32], [jnp.sum(dy * xh, axis=0, keepdims=True), lsum]
    return _rowwise("loss_head", body, [(x, "row"), (g, "vec"), (target, "row"), (pp, "row"), (gt, "row")],
                    [(D, F32), (D, _MXU), (D, _MXU)], x.shape[0], [D, _LANES])


def _ple_bwd_elem(dx, pp, gt):
    def body(dxv, ppv, gtv):
        gt32 = gtv.astype(F32)
        return [dxv * ppv * gt32 * (1.0 - gt32), dxv * gt32], []
    D = dx.shape[1]
    return _rowwise("ple_bwd_elem", body, [(dx, "row"), (pp, "row"), (gt, "row")], [(D, _MXU), (D, _MXU)], dx.shape[0])


def _glu_bwd_elem(dx, val, sg):
    def body(dxv, vv, sv):
        v32, s32 = vv.astype(F32), sv.astype(F32)
        return [jnp.concatenate([dxv * s32, dxv * v32 * s32 * (1.0 - s32)], axis=1)], []
    D = dx.shape[1]
    return _rowwise("glu_bwd_elem", body, [(dx, "row"), (val, "row"), (sg, "row")], [(2 * D, _MXU)], dx.shape[0])[0]


def _gm_common(z, ln_g, ln_b, wc_bf, bsT):
    W = z.shape[1] // 2
    zu, zv = z[:, :W], z[:, W:]
    u, v = _gelu(zu), _gelu(zv)
    mu = jnp.mean(v, axis=-1, keepdims=True)
    vc = v - mu
    rstd = lax.rsqrt(jnp.mean(vc * vc, axis=-1, keepdims=True) + _EPS)
    vh = vc * rstd
    vn = vh * ln_g + ln_b
    vnb = vn.astype(_MXU)
    svs = []
    for h in range(_GM_HEADS):
        sl = slice(h * _LANES, (h + 1) * _LANES)
        svs.append(_dot(wc_bf[h], vnb[:, sl], _NN) + bsT[:, h:h + 1])
    return zu, zv, u, vh, rstd, vnb, svs


def _causal(w):
    t = lax.broadcasted_iota(jnp.int32, w.shape, w.ndim - 2)
    s = lax.broadcasted_iota(jnp.int32, w.shape, w.ndim - 1)
    return jnp.where(s <= t, w, jnp.zeros_like(w))


def _gmlp_fwd(z, ln_g, ln_b, w_s, bsT):
    L, W2 = z.shape
    W = W2 // 2

    def body(z_ref, g_ref, b_ref, ws_ref, bs_ref, m_ref):
        wc = _causal(ws_ref[...]).astype(_MXU)
        _, _, u, _, _, _, svs = _gm_common(z_ref[...], g_ref[...], b_ref[...], wc, bs_ref[...])
        for h in range(_GM_HEADS):
            sl = slice(h * _LANES, (h + 1) * _LANES)
            m_ref[:, sl] = (u[:, sl] * svs[h]).astype(m_ref.dtype)

    return pl.pallas_call(
        body, name="gmlp_fwd", grid=(L // _GM_CHUNK,),
        in_specs=[pl.BlockSpec((_GM_CHUNK, W2), lambda n: (n, 0)),
                  pl.BlockSpec((1, W), lambda n: (0, 0)), pl.BlockSpec((1, W), lambda n: (0, 0)),
                  pl.BlockSpec(w_s.shape, lambda n: (0, 0, 0)), pl.BlockSpec(bsT.shape, lambda n: (0, 0))],
        out_specs=pl.BlockSpec((_GM_CHUNK, W), lambda n: (n, 0)),
        out_shape=jax.ShapeDtypeStruct((L, W), _MXU),
        compiler_params=_cparams(("arbitrary",)),
    )(z, ln_g, ln_b, w_s, bsT)


def _gmlp_bwd(z, dm, ln_g, ln_b, w_s, bsT):
    L, W2 = z.shape
    W = W2 // 2
    T = _GM_CHUNK

    def body(z_ref, dm_ref, g_ref, b_ref, ws_ref, bs_ref, dz_ref, dws_ref, dbs_ref, dg_ref, db_ref):
        @pl.when(pl.program_id(0) == 0)
        def _():
            dws_ref[...] = jnp.zeros(dws_ref.shape, F32)
            dbs_ref[...] = jnp.zeros(dbs_ref.shape, F32)
            dg_ref[...] = jnp.zeros(dg_ref.shape, F32)
            db_ref[...] = jnp.zeros(db_ref.shape, F32)

        wc = _causal(ws_ref[...]).astype(_MXU)
        ln_g_v = g_ref[...]
        zu, zv, u, vh, rstd, vnb, svs = _gm_common(z_ref[...], ln_g_v, b_ref[...], wc, bs_ref[...])
        dmv = dm_ref[...]
        lane = lax.broadcasted_iota(jnp.int32, (T, _LANES), 1)
        dbs = jnp.zeros((T, _LANES), F32)
        dvn_parts = []
        for h in range(_GM_HEADS):
            sl = slice(h * _LANES, (h + 1) * _LANES)
            dsv = dmv[:, sl] * u[:, sl]
            dz_ref[:, sl] = (dmv[:, sl] * svs[h] * _gelu_grad(zu[:, sl])).astype(dz_ref.dtype)
            dbs = dbs + jnp.where(lane == h, jnp.sum(dsv, axis=1, keepdims=True), 0.0)
            dsvb = dsv.astype(_MXU)
            dws_ref[h] += _dot(dsvb, vnb[:, sl], _NT)
            dvn_parts.append(_dot(wc[h], dsvb, _TN))
        dbs_ref[...] += dbs
        dvn = jnp.concatenate(dvn_parts, axis=1)
        dg_ref[...] += jnp.sum(dvn * vh, axis=0, keepdims=True)
        db_ref[...] += jnp.sum(dvn, axis=0, keepdims=True)
        dxh = dvn * ln_g_v
        dv = rstd * (dxh - jnp.mean(dxh, axis=-1, keepdims=True) - vh * jnp.mean(dxh * vh, axis=-1, keepdims=True))
        dz_ref[:, W:] = (dv * _gelu_grad(zv)).astype(dz_ref.dtype)

        @pl.when(pl.program_id(0) == pl.num_programs(0) - 1)
        def _():
            dws_ref[...] = _causal(dws_ref[...])

    return pl.pallas_call(
        body, name="gmlp_bwd", grid=(L // T,),
        in_specs=[pl.BlockSpec((T, W2), lambda n: (n, 0)), pl.BlockSpec((T, W), lambda n: (n, 0)),
                  pl.BlockSpec((1, W), lambda n: (0, 0)), pl.BlockSpec((1, W), lambda n: (0, 0)),
                  pl.BlockSpec(w_s.shape, lambda n: (0, 0, 0)), pl.BlockSpec(bsT.shape, lambda n: (0, 0))],
        out_specs=[pl.BlockSpec((T, W2), lambda n: (n, 0)),
                   pl.BlockSpec(w_s.shape, lambda n: (0, 0, 0)), pl.BlockSpec((T, _LANES), lambda n: (0, 0)),
                   pl.BlockSpec((1, W), lambda n: (0, 0)), pl.BlockSpec((1, W), lambda n: (0, 0))],
        out_shape=[jax.ShapeDtypeStruct((L, W2), _MXU), jax.ShapeDtypeStruct(w_s.shape, F32),
                   jax.ShapeDtypeStruct((T, _LANES), F32),
                   jax.ShapeDtypeStruct((1, W), F32), jax.ShapeDtypeStruct((1, W), F32)],
        compiler_params=_cparams(("arbitrary",)),
    )(z, dm, ln_g, ln_b, w_s, bsT)


def _s5_prep_math(a_re, a_im, log_dt):
    dt = jnp.exp(log_dt)
    xr, xi = a_re * dt, a_im * dt
    e = jnp.exp(xr)
    lbr, lbi = e * jnp.cos(xi), e * jnp.sin(xi)
    dn = a_re * a_re + a_im * a_im
    nr, ni = lbr - 1.0, lbi
    pr, pi = nr * a_re + ni * a_im, ni * a_re - nr * a_im
    return dt, lbr, lbi, dn, nr, ni, pr, pi


def _vm():
    return pl.BlockSpec(memory_space=pltpu.VMEM)


def _s5_prep(a_re, a_im, log_dt, b_re, b_im):
    def body(ar_ref, ai_ref, ld_ref, br_ref, bi_ref, lbr_ref, lbi_ref, Br_ref, Bi_ref):
        _, lbr, lbi, dn, _, _, pr, pi = _s5_prep_math(ar_ref[...], ai_ref[...], ld_ref[...])
        cr, ci = (pr / dn)[:, None, :], (pi / dn)[:, None, :]
        lbr_ref[...] = lbr
        lbi_ref[...] = lbi
        br, bi = br_ref[...], bi_ref[...]
        Br_ref[...] = cr * br - ci * bi
        Bi_ref[...] = cr * bi + ci * br

    sd = jax.ShapeDtypeStruct
    return pl.pallas_call(
        body, name="s5_prep", in_specs=[_vm()] * 5, out_specs=[_vm()] * 4,
        out_shape=[sd(a_re.shape, F32), sd(a_re.shape, F32), sd(b_re.shape, F32), sd(b_re.shape, F32)],
    )(a_re, a_im, log_dt, b_re, b_im)


def _s5_prep_bwd(a_re, a_im, log_dt, b_re, b_im, dlbr_s, dlbi_s, dBr, dBi):
    def body(ar_ref, ai_ref, ld_ref, br_ref, bi_ref, dlr_ref, dli_ref, dBr_ref, dBi_ref,
             dar_ref, dai_ref, dld_ref, dbr_ref, dbi_ref):
        a_re_v, a_im_v = ar_ref[...], ai_ref[...]
        dt, lbr, lbi, dn, nr, ni, pr, pi = _s5_prep_math(a_re_v, a_im_v, ld_ref[...])
        cr, ci = (pr / dn)[:, None, :], (pi / dn)[:, None, :]
        br, bi, dBr_v, dBi_v = br_ref[...], bi_ref[...], dBr_ref[...], dBi_ref[...]
        dbr_ref[...] = cr * dBr_v + ci * dBi_v
        dbi_ref[...] = cr * dBi_v - ci * dBr_v
        dcr = jnp.sum(br * dBr_v + bi * dBi_v, axis=1)
        dci = jnp.sum(br * dBi_v - bi * dBr_v, axis=1)
        dpr, dpi = dcr / dn, dci / dn
        ddn = -(dcr * pr + dci * pi) / (dn * dn)
        dnr = dpr * a_re_v - dpi * a_im_v
        dni = dpr * a_im_v + dpi * a_re_v
        dlbr = dlr_ref[...] + dnr
        dlbi = dli_ref[...] + dni
        dxr = dlbr * lbr + dlbi * lbi
        dxi = dlbi * lbr - dlbr * lbi
        dar_ref[...] = dpr * nr + dpi * ni + 2.0 * ddn * a_re_v + dxr * dt
        dai_ref[...] = dpr * ni - dpi * nr + 2.0 * ddn * a_im_v + dxi * dt
        dld_ref[...] = jnp.sum(dxr * a_re_v + dxi * a_im_v, axis=0, keepdims=True) * dt

    sd = jax.ShapeDtypeStruct
    return pl.pallas_call(
        body, name="s5_prep_bwd", in_specs=[_vm()] * 9, out_specs=[_vm()] * 5,
        out_shape=[sd(a_re.shape, F32), sd(a_re.shape, F32), sd(log_dt.shape, F32),
                   sd(b_re.shape, F32), sd(b_re.shape, F32)],
    )(a_re, a_im, log_dt, b_re, b_im, dlbr_s, dlbi_s, dBr, dBi)


def _shift_rows(v, down):
    n = v.shape[0]
    rolled = pltpu.roll(v, 1 if down else n - 1, 0)
    row = lax.broadcasted_iota(jnp.int32, v.shape, 0)
    return jnp.where(row == (0 if down else n - 1), 0.0, rolled)


def _cmul(ar, ai, br, bi):
    return ar * br - ai * bi, ar * bi + ai * br


_SEG = 8
_UNROLL = 8


def _seg_rows(k):
    if isinstance(k, int):
        return pl.ds(k * _SEG, _SEG)
    return pl.ds(pl.multiple_of(k * _SEG, _SEG), _SEG)


def _unrolled(n, step, init):
    main = n // _UNROLL

    def trip(kk, s):
        for uu in range(_UNROLL):
            s = step(kk * _UNROLL + uu, s)
        return s

    s = lax.fori_loop(0, main, trip, init)
    for r in range(main * _UNROLL, n):
        s = step(r, s)
    return s


def _interleave(src_ref, dst_ref, nk):
    def step(k, carry):
        dst_ref[_seg_rows(k), :] = src_ref[pl.ds(k, _SEG, stride=nk), :]
        return carry
    _unrolled(nk, step, 0)


def _deinterleave(src_ref, dst_ref, nk):
    def step(k, carry):
        dst_ref[pl.ds(k, _SEG, stride=nk), :] = src_ref[_seg_rows(k), :]
        return carry
    _unrolled(nk, step, 0)


def _segment_inits(er, ei, ar, ai, nk, down):
    pr, pi = ar, ai
    for _ in range(int(math.log2(nk))):
        pr, pi = _cmul(pr, pi, pr, pi)
    fr, fi = er, ei
    for _ in range(_SEG - 1):
        sr, si = _shift_rows(fr, down), _shift_rows(fi, down)
        mr, mi = _cmul(pr, pi, sr, si)
        fr, fi = er + mr, ei + mi
    return _shift_rows(fr, down), _shift_rows(fi, down)


def _scan_states(x_re, x_im, ar, ai, nk):
    lanes = ar.shape[1]

    def step(k, s):
        rows = _seg_rows(k)
        mr, mi = _cmul(ar, ai, s[0], s[1])
        return mr + x_re[rows, :], mi + x_im[rows, :]

    zero = jnp.zeros((_SEG, lanes), F32)
    er, ei = _unrolled(nk, step, (zero, zero))
    ir, ii = _segment_inits(er, ei, ar, ai, nk, True)

    def step2(k, s):
        rows = _seg_rows(k)
        mr, mi = _cmul(ar, ai, s[0], s[1])
        nr, ni = mr + x_re[rows, :], mi + x_im[rows, :]
        x_re[rows, :] = nr
        x_im[rows, :] = ni
        return nr, ni

    _unrolled(nk, step2, (ir, ii))


def _s5_tile_fwd(u, bd_re, bd_im, cd_re, cd_im, ar, ai, d, s_re, s_im, nk):
    s_re[...] = _dot(u, bd_re, _NN)
    s_im[...] = _dot(u, bd_im, _NN)
    _scan_states(s_re, s_im, ar, ai, nk)
    return _dot(s_re[...], cd_re, _NN) - _dot(s_im[...], cd_im, _NN) + d * u


def _s5_specs(L, T):
    lanes = _S5_GT * _S5_P
    u_spec = pl.BlockSpec((L, _LANES), lambda t: (0, t))
    bd_spec = pl.BlockSpec((None, _S5_GT, _S5_C, _S5_P), lambda t: (t, 0, 0, 0))
    cd_spec = pl.BlockSpec((None, _S5_GT, _S5_P, _S5_C), lambda t: (t, 0, 0, 0))
    lam_spec = pl.BlockSpec((None, 1, lanes), lambda t: (t, 0, 0))
    d_spec = pl.BlockSpec((1, _LANES), lambda t: (0, t))
    return lanes, u_spec, bd_spec, cd_spec, lam_spec, d_spec


def _fill_block_diag(dst_ref, blocks_ref):
    _, a, b = blocks_ref.shape
    dst_ref[...] = jnp.zeros(dst_ref.shape, F32)
    for g in range(_S5_GT):
        dst_ref[g * a:(g + 1) * a, g * b:(g + 1) * b] = blocks_ref[g]


def _take_block_diag(dst_ref, v):
    _, a, b = dst_ref.shape
    for g in range(_S5_GT):
        dst_ref[g] = v[g * a:(g + 1) * a, g * b:(g + 1) * b]


def _s5_dense(bdr, bdi, cdr, cdi, dense):
    for src, dst in zip((bdr, bdi, cdr, cdi), dense):
        _fill_block_diag(dst, src)
    return [dst[...] for dst in dense]


def _s5_dense_scratch(lanes):
    return [pltpu.VMEM((_LANES, lanes), F32), pltpu.VMEM((_LANES, lanes), F32),
            pltpu.VMEM((lanes, _LANES), F32), pltpu.VMEM((lanes, _LANES), F32)]


def _s5_fwd(u, bd_re, bd_im, cd_re, cd_im, lam_re, lam_im, d):
    L, Wd = u.shape
    T = Wd // _LANES
    nk = L // _SEG
    lanes, u_spec, bd_spec, cd_spec, lam_spec, d_spec = _s5_specs(L, T)
    s_spec = pl.BlockSpec((L, lanes), lambda t: (0, t))

    def body(u_ref, bdr, bdi, cdr, cdi, lr, li, d_ref, g_ref, s_re, s_im, up, tmp, *dense):
        ar = jnp.broadcast_to(lr[...], (_SEG, lanes))
        ai = jnp.broadcast_to(li[...], (_SEG, lanes))
        bd_re_v, bd_im_v, cd_re_v, cd_im_v = _s5_dense(bdr, bdi, cdr, cdi, dense)
        _interleave(u_ref, up, nk)
        y = _s5_tile_fwd(up[...], bd_re_v, bd_im_v, cd_re_v, cd_im_v, ar, ai, d_ref[...], s_re, s_im, nk)
        up[...] = _gelu(y)
        _deinterleave(up, tmp, nk)
        g_ref[...] = tmp[...].astype(g_ref.dtype)

    return pl.pallas_call(
        body, name="s5_fwd", grid=(T,),
        in_specs=[u_spec, bd_spec, bd_spec, cd_spec, cd_spec, lam_spec, lam_spec, d_spec],
        out_specs=[u_spec, s_spec, s_spec],
        out_shape=[jax.ShapeDtypeStruct((L, Wd), _MXU), jax.ShapeDtypeStruct((L, T * lanes), F32),
                   jax.ShapeDtypeStruct((L, T * lanes), F32)],
        scratch_shapes=[pltpu.VMEM((L, _LANES), F32) for _ in range(2)] + _s5_dense_scratch(lanes),
        compiler_params=_cparams(("arbitrary",)),
    )(u, bd_re, bd_im, cd_re, cd_im, lam_re, lam_im, d)


def _s5_bwd(u, dg, states_re, states_im, bd_re, bd_im, cd_re, cd_im, lam_re, lam_im, d):
    L, Wd = u.shape
    T = Wd // _LANES
    nk = L // _SEG
    lanes, u_spec, bd_spec, cd_spec, lam_spec, d_spec = _s5_specs(L, T)
    s_spec = pl.BlockSpec((L, lanes), lambda t: (0, t))

    def body(u_ref, dg_ref, s_re, s_im, bdr, bdi, cdr, cdi, lr, li, d_ref,
             du_ref, dbdr, dbdi, dcdr, dcdi, dlr, dli, dd_ref, g_re, g_im, up, dgp, tmp, *dense):
        ar = jnp.broadcast_to(lr[...], (_SEG, lanes))
        ai = jnp.broadcast_to(li[...], (_SEG, lanes))
        bd_re_v, bd_im_v, cd_re_v, cd_im_v = _s5_dense(bdr, bdi, cdr, cdi, dense)
        _interleave(u_ref, up, nk)
        _interleave(dg_ref, dgp, nk)
        uv, dv = up[...], d_ref[...]
        y = _dot(s_re[...], cd_re_v, _NN) - _dot(s_im[...], cd_im_v, _NN) + dv * uv
        dy = dgp[...] * _gelu_grad(y)
        dd_ref[...] = jnp.sum(dy * uv, axis=0, keepdims=True)
        dyb = dy.astype(_MXU)
        _take_block_diag(dcdr, _dot(dyb, s_re[...], _TN))
        _take_block_diag(dcdi, -_dot(dyb, s_im[...], _TN))
        g_re[...] = _dot(dyb, cd_re_v, _NT)
        g_im[...] = -_dot(dyb, cd_im_v, _NT)

        nai = -ai

        def step(j, s):
            rows = _seg_rows(nk - 1 - j)
            mr, mi = _cmul(ar, nai, s[0], s[1])
            return mr + g_re[rows, :], mi + g_im[rows, :]

        zero = jnp.zeros((_SEG, lanes), F32)
        er, ei = _unrolled(nk, step, (zero, zero))
        ir, ii = _segment_inits(er, ei, ar, nai, nk, False)

        def acc_lam(gr, gi, pr, pi, acc):
            return acc[0] + gr * pr + gi * pi, acc[1] + gi * pr - gr * pi

        def step2(j, carry):
            s, acc = carry
            k = nk - 1 - j
            rows = _seg_rows(k)
            mr, mi = _cmul(ar, nai, s[0], s[1])
            nr, ni = mr + g_re[rows, :], mi + g_im[rows, :]
            g_re[rows, :] = nr
            g_im[rows, :] = ni
            prev = _seg_rows(k - 1)
            return (nr, ni), acc_lam(nr, ni, s_re[prev, :], s_im[prev, :], acc)

        (g0r, g0i), acc = _unrolled(nk - 1, step2, ((ir, ii), (zero, zero)))
        first = _seg_rows(0)
        mr, mi = _cmul(ar, nai, g0r, g0i)
        nr, ni = mr + g_re[first, :], mi + g_im[first, :]
        g_re[first, :] = nr
        g_im[first, :] = ni
        last = _seg_rows(nk - 1)
        acc = acc_lam(nr, ni, _shift_rows(s_re[last, :], True), _shift_rows(s_im[last, :], True), acc)
        dlr[...] = jnp.sum(acc[0], axis=0, keepdims=True)
        dli[...] = jnp.sum(acc[1], axis=0, keepdims=True)

        gtr, gti = g_re[...].astype(_MXU), g_im[...].astype(_MXU)
        ub = uv.astype(_MXU)
        _take_block_diag(dbdr, _dot(ub, gtr, _TN))
        _take_block_diag(dbdi, _dot(ub, gti, _TN))
        dgp[...] = _dot(gtr, bd_re_v, _NT) + _dot(gti, bd_im_v, _NT) + dy * dv
        _deinterleave(dgp, tmp, nk)
        du_ref[...] = tmp[...].astype(du_ref.dtype)

    sd = jax.ShapeDtypeStruct
    big = sd((T, _S5_GT, _S5_C, _S5_P), F32)
    return pl.pallas_call(
        body, name="s5_bwd", grid=(T,),
        in_specs=[u_spec, u_spec, s_spec, s_spec, bd_spec, bd_spec, cd_spec, cd_spec, lam_spec, lam_spec, d_spec],
        out_specs=[u_spec, bd_spec, bd_spec, bd_spec, bd_spec, lam_spec, lam_spec, d_spec],
        out_shape=[sd((L, Wd), _MXU), big, big, big, big, sd((T, 1, lanes), F32), sd((T, 1, lanes), F32),
                   sd((1, Wd), F32)],
        scratch_shapes=[pltpu.VMEM((L, lanes), F32) for _ in range(2)]
        + [pltpu.VMEM((L, _LANES), F32) for _ in range(3)] + _s5_dense_scratch(lanes),
        compiler_params=_cparams(("arbitrary",)),
    )(u, dg, states_re, states_im, bd_re, bd_im, cd_re, cd_im, lam_re, lam_im, d)


def _half_tile(R, few_arrays=False):
    for t in ((512, 704, 128) if few_arrays else (256, 352, 128)):
        if R % t == 0:
            return t
    raise ValueError(R)


def _cast_shard(name, w, layer, kind, R, C):
    tr = _half_tile(R, True)
    nr = R // tr

    def body(w_ref, o_ref):
        o_ref[...] = w_ref[...].astype(o_ref.dtype)

    if kind == "col":
        in_map = lambda h, i: (layer, h * nr + i, 0)
    else:
        in_map = lambda h, i: (layer, i, h)
    return pl.pallas_call(
        body, name=name, grid=(2, nr), in_specs=[pl.BlockSpec((None, tr, C), in_map)],
        out_specs=pl.BlockSpec((None, tr, C), lambda h, i: (h, i, 0)),
        out_shape=jax.ShapeDtypeStruct((2, R, C), _WIRE),
        compiler_params=_cparams(("arbitrary", "arbitrary")),
    )(w)


def _adam_math(w, g, m, v):
    m2 = _B1 * m + (1.0 - _B1) * g
    v2 = _B2 * v + (1.0 - _B2) * (g * g)
    m_hat = m2 / (1.0 - _B1 ** _STEP)
    v_hat = v2 / (1.0 - _B2 ** _STEP)
    delta = -_LR * (m_hat / (jnp.sqrt(v_hat) + _AEPS) + _WD * w)
    return delta, m2, v2


def _adamw_big(name, w, m, v, layer, pair, kind, R, C, c, after, prev):
    tr = _half_tile(R, few_arrays=C <= 1024 and R % 512 == 0)
    nr = R // tr

    def body(c_ref, w_ref, m_ref, v_ref, own_ref, other_ref, *rest):
        go_ref, d_ref, mo_ref, vo_ref = rest[-4:]
        g = jnp.where(pl.program_id(0) == c_ref[0], own_ref[...], other_ref[...])
        delta, m2, v2 = _adam_math(w_ref[...], g, m_ref[...], v_ref[...])
        go_ref[...] = g
        d_ref[...] = delta
        mo_ref[...] = m2
        vo_ref[...] = v2

    if kind == "col":
        nat = pl.BlockSpec((None, tr, C), lambda h, i, c_ref: (layer, h * nr + i, 0))
    else:
        nat = pl.BlockSpec((None, tr, C), lambda h, i, c_ref: (layer, i, h))

    def gspec(own):
        return pl.BlockSpec((tr, C), lambda h, i, c_ref: (jnp.where((h == c_ref[0]) == own, i, 0), 0))

    carried = list(prev) if prev is not None else []
    gs = pltpu.PrefetchScalarGridSpec(
        num_scalar_prefetch=1, grid=(2, nr),
        in_specs=[nat, nat, nat, gspec(True), gspec(False), _any()] + [_any()] * len(carried),
        out_specs=[nat] * 4)
    sd = jax.ShapeDtypeStruct(w.shape, F32)
    return pl.pallas_call(
        body, name=name, grid_spec=gs, out_shape=[sd] * 4,
        input_output_aliases={7 + k: k for k in range(len(carried))},
        compiler_params=_cparams(("arbitrary", "arbitrary")),
    )(c, w, m, v, pair[0], pair[1], after, *carried)


def _adamw_small(ws, gs, ms, vs):
    n = len(ws)

    def body(*refs):
        for i in range(n):
            w_ref, g_ref, m_ref, v_ref, d_ref, mo_ref, vo_ref = refs[i::n]
            delta, m2, v2 = _adam_math(w_ref[...], g_ref[...], m_ref[...], v_ref[...])
            d_ref[...] = delta
            mo_ref[...] = m2
            vo_ref[...] = v2

    sds = [jax.ShapeDtypeStruct(w.shape, F32) for w in ws]
    outs = pl.pallas_call(
        body, name="adamw_small", in_specs=[_vm()] * (4 * n), out_specs=[_vm()] * (3 * n), out_shape=sds * 3,
        compiler_params=pltpu.CompilerParams(vmem_limit_bytes=_VMEM_LIMIT),
    )(*ws, *gs, *ms, *vs)
    return outs[:n], outs[n:2 * n], outs[2 * n:]


def _place():
    x, y, c = lax.axis_index("x"), lax.axis_index("y"), lax.axis_index("c")
    chips = [(1 - x, y), (x, 1 - y), (1 - x, 1 - y)]
    return x, y, c, 2 * x + y, chips


def _any():
    return pl.BlockSpec(memory_space=pl.ANY)


def _remote(src, dst, ssem, rsem, dev):
    return pltpu.make_async_remote_copy(src_ref=src, dst_ref=dst, send_sem=ssem, recv_sem=rsem,
                                        device_id=dev, device_id_type=_MESH)


def _allgather(name, shards):
    n = len(shards)

    def body(*refs):
        s_refs, g_refs = refs[:n], refs[n:2 * n]
        send0, recv0, send1, recv1, send2, recv2 = refs[2 * n:]
        x, y, c, q, _ = _place()
        sib, xn, yn = (x, y, 1 - c), (1 - x, y, c), (x, 1 - y, c)
        qx, qy, qd = 2 * (1 - x) + y, 2 * x + (1 - y), 2 * (1 - x) + (1 - y)
        _handshake([sib, xn, yn])
        own = [_remote(s_refs[a], g_refs[a].at[q], send0.at[a], recv0.at[a], sib) for a in range(n)]

        def pieces(a):
            g, half = g_refs[a], s_refs[a].shape[1] // 2
            return [g.at[qx, c], g.at[qy, c], g.at[qd, c, pl.ds(0, half)], g.at[qd, c, pl.ds(half, half)]]

        def relayed(a):
            g, half = g_refs[a], s_refs[a].shape[1] // 2
            return [(g.at[qx, c, pl.ds(0, half)], yn), (g.at[qy, c, pl.ds(half, half)], xn)]

        first = []
        for a in range(n):
            first.append(_remote(s_refs[a].at[c], g_refs[a].at[q, c], send1.at[4 * a], recv1.at[4 * a], xn))
            first.append(_remote(s_refs[a].at[c], g_refs[a].at[q, c], send1.at[4 * a + 1], recv1.at[4 * a + 1], yn))
        for cp in first + own:
            cp.start()
        later = []
        for a in range(n):
            land = pieces(a)
            for j in range(4):
                k = 4 * a + j
                _remote(land[j], land[j], send1.at[k], recv1.at[k], xn).wait_recv()
                if j < 2:
                    src, to = relayed(a)[j]
                    cp = _remote(src, src, send1.at[k + 2], recv1.at[k + 2], to)
                    cp.start()
                    later.append(cp)
                cp = _remote(land[j], land[j], send2.at[k], recv2.at[k], sib)
                cp.start()
                later.append(cp)
        for a in range(n):
            g, half = g_refs[a], s_refs[a].shape[1] // 2
            theirs = [g.at[qx, 1 - c], g.at[qy, 1 - c], g.at[qd, 1 - c, pl.ds(0, half)],
                      g.at[qd, 1 - c, pl.ds(half, half)]]
            for j in range(4):
                _remote(theirs[j], theirs[j], send2.at[4 * a + j], recv2.at[4 * a + j], sib).wait_recv()
        for cp in own:
            cp.wait()
        for cp in first + later:
            cp.wait_send()

    return _sequencer(name, _ID_GATHER, body, shards,
                      [jax.ShapeDtypeStruct((4,) + s.shape, s.dtype) for s in shards], [n, n] + [4 * n] * 4)


def _handshake(peers):
    barrier = pltpu.get_barrier_semaphore()
    for peer in peers:
        pl.semaphore_signal(barrier, inc=1, device_id=peer, device_id_type=_MESH)
    pl.semaphore_wait(barrier, len(peers))


_ID_SIBLING, _ID_CHIPS, _ID_GATHER, _ID_ALL = 1, 2, 3, 4


def _sequencer(name, collective_id, body, ins, out_types, sem_counts):
    mesh = plsc.ScalarSubcoreMesh(axis_name="seq", num_cores=1)
    moved = sum(math.prod(o.shape) * jnp.dtype(o.dtype).itemsize for o in out_types)
    return pl.kernel(
        body, name=name, out_type=out_types, mesh=mesh,
        scratch_types=[pltpu.SemaphoreType.DMA((k,)) for k in sem_counts],
        compiler_params=pltpu.CompilerParams(collective_id=collective_id),
        cost_estimate=pl.CostEstimate(flops=0, transcendentals=0, bytes_accessed=2 * moved,
                                      remote_bytes_transferred=moved),
    )(*ins)


def _swap_halves(name, grads):
    n = len(grads)

    def body(*refs):
        g_refs, t_refs = refs[:n], refs[n:2 * n]
        send, recv = refs[2 * n:]
        x, y, c, _, _ = _place()
        _handshake([(x, y, 1 - c)])
        cps = [_remote(g_refs[a].at[1 - c], t_refs[a], send.at[a], recv.at[a], (x, y, 1 - c)) for a in range(n)]
        for cp in cps:
            cp.start()
        for cp in cps:
            cp.wait()

    return _sequencer(name, _ID_SIBLING, body, grads,
                      [jax.ShapeDtypeStruct(g.shape[1:], g.dtype) for g in grads], [n, n])


def _chip_sum(name, grads, swapped, after, core):
    n = len(grads)
    _, _, R, C = grads[0].shape
    tr = _half_tile(R, True)

    def body(c_ref, *refs):
        for g_ref, t_ref, o_ref in zip(refs[:n], refs[n:2 * n], refs[2 * n + 1:]):
            o_ref[...] = (g_ref[...].astype(F32) + t_ref[...].astype(F32)).astype(o_ref.dtype)

    own = pl.BlockSpec((None, None, tr, C), lambda r, i, c_ref: (c_ref[0], r, i, 0))
    part = pl.BlockSpec((None, tr, C), lambda r, i, c_ref: (r, i, 0))
    gs = pltpu.PrefetchScalarGridSpec(
        num_scalar_prefetch=1, grid=(4, R // tr), in_specs=[own] * n + [part] * n + [_any()], out_specs=[part] * n)
    return pl.pallas_call(
        body, name=name, grid_spec=gs, out_shape=[jax.ShapeDtypeStruct((4, R, C), _WIRE)] * n,
        compiler_params=_cparams(("arbitrary", "arbitrary")),
    )(core, *grads, *swapped, after)


def _scatter_parts(name, parts):
    n = len(parts)

    def body(*refs):
        p_refs, t_refs = refs[:n], refs[n:2 * n]
        send, recv = refs[2 * n:]
        x, y, c, q, chips = _place()
        _handshake([(rx, ry, c) for rx, ry in chips])
        cps = []
        for a in range(n):
            for j, (rx, ry) in enumerate(chips):
                k = 3 * a + j
                cps.append(_remote(p_refs[a].at[2 * rx + ry], t_refs[a].at[q], send.at[k], recv.at[k], (rx, ry, c)))
        for cp in cps:
            cp.start()
        for a in range(n):
            for j, (rx, ry) in enumerate(chips):
                k = 3 * a + j
                land = t_refs[a].at[2 * rx + ry]
                _remote(land, land, send.at[k], recv.at[k], (rx, ry, c)).wait_recv()
        for cp in cps:
            cp.wait_send()

    return _sequencer(name, _ID_CHIPS, body, parts,
                      [jax.ShapeDtypeStruct(p.shape, p.dtype) for p in parts], [3 * n, 3 * n])


def _sum_parts(name, parts, landed, where, after):
    n = len(parts)
    _, R, C = parts[0].shape
    tr = _half_tile(R, True)

    def body(w_ref, *refs):
        for a in range(n):
            p_ref, t0_ref, t1_ref, t2_ref = refs[4 * a:4 * a + 4]
            refs[4 * n + 1 + a][...] = (p_ref[...].astype(F32) + t0_ref[...].astype(F32)
                                        + t1_ref[...].astype(F32) + t2_ref[...].astype(F32))

    def part(slot):
        return pl.BlockSpec((None, tr, C), lambda i, w_ref: (w_ref[slot], i, 0))

    gs = pltpu.PrefetchScalarGridSpec(
        num_scalar_prefetch=1, grid=(R // tr,), in_specs=[part(0), part(1), part(2), part(3)] * n + [_any()],
        out_specs=[pl.BlockSpec((tr, C), lambda i, w_ref: (i, 0))] * n)
    ops = [x for p, t in zip(parts, landed) for x in (p, t, t, t)]
    return pl.pallas_call(
        body, name=name, grid_spec=gs, out_shape=[jax.ShapeDtypeStruct((R, C), F32)] * n,
        compiler_params=_cparams(("arbitrary",)),
    )(where, *ops, after)


def _send_halves(name, halves):
    n = len(halves)

    def body(*refs):
        h_refs, o_refs = refs[:n], refs[n:2 * n]
        send, recv = refs[2 * n:]
        x, y, c, _, _ = _place()
        _handshake([(x, y, 1 - c)])
        cps = [_remote(h_refs[a], o_refs[a], send.at[a], recv.at[a], (x, y, 1 - c)) for a in range(n)]
        for cp in cps:
            cp.start()
        for cp in cps:
            cp.wait()

    return _sequencer(name, _ID_SIBLING, body, halves,
                      [jax.ShapeDtypeStruct(h.shape, h.dtype) for h in halves], [n, n])


class _Order:
    def __init__(self):
        self.tok = None
        x, y, c, q, chips = _place()
        self.core = c.astype(jnp.int32).reshape(1)
        self.where = jnp.stack([q] + [2 * rx + ry for rx, ry in chips]).astype(jnp.int32)

    def tie(self, x):
        return x if self.tok is None else lax.optimization_barrier((x, self.tok))[0]

    def done(self, outs):
        self.tok = outs[0]
        return outs


class _Reduction:
    def __init__(self, tag, grads):
        self.tag, self.grads = tag, grads
        self.swapped = _swap_halves(f"rs_swap_{tag}", grads)

    def _by_shape(self):
        groups = {}
        for a, g in enumerate(self.grads):
            groups.setdefault(g.shape, []).append(a)
        return list(groups.values())

    def scatter(self, seq):
        self.parts = [None] * len(self.grads)
        for idx in self._by_shape():
            outs = seq.done(_chip_sum(f"rs_chipsum_{self.tag}_{idx[0]}", [self.grads[a] for a in idx],
                                      [self.swapped[a] for a in idx], seq.tok, seq.core))
            for a, o in zip(idx, outs):
                self.parts[a] = o
        self.landed = _scatter_parts(f"rs_scatter_{self.tag}", self.parts)

    def finish(self, seq):
        halves = [None] * len(self.grads)
        for idx in self._by_shape():
            outs = seq.done(_sum_parts(f"rs_sum_{self.tag}_{idx[0]}", [self.parts[a] for a in idx],
                                       [self.landed[a] for a in idx], seq.where, seq.tok))
            for a, o in zip(idx, outs):
                halves[a] = o
        return list(zip(halves, _send_halves(f"rs_join_{self.tag}", halves)))


def _allreduce_small(name, v):
    _, R, _ = v.shape

    def body(v_ref, o_ref, land, acc, send1, recv1, send2, recv2):
        x, y, c = lax.axis_index("x"), lax.axis_index("y"), lax.axis_index("c")
        me = 4 * x + 2 * y + c
        peers = []
        for k in range(1, 8):
            dx, dy, dc = (k >> 2) & 1, (k >> 1) & 1, k & 1
            px, py, pc = (1 - x if dx else x), (1 - y if dy else y), (1 - c if dc else c)
            peers.append((k, (px, py, pc), 4 * px + 2 * py + pc))
        land[me] = v_ref[me]
        out1 = [_remote(v_ref.at[pid], land.at[me], send1.at[k], recv1.at[k], dev) for k, dev, pid in peers]
        for cp in out1:
            cp.start()
        for k, dev, pid in peers:
            _remote(land.at[pid], land.at[pid], send1.at[k], recv1.at[k], dev).wait_recv()
        total = land[0]
        for j in range(1, 8):
            total = total + land[j]
        acc[...] = total
        o_ref[me] = total
        out2 = [_remote(acc, o_ref.at[me], send2.at[k], recv2.at[k], dev) for k, dev, pid in peers]
        for cp in out2:
            cp.start()
        for k, dev, pid in peers:
            _remote(o_ref.at[pid], o_ref.at[pid], send2.at[k], recv2.at[k], dev).wait_recv()
        for cp in out1 + out2:
            cp.wait_send()

    return pl.pallas_call(
        body, name=name, in_specs=[_vm()], out_specs=_vm(),
        out_shape=jax.ShapeDtypeStruct(v.shape, F32),
        scratch_shapes=[pltpu.VMEM(v.shape, F32), pltpu.VMEM((R, _LANES), F32)]
        + [pltpu.SemaphoreType.DMA((8,)) for _ in range(4)],
        compiler_params=pltpu.CompilerParams(vmem_limit_bytes=_VMEM_LIMIT),
    )(v)


def _all_peers():
    x, y, c = lax.axis_index("x"), lax.axis_index("y"), lax.axis_index("c")
    peers = []
    for k in range(1, 8):
        px, py, pc = (1 - x if k & 4 else x), (1 - y if k & 2 else y), (1 - c if k & 1 else c)
        peers.append((k, (px, py, pc), 4 * px + 2 * py + pc))
    return 4 * x + 2 * y + c, peers


def _exchange_slices(name, v):
    def body(v_ref, land, send, recv):
        me, peers = _all_peers()
        _handshake([dev for _, dev, _ in peers])
        cps = [_remote(v_ref.at[pid], land.at[me], send.at[k], recv.at[k], dev) for k, dev, pid in peers]
        for cp in cps:
            cp.start()
        for k, dev, pid in peers:
            _remote(land.at[pid], land.at[pid], send.at[k], recv.at[k], dev).wait_recv()
        for cp in cps:
            cp.wait_send()

    return _sequencer(name, _ID_ALL, body, [v], [jax.ShapeDtypeStruct(v.shape, v.dtype)], [8, 8])[0]


def _sum_slices(name, v, landed, after):
    _, R, _ = v.shape

    def body(v_ref, land_ref, after_ref, o_ref):
        me, peers = _all_peers()
        acc = v_ref[me]
        for _, _, pid in peers:
            acc = acc + land_ref[pid]
        o_ref[...] = acc

    return pl.pallas_call(
        body, name=name, in_specs=[_vm(), _vm(), _any()], out_specs=_vm(),
        out_shape=jax.ShapeDtypeStruct((R, _LANES), F32),
        compiler_params=pltpu.CompilerParams(vmem_limit_bytes=_VMEM_LIMIT),
    )(v, landed, after)


def _broadcast_slices(name, s):
    def body(s_ref, out, send, recv):
        me, peers = _all_peers()
        _handshake([dev for _, dev, _ in peers])
        cps = [_remote(s_ref, out.at[me], send.at[k], recv.at[k], dev) for k, dev, pid in peers]
        for cp in cps:
            cp.start()
        for k, dev, pid in peers:
            _remote(out.at[pid], out.at[pid], send.at[k], recv.at[k], dev).wait_recv()
        for cp in cps:
            cp.wait_send()

    return _sequencer(name, _ID_ALL, body, [s], [jax.ShapeDtypeStruct((8,) + s.shape, s.dtype)], [8, 8])[0]


_WEIGHT_NAMES = ['norm_mix', 'norm_ffn', 'norm_ple', 'norm_final', 'gm_w_in', 'gm_ln_g', 'gm_ln_b', 'gm_w_s',
                 'gm_b_s', 'gm_w_out', 's5_w_in', 's5_a_re', 's5_a_im', 's5_log_dt', 's5_b_re', 's5_b_im',
                 's5_c_re', 's5_c_im', 's5_d', 's5_w_out', 'ffn_w1', 'ffn_w3', 'ffn_w2', 'ple_w_gate', 'ple_w_proj']
_BIG = {'gm_w_in': 'col', 'gm_w_out': 'row', 's5_w_in': 'row', 's5_w_out': 'col', 'ffn_w1': 'col',
        'ffn_w3': 'col', 'ffn_w2': 'row', 'ple_w_gate': 'row', 'ple_w_proj': 'col'}


_VIEW = {'s5_a_re': (0, 2, 1), 's5_a_im': (0, 2, 1), 's5_b_re': (0, 2, 3, 1), 's5_b_im': (0, 2, 3, 1),
         's5_c_re': (0, 2, 3, 1), 's5_c_im': (0, 2, 3, 1)}


def _to_view(name, a):
    return jnp.transpose(a, _VIEW[name]) if name in _VIEW else a


def _from_view(name, a):
    if name not in _VIEW:
        return a
    perm = _VIEW[name]
    return jnp.transpose(a, [perm.index(i) for i in range(len(perm))])


def _rc(kind, shard_shape):
    rows, cols = shard_shape[-2:]
    return (rows // 2, cols) if kind == "col" else (rows, cols // 2)


def _pack(vecs, rows_multiple):
    flat = jnp.concatenate([a.reshape(-1).astype(F32) for a in vecs])
    unit = rows_multiple * _LANES
    pad = (-flat.shape[0]) % unit
    return jnp.pad(flat, (0, pad)).reshape(-1, _LANES)


def _unpack(buf, shapes):
    flat = buf.reshape(-1)
    out, off = [], 0
    for s in shapes:
        n = math.prod(s)
        out.append(flat[off:off + n].reshape(s))
        off += n
    return out


def _ident(accs, ex):
    return accs


def _add_resid(accs, ex):
    return [accs[0] + ex[0]]


def _swiglu_epi(accs, ex):
    a, b = accs
    return [a, b, a * _sig(a) * b]


def _swiglu_bwd_epi(accs, ex):
    df = accs[0]
    a, b = ex[0].astype(F32), ex[1].astype(F32)
    sa = _sig(a)
    return [df * b * (sa * (1.0 + a * (1.0 - sa))), df * (a * sa)]


def _ple_epi(accs, ex):
    xin, pv = ex[0], ex[1]
    kh = pv.shape[1] // 2
    pp = jnp.concatenate([_dot(pv[:, :kh], ex[2 + 2 * part], _NN) + _dot(pv[:, kh:], ex[3 + 2 * part], _NN)
                          for part in (0, 1)], axis=1)
    gt = _sig(accs[0])
    return [xin + gt * pp, gt, pp]


def _glu_epi(accs, ex):
    val, sg = accs[0], _sig(accs[1])
    return [ex[0] + val * sg, val, sg]


def kernel(x, p, norm_mix, norm_ffn, norm_ple, norm_final, gm_w_in, gm_ln_g, gm_ln_b, gm_w_s, gm_b_s, gm_w_out, s5_w_in, s5_a_re, s5_a_im, s5_log_dt, s5_b_re, s5_b_im, s5_c_re, s5_c_im, s5_d, s5_w_out, ffn_w1, ffn_w3, ffn_w2, ple_w_gate, ple_w_proj, loss_target, m_norm_mix, m_norm_ffn, m_norm_ple, m_norm_final, m_gm_w_in, m_gm_ln_g, m_gm_ln_b, m_gm_w_s, m_gm_b_s, m_gm_w_out, m_s5_w_in, m_s5_a_re, m_s5_a_im, m_s5_log_dt, m_s5_b_re, m_s5_b_im, m_s5_c_re, m_s5_c_im, m_s5_d, m_s5_w_out, m_ffn_w1, m_ffn_w3, m_ffn_w2, m_ple_w_gate, m_ple_w_proj, v_norm_mix, v_norm_ffn, v_norm_ple, v_norm_final, v_gm_w_in, v_gm_ln_g, v_gm_ln_b, v_gm_w_s, v_gm_b_s, v_gm_w_out, v_s5_w_in, v_s5_a_re, v_s5_a_im, v_s5_log_dt, v_s5_b_re, v_s5_b_im, v_s5_c_re, v_s5_c_im, v_s5_d, v_s5_w_out, v_ffn_w1, v_ffn_w3, v_ffn_w2, v_ple_w_gate, v_ple_w_proj):
    env = dict(locals())
    w = {n: env[n] for n in _WEIGHT_NAMES}
    mom = {n: env["m_" + n] for n in _WEIGHT_NAMES}
    var = {n: env["v_" + n] for n in _WEIGHT_NAMES}
    xs, tgt = x[0], loss_target[0]
    L, D = xs.shape
    depth = norm_mix.shape[0]
    qx, qy = lax.axis_index("x"), lax.axis_index("y")
    q = 2 * qx + qy

    W, last_cast = {}, [None]

    def gather(tag, items, after=None):
        shards = []
        for name, layer in items:
            kind = _BIG[name]
            R, C = _rc(kind, w[name].shape)
            src = w[name] if after is None else lax.optimization_barrier((w[name], after))[0]
            shards.append(_cast_shard(f"cast_{name}{layer}", src, layer, kind, R, C))
        last_cast[0] = shards[-1]
        full = _allgather(f"ag_{tag}", shards)
        W.update({it: _W(f, _BIG[it[0]]) for it, f in zip(items, full)})

    gather("gm_w_in", [("gm_w_in", 0)])
    gather("gm_w_out", [("gm_w_out", 0)])
    gather("ffn_up0", [("ffn_w1", 0), ("ffn_w3", 0)])
    gather("ffn_down0", [("ffn_w2", 0)])

    def gather_rest(after):
        gather("ple0", [("ple_w_gate", 0), ("ple_w_proj", 0)], after)
        gather("s5_w_in", [("s5_w_in", 0)], after)
        gather("s5_w_out", [("s5_w_out", 0)], after)
        gather("ffn_up1", [("ffn_w1", 1), ("ffn_w3", 1)], after)
        gather("ffn_down1", [("ffn_w2", 1)], after)
        gather("ple1", [("ple_w_gate", 1), ("ple_w_proj", 1)], after)

    d_slots = jnp.zeros((4, D // 4), F32)
    d_slots = lax.dynamic_update_slice(d_slots, s5_d.astype(F32), (q, 0))
    d_sum = _allreduce_small("ar_s5_d", _pack([d_slots], 64).reshape(8, -1, _LANES))
    d_full = (d_sum.reshape(-1)[:D] * 0.5).reshape(1, D)

    def ffn_fwd(i, xin):
        hf = _rms_fwd(f"rms_ffn{i}", xin, norm_ffn[i:i + 1])
        a, b, f = _mm_nn(f"ffn_up{i}", hf, [W["ffn_w1", i], W["ffn_w3", i]], 1024, 1408, ffn_w2.shape[1] * 4,
                         [], [_MXU, _MXU, _MXU], _swiglu_epi, tm=1024)
        xo = _mm_nn(f"ffn_down{i}", f, [W["ffn_w2", i]], 1408, 1024, D, [xin], [F32], _add_resid, tm=1024)[0]
        return xo, (xin, hf, a, b, f)

    def ple_fwd(i, xin):
        hp = _rms_fwd(f"rms_ple{i}", xin, norm_ple[i:i + 1])
        pi, wp = p[i, 0], W["ple_w_proj", i]

        def more(tm):
            ops = [(pi, pl.BlockSpec((tm, pi.shape[1]), lambda i_, j, k: (i_, 0)))]
            for part in (0, 1):
                for half in (0, 1):
                    ops.append((wp.a, pl.BlockSpec((None, None, wp.R, wp.C),
                                                   lambda i_, j, k, part=part, half=half: (2 * j + part, half, 0, 0))))
            return ops

        xo, gt, pp = _mm_nn(f"ple_gate{i}", hp, [W["ple_w_gate", i]], 512, 2 * wp.C, D, [xin], [F32, _MXU, F32],
                            _ple_epi, tm=1024, more=more)
        return xo, (xin, hp, pi, pp, gt)

    h0 = _rms_fwd("rms_mix0", xs, norm_mix[0:1])
    z = _mm_nn("gm_in", h0, [W["gm_w_in", 0]], 1024, 1024, 2 * D, [], [F32], _ident, tm=1024)[0]
    bsT = gm_b_s[0].T
    gm_m = _gmlp_fwd(z, gm_ln_g, gm_ln_b, gm_w_s[0], bsT)
    x1 = _mm_nn("gm_out", gm_m, [W["gm_w_out", 0]], 512, 1024, D, [xs], [F32], _add_resid, tm=2048)[0]
    gather_rest(x1)
    x1 = lax.optimization_barrier((x1, last_cast[0]))[0]
    x2, ffn0 = ffn_fwd(0, x1)
    x3, ple0 = ple_fwd(0, x2)

    T = D // _LANES
    lanes = _S5_GT * _S5_P
    sv = {n: _to_view(n, w[n])[0] for n in _VIEW}
    a_re, a_im, log_dt = sv["s5_a_re"], sv["s5_a_im"], s5_log_dt
    lbr, lbi, Bbar_re, Bbar_im = _s5_prep(a_re, a_im, log_dt, sv["s5_b_re"], sv["s5_b_im"])

    def to_bd(B):
        return jnp.transpose(B.reshape(_S5_P, _S5_C, T, _S5_GT), (2, 3, 1, 0))

    def to_cd(cw):
        return jnp.transpose(cw.reshape(_S5_C, _S5_P, T, _S5_GT), (2, 3, 1, 0))

    def to_lam(v):
        return jnp.transpose(v).reshape(T, 1, lanes)

    bd_re, bd_im = to_bd(Bbar_re), to_bd(Bbar_im)
    cd_re, cd_im = to_cd(sv["s5_c_re"]), to_cd(sv["s5_c_im"])
    lam_re, lam_im = to_lam(lbr), to_lam(lbi)

    h1 = _rms_fwd("rms_mix1", x3, norm_mix[1:2])
    u = _mm_nn("s5_in", h1, [W["s5_w_in", 0]], 512, 1024, D, [], [F32], _ident, tm=2048)[0]
    s5_g, s5_re, s5_im = _s5_fwd(u, bd_re, bd_im, cd_re, cd_im, lam_re, lam_im, d_full)
    x4, glu_val, glu_sg = _mm_nn("s5_out", s5_g, [W["s5_w_out", 0], W["s5_w_out", 0]], 1024, 1024, D, [x3],
                                 [F32, _MXU, _MXU], _glu_epi, tm=1024, cb_offsets=[0, 2])
    x5, ffn1 = ffn_fwd(1, x4)
    x6, ple1 = ple_fwd(1, x5)

    dx, dpre1, dpp1, d_norm_final, loss_rows = _loss_head(x6, norm_final[None], tgt, ple1[3], ple1[4])

    small = {}

    seq = _Order()
    tie, done = seq.tie, seq.done
    d_norm_ple, d_norm_ffn, d_norm_mix = [None] * depth, [None] * depth, [None] * depth
    reduced = {}

    def keep(names, layer, pairs):
        for n, pr in zip(names, pairs):
            reduced[n, layer] = pr

    ple_names, up_names, down_names = ["ple_w_gate", "ple_w_proj"], ["ffn_w1", "ffn_w3"], ["ffn_w2"]

    def ple_bwd(i, dxo, saved, elem=None):
        xin, hp, pi, pp, gt = saved
        dpre, dpp = elem or done(_ple_bwd_elem(tie(dxo), pp, gt))
        dwg = done(_mm_tn(f"ple_gate_dw{i}", tie(hp), [dpre], "row", 512, 1024, 512, 1024))[0]
        dwp = done(_mm_tn(f"ple_proj_dw{i}", tie(pi), [dpp], "col", 128, 512, 128, 512))[0]
        red = _Reduction(f"ple{i}", [dwg, dwp])
        dhp = done(_mm_nt(f"ple_gate_dx{i}", [tie(dpre)], [W["ple_w_gate", i]], 512, 1024, [], [F32], _ident,
                          tm=2048))[0]
        dxin, dxin_mxu, dg = done(_rms_bwd(f"rms_ple_bwd{i}", tie(dhp), xin, norm_ple[i:i + 1], dxo))
        return dxin, dxin_mxu, dg, red

    def ffn_bwd(i, dxo, dxo_mxu, saved, before_up):
        xin, hf, a, b, f = saved
        dw2 = done(_mm_tn(f"ffn_down_dw{i}", tie(f), [dxo_mxu], "row", 1408, 1024, 1408, 1024))[0]
        r_down = _Reduction(f"ffd{i}", [dw2])
        da, db = done(_mm_nt(f"ffn_down_dx{i}", [tie(dxo_mxu)], [W["ffn_w2", i]], 1408, 1024, [a, b], [_MXU, _MXU],
                             _swiglu_bwd_epi, tm=1024))
        for step in before_up:
            step()
        r_down.scatter(seq)
        dw1, dw3 = done(_mm_tn(f"ffn_up_dw{i}", tie(hf), [da, db], "col", 1024, 1408, 1024, 1408))
        r_up = _Reduction(f"ffu{i}", [dw1, dw3])
        dhf = done(_mm_nt(f"ffn_up_dx{i}", [tie(da), db], [W["ffn_w1", i], W["ffn_w3", i]], 1024, 1408, [], [F32],
                          _ident, tm=1024))[0]
        dxin, dxin_mxu, dg = done(_rms_bwd(f"rms_ffn_bwd{i}", tie(dhf), xin, norm_ffn[i:i + 1], dxo))
        r_up.scatter(seq)
        return dxin, dxin_mxu, dg, r_down, r_up

    dx, dx_mxu, d_norm_ple[1], r_ple1 = ple_bwd(1, dx, ple1, (dpre1, dpp1))
    dx, _, d_norm_ffn[1], r_down1, r_up1 = ffn_bwd(1, dx, dx_mxu, ffn1, [lambda: r_ple1.scatter(seq)])

    do = done([_glu_bwd_elem(tie(dx), glu_val, glu_sg)])[0]
    dw_s5out = done(_mm_tn("s5_out_dw", tie(s5_g), [do], "col", 1024, 1024, 1024, 1024))[0]
    r_s5out = _Reduction("s5out", [dw_s5out])
    dgy = done(_mm_nt("s5_out_dx", [tie(do)], [W["s5_w_out", 0]], 1024, 1024, [], [F32], _ident, tm=1024))[0]
    keep(ple_names, 1, r_ple1.finish(seq))
    keep(down_names, 1, r_down1.finish(seq))
    du, dbd_re, dbd_im, dcd_re, dcd_im, dl_re, dl_im, dd = done(_s5_bwd(
        tie(u), dgy, s5_re, s5_im, bd_re, bd_im, cd_re, cd_im, lam_re, lam_im, d_full))
    r_s5out.scatter(seq)
    dw_s5in = done(_mm_tn("s5_in_dw", tie(h1), [du], "row", 512, 1024, 512, 1024))[0]
    r_s5in = _Reduction("s5in", [dw_s5in])
    dh1 = done(_mm_nt("s5_in_dx", [tie(du)], [W["s5_w_in", 0]], 512, 1024, [], [F32], _ident, tm=2048))[0]
    dx, _, d_norm_mix[1] = done(_rms_bwd("rms_mix1_bwd", tie(dh1), x3, norm_mix[1:2], dx))
    keep(up_names, 1, r_up1.finish(seq))
    r_s5in.scatter(seq)

    def from_bd(t):
        return jnp.transpose(t, (3, 2, 0, 1)).reshape(_S5_P, _S5_C, T * _S5_GT)

    def from_cdT(t):
        return jnp.transpose(t, (2, 3, 0, 1)).reshape(_S5_C, _S5_P, T * _S5_GT)

    def from_lam(t):
        return jnp.transpose(t.reshape(T * _S5_GT, _S5_P))

    da_re, da_im, dlog_dt, db_re, db_im = _s5_prep_bwd(
        a_re, a_im, log_dt, sv["s5_b_re"], sv["s5_b_im"], from_lam(dl_re), from_lam(dl_im),
        from_bd(dbd_re), from_bd(dbd_im))
    small["s5_a_re"], small["s5_a_im"], small["s5_log_dt"] = da_re[None], da_im[None], dlog_dt
    small["s5_b_re"], small["s5_b_im"] = db_re[None], db_im[None]
    small["s5_c_re"], small["s5_c_im"] = from_cdT(dcd_re)[None], from_cdT(dcd_im)[None]

    dx, dx_mxu, d_norm_ple[0], r_ple0 = ple_bwd(0, dx, ple0)
    keep(["s5_w_out"], 0, r_s5out.finish(seq))
    xin0, hf0, a0, b0, f0 = ffn0
    da0, db0 = done(_mm_nt("ffn_down_dx0", [tie(dx_mxu)], [W["ffn_w2", 0]], 1408, 1024, [a0, b0], [_MXU, _MXU],
                           _swiglu_bwd_epi, tm=1024))
    r_ple0.scatter(seq)
    dw1, dw3 = done(_mm_tn("ffn_up_dw0", tie(hf0), [da0, db0], "col", 1024, 1408, 1024, 1408))
    r_up0 = _Reduction("ffu0", [dw1, dw3])
    keep(["s5_w_in"], 0, r_s5in.finish(seq))
    dw2 = done(_mm_tn("ffn_down_dw0", tie(f0), [dx_mxu], "row", 1408, 1024, 1408, 1024))[0]
    r_down0 = _Reduction("ffd0", [dw2])
    r_up0.scatter(seq)
    dhf0 = done(_mm_nt("ffn_up_dx0", [tie(da0), db0], [W["ffn_w1", 0], W["ffn_w3", 0]], 1024, 1408, [], [F32], _ident,
                       tm=1024))[0]
    dx, dx_mxu, d_norm_ffn[0] = done(_rms_bwd("rms_ffn_bwd0", tie(dhf0), xin0, norm_ffn[0:1], dx))
    keep(ple_names, 0, r_ple0.finish(seq))
    r_down0.scatter(seq)

    dw_gmout = done(_mm_tn("gm_out_dw", tie(gm_m), [dx_mxu], "row", 512, 1024, 512, 1024))[0]
    r_gmout = _Reduction("gmout", [dw_gmout])
    dgm = done(_mm_nt("gm_out_dx", [tie(dx_mxu)], [W["gm_w_out", 0]], 512, 1024, [], [F32], _ident, tm=2048))[0]
    dz, dws, dbsT, dlng, dlnb = done(_gmlp_bwd(tie(z), dgm, gm_ln_g, gm_ln_b, gm_w_s[0], bsT))
    dw_gmin = done(_mm_tn("gm_in_dw", tie(h0), [dz], "col", 1024, 1024, 1024, 1024))[0]
    r_gmin = _Reduction("gmin", [dw_gmin])
    dh0 = done(_mm_nt("gm_in_dx", [tie(dz)], [W["gm_w_in", 0]], 1024, 1024, [], [F32], _ident, tm=1024))[0]
    dx, _, d_norm_mix[0] = done(_rms_bwd("rms_mix0_bwd", tie(dh0), xs, norm_mix[0:1], dx))
    grad_x = dx[None]

    small["norm_mix"], small["norm_ffn"] = jnp.concatenate(d_norm_mix), jnp.concatenate(d_norm_ffn)
    small["norm_ple"], small["norm_final"] = jnp.concatenate(d_norm_ple), d_norm_final[0]
    small["gm_ln_g"], small["gm_ln_b"], small["gm_w_s"] = dlng, dlnb, dws[None]
    small["gm_b_s"] = dbsT[:, :_GM_HEADS].T[None]
    small["s5_d"] = dd

    small_names = [n for n in _WEIGHT_NAMES if n not in _BIG]
    packed = _pack([small[n] for n in small_names] + [loss_rows[:, :1]], 64).reshape(8, -1, _LANES)
    grads, deltas, new_m, new_v = {}, {}, {}, {}
    my_c = lax.axis_index("c")

    def adamw(n, layer, prev):
        kind = _BIG[n]
        R, C = _rc(kind, w[n].shape)
        return done(_adamw_big(f"adamw_{n}{layer}", w[n], mom[n], var[n], layer, reduced[n, layer], kind, R, C,
                               seq.core, seq.tok, prev))

    def adamw_last(names):
        for n in names:
            grads[n], deltas[n], new_m[n], new_v[n] = adamw(n, 0, late.get(n))

    late = {}
    for n in down_names + ple_names:
        late[n] = adamw(n, 1, None)
    keep(up_names, 0, r_up0.finish(seq))
    r_gmout.scatter(seq)
    r_gmin.scatter(seq)
    landed = _exchange_slices("ar_small_in", packed)
    for n in up_names:
        late[n] = adamw(n, 1, None)
    adamw_last(["s5_w_in", "s5_w_out"] + ple_names)
    keep(down_names, 0, r_down0.finish(seq))
    mine = done([_sum_slices("ar_small_sum", packed, landed, seq.tok)])[0]
    spread = _broadcast_slices("ar_small_out", mine)
    adamw_last(up_names)
    keep(["gm_w_out"], 0, r_gmout.finish(seq))
    adamw_last(down_names)

    spread = lax.optimization_barrier((spread, seq.tok))[0]
    summed = lax.dynamic_update_slice(spread, mine[None], (4 * qx + 2 * qy + my_c, 0, 0))
    *red_list, loss_sum = _unpack(summed, [small[n].shape for n in small_names] + [(1, 1)])
    red_small, loss = dict(zip(small_names, red_list)), loss_sum.reshape(())
    red_small["s5_d"] = lax.dynamic_slice(red_small["s5_d"], (0, q * (D // 4)), (1, D // 4))

    def two_d(a):
        return a.reshape((1,) * (2 - a.ndim) + a.shape)

    def views(src):
        return [two_d(_to_view(n, src[n])) for n in small_names]

    g_views = [two_d(red_small[n]) for n in small_names]
    w_views = views(w)
    dl, mo, vo = _adamw_small([tie(w_views[0])] + w_views[1:], g_views, views(mom), views(var))
    done(dl)
    for n, g_, d_, m_, v_ in zip(small_names, g_views, dl, mo, vo):
        grads[n], deltas[n], new_m[n], new_v[n] = [_from_view(n, t_.reshape(_to_view(n, w[n]).shape)).reshape(w[n].shape)
                                                  for t_ in (g_, d_, m_, v_)]

    keep(["gm_w_in"], 0, r_gmin.finish(seq))
    adamw_last(["gm_w_out", "gm_w_in"])

    return (loss, grad_x, *[grads[n] for n in _WEIGHT_NAMES], *[deltas[n] for n in _WEIGHT_NAMES],
            *[new_m[n] for n in _WEIGHT_NAMES], *[new_v[n] for n in _WEIGHT_NAMES])
```

```python
import math

import jax
import jax.numpy as jnp
from jax import lax
from jax.experimental import pallas as pl
from jax.experimental.pallas import tpu as pltpu
from jax.experimental.pallas import tpu_sc as plsc

F32 = jnp.float32
_MXU = jnp.bfloat16
_WIRE = jnp.bfloat16
_EPS = 1e-6
_VMEM_LIMIT = 56 * 1024 * 1024
_LANES = 128
_MESH = pl.DeviceIdType.MESH

_LR, _B1, _B2, _AEPS, _WD, _STEP = 0.001, 0.9, 0.999, 1e-08, 0.01, 10

_GM_CHUNK = 128
_GM_HEADS = 16
_S5_GT = 8
_S5_P = 64
_S5_C = 16

_NN = (((1,), (0,)), ((), ()))
_NT = (((1,), (1,)), ((), ()))
_TN = (((0,), (0,)), ((), ()))


def _cparams(sem):
    return pltpu.CompilerParams(dimension_semantics=sem, vmem_limit_bytes=_VMEM_LIMIT)


def _sig(x):
    return 0.5 * jnp.tanh(0.5 * x) + 0.5


_GC = math.sqrt(2.0 / math.pi)


def _gelu(x):
    return 0.5 * x * (1.0 + jnp.tanh(_GC * (x + 0.044715 * (x * x * x))))


def _gelu_grad(x):
    t = jnp.tanh(_GC * (x + 0.044715 * (x * x * x)))
    return 0.5 * (1.0 + t) + 0.5 * x * (1.0 - t * t) * (_GC * (1.0 + 3.0 * 0.044715 * x * x))


def _dot(a, b, dn):
    return lax.dot_general(a.astype(_MXU), b.astype(_MXU), dn, preferred_element_type=F32)


class _W:
    def __init__(self, arr, kind):
        self.a, self.kind = arr, kind
        self.R, self.C = arr.shape[2], arr.shape[3]

    def full_shape(self):
        return (2 * self.R, 4 * self.C) if self.kind == "col" else (4 * self.R, 2 * self.C)


def _part_index(kind, R, C, tr, tc, rb, cb):
    nr, nc = R // tr, C // tc
    if kind == "col":
        return cb // nc, rb // nr, rb % nr, cb % nc
    return rb // nr, cb // nc, rb % nr, cb % nc


def _wspec(w, tr, tc, rb_fn, cb_fn):
    assert w.R % tr == 0 and w.C % tc == 0, (w.R, w.C, tr, tc)

    def imap(i, j, k):
        return _part_index(w.kind, w.R, w.C, tr, tc, rb_fn(i, j, k), cb_fn(i, j, k))

    return pl.BlockSpec((None, None, tr, tc), imap)


def _gspec(kind, R, C, tr, tc):
    assert R % tr == 0 and C % tc == 0, (R, C, tr, tc)

    def imap(i, j, k):
        part, half, rbi, cbi = _part_index(kind, R, C, tr, tc, i, j)
        return half, part, rbi, cbi

    return pl.BlockSpec((None, None, tr, tc), imap)


def _mm(name, grid, a_ops, b_ops, pairs, acc_shape, n_acc, extras, outs, epilogue):
    nk = grid[2]
    na, nb, ne, no = len(a_ops), len(b_ops), len(extras), len(outs)

    def body(*refs):
        a_refs = refs[:na]
        b_refs = refs[na:na + nb]
        e_refs = refs[na + nb:na + nb + ne]
        o_refs = refs[na + nb + ne:na + nb + ne + no]
        acc_refs = refs[na + nb + ne + no:]
        k = pl.program_id(2)

        def products():
            sums = [None] * n_acc
            for ai, bi, ci, dn in pairs:
                d = _dot(a_refs[ai][...], b_refs[bi][...], dn)
                sums[ci] = d if sums[ci] is None else sums[ci] + d
            return sums

        def finish(accs):
            res = epilogue(accs, [e[...] for e in e_refs])
            for o, r in zip(o_refs, res):
                o[...] = r.astype(o.dtype)

        if nk == 1:
            finish(products())
            return

        @pl.when(k == 0)
        def _():
            for acc, d in zip(acc_refs, products()):
                acc[...] = d

        @pl.when(jnp.logical_and(k > 0, k < nk - 1))
        def _():
            for acc, d in zip(acc_refs, products()):
                acc[...] += d

        @pl.when(k == nk - 1)
        def _():
            finish([acc[...] + d for acc, d in zip(acc_refs, products())])

    ops = list(a_ops) + list(b_ops) + list(extras)
    return pl.pallas_call(
        body, name=name, grid=grid,
        in_specs=[s for _, s in ops],
        out_specs=[s for _, s in outs],
        out_shape=[s for s, _ in outs],
        scratch_shapes=[pltpu.VMEM(acc_shape, F32) for _ in range(n_acc if nk > 1 else 0)],
        compiler_params=_cparams(("parallel", "parallel", "arbitrary")),
    )(*[a for a, _ in ops])


def _bs(shape, fn):
    return pl.BlockSpec(shape, fn)


def _tile_m(L):
    return min(L, 512)


def _mm_nn(name, x, ws, tk, tn, n_out, extras, outs_sd, epilogue, tm=None, cb_offsets=None, more=None):
    M, K = x.shape
    tm = min(M, tm or _tile_m(M))
    grid = (M // tm, n_out // tn, K // tk)
    a_ops = [(x, _bs((tm, tk), lambda i, j, k: (i, k)))]
    cb_offsets = cb_offsets or [0] * len(ws)
    b_ops = [(w.a, _wspec(w, tk, tn, lambda i, j, k: k, (lambda off: lambda i, j, k: j + off)(off)))
             for w, off in zip(ws, cb_offsets)]
    pairs = [(0, bi, bi, _NN) for bi in range(len(ws))]
    mn = _bs((tm, tn), lambda i, j, k: (i, j))
    ex = [(e, mn) for e in extras] + (more(tm) if more else [])
    outs = [(jax.ShapeDtypeStruct((M, n_out), dt), mn) for dt in outs_sd]
    return _mm(name, grid, a_ops, b_ops, pairs, (tm, tn), len(ws), ex, outs, epilogue)


def _mm_nt(name, xs, ws, tn, tk, extras, outs_sd, epilogue, tm=None):
    M, Nw = xs[0].shape
    Kw = ws[0].full_shape()[0]
    tm = min(M, tm or _tile_m(M))
    grid = (M // tm, Kw // tn, Nw // tk)
    a_ops = [(x, _bs((tm, tk), lambda i, j, k: (i, k))) for x in xs]
    b_ops = [(w.a, _wspec(w, tn, tk, lambda i, j, k: j, lambda i, j, k: k)) for w in ws]
    pairs = [(i, i, 0, _NT) for i in range(len(ws))]
    mn = _bs((tm, tn), lambda i, j, k: (i, j))
    ex = [(e, mn) for e in extras]
    outs = [(jax.ShapeDtypeStruct((M, Kw), dt), mn) for dt in outs_sd]
    return _mm(name, grid, a_ops, b_ops, pairs, (tm, tn), 1, ex, outs, epilogue)


def _mm_tn(name, x, dys, kind, R, C, tm, tn, tk=None):
    L, Kw = x.shape
    Nw = dys[0].shape[1]
    tk = tk or min(L, 1024)
    grid = (Kw // tm, Nw // tn, L // tk)
    a_ops = [(x, _bs((tk, tm), lambda i, j, k: (k, i)))]
    b_ops = [(dy, _bs((tk, tn), lambda i, j, k: (k, j))) for dy in dys]
    pairs = [(0, bi, bi, _TN) for bi in range(len(dys))]
    gs = _gspec(kind, R, C, tm, tn)
    outs = [(jax.ShapeDtypeStruct((2, 4, R, C), _WIRE), gs) for _ in dys]
    return _mm(name, grid, a_ops, b_ops, pairs, (tm, tn), len(dys), [], outs, lambda accs, ex: accs)


def _row_tile(L, n_streams):
    return min(L, 512 if n_streams <= 5 else 256)


def _rowwise(name, body, ins, outs, L, acc_outs=()):
    tr = _row_tile(L, sum(kind == "row" for _, kind in ins) + len(outs))
    n_in, n_out = len(ins), len(outs)

    def kbody(*refs):
        i_refs, o_refs, a_refs = refs[:n_in], refs[n_in:n_in + n_out], refs[n_in + n_out:]
        res, sums = body(*[r[...] for r in i_refs])
        for o, r in zip(o_refs, res):
            o[...] = r.astype(o.dtype)
        if a_refs:
            @pl.when(pl.program_id(0) == 0)
            def _():
                for a in a_refs:
                    a[...] = jnp.zeros(a.shape, F32)
            for a, s in zip(a_refs, sums):
                a[...] += s

    in_specs = []
    for arr, kind in ins:
        if kind == "row":
            in_specs.append(pl.BlockSpec((tr, arr.shape[1]), lambda i: (i, 0)))
        else:
            in_specs.append(pl.BlockSpec(arr.shape, lambda i: (0, 0)))
    out_specs = [pl.BlockSpec((tr, c), lambda i: (i, 0)) for c, _ in outs]
    out_shape = [jax.ShapeDtypeStruct((L, c), dt) for c, dt in outs]
    out_specs += [pl.BlockSpec((1, c), lambda i: (0, 0)) for c in acc_outs]
    out_shape += [jax.ShapeDtypeStruct((1, c), F32) for c in acc_outs]
    return pl.pallas_call(
        kbody, name=name, grid=(L // tr,), in_specs=in_specs, out_specs=out_specs, out_shape=out_shape,
        compiler_params=_cparams(("arbitrary",)),
    )(*[a for a, _ in ins])


def _rms_fwd(name, x, g):
    def body(xv, gv):
        r = lax.rsqrt(jnp.mean(xv * xv, axis=-1, keepdims=True) + _EPS)
        return [xv * r * gv], []
    return _rowwise(name, body, [(x, "row"), (g, "vec")], [(x.shape[1], _MXU)], x.shape[0])[0]


def _rms_bwd(name, dh, x, g, dres):
    def body(dhv, xv, gv, dr):
        r = lax.rsqrt(jnp.mean(xv * xv, axis=-1, keepdims=True) + _EPS)
        xh = xv * r
        dxh = dhv * gv
        dx = dr + r * (dxh - xh * jnp.mean(dxh * xh, axis=-1, keepdims=True))
        return [dx, dx], [jnp.sum(dhv * xh, axis=0, keepdims=True)]
    D = x.shape[1]
    return _rowwise(name, body, [(dh, "row"), (x, "row"), (g, "vec"), (dres, "row")], [(D, F32), (D, _MXU)],
                    x.shape[0], [D])


def _loss_head(x, g, target, pp, gt):
    D = x.shape[1]

    def body(xv, gv, tv, ppv, gtv):
        r = lax.rsqrt(jnp.mean(xv * xv, axis=-1, keepdims=True) + _EPS)
        xh = xv * r
        e = xh * gv - tv
        dy = e * (1.0 / D)
        dxh = dy * gv
        dx = r * (dxh - xh * jnp.mean(dxh * xh, axis=-1, keepdims=True))
        row_loss = 0.5 * jnp.mean(e * e, axis=-1, keepdims=True)
        lsum = jnp.sum(row_loss, axis=0, keepdims=True) + jnp.zeros((1, _LANES), F32)
        gt32 = gtv.astype(F32)
        return [dx, dx * ppv * gt32 * (1.0 - gt32), dx * gt32], [jnp.sum(dy * xh, axis=0, keepdims=True), lsum]
    return _rowwise("loss_head", body, [(x, "row"), (g, "vec"), (target, "row"), (pp, "row"), (gt, "row")],
                    [(D, F32), (D, _MXU), (D, _MXU)], x.shape[0], [D, _LANES])


def _ple_bwd_elem(dx, pp, gt):
    def body(dxv, ppv, gtv):
        gt32 = gtv.astype(F32)
        return [dxv * ppv * gt32 * (1.0 - gt32), dxv * gt32], []
    D = dx.shape[1]
    return _rowwise("ple_bwd_elem", body, [(dx, "row"), (pp, "row"), (gt, "row")], [(D, _MXU), (D, _MXU)], dx.shape[0])


def _glu_bwd_elem(dx, val, sg):
    def body(dxv, vv, sv):
        v32, s32 = vv.astype(F32), sv.astype(F32)
        return [jnp.concatenate([dxv * s32, dxv * v32 * s32 * (1.0 - s32)], axis=1)], []
    D = dx.shape[1]
    return _rowwise("glu_bwd_elem", body, [(dx, "row"), (val, "row"), (sg, "row")], [(2 * D, _MXU)], dx.shape[0])[0]


def _gm_common(z, ln_g, ln_b, wc_bf, bsT):
    W = z.shape[1] // 2
    zu, zv = z[:, :W], z[:, W:]
    u, v = _gelu(zu), _gelu(zv)
    mu = jnp.mean(v, axis=-1, keepdims=True)
    vc = v - mu
    rstd = lax.rsqrt(jnp.mean(vc * vc, axis=-1, keepdims=True) + _EPS)
    vh = vc * rstd
    vn = vh * ln_g + ln_b
    vnb = vn.astype(_MXU)
    svs = []
    for h in range(_GM_HEADS):
        sl = slice(h * _LANES, (h + 1) * _LANES)
        svs.append(_dot(wc_bf[h], vnb[:, sl], _NN) + bsT[:, h:h + 1])
    return zu, zv, u, vh, rstd, vnb, svs


def _causal(w):
    t = lax.broadcasted_iota(jnp.int32, w.shape, w.ndim - 2)
    s = lax.broadcasted_iota(jnp.int32, w.shape, w.ndim - 1)
    return jnp.where(s <= t, w, jnp.zeros_like(w))


def _gmlp_fwd(z, ln_g, ln_b, w_s, bsT):
    L, W2 = z.shape
    W = W2 // 2

    def body(z_ref, g_ref, b_ref, ws_ref, bs_ref, m_ref):
        wc = _causal(ws_ref[...]).astype(_MXU)
        _, _, u, _, _, _, svs = _gm_common(z_ref[...], g_ref[...], b_ref[...], wc, bs_ref[...])
        for h in range(_GM_HEADS):
            sl = slice(h * _LANES, (h + 1) * _LANES)
            m_ref[:, sl] = (u[:, sl] * svs[h]).astype(m_ref.dtype)

    return pl.pallas_call(
        body, name="gmlp_fwd", grid=(L // _GM_CHUNK,),
        in_specs=[pl.BlockSpec((_GM_CHUNK, W2), lambda n: (n, 0)),
                  pl.BlockSpec((1, W), lambda n: (0, 0)), pl.BlockSpec((1, W), lambda n: (0, 0)),
                  pl.BlockSpec(w_s.shape, lambda n: (0, 0, 0)), pl.BlockSpec(bsT.shape, lambda n: (0, 0))],
        out_specs=pl.BlockSpec((_GM_CHUNK, W), lambda n: (n, 0)),
        out_shape=jax.ShapeDtypeStruct((L, W), _MXU),
        compiler_params=_cparams(("arbitrary",)),
    )(z, ln_g, ln_b, w_s, bsT)


def _gmlp_bwd(z, dm, ln_g, ln_b, w_s, bsT):
    L, W2 = z.shape
    W = W2 // 2
    T = _GM_CHUNK

    def body(z_ref, dm_ref, g_ref, b_ref, ws_ref, bs_ref, dz_ref, dws_ref, dbs_ref, dg_ref, db_ref):
        @pl.when(pl.program_id(0) == 0)
        def _():
            dws_ref[...] = jnp.zeros(dws_ref.shape, F32)
            dbs_ref[...] = jnp.zeros(dbs_ref.shape, F32)
            dg_ref[...] = jnp.zeros(dg_ref.shape, F32)
            db_ref[...] = jnp.zeros(db_ref.shape, F32)

        wc = _causal(ws_ref[...]).astype(_MXU)
        ln_g_v = g_ref[...]
        zu, zv, u, vh, rstd, vnb, svs = _gm_common(z_ref[...], ln_g_v, b_ref[...], wc, bs_ref[...])
        dmv = dm_ref[...]
        lane = lax.broadcasted_iota(jnp.int32, (T, _LANES), 1)
        dbs = jnp.zeros((T, _LANES), F32)
        dvn_parts = []
        for h in range(_GM_HEADS):
            sl = slice(h * _LANES, (h + 1) * _LANES)
            dsv = dmv[:, sl] * u[:, sl]
            dz_ref[:, sl] = (dmv[:, sl] * svs[h] * _gelu_grad(zu[:, sl])).astype(dz_ref.dtype)
            dbs = dbs + jnp.where(lane == h, jnp.sum(dsv, axis=1, keepdims=True), 0.0)
            dsvb = dsv.astype(_MXU)
            dws_ref[h] += _dot(dsvb, vnb[:, sl], _NT)
            dvn_parts.append(_dot(wc[h], dsvb, _TN))
        dbs_ref[...] += dbs
        dvn = jnp.concatenate(dvn_parts, axis=1)
        dg_ref[...] += jnp.sum(dvn * vh, axis=0, keepdims=True)
        db_ref[...] += jnp.sum(dvn, axis=0, keepdims=True)
        dxh = dvn * ln_g_v
        dv = rstd * (dxh - jnp.mean(dxh, axis=-1, keepdims=True) - vh * jnp.mean(dxh * vh, axis=-1, keepdims=True))
        dz_ref[:, W:] = (dv * _gelu_grad(zv)).astype(dz_ref.dtype)

        @pl.when(pl.program_id(0) == pl.num_programs(0) - 1)
        def _():
            dws_ref[...] = _causal(dws_ref[...])

    return pl.pallas_call(
        body, name="gmlp_bwd", grid=(L // T,),
        in_specs=[pl.BlockSpec((T, W2), lambda n: (n, 0)), pl.BlockSpec((T, W), lambda n: (n, 0)),
                  pl.BlockSpec((1, W), lambda n: (0, 0)), pl.BlockSpec((1, W), lambda n: (0, 0)),
                  pl.BlockSpec(w_s.shape, lambda n: (0, 0, 0)), pl.BlockSpec(bsT.shape, lambda n: (0, 0))],
        out_specs=[pl.BlockSpec((T, W2), lambda n: (n, 0)),
                   pl.BlockSpec(w_s.shape, lambda n: (0, 0, 0)), pl.BlockSpec((T, _LANES), lambda n: (0, 0)),
                   pl.BlockSpec((1, W), lambda n: (0, 0)), pl.BlockSpec((1, W), lambda n: (0, 0))],
        out_shape=[jax.ShapeDtypeStruct((L, W2), _MXU), jax.ShapeDtypeStruct(w_s.shape, F32),
                   jax.ShapeDtypeStruct((T, _LANES), F32),
                   jax.ShapeDtypeStruct((1, W), F32), jax.ShapeDtypeStruct((1, W), F32)],
        compiler_params=_cparams(("arbitrary",)),
    )(z, dm, ln_g, ln_b, w_s, bsT)


def _s5_prep_math(a_re, a_im, log_dt):
    dt = jnp.exp(log_dt)
    xr, xi = a_re * dt, a_im * dt
    e = jnp.exp(xr)
    lbr, lbi = e * jnp.cos(xi), e * jnp.sin(xi)
    dn = a_re * a_re + a_im * a_im
    nr, ni = lbr - 1.0, lbi
    pr, pi = nr * a_re + ni * a_im, ni * a_re - nr * a_im
    return dt, lbr, lbi, dn, nr, ni, pr, pi


def _vm():
    return pl.BlockSpec(memory_space=pltpu.VMEM)


def _s5_prep(a_re, a_im, log_dt, b_re, b_im):
    def body(ar_ref, ai_ref, ld_ref, br_ref, bi_ref, lbr_ref, lbi_ref, Br_ref, Bi_ref):
        _, lbr, lbi, dn, _, _, pr, pi = _s5_prep_math(ar_ref[...], ai_ref[...], ld_ref[...])
        cr, ci = (pr / dn)[:, None, :], (pi / dn)[:, None, :]
        lbr_ref[...] = lbr
        lbi_ref[...] = lbi
        br, bi = br_ref[...], bi_ref[...]
        Br_ref[...] = cr * br - ci * bi
        Bi_ref[...] = cr * bi + ci * br

    sd = jax.ShapeDtypeStruct
    return pl.pallas_call(
        body, name="s5_prep", in_specs=[_vm()] * 5, out_specs=[_vm()] * 4,
        out_shape=[sd(a_re.shape, F32), sd(a_re.shape, F32), sd(b_re.shape, F32), sd(b_re.shape, F32)],
    )(a_re, a_im, log_dt, b_re, b_im)


def _s5_prep_bwd(a_re, a_im, log_dt, b_re, b_im, dlbr_s, dlbi_s, dBr, dBi):
    def body(ar_ref, ai_ref, ld_ref, br_ref, bi_ref, dlr_ref, dli_ref, dBr_ref, dBi_ref,
             dar_ref, dai_ref, dld_ref, dbr_ref, dbi_ref):
        a_re_v, a_im_v = ar_ref[...], ai_ref[...]
        dt, lbr, lbi, dn, nr, ni, pr, pi = _s5_prep_math(a_re_v, a_im_v, ld_ref[...])
        cr, ci = (pr / dn)[:, None, :], (pi / dn)[:, None, :]
        br, bi, dBr_v, dBi_v = br_ref[...], bi_ref[...], dBr_ref[...], dBi_ref[...]
        dbr_ref[...] = cr * dBr_v + ci * dBi_v
        dbi_ref[...] = cr * dBi_v - ci * dBr_v
        dcr = jnp.sum(br * dBr_v + bi * dBi_v, axis=1)
        dci = jnp.sum(br * dBi_v - bi * dBr_v, axis=1)
        dpr, dpi = dcr / dn, dci / dn
        ddn = -(dcr * pr + dci * pi) / (dn * dn)
        dnr = dpr * a_re_v - dpi * a_im_v
        dni = dpr * a_im_v + dpi * a_re_v
        dlbr = dlr_ref[...] + dnr
        dlbi = dli_ref[...] + dni
        dxr = dlbr * lbr + dlbi * lbi
        dxi = dlbi * lbr - dlbr * lbi
        dar_ref[...] = dpr * nr + dpi * ni + 2.0 * ddn * a_re_v + dxr * dt
        dai_ref[...] = dpr * ni - dpi * nr + 2.0 * ddn * a_im_v + dxi * dt
        dld_ref[...] = jnp.sum(dxr * a_re_v + dxi * a_im_v, axis=0, keepdims=True) * dt

    sd = jax.ShapeDtypeStruct
    return pl.pallas_call(
        body, name="s5_prep_bwd", in_specs=[_vm()] * 9, out_specs=[_vm()] * 5,
        out_shape=[sd(a_re.shape, F32), sd(a_re.shape, F32), sd(log_dt.shape, F32),
                   sd(b_re.shape, F32), sd(b_re.shape, F32)],
    )(a_re, a_im, log_dt, b_re, b_im, dlbr_s, dlbi_s, dBr, dBi)


def _shift_rows(v, down):
    n = v.shape[0]
    rolled = pltpu.roll(v, 1 if down else n - 1, 0)
    row = lax.broadcasted_iota(jnp.int32, v.shape, 0)
    return jnp.where(row == (0 if down else n - 1), 0.0, rolled)


def _cmul(ar, ai, br, bi):
    return ar * br - ai * bi, ar * bi + ai * br


_SEG = 8
_UNROLL = 8


def _seg_rows(k):
    if isinstance(k, int):
        return pl.ds(k * _SEG, _SEG)
    return pl.ds(pl.multiple_of(k * _SEG, _SEG), _SEG)


def _unrolled(n, step, init):
    main = n // _UNROLL

    def trip(kk, s):
        for uu in range(_UNROLL):
            s = step(kk * _UNROLL + uu, s)
        return s

    s = lax.fori_loop(0, main, trip, init)
    for r in range(main * _UNROLL, n):
        s = step(r, s)
    return s


def _interleave(src_ref, dst_ref, nk):
    def step(k, carry):
        dst_ref[_seg_rows(k), :] = src_ref[pl.ds(k, _SEG, stride=nk), :]
        return carry
    _unrolled(nk, step, 0)


def _deinterleave(src_ref, dst_ref, nk):
    def step(k, carry):
        dst_ref[pl.ds(k, _SEG, stride=nk), :] = src_ref[_seg_rows(k), :]
        return carry
    _unrolled(nk, step, 0)


def _segment_inits(er, ei, ar, ai, nk, down):
    pr, pi = ar, ai
    for _ in range(int(math.log2(nk))):
        pr, pi = _cmul(pr, pi, pr, pi)
    fr, fi = er, ei
    for _ in range(_SEG - 1):
        sr, si = _shift_rows(fr, down), _shift_rows(fi, down)
        mr, mi = _cmul(pr, pi, sr, si)
        fr, fi = er + mr, ei + mi
    return _shift_rows(fr, down), _shift_rows(fi, down)


def _scan_states(x_re, x_im, ar, ai, nk):
    lanes = ar.shape[1]

    def step(k, s):
        rows = _seg_rows(k)
        mr, mi = _cmul(ar, ai, s[0], s[1])
        return mr + x_re[rows, :], mi + x_im[rows, :]

    zero = jnp.zeros((_SEG, lanes), F32)
    er, ei = _unrolled(nk, step, (zero, zero))
    ir, ii = _segment_inits(er, ei, ar, ai, nk, True)

    def step2(k, s):
        rows = _seg_rows(k)
        mr, mi = _cmul(ar, ai, s[0], s[1])
        nr, ni = mr + x_re[rows, :], mi + x_im[rows, :]
        x_re[rows, :] = nr
        x_im[rows, :] = ni
        return nr, ni

    _unrolled(nk, step2, (ir, ii))


def _s5_tile_fwd(u, bd_re, bd_im, cd_re, cd_im, ar, ai, d, s_re, s_im, nk):
    s_re[...] = _dot(u, bd_re, _NN)
    s_im[...] = _dot(u, bd_im, _NN)
    _scan_states(s_re, s_im, ar, ai, nk)
    return _dot(s_re[...], cd_re, _NN) - _dot(s_im[...], cd_im, _NN) + d * u


def _s5_specs(L, T):
    lanes = _S5_GT * _S5_P
    u_spec = pl.BlockSpec((L, _LANES), lambda t: (0, t))
    bd_spec = pl.BlockSpec((None, _S5_GT, _S5_C, _S5_P), lambda t: (t, 0, 0, 0))
    cd_spec = pl.BlockSpec((None, _S5_GT, _S5_P, _S5_C), lambda t: (t, 0, 0, 0))
    lam_spec = pl.BlockSpec((None, 1, lanes), lambda t: (t, 0, 0))
    d_spec = pl.BlockSpec((1, _LANES), lambda t: (0, t))
    return lanes, u_spec, bd_spec, cd_spec, lam_spec, d_spec


def _fill_block_diag(dst_ref, blocks_ref):
    _, a, b = blocks_ref.shape
    dst_ref[...] = jnp.zeros(dst_ref.shape, F32)
    for g in range(_S5_GT):
        dst_ref[g * a:(g + 1) * a, g * b:(g + 1) * b] = blocks_ref[g]


def _take_block_diag(dst_ref, v):
    _, a, b = dst_ref.shape
    for g in range(_S5_GT):
        dst_ref[g] = v[g * a:(g + 1) * a, g * b:(g + 1) * b]


def _s5_dense(bdr, bdi, cdr, cdi, dense):
    for src, dst in zip((bdr, bdi, cdr, cdi), dense):
        _fill_block_diag(dst, src)
    return [dst[...] for dst in dense]


def _s5_dense_scratch(lanes):
    return [pltpu.VMEM((_LANES, lanes), F32), pltpu.VMEM((_LANES, lanes), F32),
            pltpu.VMEM((lanes, _LANES), F32), pltpu.VMEM((lanes, _LANES), F32)]


def _s5_fwd(u, bd_re, bd_im, cd_re, cd_im, lam_re, lam_im, d):
    L, Wd = u.shape
    T = Wd // _LANES
    nk = L // _SEG
    lanes, u_spec, bd_spec, cd_spec, lam_spec, d_spec = _s5_specs(L, T)
    s_spec = pl.BlockSpec((L, lanes), lambda t: (0, t))

    def body(u_ref, bdr, bdi, cdr, cdi, lr, li, d_ref, g_ref, s_re, s_im, up, tmp, *dense):
        ar = jnp.broadcast_to(lr[...], (_SEG, lanes))
        ai = jnp.broadcast_to(li[...], (_SEG, lanes))
        bd_re_v, bd_im_v, cd_re_v, cd_im_v = _s5_dense(bdr, bdi, cdr, cdi, dense)
        _interleave(u_ref, up, nk)
        y = _s5_tile_fwd(up[...], bd_re_v, bd_im_v, cd_re_v, cd_im_v, ar, ai, d_ref[...], s_re, s_im, nk)
        up[...] = _gelu(y)
        _deinterleave(up, tmp, nk)
        g_ref[...] = tmp[...].astype(g_ref.dtype)

    return pl.pallas_call(
        body, name="s5_fwd", grid=(T,),
        in_specs=[u_spec, bd_spec, bd_spec, cd_spec, cd_spec, lam_spec, lam_spec, d_spec],
        out_specs=[u_spec, s_spec, s_spec],
        out_shape=[jax.ShapeDtypeStruct((L, Wd), _MXU), jax.ShapeDtypeStruct((L, T * lanes), F32),
                   jax.ShapeDtypeStruct((L, T * lanes), F32)],
        scratch_shapes=[pltpu.VMEM((L, _LANES), F32) for _ in range(2)] + _s5_dense_scratch(lanes),
        compiler_params=_cparams(("arbitrary",)),
    )(u, bd_re, bd_im, cd_re, cd_im, lam_re, lam_im, d)


def _s5_bwd(u, dg, states_re, states_im, bd_re, bd_im, cd_re, cd_im, lam_re, lam_im, d):
    L, Wd = u.shape
    T = Wd // _LANES
    nk = L // _SEG
    lanes, u_spec, bd_spec, cd_spec, lam_spec, d_spec = _s5_specs(L, T)
    s_spec = pl.BlockSpec((L, lanes), lambda t: (0, t))

    def body(u_ref, dg_ref, s_re, s_im, bdr, bdi, cdr, cdi, lr, li, d_ref,
             du_ref, dbdr, dbdi, dcdr, dcdi, dlr, dli, dd_ref, g_re, g_im, up, dgp, tmp, *dense):
        ar = jnp.broadcast_to(lr[...], (_SEG, lanes))
        ai = jnp.broadcast_to(li[...], (_SEG, lanes))
        bd_re_v, bd_im_v, cd_re_v, cd_im_v = _s5_dense(bdr, bdi, cdr, cdi, dense)
        _interleave(u_ref, up, nk)
        _interleave(dg_ref, dgp, nk)
        uv, dv = up[...], d_ref[...]
        y = _dot(s_re[...], cd_re_v, _NN) - _dot(s_im[...], cd_im_v, _NN) + dv * uv
        dy = dgp[...] * _gelu_grad(y)
        dd_ref[...] = jnp.sum(dy * uv, axis=0, keepdims=True)
        dyb = dy.astype(_MXU)
        _take_block_diag(dcdr, _dot(dyb, s_re[...], _TN))
        _take_block_diag(dcdi, -_dot(dyb, s_im[...], _TN))
        g_re[...] = _dot(dyb, cd_re_v, _NT)
        g_im[...] = -_dot(dyb, cd_im_v, _NT)

        nai = -ai

        def step(j, s):
            rows = _seg_rows(nk - 1 - j)
            mr, mi = _cmul(ar, nai, s[0], s[1])
            return mr + g_re[rows, :], mi + g_im[rows, :]

        zero = jnp.zeros((_SEG, lanes), F32)
        er, ei = _unrolled(nk, step, (zero, zero))
        ir, ii = _segment_inits(er, ei, ar, nai, nk, False)

        def acc_lam(gr, gi, pr, pi, acc):
            return acc[0] + gr * pr + gi * pi, acc[1] + gi * pr - gr * pi

        def step2(j, carry):
            s, acc = carry
            k = nk - 1 - j
            rows = _seg_rows(k)
            mr, mi = _cmul(ar, nai, s[0], s[1])
            nr, ni = mr + g_re[rows, :], mi + g_im[rows, :]
            g_re[rows, :] = nr
            g_im[rows, :] = ni
            prev = _seg_rows(k - 1)
            return (nr, ni), acc_lam(nr, ni, s_re[prev, :], s_im[prev, :], acc)

        (g0r, g0i), acc = _unrolled(nk - 1, step2, ((ir, ii), (zero, zero)))
        first = _seg_rows(0)
        mr, mi = _cmul(ar, nai, g0r, g0i)
        nr, ni = mr + g_re[first, :], mi + g_im[first, :]
        g_re[first, :] = nr
        g_im[first, :] = ni
        last = _seg_rows(nk - 1)
        acc = acc_lam(nr, ni, _shift_rows(s_re[last, :], True), _shift_rows(s_im[last, :], True), acc)
        dlr[...] = jnp.sum(acc[0], axis=0, keepdims=True)
        dli[...] = jnp.sum(acc[1], axis=0, keepdims=True)

        gtr, gti = g_re[...].astype(_MXU), g_im[...].astype(_MXU)
        ub = uv.astype(_MXU)
        _take_block_diag(dbdr, _dot(ub, gtr, _TN))
        _take_block_diag(dbdi, _dot(ub, gti, _TN))
        dgp[...] = _dot(gtr, bd_re_v, _NT) + _dot(gti, bd_im_v, _NT) + dy * dv
        _deinterleave(dgp, tmp, nk)
        du_ref[...] = tmp[...].astype(du_ref.dtype)

    sd = jax.ShapeDtypeStruct
    big = sd((T, _S5_GT, _S5_C, _S5_P), F32)
    return pl.pallas_call(
        body, name="s5_bwd", grid=(T,),
        in_specs=[u_spec, u_spec, s_spec, s_spec, bd_spec, bd_spec, cd_spec, cd_spec, lam_spec, lam_spec, d_spec],
        out_specs=[u_spec, bd_spec, bd_spec, bd_spec, bd_spec, lam_spec, lam_spec, d_spec],
        out_shape=[sd((L, Wd), _MXU), big, big, big, big, sd((T, 1, lanes), F32), sd((T, 1, lanes), F32),
                   sd((1, Wd), F32)],
        scratch_shapes=[pltpu.VMEM((L, lanes), F32) for _ in range(2)]
        + [pltpu.VMEM((L, _LANES), F32) for _ in range(3)] + _s5_dense_scratch(lanes),
        compiler_params=_cparams(("arbitrary",)),
    )(u, dg, states_re, states_im, bd_re, bd_im, cd_re, cd_im, lam_re, lam_im, d)


def _half_tile(R, few_arrays=False):
    for t in ((512, 704, 128) if few_arrays else (256, 352, 128)):
        if R % t == 0:
            return t
    raise ValueError(R)


def _cast_shard(name, w, layer, kind, R, C):
    tr = _half_tile(R, True)
    nr = R // tr

    def body(w_ref, o_ref):
        o_ref[...] = w_ref[...].astype(o_ref.dtype)

    if kind == "col":
        in_map = lambda h, i: (layer, h * nr + i, 0)
    else:
        in_map = lambda h, i: (layer, i, h)
    return pl.pallas_call(
        body, name=name, grid=(2, nr), in_specs=[pl.BlockSpec((None, tr, C), in_map)],
        out_specs=pl.BlockSpec((None, tr, C), lambda h, i: (h, i, 0)),
        out_shape=jax.ShapeDtypeStruct((2, R, C), _WIRE),
        compiler_params=_cparams(("arbitrary", "arbitrary")),
    )(w)


def _adam_math(w, g, m, v):
    m2 = _B1 * m + (1.0 - _B1) * g
    v2 = _B2 * v + (1.0 - _B2) * (g * g)
    m_hat = m2 / (1.0 - _B1 ** _STEP)
    v_hat = v2 / (1.0 - _B2 ** _STEP)
    delta = -_LR * (m_hat / (jnp.sqrt(v_hat) + _AEPS) + _WD * w)
    return delta, m2, v2


def _adamw_big(name, w, m, v, layer, pair, kind, R, C, c, after, prev):
    tr = _half_tile(R, few_arrays=C <= 1024 and R % 512 == 0)
    nr = R // tr

    def body(c_ref, w_ref, m_ref, v_ref, own_ref, other_ref, *rest):
        go_ref, d_ref, mo_ref, vo_ref = rest[-4:]
        g = jnp.where(pl.program_id(0) == c_ref[0], own_ref[...], other_ref[...])
        delta, m2, v2 = _adam_math(w_ref[...], g, m_ref[...], v_ref[...])
        go_ref[...] = g
        d_ref[...] = delta
        mo_ref[...] = m2
        vo_ref[...] = v2

    if kind == "col":
        nat = pl.BlockSpec((None, tr, C), lambda h, i, c_ref: (layer, h * nr + i, 0))
    else:
        nat = pl.BlockSpec((None, tr, C), lambda h, i, c_ref: (layer, i, h))

    def gspec(own):
        return pl.BlockSpec((tr, C), lambda h, i, c_ref: (jnp.where((h == c_ref[0]) == own, i, 0), 0))

    carried = list(prev) if prev is not None else []
    gs = pltpu.PrefetchScalarGridSpec(
        num_scalar_prefetch=1, grid=(2, nr),
        in_specs=[nat, nat, nat, gspec(True), gspec(False), _any()] + [_any()] * len(carried),
        out_specs=[nat] * 4)
    sd = jax.ShapeDtypeStruct(w.shape, F32)
    return pl.pallas_call(
        body, name=name, grid_spec=gs, out_shape=[sd] * 4,
        input_output_aliases={7 + k: k for k in range(len(carried))},
        compiler_params=_cparams(("arbitrary", "arbitrary")),
    )(c, w, m, v, pair[0], pair[1], after, *carried)


def _adamw_small(ws, gs, ms, vs):
    n = len(ws)

    def body(*refs):
        for i in range(n):
            w_ref, g_ref, m_ref, v_ref, d_ref, mo_ref, vo_ref = refs[i::n]
            delta, m2, v2 = _adam_math(w_ref[...], g_ref[...], m_ref[...], v_ref[...])
            d_ref[...] = delta
            mo_ref[...] = m2
            vo_ref[...] = v2

    sds = [jax.ShapeDtypeStruct(w.shape, F32) for w in ws]
    outs = pl.pallas_call(
        body, name="adamw_small", in_specs=[_vm()] * (4 * n), out_specs=[_vm()] * (3 * n), out_shape=sds * 3,
        compiler_params=pltpu.CompilerParams(vmem_limit_bytes=_VMEM_LIMIT),
    )(*ws, *gs, *ms, *vs)
    return outs[:n], outs[n:2 * n], outs[2 * n:]


def _place():
    x, y, c = lax.axis_index("x"), lax.axis_index("y"), lax.axis_index("c")
    chips = [(1 - x, y), (x, 1 - y), (1 - x, 1 - y)]
    return x, y, c, 2 * x + y, chips


def _any():
    return pl.BlockSpec(memory_space=pl.ANY)


def _remote(src, dst, ssem, rsem, dev):
    return pltpu.make_async_remote_copy(src_ref=src, dst_ref=dst, send_sem=ssem, recv_sem=rsem,
                                        device_id=dev, device_id_type=_MESH)


def _allgather(name, shards):
    n = len(shards)

    def body(*refs):
        s_refs, g_refs = refs[:n], refs[n:2 * n]
        send0, recv0, send1, recv1, send2, recv2 = refs[2 * n:]
        x, y, c, q, _ = _place()
        sib, xn, yn = (x, y, 1 - c), (1 - x, y, c), (x, 1 - y, c)
        qx, qy, qd = 2 * (1 - x) + y, 2 * x + (1 - y), 2 * (1 - x) + (1 - y)
        _handshake([sib, xn, yn])
        own = [_remote(s_refs[a], g_refs[a].at[q], send0.at[a], recv0.at[a], sib) for a in range(n)]

        def pieces(a):
            g, half = g_refs[a], s_refs[a].shape[1] // 2
            return [g.at[qx, c], g.at[qy, c], g.at[qd, c, pl.ds(0, half)], g.at[qd, c, pl.ds(half, half)]]

        def relayed(a):
            g, half = g_refs[a], s_refs[a].shape[1] // 2
            return [(g.at[qx, c, pl.ds(0, half)], yn), (g.at[qy, c, pl.ds(half, half)], xn)]

        first = []
        for a in range(n):
            first.append(_remote(s_refs[a].at[c], g_refs[a].at[q, c], send1.at[4 * a], recv1.at[4 * a], xn))
            first.append(_remote(s_refs[a].at[c], g_refs[a].at[q, c], send1.at[4 * a + 1], recv1.at[4 * a + 1], yn))
        for cp in first + own:
            cp.start()
        later = []
        for a in range(n):
            land = pieces(a)
            for j in range(4):
                k = 4 * a + j
                _remote(land[j], land[j], send1.at[k], recv1.at[k], xn).wait_recv()
                if j < 2:
                    src, to = relayed(a)[j]
                    cp = _remote(src, src, send1.at[k + 2], recv1.at[k + 2], to)
                    cp.start()
                    later.append(cp)
                cp = _remote(land[j], land[j], send2.at[k], recv2.at[k], sib)
                cp.start()
                later.append(cp)
        for a in range(n):
            g, half = g_refs[a], s_refs[a].shape[1] // 2
            theirs = [g.at[qx, 1 - c], g.at[qy, 1 - c], g.at[qd, 1 - c, pl.ds(0, half)],
                      g.at[qd, 1 - c, pl.ds(half, half)]]
            for j in range(4):
                _remote(theirs[j], theirs[j], send2.at[4 * a + j], recv2.at[4 * a + j], sib).wait_recv()
        for cp in own:
            cp.wait()
        for cp in first + later:
            cp.wait_send()

    return _sequencer(name, _ID_GATHER, body, shards,
                      [jax.ShapeDtypeStruct((4,) + s.shape, s.dtype) for s in shards], [n, n] + [4 * n] * 4)


def _handshake(peers):
    barrier = pltpu.get_barrier_semaphore()
    for peer in peers:
        pl.semaphore_signal(barrier, inc=1, device_id=peer, device_id_type=_MESH)
    pl.semaphore_wait(barrier, len(peers))


_ID_SIBLING, _ID_CHIPS, _ID_GATHER, _ID_ALL = 1, 2, 3, 4


def _sequencer(name, collective_id, body, ins, out_types, sem_counts):
    mesh = plsc.ScalarSubcoreMesh(axis_name="seq", num_cores=1)
    moved = sum(math.prod(o.shape) * jnp.dtype(o.dtype).itemsize for o in out_types)
    return pl.kernel(
        body, name=name, out_type=out_types, mesh=mesh,
        scratch_types=[pltpu.SemaphoreType.DMA((k,)) for k in sem_counts],
        compiler_params=pltpu.CompilerParams(collective_id=collective_id),
        cost_estimate=pl.CostEstimate(flops=0, transcendentals=0, bytes_accessed=2 * moved,
                                      remote_bytes_transferred=moved),
    )(*ins)


def _swap_halves(name, grads):
    n = len(grads)

    def body(*refs):
        g_refs, t_refs = refs[:n], refs[n:2 * n]
        send, recv = refs[2 * n:]
        x, y, c, _, _ = _place()
        _handshake([(x, y, 1 - c)])
        cps = [_remote(g_refs[a].at[1 - c], t_refs[a], send.at[a], recv.at[a], (x, y, 1 - c)) for a in range(n)]
        for cp in cps:
            cp.start()
        for cp in cps:
            cp.wait()

    return _sequencer(name, _ID_SIBLING, body, grads,
                      [jax.ShapeDtypeStruct(g.shape[1:], g.dtype) for g in grads], [n, n])


def _chip_sum(name, grads, swapped, after, core):
    n = len(grads)
    _, _, R, C = grads[0].shape
    tr = _half_tile(R, True)

    def body(c_ref, *refs):
        for g_ref, t_ref, o_ref in zip(refs[:n], refs[n:2 * n], refs[2 * n + 1:]):
            o_ref[...] = (g_ref[...].astype(F32) + t_ref[...].astype(F32)).astype(o_ref.dtype)

    own = pl.BlockSpec((None, None, tr, C), lambda r, i, c_ref: (c_ref[0], r, i, 0))
    part = pl.BlockSpec((None, tr, C), lambda r, i, c_ref: (r, i, 0))
    gs = pltpu.PrefetchScalarGridSpec(
        num_scalar_prefetch=1, grid=(4, R // tr), in_specs=[own] * n + [part] * n + [_any()], out_specs=[part] * n)
    return pl.pallas_call(
        body, name=name, grid_spec=gs, out_shape=[jax.ShapeDtypeStruct((4, R, C), _WIRE)] * n,
        compiler_params=_cparams(("arbitrary", "arbitrary")),
    )(core, *grads, *swapped, after)


def _scatter_parts(name, parts):
    n = len(parts)

    def body(*refs):
        p_refs, t_refs = refs[:n], refs[n:2 * n]
        send, recv = refs[2 * n:]
        x, y, c, q, chips = _place()
        _handshake([(rx, ry, c) for rx, ry in chips])
        cps = []
        for a in range(n):
            for j, (rx, ry) in enumerate(chips):
                k = 3 * a + j
                cps.append(_remote(p_refs[a].at[2 * rx + ry], t_refs[a].at[q], send.at[k], recv.at[k], (rx, ry, c)))
        for cp in cps:
            cp.start()
        for a in range(n):
            for j, (rx, ry) in enumerate(chips):
                k = 3 * a + j
                land = t_refs[a].at[2 * rx + ry]
                _remote(land, land, send.at[k], recv.at[k], (rx, ry, c)).wait_recv()
        for cp in cps:
            cp.wait_send()

    return _sequencer(name, _ID_CHIPS, body, parts,
                      [jax.ShapeDtypeStruct(p.shape, p.dtype) for p in parts], [3 * n, 3 * n])


def _sum_parts(name, parts, landed, where, after):
    n = len(parts)
    _, R, C = parts[0].shape
    tr = _half_tile(R, True)

    def body(w_ref, *refs):
        for a in range(n):
            p_ref, t0_ref, t1_ref, t2_ref = refs[4 * a:4 * a + 4]
            refs[4 * n + 1 + a][...] = (p_ref[...].astype(F32) + t0_ref[...].astype(F32)
                                        + t1_ref[...].astype(F32) + t2_ref[...].astype(F32))

    def part(slot):
        return pl.BlockSpec((None, tr, C), lambda i, w_ref: (w_ref[slot], i, 0))

    gs = pltpu.PrefetchScalarGridSpec(
        num_scalar_prefetch=1, grid=(R // tr,), in_specs=[part(0), part(1), part(2), part(3)] * n + [_any()],
        out_specs=[pl.BlockSpec((tr, C), lambda i, w_ref: (i, 0))] * n)
    ops = [x for p, t in zip(parts, landed) for x in (p, t, t, t)]
    return pl.pallas_call(
        body, name=name, grid_spec=gs, out_shape=[jax.ShapeDtypeStruct((R, C), F32)] * n,
        compiler_params=_cparams(("arbitrary",)),
    )(where, *ops, after)


def _send_halves(name, halves):
    n = len(halves)

    def body(*refs):
        h_refs, o_refs = refs[:n], refs[n:2 * n]
        send, recv = refs[2 * n:]
        x, y, c, _, _ = _place()
        _handshake([(x, y, 1 - c)])
        cps = [_remote(h_refs[a], o_refs[a], send.at[a], recv.at[a], (x, y, 1 - c)) for a in range(n)]
        for cp in cps:
            cp.start()
        for cp in cps:
            cp.wait()

    return _sequencer(name, _ID_SIBLING, body, halves,
                      [jax.ShapeDtypeStruct(h.shape, h.dtype) for h in halves], [n, n])


class _Order:
    def __init__(self):
        self.tok = None
        x, y, c, q, chips = _place()
        self.core = c.astype(jnp.int32).reshape(1)
        self.where = jnp.stack([q] + [2 * rx + ry for rx, ry in chips]).astype(jnp.int32)

    def tie(self, x):
        return x if self.tok is None else lax.optimization_barrier((x, self.tok))[0]

    def done(self, outs):
        self.tok = outs[0]
        return outs


class _Reduction:
    def __init__(self, tag, grads):
        self.tag, self.grads = tag, grads
        self.swapped = _swap_halves(f"rs_swap_{tag}", grads)

    def _by_shape(self):
        groups = {}
        for a, g in enumerate(self.grads):
            groups.setdefault(g.shape, []).append(a)
        return list(groups.values())

    def scatter(self, seq):
        self.parts = [None] * len(self.grads)
        for idx in self._by_shape():
            outs = seq.done(_chip_sum(f"rs_chipsum_{self.tag}_{idx[0]}", [self.grads[a] for a in idx],
                                      [self.swapped[a] for a in idx], seq.tok, seq.core))
            for a, o in zip(idx, outs):
                self.parts[a] = o
        self.landed = _scatter_parts(f"rs_scatter_{self.tag}", self.parts)

    def finish(self, seq):
        halves = [None] * len(self.grads)
        for idx in self._by_shape():
            outs = seq.done(_sum_parts(f"rs_sum_{self.tag}_{idx[0]}", [self.parts[a] for a in idx],
                                       [self.landed[a] for a in idx], seq.where, seq.tok))
            for a, o in zip(idx, outs):
                halves[a] = o
        return list(zip(halves, _send_halves(f"rs_join_{self.tag}", halves)))


def _allreduce_small(name, v):
    _, R, _ = v.shape

    def body(v_ref, o_ref, land, acc, send1, recv1, send2, recv2):
        x, y, c = lax.axis_index("x"), lax.axis_index("y"), lax.axis_index("c")
        me = 4 * x + 2 * y + c
        peers = []
        for k in range(1, 8):
            dx, dy, dc = (k >> 2) & 1, (k >> 1) & 1, k & 1
            px, py, pc = (1 - x if dx else x), (1 - y if dy else y), (1 - c if dc else c)
            peers.append((k, (px, py, pc), 4 * px + 2 * py + pc))
        land[me] = v_ref[me]
        out1 = [_remote(v_ref.at[pid], land.at[me], send1.at[k], recv1.at[k], dev) for k, dev, pid in peers]
        for cp in out1:
            cp.start()
        for k, dev, pid in peers:
            _remote(land.at[pid], land.at[pid], send1.at[k], recv1.at[k], dev).wait_recv()
        total = land[0]
        for j in range(1, 8):
            total = total + land[j]
        acc[...] = total
        o_ref[me] = total
        out2 = [_remote(acc, o_ref.at[me], send2.at[k], recv2.at[k], dev) for k, dev, pid in peers]
        for cp in out2:
            cp.start()
        for k, dev, pid in peers:
            _remote(o_ref.at[pid], o_ref.at[pid], send2.at[k], recv2.at[k], dev).wait_recv()
        for cp in out1 + out2:
            cp.wait_send()

    return pl.pallas_call(
        body, name=name, in_specs=[_vm()], out_specs=_vm(),
        out_shape=jax.ShapeDtypeStruct(v.shape, F32),
        scratch_shapes=[pltpu.VMEM(v.shape, F32), pltpu.VMEM((R, _LANES), F32)]
        + [pltpu.SemaphoreType.DMA((8,)) for _ in range(4)],
        compiler_params=pltpu.CompilerParams(vmem_limit_bytes=_VMEM_LIMIT),
    )(v)


def _all_peers():
    x, y, c = lax.axis_index("x"), lax.axis_index("y"), lax.axis_index("c")
    peers = []
    for k in range(1, 8):
        px, py, pc = (1 - x if k & 4 else x), (1 - y if k & 2 else y), (1 - c if k & 1 else c)
        peers.append((k, (px, py, pc), 4 * px + 2 * py + pc))
    return 4 * x + 2 * y + c, peers


def _exchange_slices(name, v):
    def body(v_ref, land, send, recv):
        me, peers = _all_peers()
        _handshake([dev for _, dev, _ in peers])
        cps = [_remote(v_ref.at[pid], land.at[me], send.at[k], recv.at[k], dev) for k, dev, pid in peers]
        for cp in cps:
            cp.start()
        for k, dev, pid in peers:
            _remote(land.at[pid], land.at[pid], send.at[k], recv.at[k], dev).wait_recv()
        for cp in cps:
            cp.wait_send()

    return _sequencer(name, _ID_ALL, body, [v], [jax.ShapeDtypeStruct(v.shape, v.dtype)], [8, 8])[0]


def _sum_slices(name, v, landed, after):
    _, R, _ = v.shape

    def body(v_ref, land_ref, after_ref, o_ref):
        me, peers = _all_peers()
        acc = v_ref[me]
        for _, _, pid in peers:
            acc = acc + land_ref[pid]
        o_ref[...] = acc

    return pl.pallas_call(
        body, name=name, in_specs=[_vm(), _vm(), _any()], out_specs=_vm(),
        out_shape=jax.ShapeDtypeStruct((R, _LANES), F32),
        compiler_params=pltpu.CompilerParams(vmem_limit_bytes=_VMEM_LIMIT),
    )(v, landed, after)


def _broadcast_slices(name, s):
    def body(s_ref, out, send, recv):
        me, peers = _all_peers()
        _handshake([dev for _, dev, _ in peers])
        cps = [_remote(s_ref, out.at[me], send.at[k], recv.at[k], dev) for k, dev, pid in peers]
        for cp in cps:
            cp.start()
        for k, dev, pid in peers:
            _remote(out.at[pid], out.at[pid], send.at[k], recv.at[k], dev).wait_recv()
        for cp in cps:
            cp.wait_send()

    return _sequencer(name, _ID_ALL, body, [s], [jax.ShapeDtypeStruct((8,) + s.shape, s.dtype)], [8, 8])[0]


_WEIGHT_NAMES = ['norm_mix', 'norm_ffn', 'norm_ple', 'norm_final', 'gm_w_in', 'gm_ln_g', 'gm_ln_b', 'gm_w_s',
                 'gm_b_s', 'gm_w_out', 's5_w_in', 's5_a_re', 's5_a_im', 's5_log_dt', 's5_b_re', 's5_b_im',
                 's5_c_re', 's5_c_im', 's5_d', 's5_w_out', 'ffn_w1', 'ffn_w3', 'ffn_w2', 'ple_w_gate', 'ple_w_proj']
_BIG = {'gm_w_in': 'col', 'gm_w_out': 'row', 's5_w_in': 'row', 's5_w_out': 'col', 'ffn_w1': 'col',
        'ffn_w3': 'col', 'ffn_w2': 'row', 'ple_w_gate': 'row', 'ple_w_proj': 'col'}


_VIEW = {'s5_a_re': (0, 2, 1), 's5_a_im': (0, 2, 1), 's5_b_re': (0, 2, 3, 1), 's5_b_im': (0, 2, 3, 1),
         's5_c_re': (0, 2, 3, 1), 's5_c_im': (0, 2, 3, 1)}


def _to_view(name, a):
    return jnp.transpose(a, _VIEW[name]) if name in _VIEW else a


def _from_view(name, a):
    if name not in _VIEW:
        return a
    perm = _VIEW[name]
    return jnp.transpose(a, [perm.index(i) for i in range(len(perm))])


def _rc(kind, shard_shape):
    rows, cols = shard_shape[-2:]
    return (rows // 2, cols) if kind == "col" else (rows, cols // 2)


def _pack(vecs, rows_multiple):
    flat = jnp.concatenate([a.reshape(-1).astype(F32) for a in vecs])
    unit = rows_multiple * _LANES
    pad = (-flat.shape[0]) % unit
    return jnp.pad(flat, (0, pad)).reshape(-1, _LANES)


def _unpack(buf, shapes):
    flat = buf.reshape(-1)
    out, off = [], 0
    for s in shapes:
        n = math.prod(s)
        out.append(flat[off:off + n].reshape(s))
        off += n
    return out


def _ident(accs, ex):
    return accs


def _add_resid(accs, ex):
    return [accs[0] + ex[0]]


def _swiglu_epi(accs, ex):
    a, b = accs
    return [a, b, a * _sig(a) * b]


def _swiglu_bwd_epi(accs, ex):
    df = accs[0]
    a, b = ex[0].astype(F32), ex[1].astype(F32)
    sa = _sig(a)
    return [df * b * (sa * (1.0 + a * (1.0 - sa))), df * (a * sa)]


def _ple_epi(accs, ex):
    xin, pv = ex[0], ex[1]
    kh = pv.shape[1] // 2
    pp = jnp.concatenate([_dot(pv[:, :kh], ex[2 + 2 * part], _NN) + _dot(pv[:, kh:], ex[3 + 2 * part], _NN)
                          for part in (0, 1)], axis=1)
    gt = _sig(accs[0])
    return [xin + gt * pp, gt, pp]


def _glu_epi(accs, ex):
    val, sg = accs[0], _sig(accs[1])
    return [ex[0] + val * sg, val, sg]


def kernel(x, p, norm_mix, norm_ffn, norm_ple, norm_final, gm_w_in, gm_ln_g, gm_ln_b, gm_w_s, gm_b_s, gm_w_out, s5_w_in, s5_a_re, s5_a_im, s5_log_dt, s5_b_re, s5_b_im, s5_c_re, s5_c_im, s5_d, s5_w_out, ffn_w1, ffn_w3, ffn_w2, ple_w_gate, ple_w_proj, loss_target, m_norm_mix, m_norm_ffn, m_norm_ple, m_norm_final, m_gm_w_in, m_gm_ln_g, m_gm_ln_b, m_gm_w_s, m_gm_b_s, m_gm_w_out, m_s5_w_in, m_s5_a_re, m_s5_a_im, m_s5_log_dt, m_s5_b_re, m_s5_b_im, m_s5_c_re, m_s5_c_im, m_s5_d, m_s5_w_out, m_ffn_w1, m_ffn_w3, m_ffn_w2, m_ple_w_gate, m_ple_w_proj, v_norm_mix, v_norm_ffn, v_norm_ple, v_norm_final, v_gm_w_in, v_gm_ln_g, v_gm_ln_b, v_gm_w_s, v_gm_b_s, v_gm_w_out, v_s5_w_in, v_s5_a_re, v_s5_a_im, v_s5_log_dt, v_s5_b_re, v_s5_b_im, v_s5_c_re, v_s5_c_im, v_s5_d, v_s5_w_out, v_ffn_w1, v_ffn_w3, v_ffn_w2, v_ple_w_gate, v_ple_w_proj):
    env = dict(locals())
    w = {n: env[n] for n in _WEIGHT_NAMES}
    mom = {n: env["m_" + n] for n in _WEIGHT_NAMES}
    var = {n: env["v_" + n] for n in _WEIGHT_NAMES}
    xs, tgt = x[0], loss_target[0]
    L, D = xs.shape
    depth = norm_mix.shape[0]
    qx, qy = lax.axis_index("x"), lax.axis_index("y")
    q = 2 * qx + qy

    W, last_cast = {}, [None]

    def gather(tag, items, after=None):
        shards = []
        for name, layer in items:
            kind = _BIG[name]
            R, C = _rc(kind, w[name].shape)
            src = w[name] if after is None else lax.optimization_barrier((w[name], after))[0]
            shards.append(_cast_shard(f"cast_{name}{layer}", src, layer, kind, R, C))
        last_cast[0] = shards[-1]
        full = _allgather(f"ag_{tag}", shards)
        W.update({it: _W(f, _BIG[it[0]]) for it, f in zip(items, full)})

    gather("gm_w_in", [("gm_w_in", 0)])
    gather("gm_w_out", [("gm_w_out", 0)])
    gather("ffn_up0", [("ffn_w1", 0), ("ffn_w3", 0)])
    gather("ffn_down0", [("ffn_w2", 0)])

    def gather_rest(after):
        gather("ple0", [("ple_w_gate", 0), ("ple_w_proj", 0)], after)
        gather("s5_w_in", [("s5_w_in", 0)], after)
        gather("s5_w_out", [("s5_w_out", 0)], after)
        gather("ffn_up1", [("ffn_w1", 1), ("ffn_w3", 1)], after)
        gather("ffn_down1", [("ffn_w2", 1)], after)
        gather("ple1", [("ple_w_gate", 1), ("ple_w_proj", 1)], after)

    d_slots = jnp.zeros((4, D // 4), F32)
    d_slots = lax.dynamic_update_slice(d_slots, s5_d.astype(F32), (q, 0))
    d_sum = _allreduce_small("ar_s5_d", _pack([d_slots], 64).reshape(8, -1, _LANES))
    d_full = (d_sum.reshape(-1)[:D] * 0.5).reshape(1, D)

    def ffn_fwd(i, xin):
        hf = _rms_fwd(f"rms_ffn{i}", xin, norm_ffn[i:i + 1])
        a, b, f = _mm_nn(f"ffn_up{i}", hf, [W["ffn_w1", i], W["ffn_w3", i]], 1024, 1408, ffn_w2.shape[1] * 4,
                         [], [_MXU, _MXU, _MXU], _swiglu_epi, tm=1024)
        xo = _mm_nn(f"ffn_down{i}", f, [W["ffn_w2", i]], 1408, 1024, D, [xin], [F32], _add_resid, tm=1024)[0]
        return xo, (xin, hf, a, b, f)

    def ple_fwd(i, xin):
        hp = _rms_fwd(f"rms_ple{i}", xin, norm_ple[i:i + 1])
        pi, wp = p[i, 0], W["ple_w_proj", i]

        def more(tm):
            ops = [(pi, pl.BlockSpec((tm, pi.shape[1]), lambda i_, j, k: (i_, 0)))]
            for part in (0, 1):
                for half in (0, 1):
                    ops.append((wp.a, pl.BlockSpec((None, None, wp.R, wp.C),
                                                   lambda i_, j, k, part=part, half=half: (2 * j + part, half, 0, 0))))
            return ops

        xo, gt, pp = _mm_nn(f"ple_gate{i}", hp, [W["ple_w_gate", i]], 512, 2 * wp.C, D, [xin], [F32, _MXU, F32],
                            _ple_epi, tm=1024, more=more)
        return xo, (xin, hp, pi, pp, gt)

    h0 = _rms_fwd("rms_mix0", xs, norm_mix[0:1])
    z = _mm_nn("gm_in", h0, [W["gm_w_in", 0]], 1024, 1024, 2 * D, [], [F32], _ident, tm=1024)[0]
    bsT = gm_b_s[0].T
    gm_m = _gmlp_fwd(z, gm_ln_g, gm_ln_b, gm_w_s[0], bsT)
    x1 = _mm_nn("gm_out", gm_m, [W["gm_w_out", 0]], 512, 1024, D, [xs], [F32], _add_resid, tm=2048)[0]
    gather_rest(x1)
    x1 = lax.optimization_barrier((x1, last_cast[0]))[0]
    x2, ffn0 = ffn_fwd(0, x1)
    x3, ple0 = ple_fwd(0, x2)

    T = D // _LANES
    lanes = _S5_GT * _S5_P
    sv = {n: _to_view(n, w[n])[0] for n in _VIEW}
    a_re, a_im, log_dt = sv["s5_a_re"], sv["s5_a_im"], s5_log_dt
    lbr, lbi, Bbar_re, Bbar_im = _s5_prep(a_re, a_im, log_dt, sv["s5_b_re"], sv["s5_b_im"])

    def to_bd(B):
        return jnp.transpose(B.reshape(_S5_P, _S5_C, T, _S5_GT), (2, 3, 1, 0))

    def to_cd(cw):
        return jnp.transpose(cw.reshape(_S5_C, _S5_P, T, _S5_GT), (2, 3, 1, 0))

    def to_lam(v):
        return jnp.transpose(v).reshape(T, 1, lanes)

    bd_re, bd_im = to_bd(Bbar_re), to_bd(Bbar_im)
    cd_re, cd_im = to_cd(sv["s5_c_re"]), to_cd(sv["s5_c_im"])
    lam_re, lam_im = to_lam(lbr), to_lam(lbi)

    h1 = _rms_fwd("rms_mix1", x3, norm_mix[1:2])
    u = _mm_nn("s5_in", h1, [W["s5_w_in", 0]], 512, 1024, D, [], [F32], _ident, tm=2048)[0]
    s5_g, s5_re, s5_im = _s5_fwd(u, bd_re, bd_im, cd_re, cd_im, lam_re, lam_im, d_full)
    x4, glu_val, glu_sg = _mm_nn("s5_out", s5_g, [W["s5_w_out", 0], W["s5_w_out", 0]], 1024, 1024, D, [x3],
                                 [F32, _MXU, _MXU], _glu_epi, tm=1024, cb_offsets=[0, 2])
    x5, ffn1 = ffn_fwd(1, x4)
    x6, ple1 = ple_fwd(1, x5)

    dx, dpre1, dpp1, d_norm_final, loss_rows = _loss_head(x6, norm_final[None], tgt, ple1[3], ple1[4])

    small = {}

    seq = _Order()
    tie, done = seq.tie, seq.done
    d_norm_ple, d_norm_ffn, d_norm_mix = [None] * depth, [None] * depth, [None] * depth
    reduced = {}

    def keep(names, layer, pairs):
        for n, pr in zip(names, pairs):
            reduced[n, layer] = pr

    ple_names, up_names, down_names = ["ple_w_gate", "ple_w_proj"], ["ffn_w1", "ffn_w3"], ["ffn_w2"]

    def ple_bwd(i, dxo, saved, elem=None):
        xin, hp, pi, pp, gt = saved
        dpre, dpp = elem or done(_ple_bwd_elem(tie(dxo), pp, gt))
        dwg = done(_mm_tn(f"ple_gate_dw{i}", tie(hp), [dpre], "row", 512, 1024, 512, 1024))[0]
        dwp = done(_mm_tn(f"ple_proj_dw{i}", tie(pi), [dpp], "col", 128, 512, 128, 512))[0]
        red = _Reduction(f"ple{i}", [dwg, dwp])
        dhp = done(_mm_nt(f"ple_gate_dx{i}", [tie(dpre)], [W["ple_w_gate", i]], 512, 1024, [], [F32], _ident,
                          tm=2048))[0]
        dxin, dxin_mxu, dg = done(_rms_bwd(f"rms_ple_bwd{i}", tie(dhp), xin, norm_ple[i:i + 1], dxo))
        return dxin, dxin_mxu, dg, red

    def ffn_bwd(i, dxo, dxo_mxu, saved, before_up):
        xin, hf, a, b, f = saved
        dw2 = done(_mm_tn(f"ffn_down_dw{i}", tie(f), [dxo_mxu], "row", 1408, 1024, 1408, 1024))[0]
        r_down = _Reduction(f"ffd{i}", [dw2])
        da, db = done(_mm_nt(f"ffn_down_dx{i}", [tie(dxo_mxu)], [W["ffn_w2", i]], 1408, 1024, [a, b], [_MXU, _MXU],
                             _swiglu_bwd_epi, tm=1024))
        for step in before_up:
            step()
        r_down.scatter(seq)
        dw1, dw3 = done(_mm_tn(f"ffn_up_dw{i}", tie(hf), [da, db], "col", 1024, 1408, 1024, 1408))
        r_up = _Reduction(f"ffu{i}", [dw1, dw3])
        dhf = done(_mm_nt(f"ffn_up_dx{i}", [tie(da), db], [W["ffn_w1", i], W["ffn_w3", i]], 1024, 1408, [], [F32],
                          _ident, tm=1024))[0]
        dxin, dxin_mxu, dg = done(_rms_bwd(f"rms_ffn_bwd{i}", tie(dhf), xin, norm_ffn[i:i + 1], dxo))
        r_up.scatter(seq)
        return dxin, dxin_mxu, dg, r_down, r_up

    dx, dx_mxu, d_norm_ple[1], r_ple1 = ple_bwd(1, dx, ple1, (dpre1, dpp1))
    dx, _, d_norm_ffn[1], r_down1, r_up1 = ffn_bwd(1, dx, dx_mxu, ffn1, [lambda: r_ple1.scatter(seq)])

    do = done([_glu_bwd_elem(tie(dx), glu_val, glu_sg)])[0]
    dw_s5out = done(_mm_tn("s5_out_dw", tie(s5_g), [do], "col", 1024, 1024, 1024, 1024))[0]
    r_s5out = _Reduction("s5out", [dw_s5out])
    dgy = done(_mm_nt("s5_out_dx", [tie(do)], [W["s5_w_out", 0]], 1024, 1024, [], [F32], _ident, tm=2048))[0]
    keep(ple_names, 1, r_ple1.finish(seq))
    keep(down_names, 1, r_down1.finish(seq))
    du, dbd_re, dbd_im, dcd_re, dcd_im, dl_re, dl_im, dd = done(_s5_bwd(
        tie(u), dgy, s5_re, s5_im, bd_re, bd_im, cd_re, cd_im, lam_re, lam_im, d_full))
    r_s5out.scatter(seq)
    dw_s5in = done(_mm_tn("s5_in_dw", tie(h1), [du], "row", 512, 1024, 512, 1024))[0]
    r_s5in = _Reduction("s5in", [dw_s5in])
    dh1 = done(_mm_nt("s5_in_dx", [tie(du)], [W["s5_w_in", 0]], 512, 1024, [], [F32], _ident, tm=2048))[0]
    dx, _, d_norm_mix[1] = done(_rms_bwd("rms_mix1_bwd", tie(dh1), x3, norm_mix[1:2], dx))
    keep(up_names, 1, r_up1.finish(seq))
    r_s5in.scatter(seq)

    def from_bd(t):
        return jnp.transpose(t, (3, 2, 0, 1)).reshape(_S5_P, _S5_C, T * _S5_GT)

    def from_cdT(t):
        return jnp.transpose(t, (2, 3, 0, 1)).reshape(_S5_C, _S5_P, T * _S5_GT)

    def from_lam(t):
        return jnp.transpose(t.reshape(T * _S5_GT, _S5_P))

    da_re, da_im, dlog_dt, db_re, db_im = _s5_prep_bwd(
        a_re, a_im, log_dt, sv["s5_b_re"], sv["s5_b_im"], from_lam(dl_re), from_lam(dl_im),
        from_bd(dbd_re), from_bd(dbd_im))
    small["s5_a_re"], small["s5_a_im"], small["s5_log_dt"] = da_re[None], da_im[None], dlog_dt
    small["s5_b_re"], small["s5_b_im"] = db_re[None], db_im[None]
    small["s5_c_re"], small["s5_c_im"] = from_cdT(dcd_re)[None], from_cdT(dcd_im)[None]

    dx, dx_mxu, d_norm_ple[0], r_ple0 = ple_bwd(0, dx, ple0)
    keep(["s5_w_out"], 0, r_s5out.finish(seq))
    xin0, hf0, a0, b0, f0 = ffn0
    da0, db0 = done(_mm_nt("ffn_down_dx0", [tie(dx_mxu)], [W["ffn_w2", 0]], 1408, 1024, [a0, b0], [_MXU, _MXU],
                           _swiglu_bwd_epi, tm=1024))
    r_ple0.scatter(seq)
    dw1, dw3 = done(_mm_tn("ffn_up_dw0", tie(hf0), [da0, db0], "col", 1024, 1408, 1024, 1408))
    r_up0 = _Reduction("ffu0", [dw1, dw3])
    keep(["s5_w_in"], 0, r_s5in.finish(seq))
    dw2 = done(_mm_tn("ffn_down_dw0", tie(f0), [dx_mxu], "row", 1408, 1024, 1408, 1024))[0]
    r_down0 = _Reduction("ffd0", [dw2])
    r_up0.scatter(seq)
    dhf0 = done(_mm_nt("ffn_up_dx0", [tie(da0), db0], [W["ffn_w1", 0], W["ffn_w3", 0]], 1024, 1408, [], [F32], _ident,
                       tm=1024))[0]
    dx, dx_mxu, d_norm_ffn[0] = done(_rms_bwd("rms_ffn_bwd0", tie(dhf0), xin0, norm_ffn[0:1], dx))
    keep(ple_names, 0, r_ple0.finish(seq))
    r_down0.scatter(seq)

    dw_gmout = done(_mm_tn("gm_out_dw", tie(gm_m), [dx_mxu], "row", 512, 1024, 512, 1024))[0]
    r_gmout = _Reduction("gmout", [dw_gmout])
    dgm = done(_mm_nt("gm_out_dx", [tie(dx_mxu)], [W["gm_w_out", 0]], 512, 1024, [], [F32], _ident, tm=2048))[0]
    dz, dws, dbsT, dlng, dlnb = done(_gmlp_bwd(tie(z), dgm, gm_ln_g, gm_ln_b, gm_w_s[0], bsT))
    dw_gmin = done(_mm_tn("gm_in_dw", tie(h0), [dz], "col", 1024, 1024, 1024, 1024))[0]
    r_gmin = _Reduction("gmin", [dw_gmin])
    dh0 = done(_mm_nt("gm_in_dx", [tie(dz)], [W["gm_w_in", 0]], 1024, 1024, [], [F32], _ident, tm=2048))[0]
    dx, _, d_norm_mix[0] = done(_rms_bwd("rms_mix0_bwd", tie(dh0), xs, norm_mix[0:1], dx))
    grad_x = dx[None]

    small["norm_mix"], small["norm_ffn"] = jnp.concatenate(d_norm_mix), jnp.concatenate(d_norm_ffn)
    small["norm_ple"], small["norm_final"] = jnp.concatenate(d_norm_ple), d_norm_final[0]
    small["gm_ln_g"], small["gm_ln_b"], small["gm_w_s"] = dlng, dlnb, dws[None]
    small["gm_b_s"] = dbsT[:, :_GM_HEADS].T[None]
    small["s5_d"] = dd

    small_names = [n for n in _WEIGHT_NAMES if n not in _BIG]
    packed = _pack([small[n] for n in small_names] + [loss_rows[:, :1]], 64).reshape(8, -1, _LANES)
    grads, deltas, new_m, new_v = {}, {}, {}, {}
    my_c = lax.axis_index("c")

    def adamw(n, layer, prev):
        kind = _BIG[n]
        R, C = _rc(kind, w[n].shape)
        return done(_adamw_big(f"adamw_{n}{layer}", w[n], mom[n], var[n], layer, reduced[n, layer], kind, R, C,
                               seq.core, seq.tok, prev))

    def adamw_last(names):
        for n in names:
            grads[n], deltas[n], new_m[n], new_v[n] = adamw(n, 0, late.get(n))

    late = {}
    for n in down_names + ple_names:
        late[n] = adamw(n, 1, None)
    keep(up_names, 0, r_up0.finish(seq))
    r_gmout.scatter(seq)
    r_gmin.scatter(seq)
    landed = _exchange_slices("ar_small_in", packed)
    for n in up_names:
        late[n] = adamw(n, 1, None)
    adamw_last(["s5_w_in", "s5_w_out"] + ple_names)
    keep(down_names, 0, r_down0.finish(seq))
    mine = done([_sum_slices("ar_small_sum", packed, landed, seq.tok)])[0]
    spread = _broadcast_slices("ar_small_out", mine)
    adamw_last(up_names)
    keep(["gm_w_out"], 0, r_gmout.finish(seq))
    adamw_last(down_names)

    spread = lax.optimization_barrier((spread, seq.tok))[0]
    summed = lax.dynamic_update_slice(spread, mine[None], (4 * qx + 2 * qy + my_c, 0, 0))
    *red_list, loss_sum = _unpack(summed, [small[n].shape for n in small_names] + [(1, 1)])
    red_small, loss = dict(zip(small_names, red_list)), loss_sum.reshape(())
    red_small["s5_d"] = lax.dynamic_slice(red_small["s5_d"], (0, q * (D // 4)), (1, D // 4))

    def two_d(a):
        return a.reshape((1,) * (2 - a.ndim) + a.shape)

    def views(src):
        return [two_d(_to_view(n, src[n])) for n in small_names]

    g_views = [two_d(red_small[n]) for n in small_names]
    w_views = views(w)
    dl, mo, vo = _adamw_small([tie(w_views[0])] + w_views[1:], g_views, views(mom), views(var))
    done(dl)
    for n, g_, d_, m_, v_ in zip(small_names, g_views, dl, mo, vo):
        grads[n], deltas[n], new_m[n], new_v[n] = [_from_view(n, t_.reshape(_to_view(n, w[n]).shape)).reshape(w[n].shape)
                                                  for t_ in (g_, d_, m_, v_)]

    keep(["gm_w_in"], 0, r_gmin.finish(seq))
    adamw_last(["gm_w_out", "gm_w_in"])

    return (loss, grad_x, *[grads[n] for n in _WEIGHT_NAMES], *[deltas[n] for n in _WEIGHT_NAMES],
            *[new_m[n] for n in _WEIGHT_NAMES], *[new_v[n] for n in _WEIGHT_NAMES])
```

```python
import math

import jax
import jax.numpy as jnp
from jax import lax
from jax.experimental import pallas as pl
from jax.experimental.pallas import tpu as pltpu
from jax.experimental.pallas import tpu_sc as plsc

F32 = jnp.float32
_MXU = jnp.bfloat16
_WIRE = jnp.bfloat16
_EPS = 1e-6
_VMEM_LIMIT = 56 * 1024 * 1024
_LANES = 128
_MESH = pl.DeviceIdType.MESH

_LR, _B1, _B2, _AEPS, _WD, _STEP = 0.001, 0.9, 0.999, 1e-08, 0.01, 10

_GM_CHUNK = 128
_GM_HEADS = 16
_S5_GT = 8
_S5_P = 64
_S5_C = 16

_NN = (((1,), (0,)), ((), ()))
_NT = (((1,), (1,)), ((), ()))
_TN = (((0,), (0,)), ((), ()))


def _cparams(sem):
    return pltpu.CompilerParams(dimension_semantics=sem, vmem_limit_bytes=_VMEM_LIMIT)


def _sig(x):
    return 0.5 * jnp.tanh(0.5 * x) + 0.5


_GC = math.sqrt(2.0 / math.pi)


def _gelu(x):
    return 0.5 * x * (1.0 + jnp.tanh(_GC * (x + 0.044715 * (x * x * x))))


def _gelu_grad(x):
    t = jnp.tanh(_GC * (x + 0.044715 * (x * x * x)))
    return 0.5 * (1.0 + t) + 0.5 * x * (1.0 - t * t) * (_GC * (1.0 + 3.0 * 0.044715 * x * x))


def _dot(a, b, dn):
    return lax.dot_general(a.astype(_MXU), b.astype(_MXU), dn, preferred_element_type=F32)


class _W:
    def __init__(self, arr, kind):
        self.a, self.kind = arr, kind
        self.R, self.C = arr.shape[2], arr.shape[3]

    def full_shape(self):
        return (2 * self.R, 4 * self.C) if self.kind == "col" else (4 * self.R, 2 * self.C)


def _part_index(kind, R, C, tr, tc, rb, cb):
    nr, nc = R // tr, C // tc
    if kind == "col":
        return cb // nc, rb // nr, rb % nr, cb % nc
    return rb // nr, cb // nc, rb % nr, cb % nc


def _wspec(w, tr, tc, rb_fn, cb_fn):
    assert w.R % tr == 0 and w.C % tc == 0, (w.R, w.C, tr, tc)

    def imap(i, j, k):
        return _part_index(w.kind, w.R, w.C, tr, tc, rb_fn(i, j, k), cb_fn(i, j, k))

    return pl.BlockSpec((None, None, tr, tc), imap)


def _gspec(kind, R, C, tr, tc):
    assert R % tr == 0 and C % tc == 0, (R, C, tr, tc)

    def imap(i, j, k):
        part, half, rbi, cbi = _part_index(kind, R, C, tr, tc, i, j)
        return half, part, rbi, cbi

    return pl.BlockSpec((None, None, tr, tc), imap)


def _mm(name, grid, a_ops, b_ops, pairs, acc_shape, n_acc, extras, outs, epilogue):
    nk = grid[2]
    na, nb, ne, no = len(a_ops), len(b_ops), len(extras), len(outs)

    def body(*refs):
        a_refs = refs[:na]
        b_refs = refs[na:na + nb]
        e_refs = refs[na + nb:na + nb + ne]
        o_refs = refs[na + nb + ne:na + nb + ne + no]
        acc_refs = refs[na + nb + ne + no:]
        k = pl.program_id(2)

        def products():
            sums = [None] * n_acc
            for ai, bi, ci, dn in pairs:
                d = _dot(a_refs[ai][...], b_refs[bi][...], dn)
                sums[ci] = d if sums[ci] is None else sums[ci] + d
            return sums

        def finish(accs):
            res = epilogue(accs, [e[...] for e in e_refs])
            for o, r in zip(o_refs, res):
                o[...] = r.astype(o.dtype)

        if nk == 1:
            finish(products())
            return

        @pl.when(k == 0)
        def _():
            for acc, d in zip(acc_refs, products()):
                acc[...] = d

        @pl.when(jnp.logical_and(k > 0, k < nk - 1))
        def _():
            for acc, d in zip(acc_refs, products()):
                acc[...] += d

        @pl.when(k == nk - 1)
        def _():
            finish([acc[...] + d for acc, d in zip(acc_refs, products())])

    ops = list(a_ops) + list(b_ops) + list(extras)
    return pl.pallas_call(
        body, name=name, grid=grid,
        in_specs=[s for _, s in ops],
        out_specs=[s for _, s in outs],
        out_shape=[s for s, _ in outs],
        scratch_shapes=[pltpu.VMEM(acc_shape, F32) for _ in range(n_acc if nk > 1 else 0)],
        compiler_params=_cparams(("parallel", "parallel", "arbitrary")),
    )(*[a for a, _ in ops])


def _bs(shape, fn):
    return pl.BlockSpec(shape, fn)


def _tile_m(L):
    return min(L, 512)


def _mm_nn(name, x, ws, tk, tn, n_out, extras, outs_sd, epilogue, tm=None, cb_offsets=None, more=None):
    M, K = x.shape
    tm = min(M, tm or _tile_m(M))
    grid = (M // tm, n_out // tn, K // tk)
    a_ops = [(x, _bs((tm, tk), lambda i, j, k: (i, k)))]
    cb_offsets = cb_offsets or [0] * len(ws)
    b_ops = [(w.a, _wspec(w, tk, tn, lambda i, j, k: k, (lambda off: lambda i, j, k: j + off)(off)))
             for w, off in zip(ws, cb_offsets)]
    pairs = [(0, bi, bi, _NN) for bi in range(len(ws))]
    mn = _bs((tm, tn), lambda i, j, k: (i, j))
    ex = [(e, mn) for e in extras] + (more(tm) if more else [])
    outs = [(jax.ShapeDtypeStruct((M, n_out), dt), mn) for dt in outs_sd]
    return _mm(name, grid, a_ops, b_ops, pairs, (tm, tn), len(ws), ex, outs, epilogue)


def _mm_nt(name, xs, ws, tn, tk, extras, outs_sd, epilogue, tm=None):
    M, Nw = xs[0].shape
    Kw = ws[0].full_shape()[0]
    tm = min(M, tm or _tile_m(M))
    grid = (M // tm, Kw // tn, Nw // tk)
    a_ops = [(x, _bs((tm, tk), lambda i, j, k: (i, k))) for x in xs]
    b_ops = [(w.a, _wspec(w, tn, tk, lambda i, j, k: j, lambda i, j, k: k)) for w in ws]
    pairs = [(i, i, 0, _NT) for i in range(len(ws))]
    mn = _bs((tm, tn), lambda i, j, k: (i, j))
    ex = [(e, mn) for e in extras]
    outs = [(jax.ShapeDtypeStruct((M, Kw), dt), mn) for dt in outs_sd]
    return _mm(name, grid, a_ops, b_ops, pairs, (tm, tn), 1, ex, outs, epilogue)


def _mm_tn(name, x, dys, kind, R, C, tm, tn, tk=None):
    L, Kw = x.shape
    Nw = dys[0].shape[1]
    tk = tk or min(L, 1024)
    grid = (Kw // tm, Nw // tn, L // tk)
    a_ops = [(x, _bs((tk, tm), lambda i, j, k: (k, i)))]
    b_ops = [(dy, _bs((tk, tn), lambda i, j, k: (k, j))) for dy in dys]
    pairs = [(0, bi, bi, _TN) for bi in range(len(dys))]
    gs = _gspec(kind, R, C, tm, tn)
    outs = [(jax.ShapeDtypeStruct((2, 4, R, C), _WIRE), gs) for _ in dys]
    return _mm(name, grid, a_ops, b_ops, pairs, (tm, tn), len(dys), [], outs, lambda accs, ex: accs)


def _row_tile(L, n_streams):
    return min(L, 512 if n_streams <= 5 else 256)


def _rowwise(name, body, ins, outs, L, acc_outs=()):
    tr = _row_tile(L, sum(kind == "row" for _, kind in ins) + len(outs))
    n_in, n_out = len(ins), len(outs)

    def kbody(*refs):
        i_refs, o_refs, a_refs = refs[:n_in], refs[n_in:n_in + n_out], refs[n_in + n_out:]
        res, sums = body(*[r[...] for r in i_refs])
        for o, r in zip(o_refs, res):
            o[...] = r.astype(o.dtype)
        if a_refs:
            @pl.when(pl.program_id(0) == 0)
            def _():
                for a in a_refs:
                    a[...] = jnp.zeros(a.shape, F32)
            for a, s in zip(a_refs, sums):
                a[...] += s

    in_specs = []
    for arr, kind in ins:
        if kind == "row":
            in_specs.append(pl.BlockSpec((tr, arr.shape[1]), lambda i: (i, 0)))
        else:
            in_specs.append(pl.BlockSpec(arr.shape, lambda i: (0, 0)))
    out_specs = [pl.BlockSpec((tr, c), lambda i: (i, 0)) for c, _ in outs]
    out_shape = [jax.ShapeDtypeStruct((L, c), dt) for c, dt in outs]
    out_specs += [pl.BlockSpec((1, c), lambda i: (0, 0)) for c in acc_outs]
    out_shape += [jax.ShapeDtypeStruct((1, c), F32) for c in acc_outs]
    return pl.pallas_call(
        kbody, name=name, grid=(L // tr,), in_specs=in_specs, out_specs=out_specs, out_shape=out_shape,
        compiler_params=_cparams(("arbitrary",)),
    )(*[a for a, _ in ins])


def _rms_fwd(name, x, g):
    def body(xv, gv):
        r = lax.rsqrt(jnp.mean(xv * xv, axis=-1, keepdims=True) + _EPS)
        return [xv * r * gv], []
    return _rowwise(name, body, [(x, "row"), (g, "vec")], [(x.shape[1], _MXU)], x.shape[0])[0]


def _rms_bwd(name, dh, x, g, dres):
    def body(dhv, xv, gv, dr):
        r = lax.rsqrt(jnp.mean(xv * xv, axis=-1, keepdims=True) + _EPS)
        xh = xv * r
        dxh = dhv * gv
        dx = dr + r * (dxh - xh * jnp.mean(dxh * xh, axis=-1, keepdims=True))
        return [dx, dx], [jnp.sum(dhv * xh, axis=0, keepdims=True)]
    D = x.shape[1]
    return _rowwise(name, body, [(dh, "row"), (x, "row"), (g, "vec"), (dres, "row")], [(D, F32), (D, _MXU)],
                    x.shape[0], [D])


def _loss_head(x, g, target, pp, gt):
    D = x.shape[1]

    def body(xv, gv, tv, ppv, gtv):
        r = lax.rsqrt(jnp.mean(xv * xv, axis=-1, keepdims=True) + _EPS)
        xh = xv * r
        e = xh * gv - tv
        dy = e * (1.0 / D)
        dxh = dy * gv
        dx = r * (dxh - xh * jnp.mean(dxh * xh, axis=-1, keepdims=True))
        row_loss = 0.5 * jnp.mean(e * e, axis=-1, keepdims=True)
        lsum = jnp.sum(row_loss, axis=0, keepdims=True) + jnp.zeros((1, _LANES), F32)
        gt32 = gtv.astype(F32)
        return [dx, dx * ppv * gt32 * (1.0 - gt32), dx * gt32], [jnp.sum(dy * xh, axis=0, keepdims=True), lsum]
    return _rowwise("loss_head", body, [(x, "row"), (g, "vec"), (target, "row"), (pp, "row"), (gt, "row")],
                    [(D, F32), (D, _MXU), (D, _MXU)], x.shape[0], [D, _LANES])


def _ple_bwd_elem(dx, pp, gt):
    def body(dxv, ppv, gtv):
        gt32 = gtv.astype(F32)
        return [dxv * ppv * gt32 * (1.0 - gt32), dxv * gt32], []
    D = dx.shape[1]
    return _rowwise("ple_bwd_elem", body, [(dx, "row"), (pp, "row"), (gt, "row")], [(D, _MXU), (D, _MXU)], dx.shape[0])


def _glu_bwd_elem(dx, val, sg):
    def body(dxv, vv, sv):
        v32, s32 = vv.astype(F32), sv.astype(F32)
        return [jnp.concatenate([dxv * s32, dxv * v32 * s32 * (1.0 - s32)], axis=1)], []
    D = dx.shape[1]
    return _rowwise("glu_bwd_elem", body, [(dx, "row"), (val, "row"), (sg, "row")], [(2 * D, _MXU)], dx.shape[0])[0]


def _gm_common(z, ln_g, ln_b, wc_bf, bsT):
    W = z.shape[1] // 2
    zu, zv = z[:, :W], z[:, W:]
    u, v = _gelu(zu), _gelu(zv)
    mu = jnp.mean(v, axis=-1, keepdims=True)
    vc = v - mu
    rstd = lax.rsqrt(jnp.mean(vc * vc, axis=-1, keepdims=True) + _EPS)
    vh = vc * rstd
    vn = vh * ln_g + ln_b
    vnb = vn.astype(_MXU)
    svs = []
    for h in range(_GM_HEADS):
        sl = slice(h * _LANES, (h + 1) * _LANES)
        svs.append(_dot(wc_bf[h], vnb[:, sl], _NN) + bsT[:, h:h + 1])
    return zu, zv, u, vh, rstd, vnb, svs


def _causal(w):
    t = lax.broadcasted_iota(jnp.int32, w.shape, w.ndim - 2)
    s = lax.broadcasted_iota(jnp.int32, w.shape, w.ndim - 1)
    return jnp.where(s <= t, w, jnp.zeros_like(w))


def _gmlp_fwd(z, ln_g, ln_b, w_s, bsT):
    L, W2 = z.shape
    W = W2 // 2

    def body(z_ref, g_ref, b_ref, ws_ref, bs_ref, m_ref):
        wc = _causal(ws_ref[...]).astype(_MXU)
        _, _, u, _, _, _, svs = _gm_common(z_ref[...], g_ref[...], b_ref[...], wc, bs_ref[...])
        for h in range(_GM_HEADS):
            sl = slice(h * _LANES, (h + 1) * _LANES)
            m_ref[:, sl] = (u[:, sl] * svs[h]).astype(m_ref.dtype)

    return pl.pallas_call(
        body, name="gmlp_fwd", grid=(L // _GM_CHUNK,),
        in_specs=[pl.BlockSpec((_GM_CHUNK, W2), lambda n: (n, 0)),
                  pl.BlockSpec((1, W), lambda n: (0, 0)), pl.BlockSpec((1, W), lambda n: (0, 0)),
                  pl.BlockSpec(w_s.shape, lambda n: (0, 0, 0)), pl.BlockSpec(bsT.shape, lambda n: (0, 0))],
        out_specs=pl.BlockSpec((_GM_CHUNK, W), lambda n: (n, 0)),
        out_shape=jax.ShapeDtypeStruct((L, W), _MXU),
        compiler_params=_cparams(("arbitrary",)),
    )(z, ln_g, ln_b, w_s, bsT)


def _gmlp_bwd(z, dm, ln_g, ln_b, w_s, bsT):
    L, W2 = z.shape
    W = W2 // 2
    T = _GM_CHUNK

    def body(z_ref, dm_ref, g_ref, b_ref, ws_ref, bs_ref, dz_ref, dws_ref, dbs_ref, dg_ref, db_ref):
        @pl.when(pl.program_id(0) == 0)
        def _():
            dws_ref[...] = jnp.zeros(dws_ref.shape, F32)
            dbs_ref[...] = jnp.zeros(dbs_ref.shape, F32)
            dg_ref[...] = jnp.zeros(dg_ref.shape, F32)
            db_ref[...] = jnp.zeros(db_ref.shape, F32)

        wc = _causal(ws_ref[...]).astype(_MXU)
        ln_g_v = g_ref[...]
        zu, zv, u, vh, rstd, vnb, svs = _gm_common(z_ref[...], ln_g_v, b_ref[...], wc, bs_ref[...])
        dmv = dm_ref[...]
        lane = lax.broadcasted_iota(jnp.int32, (T, _LANES), 1)
        dbs = jnp.zeros((T, _LANES), F32)
        dvn_parts = []
        for h in range(_GM_HEADS):
            sl = slice(h * _LANES, (h + 1) * _LANES)
            dsv = dmv[:, sl] * u[:, sl]
            dz_ref[:, sl] = (dmv[:, sl] * svs[h] * _gelu_grad(zu[:, sl])).astype(dz_ref.dtype)
            dbs = dbs + jnp.where(lane == h, jnp.sum(dsv, axis=1, keepdims=True), 0.0)
            dsvb = dsv.astype(_MXU)
            dws_ref[h] += _dot(dsvb, vnb[:, sl], _NT)
            dvn_parts.append(_dot(wc[h], dsvb, _TN))
        dbs_ref[...] += dbs
        dvn = jnp.concatenate(dvn_parts, axis=1)
        dg_ref[...] += jnp.sum(dvn * vh, axis=0, keepdims=True)
        db_ref[...] += jnp.sum(dvn, axis=0, keepdims=True)
        dxh = dvn * ln_g_v
        dv = rstd * (dxh - jnp.mean(dxh, axis=-1, keepdims=True) - vh * jnp.mean(dxh * vh, axis=-1, keepdims=True))
        dz_ref[:, W:] = (dv * _gelu_grad(zv)).astype(dz_ref.dtype)

        @pl.when(pl.program_id(0) == pl.num_programs(0) - 1)
        def _():
            dws_ref[...] = _causal(dws_ref[...])

    return pl.pallas_call(
        body, name="gmlp_bwd", grid=(L // T,),
        in_specs=[pl.BlockSpec((T, W2), lambda n: (n, 0)), pl.BlockSpec((T, W), lambda n: (n, 0)),
                  pl.BlockSpec((1, W), lambda n: (0, 0)), pl.BlockSpec((1, W), lambda n: (0, 0)),
                  pl.BlockSpec(w_s.shape, lambda n: (0, 0, 0)), pl.BlockSpec(bsT.shape, lambda n: (0, 0))],
        out_specs=[pl.BlockSpec((T, W2), lambda n: (n, 0)),
                   pl.BlockSpec(w_s.shape, lambda n: (0, 0, 0)), pl.BlockSpec((T, _LANES), lambda n: (0, 0)),
                   pl.BlockSpec((1, W), lambda n: (0, 0)), pl.BlockSpec((1, W), lambda n: (0, 0))],
        out_shape=[jax.ShapeDtypeStruct((L, W2), _MXU), jax.ShapeDtypeStruct(w_s.shape, F32),
                   jax.ShapeDtypeStruct((T, _LANES), F32),
                   jax.ShapeDtypeStruct((1, W), F32), jax.ShapeDtypeStruct((1, W), F32)],
        compiler_params=_cparams(("arbitrary",)),
    )(z, dm, ln_g, ln_b, w_s, bsT)


def _s5_prep_math(a_re, a_im, log_dt):
    dt = jnp.exp(log_dt)
    xr, xi = a_re * dt, a_im * dt
    e = jnp.exp(xr)
    lbr, lbi = e * jnp.cos(xi), e * jnp.sin(xi)
    dn = a_re * a_re + a_im * a_im
    nr, ni = lbr - 1.0, lbi
    pr, pi = nr * a_re + ni * a_im, ni * a_re - nr * a_im
    return dt, lbr, lbi, dn, nr, ni, pr, pi


def _vm():
    return pl.BlockSpec(memory_space=pltpu.VMEM)


def _s5_prep(a_re, a_im, log_dt, b_re, b_im):
    def body(ar_ref, ai_ref, ld_ref, br_ref, bi_ref, lbr_ref, lbi_ref, Br_ref, Bi_ref):
        _, lbr, lbi, dn, _, _, pr, pi = _s5_prep_math(ar_ref[...], ai_ref[...], ld_ref[...])
        cr, ci = (pr / dn)[:, None, :], (pi / dn)[:, None, :]
        lbr_ref[...] = lbr
        lbi_ref[...] = lbi
        br, bi = br_ref[...], bi_ref[...]
        Br_ref[...] = cr * br - ci * bi
        Bi_ref[...] = cr * bi + ci * br

    sd = jax.ShapeDtypeStruct
    return pl.pallas_call(
        body, name="s5_prep", in_specs=[_vm()] * 5, out_specs=[_vm()] * 4,
        out_shape=[sd(a_re.shape, F32), sd(a_re.shape, F32), sd(b_re.shape, F32), sd(b_re.shape, F32)],
    )(a_re, a_im, log_dt, b_re, b_im)


def _s5_prep_bwd(a_re, a_im, log_dt, b_re, b_im, dlbr_s, dlbi_s, dBr, dBi):
    def body(ar_ref, ai_ref, ld_ref, br_ref, bi_ref, dlr_ref, dli_ref, dBr_ref, dBi_ref,
             dar_ref, dai_ref, dld_ref, dbr_ref, dbi_ref):
        a_re_v, a_im_v = ar_ref[...], ai_ref[...]
        dt, lbr, lbi, dn, nr, ni, pr, pi = _s5_prep_math(a_re_v, a_im_v, ld_ref[...])
        cr, ci = (pr / dn)[:, None, :], (pi / dn)[:, None, :]
        br, bi, dBr_v, dBi_v = br_ref[...], bi_ref[...], dBr_ref[...], dBi_ref[...]
        dbr_ref[...] = cr * dBr_v + ci * dBi_v
        dbi_ref[...] = cr * dBi_v - ci * dBr_v
        dcr = jnp.sum(br * dBr_v + bi * dBi_v, axis=1)
        dci = jnp.sum(br * dBi_v - bi * dBr_v, axis=1)
        dpr, dpi = dcr / dn, dci / dn
        ddn = -(dcr * pr + dci * pi) / (dn * dn)
        dnr = dpr * a_re_v - dpi * a_im_v
        dni = dpr * a_im_v + dpi * a_re_v
        dlbr = dlr_ref[...] + dnr
        dlbi = dli_ref[...] + dni
        dxr = dlbr * lbr + dlbi * lbi
        dxi = dlbi * lbr - dlbr * lbi
        dar_ref[...] = dpr * nr + dpi * ni + 2.0 * ddn * a_re_v + dxr * dt
        dai_ref[...] = dpr * ni - dpi * nr + 2.0 * ddn * a_im_v + dxi * dt
        dld_ref[...] = jnp.sum(dxr * a_re_v + dxi * a_im_v, axis=0, keepdims=True) * dt

    sd = jax.ShapeDtypeStruct
    return pl.pallas_call(
        body, name="s5_prep_bwd", in_specs=[_vm()] * 9, out_specs=[_vm()] * 5,
        out_shape=[sd(a_re.shape, F32), sd(a_re.shape, F32), sd(log_dt.shape, F32),
                   sd(b_re.shape, F32), sd(b_re.shape, F32)],
    )(a_re, a_im, log_dt, b_re, b_im, dlbr_s, dlbi_s, dBr, dBi)


def _shift_rows(v, down):
    n = v.shape[0]
    rolled = pltpu.roll(v, 1 if down else n - 1, 0)
    row = lax.broadcasted_iota(jnp.int32, v.shape, 0)
    return jnp.where(row == (0 if down else n - 1), 0.0, rolled)


def _cmul(ar, ai, br, bi):
    return ar * br - ai * bi, ar * bi + ai * br


_SEG = 8
_UNROLL = 8


def _seg_rows(k):
    if isinstance(k, int):
        return pl.ds(k * _SEG, _SEG)
    return pl.ds(pl.multiple_of(k * _SEG, _SEG), _SEG)


def _unrolled(n, step, init):
    main = n // _UNROLL

    def trip(kk, s):
        for uu in range(_UNROLL):
            s = step(kk * _UNROLL + uu, s)
        return s

    s = lax.fori_loop(0, main, trip, init)
    for r in range(main * _UNROLL, n):
        s = step(r, s)
    return s


def _interleave(src_ref, dst_ref, nk):
    def step(k, carry):
        dst_ref[_seg_rows(k), :] = src_ref[pl.ds(k, _SEG, stride=nk), :]
        return carry
    _unrolled(nk, step, 0)


def _deinterleave(src_ref, dst_ref, nk):
    def step(k, carry):
        dst_ref[pl.ds(k, _SEG, stride=nk), :] = src_ref[_seg_rows(k), :]
        return carry
    _unrolled(nk, step, 0)


def _segment_inits(er, ei, ar, ai, nk, down):
    pr, pi = ar, ai
    for _ in range(int(math.log2(nk))):
        pr, pi = _cmul(pr, pi, pr, pi)
    fr, fi = er, ei
    for _ in range(_SEG - 1):
        sr, si = _shift_rows(fr, down), _shift_rows(fi, down)
        mr, mi = _cmul(pr, pi, sr, si)
        fr, fi = er + mr, ei + mi
    return _shift_rows(fr, down), _shift_rows(fi, down)


def _scan_states(x_re, x_im, ar, ai, nk):
    lanes = ar.shape[1]

    def step(k, s):
        rows = _seg_rows(k)
        mr, mi = _cmul(ar, ai, s[0], s[1])
        return mr + x_re[rows, :], mi + x_im[rows, :]

    zero = jnp.zeros((_SEG, lanes), F32)
    er, ei = _unrolled(nk, step, (zero, zero))
    ir, ii = _segment_inits(er, ei, ar, ai, nk, True)

    def step2(k, s):
        rows = _seg_rows(k)
        mr, mi = _cmul(ar, ai, s[0], s[1])
        nr, ni = mr + x_re[rows, :], mi + x_im[rows, :]
        x_re[rows, :] = nr
        x_im[rows, :] = ni
        return nr, ni

    _unrolled(nk, step2, (ir, ii))


def _s5_tile_fwd(u, bd_re, bd_im, cd_re, cd_im, ar, ai, d, s_re, s_im, nk):
    s_re[...] = _dot(u, bd_re, _NN)
    s_im[...] = _dot(u, bd_im, _NN)
    _scan_states(s_re, s_im, ar, ai, nk)
    return _dot(s_re[...], cd_re, _NN) - _dot(s_im[...], cd_im, _NN) + d * u


def _s5_specs(L, T):
    lanes = _S5_GT * _S5_P
    u_spec = pl.BlockSpec((L, _LANES), lambda t: (0, t))
    bd_spec = pl.BlockSpec((None, _S5_GT, _S5_C, _S5_P), lambda t: (t, 0, 0, 0))
    cd_spec = pl.BlockSpec((None, _S5_GT, _S5_P, _S5_C), lambda t: (t, 0, 0, 0))
    lam_spec = pl.BlockSpec((None, 1, lanes), lambda t: (t, 0, 0))
    d_spec = pl.BlockSpec((1, _LANES), lambda t: (0, t))
    return lanes, u_spec, bd_spec, cd_spec, lam_spec, d_spec


def _fill_block_diag(dst_ref, blocks_ref):
    _, a, b = blocks_ref.shape
    dst_ref[...] = jnp.zeros(dst_ref.shape, F32)
    for g in range(_S5_GT):
        dst_ref[g * a:(g + 1) * a, g * b:(g + 1) * b] = blocks_ref[g]


def _take_block_diag(dst_ref, v):
    _, a, b = dst_ref.shape
    for g in range(_S5_GT):
        dst_ref[g] = v[g * a:(g + 1) * a, g * b:(g + 1) * b]


def _s5_dense(bdr, bdi, cdr, cdi, dense):
    for src, dst in zip((bdr, bdi, cdr, cdi), dense):
        _fill_block_diag(dst, src)
    return [dst[...] for dst in dense]


def _s5_dense_scratch(lanes):
    return [pltpu.VMEM((_LANES, lanes), F32), pltpu.VMEM((_LANES, lanes), F32),
            pltpu.VMEM((lanes, _LANES), F32), pltpu.VMEM((lanes, _LANES), F32)]


def _s5_fwd(u, bd_re, bd_im, cd_re, cd_im, lam_re, lam_im, d):
    L, Wd = u.shape
    T = Wd // _LANES
    nk = L // _SEG
    lanes, u_spec, bd_spec, cd_spec, lam_spec, d_spec = _s5_specs(L, T)
    s_spec = pl.BlockSpec((L, lanes), lambda t: (0, t))

    def body(u_ref, bdr, bdi, cdr, cdi, lr, li, d_ref, g_ref, s_re, s_im, up, tmp, *dense):
        ar = jnp.broadcast_to(lr[...], (_SEG, lanes))
        ai = jnp.broadcast_to(li[...], (_SEG, lanes))
        bd_re_v, bd_im_v, cd_re_v, cd_im_v = _s5_dense(bdr, bdi, cdr, cdi, dense)
        _interleave(u_ref, up, nk)
        y = _s5_tile_fwd(up[...], bd_re_v, bd_im_v, cd_re_v, cd_im_v, ar, ai, d_ref[...], s_re, s_im, nk)
        up[...] = _gelu(y)
        _deinterleave(up, tmp, nk)
        g_ref[...] = tmp[...].astype(g_ref.dtype)

    return pl.pallas_call(
        body, name="s5_fwd", grid=(T,),
        in_specs=[u_spec, bd_spec, bd_spec, cd_spec, cd_spec, lam_spec, lam_spec, d_spec],
        out_specs=[u_spec, s_spec, s_spec],
        out_shape=[jax.ShapeDtypeStruct((L, Wd), _MXU), jax.ShapeDtypeStruct((L, T * lanes), F32),
                   jax.ShapeDtypeStruct((L, T * lanes), F32)],
        scratch_shapes=[pltpu.VMEM((L, _LANES), F32) for _ in range(2)] + _s5_dense_scratch(lanes),
        compiler_params=_cparams(("arbitrary",)),
    )(u, bd_re, bd_im, cd_re, cd_im, lam_re, lam_im, d)


def _s5_bwd(u, dg, states_re, states_im, bd_re, bd_im, cd_re, cd_im, lam_re, lam_im, d):
    L, Wd = u.shape
    T = Wd // _LANES
    nk = L // _SEG
    lanes, u_spec, bd_spec, cd_spec, lam_spec, d_spec = _s5_specs(L, T)
    s_spec = pl.BlockSpec((L, lanes), lambda t: (0, t))

    def body(u_ref, dg_ref, s_re, s_im, bdr, bdi, cdr, cdi, lr, li, d_ref,
             du_ref, dbdr, dbdi, dcdr, dcdi, dlr, dli, dd_ref, g_re, g_im, up, dgp, tmp, *dense):
        ar = jnp.broadcast_to(lr[...], (_SEG, lanes))
        ai = jnp.broadcast_to(li[...], (_SEG, lanes))
        bd_re_v, bd_im_v, cd_re_v, cd_im_v = _s5_dense(bdr, bdi, cdr, cdi, dense)
        _interleave(u_ref, up, nk)
        _interleave(dg_ref, dgp, nk)
        uv, dv = up[...], d_ref[...]
        y = _dot(s_re[...], cd_re_v, _NN) - _dot(s_im[...], cd_im_v, _NN) + dv * uv
        dy = dgp[...] * _gelu_grad(y)
        dd_ref[...] = jnp.sum(dy * uv, axis=0, keepdims=True)
        dyb = dy.astype(_MXU)
        _take_block_diag(dcdr, _dot(dyb, s_re[...], _TN))
        _take_block_diag(dcdi, -_dot(dyb, s_im[...], _TN))
        g_re[...] = _dot(dyb, cd_re_v, _NT)
        g_im[...] = -_dot(dyb, cd_im_v, _NT)

        nai = -ai

        def step(j, s):
            rows = _seg_rows(nk - 1 - j)
            mr, mi = _cmul(ar, nai, s[0], s[1])
            return mr + g_re[rows, :], mi + g_im[rows, :]

        zero = jnp.zeros((_SEG, lanes), F32)
        er, ei = _unrolled(nk, step, (zero, zero))
        ir, ii = _segment_inits(er, ei, ar, nai, nk, False)

        def acc_lam(gr, gi, pr, pi, acc):
            return acc[0] + gr * pr + gi * pi, acc[1] + gi * pr - gr * pi

        def step2(j, carry):
            s, acc = carry
            k = nk - 1 - j
            rows = _seg_rows(k)
            mr, mi = _cmul(ar, nai, s[0], s[1])
            nr, ni = mr + g_re[rows, :], mi + g_im[rows, :]
            g_re[rows, :] = nr
            g_im[rows, :] = ni
            prev = _seg_rows(k - 1)
            return (nr, ni), acc_lam(nr, ni, s_re[prev, :], s_im[prev, :], acc)

        (g0r, g0i), acc = _unrolled(nk - 1, step2, ((ir, ii), (zero, zero)))
        first = _seg_rows(0)
        mr, mi = _cmul(ar, nai, g0r, g0i)
        nr, ni = mr + g_re[first, :], mi + g_im[first, :]
        g_re[first, :] = nr
        g_im[first, :] = ni
        last = _seg_rows(nk - 1)
        acc = acc_lam(nr, ni, _shift_rows(s_re[last, :], True), _shift_rows(s_im[last, :], True), acc)
        dlr[...] = jnp.sum(acc[0], axis=0, keepdims=True)
        dli[...] = jnp.sum(acc[1], axis=0, keepdims=True)

        gtr, gti = g_re[...].astype(_MXU), g_im[...].astype(_MXU)
        ub = uv.astype(_MXU)
        _take_block_diag(dbdr, _dot(ub, gtr, _TN))
        _take_block_diag(dbdi, _dot(ub, gti, _TN))
        dgp[...] = _dot(gtr, bd_re_v, _NT) + _dot(gti, bd_im_v, _NT) + dy * dv
        _deinterleave(dgp, tmp, nk)
        du_ref[...] = tmp[...].astype(du_ref.dtype)

    sd = jax.ShapeDtypeStruct
    big = sd((T, _S5_GT, _S5_C, _S5_P), F32)
    return pl.pallas_call(
        body, name="s5_bwd", grid=(T,),
        in_specs=[u_spec, u_spec, s_spec, s_spec, bd_spec, bd_spec, cd_spec, cd_spec, lam_spec, lam_spec, d_spec],
        out_specs=[u_spec, bd_spec, bd_spec, bd_spec, bd_spec, lam_spec, lam_spec, d_spec],
        out_shape=[sd((L, Wd), _MXU), big, big, big, big, sd((T, 1, lanes), F32), sd((T, 1, lanes), F32),
                   sd((1, Wd), F32)],
        scratch_shapes=[pltpu.VMEM((L, lanes), F32) for _ in range(2)]
        + [pltpu.VMEM((L, _LANES), F32) for _ in range(3)] + _s5_dense_scratch(lanes),
        compiler_params=_cparams(("arbitrary",)),
    )(u, dg, states_re, states_im, bd_re, bd_im, cd_re, cd_im, lam_re, lam_im, d)


def _half_tile(R, few_arrays=False):
    for t in ((512, 704, 128) if few_arrays else (256, 352, 128)):
        if R % t == 0:
            return t
    raise ValueError(R)


def _cast_shard(name, w, layer, kind, R, C):
    tr = _half_tile(R, True)
    nr = R // tr

    def body(w_ref, o_ref):
        o_ref[...] = w_ref[...].astype(o_ref.dtype)

    if kind == "col":
        in_map = lambda h, i: (layer, h * nr + i, 0)
    else:
        in_map = lambda h, i: (layer, i, h)
    return pl.pallas_call(
        body, name=name, grid=(2, nr), in_specs=[pl.BlockSpec((None, tr, C), in_map)],
        out_specs=pl.BlockSpec((None, tr, C), lambda h, i: (h, i, 0)),
        out_shape=jax.ShapeDtypeStruct((2, R, C), _WIRE),
        compiler_params=_cparams(("arbitrary", "arbitrary")),
    )(w)


def _adam_math(w, g, m, v):
    m2 = _B1 * m + (1.0 - _B1) * g
    v2 = _B2 * v + (1.0 - _B2) * (g * g)
    m_hat = m2 / (1.0 - _B1 ** _STEP)
    v_hat = v2 / (1.0 - _B2 ** _STEP)
    delta = -_LR * (m_hat / (jnp.sqrt(v_hat) + _AEPS) + _WD * w)
    return delta, m2, v2


def _adamw_big(name, w, m, v, layer, pair, kind, R, C, c, after, prev):
    tr = _half_tile(R, few_arrays=C <= 1024 and R % 512 == 0)
    nr = R // tr

    def body(c_ref, w_ref, m_ref, v_ref, own_ref, other_ref, *rest):
        go_ref, d_ref, mo_ref, vo_ref = rest[-4:]
        g = jnp.where(pl.program_id(0) == c_ref[0], own_ref[...], other_ref[...])
        delta, m2, v2 = _adam_math(w_ref[...], g, m_ref[...], v_ref[...])
        go_ref[...] = g
        d_ref[...] = delta
        mo_ref[...] = m2
        vo_ref[...] = v2

    if kind == "col":
        nat = pl.BlockSpec((None, tr, C), lambda h, i, c_ref: (layer, h * nr + i, 0))
    else:
        nat = pl.BlockSpec((None, tr, C), lambda h, i, c_ref: (layer, i, h))

    def gspec(own):
        return pl.BlockSpec((tr, C), lambda h, i, c_ref: (jnp.where((h == c_ref[0]) == own, i, 0), 0))

    carried = list(prev) if prev is not None else []
    gs = pltpu.PrefetchScalarGridSpec(
        num_scalar_prefetch=1, grid=(2, nr),
        in_specs=[nat, nat, nat, gspec(True), gspec(False), _any()] + [_any()] * len(carried),
        out_specs=[nat] * 4)
    sd = jax.ShapeDtypeStruct(w.shape, F32)
    return pl.pallas_call(
        body, name=name, grid_spec=gs, out_shape=[sd] * 4,
        input_output_aliases={7 + k: k for k in range(len(carried))},
        compiler_params=_cparams(("arbitrary", "arbitrary")),
    )(c, w, m, v, pair[0], pair[1], after, *carried)


def _adamw_small(ws, gs, ms, vs):
    n = len(ws)

    def body(*refs):
        for i in range(n):
            w_ref, g_ref, m_ref, v_ref, d_ref, mo_ref, vo_ref = refs[i::n]
            delta, m2, v2 = _adam_math(w_ref[...], g_ref[...], m_ref[...], v_ref[...])
            d_ref[...] = delta
            mo_ref[...] = m2
            vo_ref[...] = v2

    sds = [jax.ShapeDtypeStruct(w.shape, F32) for w in ws]
    outs = pl.pallas_call(
        body, name="adamw_small", in_specs=[_vm()] * (4 * n), out_specs=[_vm()] * (3 * n), out_shape=sds * 3,
        compiler_params=pltpu.CompilerParams(vmem_limit_bytes=_VMEM_LIMIT),
    )(*ws, *gs, *ms, *vs)
    return outs[:n], outs[n:2 * n], outs[2 * n:]


def _place():
    x, y, c = lax.axis_index("x"), lax.axis_index("y"), lax.axis_index("c")
    chips = [(1 - x, y), (x, 1 - y), (1 - x, 1 - y)]
    return x, y, c, 2 * x + y, chips


def _any():
    return pl.BlockSpec(memory_space=pl.ANY)


def _remote(src, dst, ssem, rsem, dev):
    return pltpu.make_async_remote_copy(src_ref=src, dst_ref=dst, send_sem=ssem, recv_sem=rsem,
                                        device_id=dev, device_id_type=_MESH)


def _allgather(name, shards):
    n = len(shards)

    def body(*refs):
        s_refs, g_refs = refs[:n], refs[n:2 * n]
        send0, recv0, send1, recv1, send2, recv2 = refs[2 * n:]
        x, y, c, q, _ = _place()
        sib, xn, yn = (x, y, 1 - c), (1 - x, y, c), (x, 1 - y, c)
        qx, qy, qd = 2 * (1 - x) + y, 2 * x + (1 - y), 2 * (1 - x) + (1 - y)
        _handshake([sib, xn, yn])
        own = [_remote(s_refs[a], g_refs[a].at[q], send0.at[a], recv0.at[a], sib) for a in range(n)]

        def pieces(a):
            g, half = g_refs[a], s_refs[a].shape[1] // 2
            return [g.at[qx, c], g.at[qy, c], g.at[qd, c, pl.ds(0, half)], g.at[qd, c, pl.ds(half, half)]]

        def relayed(a):
            g, half = g_refs[a], s_refs[a].shape[1] // 2
            return [(g.at[qx, c, pl.ds(0, half)], yn), (g.at[qy, c, pl.ds(half, half)], xn)]

        first = []
        for a in range(n):
            first.append(_remote(s_refs[a].at[c], g_refs[a].at[q, c], send1.at[4 * a], recv1.at[4 * a], xn))
            first.append(_remote(s_refs[a].at[c], g_refs[a].at[q, c], send1.at[4 * a + 1], recv1.at[4 * a + 1], yn))
        for cp in first + own:
            cp.start()
        later = []
        for a in range(n):
            land = pieces(a)
            for j in range(4):
                k = 4 * a + j
                _remote(land[j], land[j], send1.at[k], recv1.at[k], xn).wait_recv()
                if j < 2:
                    src, to = relayed(a)[j]
                    cp = _remote(src, src, send1.at[k + 2], recv1.at[k + 2], to)
                    cp.start()
                    later.append(cp)
                cp = _remote(land[j], land[j], send2.at[k], recv2.at[k], sib)
                cp.start()
                later.append(cp)
        for a in range(n):
            g, half = g_refs[a], s_refs[a].shape[1] // 2
            theirs = [g.at[qx, 1 - c], g.at[qy, 1 - c], g.at[qd, 1 - c, pl.ds(0, half)],
                      g.at[qd, 1 - c, pl.ds(half, half)]]
            for j in range(4):
                _remote(theirs[j], theirs[j], send2.at[4 * a + j], recv2.at[4 * a + j], sib).wait_recv()
        for cp in own:
            cp.wait()
        for cp in first + later:
            cp.wait_send()

    return _sequencer(name, _ID_GATHER, body, shards,
                      [jax.ShapeDtypeStruct((4,) + s.shape, s.dtype) for s in shards], [n, n] + [4 * n] * 4)


def _handshake(peers):
    barrier = pltpu.get_barrier_semaphore()
    for peer in peers:
        pl.semaphore_signal(barrier, inc=1, device_id=peer, device_id_type=_MESH)
    pl.semaphore_wait(barrier, len(peers))


_ID_SIBLING, _ID_CHIPS, _ID_GATHER, _ID_ALL = 1, 2, 3, 4


def _sequencer(name, collective_id, body, ins, out_types, sem_counts):
    mesh = plsc.ScalarSubcoreMesh(axis_name="seq", num_cores=1)
    moved = sum(math.prod(o.shape) * jnp.dtype(o.dtype).itemsize for o in out_types)
    return pl.kernel(
        body, name=name, out_type=out_types, mesh=mesh,
        scratch_types=[pltpu.SemaphoreType.DMA((k,)) for k in sem_counts],
        compiler_params=pltpu.CompilerParams(collective_id=collective_id),
        cost_estimate=pl.CostEstimate(flops=0, transcendentals=0, bytes_accessed=2 * moved,
                                      remote_bytes_transferred=moved),
    )(*ins)


def _swap_halves(name, grads):
    n = len(grads)

    def body(*refs):
        g_refs, t_refs = refs[:n], refs[n:2 * n]
        send, recv = refs[2 * n:]
        x, y, c, _, _ = _place()
        _handshake([(x, y, 1 - c)])
        cps = [_remote(g_refs[a].at[1 - c], t_refs[a], send.at[a], recv.at[a], (x, y, 1 - c)) for a in range(n)]
        for cp in cps:
            cp.start()
        for cp in cps:
            cp.wait()

    return _sequencer(name, _ID_SIBLING, body, grads,
                      [jax.ShapeDtypeStruct(g.shape[1:], g.dtype) for g in grads], [n, n])


def _chip_sum(name, grads, swapped, after, core):
    n = len(grads)
    _, _, R, C = grads[0].shape
    tr = _half_tile(R, True)

    def body(c_ref, *refs):
        for g_ref, t_ref, o_ref in zip(refs[:n], refs[n:2 * n], refs[2 * n + 1:]):
            o_ref[...] = (g_ref[...].astype(F32) + t_ref[...].astype(F32)).astype(o_ref.dtype)

    own = pl.BlockSpec((None, None, tr, C), lambda r, i, c_ref: (c_ref[0], r, i, 0))
    part = pl.BlockSpec((None, tr, C), lambda r, i, c_ref: (r, i, 0))
    gs = pltpu.PrefetchScalarGridSpec(
        num_scalar_prefetch=1, grid=(4, R // tr), in_specs=[own] * n + [part] * n + [_any()], out_specs=[part] * n)
    return pl.pallas_call(
        body, name=name, grid_spec=gs, out_shape=[jax.ShapeDtypeStruct((4, R, C), _WIRE)] * n,
        compiler_params=_cparams(("arbitrary", "arbitrary")),
    )(core, *grads, *swapped, after)


def _scatter_parts(name, parts):
    n = len(parts)

    def body(*refs):
        p_refs, t_refs = refs[:n], refs[n:2 * n]
        send, recv = refs[2 * n:]
        x, y, c, q, chips = _place()
        _handshake([(rx, ry, c) for rx, ry in chips])
        cps = []
        for a in range(n):
            for j, (rx, ry) in enumerate(chips):
                k = 3 * a + j
                cps.append(_remote(p_refs[a].at[2 * rx + ry], t_refs[a].at[q], send.at[k], recv.at[k], (rx, ry, c)))
        for cp in cps:
            cp.start()
        for a in range(n):
            for j, (rx, ry) in enumerate(chips):
                k = 3 * a + j
                land = t_refs[a].at[2 * rx + ry]
                _remote(land, land, send.at[k], recv.at[k], (rx, ry, c)).wait_recv()
        for cp in cps:
            cp.wait_send()

    return _sequencer(name, _ID_CHIPS, body, parts,
                      [jax.ShapeDtypeStruct(p.shape, p.dtype) for p in parts], [3 * n, 3 * n])


def _sum_parts(name, parts, landed, where, after):
    n = len(parts)
    _, R, C = parts[0].shape
    tr = _half_tile(R, True)

    def body(w_ref, *refs):
        for a in range(n):
            p_ref, t0_ref, t1_ref, t2_ref = refs[4 * a:4 * a + 4]
            refs[4 * n + 1 + a][...] = (p_ref[...].astype(F32) + t0_ref[...].astype(F32)
                                        + t1_ref[...].astype(F32) + t2_ref[...].astype(F32))

    def part(slot):
        return pl.BlockSpec((None, tr, C), lambda i, w_ref: (w_ref[slot], i, 0))

    gs = pltpu.PrefetchScalarGridSpec(
        num_scalar_prefetch=1, grid=(R // tr,), in_specs=[part(0), part(1), part(2), part(3)] * n + [_any()],
        out_specs=[pl.BlockSpec((tr, C), lambda i, w_ref: (i, 0))] * n)
    ops = [x for p, t in zip(parts, landed) for x in (p, t, t, t)]
    return pl.pallas_call(
        body, name=name, grid_spec=gs, out_shape=[jax.ShapeDtypeStruct((R, C), F32)] * n,
        compiler_params=_cparams(("arbitrary",)),
    )(where, *ops, after)


def _send_halves(name, halves):
    n = len(halves)

    def body(*refs):
        h_refs, o_refs = refs[:n], refs[n:2 * n]
        send, recv = refs[2 * n:]
        x, y, c, _, _ = _place()
        _handshake([(x, y, 1 - c)])
        cps = [_remote(h_refs[a], o_refs[a], send.at[a], recv.at[a], (x, y, 1 - c)) for a in range(n)]
        for cp in cps:
            cp.start()
        for cp in cps:
            cp.wait()

    return _sequencer(name, _ID_SIBLING, body, halves,
                      [jax.ShapeDtypeStruct(h.shape, h.dtype) for h in halves], [n, n])


class _Order:
    def __init__(self):
        self.tok = None
        x, y, c, q, chips = _place()
        self.core = c.astype(jnp.int32).reshape(1)
        self.where = jnp.stack([q] + [2 * rx + ry for rx, ry in chips]).astype(jnp.int32)

    def tie(self, x):
        return x if self.tok is None else lax.optimization_barrier((x, self.tok))[0]

    def done(self, outs):
        self.tok = outs[0]
        return outs


class _Reduction:
    def __init__(self, tag, grads):
        self.tag, self.grads = tag, grads
        self.swapped = _swap_halves(f"rs_swap_{tag}", grads)

    def _by_shape(self):
        groups = {}
        for a, g in enumerate(self.grads):
            groups.setdefault(g.shape, []).append(a)
        return list(groups.values())

    def scatter(self, seq):
        self.parts = [None] * len(self.grads)
        for idx in self._by_shape():
            outs = seq.done(_chip_sum(f"rs_chipsum_{self.tag}_{idx[0]}", [self.grads[a] for a in idx],
                                      [self.swapped[a] for a in idx], seq.tok, seq.core))
            for a, o in zip(idx, outs):
                self.parts[a] = o
        self.landed = _scatter_parts(f"rs_scatter_{self.tag}", self.parts)

    def finish(self, seq):
        halves = [None] * len(self.grads)
        for idx in self._by_shape():
            outs = seq.done(_sum_parts(f"rs_sum_{self.tag}_{idx[0]}", [self.parts[a] for a in idx],
                                       [self.landed[a] for a in idx], seq.where, seq.tok))
            for a, o in zip(idx, outs):
                halves[a] = o
        return list(zip(halves, _send_halves(f"rs_join_{self.tag}", halves)))


def _allreduce_small(name, v):
    _, R, _ = v.shape

    def body(v_ref, o_ref, land, acc, send1, recv1, send2, recv2):
        x, y, c = lax.axis_index("x"), lax.axis_index("y"), lax.axis_index("c")
        me = 4 * x + 2 * y + c
        peers = []
        for k in range(1, 8):
            dx, dy, dc = (k >> 2) & 1, (k >> 1) & 1, k & 1
            px, py, pc = (1 - x if dx else x), (1 - y if dy else y), (1 - c if dc else c)
            peers.append((k, (px, py, pc), 4 * px + 2 * py + pc))
        land[me] = v_ref[me]
        out1 = [_remote(v_ref.at[pid], land.at[me], send1.at[k], recv1.at[k], dev) for k, dev, pid in peers]
        for cp in out1:
            cp.start()
        for k, dev, pid in peers:
            _remote(land.at[pid], land.at[pid], send1.at[k], recv1.at[k], dev).wait_recv()
        total = land[0]
        for j in range(1, 8):
            total = total + land[j]
        acc[...] = total
        o_ref[me] = total
        out2 = [_remote(acc, o_ref.at[me], send2.at[k], recv2.at[k], dev) for k, dev, pid in peers]
        for cp in out2:
            cp.start()
        for k, dev, pid in peers:
            _remote(o_ref.at[pid], o_ref.at[pid], send2.at[k], recv2.at[k], dev).wait_recv()
        for cp in out1 + out2:
            cp.wait_send()

    return pl.pallas_call(
        body, name=name, in_specs=[_vm()], out_specs=_vm(),
        out_shape=jax.ShapeDtypeStruct(v.shape, F32),
        scratch_shapes=[pltpu.VMEM(v.shape, F32), pltpu.VMEM((R, _LANES), F32)]
        + [pltpu.SemaphoreType.DMA((8,)) for _ in range(4)],
        compiler_params=pltpu.CompilerParams(vmem_limit_bytes=_VMEM_LIMIT),
    )(v)


def _all_peers():
    x, y, c = lax.axis_index("x"), lax.axis_index("y"), lax.axis_index("c")
    peers = []
    for k in range(1, 8):
        px, py, pc = (1 - x if k & 4 else x), (1 - y if k & 2 else y), (1 - c if k & 1 else c)
        peers.append((k, (px, py, pc), 4 * px + 2 * py + pc))
    return 4 * x + 2 * y + c, peers


def _exchange_slices(name, v):
    def body(v_ref, land, send, recv):
        me, peers = _all_peers()
        _handshake([dev for _, dev, _ in peers])
        cps = [_remote(v_ref.at[pid], land.at[me], send.at[k], recv.at[k], dev) for k, dev, pid in peers]
        for cp in cps:
            cp.start()
        for k, dev, pid in peers:
            _remote(land.at[pid], land.at[pid], send.at[k], recv.at[k], dev).wait_recv()
        for cp in cps:
            cp.wait_send()

    return _sequencer(name, _ID_ALL, body, [v], [jax.ShapeDtypeStruct(v.shape, v.dtype)], [8, 8])[0]


def _sum_slices(name, v, landed, after):
    _, R, _ = v.shape

    def body(v_ref, land_ref, after_ref, o_ref):
        me, peers = _all_peers()
        acc = v_ref[me]
        for _, _, pid in peers:
            acc = acc + land_ref[pid]
        o_ref[...] = acc

    return pl.pallas_call(
        body, name=name, in_specs=[_vm(), _vm(), _any()], out_specs=_vm(),
        out_shape=jax.ShapeDtypeStruct((R, _LANES), F32),
        compiler_params=pltpu.CompilerParams(vmem_limit_bytes=_VMEM_LIMIT),
    )(v, landed, after)


def _broadcast_slices(name, s):
    def body(s_ref, out, send, recv):
        me, peers = _all_peers()
        _handshake([dev for _, dev, _ in peers])
        cps = [_remote(s_ref, out.at[me], send.at[k], recv.at[k], dev) for k, dev, pid in peers]
        for cp in cps:
            cp.start()
        for k, dev, pid in peers:
            _remote(out.at[pid], out.at[pid], send.at[k], recv.at[k], dev).wait_recv()
        for cp in cps:
            cp.wait_send()

    return _sequencer(name, _ID_ALL, body, [s], [jax.ShapeDtypeStruct((8,) + s.shape, s.dtype)], [8, 8])[0]


_WEIGHT_NAMES = ['norm_mix', 'norm_ffn', 'norm_ple', 'norm_final', 'gm_w_in', 'gm_ln_g', 'gm_ln_b', 'gm_w_s',
                 'gm_b_s', 'gm_w_out', 's5_w_in', 's5_a_re', 's5_a_im', 's5_log_dt', 's5_b_re', 's5_b_im',
                 's5_c_re', 's5_c_im', 's5_d', 's5_w_out', 'ffn_w1', 'ffn_w3', 'ffn_w2', 'ple_w_gate', 'ple_w_proj']
_BIG = {'gm_w_in': 'col', 'gm_w_out': 'row', 's5_w_in': 'row', 's5_w_out': 'col', 'ffn_w1': 'col',
        'ffn_w3': 'col', 'ffn_w2': 'row', 'ple_w_gate': 'row', 'ple_w_proj': 'col'}


_VIEW = {'s5_a_re': (0, 2, 1), 's5_a_im': (0, 2, 1), 's5_b_re': (0, 2, 3, 1), 's5_b_im': (0, 2, 3, 1),
         's5_c_re': (0, 2, 3, 1), 's5_c_im': (0, 2, 3, 1)}


def _to_view(name, a):
    return jnp.transpose(a, _VIEW[name]) if name in _VIEW else a


def _from_view(name, a):
    if name not in _VIEW:
        return a
    perm = _VIEW[name]
    return jnp.transpose(a, [perm.index(i) for i in range(len(perm))])


def _rc(kind, shard_shape):
    rows, cols = shard_shape[-2:]
    return (rows // 2, cols) if kind == "col" else (rows, cols // 2)


def _pack(vecs, rows_multiple):
    flat = jnp.concatenate([a.reshape(-1).astype(F32) for a in vecs])
    unit = rows_multiple * _LANES
    pad = (-flat.shape[0]) % unit
    return jnp.pad(flat, (0, pad)).reshape(-1, _LANES)


def _unpack(buf, shapes):
    flat = buf.reshape(-1)
    out, off = [], 0
    for s in shapes:
        n = math.prod(s)
        out.append(flat[off:off + n].reshape(s))
        off += n
    return out


def _ident(accs, ex):
    return accs


def _add_resid(accs, ex):
    return [accs[0] + ex[0]]


def _swiglu_epi(accs, ex):
    a, b = accs
    return [a, b, a * _sig(a) * b]


def _swiglu_bwd_epi(accs, ex):
    df = accs[0]
    a, b = ex[0].astype(F32), ex[1].astype(F32)
    sa = _sig(a)
    return [df * b * (sa * (1.0 + a * (1.0 - sa))), df * (a * sa)]


def _ple_epi(accs, ex):
    xin, pv = ex[0], ex[1]
    kh = pv.shape[1] // 2
    pp = jnp.concatenate([_dot(pv[:, :kh], ex[2 + 2 * part], _NN) + _dot(pv[:, kh:], ex[3 + 2 * part], _NN)
                          for part in (0, 1)], axis=1)
    gt = _sig(accs[0])
    return [xin + gt * pp, gt, pp]


def _glu_epi(accs, ex):
    val, sg = accs[0], _sig(accs[1])
    return [ex[0] + val * sg, val, sg]


def kernel(x, p, norm_mix, norm_ffn, norm_ple, norm_final, gm_w_in, gm_ln_g, gm_ln_b, gm_w_s, gm_b_s, gm_w_out, s5_w_in, s5_a_re, s5_a_im, s5_log_dt, s5_b_re, s5_b_im, s5_c_re, s5_c_im, s5_d, s5_w_out, ffn_w1, ffn_w3, ffn_w2, ple_w_gate, ple_w_proj, loss_target, m_norm_mix, m_norm_ffn, m_norm_ple, m_norm_final, m_gm_w_in, m_gm_ln_g, m_gm_ln_b, m_gm_w_s, m_gm_b_s, m_gm_w_out, m_s5_w_in, m_s5_a_re, m_s5_a_im, m_s5_log_dt, m_s5_b_re, m_s5_b_im, m_s5_c_re, m_s5_c_im, m_s5_d, m_s5_w_out, m_ffn_w1, m_ffn_w3, m_ffn_w2, m_ple_w_gate, m_ple_w_proj, v_norm_mix, v_norm_ffn, v_norm_ple, v_norm_final, v_gm_w_in, v_gm_ln_g, v_gm_ln_b, v_gm_w_s, v_gm_b_s, v_gm_w_out, v_s5_w_in, v_s5_a_re, v_s5_a_im, v_s5_log_dt, v_s5_b_re, v_s5_b_im, v_s5_c_re, v_s5_c_im, v_s5_d, v_s5_w_out, v_ffn_w1, v_ffn_w3, v_ffn_w2, v_ple_w_gate, v_ple_w_proj):
    env = dict(locals())
    w = {n: env[n] for n in _WEIGHT_NAMES}
    mom = {n: env["m_" + n] for n in _WEIGHT_NAMES}
    var = {n: env["v_" + n] for n in _WEIGHT_NAMES}
    xs, tgt = x[0], loss_target[0]
    L, D = xs.shape
    depth = norm_mix.shape[0]
    qx, qy = lax.axis_index("x"), lax.axis_index("y")
    q = 2 * qx + qy

    W, last_cast = {}, [None]

    def gather(tag, items, after=None):
        shards = []
        for name, layer in items:
            kind = _BIG[name]
            R, C = _rc(kind, w[name].shape)
            src = w[name] if after is None else lax.optimization_barrier((w[name], after))[0]
            shards.append(_cast_shard(f"cast_{name}{layer}", src, layer, kind, R, C))
        last_cast[0] = shards[-1]
        full = _allgather(f"ag_{tag}", shards)
        W.update({it: _W(f, _BIG[it[0]]) for it, f in zip(items, full)})

    gather("gm_w_in", [("gm_w_in", 0)])
    gather("gm_w_out", [("gm_w_out", 0)])
    gather("ffn_up0", [("ffn_w1", 0), ("ffn_w3", 0)])
    gather("ffn_down0", [("ffn_w2", 0)])

    def gather_rest(after):
        gather("ple0", [("ple_w_gate", 0), ("ple_w_proj", 0)], after)
        gather("s5_w_in", [("s5_w_in", 0)], after)
        gather("s5_w_out", [("s5_w_out", 0)], after)
        gather("ffn_up1", [("ffn_w1", 1), ("ffn_w3", 1)], after)
        gather("ffn_down1", [("ffn_w2", 1)], after)
        gather("ple1", [("ple_w_gate", 1), ("ple_w_proj", 1)], after)

    d_slots = jnp.zeros((4, D // 4), F32)
    d_slots = lax.dynamic_update_slice(d_slots, s5_d.astype(F32), (q, 0))
    d_sum = _allreduce_small("ar_s5_d", _pack([d_slots], 64).reshape(8, -1, _LANES))
    d_full = (d_sum.reshape(-1)[:D] * 0.5).reshape(1, D)

    def ffn_fwd(i, xin):
        hf = _rms_fwd(f"rms_ffn{i}", xin, norm_ffn[i:i + 1])
        a, b, f = _mm_nn(f"ffn_up{i}", hf, [W["ffn_w1", i], W["ffn_w3", i]], 1024, 1408, ffn_w2.shape[1] * 4,
                         [], [_MXU, _MXU, _MXU], _swiglu_epi, tm=1024)
        xo = _mm_nn(f"ffn_down{i}", f, [W["ffn_w2", i]], 1408, 1024, D, [xin], [F32], _add_resid, tm=1024)[0]
        return xo, (xin, hf, a, b, f)

    def ple_fwd(i, xin):
        hp = _rms_fwd(f"rms_ple{i}", xin, norm_ple[i:i + 1])
        pi, wp = p[i, 0], W["ple_w_proj", i]

        def more(tm):
            ops = [(pi, pl.BlockSpec((tm, pi.shape[1]), lambda i_, j, k: (i_, 0)))]
            for part in (0, 1):
                for half in (0, 1):
                    ops.append((wp.a, pl.BlockSpec((None, None, wp.R, wp.C),
                                                   lambda i_, j, k, part=part, half=half: (2 * j + part, half, 0, 0))))
            return ops

        xo, gt, pp = _mm_nn(f"ple_gate{i}", hp, [W["ple_w_gate", i]], 512, 2 * wp.C, D, [xin], [F32, _MXU, F32],
                            _ple_epi, tm=1024, more=more)
        return xo, (xin, hp, pi, pp, gt)

    h0 = _rms_fwd("rms_mix0", xs, norm_mix[0:1])
    z = _mm_nn("gm_in", h0, [W["gm_w_in", 0]], 1024, 1024, 2 * D, [], [F32], _ident, tm=1024)[0]
    bsT = gm_b_s[0].T
    gm_m = _gmlp_fwd(z, gm_ln_g, gm_ln_b, gm_w_s[0], bsT)
    x1 = _mm_nn("gm_out", gm_m, [W["gm_w_out", 0]], 512, 1024, D, [xs], [F32], _add_resid, tm=2048)[0]
    gather_rest(x1)
    x1 = lax.optimization_barrier((x1, last_cast[0]))[0]
    x2, ffn0 = ffn_fwd(0, x1)
    x3, ple0 = ple_fwd(0, x2)

    T = D // _LANES
    lanes = _S5_GT * _S5_P
    sv = {n: _to_view(n, w[n])[0] for n in _VIEW}
    a_re, a_im, log_dt = sv["s5_a_re"], sv["s5_a_im"], s5_log_dt
    lbr, lbi, Bbar_re, Bbar_im = _s5_prep(a_re, a_im, log_dt, sv["s5_b_re"], sv["s5_b_im"])

    def to_bd(B):
        return jnp.transpose(B.reshape(_S5_P, _S5_C, T, _S5_GT), (2, 3, 1, 0))

    def to_cd(cw):
        return jnp.transpose(cw.reshape(_S5_C, _S5_P, T, _S5_GT), (2, 3, 1, 0))

    def to_lam(v):
        return jnp.transpose(v).reshape(T, 1, lanes)

    bd_re, bd_im = to_bd(Bbar_re), to_bd(Bbar_im)
    cd_re, cd_im = to_cd(sv["s5_c_re"]), to_cd(sv["s5_c_im"])
    lam_re, lam_im = to_lam(lbr), to_lam(lbi)

    h1 = _rms_fwd("rms_mix1", x3, norm_mix[1:2])
    u = _mm_nn("s5_in", h1, [W["s5_w_in", 0]], 512, 1024, D, [], [F32], _ident, tm=2048)[0]
    s5_g, s5_re, s5_im = _s5_fwd(u, bd_re, bd_im, cd_re, cd_im, lam_re, lam_im, d_full)
    x4, glu_val, glu_sg = _mm_nn("s5_out", s5_g, [W["s5_w_out", 0], W["s5_w_out", 0]], 1024, 1024, D, [x3],
                                 [F32, _MXU, _MXU], _glu_epi, tm=1024, cb_offsets=[0, 2])
    x5, ffn1 = ffn_fwd(1, x4)
    x6, ple1 = ple_fwd(1, x5)

    dx, dpre1, dpp1, d_norm_final, loss_rows = _loss_head(x6, norm_final[None], tgt, ple1[3], ple1[4])

    small = {}

    seq = _Order()
    tie, done = seq.tie, seq.done
    d_norm_ple, d_norm_ffn, d_norm_mix = [None] * depth, [None] * depth, [None] * depth
    reduced = {}

    def keep(names, layer, pairs):
        for n, pr in zip(names, pairs):
            reduced[n, layer] = pr

    ple_names, up_names, down_names = ["ple_w_gate", "ple_w_proj"], ["ffn_w1", "ffn_w3"], ["ffn_w2"]

    def ple_bwd(i, dxo, saved, elem=None):
        xin, hp, pi, pp, gt = saved
        dpre, dpp = elem or done(_ple_bwd_elem(tie(dxo), pp, gt))
        dwg = done(_mm_tn(f"ple_gate_dw{i}", tie(hp), [dpre], "row", 512, 1024, 512, 1024))[0]
        dwp = done(_mm_tn(f"ple_proj_dw{i}", tie(pi), [dpp], "col", 128, 512, 128, 512))[0]
        red = _Reduction(f"ple{i}", [dwg, dwp])
        dhp = done(_mm_nt(f"ple_gate_dx{i}", [tie(dpre)], [W["ple_w_gate", i]], 512, 1024, [], [F32], _ident,
                          tm=2048))[0]
        dxin, dxin_mxu, dg = done(_rms_bwd(f"rms_ple_bwd{i}", tie(dhp), xin, norm_ple[i:i + 1], dxo))
        return dxin, dxin_mxu, dg, red

    def ffn_bwd(i, dxo, dxo_mxu, saved, before_up):
        xin, hf, a, b, f = saved
        dw2 = done(_mm_tn(f"ffn_down_dw{i}", tie(f), [dxo_mxu], "row", 1408, 1024, 1408, 1024))[0]
        r_down = _Reduction(f"ffd{i}", [dw2])
        da, db = done(_mm_nt(f"ffn_down_dx{i}", [tie(dxo_mxu)], [W["ffn_w2", i]], 1408, 1024, [a, b], [_MXU, _MXU],
                             _swiglu_bwd_epi, tm=1024))
        for step in before_up:
            step()
        r_down.scatter(seq)
        dw1, dw3 = done(_mm_tn(f"ffn_up_dw{i}", tie(hf), [da, db], "col", 1024, 1408, 1024, 1408))
        r_up = _Reduction(f"ffu{i}", [dw1, dw3])
        dhf = done(_mm_nt(f"ffn_up_dx{i}", [tie(da), db], [W["ffn_w1", i], W["ffn_w3", i]], 1024, 1408, [], [F32],
                          _ident, tm=1024))[0]
        dxin, dxin_mxu, dg = done(_rms_bwd(f"rms_ffn_bwd{i}", tie(dhf), xin, norm_ffn[i:i + 1], dxo))
        r_up.scatter(seq)
        return dxin, dxin_mxu, dg, r_down, r_up

    dx, dx_mxu, d_norm_ple[1], r_ple1 = ple_bwd(1, dx, ple1, (dpre1, dpp1))
    dx, _, d_norm_ffn[1], r_down1, r_up1 = ffn_bwd(1, dx, dx_mxu, ffn1, [lambda: r_ple1.scatter(seq)])

    do = done([_glu_bwd_elem(tie(dx), glu_val, glu_sg)])[0]
    dw_s5out = done(_mm_tn("s5_out_dw", tie(s5_g), [do], "col", 1024, 1024, 1024, 1024))[0]
    r_s5out = _Reduction("s5out", [dw_s5out])
    dgy = done(_mm_nt("s5_out_dx", [tie(do)], [W["s5_w_out", 0]], 1024, 1024, [], [F32], _ident, tm=1024))[0]
    keep(ple_names, 1, r_ple1.finish(seq))
    keep(down_names, 1, r_down1.finish(seq))
    du, dbd_re, dbd_im, dcd_re, dcd_im, dl_re, dl_im, dd = done(_s5_bwd(
        tie(u), dgy, s5_re, s5_im, bd_re, bd_im, cd_re, cd_im, lam_re, lam_im, d_full))
    r_s5out.scatter(seq)
    dw_s5in = done(_mm_tn("s5_in_dw", tie(h1), [du], "row", 512, 1024, 512, 1024))[0]
    r_s5in = _Reduction("s5in", [dw_s5in])
    dh1 = done(_mm_nt("s5_in_dx", [tie(du)], [W["s5_w_in", 0]], 512, 1024, [], [F32], _ident, tm=2048))[0]
    dx, _, d_norm_mix[1] = done(_rms_bwd("rms_mix1_bwd", tie(dh1), x3, norm_mix[1:2], dx))
    keep(up_names, 1, r_up1.finish(seq))
    r_s5in.scatter(seq)

    def from_bd(t):
        return jnp.transpose(t, (3, 2, 0, 1)).reshape(_S5_P, _S5_C, T * _S5_GT)

    def from_cdT(t):
        return jnp.transpose(t, (2, 3, 0, 1)).reshape(_S5_C, _S5_P, T * _S5_GT)

    def from_lam(t):
        return jnp.transpose(t.reshape(T * _S5_GT, _S5_P))

    da_re, da_im, dlog_dt, db_re, db_im = _s5_prep_bwd(
        a_re, a_im, log_dt, sv["s5_b_re"], sv["s5_b_im"], from_lam(dl_re), from_lam(dl_im),
        from_bd(dbd_re), from_bd(dbd_im))
    small["s5_a_re"], small["s5_a_im"], small["s5_log_dt"] = da_re[None], da_im[None], dlog_dt
    small["s5_b_re"], small["s5_b_im"] = db_re[None], db_im[None]
    small["s5_c_re"], small["s5_c_im"] = from_cdT(dcd_re)[None], from_cdT(dcd_im)[None]

    dx, dx_mxu, d_norm_ple[0], r_ple0 = ple_bwd(0, dx, ple0)
    keep(["s5_w_out"], 0, r_s5out.finish(seq))
    xin0, hf0, a0, b0, f0 = ffn0
    da0, db0 = done(_mm_nt("ffn_down_dx0", [tie(dx_mxu)], [W["ffn_w2", 0]], 1408, 1024, [a0, b0], [_MXU, _MXU],
                           _swiglu_bwd_epi, tm=1024))
    r_ple0.scatter(seq)
    dw1, dw3 = done(_mm_tn("ffn_up_dw0", tie(hf0), [da0, db0], "col", 1024, 1408, 1024, 1408))
    r_up0 = _Reduction("ffu0", [dw1, dw3])
    keep(["s5_w_in"], 0, r_s5in.finish(seq))
    dw2 = done(_mm_tn("ffn_down_dw0", tie(f0), [dx_mxu], "row", 1408, 1024, 1408, 1024))[0]
    r_down0 = _Reduction("ffd0", [dw2])
    r_up0.scatter(seq)
    dhf0 = done(_mm_nt("ffn_up_dx0", [tie(da0), db0], [W["ffn_w1", 0], W["ffn_w3", 0]], 1024, 1408, [], [F32], _ident,
                       tm=1024))[0]
    dx, dx_mxu, d_norm_ffn[0] = done(_rms_bwd("rms_ffn_bwd0", tie(dhf0), xin0, norm_ffn[0:1], dx))
    keep(ple_names, 0, r_ple0.finish(seq))
    r_down0.scatter(seq)

    dw_gmout = done(_mm_tn("gm_out_dw", tie(gm_m), [dx_mxu], "row", 512, 1024, 512, 1024))[0]
    r_gmout = _Reduction("gmout", [dw_gmout])
    dgm = done(_mm_nt("gm_out_dx", [tie(dx_mxu)], [W["gm_w_out", 0]], 512, 1024, [], [F32], _ident, tm=2048))[0]
    dz, dws, dbsT, dlng, dlnb = done(_gmlp_bwd(tie(z), dgm, gm_ln_g, gm_ln_b, gm_w_s[0], bsT))
    dw_gmin = done(_mm_tn("gm_in_dw", tie(h0), [dz], "col", 1024, 1024, 1024, 1024))[0]
    r_gmin = _Reduction("gmin", [dw_gmin])
    dh0 = done(_mm_nt("gm_in_dx", [tie(dz)], [W["gm_w_in", 0]], 1024, 1024, [], [F32], _ident, tm=1024))[0]
    dx, _, d_norm_mix[0] = done(_rms_bwd("rms_mix0_bwd", tie(dh0), xs, norm_mix[0:1], dx))
    grad_x = dx[None]

    small["norm_mix"], small["norm_ffn"] = jnp.concatenate(d_norm_mix), jnp.concatenate(d_norm_ffn)
    small["norm_ple"], small["norm_final"] = jnp.concatenate(d_norm_ple), d_norm_final[0]
    small["gm_ln_g"], small["gm_ln_b"], small["gm_w_s"] = dlng, dlnb, dws[None]
    small["gm_b_s"] = dbsT[:, :_GM_HEADS].T[None]
    small["s5_d"] = dd

    small_names = [n for n in _WEIGHT_NAMES if n not in _BIG]
    packed = _pack([small[n] for n in small_names] + [loss_rows[:, :1]], 64).reshape(8, -1, _LANES)
    grads, deltas, new_m, new_v = {}, {}, {}, {}
    my_c = lax.axis_index("c")

    def adamw(n, layer, prev):
        kind = _BIG[n]
        R, C = _rc(kind, w[n].shape)
        return done(_adamw_big(f"adamw_{n}{layer}", w[n], mom[n], var[n], layer, reduced[n, layer], kind, R, C,
                               seq.core, seq.tok, prev))

    def adamw_last(names):
        for n in names:
            grads[n], deltas[n], new_m[n], new_v[n] = adamw(n, 0, late.get(n))

    late = {}
    for n in down_names + ple_names:
        late[n] = adamw(n, 1, None)
    keep(up_names, 0, r_up0.finish(seq))
    r_gmout.scatter(seq)
    r_gmin.scatter(seq)
    landed = _exchange_slices("ar_small_in", packed)
    for n in up_names:
        late[n] = adamw(n, 1, None)
    adamw_last(["s5_w_in", "s5_w_out"] + ple_names)
    keep(down_names, 0, r_down0.finish(seq))
    mine = done([_sum_slices("ar_small_sum", packed, landed, seq.tok)])[0]
    spread = _broadcast_slices("ar_small_out", mine)
    adamw_last(up_names)
    keep(["gm_w_out"], 0, r_gmout.finish(seq))
    keep(["gm_w_in"], 0, r_gmin.finish(seq))
    adamw_last(down_names)

    spread = lax.optimization_barrier((spread, seq.tok))[0]
    summed = lax.dynamic_update_slice(spread, mine[None], (4 * qx + 2 * qy + my_c, 0, 0))
    *red_list, loss_sum = _unpack(summed, [small[n].shape for n in small_names] + [(1, 1)])
    red_small, loss = dict(zip(small_names, red_list)), loss_sum.reshape(())
    red_small["s5_d"] = lax.dynamic_slice(red_small["s5_d"], (0, q * (D // 4)), (1, D // 4))

    def two_d(a):
        return a.reshape((1,) * (2 - a.ndim) + a.shape)

    def views(src):
        return [two_d(_to_view(n, src[n])) for n in small_names]

    g_views = [two_d(red_small[n]) for n in small_names]
    w_views = views(w)
    dl, mo, vo = _adamw_small([tie(w_views[0])] + w_views[1:], g_views, views(mom), views(var))
    done(dl)
    for n, g_, d_, m_, v_ in zip(small_names, g_views, dl, mo, vo):
        grads[n], deltas[n], new_m[n], new_v[n] = [_from_view(n, t_.reshape(_to_view(n, w[n]).shape)).reshape(w[n].shape)
                                                  for t_ in (g_, d_, m_, v_)]

    adamw_last(["gm_w_out", "gm_w_in"])

    return (loss, grad_x, *[grads[n] for n in _WEIGHT_NAMES], *[deltas[n] for n in _WEIGHT_NAMES],
            *[new_m[n] for n in _WEIGHT_NAMES], *[new_v[n] for n in _WEIGHT_NAMES])
```
